```python
import jax, jax.numpy as jnp
from jax import lax
import numpy as np

D_MODEL = 1024
BATCH = 16
SEQ = 2048
DEPTH = 2

CHUNK = 128
E_A = 2 * D_MODEL
N_GROUPS_A = 16
GROUP_A = E_A // N_GROUPS_A
E_B = D_MODEL
CONV_W = 3
D_FF = ((8 * D_MODEL // 3 + 255) // 256) * 256
N_A = (DEPTH + 1) // 2
N_B = DEPTH // 2
EPS = 1e-6

kernel_name = "hybrid_sgu_shortconv_swiglu"


def rms_norm(x, g):
    xf = x.astype(jnp.float32)
    y = xf * lax.rsqrt(jnp.mean(xf * xf, axis=-1, keepdims=True) + EPS)
    return (y * g.astype(jnp.float32)).astype(x.dtype)


def layer_norm(x, g, b):
    xf = x.astype(jnp.float32)
    mu = jnp.mean(xf, axis=-1, keepdims=True)
    xc = xf - mu
    y = xc * lax.rsqrt(jnp.mean(xc * xc, axis=-1, keepdims=True) + EPS)
    return (y * g.astype(jnp.float32) + b.astype(jnp.float32)).astype(x.dtype)


def spatial_gating_mixer(h, w_in, v_gain, v_bias, w_s, b_s, w_out):
    bsz, seq, _ = h.shape
    z = jax.nn.gelu(jnp.einsum('bsd,de->bse', h, w_in))
    u, v = jnp.split(z, 2, axis=-1)
    v = layer_norm(v, v_gain, v_bias)
    v = v.reshape(bsz, seq // CHUNK, CHUNK, N_GROUPS_A, GROUP_A)
    causal = jnp.tril(jnp.ones((CHUNK, CHUNK), dtype=bool))
    w = jnp.where(causal[None], w_s, jnp.zeros((), w_s.dtype))
    sv = jnp.einsum('hts,bnshc->bnthc', w, v) + b_s.T[None, None, :, :, None]
    y = u * sv.reshape(bsz, seq, E_A)
    return jnp.einsum('bse,ed->bsd', y, w_out)


def short_conv_mixer(h, w_in, conv_w, w_out):
    seq = h.shape[1]
    p = jnp.einsum('bsd,de->bse', h, w_in)
    b_gate, c_gate, hx = jnp.split(p, 3, axis=-1)
    z = c_gate * hx
    zp = jnp.pad(z, ((0, 0), (CONV_W - 1, 0), (0, 0)))
    conv = zp[:, 0:seq] * conv_w[0]
    for k in range(1, CONV_W):
        conv = conv + zp[:, k:k + seq] * conv_w[k]
    y = b_gate * conv
    return jnp.einsum('bse,ed->bsd', y, w_out)


def swiglu(h, w_gate, w_up, w_down):
    g = jnp.einsum('bsd,df->bsf', h, w_gate)
    u = jnp.einsum('bsd,df->bsf', h, w_up)
    return jnp.einsum('bsf,fd->bsd', jax.nn.silu(g) * u, w_down)


def _fwd_setup_inputs(seed: int = 0) -> dict:
    key = jax.random.key(seed)
    ks = jax.random.split(key, 20)
    f32 = jnp.float32

    def nrm(k, shape, fan_in):
        return jax.random.normal(k, shape, f32) * (fan_in ** -0.5)

    def gain(k, shape):
        return 1.0 + 0.02 * jax.random.normal(k, shape, f32)

    return {
        "x": jax.random.normal(ks[0], (BATCH, SEQ, D_MODEL), f32),
        "mix_norm": gain(ks[1], (DEPTH, D_MODEL)),
        "ffn_norm": gain(ks[2], (DEPTH, D_MODEL)),
        "a_w_in": nrm(ks[3], (N_A, D_MODEL, 2 * E_A), D_MODEL),
        "a_v_gain": gain(ks[4], (N_A, E_A)),
        "a_v_bias": 0.02 * jax.random.normal(ks[5], (N_A, E_A), f32),
        "a_w_s": nrm(ks[6], (N_A, N_GROUPS_A, CHUNK, CHUNK), CHUNK),
        "a_b_s": gain(ks[7], (N_A, N_GROUPS_A, CHUNK)),
        "a_w_out": nrm(ks[8], (N_A, E_A, D_MODEL), E_A),
        "b_w_in": nrm(ks[9], (N_B, D_MODEL, 3 * E_B), D_MODEL),
        "b_conv_w": nrm(ks[10], (N_B, CONV_W, E_B), CONV_W),
        "b_w_out": nrm(ks[11], (N_B, E_B, D_MODEL), E_B),
        "ffn_w_gate": nrm(ks[12], (DEPTH, D_MODEL, D_FF), D_MODEL),
        "ffn_w_up": nrm(ks[13], (DEPTH, D_MODEL, D_FF), D_MODEL),
        "ffn_w_down": nrm(ks[14], (DEPTH, D_FF, D_MODEL), D_FF),
        "final_norm": gain(ks[15], (D_MODEL,)),
    }


def _fwd_reference(x, mix_norm, ffn_norm, a_w_in, a_v_gain, a_v_bias, a_w_s, a_b_s, a_w_out,
              b_w_in, b_conv_w, b_w_out, ffn_w_gate, ffn_w_up, ffn_w_down, final_norm):
    for i in range(DEPTH):
        h = rms_norm(x, mix_norm[i])
        j = i // 2
        if i % 2 == 0:
            mix = spatial_gating_mixer(h, a_w_in[j], a_v_gain[j], a_v_bias[j],
                                       a_w_s[j], a_b_s[j], a_w_out[j])
        else:
            mix = short_conv_mixer(h, b_w_in[j], b_conv_w[j], b_w_out[j])
        x = x + mix
        x = x + swiglu(rms_norm(x, ffn_norm[i]), ffn_w_gate[i], ffn_w_up[i], ffn_w_down[i])
    return rms_norm(x, final_norm)


import jax as _jax
import jax.numpy as _jnp

TWIN_FORMAT = 'train_step'
FWD_PARAMS = ['x', 'mix_norm', 'ffn_norm', 'a_w_in', 'a_v_gain', 'a_v_bias', 'a_w_s', 'a_b_s', 'a_w_out', 'b_w_in', 'b_conv_w', 'b_w_out', 'ffn_w_gate', 'ffn_w_up', 'ffn_w_down', 'final_norm']
TWIN_WEIGHTS = ['mix_norm', 'ffn_norm', 'a_w_in', 'a_v_gain', 'a_v_bias', 'a_w_s', 'a_b_s', 'a_w_out', 'b_w_in', 'b_conv_w', 'b_w_out', 'ffn_w_gate', 'ffn_w_up', 'ffn_w_down', 'final_norm']
TWIN_DIFF_INPUT = 'x'
TWIN_INPUTS = ['x', 'mix_norm', 'ffn_norm', 'a_w_in', 'a_v_gain', 'a_v_bias', 'a_w_s', 'a_b_s', 'a_w_out', 'b_w_in', 'b_conv_w', 'b_w_out', 'ffn_w_gate', 'ffn_w_up', 'ffn_w_down', 'final_norm', 'loss_target', 'm_mix_norm', 'm_ffn_norm', 'm_a_w_in', 'm_a_v_gain', 'm_a_v_bias', 'm_a_w_s', 'm_a_b_s', 'm_a_w_out', 'm_b_w_in', 'm_b_conv_w', 'm_b_w_out', 'm_ffn_w_gate', 'm_ffn_w_up', 'm_ffn_w_down', 'm_final_norm', 'v_mix_norm', 'v_ffn_norm', 'v_a_w_in', 'v_a_v_gain', 'v_a_v_bias', 'v_a_w_s', 'v_a_b_s', 'v_a_w_out', 'v_b_w_in', 'v_b_conv_w', 'v_b_w_out', 'v_ffn_w_gate', 'v_ffn_w_up', 'v_ffn_w_down', 'v_final_norm']
TWIN_OUTPUTS = ['loss', 'grad_x', 'grad_mix_norm', 'grad_ffn_norm', 'grad_a_w_in', 'grad_a_v_gain', 'grad_a_v_bias', 'grad_a_w_s', 'grad_a_b_s', 'grad_a_w_out', 'grad_b_w_in', 'grad_b_conv_w', 'grad_b_w_out', 'grad_ffn_w_gate', 'grad_ffn_w_up', 'grad_ffn_w_down', 'grad_final_norm', 'delta_mix_norm', 'delta_ffn_norm', 'delta_a_w_in', 'delta_a_v_gain', 'delta_a_v_bias', 'delta_a_w_s', 'delta_a_b_s', 'delta_a_w_out', 'delta_b_w_in', 'delta_b_conv_w', 'delta_b_w_out', 'delta_ffn_w_gate', 'delta_ffn_w_up', 'delta_ffn_w_down', 'delta_final_norm', 'new_m_mix_norm', 'new_m_ffn_norm', 'new_m_a_w_in', 'new_m_a_v_gain', 'new_m_a_v_bias', 'new_m_a_w_s', 'new_m_a_b_s', 'new_m_a_w_out', 'new_m_b_w_in', 'new_m_b_conv_w', 'new_m_b_w_out', 'new_m_ffn_w_gate', 'new_m_ffn_w_up', 'new_m_ffn_w_down', 'new_m_final_norm', 'new_v_mix_norm', 'new_v_ffn_norm', 'new_v_a_w_in', 'new_v_a_v_gain', 'new_v_a_v_bias', 'new_v_a_w_s', 'new_v_a_b_s', 'new_v_a_w_out', 'new_v_b_w_in', 'new_v_b_conv_w', 'new_v_b_w_out', 'new_v_ffn_w_gate', 'new_v_ffn_w_up', 'new_v_ffn_w_down', 'new_v_final_norm']
TWIN_LEAF_KINDS = {'loss': 'loss', 'grad_x': 'grad_x', 'grad_mix_norm': 'grad_w', 'grad_ffn_norm': 'grad_w', 'grad_a_w_in': 'grad_w', 'grad_a_v_gain': 'grad_w', 'grad_a_v_bias': 'grad_w', 'grad_a_w_s': 'grad_w', 'grad_a_b_s': 'grad_w', 'grad_a_w_out': 'grad_w', 'grad_b_w_in': 'grad_w', 'grad_b_conv_w': 'grad_w', 'grad_b_w_out': 'grad_w', 'grad_ffn_w_gate': 'grad_w', 'grad_ffn_w_up': 'grad_w', 'grad_ffn_w_down': 'grad_w', 'grad_final_norm': 'grad_w', 'delta_mix_norm': 'delta_w', 'delta_ffn_norm': 'delta_w', 'delta_a_w_in': 'delta_w', 'delta_a_v_gain': 'delta_w', 'delta_a_v_bias': 'delta_w', 'delta_a_w_s': 'delta_w', 'delta_a_b_s': 'delta_w', 'delta_a_w_out': 'delta_w', 'delta_b_w_in': 'delta_w', 'delta_b_conv_w': 'delta_w', 'delta_b_w_out': 'delta_w', 'delta_ffn_w_gate': 'delta_w', 'delta_ffn_w_up': 'delta_w', 'delta_ffn_w_down': 'delta_w', 'delta_final_norm': 'delta_w', 'new_m_mix_norm': 'new_m', 'new_m_ffn_norm': 'new_m', 'new_m_a_w_in': 'new_m', 'new_m_a_v_gain': 'new_m', 'new_m_a_v_bias': 'new_m', 'new_m_a_w_s': 'new_m', 'new_m_a_b_s': 'new_m', 'new_m_a_w_out': 'new_m', 'new_m_b_w_in': 'new_m', 'new_m_b_conv_w': 'new_m', 'new_m_b_w_out': 'new_m', 'new_m_ffn_w_gate': 'new_m', 'new_m_ffn_w_up': 'new_m', 'new_m_ffn_w_down': 'new_m', 'new_m_final_norm': 'new_m', 'new_v_mix_norm': 'new_v', 'new_v_ffn_norm': 'new_v', 'new_v_a_w_in': 'new_v', 'new_v_a_v_gain': 'new_v', 'new_v_a_v_bias': 'new_v', 'new_v_a_w_s': 'new_v', 'new_v_a_b_s': 'new_v', 'new_v_a_w_out': 'new_v', 'new_v_b_w_in': 'new_v', 'new_v_b_conv_w': 'new_v', 'new_v_b_w_out': 'new_v', 'new_v_ffn_w_gate': 'new_v', 'new_v_ffn_w_up': 'new_v', 'new_v_ffn_w_down': 'new_v', 'new_v_final_norm': 'new_v'}


def _forward(args):
    return _fwd_reference(*[args[k] for k in FWD_PARAMS])


def _output_shape():
    out = _jax.eval_shape(lambda: _forward(_fwd_setup_inputs(0)))
    return out.shape, out.dtype

N_MICROBATCH = 1
ADAM_LR = 0.001
ADAM_B1 = 0.9
ADAM_B2 = 0.999
ADAM_EPS = 1e-08
ADAM_WD = 0.01
ADAM_STEP = 10
PER_EXAMPLE_BATCH_AXIS = {'x': 0, 'loss_target': 0}
SHARED_INPUTS = []
_WEIGHT_DTYPES = {'mix_norm': _jnp.float32, 'ffn_norm': _jnp.float32, 'a_w_in': _jnp.float32, 'a_v_gain': _jnp.float32, 'a_v_bias': _jnp.float32, 'a_w_s': _jnp.float32, 'a_b_s': _jnp.float32, 'a_w_out': _jnp.float32, 'b_w_in': _jnp.float32, 'b_conv_w': _jnp.float32, 'b_w_out': _jnp.float32, 'ffn_w_gate': _jnp.float32, 'ffn_w_up': _jnp.float32, 'ffn_w_down': _jnp.float32, 'final_norm': _jnp.float32}
MOMENT_SCALE = {'mix_norm': 1.921610e-01, 'ffn_norm': 1.203370e-01, 'a_w_in': 1.005464e-01, 'a_v_gain': 6.958890e-02, 'a_v_bias': 6.674753e-02, 'a_w_s': 6.606898e-02, 'a_b_s': 9.497756e-02, 'a_w_out': 1.649462e-01, 'b_w_in': 1.076081e-01, 'b_conv_w': 1.089573e-01, 'b_w_out': 1.074455e-01, 'ffn_w_gate': 5.189669e-02, 'ffn_w_up': 5.026359e-02, 'ffn_w_down': 8.318388e-02, 'final_norm': 3.201941e+01}


def _to_microbatches(a, axis):
    t = _jnp.moveaxis(a, axis, 0)
    t = t.reshape((N_MICROBATCH, t.shape[0] // N_MICROBATCH) + t.shape[1:])
    return _jnp.moveaxis(t, 1, axis + 1)


def setup_inputs(seed: int = 0) -> dict:
    inp = _fwd_setup_inputs(seed)
    key = _jax.random.fold_in(_jax.random.key(seed), 7919)
    shape, _ = _output_shape()
    out = dict(inp)
    out["loss_target"] = _jax.random.normal(_jax.random.fold_in(key, 0), shape, _jnp.float32)
    for i, name in enumerate(TWIN_WEIGHTS):
        w = inp[name].astype(_jnp.float32)
        if MOMENT_SCALE is None:
            s = _jnp.sqrt(_jnp.mean(_jnp.square(w)) + 1e-30)
        else:
            s = MOMENT_SCALE[name]
        km, kv = _jax.random.split(_jax.random.fold_in(key, i + 1))
        out[name] = w
        out["m_" + name] = s * _jax.random.normal(km, w.shape, _jnp.float32)
        out["v_" + name] = (s * s) * _jax.random.uniform(kv, w.shape, _jnp.float32, 0.5, 1.5)
    if N_MICROBATCH > 1:
        for name, axis in PER_EXAMPLE_BATCH_AXIS.items():
            out[name] = _to_microbatches(out[name], axis)
    return {'x': out['x'], 'mix_norm': out['mix_norm'], 'ffn_norm': out['ffn_norm'], 'a_w_in': out['a_w_in'], 'a_v_gain': out['a_v_gain'], 'a_v_bias': out['a_v_bias'], 'a_w_s': out['a_w_s'], 'a_b_s': out['a_b_s'], 'a_w_out': out['a_w_out'], 'b_w_in': out['b_w_in'], 'b_conv_w': out['b_conv_w'], 'b_w_out': out['b_w_out'], 'ffn_w_gate': out['ffn_w_gate'], 'ffn_w_up': out['ffn_w_up'], 'ffn_w_down': out['ffn_w_down'], 'final_norm': out['final_norm'], 'loss_target': out['loss_target'], 'm_mix_norm': out['m_mix_norm'], 'm_ffn_norm': out['m_ffn_norm'], 'm_a_w_in': out['m_a_w_in'], 'm_a_v_gain': out['m_a_v_gain'], 'm_a_v_bias': out['m_a_v_bias'], 'm_a_w_s': out['m_a_w_s'], 'm_a_b_s': out['m_a_b_s'], 'm_a_w_out': out['m_a_w_out'], 'm_b_w_in': out['m_b_w_in'], 'm_b_conv_w': out['m_b_conv_w'], 'm_b_w_out': out['m_b_w_out'], 'm_ffn_w_gate': out['m_ffn_w_gate'], 'm_ffn_w_up': out['m_ffn_w_up'], 'm_ffn_w_down': out['m_ffn_w_down'], 'm_final_norm': out['m_final_norm'], 'v_mix_norm': out['v_mix_norm'], 'v_ffn_norm': out['v_ffn_norm'], 'v_a_w_in': out['v_a_w_in'], 'v_a_v_gain': out['v_a_v_gain'], 'v_a_v_bias': out['v_a_v_bias'], 'v_a_w_s': out['v_a_w_s'], 'v_a_b_s': out['v_a_b_s'], 'v_a_w_out': out['v_a_w_out'], 'v_b_w_in': out['v_b_w_in'], 'v_b_conv_w': out['v_b_conv_w'], 'v_b_w_out': out['v_b_w_out'], 'v_ffn_w_gate': out['v_ffn_w_gate'], 'v_ffn_w_up': out['v_ffn_w_up'], 'v_ffn_w_down': out['v_ffn_w_down'], 'v_final_norm': out['v_final_norm']}


def _loss(weights, diff, rest, loss_target):
    with _jax.named_scope("forward"):
        args = {**rest, TWIN_DIFF_INPUT: diff, **{k: w.astype(_WEIGHT_DTYPES[k]) for k, w in weights.items()}}
        y = _forward(args)
    with _jax.named_scope("loss_head"):
        err = _jnp.square(y.astype(_jnp.float32) - loss_target)
        return 0.5 * _jnp.sum(_jnp.mean(err, axis=-1)) if err.ndim else 0.5 * err


def _adamw(w, g, m, v):
    m = ADAM_B1 * m + (1.0 - ADAM_B1) * g
    v = ADAM_B2 * v + (1.0 - ADAM_B2) * _jnp.square(g)
    m_hat = m / (1.0 - ADAM_B1 ** ADAM_STEP)
    v_hat = v / (1.0 - ADAM_B2 ** ADAM_STEP)
    delta = -ADAM_LR * (m_hat / (_jnp.sqrt(v_hat) + ADAM_EPS) + ADAM_WD * w)
    return delta, m, v


def reference(x, mix_norm, ffn_norm, a_w_in, a_v_gain, a_v_bias, a_w_s, a_b_s, a_w_out, b_w_in, b_conv_w, b_w_out, ffn_w_gate, ffn_w_up, ffn_w_down, final_norm, loss_target, m_mix_norm, m_ffn_norm, m_a_w_in, m_a_v_gain, m_a_v_bias, m_a_w_s, m_a_b_s, m_a_w_out, m_b_w_in, m_b_conv_w, m_b_w_out, m_ffn_w_gate, m_ffn_w_up, m_ffn_w_down, m_final_norm, v_mix_norm, v_ffn_norm, v_a_w_in, v_a_v_gain, v_a_v_bias, v_a_w_s, v_a_b_s, v_a_w_out, v_b_w_in, v_b_conv_w, v_b_w_out, v_ffn_w_gate, v_ffn_w_up, v_ffn_w_down, v_final_norm):
    given = dict(x=x, mix_norm=mix_norm, ffn_norm=ffn_norm, a_w_in=a_w_in, a_v_gain=a_v_gain, a_v_bias=a_v_bias, a_w_s=a_w_s, a_b_s=a_b_s, a_w_out=a_w_out, b_w_in=b_w_in, b_conv_w=b_conv_w, b_w_out=b_w_out, ffn_w_gate=ffn_w_gate, ffn_w_up=ffn_w_up, ffn_w_down=ffn_w_down, final_norm=final_norm, loss_target=loss_target, m_mix_norm=m_mix_norm, m_ffn_norm=m_ffn_norm, m_a_w_in=m_a_w_in, m_a_v_gain=m_a_v_gain, m_a_v_bias=m_a_v_bias, m_a_w_s=m_a_w_s, m_a_b_s=m_a_b_s, m_a_w_out=m_a_w_out, m_b_w_in=m_b_w_in, m_b_conv_w=m_b_conv_w, m_b_w_out=m_b_w_out, m_ffn_w_gate=m_ffn_w_gate, m_ffn_w_up=m_ffn_w_up, m_ffn_w_down=m_ffn_w_down, m_final_norm=m_final_norm, v_mix_norm=v_mix_norm, v_ffn_norm=v_ffn_norm, v_a_w_in=v_a_w_in, v_a_v_gain=v_a_v_gain, v_a_v_bias=v_a_v_bias, v_a_w_s=v_a_w_s, v_a_b_s=v_a_b_s, v_a_w_out=v_a_w_out, v_b_w_in=v_b_w_in, v_b_conv_w=v_b_conv_w, v_b_w_out=v_b_w_out, v_ffn_w_gate=v_ffn_w_gate, v_ffn_w_up=v_ffn_w_up, v_ffn_w_down=v_ffn_w_down, v_final_norm=v_final_norm)
    weights = {n: given[n] for n in TWIN_WEIGHTS}
    shared = {n: given[n] for n in SHARED_INPUTS}
    per_example = {n: given[n] for n in ['x']}
    grad_fn = _jax.value_and_grad(_loss, argnums=(0, 1))

    def one_microbatch(ex, loss_target):
        ex = dict(ex)
        diff = ex.pop(TWIN_DIFF_INPUT)
        return grad_fn(weights, diff, {**shared, **ex}, loss_target)

    if N_MICROBATCH == 1:
        loss, (grad_w, grad_x) = one_microbatch(per_example, given["loss_target"])
    else:
        def body(carry, xs):
            loss_sum, grad_sum = carry
            l_k, (gw_k, gx_k) = one_microbatch(xs[0], xs[1])
            with _jax.named_scope("update"):
                return (loss_sum + l_k, _jax.tree.map(_jnp.add, grad_sum, gw_k)), gx_k

        init = (_jnp.zeros((), _jnp.float32), _jax.tree.map(_jnp.zeros_like, weights))
        (loss, grad_w), grad_x = _jax.lax.scan(body, init, (per_example, given["loss_target"]))
    with _jax.named_scope("update"):
        delta_w, new_m, new_v = {}, {}, {}
        for n in TWIN_WEIGHTS:
            delta_w[n], new_m[n], new_v[n] = _adamw(weights[n], grad_w[n], given["m_" + n], given["v_" + n])
    return (loss, grad_x, *[grad_w[n] for n in TWIN_WEIGHTS], *[delta_w[n] for n in TWIN_WEIGHTS],
            *[new_m[n] for n in TWIN_WEIGHTS], *[new_v[n] for n in TWIN_WEIGHTS])
```

```python
import functools
import math

import jax
import jax.numpy as jnp
from jax import lax
from jax.experimental import pallas as pl
from jax.experimental.pallas import tpu as pltpu

F32 = jnp.float32
BF16 = jnp.bfloat16

N_DEV = 8
CHUNK = 128
HEADS = 16
GROUP = 128
CONV_W = 3
NORM_EPS = 1e-6
GELU_C = math.sqrt(2.0 / math.pi)
GELU_K = 0.044715

ADAM_LR = 0.001
ADAM_B1 = 0.9
ADAM_B2 = 0.999
ADAM_EPS = 1e-08
ADAM_WD = 0.01
ADAM_STEP = 10

LANES = 128
SUBLANES = 8
VMEM_LIMIT = 60 * 1024 * 1024

MESH = pl.DeviceIdType.MESH
ANY = pl.BlockSpec(memory_space=pl.ANY)


def _params(sequential=True):
    return pltpu.CompilerParams(
        dimension_semantics=("arbitrary",) if sequential else None,
        vmem_limit_bytes=VMEM_LIMIT)


def _nn(a, b):
    return jnp.dot(a, b, preferred_element_type=F32)


def _nt(a, b):
    return lax.dot_general(a, b, (((1,), (1,)), ((), ())), preferred_element_type=F32)


def _tn(a, b):
    return lax.dot_general(a, b, (((0,), (0,)), ((), ())), preferred_element_type=F32)


def _row_mean(a):
    return jnp.mean(a, axis=-1, keepdims=True)


def _col_sum(a):
    return jnp.sum(a, axis=0, keepdims=True)


def _rms_fwd(x, g):
    r = lax.rsqrt(_row_mean(x * x) + NORM_EPS)
    xhat = x * r
    return xhat * g, xhat, r


def _rms_bwd(dh, xhat, r, g):
    a = dh * g
    dx = r * (a - xhat * _row_mean(a * xhat))
    return dx, _col_sum(dh * xhat)


def _gelu(x):
    t = jnp.tanh(GELU_C * (x + GELU_K * x * x * x))
    return 0.5 * x * (1.0 + t)


def _gelu_and_grad(x):
    x2 = x * x
    t = jnp.tanh(GELU_C * (x + GELU_K * x2 * x))
    half = 0.5 * (1.0 + t)
    d = half + 0.5 * x * (1.0 - t * t) * (GELU_C * (1.0 + 3.0 * GELU_K * x2))
    return x * half, d


def _sigmoid(x):
    return 1.0 / (1.0 + jnp.exp(-x))


def _load_weights(pairs, sems):
    @pl.when(pl.program_id(0) == 0)
    def _():
        copies = [pltpu.make_async_copy(src, dst, sems.at[i]) for i, (src, dst) in enumerate(pairs)]
        for cp in copies:
            cp.start()
        for cp in copies:
            cp.wait()


def _row_spec(tm, width):
    return pl.BlockSpec((tm, width), lambda i: (i, 0))


def _const_spec(shape):
    nd = len(shape)
    return pl.BlockSpec(shape, lambda i: (0,) * nd)


def _tril_weights(ws_ref):
    r = lax.broadcasted_iota(jnp.int32, (CHUNK, CHUNK), 0)
    c = lax.broadcasted_iota(jnp.int32, (CHUNK, CHUNK), 1)
    return [jnp.where(r >= c, ws_ref[h], 0.0).astype(BF16) for h in range(HEADS)]


def _sgu_stats(zpre, gain, bias):
    e = zpre.shape[1] // 2
    z, dz = _gelu_and_grad(zpre)
    u, v = z[:, :e], z[:, e:]
    vc = v - _row_mean(v)
    rstd = lax.rsqrt(_row_mean(vc * vc) + NORM_EPS)
    vhat = vc * rstd
    return u, vhat, rstd, vhat * gain + bias, dz


def _spatial_fwd(wt, vn_bf, bfull_ref, sv_ref, tm):
    for ci in range(tm // CHUNK):
        rows = slice(ci * CHUNK, (ci + 1) * CHUNK)
        for h in range(HEADS):
            cols = slice(h * GROUP, (h + 1) * GROUP)
            sv_ref[rows, cols] = _nn(wt[h], vn_bf[rows, cols]) + bfull_ref[:, cols]


def _mixer_a_fwd(x, g, win_t, gain, bias, ws, bfull, wout, tm):
    t_tok, d = x.shape
    e2 = win_t.shape[0]
    e = e2 // 2

    def body(x_ref, g_ref, gain_ref, bias_ref, ws_ref, bfull_ref, win_hbm, wout_hbm,
             xo_ref, zpre_ref, win_v, wout_v, sv_v, sems):
        _load_weights([(win_hbm, win_v), (wout_hbm, wout_v)], sems)
        xv = x_ref[...]
        h = _rms_fwd(xv, g_ref[...])[0].astype(BF16)
        zpre = _nt(h, win_v[...])
        zpre_ref[...] = zpre
        u, _, _, vn, _ = _sgu_stats(zpre, gain_ref[...], bias_ref[...])
        _spatial_fwd(_tril_weights(ws_ref), vn.astype(BF16), bfull_ref, sv_v, tm)
        y = (u * sv_v[...]).astype(BF16)
        xo_ref[...] = xv + _nn(y, wout_v[...])

    return pl.pallas_call(
        body, name="mixer_a_fwd", grid=(t_tok // tm,),
        in_specs=[_row_spec(tm, d), _const_spec((1, d)), _const_spec((1, e)), _const_spec((1, e)),
                  _const_spec((HEADS, CHUNK, CHUNK)), _const_spec((CHUNK, e)), ANY, ANY],
        out_specs=[_row_spec(tm, d), _row_spec(tm, e2)],
        out_shape=[jax.ShapeDtypeStruct((t_tok, d), F32), jax.ShapeDtypeStruct((t_tok, e2), F32)],
        scratch_shapes=[pltpu.VMEM((e2, d), BF16), pltpu.VMEM((e, d), BF16), pltpu.VMEM((tm, e), F32),
                        pltpu.SemaphoreType.DMA((2,))],
        compiler_params=_params(),
    )(x, g, gain, bias, ws, bfull, win_t, wout)


def _mixer_a_bwd(dout, x, zpre, g, win_t, gain, bias, ws, bfull, wout, tm):
    t_tok, d = x.shape
    e2 = win_t.shape[0]
    e = e2 // 2
    n_steps = t_tok // tm

    def body(dout_ref, x_ref, zpre_ref, g_ref, gain_ref, bias_ref, ws_ref, bfull_ref, win_hbm, wout_hbm,
             dx_ref, dxb_ref, h_ref, y_ref, dz_ref, dg_ref, dgain_ref, dbias_ref, dws_ref, dbs_ref,
             win_v, wout_v, sv_v, dvn_v, sems):
        i = pl.program_id(0)
        _load_weights([(win_hbm, win_v), (wout_hbm, wout_v)], sems)

        @pl.when(i == 0)
        def _():
            dg_ref[...] = jnp.zeros_like(dg_ref)
            dgain_ref[...] = jnp.zeros_like(dgain_ref)
            dbias_ref[...] = jnp.zeros_like(dbias_ref)
            dws_ref[...] = jnp.zeros_like(dws_ref)
            dbs_ref[...] = jnp.zeros_like(dbs_ref)

        xv = x_ref[...]
        gv = g_ref[...]
        hv, xhat, r = _rms_fwd(xv, gv)
        h_ref[...] = hv.astype(BF16)
        gain_v = gain_ref[...]
        u, vhat, rstd, vn, gelu_d = _sgu_stats(zpre_ref[...], gain_v, bias_ref[...])
        vn_bf = vn.astype(BF16)
        wt = _tril_weights(ws_ref)
        _spatial_fwd(wt, vn_bf, bfull_ref, sv_v, tm)
        sv = sv_v[...]
        y_ref[...] = (u * sv).astype(BF16)

        dov = dout_ref[...]
        dy = _nt(dov.astype(BF16), wout_v[...])
        du = dy * sv
        dsv = dy * u
        dsv_bf = dsv.astype(BF16)
        for ci in range(tm // CHUNK):
            rows = slice(ci * CHUNK, (ci + 1) * CHUNK)
            dbs_ref[...] += dsv[rows, :]
            for h in range(HEADS):
                cols = slice(h * GROUP, (h + 1) * GROUP)
                dvn_v[rows, cols] = _tn(wt[h], dsv_bf[rows, cols])
                dws_ref[h] += _nt(dsv_bf[rows, cols], vn_bf[rows, cols])
        dvn = dvn_v[...]
        dgain_ref[...] += _col_sum(dvn * vhat)
        dbias_ref[...] += _col_sum(dvn)
        dvhat = dvn * gain_v
        dv = rstd * (dvhat - _row_mean(dvhat) - vhat * _row_mean(dvhat * vhat))
        dzpre = (jnp.concatenate([du, dv], axis=1) * gelu_d).astype(BF16)
        dz_ref[...] = dzpre
        dh = _nn(dzpre, win_v[...])
        dxr, dg_row = _rms_bwd(dh, xhat, r, gv)
        dg_ref[...] += dg_row
        dx = dov + dxr
        dx_ref[...] = dx
        dxb_ref[...] = dx.astype(BF16)

        @pl.when(i == n_steps - 1)
        def _():
            rr = lax.broadcasted_iota(jnp.int32, (CHUNK, CHUNK), 0)
            cc = lax.broadcasted_iota(jnp.int32, (CHUNK, CHUNK), 1)
            for h in range(HEADS):
                dws_ref[h] = jnp.where(rr >= cc, dws_ref[h], 0.0)

    return pl.pallas_call(
        body, name="mixer_a_bwd", grid=(n_steps,),
        in_specs=[_row_spec(tm, d), _row_spec(tm, d), _row_spec(tm, e2), _const_spec((1, d)),
                  _const_spec((1, e)), _const_spec((1, e)), _const_spec((HEADS, CHUNK, CHUNK)),
                  _const_spec((CHUNK, e)), ANY, ANY],
        out_specs=[_row_spec(tm, d), _row_spec(tm, d), _row_spec(tm, d), _row_spec(tm, e), _row_spec(tm, e2),
                   _const_spec((1, d)), _const_spec((1, e)), _const_spec((1, e)),
                   _const_spec((HEADS, CHUNK, CHUNK)), _const_spec((CHUNK, e))],
        out_shape=[jax.ShapeDtypeStruct((t_tok, d), F32), jax.ShapeDtypeStruct((t_tok, d), BF16),
                   jax.ShapeDtypeStruct((t_tok, d), BF16), jax.ShapeDtypeStruct((t_tok, e), BF16),
                   jax.ShapeDtypeStruct((t_tok, e2), BF16),
                   jax.ShapeDtypeStruct((1, d), F32), jax.ShapeDtypeStruct((1, e), F32),
                   jax.ShapeDtypeStruct((1, e), F32), jax.ShapeDtypeStruct((HEADS, CHUNK, CHUNK), F32),
                   jax.ShapeDtypeStruct((CHUNK, e), F32)],
        scratch_shapes=[pltpu.VMEM((e2, d), BF16), pltpu.VMEM((e, d), BF16), pltpu.VMEM((tm, e), F32),
                        pltpu.VMEM((tm, e), F32), pltpu.SemaphoreType.DMA((2,))],
        compiler_params=_params(),
    )(dout, x, zpre, g, gain, bias, ws, bfull, win_t, wout)


def _ffn_fwd(x, g, wg_t, wu_t, wd, tm):
    t_tok, d = x.shape
    f = wg_t.shape[0]

    def body(x_ref, g_ref, wg_hbm, wu_hbm, wd_hbm, xo_ref, gate_ref, up_ref, wg_v, wu_v, wd_v, sems):
        _load_weights([(wg_hbm, wg_v), (wu_hbm, wu_v), (wd_hbm, wd_v)], sems)
        xv = x_ref[...]
        h = _rms_fwd(xv, g_ref[...])[0].astype(BF16)
        gate = _nt(h, wg_v[...])
        up = _nt(h, wu_v[...])
        gate_ref[...] = gate
        up_ref[...] = up
        act = (gate * _sigmoid(gate) * up).astype(BF16)
        xo_ref[...] = xv + _nn(act, wd_v[...])

    return pl.pallas_call(
        body, name="ffn_fwd", grid=(t_tok // tm,),
        in_specs=[_row_spec(tm, d), _const_spec((1, d)), ANY, ANY, ANY],
        out_specs=[_row_spec(tm, d), _row_spec(tm, f), _row_spec(tm, f)],
        out_shape=[jax.ShapeDtypeStruct((t_tok, d), F32), jax.ShapeDtypeStruct((t_tok, f), F32),
                   jax.ShapeDtypeStruct((t_tok, f), F32)],
        scratch_shapes=[pltpu.VMEM((f, d), BF16), pltpu.VMEM((f, d), BF16), pltpu.VMEM((f, d), BF16),
                        pltpu.SemaphoreType.DMA((3,))],
        compiler_params=_params(),
    )(x, g, wg_t, wu_t, wd)


def _ffn_bwd(dout, x, gate, up, g, wg_t, wu_t, wd, tm):
    t_tok, d = x.shape
    f = wg_t.shape[0]

    def body(dout_ref, x_ref, gate_ref, up_ref, g_ref, wg_hbm, wu_hbm, wd_hbm,
             dx_ref, dxb_ref, h_ref, act_ref, dgu_ref, dg_ref, wg_v, wu_v, wd_v, sems):
        _load_weights([(wg_hbm, wg_v), (wu_hbm, wu_v), (wd_hbm, wd_v)], sems)

        @pl.when(pl.program_id(0) == 0)
        def _():
            dg_ref[...] = jnp.zeros_like(dg_ref)

        xv = x_ref[...]
        gv = g_ref[...]
        hv, xhat, r = _rms_fwd(xv, gv)
        h_ref[...] = hv.astype(BF16)
        gate_v = gate_ref[...]
        up_v = up_ref[...]
        sig = _sigmoid(gate_v)
        silu = gate_v * sig
        act_ref[...] = (silu * up_v).astype(BF16)
        dov = dout_ref[...]
        dact = _nt(dov.astype(BF16), wd_v[...])
        dup = (dact * silu).astype(BF16)
        dgate = (dact * up_v * (sig * (1.0 + gate_v * (1.0 - sig)))).astype(BF16)
        dgu_ref[:, :f] = dgate
        dgu_ref[:, f:] = dup
        dh = _nn(dgate, wg_v[...]) + _nn(dup, wu_v[...])
        dxr, dg_row = _rms_bwd(dh, xhat, r, gv)
        dg_ref[...] += dg_row
        dx = dov + dxr
        dx_ref[...] = dx
        dxb_ref[...] = dx.astype(BF16)

    return pl.pallas_call(
        body, name="ffn_bwd", grid=(t_tok // tm,),
        in_specs=[_row_spec(tm, d), _row_spec(tm, d), _row_spec(tm, f), _row_spec(tm, f), _const_spec((1, d)),
                  ANY, ANY, ANY],
        out_specs=[_row_spec(tm, d), _row_spec(tm, d), _row_spec(tm, d), _row_spec(tm, f), _row_spec(tm, 2 * f),
                   _const_spec((1, d))],
        out_shape=[jax.ShapeDtypeStruct((t_tok, d), F32), jax.ShapeDtypeStruct((t_tok, d), BF16),
                   jax.ShapeDtypeStruct((t_tok, d), BF16), jax.ShapeDtypeStruct((t_tok, f), BF16),
                   jax.ShapeDtypeStruct((t_tok, 2 * f), BF16), jax.ShapeDtypeStruct((1, d), F32)],
        scratch_shapes=[pltpu.VMEM((f, d), BF16), pltpu.VMEM((f, d), BF16), pltpu.VMEM((f, d), BF16),
                        pltpu.SemaphoreType.DMA((3,))],
        compiler_params=_params(),
    )(dout, x, gate, up, g, wg_t, wu_t, wd)


def _shift_down(z, k, prev_rows):
    tm = z.shape[0]
    row = lax.broadcasted_iota(jnp.int32, z.shape, 0)
    out = pltpu.roll(z, k, 0)
    for j in range(k):
        out = jnp.where(row == j, prev_rows[j], out)
    return out


def _shift_up(z, k, next_rows):
    tm = z.shape[0]
    row = lax.broadcasted_iota(jnp.int32, z.shape, 0)
    out = pltpu.roll(z, tm - k, 0)
    for j in range(k):
        out = jnp.where(row == tm - k + j, next_rows[j], out)
    return out


def _mixer_b_fwd(x, g, win_t, conv_w, wout, tm, seq):
    t_tok, d = x.shape
    e3 = win_t.shape[0]
    e = e3 // 3
    tiles_per_seq = seq // tm

    def body(x_ref, g_ref, cw_ref, win_hbm, wout_hbm, xo_ref, p_ref, win_v, wout_v, tail_v, sems):
        i = pl.program_id(0)
        _load_weights([(win_hbm, win_v), (wout_hbm, wout_v)], sems)

        @pl.when(i % tiles_per_seq == 0)
        def _():
            tail_v[...] = jnp.zeros_like(tail_v)

        xv = x_ref[...]
        h = _rms_fwd(xv, g_ref[...])[0].astype(BF16)
        p = _nt(h, win_v[...])
        p_ref[...] = p
        z = p[:, e:2 * e] * p[:, 2 * e:]
        prev = [tail_v[SUBLANES - 2:SUBLANES - 1, :], tail_v[SUBLANES - 1:SUBLANES, :]]
        conv = (cw_ref[2:3, :] * z + cw_ref[1:2, :] * _shift_down(z, 1, prev[1:])
                + cw_ref[0:1, :] * _shift_down(z, 2, prev))
        tail_v[...] = z[tm - SUBLANES:, :]
        y = (p[:, :e] * conv).astype(BF16)
        xo_ref[...] = xv + _nn(y, wout_v[...])

    return pl.pallas_call(
        body, name="mixer_b_fwd", grid=(t_tok // tm,),
        in_specs=[_row_spec(tm, d), _const_spec((1, d)), _const_spec((SUBLANES, e)), ANY, ANY],
        out_specs=[_row_spec(tm, d), _row_spec(tm, e3)],
        out_shape=[jax.ShapeDtypeStruct((t_tok, d), F32), jax.ShapeDtypeStruct((t_tok, e3), F32)],
        scratch_shapes=[pltpu.VMEM((e3, d), BF16), pltpu.VMEM((e, d), BF16), pltpu.VMEM((SUBLANES, e), F32),
                        pltpu.SemaphoreType.DMA((2,))],
        compiler_params=_params(),
    )(x, g, conv_w, win_t, wout)


def _mixer_b_bwd(dout, x, p, g, win_t, conv_w, wout, tm, seq):
    t_tok, d = x.shape
    e3 = win_t.shape[0]
    e = e3 // 3
    tiles_per_seq = seq // tm
    halo_per_tile = tm // SUBLANES
    n_halo = t_tok // SUBLANES

    def body(dout_ref, dnext_ref, x_ref, p_ref, pprev_ref, pnext_ref, g_ref, cw_ref, win_hbm, wout_hbm,
             dx_ref, dxb_ref, h_ref, y_ref, dp_ref, dg_ref, dcw_ref, win_v, wout_v, sems):
        i = pl.program_id(0)
        _load_weights([(win_hbm, win_v), (wout_hbm, wout_v)], sems)

        @pl.when(i == 0)
        def _():
            dg_ref[...] = jnp.zeros_like(dg_ref)
            dcw_ref[...] = jnp.zeros_like(dcw_ref)

        first = (i % tiles_per_seq == 0).astype(F32)
        last = (i % tiles_per_seq == tiles_per_seq - 1).astype(F32)
        xv = x_ref[...]
        gv = g_ref[...]
        hv, xhat, r = _rms_fwd(xv, gv)
        h_ref[...] = hv.astype(BF16)
        pv = p_ref[...]
        bg, cg, hx = pv[:, :e], pv[:, e:2 * e], pv[:, 2 * e:]
        z = cg * hx
        zprev = pprev_ref[:, e:2 * e] * pprev_ref[:, 2 * e:] * (1.0 - first)
        prev = [zprev[SUBLANES - 2:SUBLANES - 1, :], zprev[SUBLANES - 1:SUBLANES, :]]
        zs1 = _shift_down(z, 1, prev[1:])
        zs2 = _shift_down(z, 2, prev)
        w0, w1, w2 = cw_ref[0:1, :], cw_ref[1:2, :], cw_ref[2:3, :]
        conv = w2 * z + w1 * zs1 + w0 * zs2
        y_ref[...] = (bg * conv).astype(BF16)

        dov = dout_ref[...]
        wout_bf = wout_v[...]
        dy = _nt(dov.astype(BF16), wout_bf)
        dconv = dy * bg
        dnext = _nt(dnext_ref[...].astype(BF16), wout_bf) * pnext_ref[:, :e] * (1.0 - last)
        nxt = [dnext[0:1, :], dnext[1:2, :]]
        dz = w2 * dconv + w1 * _shift_up(dconv, 1, nxt[:1]) + w0 * _shift_up(dconv, 2, nxt)
        dcw_ref[0:1, :] += _col_sum(dconv * zs2)
        dcw_ref[1:2, :] += _col_sum(dconv * zs1)
        dcw_ref[2:3, :] += _col_sum(dconv * z)
        dp = jnp.concatenate([dy * conv, dz * hx, dz * cg], axis=1).astype(BF16)
        dp_ref[...] = dp
        dh = _nn(dp, win_v[...])
        dxr, dg_row = _rms_bwd(dh, xhat, r, gv)
        dg_ref[...] += dg_row
        dx = dov + dxr
        dx_ref[...] = dx
        dxb_ref[...] = dx.astype(BF16)

    prev_spec = lambda w: pl.BlockSpec((SUBLANES, w), lambda i: (jnp.maximum(i * halo_per_tile - 1, 0), 0))
    next_spec = lambda w: pl.BlockSpec((SUBLANES, w), lambda i: (jnp.minimum((i + 1) * halo_per_tile, n_halo - 1), 0))
    return pl.pallas_call(
        body, name="mixer_b_bwd", grid=(t_tok // tm,),
        in_specs=[_row_spec(tm, d), next_spec(d), _row_spec(tm, d), _row_spec(tm, e3), prev_spec(e3), next_spec(e3),
                  _const_spec((1, d)), _const_spec((SUBLANES, e)), ANY, ANY],
        out_specs=[_row_spec(tm, d), _row_spec(tm, d), _row_spec(tm, d), _row_spec(tm, e), _row_spec(tm, e3),
                   _const_spec((1, d)), _const_spec((SUBLANES, e))],
        out_shape=[jax.ShapeDtypeStruct((t_tok, d), F32), jax.ShapeDtypeStruct((t_tok, d), BF16),
                   jax.ShapeDtypeStruct((t_tok, d), BF16), jax.ShapeDtypeStruct((t_tok, e), BF16),
                   jax.ShapeDtypeStruct((t_tok, e3), BF16), jax.ShapeDtypeStruct((1, d), F32),
                   jax.ShapeDtypeStruct((SUBLANES, e), F32)],
        scratch_shapes=[pltpu.VMEM((e3, d), BF16), pltpu.VMEM((e, d), BF16), pltpu.SemaphoreType.DMA((2,))],
        compiler_params=_params(),
    )(dout, dout, x, p, p, p, g, conv_w, win_t, wout)


def _loss_head(x, target, g, tm):
    t_tok, d = x.shape

    def body(x_ref, t_ref, g_ref, loss_ref, dx_ref, dxb_ref, dg_ref):
        @pl.when(pl.program_id(0) == 0)
        def _():
            loss_ref[...] = jnp.zeros_like(loss_ref)
            dg_ref[...] = jnp.zeros_like(dg_ref)

        gv = g_ref[...]
        y, xhat, r = _rms_fwd(x_ref[...], gv)
        err = y - t_ref[...]
        loss_ref[...] += 0.5 * jnp.sum(_row_mean(err * err), axis=0, keepdims=True)
        dxr, dg_row = _rms_bwd(err * (1.0 / d), xhat, r, gv)
        dg_ref[...] += dg_row
        dx_ref[...] = dxr
        dxb_ref[...] = dxr.astype(BF16)

    return pl.pallas_call(
        body, name="loss_head", grid=(t_tok // tm,),
        in_specs=[_row_spec(tm, d), _row_spec(tm, d), _const_spec((1, d))],
        out_specs=[_const_spec((1, 1)), _row_spec(tm, d), _row_spec(tm, d), _const_spec((1, d))],
        out_shape=[jax.ShapeDtypeStruct((1, 1), F32), jax.ShapeDtypeStruct((t_tok, d), F32),
                   jax.ShapeDtypeStruct((t_tok, d), BF16), jax.ShapeDtypeStruct((1, d), F32)],
        compiler_params=_params(),
    )(x, target, g)


def _wgrad(a, b, bm, name):
    t_tok, m = a.shape
    n = b.shape[1]

    def body(a_ref, b_ref, o_ref):
        o_ref[...] = _tn(a_ref[...], b_ref[...]).astype(o_ref.dtype)

    return pl.pallas_call(
        body, name=name, grid=(m // bm,),
        in_specs=[pl.BlockSpec((t_tok, bm), lambda i: (0, i)), _const_spec((t_tok, n))],
        out_specs=pl.BlockSpec((bm, n), lambda i: (i, 0)),
        out_shape=jax.ShapeDtypeStruct((m, n), BF16),
        compiler_params=_params(sequential=False),
    )(a, b)


def _sum_slots(land, rb, name):
    _, rows, cols = land.shape

    def body(l_ref, o_ref):
        acc = l_ref[0].astype(F32)
        for k in range(1, N_DEV):
            acc = acc + l_ref[k].astype(F32)
        o_ref[...] = acc

    return pl.pallas_call(
        body, name=name, grid=(rows // rb,),
        in_specs=[pl.BlockSpec((N_DEV, rb, cols), lambda i: (0, i, 0))],
        out_specs=pl.BlockSpec((rb, cols), lambda i: (i, 0)),
        out_shape=jax.ShapeDtypeStruct((rows, cols), F32),
        compiler_params=_params(sequential=False),
    )(land)


def _adamw(w, grad, m, v, rb, name):
    rows, cols = w.shape
    c1 = 1.0 / (1.0 - ADAM_B1 ** ADAM_STEP)
    c2 = 1.0 / (1.0 - ADAM_B2 ** ADAM_STEP)

    def body(w_ref, g_ref, m_ref, v_ref, d_ref, mo_ref, vo_ref):
        gv = g_ref[...]
        mn = ADAM_B1 * m_ref[...] + (1.0 - ADAM_B1) * gv
        vn = ADAM_B2 * v_ref[...] + (1.0 - ADAM_B2) * (gv * gv)
        mo_ref[...] = mn
        vo_ref[...] = vn
        d_ref[...] = -ADAM_LR * ((mn * c1) / (jnp.sqrt(vn * c2) + ADAM_EPS) + ADAM_WD * w_ref[...])

    spec = pl.BlockSpec((rb, cols), lambda i: (i, 0))
    shape = jax.ShapeDtypeStruct((rows, cols), F32)
    return pl.pallas_call(
        body, name=name, grid=(rows // rb,),
        in_specs=[spec] * 4, out_specs=[spec] * 3, out_shape=[shape] * 3,
        compiler_params=_params(sequential=False),
    )(w, grad, m, v)


def _my_index():
    return 4 * lax.axis_index("x") + 2 * lax.axis_index("y") + lax.axis_index("c")


def _all_gather(shards, name):
    n_arr = len(shards)

    def body(*refs):
        ins, outs = refs[:n_arr], refs[n_arr:2 * n_arr]
        send_sems, recv_sems, local_sems = refs[2 * n_arr:]
        x, y, c = lax.axis_index("x"), lax.axis_index("y"), lax.axis_index("c")
        me, sibling = (x, y, c), (x, y, 1 - c)
        chips = [(1 - x, y), (x, 1 - y), (1 - x, 1 - y)]

        def slot(a, dev):
            return outs[a].at[4 * dev[0] + 2 * dev[1] + dev[2]]

        def copy(a, k, block, to, src=None):
            return pltpu.make_async_remote_copy(
                src_ref=slot(a, block) if src is None else src, dst_ref=slot(a, block),
                send_sem=send_sems.at[a, k], recv_sem=recv_sems.at[a, k], device_id=to, device_id_type=MESH)

        mine = [pltpu.make_async_copy(ins[a], slot(a, me), local_sems.at[a]) for a in range(n_arr)]
        for cp in mine:
            cp.start()
        first = []
        for a in range(n_arr):
            first.append(copy(a, 0, me, sibling, src=ins[a]))
            first += [copy(a, 1 + j, me, (*chip, c), src=ins[a]) for j, chip in enumerate(chips)]
        for cp in first:
            cp.start()
        passed = []
        for j, chip in enumerate(chips):
            for a in range(n_arr):
                copy(a, 1 + j, (*chip, c), me).wait_recv()
                cp = copy(a, 4 + j, (*chip, c), sibling)
                cp.start()
                passed.append(cp)
        for a in range(n_arr):
            copy(a, 0, sibling, me).wait_recv()
            for j, chip in enumerate(chips):
                copy(a, 4 + j, (*chip, 1 - c), me).wait_recv()
        for cp in first + passed:
            cp.wait_send()
        for cp in mine:
            cp.wait()

    return pl.pallas_call(
        body, name=name,
        in_specs=[ANY] * n_arr, out_specs=[ANY] * n_arr,
        out_shape=[jax.ShapeDtypeStruct((N_DEV,) + s.shape, s.dtype) for s in shards],
        scratch_shapes=[pltpu.SemaphoreType.DMA((n_arr, N_DEV - 1)), pltpu.SemaphoreType.DMA((n_arr, N_DEV - 1)),
                        pltpu.SemaphoreType.DMA((n_arr,))],
    )(*shards)


def _scatter_blocks(arrays, entries, land_rows, cols, dtype, name):
    n_arr = len(arrays)
    n_ent = len(entries)
    offsets = []
    total = 0
    for _, _, n in entries:
        offsets.append(total)
        total += n
    assert total == land_rows

    def body(*refs):
        ins, land = refs[:n_arr], refs[n_arr]
        send_sems, recv_sems, local_sems = refs[n_arr + 1:]
        me = _my_index()

        def block(e, p):
            ai, first, n = entries[e]
            return ins[ai].at[pl.ds(first + p * n, n), :]

        def dest(e, q):
            return land.at[q, pl.ds(offsets[e], entries[e][2]), :]

        local = [pltpu.make_async_copy(block(e, me), dest(e, me), local_sems.at[e]) for e in range(n_ent)]
        for cp in local:
            cp.start()
        sends = []
        for k in range(1, N_DEV):
            p = (me + k) % N_DEV
            to = (p // 4, (p // 2) % 2, p % 2)
            for e in range(n_ent):
                cp = pltpu.make_async_remote_copy(
                    src_ref=block(e, p), dst_ref=dest(e, me), send_sem=send_sems.at[e, k - 1],
                    recv_sem=recv_sems.at[e, k - 1], device_id=to, device_id_type=MESH)
                cp.start()
                sends.append(cp)
        for k in range(1, N_DEV):
            q = (me + N_DEV - k) % N_DEV
            for e in range(n_ent):
                pltpu.make_async_remote_copy(
                    src_ref=block(e, me), dst_ref=dest(e, q), send_sem=send_sems.at[e, k - 1],
                    recv_sem=recv_sems.at[e, k - 1], device_id=(0, 0, 0), device_id_type=MESH).wait_recv()
        for cp in sends:
            cp.wait_send()
        for cp in local:
            cp.wait()

    return pl.pallas_call(
        body, name=name,
        in_specs=[ANY] * n_arr, out_specs=ANY,
        out_shape=jax.ShapeDtypeStruct((N_DEV, land_rows, cols), dtype),
        scratch_shapes=[pltpu.SemaphoreType.DMA((n_ent, N_DEV - 1)), pltpu.SemaphoreType.DMA((n_ent, N_DEV - 1)),
                        pltpu.SemaphoreType.DMA((n_ent,))],
    )(*arrays)


def _pack_small(parts, rows):
    flat = jnp.concatenate([p.reshape(-1).astype(F32) for p in parts])
    return jnp.pad(flat, (0, rows * LANES - flat.shape[0])).reshape(rows, LANES)


def _unpack_small(packed, shapes):
    flat = packed.reshape(-1)
    out = []
    pos = 0
    for s in shapes:
        n = math.prod(s)
        out.append(flat[pos:pos + n].reshape(s))
        pos += n
    return out


def kernel(x, mix_norm, ffn_norm, a_w_in, a_v_gain, a_v_bias, a_w_s, a_b_s, a_w_out, b_w_in, b_conv_w, b_w_out, ffn_w_gate, ffn_w_up, ffn_w_down, final_norm, loss_target, m_mix_norm, m_ffn_norm, m_a_w_in, m_a_v_gain, m_a_v_bias, m_a_w_s, m_a_b_s, m_a_w_out, m_b_w_in, m_b_conv_w, m_b_w_out, m_ffn_w_gate, m_ffn_w_up, m_ffn_w_down, m_final_norm, v_mix_norm, v_ffn_norm, v_a_w_in, v_a_v_gain, v_a_v_bias, v_a_w_s, v_a_b_s, v_a_w_out, v_b_w_in, v_b_conv_w, v_b_w_out, v_ffn_w_gate, v_ffn_w_up, v_ffn_w_down, v_final_norm):
    bsz, seq, d = x.shape
    t_tok = bsz * seq
    me = _my_index()
    xt = x.reshape(t_tok, d)
    target = loss_target.reshape(t_tok, d)
    e_a = a_v_gain.shape[1]
    e_b = b_w_out.shape[1] * N_DEV
    n_layers = ffn_w_gate.shape[0]

    conv_pad = jnp.pad(b_conv_w[0], ((0, SUBLANES - CONV_W), (0, 0)))
    shards = [a_w_in[0].T.astype(BF16), a_w_out[0].astype(BF16), b_w_in[0].T.astype(BF16), b_w_out[0].astype(BF16)]
    for l in range(n_layers):
        shards += [ffn_w_gate[l].T.astype(BF16), ffn_w_up[l].T.astype(BF16), ffn_w_down[l].astype(BF16)]
    gathered = _all_gather(shards + [conv_pad], "gather_weights")
    full = [w.reshape(N_DEV * w.shape[1], d) for w in gathered[:-1]]
    a_in_t, a_out, b_in_t, b_out = full[:4]
    ffn_w = [full[4 + 3 * l:7 + 3 * l] for l in range(n_layers)]
    conv_full = jnp.pad(gathered[-1][:, :CONV_W, :].transpose(1, 0, 2).reshape(CONV_W, e_b),
                        ((0, SUBLANES - CONV_W), (0, 0)))
    bfull = jnp.repeat(a_b_s[0].T, GROUP, axis=1)

    x1, zpre = _mixer_a_fwd(xt, mix_norm[0:1], a_in_t, a_v_gain, a_v_bias, a_w_s[0], bfull, a_out, tm=128)
    x2, gate0, up0 = _ffn_fwd(x1, ffn_norm[0:1], *ffn_w[0], tm=256)
    x3, p_b = _mixer_b_fwd(x2, mix_norm[1:2], b_in_t, conv_full, b_out, tm=256, seq=seq)
    x4, gate1, up1 = _ffn_fwd(x3, ffn_norm[1:2], *ffn_w[1], tm=256)
    loss_part, dx4, dx4_bf, d_final = _loss_head(x4, target, final_norm.reshape(1, d), tm=512)

    dx3, dx3_bf, h_f1, act1, dgu1, d_fn1 = _ffn_bwd(dx4, x3, gate1, up1, ffn_norm[1:2], *ffn_w[1], tm=128)
    g_down1 = _wgrad(act1, dx4_bf, 256, "wgrad_down1")
    g_gu1 = _wgrad(dgu1, h_f1, 256, "wgrad_gate_up1")
    dx2, dx2_bf, h_b, y_b, dp_b, d_mn1, d_conv = _mixer_b_bwd(dx3, x2, p_b, mix_norm[1:2], b_in_t, conv_full, b_out,
                                                              tm=128, seq=seq)
    g_b_out = _wgrad(y_b, dx3_bf, 256, "wgrad_b_out")
    g_b_in = _wgrad(dp_b, h_b, 256, "wgrad_b_in")
    dx1, dx1_bf, h_f0, act0, dgu0, d_fn0 = _ffn_bwd(dx2, x1, gate0, up0, ffn_norm[0:1], *ffn_w[0], tm=128)
    g_down0 = _wgrad(act0, dx2_bf, 256, "wgrad_down0")
    g_gu0 = _wgrad(dgu0, h_f0, 256, "wgrad_gate_up0")
    (dx0, _, h_a, y_a, dz_a, d_mn0, d_gain, d_bias, d_ws, d_bs_acc) = _mixer_a_bwd(
        dx1, xt, zpre, mix_norm[0:1], a_in_t, a_v_gain, a_v_bias, a_w_s[0], bfull, a_out, tm=128)
    g_a_out = _wgrad(y_a, dx1_bf, 256, "wgrad_a_out")
    g_a_in = _wgrad(dz_a, h_a, 256, "wgrad_a_in")
    d_bs = d_bs_acc.reshape(CHUNK, HEADS, GROUP).sum(axis=2).T

    f_shard = ffn_w_gate.shape[2]
    f_full = f_shard * N_DEV
    arrays = [g_a_in, g_a_out, g_b_in, g_b_out, g_gu0, g_down0, g_gu1, g_down1]
    entries = [(0, 0, a_w_in.shape[2]), (1, 0, a_w_out.shape[1]), (2, 0, b_w_in.shape[2]), (3, 0, b_w_out.shape[1]),
               (4, 0, f_shard), (4, f_full, f_shard), (5, 0, f_shard),
               (6, 0, f_shard), (6, f_full, f_shard), (7, 0, f_shard)]
    land_rows = sum(n for _, _, n in entries)
    land = _scatter_blocks(arrays, entries, land_rows, d, BF16, "scatter_grads")
    g_all = _sum_slots(land, land_rows // 4, "sum_grads")

    small_grads = [jnp.concatenate([d_mn0, d_mn1]), jnp.concatenate([d_fn0, d_fn1]), d_gain, d_bias, d_ws, d_bs,
                   d_final, d_conv[:CONV_W]]
    small_shapes = [(n_layers, d), (n_layers, d), (1, e_a), (1, e_a), (1, HEADS, CHUNK, CHUNK), (1, HEADS, CHUNK), (d,),
                    (CONV_W, e_b)]
    n_small = sum(math.prod(s) for s in small_shapes)
    blk_rows = -(-n_small // (N_DEV * LANES * SUBLANES)) * SUBLANES
    small_rows = blk_rows * N_DEV
    packed = _pack_small(small_grads, small_rows)
    small_land = _scatter_blocks([packed], [(0, 0, blk_rows)], blk_rows, LANES, F32, "scatter_small")
    small_sum = _sum_slots(small_land, blk_rows, "sum_small")
    small_all = _all_gather([small_sum], "gather_small")[0].reshape(small_rows, LANES)
    (gr_mix, gr_ffn, gr_gain, gr_bias, gr_ws, gr_bs, gr_final, gr_conv_full) = _unpack_small(small_all, small_shapes)
    gr_conv = lax.dynamic_slice_in_dim(gr_conv_full, me * (e_b // N_DEV), e_b // N_DEV, axis=1)[None]

    pos = 0
    pieces = []
    for _, _, n in entries:
        pieces.append(g_all[pos:pos + n])
        pos += n
    gr_a_in = pieces[0].T[None]
    gr_a_out = pieces[1][None]
    gr_b_in = pieces[2].T[None]
    gr_b_out = pieces[3][None]
    gr_gate = jnp.stack([pieces[4].T, pieces[7].T])
    gr_up = jnp.stack([pieces[5].T, pieces[8].T])
    gr_down = jnp.stack([pieces[6], pieces[9]])

    def big(w, g, m, v, name):
        shape = w.shape
        two_d = (math.prod(shape[:-1]), shape[-1])
        rb = two_d[0] // 2
        out = _adamw(w.reshape(two_d), g.reshape(two_d), m.reshape(two_d), v.reshape(two_d), rb, name)
        return [o.reshape(shape) for o in out]

    small_w = [mix_norm, ffn_norm, a_v_gain, a_v_bias, a_w_s, a_b_s, final_norm]
    small_m = [m_mix_norm, m_ffn_norm, m_a_v_gain, m_a_v_bias, m_a_w_s, m_a_b_s, m_final_norm]
    small_v = [v_mix_norm, v_ffn_norm, v_a_v_gain, v_a_v_bias, v_a_w_s, v_a_b_s, v_final_norm]
    small_g = [gr_mix, gr_ffn, gr_gain, gr_bias, gr_ws, gr_bs, gr_final]
    sm_shapes = small_shapes[:-1]
    sm_out = _adamw(_pack_small(small_w, small_rows), _pack_small(small_g, small_rows), _pack_small(small_m, small_rows),
                    _pack_small(small_v, small_rows), small_rows, "adamw_small")
    sm_delta, sm_m, sm_v = [_unpack_small(o, sm_shapes) for o in sm_out]

    conv_out = _adamw(b_conv_w[0], gr_conv[0], m_b_conv_w[0], v_b_conv_w[0], CONV_W, "adamw_conv")
    conv_delta, conv_m, conv_v = [o[None] for o in conv_out]

    res = {
        "a_w_in": big(a_w_in, gr_a_in, m_a_w_in, v_a_w_in, "adamw_a_in"),
        "a_w_out": big(a_w_out, gr_a_out, m_a_w_out, v_a_w_out, "adamw_a_out"),
        "b_w_in": big(b_w_in, gr_b_in, m_b_w_in, v_b_w_in, "adamw_b_in"),
        "b_w_out": big(b_w_out, gr_b_out, m_b_w_out, v_b_w_out, "adamw_b_out"),
        "ffn_w_gate": big(ffn_w_gate, gr_gate, m_ffn_w_gate, v_ffn_w_gate, "adamw_gate"),
        "ffn_w_up": big(ffn_w_up, gr_up, m_ffn_w_up, v_ffn_w_up, "adamw_up"),
        "ffn_w_down": big(ffn_w_down, gr_down, m_ffn_w_down, v_ffn_w_down, "adamw_down"),
    }

    order = ["mix_norm", "ffn_norm", "a_w_in", "a_v_gain", "a_v_bias", "a_w_s", "a_b_s", "a_w_out", "b_w_in",
             "b_conv_w", "b_w_out", "ffn_w_gate", "ffn_w_up", "ffn_w_down", "final_norm"]
    small_names = ["mix_norm", "ffn_norm", "a_v_gain", "a_v_bias", "a_w_s", "a_b_s", "final_norm"]
    grads = {"a_w_in": gr_a_in, "a_w_out": gr_a_out, "b_w_in": gr_b_in, "b_w_out": gr_b_out, "ffn_w_gate": gr_gate,
             "ffn_w_up": gr_up, "ffn_w_down": gr_down, "b_conv_w": gr_conv}
    deltas, new_m, new_v = {}, {}, {}
    for k, name in enumerate(small_names):
        grads[name] = small_g[k]
        deltas[name], new_m[name], new_v[name] = sm_delta[k], sm_m[k], sm_v[k]
    deltas["b_conv_w"], new_m["b_conv_w"], new_v["b_conv_w"] = conv_delta, conv_m, conv_v
    for name, (dl, mm, vv) in res.items():
        deltas[name], new_m[name], new_v[name] = dl, mm, vv

    loss = lax.psum(loss_part[0, 0], ("x", "y", "c"))
    grad_x = dx0.reshape(bsz, seq, d)
    return (loss, grad_x, *[grads[n] for n in order], *[deltas[n] for n in order],
            *[new_m[n] for n in order], *[new_v[n] for n in order])
```

```python
import math

import jax
import jax.numpy as jnp
from jax import lax
from jax.experimental import pallas as pl
from jax.experimental.pallas import tpu as pltpu

F32 = jnp.float32
BF16 = jnp.bfloat16

N_DEV = 8
N_CHIP = 4
CHUNK = 128
HEADS = 16
GROUP = 128
CONV_W = 3
NORM_EPS = 1e-6
GELU_C = math.sqrt(2.0 / math.pi)
GELU_K = 0.044715

ADAM_LR = 0.001
ADAM_B1 = 0.9
ADAM_B2 = 0.999
ADAM_EPS = 1e-08
ADAM_WD = 0.01
ADAM_STEP = 10

LANES = 128
SUBLANES = 8
VMEM_LIMIT = 60 * 1024 * 1024

MESH = pl.DeviceIdType.MESH
ANY = pl.BlockSpec(memory_space=pl.ANY)


def _params(sequential=True):
    return pltpu.CompilerParams(
        dimension_semantics=("arbitrary",) if sequential else None,
        vmem_limit_bytes=VMEM_LIMIT)


def _nn(a, b):
    return jnp.dot(a, b, preferred_element_type=F32)


def _nt(a, b):
    return lax.dot_general(a, b, (((1,), (1,)), ((), ())), preferred_element_type=F32)


def _tn(a, b):
    return lax.dot_general(a, b, (((0,), (0,)), ((), ())), preferred_element_type=F32)


def _row_mean(a):
    return jnp.mean(a, axis=-1, keepdims=True)


def _col_sum(a):
    return jnp.sum(a, axis=0, keepdims=True)


def _rms_fwd(x, g):
    r = lax.rsqrt(_row_mean(x * x) + NORM_EPS)
    xhat = x * r
    return xhat * g, xhat, r


def _rms_bwd(dh, xhat, r, g):
    a = dh * g
    dx = r * (a - xhat * _row_mean(a * xhat))
    return dx, _col_sum(dh * xhat)


def _gelu_and_grad(x):
    x2 = x * x
    t = jnp.tanh(GELU_C * (x + GELU_K * x2 * x))
    half = 0.5 * (1.0 + t)
    d = half + 0.5 * x * (1.0 - t * t) * (GELU_C * (1.0 + 3.0 * GELU_K * x2))
    return x * half, d


def _sigmoid(x):
    return 1.0 / (1.0 + jnp.exp(-x))


def _row_spec(tm, width):
    return pl.BlockSpec((tm, width), lambda i: (i, 0))


def _const_spec(shape):
    nd = len(shape)
    return pl.BlockSpec(shape, lambda i: (0,) * nd)


def _load_group(gath_ref, parts, sems):
    @pl.when(pl.program_id(0) == 0)
    def _():
        copies = []
        for k, (first, n, dst) in enumerate(parts):
            for j in range(N_DEV):
                copies.append(pltpu.make_async_copy(gath_ref.at[j, pl.ds(first, n), :], dst.at[pl.ds(j * n, n), :],
                                                    sems.at[k * N_DEV + j]))
        for cp in copies:
            cp.start()
        for cp in copies:
            cp.wait()


def _hosting_call(body, name, n_steps, arrays, in_specs, out_specs, out_shape, scratch, hosted=None):
    n_in, n_out, n_scr = len(arrays), len(out_shape), len(scratch)
    h_arrays = list(hosted.arrays) if hosted else []
    h_shapes = list(hosted.out_shapes) if hosted else []
    h_sems = list(hosted.sem_shapes) if hosted else []

    def full_body(*refs):
        pos = 0
        groups = []
        for n in (n_in, len(h_arrays), n_out, len(h_shapes), n_scr, len(h_sems)):
            groups.append(refs[pos:pos + n])
            pos += n
        own_in, h_in, own_out, h_out, own_scr, h_sem = groups
        if hosted:
            hosted.begin(h_in, h_out, h_sem, n_steps)
        body(*own_in, *own_out, *own_scr)
        if hosted:
            hosted.end(h_in, h_out, h_sem, n_steps)

    outs = pl.pallas_call(
        full_body, name=name, grid=(n_steps,),
        in_specs=list(in_specs) + [ANY] * len(h_arrays),
        out_specs=list(out_specs) + [ANY] * len(h_shapes),
        out_shape=list(out_shape) + h_shapes,
        scratch_shapes=list(scratch) + h_sems,
        compiler_params=_params(),
    )(*arrays, *h_arrays)
    return outs[:n_out], outs[n_out:]


def _my_index():
    return 4 * lax.axis_index("x") + 2 * lax.axis_index("y") + lax.axis_index("c")


class _Gather:
    def __init__(self, shard, out, send_sems, recv_sems, local_sem):
        self.shard, self.out = shard, out
        self.send_sems, self.recv_sems, self.local_sem = send_sems, recv_sems, local_sem
        x, y, c = lax.axis_index("x"), lax.axis_index("y"), lax.axis_index("c")
        self.c = c
        self.me, self.sibling = (x, y, c), (x, y, 1 - c)
        self.chips = [(1 - x, y), (x, 1 - y), (1 - x, 1 - y)]

    def _slot(self, dev):
        return self.out.at[4 * dev[0] + 2 * dev[1] + dev[2]]

    def _copy(self, k, block, to, src=None):
        return pltpu.make_async_remote_copy(
            src_ref=self._slot(block) if src is None else src, dst_ref=self._slot(block),
            send_sem=self.send_sems.at[k], recv_sem=self.recv_sems.at[k], device_id=to, device_id_type=MESH)

    def _local(self):
        return pltpu.make_async_copy(self.shard, self._slot(self.me), self.local_sem)

    def start(self):
        self._local().start()
        self._copy(0, self.me, self.sibling, src=self.shard).start()
        for j, chip in enumerate(self.chips):
            self._copy(1 + j, self.me, (*chip, self.c), src=self.shard).start()

    def forward(self):
        for j, chip in enumerate(self.chips):
            self._copy(1 + j, (*chip, self.c), self.me).wait_recv()
            self._copy(4 + j, (*chip, self.c), self.sibling).start()

    def finish(self):
        self._copy(0, self.sibling, self.me).wait_recv()
        for j, chip in enumerate(self.chips):
            self._copy(4 + j, (*chip, 1 - self.c), self.me).wait_recv()
        for k in range(N_DEV - 1):
            self._copy(k, self.me, self.sibling).wait_send()
        self._local().wait()


class _HostedGathers:
    def __init__(self, shards, mid_lead):
        n = len(shards)
        self.arrays = shards
        self.mid_lead = mid_lead
        self.out_shapes = [jax.ShapeDtypeStruct((N_DEV,) + s.shape, s.dtype) for s in shards]
        self.sem_shapes = [pltpu.SemaphoreType.DMA((n, N_DEV - 1)), pltpu.SemaphoreType.DMA((n, N_DEV - 1)),
                           pltpu.SemaphoreType.DMA((n,))]

    def _gathers(self, ins, outs, sems):
        return [_Gather(ins[a], outs[a], sems[0].at[a], sems[1].at[a], sems[2].at[a]) for a in range(len(ins))]

    def begin(self, ins, outs, sems, n_steps):
        i = pl.program_id(0)

        @pl.when(i == 0)
        def _():
            for g in self._gathers(ins, outs, sems):
                g.start()

        @pl.when(i == max(n_steps - 1 - self.mid_lead, 0))
        def _():
            for g in self._gathers(ins, outs, sems):
                g.forward()

    def end(self, ins, outs, sems, n_steps):
        @pl.when(pl.program_id(0) == n_steps - 1)
        def _():
            for g in self._gathers(ins, outs, sems):
                g.finish()


def _all_gather(shards, name):
    n_arr = len(shards)
    host = _HostedGathers(shards, 0)

    def body(*refs):
        gathers = host._gathers(refs[:n_arr], refs[n_arr:2 * n_arr], refs[2 * n_arr:])
        for g in gathers:
            g.start()
        for g in gathers:
            g.forward()
        for g in gathers:
            g.finish()

    return pl.pallas_call(
        body, name=name, in_specs=[ANY] * n_arr, out_specs=[ANY] * n_arr,
        out_shape=host.out_shapes, scratch_shapes=host.sem_shapes,
    )(*shards)


class _ChipScatter:
    def __init__(self, pairsum, land, send_sems, recv_sems, local_sem):
        self.pairsum, self.land = pairsum, land
        self.send_sems, self.recv_sems, self.local_sem = send_sems, recv_sems, local_sem
        x, y, c = lax.axis_index("x"), lax.axis_index("y"), lax.axis_index("c")
        self.c = c
        self.chip = 2 * x + y
        self.others = [(1 - x, y), (x, 1 - y), (1 - x, 1 - y)]

    def _copy(self, k):
        ox, oy = self.others[k]
        return pltpu.make_async_remote_copy(
            src_ref=self.pairsum.at[2 * ox + oy], dst_ref=self.land.at[self.chip],
            send_sem=self.send_sems.at[k], recv_sem=self.recv_sems.at[k], device_id=(ox, oy, self.c),
            device_id_type=MESH)

    def _arrival(self, k):
        ox, oy = self.others[k]
        return pltpu.make_async_remote_copy(
            src_ref=self.pairsum.at[self.chip], dst_ref=self.land.at[2 * ox + oy],
            send_sem=self.send_sems.at[k], recv_sem=self.recv_sems.at[k], device_id=(ox, oy, self.c),
            device_id_type=MESH)

    def _local(self):
        return pltpu.make_async_copy(self.pairsum.at[self.chip], self.land.at[self.chip], self.local_sem)

    def start(self):
        self._local().start()
        for k in range(N_CHIP - 1):
            self._copy(k).start()

    def finish(self):
        for k in range(N_CHIP - 1):
            self._arrival(k).wait_recv()
        for k in range(N_CHIP - 1):
            self._copy(k).wait_send()
        self._local().wait()


class _HostedChipScatter:
    def __init__(self, pairsum):
        self.arrays = [pairsum]
        self.out_shapes = [jax.ShapeDtypeStruct(pairsum.shape, pairsum.dtype)]
        self.sem_shapes = [pltpu.SemaphoreType.DMA((N_CHIP - 1,)), pltpu.SemaphoreType.DMA((N_CHIP - 1,)),
                           pltpu.SemaphoreType.DMA(())]

    def begin(self, ins, outs, sems, n_steps):
        @pl.when(pl.program_id(0) == 0)
        def _():
            _ChipScatter(ins[0], outs[0], *sems).start()

    def end(self, ins, outs, sems, n_steps):
        @pl.when(pl.program_id(0) == n_steps - 1)
        def _():
            _ChipScatter(ins[0], outs[0], *sems).finish()


def _chip_scatter(pairsum, name):
    host = _HostedChipScatter(pairsum)

    def body(ps_ref, land_ref, *sems):
        cs = _ChipScatter(ps_ref, land_ref, *sems)
        cs.start()
        cs.finish()

    return pl.pallas_call(
        body, name=name, in_specs=[ANY], out_specs=ANY, out_shape=host.out_shapes[0], scratch_shapes=host.sem_shapes,
    )(pairsum)


def _pair_reduce(arrays, entries, name):
    n_arr, n_ent = len(arrays), len(entries)
    cols = arrays[0].shape[1]
    offsets = []
    total = 0
    for _, _, n in entries:
        offsets.append(total)
        total += n

    def body(*refs):
        ins, out_ref = refs[:n_arr], refs[n_arr]
        rbuf, own, send_sems, recv_sems, own_sems = refs[n_arr + 1:]
        q = pl.program_id(0)
        x, y, c = lax.axis_index("x"), lax.axis_index("y"), lax.axis_index("c")

        def block(e, chip, core):
            ai, first, n = entries[e]
            return ins[ai].at[pl.ds(first + (2 * chip + core) * n, n), :]

        def to_sibling(e, chip):
            return pltpu.make_async_remote_copy(
                src_ref=block(e, chip, 1 - c), dst_ref=rbuf.at[chip, pl.ds(offsets[e], entries[e][2]), :],
                send_sem=send_sems.at[e, chip], recv_sem=recv_sems.at[e, chip], device_id=(x, y, 1 - c),
                device_id_type=MESH)

        @pl.when(q == 0)
        def _():
            for chip in range(N_CHIP):
                for e in range(n_ent):
                    to_sibling(e, chip).start()

        loads = [pltpu.make_async_copy(block(e, q, c), own.at[pl.ds(offsets[e], entries[e][2]), :], own_sems.at[e])
                 for e in range(n_ent)]
        for cp in loads:
            cp.start()
        for cp in loads:
            cp.wait()
        for e in range(n_ent):
            to_sibling(e, q).wait_recv()
        out_ref[...] = (own[...].astype(F32) + rbuf[q].astype(F32)).astype(out_ref.dtype)

        @pl.when(q == N_CHIP - 1)
        def _():
            for chip in range(N_CHIP):
                for e in range(n_ent):
                    to_sibling(e, chip).wait_send()

    return pl.pallas_call(
        body, name=name, grid=(N_CHIP,),
        in_specs=[ANY] * n_arr,
        out_specs=pl.BlockSpec((None, total, cols), lambda q: (q, 0, 0)),
        out_shape=jax.ShapeDtypeStruct((N_CHIP, total, cols), BF16),
        scratch_shapes=[pltpu.VMEM((N_CHIP, total, cols), BF16), pltpu.VMEM((total, cols), BF16),
                        pltpu.SemaphoreType.DMA((n_ent, N_CHIP)), pltpu.SemaphoreType.DMA((n_ent, N_CHIP)),
                        pltpu.SemaphoreType.DMA((n_ent,))],
        compiler_params=_params(),
    )(*arrays)


def _scatter_blocks(arrays, entries, land_rows, cols, dtype, name):
    n_arr = len(arrays)
    n_ent = len(entries)
    offsets = []
    total = 0
    for _, _, n in entries:
        offsets.append(total)
        total += n
    assert total == land_rows

    def body(*refs):
        ins, land = refs[:n_arr], refs[n_arr]
        send_sems, recv_sems, local_sems = refs[n_arr + 1:]
        me = _my_index()

        def block(e, p):
            ai, first, n = entries[e]
            return ins[ai].at[pl.ds(first + p * n, n), :]

        def dest(e, q):
            return land.at[q, pl.ds(offsets[e], entries[e][2]), :]

        local = [pltpu.make_async_copy(block(e, me), dest(e, me), local_sems.at[e]) for e in range(n_ent)]
        for cp in local:
            cp.start()
        sends = []
        for k in range(1, N_DEV):
            p = (me + k) % N_DEV
            to = (p // 4, (p // 2) % 2, p % 2)
            for e in range(n_ent):
                cp = pltpu.make_async_remote_copy(
                    src_ref=block(e, p), dst_ref=dest(e, me), send_sem=send_sems.at[e, k - 1],
                    recv_sem=recv_sems.at[e, k - 1], device_id=to, device_id_type=MESH)
                cp.start()
                sends.append(cp)
        for k in range(1, N_DEV):
            q = (me + N_DEV - k) % N_DEV
            for e in range(n_ent):
                pltpu.make_async_remote_copy(
                    src_ref=block(e, me), dst_ref=dest(e, q), send_sem=send_sems.at[e, k - 1],
                    recv_sem=recv_sems.at[e, k - 1], device_id=(0, 0, 0), device_id_type=MESH).wait_recv()
        for cp in sends:
            cp.wait_send()
        for cp in local:
            cp.wait()

    return pl.pallas_call(
        body, name=name,
        in_specs=[ANY] * n_arr, out_specs=ANY,
        out_shape=jax.ShapeDtypeStruct((N_DEV, land_rows, cols), dtype),
        scratch_shapes=[pltpu.SemaphoreType.DMA((n_ent, N_DEV - 1)), pltpu.SemaphoreType.DMA((n_ent, N_DEV - 1)),
                        pltpu.SemaphoreType.DMA((n_ent,))],
    )(*arrays)


def _tril_weights(ws_ref):
    r = lax.broadcasted_iota(jnp.int32, (CHUNK, CHUNK), 0)
    c = lax.broadcasted_iota(jnp.int32, (CHUNK, CHUNK), 1)
    return [jnp.where(r >= c, ws_ref[h], 0.0).astype(BF16) for h in range(HEADS)]


def _sgu_stats(zpre, gain, bias):
    e = zpre.shape[1] // 2
    z, dz = _gelu_and_grad(zpre)
    u, v = z[:, :e], z[:, e:]
    vc = v - _row_mean(v)
    rstd = lax.rsqrt(_row_mean(vc * vc) + NORM_EPS)
    vhat = vc * rstd
    return u, vhat, rstd, vhat * gain + bias, dz


def _spatial_fwd(wt, vn_bf, bfull_ref, sv_ref, tm):
    for ci in range(tm // CHUNK):
        rows = slice(ci * CHUNK, (ci + 1) * CHUNK)
        for h in range(HEADS):
            cols = slice(h * GROUP, (h + 1) * GROUP)
            sv_ref[rows, cols] = _nn(wt[h], vn_bf[rows, cols]) + bfull_ref[:, cols]


def _mixer_a_fwd(x, g, gath, gain, bias, ws, bfull, tm, hosted=None):
    t_tok, d = x.shape
    e = gain.shape[1]
    e2 = 2 * e
    n_in, n_out = e2 // N_DEV, e // N_DEV

    def body(x_ref, g_ref, gain_ref, bias_ref, ws_ref, bfull_ref, gath_ref,
             xo_ref, zpre_ref, win_v, wout_v, sv_v, sems):
        _load_group(gath_ref, [(0, n_in, win_v), (n_in, n_out, wout_v)], sems)
        xv = x_ref[...]
        h = _rms_fwd(xv, g_ref[...])[0].astype(BF16)
        zpre = _nt(h, win_v[...])
        zpre_ref[...] = zpre
        u, _, _, vn, _ = _sgu_stats(zpre, gain_ref[...], bias_ref[...])
        _spatial_fwd(_tril_weights(ws_ref), vn.astype(BF16), bfull_ref, sv_v, tm)
        y = (u * sv_v[...]).astype(BF16)
        xo_ref[...] = xv + _nn(y, wout_v[...])

    return _hosting_call(
        body, "mixer_a_fwd", t_tok // tm, [x, g, gain, bias, ws, bfull, gath],
        in_specs=[_row_spec(tm, d), _const_spec((1, d)), _const_spec((1, e)), _const_spec((1, e)),
                  _const_spec((HEADS, CHUNK, CHUNK)), _const_spec((CHUNK, e)), ANY],
        out_specs=[_row_spec(tm, d), _row_spec(tm, e2)],
        out_shape=[jax.ShapeDtypeStruct((t_tok, d), F32), jax.ShapeDtypeStruct((t_tok, e2), F32)],
        scratch=[pltpu.VMEM((e2, d), BF16), pltpu.VMEM((e, d), BF16), pltpu.VMEM((tm, e), F32),
                 pltpu.SemaphoreType.DMA((2 * N_DEV,))],
        hosted=hosted)


def _mixer_a_bwd(dout, x, zpre, g, gath, gain, bias, ws, bfull, tm, hosted=None):
    t_tok, d = x.shape
    e = gain.shape[1]
    e2 = 2 * e
    n_in, n_out = e2 // N_DEV, e // N_DEV
    n_steps = t_tok // tm

    def body(dout_ref, x_ref, zpre_ref, g_ref, gain_ref, bias_ref, ws_ref, bfull_ref, gath_ref,
             dx_ref, dxb_ref, h_ref, y_ref, dz_ref, dg_ref, dgain_ref, dbias_ref, dws_ref, dbs_ref,
             win_v, wout_v, sv_v, dvn_v, sems):
        i = pl.program_id(0)
        _load_group(gath_ref, [(0, n_in, win_v), (n_in, n_out, wout_v)], sems)

        @pl.when(i == 0)
        def _():
            dg_ref[...] = jnp.zeros_like(dg_ref)
            dgain_ref[...] = jnp.zeros_like(dgain_ref)
            dbias_ref[...] = jnp.zeros_like(dbias_ref)
            dws_ref[...] = jnp.zeros_like(dws_ref)
            dbs_ref[...] = jnp.zeros_like(dbs_ref)

        xv = x_ref[...]
        gv = g_ref[...]
        hv, xhat, r = _rms_fwd(xv, gv)
        h_ref[...] = hv.astype(BF16)
        gain_v = gain_ref[...]
        u, vhat, rstd, vn, gelu_d = _sgu_stats(zpre_ref[...], gain_v, bias_ref[...])
        vn_bf = vn.astype(BF16)
        wt = _tril_weights(ws_ref)
        _spatial_fwd(wt, vn_bf, bfull_ref, sv_v, tm)
        sv = sv_v[...]
        y_ref[...] = (u * sv).astype(BF16)

        dov = dout_ref[...]
        dy = _nt(dov.astype(BF16), wout_v[...])
        du = dy * sv
        dsv = dy * u
        dsv_bf = dsv.astype(BF16)
        for ci in range(tm // CHUNK):
            rows = slice(ci * CHUNK, (ci + 1) * CHUNK)
            dbs_ref[...] += dsv[rows, :]
            for h in range(HEADS):
                cols = slice(h * GROUP, (h + 1) * GROUP)
                dvn_v[rows, cols] = _tn(wt[h], dsv_bf[rows, cols])
                dws_ref[h] += _nt(dsv_bf[rows, cols], vn_bf[rows, cols])
        dvn = dvn_v[...]
        dgain_ref[...] += _col_sum(dvn * vhat)
        dbias_ref[...] += _col_sum(dvn)
        dvhat = dvn * gain_v
        dv = rstd * (dvhat - _row_mean(dvhat) - vhat * _row_mean(dvhat * vhat))
        dzpre = (jnp.concatenate([du, dv], axis=1) * gelu_d).astype(BF16)
        dz_ref[...] = dzpre
        dh = _nn(dzpre, win_v[...])
        dxr, dg_row = _rms_bwd(dh, xhat, r, gv)
        dg_ref[...] += dg_row
        dx = dov + dxr
        dx_ref[...] = dx
        dxb_ref[...] = dx.astype(BF16)

        @pl.when(i == n_steps - 1)
        def _():
            rr = lax.broadcasted_iota(jnp.int32, (CHUNK, CHUNK), 0)
            cc = lax.broadcasted_iota(jnp.int32, (CHUNK, CHUNK), 1)
            for h in range(HEADS):
                dws_ref[h] = jnp.where(rr >= cc, dws_ref[h], 0.0)

    return _hosting_call(
        body, "mixer_a_bwd", n_steps, [dout, x, zpre, g, gain, bias, ws, bfull, gath],
        in_specs=[_row_spec(tm, d), _row_spec(tm, d), _row_spec(tm, e2), _const_spec((1, d)),
                  _const_spec((1, e)), _const_spec((1, e)), _const_spec((HEADS, CHUNK, CHUNK)),
                  _const_spec((CHUNK, e)), ANY],
        out_specs=[_row_spec(tm, d), _row_spec(tm, d), _row_spec(tm, d), _row_spec(tm, e), _row_spec(tm, e2),
                   _const_spec((1, d)), _const_spec((1, e)), _const_spec((1, e)),
                   _const_spec((HEADS, CHUNK, CHUNK)), _const_spec((CHUNK, e))],
        out_shape=[jax.ShapeDtypeStruct((t_tok, d), F32), jax.ShapeDtypeStruct((t_tok, d), BF16),
                   jax.ShapeDtypeStruct((t_tok, d), BF16), jax.ShapeDtypeStruct((t_tok, e), BF16),
                   jax.ShapeDtypeStruct((t_tok, e2), BF16),
                   jax.ShapeDtypeStruct((1, d), F32), jax.ShapeDtypeStruct((1, e), F32),
                   jax.ShapeDtypeStruct((1, e), F32), jax.ShapeDtypeStruct((HEADS, CHUNK, CHUNK), F32),
                   jax.ShapeDtypeStruct((CHUNK, e), F32)],
        scratch=[pltpu.VMEM((e2, d), BF16), pltpu.VMEM((e, d), BF16), pltpu.VMEM((tm, e), F32),
                 pltpu.VMEM((tm, e), F32), pltpu.SemaphoreType.DMA((2 * N_DEV,))],
        hosted=hosted)


def _ffn_fwd(x, g, gath, tm, name, hosted=None):
    t_tok, d = x.shape
    nf = gath.shape[1] // 3
    f = nf * N_DEV

    def body(x_ref, g_ref, gath_ref, xo_ref, gate_ref, up_ref, wg_v, wu_v, wd_v, sems):
        _load_group(gath_ref, [(0, nf, wg_v), (nf, nf, wu_v), (2 * nf, nf, wd_v)], sems)
        xv = x_ref[...]
        h = _rms_fwd(xv, g_ref[...])[0].astype(BF16)
        gate = _nt(h, wg_v[...])
        up = _nt(h, wu_v[...])
        gate_ref[...] = gate
        up_ref[...] = up
        act = (gate * _sigmoid(gate) * up).astype(BF16)
        xo_ref[...] = xv + _nn(act, wd_v[...])

    return _hosting_call(
        body, name, t_tok // tm, [x, g, gath],
        in_specs=[_row_spec(tm, d), _const_spec((1, d)), ANY],
        out_specs=[_row_spec(tm, d), _row_spec(tm, f), _row_spec(tm, f)],
        out_shape=[jax.ShapeDtypeStruct((t_tok, d), F32), jax.ShapeDtypeStruct((t_tok, f), F32),
                   jax.ShapeDtypeStruct((t_tok, f), F32)],
        scratch=[pltpu.VMEM((f, d), BF16), pltpu.VMEM((f, d), BF16), pltpu.VMEM((f, d), BF16),
                 pltpu.SemaphoreType.DMA((3 * N_DEV,))],
        hosted=hosted)


def _ffn_bwd(dout, x, gate, up, g, gath, tm, name, hosted=None):
    t_tok, d = x.shape
    nf = gath.shape[1] // 3
    f = nf * N_DEV

    def body(dout_ref, x_ref, gate_ref, up_ref, g_ref, gath_ref,
             dx_ref, dxb_ref, h_ref, act_ref, dgu_ref, dg_ref, wg_v, wu_v, wd_v, sems):
        _load_group(gath_ref, [(0, nf, wg_v), (nf, nf, wu_v), (2 * nf, nf, wd_v)], sems)

        @pl.when(pl.program_id(0) == 0)
        def _():
            dg_ref[...] = jnp.zeros_like(dg_ref)

        xv = x_ref[...]
        gv = g_ref[...]
        hv, xhat, r = _rms_fwd(xv, gv)
        h_ref[...] = hv.astype(BF16)
        gate_v = gate_ref[...]
        up_v = up_ref[...]
        sig = _sigmoid(gate_v)
        silu = gate_v * sig
        act_ref[...] = (silu * up_v).astype(BF16)
        dov = dout_ref[...]
        dact = _nt(dov.astype(BF16), wd_v[...])
        dup = (dact * silu).astype(BF16)
        dgate = (dact * up_v * (sig * (1.0 + gate_v * (1.0 - sig)))).astype(BF16)
        dgu_ref[:, :f] = dgate
        dgu_ref[:, f:] = dup
        dh = _nn(dgate, wg_v[...]) + _nn(dup, wu_v[...])
        dxr, dg_row = _rms_bwd(dh, xhat, r, gv)
        dg_ref[...] += dg_row
        dx = dov + dxr
        dx_ref[...] = dx
        dxb_ref[...] = dx.astype(BF16)

    return _hosting_call(
        body, name, t_tok // tm, [dout, x, gate, up, g, gath],
        in_specs=[_row_spec(tm, d), _row_spec(tm, d), _row_spec(tm, f), _row_spec(tm, f), _const_spec((1, d)), ANY],
        out_specs=[_row_spec(tm, d), _row_spec(tm, d), _row_spec(tm, d), _row_spec(tm, f), _row_spec(tm, 2 * f),
                   _const_spec((1, d))],
        out_shape=[jax.ShapeDtypeStruct((t_tok, d), F32), jax.ShapeDtypeStruct((t_tok, d), BF16),
                   jax.ShapeDtypeStruct((t_tok, d), BF16), jax.ShapeDtypeStruct((t_tok, f), BF16),
                   jax.ShapeDtypeStruct((t_tok, 2 * f), BF16), jax.ShapeDtypeStruct((1, d), F32)],
        scratch=[pltpu.VMEM((f, d), BF16), pltpu.VMEM((f, d), BF16), pltpu.VMEM((f, d), BF16),
                 pltpu.SemaphoreType.DMA((3 * N_DEV,))],
        hosted=hosted)


def _shift_down(z, k, prev_rows):
    row = lax.broadcasted_iota(jnp.int32, z.shape, 0)
    out = pltpu.roll(z, k, 0)
    for j in range(k):
        out = jnp.where(row == j, prev_rows[j], out)
    return out


def _shift_up(z, k, next_rows):
    tm = z.shape[0]
    row = lax.broadcasted_iota(jnp.int32, z.shape, 0)
    out = pltpu.roll(z, tm - k, 0)
    for j in range(k):
        out = jnp.where(row == tm - k + j, next_rows[j], out)
    return out


def _mixer_b_fwd(x, g, gath, conv_w, tm, seq):
    t_tok, d = x.shape
    e = conv_w.shape[1]
    e3 = 3 * e
    n_in, n_out = e3 // N_DEV, e // N_DEV
    tiles_per_seq = seq // tm

    def body(x_ref, g_ref, cw_ref, gath_ref, xo_ref, p_ref, win_v, wout_v, tail_v, sems):
        i = pl.program_id(0)
        _load_group(gath_ref, [(0, n_in, win_v), (n_in, n_out, wout_v)], sems)

        @pl.when(i % tiles_per_seq == 0)
        def _():
            tail_v[...] = jnp.zeros_like(tail_v)

        xv = x_ref[...]
        h = _rms_fwd(xv, g_ref[...])[0].astype(BF16)
        p = _nt(h, win_v[...])
        p_ref[...] = p
        z = p[:, e:2 * e] * p[:, 2 * e:]
        prev = [tail_v[SUBLANES - 2:SUBLANES - 1, :], tail_v[SUBLANES - 1:SUBLANES, :]]
        conv = (cw_ref[2:3, :] * z + cw_ref[1:2, :] * _shift_down(z, 1, prev[1:])
                + cw_ref[0:1, :] * _shift_down(z, 2, prev))
        tail_v[...] = z[tm - SUBLANES:, :]
        y = (p[:, :e] * conv).astype(BF16)
        xo_ref[...] = xv + _nn(y, wout_v[...])

    return _hosting_call(
        body, "mixer_b_fwd", t_tok // tm, [x, g, conv_w, gath],
        in_specs=[_row_spec(tm, d), _const_spec((1, d)), _const_spec((SUBLANES, e)), ANY],
        out_specs=[_row_spec(tm, d), _row_spec(tm, e3)],
        out_shape=[jax.ShapeDtypeStruct((t_tok, d), F32), jax.ShapeDtypeStruct((t_tok, e3), F32)],
        scratch=[pltpu.VMEM((e3, d), BF16), pltpu.VMEM((e, d), BF16), pltpu.VMEM((SUBLANES, e), F32),
                 pltpu.SemaphoreType.DMA((2 * N_DEV,))])[0]


def _mixer_b_bwd(dout, x, p, g, gath, conv_w, tm, seq, hosted=None):
    t_tok, d = x.shape
    e = conv_w.shape[1]
    e3 = 3 * e
    n_in, n_out = e3 // N_DEV, e // N_DEV
    tiles_per_seq = seq // tm
    halo_per_tile = tm // SUBLANES
    n_halo = t_tok // SUBLANES

    def body(dout_ref, dnext_ref, x_ref, p_ref, pprev_ref, pnext_ref, g_ref, cw_ref, gath_ref,
             dx_ref, dxb_ref, h_ref, y_ref, dp_ref, dg_ref, dcw_ref, win_v, wout_v, sems):
        i = pl.program_id(0)
        _load_group(gath_ref, [(0, n_in, win_v), (n_in, n_out, wout_v)], sems)

        @pl.when(i == 0)
        def _():
            dg_ref[...] = jnp.zeros_like(dg_ref)
            dcw_ref[...] = jnp.zeros_like(dcw_ref)

        first = (i % tiles_per_seq == 0).astype(F32)
        last = (i % tiles_per_seq == tiles_per_seq - 1).astype(F32)
        xv = x_ref[...]
        gv = g_ref[...]
        hv, xhat, r = _rms_fwd(xv, gv)
        h_ref[...] = hv.astype(BF16)
        pv = p_ref[...]
        bg, cg, hx = pv[:, :e], pv[:, e:2 * e], pv[:, 2 * e:]
        z = cg * hx
        zprev = pprev_ref[:, e:2 * e] * pprev_ref[:, 2 * e:] * (1.0 - first)
        prev = [zprev[SUBLANES - 2:SUBLANES - 1, :], zprev[SUBLANES - 1:SUBLANES, :]]
        zs1 = _shift_down(z, 1, prev[1:])
        zs2 = _shift_down(z, 2, prev)
        w0, w1, w2 = cw_ref[0:1, :], cw_ref[1:2, :], cw_ref[2:3, :]
        conv = w2 * z + w1 * zs1 + w0 * zs2
        y_ref[...] = (bg * conv).astype(BF16)

        dov = dout_ref[...]
        wout_bf = wout_v[...]
        dy = _nt(dov.astype(BF16), wout_bf)
        dconv = dy * bg
        dnext = _nt(dnext_ref[...].astype(BF16), wout_bf) * pnext_ref[:, :e] * (1.0 - last)
        nxt = [dnext[0:1, :], dnext[1:2, :]]
        dz = w2 * dconv + w1 * _shift_up(dconv, 1, nxt[:1]) + w0 * _shift_up(dconv, 2, nxt)
        dcw_ref[0:1, :] += _col_sum(dconv * zs2)
        dcw_ref[1:2, :] += _col_sum(dconv * zs1)
        dcw_ref[2:3, :] += _col_sum(dconv * z)
        dp = jnp.concatenate([dy * conv, dz * hx, dz * cg], axis=1).astype(BF16)
        dp_ref[...] = dp
        dh = _nn(dp, win_v[...])
        dxr, dg_row = _rms_bwd(dh, xhat, r, gv)
        dg_ref[...] += dg_row
        dx = dov + dxr
        dx_ref[...] = dx
        dxb_ref[...] = dx.astype(BF16)

    prev_spec = lambda w: pl.BlockSpec((SUBLANES, w), lambda i: (jnp.maximum(i * halo_per_tile - 1, 0), 0))
    next_spec = lambda w: pl.BlockSpec((SUBLANES, w), lambda i: (jnp.minimum((i + 1) * halo_per_tile, n_halo - 1), 0))
    return _hosting_call(
        body, "mixer_b_bwd", t_tok // tm, [dout, dout, x, p, p, p, g, conv_w, gath],
        in_specs=[_row_spec(tm, d), next_spec(d), _row_spec(tm, d), _row_spec(tm, e3), prev_spec(e3), next_spec(e3),
                  _const_spec((1, d)), _const_spec((SUBLANES, e)), ANY],
        out_specs=[_row_spec(tm, d), _row_spec(tm, d), _row_spec(tm, d), _row_spec(tm, e), _row_spec(tm, e3),
                   _const_spec((1, d)), _const_spec((SUBLANES, e))],
        out_shape=[jax.ShapeDtypeStruct((t_tok, d), F32), jax.ShapeDtypeStruct((t_tok, d), BF16),
                   jax.ShapeDtypeStruct((t_tok, d), BF16), jax.ShapeDtypeStruct((t_tok, e), BF16),
                   jax.ShapeDtypeStruct((t_tok, e3), BF16), jax.ShapeDtypeStruct((1, d), F32),
                   jax.ShapeDtypeStruct((SUBLANES, e), F32)],
        scratch=[pltpu.VMEM((e3, d), BF16), pltpu.VMEM((e, d), BF16), pltpu.SemaphoreType.DMA((2 * N_DEV,))],
        hosted=hosted)


def _loss_head(x, target, g, tm):
    t_tok, d = x.shape

    def body(x_ref, t_ref, g_ref, loss_ref, dx_ref, dxb_ref, dg_ref):
        @pl.when(pl.program_id(0) == 0)
        def _():
            loss_ref[...] = jnp.zeros_like(loss_ref)
            dg_ref[...] = jnp.zeros_like(dg_ref)

        gv = g_ref[...]
        y, xhat, r = _rms_fwd(x_ref[...], gv)
        err = y - t_ref[...]
        loss_ref[...] += 0.5 * jnp.sum(_row_mean(err * err), axis=0, keepdims=True)
        dxr, dg_row = _rms_bwd(err * (1.0 / d), xhat, r, gv)
        dg_ref[...] += dg_row
        dx_ref[...] = dxr
        dxb_ref[...] = dxr.astype(BF16)

    return pl.pallas_call(
        body, name="loss_head", grid=(t_tok // tm,),
        in_specs=[_row_spec(tm, d), _row_spec(tm, d), _const_spec((1, d))],
        out_specs=[_const_spec((1, 1)), _row_spec(tm, d), _row_spec(tm, d), _const_spec((1, d))],
        out_shape=[jax.ShapeDtypeStruct((1, 1), F32), jax.ShapeDtypeStruct((t_tok, d), F32),
                   jax.ShapeDtypeStruct((t_tok, d), BF16), jax.ShapeDtypeStruct((1, d), F32)],
        compiler_params=_params(),
    )(x, target, g)


def _wgrad(a, b, bm, name):
    t_tok, m = a.shape
    n = b.shape[1]

    def body(a_ref, b_ref, o_ref):
        o_ref[...] = _tn(a_ref[...], b_ref[...]).astype(o_ref.dtype)

    return pl.pallas_call(
        body, name=name, grid=(m // bm,),
        in_specs=[pl.BlockSpec((t_tok, bm), lambda i: (0, i)), _const_spec((t_tok, n))],
        out_specs=pl.BlockSpec((bm, n), lambda i: (i, 0)),
        out_shape=jax.ShapeDtypeStruct((m, n), BF16),
        compiler_params=_params(sequential=False),
    )(a, b)


def _sum_slots(land, rb, name):
    n_slots, rows, cols = land.shape

    def body(l_ref, o_ref):
        acc = l_ref[0].astype(F32)
        for k in range(1, n_slots):
            acc = acc + l_ref[k].astype(F32)
        o_ref[...] = acc

    return pl.pallas_call(
        body, name=name, grid=(rows // rb,),
        in_specs=[pl.BlockSpec((n_slots, rb, cols), lambda i: (0, i, 0))],
        out_specs=pl.BlockSpec((rb, cols), lambda i: (i, 0)),
        out_shape=jax.ShapeDtypeStruct((rows, cols), F32),
        compiler_params=_params(sequential=False),
    )(land)


def _adamw(w, grad, m, v, rb, name):
    rows, cols = w.shape
    c1 = 1.0 / (1.0 - ADAM_B1 ** ADAM_STEP)
    c2 = 1.0 / (1.0 - ADAM_B2 ** ADAM_STEP)

    def body(w_ref, g_ref, m_ref, v_ref, d_ref, mo_ref, vo_ref):
        gv = g_ref[...]
        mn = ADAM_B1 * m_ref[...] + (1.0 - ADAM_B1) * gv
        vn = ADAM_B2 * v_ref[...] + (1.0 - ADAM_B2) * (gv * gv)
        mo_ref[...] = mn
        vo_ref[...] = vn
        d_ref[...] = -ADAM_LR * ((mn * c1) / (jnp.sqrt(vn * c2) + ADAM_EPS) + ADAM_WD * w_ref[...])

    spec = pl.BlockSpec((rb, cols), lambda i: (i, 0))
    shape = jax.ShapeDtypeStruct((rows, cols), F32)
    return pl.pallas_call(
        body, name=name, grid=(rows // rb,),
        in_specs=[spec] * 4, out_specs=[spec] * 3, out_shape=[shape] * 3,
        compiler_params=_params(sequential=False),
    )(w, grad, m, v)


def _pack_small(parts, rows):
    flat = jnp.concatenate([p.reshape(-1).astype(F32) for p in parts])
    return jnp.pad(flat, (0, rows * LANES - flat.shape[0])).reshape(rows, LANES)


def _unpack_small(packed, shapes):
    flat = packed.reshape(-1)
    out = []
    pos = 0
    for s in shapes:
        n = math.prod(s)
        out.append(flat[pos:pos + n].reshape(s))
        pos += n
    return out


def kernel(x, mix_norm, ffn_norm, a_w_in, a_v_gain, a_v_bias, a_w_s, a_b_s, a_w_out, b_w_in, b_conv_w, b_w_out, ffn_w_gate, ffn_w_up, ffn_w_down, final_norm, loss_target, m_mix_norm, m_ffn_norm, m_a_w_in, m_a_v_gain, m_a_v_bias, m_a_w_s, m_a_b_s, m_a_w_out, m_b_w_in, m_b_conv_w, m_b_w_out, m_ffn_w_gate, m_ffn_w_up, m_ffn_w_down, m_final_norm, v_mix_norm, v_ffn_norm, v_a_w_in, v_a_v_gain, v_a_v_bias, v_a_w_s, v_a_b_s, v_a_w_out, v_b_w_in, v_b_conv_w, v_b_w_out, v_ffn_w_gate, v_ffn_w_up, v_ffn_w_down, v_final_norm):
    bsz, seq, d = x.shape
    t_tok = bsz * seq
    me = _my_index()
    xt = x.reshape(t_tok, d)
    target = loss_target.reshape(t_tok, d)
    e_a = a_v_gain.shape[1]
    e_b = b_w_out.shape[1] * N_DEV
    n_layers = ffn_w_gate.shape[0]
    f_shard = ffn_w_gate.shape[2]
    f_full = f_shard * N_DEV

    conv_pad = jnp.pad(b_conv_w[0], ((0, SUBLANES - CONV_W), (0, 0)))
    sh_a = jnp.concatenate([a_w_in[0].T, a_w_out[0]]).astype(BF16)
    sh_b = jnp.concatenate([b_w_in[0].T, b_w_out[0]]).astype(BF16)
    sh_f = [jnp.concatenate([ffn_w_gate[l].T, ffn_w_up[l].T, ffn_w_down[l]]).astype(BF16) for l in range(n_layers)]
    bfull = jnp.repeat(a_b_s[0].T, GROUP, axis=1)

    gath_a, conv_g = _all_gather([sh_a, conv_pad], "gather_a")
    conv_full = jnp.pad(conv_g[:, :CONV_W, :].transpose(1, 0, 2).reshape(CONV_W, e_b), ((0, SUBLANES - CONV_W), (0, 0)))
    (x1, zpre), (gath_f0, gath_b) = _mixer_a_fwd(xt, mix_norm[0:1], gath_a, a_v_gain, a_v_bias, a_w_s[0], bfull, tm=128,
                                                 hosted=_HostedGathers([sh_f[0], sh_b], mid_lead=3))
    (x2, gate0, up0), (gath_f1,) = _ffn_fwd(x1, ffn_norm[0:1], gath_f0, tm=256, name="ffn_fwd0",
                                            hosted=_HostedGathers([sh_f[1]], mid_lead=2))
    x3, p_b = _mixer_b_fwd(x2, mix_norm[1:2], gath_b, conv_full, tm=256, seq=seq)
    (x4, gate1, up1), _ = _ffn_fwd(x3, ffn_norm[1:2], gath_f1, tm=256, name="ffn_fwd1")
    loss_part, dx4, dx4_bf, d_final = _loss_head(x4, target, final_norm.reshape(1, d), tm=512)

    ffn_entries = [(0, 0, f_shard), (0, f_full, f_shard), (1, 0, f_shard)]
    (dx3, dx3_bf, h_f1, act1, dgu1, d_fn1), _ = _ffn_bwd(dx4, x3, gate1, up1, ffn_norm[1:2], gath_f1, tm=128,
                                                         name="ffn_bwd1")
    g_down1 = _wgrad(act1, dx4_bf, 256, "wgrad_down1")
    g_gu1 = _wgrad(dgu1, h_f1, 256, "wgrad_gate_up1")
    ps_f1 = _pair_reduce([g_gu1, g_down1], ffn_entries, "pair_reduce_f1")
    (dx2, dx2_bf, h_b, y_b, dp_b, d_mn1, d_conv), (land_f1,) = _mixer_b_bwd(
        dx3, x2, p_b, mix_norm[1:2], gath_b, conv_full, tm=128, seq=seq, hosted=_HostedChipScatter(ps_f1))
    g_b_out = _wgrad(y_b, dx3_bf, 256, "wgrad_b_out")
    g_b_in = _wgrad(dp_b, h_b, 256, "wgrad_b_in")
    ps_b = _pair_reduce([g_b_in, g_b_out], [(0, 0, b_w_in.shape[2]), (1, 0, b_w_out.shape[1])], "pair_reduce_b")
    (dx1, dx1_bf, h_f0, act0, dgu0, d_fn0), (land_b,) = _ffn_bwd(dx2, x1, gate0, up0, ffn_norm[0:1], gath_f0, tm=128,
                                                                 name="ffn_bwd0", hosted=_HostedChipScatter(ps_b))
    g_down0 = _wgrad(act0, dx2_bf, 256, "wgrad_down0")
    g_gu0 = _wgrad(dgu0, h_f0, 256, "wgrad_gate_up0")
    ps_f0 = _pair_reduce([g_gu0, g_down0], ffn_entries, "pair_reduce_f0")
    (dx0, _, h_a, y_a, dz_a, d_mn0, d_gain, d_bias, d_ws, d_bs_acc), (land_f0,) = _mixer_a_bwd(
        dx1, xt, zpre, mix_norm[0:1], gath_a, a_v_gain, a_v_bias, a_w_s[0], bfull, tm=128,
        hosted=_HostedChipScatter(ps_f0))
    g_a_out = _wgrad(y_a, dx1_bf, 256, "wgrad_a_out")
    g_a_in = _wgrad(dz_a, h_a, 256, "wgrad_a_in")
    ps_a = _pair_reduce([g_a_in, g_a_out], [(0, 0, a_w_in.shape[2]), (1, 0, a_w_out.shape[1])], "pair_reduce_a")
    land_a = _chip_scatter(ps_a, "chip_scatter_a")
    d_bs = d_bs_acc.reshape(CHUNK, HEADS, GROUP).sum(axis=2).T

    gs_a = _sum_slots(land_a, land_a.shape[1] // 2, "sum_a")
    gs_b = _sum_slots(land_b, land_b.shape[1] // 2, "sum_b")
    gs_f = [_sum_slots(land_f0, land_f0.shape[1] // 2, "sum_f0"), _sum_slots(land_f1, land_f1.shape[1] // 2, "sum_f1")]

    small_grads = [jnp.concatenate([d_mn0, d_mn1]), jnp.concatenate([d_fn0, d_fn1]), d_gain, d_bias, d_ws, d_bs,
                   d_final, d_conv[:CONV_W]]
    small_shapes = [(n_layers, d), (n_layers, d), (1, e_a), (1, e_a), (1, HEADS, CHUNK, CHUNK), (1, HEADS, CHUNK), (d,),
                    (CONV_W, e_b)]
    n_small = sum(math.prod(s) for s in small_shapes)
    blk_rows = -(-n_small // (N_DEV * LANES * SUBLANES)) * SUBLANES
    small_rows = blk_rows * N_DEV
    packed = _pack_small(small_grads, small_rows)
    small_land = _scatter_blocks([packed], [(0, 0, blk_rows)], blk_rows, LANES, F32, "scatter_small")
    small_sum = _sum_slots(small_land, blk_rows, "sum_small")
    small_all = _all_gather([small_sum], "gather_small")[0].reshape(small_rows, LANES)
    (gr_mix, gr_ffn, gr_gain, gr_bias, gr_ws, gr_bs, gr_final, gr_conv_full) = _unpack_small(small_all, small_shapes)
    gr_conv = lax.dynamic_slice_in_dim(gr_conv_full, me * (e_b // N_DEV), e_b // N_DEV, axis=1)[None]

    n_a_in, n_b_in = a_w_in.shape[2], b_w_in.shape[2]
    gr_a_in = gs_a[:n_a_in].T[None]
    gr_a_out = gs_a[n_a_in:][None]
    gr_b_in = gs_b[:n_b_in].T[None]
    gr_b_out = gs_b[n_b_in:][None]
    gr_gate = jnp.stack([gs_f[l][:f_shard].T for l in range(n_layers)])
    gr_up = jnp.stack([gs_f[l][f_shard:2 * f_shard].T for l in range(n_layers)])
    gr_down = jnp.stack([gs_f[l][2 * f_shard:] for l in range(n_layers)])

    def big(w, g, m, v, name):
        shape = w.shape
        two_d = (math.prod(shape[:-1]), shape[-1])
        rb = two_d[0] // 2
        out = _adamw(w.reshape(two_d), g.reshape(two_d), m.reshape(two_d), v.reshape(two_d), rb, name)
        return [o.reshape(shape) for o in out]

    small_w = [mix_norm, ffn_norm, a_v_gain, a_v_bias, a_w_s, a_b_s, final_norm]
    small_m = [m_mix_norm, m_ffn_norm, m_a_v_gain, m_a_v_bias, m_a_w_s, m_a_b_s, m_final_norm]
    small_v = [v_mix_norm, v_ffn_norm, v_a_v_gain, v_a_v_bias, v_a_w_s, v_a_b_s, v_final_norm]
    small_g = [gr_mix, gr_ffn, gr_gain, gr_bias, gr_ws, gr_bs, gr_final]
    sm_shapes = small_shapes[:-1]
    sm_out = _adamw(_pack_small(small_w, small_rows), _pack_small(small_g, small_rows), _pack_small(small_m, small_rows),
                    _pack_small(small_v, small_rows), small_rows, "adamw_small")
    sm_delta, sm_m, sm_v = [_unpack_small(o, sm_shapes) for o in sm_out]

    conv_out = _adamw(b_conv_w[0], gr_conv[0], m_b_conv_w[0], v_b_conv_w[0], CONV_W, "adamw_conv")
    conv_delta, conv_m, conv_v = [o[None] for o in conv_out]

    res = {
        "a_w_in": big(a_w_in, gr_a_in, m_a_w_in, v_a_w_in, "adamw_a_in"),
        "a_w_out": big(a_w_out, gr_a_out, m_a_w_out, v_a_w_out, "adamw_a_out"),
        "b_w_in": big(b_w_in, gr_b_in, m_b_w_in, v_b_w_in, "adamw_b_in"),
        "b_w_out": big(b_w_out, gr_b_out, m_b_w_out, v_b_w_out, "adamw_b_out"),
        "ffn_w_gate": big(ffn_w_gate, gr_gate, m_ffn_w_gate, v_ffn_w_gate, "adamw_gate"),
        "ffn_w_up": big(ffn_w_up, gr_up, m_ffn_w_up, v_ffn_w_up, "adamw_up"),
        "ffn_w_down": big(ffn_w_down, gr_down, m_ffn_w_down, v_ffn_w_down, "adamw_down"),
    }

    order = ["mix_norm", "ffn_norm", "a_w_in", "a_v_gain", "a_v_bias", "a_w_s", "a_b_s", "a_w_out", "b_w_in",
             "b_conv_w", "b_w_out", "ffn_w_gate", "ffn_w_up", "ffn_w_down", "final_norm"]
    small_names = ["mix_norm", "ffn_norm", "a_v_gain", "a_v_bias", "a_w_s", "a_b_s", "final_norm"]
    grads = {"a_w_in": gr_a_in, "a_w_out": gr_a_out, "b_w_in": gr_b_in, "b_w_out": gr_b_out, "ffn_w_gate": gr_gate,
             "ffn_w_up": gr_up, "ffn_w_down": gr_down, "b_conv_w": gr_conv}
    deltas, new_m, new_v = {}, {}, {}
    for k, name in enumerate(small_names):
        grads[name] = small_g[k]
        deltas[name], new_m[name], new_v[name] = sm_delta[k], sm_m[k], sm_v[k]
    deltas["b_conv_w"], new_m["b_conv_w"], new_v["b_conv_w"] = conv_delta, conv_m, conv_v
    for name, (dl, mm, vv) in res.items():
        deltas[name], new_m[name], new_v[name] = dl, mm, vv

    loss = lax.psum(loss_part[0, 0], ("x", "y", "c"))
    grad_x = dx0.reshape(bsz, seq, d)
    return (loss, grad_x, *[grads[n] for n in order], *[deltas[n] for n in order],
            *[new_m[n] for n in order], *[new_v[n] for n in order])
```

```python
import math

import jax
import jax.numpy as jnp
from jax import lax
from jax.experimental import pallas as pl
from jax.experimental.pallas import tpu as pltpu

F32 = jnp.float32
BF16 = jnp.bfloat16

N_DEV = 8
N_CHIP = 4
CHUNK = 128
HEADS = 16
GROUP = 128
CONV_W = 3
NORM_EPS = 1e-6
GELU_C = math.sqrt(2.0 / math.pi)
GELU_K = 0.044715

ADAM_LR = 0.001
ADAM_B1 = 0.9
ADAM_B2 = 0.999
ADAM_EPS = 1e-08
ADAM_WD = 0.01
ADAM_STEP = 10

LANES = 128
SUBLANES = 8
VMEM_LIMIT = 60 * 1024 * 1024
HALO = 16
MXU_WIDTH = 256
FFN_CHUNKS = 2

MESH = pl.DeviceIdType.MESH
ANY = pl.BlockSpec(memory_space=pl.ANY)


def _params(sequential=True):
    return pltpu.CompilerParams(
        dimension_semantics=("arbitrary",) if sequential else None,
        vmem_limit_bytes=VMEM_LIMIT)


def _nn(a, b):
    return jnp.dot(a, b, preferred_element_type=F32)


def _nt(a, b):
    return lax.dot_general(a, b, (((1,), (1,)), ((), ())), preferred_element_type=F32)


def _tn(a, b):
    return lax.dot_general(a, b, (((0,), (0,)), ((), ())), preferred_element_type=F32)


def _row_mean(a):
    return jnp.mean(a, axis=-1, keepdims=True)


def _col_sum(a):
    return jnp.sum(a, axis=0, keepdims=True)


def _rms_fwd(x, g):
    r = lax.rsqrt(_row_mean(x * x) + NORM_EPS)
    xhat = x * r
    return xhat * g, xhat, r


def _rms_bwd(dh, xhat, r, g):
    a = dh * g
    dx = r * (a - xhat * _row_mean(a * xhat))
    return dx, _col_sum(dh * xhat)


def _gelu_and_grad(x):
    x2 = x * x
    t = jnp.tanh(GELU_C * (x + GELU_K * x2 * x))
    half = 0.5 * (1.0 + t)
    d = half + 0.5 * x * (1.0 - t * t) * (GELU_C * (1.0 + 3.0 * GELU_K * x2))
    return x * half, d


def _sigmoid(x):
    return 1.0 / (1.0 + jnp.exp(-x))


def _row_spec(tm, width):
    return pl.BlockSpec((tm, width), lambda i: (i, 0))


def _const_spec(shape):
    nd = len(shape)
    return pl.BlockSpec(shape, lambda i: (0,) * nd)


def _load_group(gath_ref, parts, sems):
    @pl.when(pl.program_id(0) == 0)
    def _():
        copies = []
        for k, (first, n, dst) in enumerate(parts):
            for j in range(N_DEV):
                copies.append(pltpu.make_async_copy(gath_ref.at[j, pl.ds(first, n), :], dst.at[pl.ds(j * n, n), :],
                                                    sems.at[k * N_DEV + j]))
        for cp in copies:
            cp.start()
        for cp in copies:
            cp.wait()


def _hosting_call(body, name, n_steps, arrays, in_specs, out_specs, out_shape, scratch, hosted=None):
    n_in, n_out, n_scr = len(arrays), len(out_shape), len(scratch)
    h_arrays = list(hosted.arrays) if hosted else []
    h_shapes = list(hosted.out_shapes) if hosted else []
    h_sems = list(hosted.sem_shapes) if hosted else []

    def full_body(*refs):
        pos = 0
        groups = []
        for n in (n_in, len(h_arrays), n_out, len(h_shapes), n_scr, len(h_sems)):
            groups.append(refs[pos:pos + n])
            pos += n
        own_in, h_in, own_out, h_out, own_scr, h_sem = groups
        if hosted:
            hosted.begin(h_in, h_out, h_sem, n_steps)
        body(*own_in, *own_out, *own_scr)
        if hosted:
            hosted.end(h_in, h_out, h_sem, n_steps)

    outs = pl.pallas_call(
        full_body, name=name, grid=(n_steps,),
        in_specs=list(in_specs) + [ANY] * len(h_arrays),
        out_specs=list(out_specs) + [ANY] * len(h_shapes),
        out_shape=list(out_shape) + h_shapes,
        scratch_shapes=list(scratch) + h_sems,
        compiler_params=_params(),
    )(*arrays, *h_arrays)
    return outs[:n_out], outs[n_out:]


def _my_index():
    return 4 * lax.axis_index("x") + 2 * lax.axis_index("y") + lax.axis_index("c")


class _Gather:
    def __init__(self, shard, out, send_sems, recv_sems, local_sem):
        self.shard, self.out = shard, out
        self.send_sems, self.recv_sems, self.local_sem = send_sems, recv_sems, local_sem
        x, y, c = lax.axis_index("x"), lax.axis_index("y"), lax.axis_index("c")
        self.c = c
        self.me, self.sibling = (x, y, c), (x, y, 1 - c)
        self.chips = [(1 - x, y), (x, 1 - y), (1 - x, 1 - y)]

    def _slot(self, dev):
        return self.out.at[4 * dev[0] + 2 * dev[1] + dev[2]]

    def _copy(self, k, block, to, src=None):
        return pltpu.make_async_remote_copy(
            src_ref=self._slot(block) if src is None else src, dst_ref=self._slot(block),
            send_sem=self.send_sems.at[k], recv_sem=self.recv_sems.at[k], device_id=to, device_id_type=MESH)

    def _local(self):
        return pltpu.make_async_copy(self.shard, self._slot(self.me), self.local_sem)

    def start(self):
        self._local().start()
        self._copy(0, self.me, self.sibling, src=self.shard).start()
        for j, chip in enumerate(self.chips):
            self._copy(1 + j, self.me, (*chip, self.c), src=self.shard).start()

    def forward(self):
        for j, chip in enumerate(self.chips):
            self._copy(1 + j, (*chip, self.c), self.me).wait_recv()
            self._copy(4 + j, (*chip, self.c), self.sibling).start()

    def finish(self):
        self._copy(0, self.sibling, self.me).wait_recv()
        for j, chip in enumerate(self.chips):
            self._copy(4 + j, (*chip, 1 - self.c), self.me).wait_recv()
        for k in range(N_DEV - 1):
            self._copy(k, self.me, self.sibling).wait_send()
        self._local().wait()


class _HostedGathers:
    def __init__(self, shards, mid_lead):
        n = len(shards)
        self.arrays = shards
        self.mid_lead = mid_lead
        self.out_shapes = [jax.ShapeDtypeStruct((N_DEV,) + s.shape, s.dtype) for s in shards]
        self.sem_shapes = [pltpu.SemaphoreType.DMA((n, N_DEV - 1)), pltpu.SemaphoreType.DMA((n, N_DEV - 1)),
                           pltpu.SemaphoreType.DMA((n,))]

    def _gathers(self, ins, outs, sems):
        return [_Gather(ins[a], outs[a], sems[0].at[a], sems[1].at[a], sems[2].at[a]) for a in range(len(ins))]

    def begin(self, ins, outs, sems, n_steps):
        i = pl.program_id(0)

        @pl.when(i == 0)
        def _():
            for g in self._gathers(ins, outs, sems):
                g.start()

        @pl.when(i == max(n_steps - 1 - self.mid_lead, 0))
        def _():
            for g in self._gathers(ins, outs, sems):
                g.forward()

    def end(self, ins, outs, sems, n_steps):
        @pl.when(pl.program_id(0) == n_steps - 1)
        def _():
            for g in self._gathers(ins, outs, sems):
                g.finish()


def _all_gather(shards, name):
    n_arr = len(shards)
    host = _HostedGathers(shards, 0)

    def body(*refs):
        gathers = host._gathers(refs[:n_arr], refs[n_arr:2 * n_arr], refs[2 * n_arr:])
        for g in gathers:
            g.start()
        for g in gathers:
            g.forward()
        for g in gathers:
            g.finish()

    return pl.pallas_call(
        body, name=name, in_specs=[ANY] * n_arr, out_specs=[ANY] * n_arr,
        out_shape=host.out_shapes, scratch_shapes=host.sem_shapes,
    )(*shards)


class _ChipScatter:
    def __init__(self, pairsum, land, send_sems, recv_sems, local_sem):
        self.pairsum, self.land = pairsum, land
        self.send_sems, self.recv_sems, self.local_sem = send_sems, recv_sems, local_sem
        x, y, c = lax.axis_index("x"), lax.axis_index("y"), lax.axis_index("c")
        self.c = c
        self.chip = 2 * x + y
        self.others = [(1 - x, y), (x, 1 - y), (1 - x, 1 - y)]

    def _copy(self, k):
        ox, oy = self.others[k]
        return pltpu.make_async_remote_copy(
            src_ref=self.pairsum.at[2 * ox + oy], dst_ref=self.land.at[self.chip],
            send_sem=self.send_sems.at[k], recv_sem=self.recv_sems.at[k], device_id=(ox, oy, self.c),
            device_id_type=MESH)

    def _arrival(self, k):
        ox, oy = self.others[k]
        return pltpu.make_async_remote_copy(
            src_ref=self.pairsum.at[self.chip], dst_ref=self.land.at[2 * ox + oy],
            send_sem=self.send_sems.at[k], recv_sem=self.recv_sems.at[k], device_id=(ox, oy, self.c),
            device_id_type=MESH)

    def _local(self):
        return pltpu.make_async_copy(self.pairsum.at[self.chip], self.land.at[self.chip], self.local_sem)

    def start(self):
        self._local().start()
        for k in range(N_CHIP - 1):
            self._copy(k).start()

    def finish(self):
        for k in range(N_CHIP - 1):
            self._arrival(k).wait_recv()
        for k in range(N_CHIP - 1):
            self._copy(k).wait_send()
        self._local().wait()


class _HostedChipScatter:
    def __init__(self, pairsum):
        self.arrays = [pairsum]
        self.out_shapes = [jax.ShapeDtypeStruct(pairsum.shape, pairsum.dtype)]
        self.sem_shapes = [pltpu.SemaphoreType.DMA((N_CHIP - 1,)), pltpu.SemaphoreType.DMA((N_CHIP - 1,)),
                           pltpu.SemaphoreType.DMA(())]

    def begin(self, ins, outs, sems, n_steps):
        @pl.when(pl.program_id(0) == 0)
        def _():
            _ChipScatter(ins[0], outs[0], *sems).start()

    def end(self, ins, outs, sems, n_steps):
        @pl.when(pl.program_id(0) == n_steps - 1)
        def _():
            _ChipScatter(ins[0], outs[0], *sems).finish()


def _chip_scatter(pairsum, name):
    host = _HostedChipScatter(pairsum)

    def body(ps_ref, land_ref, *sems):
        cs = _ChipScatter(ps_ref, land_ref, *sems)
        cs.start()
        cs.finish()

    return pl.pallas_call(
        body, name=name, in_specs=[ANY], out_specs=ANY, out_shape=host.out_shapes[0], scratch_shapes=host.sem_shapes,
    )(pairsum)


def _pair_reduce(arrays, entries, name):
    n_arr, n_ent = len(arrays), len(entries)
    cols = arrays[0].shape[1]
    offsets = []
    total = 0
    for _, _, n in entries:
        offsets.append(total)
        total += n

    def body(*refs):
        ins, out_ref = refs[:n_arr], refs[n_arr]
        rbuf, own, send_sems, recv_sems, own_sems = refs[n_arr + 1:]
        q = pl.program_id(0)
        x, y, c = lax.axis_index("x"), lax.axis_index("y"), lax.axis_index("c")

        def block(e, chip, core):
            ai, first, n = entries[e]
            return ins[ai].at[pl.ds(first + (2 * chip + core) * n, n), :]

        def to_sibling(e, chip):
            return pltpu.make_async_remote_copy(
                src_ref=block(e, chip, 1 - c), dst_ref=rbuf.at[chip, pl.ds(offsets[e], entries[e][2]), :],
                send_sem=send_sems.at[e, chip], recv_sem=recv_sems.at[e, chip], device_id=(x, y, 1 - c),
                device_id_type=MESH)

        @pl.when(q == 0)
        def _():
            for chip in range(N_CHIP):
                for e in range(n_ent):
                    to_sibling(e, chip).start()

        loads = [pltpu.make_async_copy(block(e, q, c), own.at[pl.ds(offsets[e], entries[e][2]), :], own_sems.at[e])
                 for e in range(n_ent)]
        for cp in loads:
            cp.start()
        for cp in loads:
            cp.wait()
        for e in range(n_ent):
            to_sibling(e, q).wait_recv()
        out_ref[...] = (own[...].astype(F32) + rbuf[q].astype(F32)).astype(out_ref.dtype)

        @pl.when(q == N_CHIP - 1)
        def _():
            for chip in range(N_CHIP):
                for e in range(n_ent):
                    to_sibling(e, chip).wait_send()

    return pl.pallas_call(
        body, name=name, grid=(N_CHIP,),
        in_specs=[ANY] * n_arr,
        out_specs=pl.BlockSpec((None, total, cols), lambda q: (q, 0, 0)),
        out_shape=jax.ShapeDtypeStruct((N_CHIP, total, cols), BF16),
        scratch_shapes=[pltpu.VMEM((N_CHIP, total, cols), BF16), pltpu.VMEM((total, cols), BF16),
                        pltpu.SemaphoreType.DMA((n_ent, N_CHIP)), pltpu.SemaphoreType.DMA((n_ent, N_CHIP)),
                        pltpu.SemaphoreType.DMA((n_ent,))],
        compiler_params=_params(),
    )(*arrays)


def _scatter_blocks(arrays, entries, land_rows, cols, dtype, name):
    n_arr = len(arrays)
    n_ent = len(entries)
    offsets = []
    total = 0
    for _, _, n in entries:
        offsets.append(total)
        total += n
    assert total == land_rows

    def body(*refs):
        ins, land = refs[:n_arr], refs[n_arr]
        send_sems, recv_sems, local_sems = refs[n_arr + 1:]
        me = _my_index()

        def block(e, p):
            ai, first, n = entries[e]
            return ins[ai].at[pl.ds(first + p * n, n), :]

        def dest(e, q):
            return land.at[q, pl.ds(offsets[e], entries[e][2]), :]

        local = [pltpu.make_async_copy(block(e, me), dest(e, me), local_sems.at[e]) for e in range(n_ent)]
        for cp in local:
            cp.start()
        sends = []
        for k in range(1, N_DEV):
            p = (me + k) % N_DEV
            to = (p // 4, (p // 2) % 2, p % 2)
            for e in range(n_ent):
                cp = pltpu.make_async_remote_copy(
                    src_ref=block(e, p), dst_ref=dest(e, me), send_sem=send_sems.at[e, k - 1],
                    recv_sem=recv_sems.at[e, k - 1], device_id=to, device_id_type=MESH)
                cp.start()
                sends.append(cp)
        for k in range(1, N_DEV):
            q = (me + N_DEV - k) % N_DEV
            for e in range(n_ent):
                pltpu.make_async_remote_copy(
                    src_ref=block(e, me), dst_ref=dest(e, q), send_sem=send_sems.at[e, k - 1],
                    recv_sem=recv_sems.at[e, k - 1], device_id=(0, 0, 0), device_id_type=MESH).wait_recv()
        for cp in sends:
            cp.wait_send()
        for cp in local:
            cp.wait()

    return pl.pallas_call(
        body, name=name,
        in_specs=[ANY] * n_arr, out_specs=ANY,
        out_shape=jax.ShapeDtypeStruct((N_DEV, land_rows, cols), dtype),
        scratch_shapes=[pltpu.SemaphoreType.DMA((n_ent, N_DEV - 1)), pltpu.SemaphoreType.DMA((n_ent, N_DEV - 1)),
                        pltpu.SemaphoreType.DMA((n_ent,))],
    )(*arrays)


def _tril_weights(ws_ref):
    r = lax.broadcasted_iota(jnp.int32, (CHUNK, CHUNK), 0)
    c = lax.broadcasted_iota(jnp.int32, (CHUNK, CHUNK), 1)
    return [jnp.where(r >= c, ws_ref[h], 0.0).astype(BF16) for h in range(HEADS)]


def _sgu_stats(zpre, gain, bias):
    e = zpre.shape[1] // 2
    z, dz = _gelu_and_grad(zpre)
    u, v = z[:, :e], z[:, e:]
    vc = v - _row_mean(v)
    rstd = lax.rsqrt(_row_mean(vc * vc) + NORM_EPS)
    vhat = vc * rstd
    return u, vhat, rstd, vhat * gain + bias, dz


def _spatial_fwd(wt, vn_bf, bfull_ref, sv_ref, tm):
    for ci in range(tm // CHUNK):
        rows = slice(ci * CHUNK, (ci + 1) * CHUNK)
        for h in range(HEADS):
            cols = slice(h * GROUP, (h + 1) * GROUP)
            sv_ref[rows, cols] = _nn(wt[h], vn_bf[rows, cols]) + bfull_ref[:, cols]


def _mixer_a_fwd(x, g, gath, gain, bias, ws, bfull, tm, hosted=None):
    t_tok, d = x.shape
    e = gain.shape[1]
    e2 = 2 * e
    n_in, n_out = e2 // N_DEV, e // N_DEV

    def body(x_ref, g_ref, gain_ref, bias_ref, ws_ref, bfull_ref, gath_ref,
             xo_ref, zpre_ref, win_v, wout_v, sv_v, sems):
        _load_group(gath_ref, [(0, n_in, win_v), (n_in, n_out, wout_v)], sems)
        xv = x_ref[...]
        h = _rms_fwd(xv, g_ref[...])[0].astype(BF16)
        zpre = _nt(h, win_v[...])
        zpre_ref[...] = zpre.astype(BF16)
        u, _, _, vn, _ = _sgu_stats(zpre, gain_ref[...], bias_ref[...])
        _spatial_fwd(_tril_weights(ws_ref), vn.astype(BF16), bfull_ref, sv_v, tm)
        y = (u * sv_v[...]).astype(BF16)
        xo_ref[...] = xv + _nn(y, wout_v[...])

    return _hosting_call(
        body, "mixer_a_fwd", t_tok // tm, [x, g, gain, bias, ws, bfull, gath],
        in_specs=[_row_spec(tm, d), _const_spec((1, d)), _const_spec((1, e)), _const_spec((1, e)),
                  _const_spec((HEADS, CHUNK, CHUNK)), _const_spec((CHUNK, e)), ANY],
        out_specs=[_row_spec(tm, d), _row_spec(tm, e2)],
        out_shape=[jax.ShapeDtypeStruct((t_tok, d), F32), jax.ShapeDtypeStruct((t_tok, e2), BF16)],
        scratch=[pltpu.VMEM((e2, d), BF16), pltpu.VMEM((e, d), BF16), pltpu.VMEM((tm, e), F32),
                 pltpu.SemaphoreType.DMA((2 * N_DEV,))],
        hosted=hosted)


def _mixer_a_bwd(dout, x, zpre, g, gath, gain, bias, ws, bfull, tm, hosted=None):
    t_tok, d = x.shape
    e = gain.shape[1]
    e2 = 2 * e
    n_in, n_out = e2 // N_DEV, e // N_DEV
    n_steps = t_tok // tm

    def body(dout_ref, x_ref, zpre_ref, g_ref, gain_ref, bias_ref, ws_ref, bfull_ref, gath_ref,
             dx_ref, dxb_ref, h_ref, y_ref, dz_ref, dg_ref, dgain_ref, dbias_ref, dws_ref, dbs_ref,
             win_v, wout_v, sv_v, dvn_v, sems):
        i = pl.program_id(0)
        _load_group(gath_ref, [(0, n_in, win_v), (n_in, n_out, wout_v)], sems)

        @pl.when(i == 0)
        def _():
            dg_ref[...] = jnp.zeros_like(dg_ref)
            dgain_ref[...] = jnp.zeros_like(dgain_ref)
            dbias_ref[...] = jnp.zeros_like(dbias_ref)
            dws_ref[...] = jnp.zeros_like(dws_ref)
            dbs_ref[...] = jnp.zeros_like(dbs_ref)

        xv = x_ref[...]
        gv = g_ref[...]
        hv, xhat, r = _rms_fwd(xv, gv)
        h_ref[...] = hv.astype(BF16)
        gain_v = gain_ref[...]
        u, vhat, rstd, vn, gelu_d = _sgu_stats(zpre_ref[...].astype(F32), gain_v, bias_ref[...])
        vn_bf = vn.astype(BF16)
        wt = _tril_weights(ws_ref)
        _spatial_fwd(wt, vn_bf, bfull_ref, sv_v, tm)
        sv = sv_v[...]
        y_ref[...] = (u * sv).astype(BF16)

        dov = dout_ref[...]
        dy = _nt(dov.astype(BF16), wout_v[...])
        du = dy * sv
        dsv = dy * u
        dsv_bf = dsv.astype(BF16)
        for ci in range(tm // CHUNK):
            rows = slice(ci * CHUNK, (ci + 1) * CHUNK)
            dbs_ref[...] += dsv[rows, :]
            for h in range(HEADS):
                cols = slice(h * GROUP, (h + 1) * GROUP)
                dvn_v[rows, cols] = _tn(wt[h], dsv_bf[rows, cols])
                dws_ref[h] += _nt(dsv_bf[rows, cols], vn_bf[rows, cols])
        dvn = dvn_v[...]
        dgain_ref[...] += _col_sum(dvn * vhat)
        dbias_ref[...] += _col_sum(dvn)
        dvhat = dvn * gain_v
        dv = rstd * (dvhat - _row_mean(dvhat) - vhat * _row_mean(dvhat * vhat))
        dzpre = (jnp.concatenate([du, dv], axis=1) * gelu_d).astype(BF16)
        dz_ref[...] = dzpre
        dh = _nn(dzpre, win_v[...])
        dxr, dg_row = _rms_bwd(dh, xhat, r, gv)
        dg_ref[...] += dg_row
        dx = dov + dxr
        dx_ref[...] = dx
        dxb_ref[...] = dx.astype(BF16)

        @pl.when(i == n_steps - 1)
        def _():
            rr = lax.broadcasted_iota(jnp.int32, (CHUNK, CHUNK), 0)
            cc = lax.broadcasted_iota(jnp.int32, (CHUNK, CHUNK), 1)
            for h in range(HEADS):
                dws_ref[h] = jnp.where(rr >= cc, dws_ref[h], 0.0)

    return _hosting_call(
        body, "mixer_a_bwd", n_steps, [dout, x, zpre, g, gain, bias, ws, bfull, gath],
        in_specs=[_row_spec(tm, d), _row_spec(tm, d), _row_spec(tm, e2), _const_spec((1, d)),
                  _const_spec((1, e)), _const_spec((1, e)), _const_spec((HEADS, CHUNK, CHUNK)),
                  _const_spec((CHUNK, e)), ANY],
        out_specs=[_row_spec(tm, d), _row_spec(tm, d), _row_spec(tm, d), _row_spec(tm, e), _row_spec(tm, e2),
                   _const_spec((1, d)), _const_spec((1, e)), _const_spec((1, e)),
                   _const_spec((HEADS, CHUNK, CHUNK)), _const_spec((CHUNK, e))],
        out_shape=[jax.ShapeDtypeStruct((t_tok, d), F32), jax.ShapeDtypeStruct((t_tok, d), BF16),
                   jax.ShapeDtypeStruct((t_tok, d), BF16), jax.ShapeDtypeStruct((t_tok, e), BF16),
                   jax.ShapeDtypeStruct((t_tok, e2), BF16),
                   jax.ShapeDtypeStruct((1, d), F32), jax.ShapeDtypeStruct((1, e), F32),
                   jax.ShapeDtypeStruct((1, e), F32), jax.ShapeDtypeStruct((HEADS, CHUNK, CHUNK), F32),
                   jax.ShapeDtypeStruct((CHUNK, e), F32)],
        scratch=[pltpu.VMEM((e2, d), BF16), pltpu.VMEM((e, d), BF16), pltpu.VMEM((tm, e), F32),
                 pltpu.VMEM((tm, e), F32), pltpu.SemaphoreType.DMA((2 * N_DEV,))],
        hosted=hosted)


def _ffn_fwd(x, g, gath, tm, name, hosted=None):
    t_tok, d = x.shape
    nf = gath.shape[1] // 3
    f = nf * N_DEV

    def body(x_ref, g_ref, gath_ref, xo_ref, gate_ref, up_ref, wg_v, wu_v, wd_v, sems):
        _load_group(gath_ref, [(0, nf, wg_v), (nf, nf, wu_v), (2 * nf, nf, wd_v)], sems)
        xv = x_ref[...]
        h = _rms_fwd(xv, g_ref[...])[0].astype(BF16)
        gate = _nt(h, wg_v[...])
        up = _nt(h, wu_v[...])
        gate_ref[...] = gate.astype(BF16)
        up_ref[...] = up.astype(BF16)
        act = (gate * _sigmoid(gate) * up).astype(BF16)
        xo_ref[...] = xv + _nn(act, wd_v[...])

    return _hosting_call(
        body, name, t_tok // tm, [x, g, gath],
        in_specs=[_row_spec(tm, d), _const_spec((1, d)), ANY],
        out_specs=[_row_spec(tm, d), _row_spec(tm, f), _row_spec(tm, f)],
        out_shape=[jax.ShapeDtypeStruct((t_tok, d), F32), jax.ShapeDtypeStruct((t_tok, f), BF16),
                   jax.ShapeDtypeStruct((t_tok, f), BF16)],
        scratch=[pltpu.VMEM((f, d), BF16), pltpu.VMEM((f, d), BF16), pltpu.VMEM((f, d), BF16),
                 pltpu.SemaphoreType.DMA((3 * N_DEV,))],
        hosted=hosted)


def _ffn_bwd(dout, x, gate, up, g, gath, tm, name, hosted=None):
    t_tok, d = x.shape
    nf = gath.shape[1] // 3
    f = nf * N_DEV
    per_chunk = -(-f // (FFN_CHUNKS * MXU_WIDTH)) * MXU_WIDTH
    bounds = [min(ck * per_chunk, f) for ck in range(FFN_CHUNKS + 1)]

    def body(dout_ref, x_ref, gate_ref, up_ref, g_ref, gath_ref,
             dx_ref, dxb_ref, h_ref, act_ref, dgu_ref, dg_ref, wg_v, wu_v, wd_v, sems):
        _load_group(gath_ref, [(0, nf, wg_v), (nf, nf, wu_v), (2 * nf, nf, wd_v)], sems)

        @pl.when(pl.program_id(0) == 0)
        def _():
            dg_ref[...] = jnp.zeros_like(dg_ref)

        xv = x_ref[...]
        gv = g_ref[...]
        hv, xhat, r = _rms_fwd(xv, gv)
        h_ref[...] = hv.astype(BF16)
        dov = dout_ref[...]
        dob = dov.astype(BF16)
        dh = None
        for ck in range(FFN_CHUNKS):
            cols = slice(bounds[ck], bounds[ck + 1])
            gate_v = gate_ref[:, cols].astype(F32)
            up_v = up_ref[:, cols].astype(F32)
            sig = _sigmoid(gate_v)
            silu = gate_v * sig
            act_ref[:, cols] = (silu * up_v).astype(BF16)
            dact = _nt(dob, wd_v[cols, :])
            dup = (dact * silu).astype(BF16)
            dgate = (dact * up_v * (sig * (1.0 + gate_v * (1.0 - sig)))).astype(BF16)
            dgu_ref[:, cols] = dgate
            dgu_ref[:, f + bounds[ck]:f + bounds[ck + 1]] = dup
            part = _nn(dgate, wg_v[cols, :]) + _nn(dup, wu_v[cols, :])
            dh = part if dh is None else dh + part
        dxr, dg_row = _rms_bwd(dh, xhat, r, gv)
        dg_ref[...] += dg_row
        dx = dov + dxr
        dx_ref[...] = dx
        dxb_ref[...] = dx.astype(BF16)

    return _hosting_call(
        body, name, t_tok // tm, [dout, x, gate, up, g, gath],
        in_specs=[_row_spec(tm, d), _row_spec(tm, d), _row_spec(tm, f), _row_spec(tm, f), _const_spec((1, d)), ANY],
        out_specs=[_row_spec(tm, d), _row_spec(tm, d), _row_spec(tm, d), _row_spec(tm, f), _row_spec(tm, 2 * f),
                   _const_spec((1, d))],
        out_shape=[jax.ShapeDtypeStruct((t_tok, d), F32), jax.ShapeDtypeStruct((t_tok, d), BF16),
                   jax.ShapeDtypeStruct((t_tok, d), BF16), jax.ShapeDtypeStruct((t_tok, f), BF16),
                   jax.ShapeDtypeStruct((t_tok, 2 * f), BF16), jax.ShapeDtypeStruct((1, d), F32)],
        scratch=[pltpu.VMEM((f, d), BF16), pltpu.VMEM((f, d), BF16), pltpu.VMEM((f, d), BF16),
                 pltpu.SemaphoreType.DMA((3 * N_DEV,))],
        hosted=hosted)


def _shift_down(z, k, prev_rows):
    row = lax.broadcasted_iota(jnp.int32, z.shape, 0)
    out = pltpu.roll(z, k, 0)
    for j in range(k):
        out = jnp.where(row == j, prev_rows[j], out)
    return out


def _shift_up(z, k, next_rows):
    tm = z.shape[0]
    row = lax.broadcasted_iota(jnp.int32, z.shape, 0)
    out = pltpu.roll(z, tm - k, 0)
    for j in range(k):
        out = jnp.where(row == tm - k + j, next_rows[j], out)
    return out


def _mixer_b_fwd(x, g, gath, conv_w, tm, seq):
    t_tok, d = x.shape
    e = conv_w.shape[1]
    e3 = 3 * e
    n_in, n_out = e3 // N_DEV, e // N_DEV
    tiles_per_seq = seq // tm

    def body(x_ref, g_ref, cw_ref, gath_ref, xo_ref, p_ref, win_v, wout_v, tail_v, sems):
        i = pl.program_id(0)
        _load_group(gath_ref, [(0, n_in, win_v), (n_in, n_out, wout_v)], sems)

        @pl.when(i % tiles_per_seq == 0)
        def _():
            tail_v[...] = jnp.zeros_like(tail_v)

        xv = x_ref[...]
        h = _rms_fwd(xv, g_ref[...])[0].astype(BF16)
        p = _nt(h, win_v[...])
        p_ref[...] = p.astype(BF16)
        z = p[:, e:2 * e] * p[:, 2 * e:]
        prev = [tail_v[SUBLANES - 2:SUBLANES - 1, :], tail_v[SUBLANES - 1:SUBLANES, :]]
        conv = (cw_ref[2:3, :] * z + cw_ref[1:2, :] * _shift_down(z, 1, prev[1:])
                + cw_ref[0:1, :] * _shift_down(z, 2, prev))
        tail_v[...] = z[tm - SUBLANES:, :]
        y = (p[:, :e] * conv).astype(BF16)
        xo_ref[...] = xv + _nn(y, wout_v[...])

    return _hosting_call(
        body, "mixer_b_fwd", t_tok // tm, [x, g, conv_w, gath],
        in_specs=[_row_spec(tm, d), _const_spec((1, d)), _const_spec((SUBLANES, e)), ANY],
        out_specs=[_row_spec(tm, d), _row_spec(tm, e3)],
        out_shape=[jax.ShapeDtypeStruct((t_tok, d), F32), jax.ShapeDtypeStruct((t_tok, e3), BF16)],
        scratch=[pltpu.VMEM((e3, d), BF16), pltpu.VMEM((e, d), BF16), pltpu.VMEM((SUBLANES, e), F32),
                 pltpu.SemaphoreType.DMA((2 * N_DEV,))])[0]


def _mixer_b_bwd(dout, x, p, g, gath, conv_w, tm, seq, hosted=None):
    t_tok, d = x.shape
    e = conv_w.shape[1]
    e3 = 3 * e
    n_in, n_out = e3 // N_DEV, e // N_DEV
    tiles_per_seq = seq // tm
    halo_per_tile = tm // HALO
    n_halo = t_tok // HALO

    def body(dout_ref, dnext_ref, x_ref, p_ref, pprev_ref, pnext_ref, g_ref, cw_ref, gath_ref,
             dx_ref, dxb_ref, h_ref, y_ref, dp_ref, dg_ref, dcw_ref, win_v, wout_v, sems):
        i = pl.program_id(0)
        _load_group(gath_ref, [(0, n_in, win_v), (n_in, n_out, wout_v)], sems)

        @pl.when(i == 0)
        def _():
            dg_ref[...] = jnp.zeros_like(dg_ref)
            dcw_ref[...] = jnp.zeros_like(dcw_ref)

        first = (i % tiles_per_seq == 0).astype(F32)
        last = (i % tiles_per_seq == tiles_per_seq - 1).astype(F32)
        xv = x_ref[...]
        gv = g_ref[...]
        hv, xhat, r = _rms_fwd(xv, gv)
        h_ref[...] = hv.astype(BF16)
        pv = p_ref[...].astype(F32)
        bg, cg, hx = pv[:, :e], pv[:, e:2 * e], pv[:, 2 * e:]
        z = cg * hx
        pprev = pprev_ref[...].astype(F32)
        zprev = pprev[:, e:2 * e] * pprev[:, 2 * e:] * (1.0 - first)
        prev = [zprev[HALO - 2:HALO - 1, :], zprev[HALO - 1:HALO, :]]
        zs1 = _shift_down(z, 1, prev[1:])
        zs2 = _shift_down(z, 2, prev)
        w0, w1, w2 = cw_ref[0:1, :], cw_ref[1:2, :], cw_ref[2:3, :]
        conv = w2 * z + w1 * zs1 + w0 * zs2
        y_ref[...] = (bg * conv).astype(BF16)

        dov = dout_ref[...]
        wout_bf = wout_v[...]
        dy = _nt(dov.astype(BF16), wout_bf)
        dconv = dy * bg
        dnext = _nt(dnext_ref[...].astype(BF16), wout_bf) * pnext_ref[:, :e].astype(F32) * (1.0 - last)
        nxt = [dnext[0:1, :], dnext[1:2, :]]
        dz = w2 * dconv + w1 * _shift_up(dconv, 1, nxt[:1]) + w0 * _shift_up(dconv, 2, nxt)
        dcw_ref[0:1, :] += _col_sum(dconv * zs2)
        dcw_ref[1:2, :] += _col_sum(dconv * zs1)
        dcw_ref[2:3, :] += _col_sum(dconv * z)
        dp = jnp.concatenate([dy * conv, dz * hx, dz * cg], axis=1).astype(BF16)
        dp_ref[...] = dp
        dh = _nn(dp, win_v[...])
        dxr, dg_row = _rms_bwd(dh, xhat, r, gv)
        dg_ref[...] += dg_row
        dx = dov + dxr
        dx_ref[...] = dx
        dxb_ref[...] = dx.astype(BF16)

    prev_spec = lambda w: pl.BlockSpec((HALO, w), lambda i: (jnp.maximum(i * halo_per_tile - 1, 0), 0))
    next_spec = lambda w: pl.BlockSpec((HALO, w), lambda i: (jnp.minimum((i + 1) * halo_per_tile, n_halo - 1), 0))
    return _hosting_call(
        body, "mixer_b_bwd", t_tok // tm, [dout, dout, x, p, p, p, g, conv_w, gath],
        in_specs=[_row_spec(tm, d), next_spec(d), _row_spec(tm, d), _row_spec(tm, e3), prev_spec(e3), next_spec(e3),
                  _const_spec((1, d)), _const_spec((SUBLANES, e)), ANY],
        out_specs=[_row_spec(tm, d), _row_spec(tm, d), _row_spec(tm, d), _row_spec(tm, e), _row_spec(tm, e3),
                   _const_spec((1, d)), _const_spec((SUBLANES, e))],
        out_shape=[jax.ShapeDtypeStruct((t_tok, d), F32), jax.ShapeDtypeStruct((t_tok, d), BF16),
                   jax.ShapeDtypeStruct((t_tok, d), BF16), jax.ShapeDtypeStruct((t_tok, e), BF16),
                   jax.ShapeDtypeStruct((t_tok, e3), BF16), jax.ShapeDtypeStruct((1, d), F32),
                   jax.ShapeDtypeStruct((SUBLANES, e), F32)],
        scratch=[pltpu.VMEM((e3, d), BF16), pltpu.VMEM((e, d), BF16), pltpu.SemaphoreType.DMA((2 * N_DEV,))],
        hosted=hosted)


def _loss_head(x, target, g, tm):
    t_tok, d = x.shape

    def body(x_ref, t_ref, g_ref, loss_ref, dx_ref, dxb_ref, dg_ref):
        @pl.when(pl.program_id(0) == 0)
        def _():
            loss_ref[...] = jnp.zeros_like(loss_ref)
            dg_ref[...] = jnp.zeros_like(dg_ref)

        gv = g_ref[...]
        y, xhat, r = _rms_fwd(x_ref[...], gv)
        err = y - t_ref[...]
        loss_ref[...] += 0.5 * jnp.sum(_row_mean(err * err), axis=0, keepdims=True)
        dxr, dg_row = _rms_bwd(err * (1.0 / d), xhat, r, gv)
        dg_ref[...] += dg_row
        dx_ref[...] = dxr
        dxb_ref[...] = dxr.astype(BF16)

    return pl.pallas_call(
        body, name="loss_head", grid=(t_tok // tm,),
        in_specs=[_row_spec(tm, d), _row_spec(tm, d), _const_spec((1, d))],
        out_specs=[_const_spec((1, 1)), _row_spec(tm, d), _row_spec(tm, d), _const_spec((1, d))],
        out_shape=[jax.ShapeDtypeStruct((1, 1), F32), jax.ShapeDtypeStruct((t_tok, d), F32),
                   jax.ShapeDtypeStruct((t_tok, d), BF16), jax.ShapeDtypeStruct((1, d), F32)],
        compiler_params=_params(),
    )(x, target, g)


def _wgrad(a, b, bm, name):
    t_tok, m = a.shape
    n = b.shape[1]

    def body(a_ref, b_ref, o_ref):
        o_ref[...] = _tn(a_ref[...], b_ref[...]).astype(o_ref.dtype)

    return pl.pallas_call(
        body, name=name, grid=(m // bm,),
        in_specs=[pl.BlockSpec((t_tok, bm), lambda i: (0, i)), _const_spec((t_tok, n))],
        out_specs=pl.BlockSpec((bm, n), lambda i: (i, 0)),
        out_shape=jax.ShapeDtypeStruct((m, n), BF16),
        compiler_params=_params(sequential=False),
    )(a, b)


def _sum_slots(land, rb, name):
    n_slots, rows, cols = land.shape

    def body(l_ref, o_ref):
        acc = l_ref[0].astype(F32)
        for k in range(1, n_slots):
            acc = acc + l_ref[k].astype(F32)
        o_ref[...] = acc

    return pl.pallas_call(
        body, name=name, grid=(rows // rb,),
        in_specs=[pl.BlockSpec((n_slots, rb, cols), lambda i: (0, i, 0))],
        out_specs=pl.BlockSpec((rb, cols), lambda i: (i, 0)),
        out_shape=jax.ShapeDtypeStruct((rows, cols), F32),
        compiler_params=_params(sequential=False),
    )(land)


def _adamw(w, grad, m, v, rb, name):
    rows, cols = w.shape
    c1 = 1.0 / (1.0 - ADAM_B1 ** ADAM_STEP)
    c2 = 1.0 / (1.0 - ADAM_B2 ** ADAM_STEP)

    def body(w_ref, g_ref, m_ref, v_ref, d_ref, mo_ref, vo_ref):
        gv = g_ref[...]
        mn = ADAM_B1 * m_ref[...] + (1.0 - ADAM_B1) * gv
        vn = ADAM_B2 * v_ref[...] + (1.0 - ADAM_B2) * (gv * gv)
        mo_ref[...] = mn
        vo_ref[...] = vn
        d_ref[...] = -ADAM_LR * ((mn * c1) / (jnp.sqrt(vn * c2) + ADAM_EPS) + ADAM_WD * w_ref[...])

    spec = pl.BlockSpec((rb, cols), lambda i: (i, 0))
    shape = jax.ShapeDtypeStruct((rows, cols), F32)
    return pl.pallas_call(
        body, name=name, grid=(rows // rb,),
        in_specs=[spec] * 4, out_specs=[spec] * 3, out_shape=[shape] * 3,
        compiler_params=_params(sequential=False),
    )(w, grad, m, v)


def _pack_small(parts, rows):
    flat = jnp.concatenate([p.reshape(-1).astype(F32) for p in parts])
    return jnp.pad(flat, (0, rows * LANES - flat.shape[0])).reshape(rows, LANES)


def _unpack_small(packed, shapes):
    flat = packed.reshape(-1)
    out = []
    pos = 0
    for s in shapes:
        n = math.prod(s)
        out.append(flat[pos:pos + n].reshape(s))
        pos += n
    return out


def kernel(x, mix_norm, ffn_norm, a_w_in, a_v_gain, a_v_bias, a_w_s, a_b_s, a_w_out, b_w_in, b_conv_w, b_w_out, ffn_w_gate, ffn_w_up, ffn_w_down, final_norm, loss_target, m_mix_norm, m_ffn_norm, m_a_w_in, m_a_v_gain, m_a_v_bias, m_a_w_s, m_a_b_s, m_a_w_out, m_b_w_in, m_b_conv_w, m_b_w_out, m_ffn_w_gate, m_ffn_w_up, m_ffn_w_down, m_final_norm, v_mix_norm, v_ffn_norm, v_a_w_in, v_a_v_gain, v_a_v_bias, v_a_w_s, v_a_b_s, v_a_w_out, v_b_w_in, v_b_conv_w, v_b_w_out, v_ffn_w_gate, v_ffn_w_up, v_ffn_w_down, v_final_norm):
    bsz, seq, d = x.shape
    t_tok = bsz * seq
    me = _my_index()
    xt = x.reshape(t_tok, d)
    target = loss_target.reshape(t_tok, d)
    e_a = a_v_gain.shape[1]
    e_b = b_w_out.shape[1] * N_DEV
    n_layers = ffn_w_gate.shape[0]
    f_shard = ffn_w_gate.shape[2]
    f_full = f_shard * N_DEV

    conv_pad = jnp.pad(b_conv_w[0], ((0, SUBLANES - CONV_W), (0, 0)))
    sh_a = jnp.concatenate([a_w_in[0].T, a_w_out[0]]).astype(BF16)
    sh_b = jnp.concatenate([b_w_in[0].T, b_w_out[0]]).astype(BF16)
    sh_f = [jnp.concatenate([ffn_w_gate[l].T, ffn_w_up[l].T, ffn_w_down[l]]).astype(BF16) for l in range(n_layers)]
    bfull = jnp.repeat(a_b_s[0].T, GROUP, axis=1)

    gath_a, conv_g = _all_gather([sh_a, conv_pad], "gather_a")
    conv_full = jnp.pad(conv_g[:, :CONV_W, :].transpose(1, 0, 2).reshape(CONV_W, e_b), ((0, SUBLANES - CONV_W), (0, 0)))
    (x1, zpre), (gath_f0, gath_b) = _mixer_a_fwd(xt, mix_norm[0:1], gath_a, a_v_gain, a_v_bias, a_w_s[0], bfull, tm=256,
                                                 hosted=_HostedGathers([sh_f[0], sh_b], mid_lead=2))
    (x2, gate0, up0), (gath_f1,) = _ffn_fwd(x1, ffn_norm[0:1], gath_f0, tm=256, name="ffn_fwd0",
                                            hosted=_HostedGathers([sh_f[1]], mid_lead=2))
    x3, p_b = _mixer_b_fwd(x2, mix_norm[1:2], gath_b, conv_full, tm=256, seq=seq)
    (x4, gate1, up1), _ = _ffn_fwd(x3, ffn_norm[1:2], gath_f1, tm=256, name="ffn_fwd1")
    loss_part, dx4, dx4_bf, d_final = _loss_head(x4, target, final_norm.reshape(1, d), tm=512)

    ffn_entries = [(0, 0, f_shard), (0, f_full, f_shard), (1, 0, f_shard)]
    (dx3, dx3_bf, h_f1, act1, dgu1, d_fn1), _ = _ffn_bwd(dx4, x3, gate1, up1, ffn_norm[1:2], gath_f1, tm=256,
                                                         name="ffn_bwd1")
    g_down1 = _wgrad(act1, dx4_bf, 256, "wgrad_down1")
    g_gu1 = _wgrad(dgu1, h_f1, 512, "wgrad_gate_up1")
    ps_f1 = _pair_reduce([g_gu1, g_down1], ffn_entries, "pair_reduce_f1")
    (dx2, dx2_bf, h_b, y_b, dp_b, d_mn1, d_conv), (land_f1,) = _mixer_b_bwd(
        dx3, x2, p_b, mix_norm[1:2], gath_b, conv_full, tm=256, seq=seq, hosted=_HostedChipScatter(ps_f1))
    g_b_out = _wgrad(y_b, dx3_bf, 256, "wgrad_b_out")
    g_b_in = _wgrad(dp_b, h_b, 512, "wgrad_b_in")
    ps_b = _pair_reduce([g_b_in, g_b_out], [(0, 0, b_w_in.shape[2]), (1, 0, b_w_out.shape[1])], "pair_reduce_b")
    (dx1, dx1_bf, h_f0, act0, dgu0, d_fn0), (land_b,) = _ffn_bwd(dx2, x1, gate0, up0, ffn_norm[0:1], gath_f0, tm=256,
                                                                 name="ffn_bwd0", hosted=_HostedChipScatter(ps_b))
    g_down0 = _wgrad(act0, dx2_bf, 256, "wgrad_down0")
    g_gu0 = _wgrad(dgu0, h_f0, 512, "wgrad_gate_up0")
    ps_f0 = _pair_reduce([g_gu0, g_down0], ffn_entries, "pair_reduce_f0")
    (dx0, _, h_a, y_a, dz_a, d_mn0, d_gain, d_bias, d_ws, d_bs_acc), (land_f0,) = _mixer_a_bwd(
        dx1, xt, zpre, mix_norm[0:1], gath_a, a_v_gain, a_v_bias, a_w_s[0], bfull, tm=256,
        hosted=_HostedChipScatter(ps_f0))
    g_a_out = _wgrad(y_a, dx1_bf, 256, "wgrad_a_out")
    g_a_in = _wgrad(dz_a, h_a, 512, "wgrad_a_in")
    ps_a = _pair_reduce([g_a_in, g_a_out], [(0, 0, a_w_in.shape[2]), (1, 0, a_w_out.shape[1])], "pair_reduce_a")
    land_a = _chip_scatter(ps_a, "chip_scatter_a")
    d_bs = d_bs_acc.reshape(CHUNK, HEADS, GROUP).sum(axis=2).T

    gs_a = _sum_slots(land_a, land_a.shape[1] // 2, "sum_a")
    gs_b = _sum_slots(land_b, land_b.shape[1] // 2, "sum_b")
    gs_f = [_sum_slots(land_f0, land_f0.shape[1] // 2, "sum_f0"), _sum_slots(land_f1, land_f1.shape[1] // 2, "sum_f1")]

    small_grads = [jnp.concatenate([d_mn0, d_mn1]), jnp.concatenate([d_fn0, d_fn1]), d_gain, d_bias, d_ws, d_bs,
                   d_final, d_conv[:CONV_W]]
    small_shapes = [(n_layers, d), (n_layers, d), (1, e_a), (1, e_a), (1, HEADS, CHUNK, CHUNK), (1, HEADS, CHUNK), (d,),
                    (CONV_W, e_b)]
    n_small = sum(math.prod(s) for s in small_shapes)
    blk_rows = -(-n_small // (N_DEV * LANES * SUBLANES)) * SUBLANES
    small_rows = blk_rows * N_DEV
    packed = _pack_small(small_grads, small_rows)
    small_land = _scatter_blocks([packed], [(0, 0, blk_rows)], blk_rows, LANES, F32, "scatter_small")
    small_sum = _sum_slots(small_land, blk_rows, "sum_small")
    small_all = _all_gather([small_sum], "gather_small")[0].reshape(small_rows, LANES)
    (gr_mix, gr_ffn, gr_gain, gr_bias, gr_ws, gr_bs, gr_final, gr_conv_full) = _unpack_small(small_all, small_shapes)
    gr_conv = lax.dynamic_slice_in_dim(gr_conv_full, me * (e_b // N_DEV), e_b // N_DEV, axis=1)[None]

    n_a_in, n_b_in = a_w_in.shape[2], b_w_in.shape[2]
    gr_a_in = gs_a[:n_a_in].T[None]
    gr_a_out = gs_a[n_a_in:][None]
    gr_b_in = gs_b[:n_b_in].T[None]
    gr_b_out = gs_b[n_b_in:][None]
    gr_gate = jnp.stack([gs_f[l][:f_shard].T for l in range(n_layers)])
    gr_up = jnp.stack([gs_f[l][f_shard:2 * f_shard].T for l in range(n_layers)])
    gr_down = jnp.stack([gs_f[l][2 * f_shard:] for l in range(n_layers)])

    def big(w, g, m, v, name):
        shape = w.shape
        two_d = (math.prod(shape[:-1]), shape[-1])
        rb = two_d[0] // 2
        out = _adamw(w.reshape(two_d), g.reshape(two_d), m.reshape(two_d), v.reshape(two_d), rb, name)
        return [o.reshape(shape) for o in out]

    small_w = [mix_norm, ffn_norm, a_v_gain, a_v_bias, a_w_s, a_b_s, final_norm]
    small_m = [m_mix_norm, m_ffn_norm, m_a_v_gain, m_a_v_bias, m_a_w_s, m_a_b_s, m_final_norm]
    small_v = [v_mix_norm, v_ffn_norm, v_a_v_gain, v_a_v_bias, v_a_w_s, v_a_b_s, v_final_norm]
    small_g = [gr_mix, gr_ffn, gr_gain, gr_bias, gr_ws, gr_bs, gr_final]
    sm_shapes = small_shapes[:-1]
    sm_out = _adamw(_pack_small(small_w, small_rows), _pack_small(small_g, small_rows), _pack_small(small_m, small_rows),
                    _pack_small(small_v, small_rows), small_rows, "adamw_small")
    sm_delta, sm_m, sm_v = [_unpack_small(o, sm_shapes) for o in sm_out]

    conv_out = _adamw(b_conv_w[0], gr_conv[0], m_b_conv_w[0], v_b_conv_w[0], CONV_W, "adamw_conv")
    conv_delta, conv_m, conv_v = [o[None] for o in conv_out]

    res = {
        "a_w_in": big(a_w_in, gr_a_in, m_a_w_in, v_a_w_in, "adamw_a_in"),
        "a_w_out": big(a_w_out, gr_a_out, m_a_w_out, v_a_w_out, "adamw_a_out"),
        "b_w_in": big(b_w_in, gr_b_in, m_b_w_in, v_b_w_in, "adamw_b_in"),
        "b_w_out": big(b_w_out, gr_b_out, m_b_w_out, v_b_w_out, "adamw_b_out"),
        "ffn_w_gate": big(ffn_w_gate, gr_gate, m_ffn_w_gate, v_ffn_w_gate, "adamw_gate"),
        "ffn_w_up": big(ffn_w_up, gr_up, m_ffn_w_up, v_ffn_w_up, "adamw_up"),
        "ffn_w_down": big(ffn_w_down, gr_down, m_ffn_w_down, v_ffn_w_down, "adamw_down"),
    }

    order = ["mix_norm", "ffn_norm", "a_w_in", "a_v_gain", "a_v_bias", "a_w_s", "a_b_s", "a_w_out", "b_w_in",
             "b_conv_w", "b_w_out", "ffn_w_gate", "ffn_w_up", "ffn_w_down", "final_norm"]
    small_names = ["mix_norm", "ffn_norm", "a_v_gain", "a_v_bias", "a_w_s", "a_b_s", "final_norm"]
    grads = {"a_w_in": gr_a_in, "a_w_out": gr_a_out, "b_w_in": gr_b_in, "b_w_out": gr_b_out, "ffn_w_gate": gr_gate,
             "ffn_w_up": gr_up, "ffn_w_down": gr_down, "b_conv_w": gr_conv}
    deltas, new_m, new_v = {}, {}, {}
    for k, name in enumerate(small_names):
        grads[name] = small_g[k]
        deltas[name], new_m[name], new_v[name] = sm_delta[k], sm_m[k], sm_v[k]
    deltas["b_conv_w"], new_m["b_conv_w"], new_v["b_conv_w"] = conv_delta, conv_m, conv_v
    for name, (dl, mm, vv) in res.items():
        deltas[name], new_m[name], new_v[name] = dl, mm, vv

    loss = lax.psum(loss_part[0, 0], ("x", "y", "c"))
    grad_x = dx0.reshape(bsz, seq, d)
    return (loss, grad_x, *[grads[n] for n in order], *[deltas[n] for n in order],
            *[new_m[n] for n in order], *[new_v[n] for n in order])
```

```python
import math

import jax
import jax.numpy as jnp
from jax import lax
from jax.experimental import pallas as pl
from jax.experimental.pallas import tpu as pltpu

F32 = jnp.float32
BF16 = jnp.bfloat16

N_DEV = 8
N_CHIP = 4
CHUNK = 128
HEADS = 16
GROUP = 128
CONV_W = 3
NORM_EPS = 1e-6
GELU_C = math.sqrt(2.0 / math.pi)
GELU_K = 0.044715

ADAM_LR = 0.001
ADAM_B1 = 0.9
ADAM_B2 = 0.999
ADAM_EPS = 1e-08
ADAM_WD = 0.01
ADAM_STEP = 10

LANES = 128
SUBLANES = 8
VMEM_LIMIT = 60 * 1024 * 1024
HALO = 16
MXU_WIDTH = 256
FFN_CHUNKS = 2

MESH = pl.DeviceIdType.MESH
ANY = pl.BlockSpec(memory_space=pl.ANY)


def _params(sequential=True):
    return pltpu.CompilerParams(
        dimension_semantics=("arbitrary",) if sequential else None,
        vmem_limit_bytes=VMEM_LIMIT)


def _nn(a, b):
    return jnp.dot(a, b, preferred_element_type=F32)


def _nt(a, b):
    return lax.dot_general(a, b, (((1,), (1,)), ((), ())), preferred_element_type=F32)


def _tn(a, b):
    return lax.dot_general(a, b, (((0,), (0,)), ((), ())), preferred_element_type=F32)


def _row_mean(a):
    return jnp.mean(a, axis=-1, keepdims=True)


def _col_sum(a):
    return jnp.sum(a, axis=0, keepdims=True)


def _rms_fwd(x, g):
    r = lax.rsqrt(_row_mean(x * x) + NORM_EPS)
    xhat = x * r
    return xhat * g, xhat, r


def _rms_bwd(dh, xhat, r, g):
    a = dh * g
    dx = r * (a - xhat * _row_mean(a * xhat))
    return dx, _col_sum(dh * xhat)


def _gelu_and_grad(x):
    x2 = x * x
    t = jnp.tanh(GELU_C * (x + GELU_K * x2 * x))
    half = 0.5 * (1.0 + t)
    d = half + 0.5 * x * (1.0 - t * t) * (GELU_C * (1.0 + 3.0 * GELU_K * x2))
    return x * half, d


def _sigmoid(x):
    return 1.0 / (1.0 + jnp.exp(-x))


def _row_spec(tm, width):
    return pl.BlockSpec((tm, width), lambda i: (i, 0))


def _const_spec(shape):
    nd = len(shape)
    return pl.BlockSpec(shape, lambda i: (0,) * nd)


def _load_group(parts, sems):
    @pl.when(pl.program_id(0) == 0)
    def _():
        copies = []
        for k, (gath_ref, first, n, dst) in enumerate(parts):
            for j in range(N_DEV):
                copies.append(pltpu.make_async_copy(gath_ref.at[j, pl.ds(first, n), :], dst.at[pl.ds(j * n, n), :],
                                                    sems.at[k * N_DEV + j]))
        for cp in copies:
            cp.start()
        for cp in copies:
            cp.wait()


def _hosting_call(body, name, n_steps, arrays, in_specs, out_specs, out_shape, scratch, hosted=()):
    n_in, n_out, n_scr = len(arrays), len(out_shape), len(scratch)
    h_arrays = [a for h in hosted for a in h.arrays]
    h_shapes = [s for h in hosted for s in h.out_shapes]
    h_sems = [s for h in hosted for s in h.sem_shapes]

    def full_body(*refs):
        pos = 0
        groups = []
        for n in (n_in, len(h_arrays), n_out, len(h_shapes), n_scr, len(h_sems)):
            groups.append(refs[pos:pos + n])
            pos += n
        own_in, h_in, own_out, h_out, own_scr, h_sem = groups
        per_host = []
        pi = po = ps = 0
        for h in hosted:
            ni, no, ns = len(h.arrays), len(h.out_shapes), len(h.sem_shapes)
            per_host.append((h, h_in[pi:pi + ni], h_out[po:po + no], h_sem[ps:ps + ns]))
            pi, po, ps = pi + ni, po + no, ps + ns
        for h, ins, outs, sems in per_host:
            h.begin(ins, outs, sems, n_steps)
        body(*own_in, *own_out, *own_scr)
        for h, ins, outs, sems in per_host:
            h.end(ins, outs, sems, n_steps)

    outs = pl.pallas_call(
        full_body, name=name, grid=(n_steps,),
        in_specs=list(in_specs) + [ANY] * len(h_arrays),
        out_specs=list(out_specs) + [ANY] * len(h_shapes),
        out_shape=list(out_shape) + h_shapes,
        scratch_shapes=list(scratch) + h_sems,
        compiler_params=_params(),
    )(*arrays, *h_arrays)
    return outs[:n_out], outs[n_out:]


def _my_index():
    return 4 * lax.axis_index("x") + 2 * lax.axis_index("y") + lax.axis_index("c")


class _Gather:
    def __init__(self, shard, out, send_sems, recv_sems, local_sem):
        self.shard, self.out = shard, out
        self.send_sems, self.recv_sems, self.local_sem = send_sems, recv_sems, local_sem
        x, y, c = lax.axis_index("x"), lax.axis_index("y"), lax.axis_index("c")
        self.c = c
        self.me, self.sibling = (x, y, c), (x, y, 1 - c)
        self.chips = [(1 - x, y), (x, 1 - y), (1 - x, 1 - y)]

    def _slot(self, dev):
        return self.out.at[4 * dev[0] + 2 * dev[1] + dev[2]]

    def _copy(self, k, block, to, src=None):
        return pltpu.make_async_remote_copy(
            src_ref=self._slot(block) if src is None else src, dst_ref=self._slot(block),
            send_sem=self.send_sems.at[k], recv_sem=self.recv_sems.at[k], device_id=to, device_id_type=MESH)

    def _local(self):
        return pltpu.make_async_copy(self.shard, self._slot(self.me), self.local_sem)

    def start(self):
        self._local().start()
        self._copy(0, self.me, self.sibling, src=self.shard).start()
        for j, chip in enumerate(self.chips):
            self._copy(1 + j, self.me, (*chip, self.c), src=self.shard).start()

    def forward(self):
        for j, chip in enumerate(self.chips):
            self._copy(1 + j, (*chip, self.c), self.me).wait_recv()
            self._copy(4 + j, (*chip, self.c), self.sibling).start()

    def finish(self):
        self._copy(0, self.sibling, self.me).wait_recv()
        for j, chip in enumerate(self.chips):
            self._copy(4 + j, (*chip, 1 - self.c), self.me).wait_recv()
        for k in range(N_DEV - 1):
            self._copy(k, self.me, self.sibling).wait_send()
        self._local().wait()


class _HostedGathers:
    def __init__(self, shards, mid_lead):
        n = len(shards)
        self.arrays = shards
        self.mid_lead = mid_lead
        self.out_shapes = [jax.ShapeDtypeStruct((N_DEV,) + s.shape, s.dtype) for s in shards]
        self.sem_shapes = [pltpu.SemaphoreType.DMA((n, N_DEV - 1)), pltpu.SemaphoreType.DMA((n, N_DEV - 1)),
                           pltpu.SemaphoreType.DMA((n,))]

    def _gathers(self, ins, outs, sems):
        return [_Gather(ins[a], outs[a], sems[0].at[a], sems[1].at[a], sems[2].at[a]) for a in range(len(ins))]

    def begin(self, ins, outs, sems, n_steps):
        i = pl.program_id(0)

        @pl.when(i == 0)
        def _():
            for g in self._gathers(ins, outs, sems):
                g.start()

        @pl.when(i == max(n_steps - 1 - self.mid_lead, 0))
        def _():
            for g in self._gathers(ins, outs, sems):
                g.forward()

    def end(self, ins, outs, sems, n_steps):
        @pl.when(pl.program_id(0) == n_steps - 1)
        def _():
            for g in self._gathers(ins, outs, sems):
                g.finish()


def _exchange(hosted, name):
    return _hosting_call(lambda: None, name, 1, [], [], [], [], [], hosted=hosted)[1]


class _ChipScatter:
    def __init__(self, pairsum, row0, land, send_sems, recv_sems, local_sem):
        self.pairsum, self.row0, self.land = pairsum, row0, land
        self.send_sems, self.recv_sems, self.local_sem = send_sems, recv_sems, local_sem
        x, y, c = lax.axis_index("x"), lax.axis_index("y"), lax.axis_index("c")
        self.c = c
        self.chip = 2 * x + y
        self.others = [(1 - x, y), (x, 1 - y), (1 - x, 1 - y)]

    def _src(self, chip):
        return self.pairsum.at[chip, pl.ds(self.row0, self.land.shape[1]), :]

    def _copy(self, k):
        ox, oy = self.others[k]
        return pltpu.make_async_remote_copy(
            src_ref=self._src(2 * ox + oy), dst_ref=self.land.at[self.chip],
            send_sem=self.send_sems.at[k], recv_sem=self.recv_sems.at[k], device_id=(ox, oy, self.c),
            device_id_type=MESH)

    def _arrival(self, k):
        ox, oy = self.others[k]
        return pltpu.make_async_remote_copy(
            src_ref=self._src(self.chip), dst_ref=self.land.at[2 * ox + oy],
            send_sem=self.send_sems.at[k], recv_sem=self.recv_sems.at[k], device_id=(ox, oy, self.c),
            device_id_type=MESH)

    def _local(self):
        return pltpu.make_async_copy(self._src(self.chip), self.land.at[self.chip], self.local_sem)

    def start(self):
        self._local().start()
        for k in range(N_CHIP - 1):
            self._copy(k).start()

    def finish(self):
        for k in range(N_CHIP - 1):
            self._arrival(k).wait_recv()
        for k in range(N_CHIP - 1):
            self._copy(k).wait_send()
        self._local().wait()


class _HostedChipScatter:
    def __init__(self, pairsum, row0=0, n=None):
        n = pairsum.shape[1] - row0 if n is None else n
        self.row0 = row0
        self.arrays = [pairsum]
        self.out_shapes = [jax.ShapeDtypeStruct((N_CHIP, n, pairsum.shape[2]), pairsum.dtype)]
        self.sem_shapes = [pltpu.SemaphoreType.DMA((N_CHIP - 1,)), pltpu.SemaphoreType.DMA((N_CHIP - 1,)),
                           pltpu.SemaphoreType.DMA(())]

    def begin(self, ins, outs, sems, n_steps):
        @pl.when(pl.program_id(0) == 0)
        def _():
            _ChipScatter(ins[0], self.row0, outs[0], *sems).start()

    def end(self, ins, outs, sems, n_steps):
        @pl.when(pl.program_id(0) == n_steps - 1)
        def _():
            _ChipScatter(ins[0], self.row0, outs[0], *sems).finish()


def _pair_reduce(arrays, entries, name):
    n_arr, n_ent = len(arrays), len(entries)
    cols = arrays[0].shape[1]
    offsets = []
    total = 0
    for _, _, n in entries:
        offsets.append(total)
        total += n

    def body(*refs):
        ins, out_ref = refs[:n_arr], refs[n_arr]
        rbuf, own, send_sems, recv_sems, own_sems = refs[n_arr + 1:]
        q = pl.program_id(0)
        x, y, c = lax.axis_index("x"), lax.axis_index("y"), lax.axis_index("c")

        def block(e, chip, core):
            ai, first, n = entries[e]
            return ins[ai].at[pl.ds(first + (2 * chip + core) * n, n), :]

        def to_sibling(e, chip):
            return pltpu.make_async_remote_copy(
                src_ref=block(e, chip, 1 - c), dst_ref=rbuf.at[chip, pl.ds(offsets[e], entries[e][2]), :],
                send_sem=send_sems.at[e, chip], recv_sem=recv_sems.at[e, chip], device_id=(x, y, 1 - c),
                device_id_type=MESH)

        @pl.when(q == 0)
        def _():
            for chip in range(N_CHIP):
                for e in range(n_ent):
                    to_sibling(e, chip).start()

        loads = [pltpu.make_async_copy(block(e, q, c), own.at[pl.ds(offsets[e], entries[e][2]), :], own_sems.at[e])
                 for e in range(n_ent)]
        for cp in loads:
            cp.start()
        for cp in loads:
            cp.wait()
        for e in range(n_ent):
            to_sibling(e, q).wait_recv()
        out_ref[...] = (own[...].astype(F32) + rbuf[q].astype(F32)).astype(out_ref.dtype)

        @pl.when(q == N_CHIP - 1)
        def _():
            for chip in range(N_CHIP):
                for e in range(n_ent):
                    to_sibling(e, chip).wait_send()

    return pl.pallas_call(
        body, name=name, grid=(N_CHIP,),
        in_specs=[ANY] * n_arr,
        out_specs=pl.BlockSpec((None, total, cols), lambda q: (q, 0, 0)),
        out_shape=jax.ShapeDtypeStruct((N_CHIP, total, cols), BF16),
        scratch_shapes=[pltpu.VMEM((N_CHIP, total, cols), BF16), pltpu.VMEM((total, cols), BF16),
                        pltpu.SemaphoreType.DMA((n_ent, N_CHIP)), pltpu.SemaphoreType.DMA((n_ent, N_CHIP)),
                        pltpu.SemaphoreType.DMA((n_ent,))],
        compiler_params=_params(),
    )(*arrays)


class _HostedScatterAll:
    def __init__(self, packed):
        n = packed.shape[0] // N_DEV
        self.n = n
        self.arrays = [packed]
        self.out_shapes = [jax.ShapeDtypeStruct((N_DEV, n, packed.shape[1]), packed.dtype)]
        self.sem_shapes = [pltpu.SemaphoreType.DMA((N_DEV - 1,)), pltpu.SemaphoreType.DMA((N_DEV - 1,)),
                           pltpu.SemaphoreType.DMA(())]

    def _copies(self, ins, outs, sems, with_arrivals):
        src, land = ins[0], outs[0]
        send_sems, recv_sems, local_sem = sems
        me = _my_index()

        def block(p):
            return src.at[pl.ds(p * self.n, self.n), :]

        local = pltpu.make_async_copy(block(me), land.at[me], local_sem)
        sends, arrivals = [], []
        for k in range(1, N_DEV):
            p = (me + k) % N_DEV
            q = (me + N_DEV - k) % N_DEV
            sends.append(pltpu.make_async_remote_copy(
                src_ref=block(p), dst_ref=land.at[me], send_sem=send_sems.at[k - 1], recv_sem=recv_sems.at[k - 1],
                device_id=(p // 4, (p // 2) % 2, p % 2), device_id_type=MESH))
            if with_arrivals:
                arrivals.append(pltpu.make_async_remote_copy(
                    src_ref=block(me), dst_ref=land.at[q], send_sem=send_sems.at[k - 1], recv_sem=recv_sems.at[k - 1],
                    device_id=(q // 4, (q // 2) % 2, q % 2), device_id_type=MESH))
        return local, sends, arrivals

    def begin(self, ins, outs, sems, n_steps):
        @pl.when(pl.program_id(0) == 0)
        def _():
            local, sends, _ = self._copies(ins, outs, sems, with_arrivals=False)
            local.start()
            for cp in sends:
                cp.start()

    def end(self, ins, outs, sems, n_steps):
        @pl.when(pl.program_id(0) == n_steps - 1)
        def _():
            local, sends, arrivals = self._copies(ins, outs, sems, with_arrivals=True)
            for cp in arrivals:
                cp.wait_recv()
            for cp in sends:
                cp.wait_send()
            local.wait()


def _tril_weights(ws_ref):
    r = lax.broadcasted_iota(jnp.int32, (CHUNK, CHUNK), 0)
    c = lax.broadcasted_iota(jnp.int32, (CHUNK, CHUNK), 1)
    return [jnp.where(r >= c, ws_ref[h], 0.0).astype(BF16) for h in range(HEADS)]


def _sgu_stats(zpre, gain, bias):
    e = zpre.shape[1] // 2
    z, dz = _gelu_and_grad(zpre)
    u, v = z[:, :e], z[:, e:]
    vc = v - _row_mean(v)
    rstd = lax.rsqrt(_row_mean(vc * vc) + NORM_EPS)
    vhat = vc * rstd
    return u, vhat, rstd, vhat * gain + bias, dz


def _spatial_fwd(wt, vn_bf, bfull_ref, sv_ref, tm):
    for ci in range(tm // CHUNK):
        rows = slice(ci * CHUNK, (ci + 1) * CHUNK)
        for h in range(HEADS):
            cols = slice(h * GROUP, (h + 1) * GROUP)
            sv_ref[rows, cols] = _nn(wt[h], vn_bf[rows, cols]) + bfull_ref[:, cols]


def _mixer_a_fwd(x, g, gath, gain, bias, ws, bfull, tm, hosted=()):
    t_tok, d = x.shape
    e = gain.shape[1]
    e2 = 2 * e
    n_in, n_out = e2 // N_DEV, e // N_DEV

    def body(x_ref, g_ref, gain_ref, bias_ref, ws_ref, bfull_ref, gath_ref,
             xo_ref, zpre_ref, y_ref, win_v, wout_v, sv_v, sems):
        _load_group([(gath_ref, 0, n_in, win_v), (gath_ref, n_in, n_out, wout_v)], sems)
        xv = x_ref[...]
        h = _rms_fwd(xv, g_ref[...])[0].astype(BF16)
        zpre = _nt(h, win_v[...])
        zpre_ref[...] = zpre.astype(BF16)
        u, _, _, vn, _ = _sgu_stats(zpre, gain_ref[...], bias_ref[...])
        _spatial_fwd(_tril_weights(ws_ref), vn.astype(BF16), bfull_ref, sv_v, tm)
        y = (u * sv_v[...]).astype(BF16)
        y_ref[...] = y
        xo_ref[...] = xv + _nn(y, wout_v[...])

    return _hosting_call(
        body, "mixer_a_fwd", t_tok // tm, [x, g, gain, bias, ws, bfull, gath],
        in_specs=[_row_spec(tm, d), _const_spec((1, d)), _const_spec((1, e)), _const_spec((1, e)),
                  _const_spec((HEADS, CHUNK, CHUNK)), _const_spec((CHUNK, e)), ANY],
        out_specs=[_row_spec(tm, d), _row_spec(tm, e2), _row_spec(tm, e)],
        out_shape=[jax.ShapeDtypeStruct((t_tok, d), F32), jax.ShapeDtypeStruct((t_tok, e2), BF16),
                   jax.ShapeDtypeStruct((t_tok, e), BF16)],
        scratch=[pltpu.VMEM((e2, d), BF16), pltpu.VMEM((e, d), BF16), pltpu.VMEM((tm, e), F32),
                 pltpu.SemaphoreType.DMA((2 * N_DEV,))],
        hosted=hosted)


def _mixer_a_bwd(dout, x, zpre, g, gath, gain, bias, ws, bfull, tm, hosted=()):
    t_tok, d = x.shape
    e = gain.shape[1]
    e2 = 2 * e
    n_in, n_out = e2 // N_DEV, e // N_DEV
    n_steps = t_tok // tm

    def body(dout_ref, x_ref, zpre_ref, g_ref, gain_ref, bias_ref, ws_ref, bfull_ref, gath_ref,
             dx_ref, dxb_ref, h_ref, dz_ref, dg_ref, dgain_ref, dbias_ref, dws_ref, dbs_ref,
             win_v, wout_v, sv_v, dvn_v, sems):
        i = pl.program_id(0)
        _load_group([(gath_ref, 0, n_in, win_v), (gath_ref, n_in, n_out, wout_v)], sems)

        @pl.when(i == 0)
        def _():
            dg_ref[...] = jnp.zeros_like(dg_ref)
            dgain_ref[...] = jnp.zeros_like(dgain_ref)
            dbias_ref[...] = jnp.zeros_like(dbias_ref)
            dws_ref[...] = jnp.zeros_like(dws_ref)
            dbs_ref[...] = jnp.zeros_like(dbs_ref)

        xv = x_ref[...]
        gv = g_ref[...]
        hv, xhat, r = _rms_fwd(xv, gv)
        h_ref[...] = hv.astype(BF16)
        gain_v = gain_ref[...]
        u, vhat, rstd, vn, gelu_d = _sgu_stats(zpre_ref[...].astype(F32), gain_v, bias_ref[...])
        vn_bf = vn.astype(BF16)
        wt = _tril_weights(ws_ref)
        _spatial_fwd(wt, vn_bf, bfull_ref, sv_v, tm)
        sv = sv_v[...]

        dov = dout_ref[...]
        dy = _nt(dov.astype(BF16), wout_v[...])
        du = dy * sv
        dsv = dy * u
        dsv_bf = dsv.astype(BF16)
        for ci in range(tm // CHUNK):
            rows = slice(ci * CHUNK, (ci + 1) * CHUNK)
            dbs_ref[...] += dsv[rows, :]
            for h in range(HEADS):
                cols = slice(h * GROUP, (h + 1) * GROUP)
                dvn_v[rows, cols] = _tn(wt[h], dsv_bf[rows, cols])
                dws_ref[h] += _nt(dsv_bf[rows, cols], vn_bf[rows, cols])
        dvn = dvn_v[...]
        dgain_ref[...] += _col_sum(dvn * vhat)
        dbias_ref[...] += _col_sum(dvn)
        dvhat = dvn * gain_v
        dv = rstd * (dvhat - _row_mean(dvhat) - vhat * _row_mean(dvhat * vhat))
        dzpre = (jnp.concatenate([du, dv], axis=1) * gelu_d).astype(BF16)
        dz_ref[...] = dzpre
        dh = _nn(dzpre, win_v[...])
        dxr, dg_row = _rms_bwd(dh, xhat, r, gv)
        dg_ref[...] += dg_row
        dx = dov + dxr
        dx_ref[...] = dx
        dxb_ref[...] = dx.astype(BF16)

        @pl.when(i == n_steps - 1)
        def _():
            rr = lax.broadcasted_iota(jnp.int32, (CHUNK, CHUNK), 0)
            cc = lax.broadcasted_iota(jnp.int32, (CHUNK, CHUNK), 1)
            for h in range(HEADS):
                dws_ref[h] = jnp.where(rr >= cc, dws_ref[h], 0.0)

    return _hosting_call(
        body, "mixer_a_bwd", n_steps, [dout, x, zpre, g, gain, bias, ws, bfull, gath],
        in_specs=[_row_spec(tm, d), _row_spec(tm, d), _row_spec(tm, e2), _const_spec((1, d)),
                  _const_spec((1, e)), _const_spec((1, e)), _const_spec((HEADS, CHUNK, CHUNK)),
                  _const_spec((CHUNK, e)), ANY],
        out_specs=[_row_spec(tm, d), _row_spec(tm, d), _row_spec(tm, d), _row_spec(tm, e2),
                   _const_spec((1, d)), _const_spec((1, e)), _const_spec((1, e)),
                   _const_spec((HEADS, CHUNK, CHUNK)), _const_spec((CHUNK, e))],
        out_shape=[jax.ShapeDtypeStruct((t_tok, d), F32), jax.ShapeDtypeStruct((t_tok, d), BF16),
                   jax.ShapeDtypeStruct((t_tok, d), BF16), jax.ShapeDtypeStruct((t_tok, e2), BF16),
                   jax.ShapeDtypeStruct((1, d), F32), jax.ShapeDtypeStruct((1, e), F32),
                   jax.ShapeDtypeStruct((1, e), F32), jax.ShapeDtypeStruct((HEADS, CHUNK, CHUNK), F32),
                   jax.ShapeDtypeStruct((CHUNK, e), F32)],
        scratch=[pltpu.VMEM((e2, d), BF16), pltpu.VMEM((e, d), BF16), pltpu.VMEM((tm, e), F32),
                 pltpu.VMEM((tm, e), F32), pltpu.SemaphoreType.DMA((2 * N_DEV,))],
        hosted=hosted)


def _ffn_fwd(x, g, srcs, nf, tm, name, hosted=()):
    t_tok, d = x.shape
    f = nf * N_DEV
    firsts = [first for _, first in srcs]

    def body(x_ref, g_ref, sg_ref, su_ref, sd_ref, xo_ref, gate_ref, up_ref, wg_v, wu_v, wd_v, sems):
        _load_group([(sg_ref, firsts[0], nf, wg_v), (su_ref, firsts[1], nf, wu_v), (sd_ref, firsts[2], nf, wd_v)],
                    sems)
        xv = x_ref[...]
        h = _rms_fwd(xv, g_ref[...])[0].astype(BF16)
        gate = _nt(h, wg_v[...])
        up = _nt(h, wu_v[...])
        gate_ref[...] = gate.astype(BF16)
        up_ref[...] = up.astype(BF16)
        act = (gate * _sigmoid(gate) * up).astype(BF16)
        xo_ref[...] = xv + _nn(act, wd_v[...])

    return _hosting_call(
        body, name, t_tok // tm, [x, g] + [arr for arr, _ in srcs],
        in_specs=[_row_spec(tm, d), _const_spec((1, d)), ANY, ANY, ANY],
        out_specs=[_row_spec(tm, d), _row_spec(tm, f), _row_spec(tm, f)],
        out_shape=[jax.ShapeDtypeStruct((t_tok, d), F32), jax.ShapeDtypeStruct((t_tok, f), BF16),
                   jax.ShapeDtypeStruct((t_tok, f), BF16)],
        scratch=[pltpu.VMEM((f, d), BF16), pltpu.VMEM((f, d), BF16), pltpu.VMEM((f, d), BF16),
                 pltpu.SemaphoreType.DMA((3 * N_DEV,))],
        hosted=hosted)


def _ffn_bwd(dout, x, gate, up, g, srcs, nf, tm, name, hosted=()):
    t_tok, d = x.shape
    f = nf * N_DEV
    firsts = [first for _, first in srcs]
    per_chunk = -(-f // (FFN_CHUNKS * MXU_WIDTH)) * MXU_WIDTH
    bounds = [min(ck * per_chunk, f) for ck in range(FFN_CHUNKS + 1)]

    def body(dout_ref, x_ref, gate_ref, up_ref, g_ref, sg_ref, su_ref, sd_ref,
             dx_ref, dxb_ref, h_ref, act_ref, dgu_ref, dg_ref, wg_v, wu_v, wd_v, sems):
        _load_group([(sg_ref, firsts[0], nf, wg_v), (su_ref, firsts[1], nf, wu_v), (sd_ref, firsts[2], nf, wd_v)],
                    sems)

        @pl.when(pl.program_id(0) == 0)
        def _():
            dg_ref[...] = jnp.zeros_like(dg_ref)

        xv = x_ref[...]
        gv = g_ref[...]
        hv, xhat, r = _rms_fwd(xv, gv)
        h_ref[...] = hv.astype(BF16)
        dov = dout_ref[...]
        dob = dov.astype(BF16)
        dh = None
        for ck in range(FFN_CHUNKS):
            cols = slice(bounds[ck], bounds[ck + 1])
            gate_v = gate_ref[:, cols].astype(F32)
            up_v = up_ref[:, cols].astype(F32)
            sig = _sigmoid(gate_v)
            silu = gate_v * sig
            act_ref[:, cols] = (silu * up_v).astype(BF16)
            dact = _nt(dob, wd_v[cols, :])
            dup = (dact * silu).astype(BF16)
            dgate = (dact * up_v * (sig * (1.0 + gate_v * (1.0 - sig)))).astype(BF16)
            dgu_ref[:, cols] = dgate
            dgu_ref[:, f + bounds[ck]:f + bounds[ck + 1]] = dup
            part = _nn(dgate, wg_v[cols, :]) + _nn(dup, wu_v[cols, :])
            dh = part if dh is None else dh + part
        dxr, dg_row = _rms_bwd(dh, xhat, r, gv)
        dg_ref[...] += dg_row
        dx = dov + dxr
        dx_ref[...] = dx
        dxb_ref[...] = dx.astype(BF16)

    return _hosting_call(
        body, name, t_tok // tm, [dout, x, gate, up, g] + [arr for arr, _ in srcs],
        in_specs=[_row_spec(tm, d), _row_spec(tm, d), _row_spec(tm, f), _row_spec(tm, f), _const_spec((1, d)),
                  ANY, ANY, ANY],
        out_specs=[_row_spec(tm, d), _row_spec(tm, d), _row_spec(tm, d), _row_spec(tm, f), _row_spec(tm, 2 * f),
                   _const_spec((1, d))],
        out_shape=[jax.ShapeDtypeStruct((t_tok, d), F32), jax.ShapeDtypeStruct((t_tok, d), BF16),
                   jax.ShapeDtypeStruct((t_tok, d), BF16), jax.ShapeDtypeStruct((t_tok, f), BF16),
                   jax.ShapeDtypeStruct((t_tok, 2 * f), BF16), jax.ShapeDtypeStruct((1, d), F32)],
        scratch=[pltpu.VMEM((f, d), BF16), pltpu.VMEM((f, d), BF16), pltpu.VMEM((f, d), BF16),
                 pltpu.SemaphoreType.DMA((3 * N_DEV,))],
        hosted=hosted)


def _shift_down(z, k, prev_rows):
    row = lax.broadcasted_iota(jnp.int32, z.shape, 0)
    out = pltpu.roll(z, k, 0)
    for j in range(k):
        out = jnp.where(row == j, prev_rows[j], out)
    return out


def _shift_up(z, k, next_rows):
    tm = z.shape[0]
    row = lax.broadcasted_iota(jnp.int32, z.shape, 0)
    out = pltpu.roll(z, tm - k, 0)
    for j in range(k):
        out = jnp.where(row == tm - k + j, next_rows[j], out)
    return out


def _mixer_b_fwd(x, g, gath, conv_w, tm, seq, hosted=()):
    t_tok, d = x.shape
    e = conv_w.shape[1]
    e3 = 3 * e
    n_in, n_out = e3 // N_DEV, e // N_DEV
    tiles_per_seq = seq // tm

    def body(x_ref, g_ref, cw_ref, gath_ref, xo_ref, p_ref, win_v, wout_v, tail_v, sems):
        i = pl.program_id(0)
        _load_group([(gath_ref, 0, n_in, win_v), (gath_ref, n_in, n_out, wout_v)], sems)

        @pl.when(i % tiles_per_seq == 0)
        def _():
            tail_v[...] = jnp.zeros_like(tail_v)

        xv = x_ref[...]
        h = _rms_fwd(xv, g_ref[...])[0].astype(BF16)
        p = _nt(h, win_v[...])
        p_ref[...] = p.astype(BF16)
        z = p[:, e:2 * e] * p[:, 2 * e:]
        prev = [tail_v[SUBLANES - 2:SUBLANES - 1, :], tail_v[SUBLANES - 1:SUBLANES, :]]
        conv = (cw_ref[2:3, :] * z + cw_ref[1:2, :] * _shift_down(z, 1, prev[1:])
                + cw_ref[0:1, :] * _shift_down(z, 2, prev))
        tail_v[...] = z[tm - SUBLANES:, :]
        y = (p[:, :e] * conv).astype(BF16)
        xo_ref[...] = xv + _nn(y, wout_v[...])

    return _hosting_call(
        body, "mixer_b_fwd", t_tok // tm, [x, g, conv_w, gath],
        in_specs=[_row_spec(tm, d), _const_spec((1, d)), _const_spec((SUBLANES, e)), ANY],
        out_specs=[_row_spec(tm, d), _row_spec(tm, e3)],
        out_shape=[jax.ShapeDtypeStruct((t_tok, d), F32), jax.ShapeDtypeStruct((t_tok, e3), BF16)],
        scratch=[pltpu.VMEM((e3, d), BF16), pltpu.VMEM((e, d), BF16), pltpu.VMEM((SUBLANES, e), F32),
                 pltpu.SemaphoreType.DMA((2 * N_DEV,))],
        hosted=hosted)


def _mixer_b_bwd(dout, x, p, g, gath, conv_w, tm, seq, hosted=()):
    t_tok, d = x.shape
    e = conv_w.shape[1]
    e3 = 3 * e
    n_in, n_out = e3 // N_DEV, e // N_DEV
    tiles_per_seq = seq // tm
    halo_per_tile = tm // HALO
    n_halo = t_tok // HALO

    def body(dout_ref, dnext_ref, x_ref, p_ref, pprev_ref, pnext_ref, g_ref, cw_ref, gath_ref,
             dx_ref, dxb_ref, h_ref, y_ref, dp_ref, dg_ref, dcw_ref, win_v, wout_v, sems):
        i = pl.program_id(0)
        _load_group([(gath_ref, 0, n_in, win_v), (gath_ref, n_in, n_out, wout_v)], sems)

        @pl.when(i == 0)
        def _():
            dg_ref[...] = jnp.zeros_like(dg_ref)
            dcw_ref[...] = jnp.zeros_like(dcw_ref)

        first = (i % tiles_per_seq == 0).astype(F32)
        last = (i % tiles_per_seq == tiles_per_seq - 1).astype(F32)
        xv = x_ref[...]
        gv = g_ref[...]
        hv, xhat, r = _rms_fwd(xv, gv)
        h_ref[...] = hv.astype(BF16)
        pv = p_ref[...].astype(F32)
        bg, cg, hx = pv[:, :e], pv[:, e:2 * e], pv[:, 2 * e:]
        z = cg * hx
        pprev = pprev_ref[...].astype(F32)
        zprev = pprev[:, e:2 * e] * pprev[:, 2 * e:] * (1.0 - first)
        prev = [zprev[HALO - 2:HALO - 1, :], zprev[HALO - 1:HALO, :]]
        zs1 = _shift_down(z, 1, prev[1:])
        zs2 = _shift_down(z, 2, prev)
        w0, w1, w2 = cw_ref[0:1, :], cw_ref[1:2, :], cw_ref[2:3, :]
        conv = w2 * z + w1 * zs1 + w0 * zs2
        y_ref[...] = (bg * conv).astype(BF16)

        dov = dout_ref[...]
        wout_bf = wout_v[...]
        dy = _nt(dov.astype(BF16), wout_bf)
        dconv = dy * bg
        dnext = _nt(dnext_ref[...].astype(BF16), wout_bf) * pnext_ref[:, :e].astype(F32) * (1.0 - last)
        nxt = [dnext[0:1, :], dnext[1:2, :]]
        dz = w2 * dconv + w1 * _shift_up(dconv, 1, nxt[:1]) + w0 * _shift_up(dconv, 2, nxt)
        dcw_ref[0:1, :] += _col_sum(dconv * zs2)
        dcw_ref[1:2, :] += _col_sum(dconv * zs1)
        dcw_ref[2:3, :] += _col_sum(dconv * z)
        dp = jnp.concatenate([dy * conv, dz * hx, dz * cg], axis=1).astype(BF16)
        dp_ref[...] = dp
        dh = _nn(dp, win_v[...])
        dxr, dg_row = _rms_bwd(dh, xhat, r, gv)
        dg_ref[...] += dg_row
        dx = dov + dxr
        dx_ref[...] = dx
        dxb_ref[...] = dx.astype(BF16)

    prev_spec = lambda w: pl.BlockSpec((HALO, w), lambda i: (jnp.maximum(i * halo_per_tile - 1, 0), 0))
    next_spec = lambda w: pl.BlockSpec((HALO, w), lambda i: (jnp.minimum((i + 1) * halo_per_tile, n_halo - 1), 0))
    return _hosting_call(
        body, "mixer_b_bwd", t_tok // tm, [dout, dout, x, p, p, p, g, conv_w, gath],
        in_specs=[_row_spec(tm, d), next_spec(d), _row_spec(tm, d), _row_spec(tm, e3), prev_spec(e3), next_spec(e3),
                  _const_spec((1, d)), _const_spec((SUBLANES, e)), ANY],
        out_specs=[_row_spec(tm, d), _row_spec(tm, d), _row_spec(tm, d), _row_spec(tm, e), _row_spec(tm, e3),
                   _const_spec((1, d)), _const_spec((SUBLANES, e))],
        out_shape=[jax.ShapeDtypeStruct((t_tok, d), F32), jax.ShapeDtypeStruct((t_tok, d), BF16),
                   jax.ShapeDtypeStruct((t_tok, d), BF16), jax.ShapeDtypeStruct((t_tok, e), BF16),
                   jax.ShapeDtypeStruct((t_tok, e3), BF16), jax.ShapeDtypeStruct((1, d), F32),
                   jax.ShapeDtypeStruct((SUBLANES, e), F32)],
        scratch=[pltpu.VMEM((e3, d), BF16), pltpu.VMEM((e, d), BF16), pltpu.SemaphoreType.DMA((2 * N_DEV,))],
        hosted=hosted)


def _loss_head(x, target, g, tm):
    t_tok, d = x.shape

    def body(x_ref, t_ref, g_ref, loss_ref, dx_ref, dxb_ref, dg_ref):
        @pl.when(pl.program_id(0) == 0)
        def _():
            loss_ref[...] = jnp.zeros_like(loss_ref)
            dg_ref[...] = jnp.zeros_like(dg_ref)

        gv = g_ref[...]
        y, xhat, r = _rms_fwd(x_ref[...], gv)
        err = y - t_ref[...]
        loss_ref[...] += 0.5 * jnp.sum(_row_mean(err * err), axis=0, keepdims=True)
        dxr, dg_row = _rms_bwd(err * (1.0 / d), xhat, r, gv)
        dg_ref[...] += dg_row
        dx_ref[...] = dxr
        dxb_ref[...] = dxr.astype(BF16)

    return pl.pallas_call(
        body, name="loss_head", grid=(t_tok // tm,),
        in_specs=[_row_spec(tm, d), _row_spec(tm, d), _const_spec((1, d))],
        out_specs=[_const_spec((1, 1)), _row_spec(tm, d), _row_spec(tm, d), _const_spec((1, d))],
        out_shape=[jax.ShapeDtypeStruct((1, 1), F32), jax.ShapeDtypeStruct((t_tok, d), F32),
                   jax.ShapeDtypeStruct((t_tok, d), BF16), jax.ShapeDtypeStruct((1, d), F32)],
        compiler_params=_params(),
    )(x, target, g)


def _wgrad(a, b, bm, name, hosted=()):
    t_tok, m = a.shape
    n = b.shape[1]

    def body(a_ref, b_ref, o_ref):
        o_ref[...] = _tn(a_ref[...], b_ref[...]).astype(o_ref.dtype)

    outs, h_outs = _hosting_call(
        body, name, m // bm, [a, b],
        in_specs=[pl.BlockSpec((t_tok, bm), lambda i: (0, i)), _const_spec((t_tok, n))],
        out_specs=[pl.BlockSpec((bm, n), lambda i: (i, 0))],
        out_shape=[jax.ShapeDtypeStruct((m, n), BF16)],
        scratch=[], hosted=hosted)
    return (outs[0], h_outs) if hosted else outs[0]


def _sum_slots(land, rb, name):
    n_slots, rows, cols = land.shape

    def body(l_ref, o_ref):
        acc = l_ref[0].astype(F32)
        for k in range(1, n_slots):
            acc = acc + l_ref[k].astype(F32)
        o_ref[...] = acc

    return pl.pallas_call(
        body, name=name, grid=(rows // rb,),
        in_specs=[pl.BlockSpec((n_slots, rb, cols), lambda i: (0, i, 0))],
        out_specs=pl.BlockSpec((rb, cols), lambda i: (i, 0)),
        out_shape=jax.ShapeDtypeStruct((rows, cols), F32),
        compiler_params=_params(sequential=False),
    )(land)


def _adamw(w, grad, m, v, rb, name):
    rows, cols = w.shape
    c1 = 1.0 / (1.0 - ADAM_B1 ** ADAM_STEP)
    c2 = 1.0 / (1.0 - ADAM_B2 ** ADAM_STEP)

    def body(w_ref, g_ref, m_ref, v_ref, d_ref, mo_ref, vo_ref):
        gv = g_ref[...]
        mn = ADAM_B1 * m_ref[...] + (1.0 - ADAM_B1) * gv
        vn = ADAM_B2 * v_ref[...] + (1.0 - ADAM_B2) * (gv * gv)
        mo_ref[...] = mn
        vo_ref[...] = vn
        d_ref[...] = -ADAM_LR * ((mn * c1) / (jnp.sqrt(vn * c2) + ADAM_EPS) + ADAM_WD * w_ref[...])

    spec = pl.BlockSpec((rb, cols), lambda i: (i, 0))
    shape = jax.ShapeDtypeStruct((rows, cols), F32)
    return pl.pallas_call(
        body, name=name, grid=(rows // rb,),
        in_specs=[spec] * 4, out_specs=[spec] * 3, out_shape=[shape] * 3,
        compiler_params=_params(sequential=False),
    )(w, grad, m, v)


def _pack_small(parts, rows):
    flat = jnp.concatenate([p.reshape(-1).astype(F32) for p in parts])
    return jnp.pad(flat, (0, rows * LANES - flat.shape[0])).reshape(rows, LANES)


def _unpack_small(packed, shapes):
    flat = packed.reshape(-1)
    out = []
    pos = 0
    for s in shapes:
        n = math.prod(s)
        out.append(flat[pos:pos + n].reshape(s))
        pos += n
    return out


def kernel(x, mix_norm, ffn_norm, a_w_in, a_v_gain, a_v_bias, a_w_s, a_b_s, a_w_out, b_w_in, b_conv_w, b_w_out, ffn_w_gate, ffn_w_up, ffn_w_down, final_norm, loss_target, m_mix_norm, m_ffn_norm, m_a_w_in, m_a_v_gain, m_a_v_bias, m_a_w_s, m_a_b_s, m_a_w_out, m_b_w_in, m_b_conv_w, m_b_w_out, m_ffn_w_gate, m_ffn_w_up, m_ffn_w_down, m_final_norm, v_mix_norm, v_ffn_norm, v_a_w_in, v_a_v_gain, v_a_v_bias, v_a_w_s, v_a_b_s, v_a_w_out, v_b_w_in, v_b_conv_w, v_b_w_out, v_ffn_w_gate, v_ffn_w_up, v_ffn_w_down, v_final_norm):
    bsz, seq, d = x.shape
    t_tok = bsz * seq
    me = _my_index()
    xt = x.reshape(t_tok, d)
    target = loss_target.reshape(t_tok, d)
    e_a = a_v_gain.shape[1]
    e_b = b_w_out.shape[1] * N_DEV
    n_layers = ffn_w_gate.shape[0]
    f_shard = ffn_w_gate.shape[2]
    f_full = f_shard * N_DEV

    conv_pad = jnp.pad(b_conv_w[0], ((0, SUBLANES - CONV_W), (0, 0)))
    sh_a = jnp.concatenate([a_w_in[0].T, a_w_out[0]]).astype(BF16)
    sh_b = jnp.concatenate([b_w_in[0].T, b_w_out[0]]).astype(BF16)
    sh_f0 = jnp.concatenate([ffn_w_gate[0].T, ffn_w_up[0].T, ffn_w_down[0]]).astype(BF16)
    sh_f1g = ffn_w_gate[1].T.astype(BF16)
    sh_f1ud = jnp.concatenate([ffn_w_up[1].T, ffn_w_down[1]]).astype(BF16)
    bfull = jnp.repeat(a_b_s[0].T, GROUP, axis=1)

    gath_a, conv_g = _exchange([_HostedGathers([sh_a, conv_pad], 0)], "gather_a")
    conv_full = jnp.pad(conv_g[:, :CONV_W, :].transpose(1, 0, 2).reshape(CONV_W, e_b), ((0, SUBLANES - CONV_W), (0, 0)))
    (x1, zpre, y_a), (gath_f0,) = _mixer_a_fwd(xt, mix_norm[0:1], gath_a, a_v_gain, a_v_bias, a_w_s[0], bfull, tm=256,
                                               hosted=[_HostedGathers([sh_f0], mid_lead=2)])
    srcs0 = [(gath_f0, 0), (gath_f0, f_shard), (gath_f0, 2 * f_shard)]
    (x2, gate0, up0), (gath_b, gath_f1g) = _ffn_fwd(x1, ffn_norm[0:1], srcs0, f_shard, tm=256, name="ffn_fwd0",
                                                    hosted=[_HostedGathers([sh_b, sh_f1g], mid_lead=2)])
    (x3, p_b), (gath_f1ud,) = _mixer_b_fwd(x2, mix_norm[1:2], gath_b, conv_full, tm=256, seq=seq,
                                           hosted=[_HostedGathers([sh_f1ud], mid_lead=2)])
    srcs1 = [(gath_f1g, 0), (gath_f1ud, 0), (gath_f1ud, f_shard)]
    (x4, gate1, up1), _ = _ffn_fwd(x3, ffn_norm[1:2], srcs1, f_shard, tm=256, name="ffn_fwd1")
    loss_part, dx4, dx4_bf, d_final = _loss_head(x4, target, final_norm.reshape(1, d), tm=512)

    ffn_entries = [(0, 0, f_shard), (0, f_full, f_shard), (1, 0, f_shard)]
    (dx3, dx3_bf, h_f1, act1, dgu1, d_fn1), _ = _ffn_bwd(dx4, x3, gate1, up1, ffn_norm[1:2], srcs1, f_shard, tm=256,
                                                         name="ffn_bwd1")
    g_down1 = _wgrad(act1, dx4_bf, 256, "wgrad_down1")
    g_gu1 = _wgrad(dgu1, h_f1, 512, "wgrad_gate_up1")
    ps_f1 = _pair_reduce([g_gu1, g_down1], ffn_entries, "pair_reduce_f1")
    (dx2, dx2_bf, h_b, y_b, dp_b, d_mn1, d_conv), (land_f1gu,) = _mixer_b_bwd(
        dx3, x2, p_b, mix_norm[1:2], gath_b, conv_full, tm=256, seq=seq,
        hosted=[_HostedChipScatter(ps_f1, 0, 2 * f_shard)])
    g_b_out = _wgrad(y_b, dx3_bf, 256, "wgrad_b_out")
    g_b_in = _wgrad(dp_b, h_b, 512, "wgrad_b_in")
    ps_b = _pair_reduce([g_b_in, g_b_out], [(0, 0, b_w_in.shape[2]), (1, 0, b_w_out.shape[1])], "pair_reduce_b")
    (dx1, dx1_bf, h_f0, act0, dgu0, d_fn0), (land_f1d, land_b) = _ffn_bwd(
        dx2, x1, gate0, up0, ffn_norm[0:1], srcs0, f_shard, tm=256, name="ffn_bwd0",
        hosted=[_HostedChipScatter(ps_f1, 2 * f_shard, f_shard), _HostedChipScatter(ps_b)])
    g_down0 = _wgrad(act0, dx2_bf, 256, "wgrad_down0")
    g_gu0 = _wgrad(dgu0, h_f0, 512, "wgrad_gate_up0")
    ps_f0 = _pair_reduce([g_gu0, g_down0], ffn_entries, "pair_reduce_f0")
    g_a_out = _wgrad(y_a, dx1_bf, 256, "wgrad_a_out")
    ps_ao = _pair_reduce([g_a_out], [(0, 0, a_w_out.shape[1])], "pair_reduce_a_out")
    (dx0, _, h_a, dz_a, d_mn0, d_gain, d_bias, d_ws, d_bs_acc), (land_f0, land_ao) = _mixer_a_bwd(
        dx1, xt, zpre, mix_norm[0:1], gath_a, a_v_gain, a_v_bias, a_w_s[0], bfull, tm=256,
        hosted=[_HostedChipScatter(ps_f0), _HostedChipScatter(ps_ao)])
    d_bs = d_bs_acc.reshape(CHUNK, HEADS, GROUP).sum(axis=2).T

    small_grads = [jnp.concatenate([d_mn0, d_mn1]), jnp.concatenate([d_fn0, d_fn1]), d_gain, d_bias, d_ws, d_bs,
                   d_final, d_conv[:CONV_W]]
    small_shapes = [(n_layers, d), (n_layers, d), (1, e_a), (1, e_a), (1, HEADS, CHUNK, CHUNK), (1, HEADS, CHUNK), (d,),
                    (CONV_W, e_b)]
    n_small = sum(math.prod(s) for s in small_shapes)
    blk_rows = -(-n_small // (N_DEV * LANES * SUBLANES)) * SUBLANES
    small_rows = blk_rows * N_DEV
    packed = _pack_small(small_grads, small_rows)
    g_a_in, (small_land,) = _wgrad(dz_a, h_a, 512, "wgrad_a_in", hosted=[_HostedScatterAll(packed)])
    ps_ai = _pair_reduce([g_a_in], [(0, 0, a_w_in.shape[2])], "pair_reduce_a_in")
    small_sum = _sum_slots(small_land, blk_rows, "sum_small")
    land_ai, small_gath = _exchange([_HostedChipScatter(ps_ai), _HostedGathers([small_sum], 0)], "tail_exchange")
    small_all = small_gath.reshape(small_rows, LANES)

    def summed(land, name):
        return _sum_slots(land, land.shape[1] // 2, name)

    gs_ai, gs_ao, gs_b = summed(land_ai, "sum_a_in"), summed(land_ao, "sum_a_out"), summed(land_b, "sum_b")
    gs_f0, gs_f1gu, gs_f1d = summed(land_f0, "sum_f0"), summed(land_f1gu, "sum_f1_gate_up"), summed(land_f1d, "sum_f1_down")

    (gr_mix, gr_ffn, gr_gain, gr_bias, gr_ws, gr_bs, gr_final, gr_conv_full) = _unpack_small(small_all, small_shapes)
    gr_conv = lax.dynamic_slice_in_dim(gr_conv_full, me * (e_b // N_DEV), e_b // N_DEV, axis=1)[None]

    n_b_in = b_w_in.shape[2]
    gr_a_in = gs_ai.T[None]
    gr_a_out = gs_ao[None]
    gr_b_in = gs_b[:n_b_in].T[None]
    gr_b_out = gs_b[n_b_in:][None]
    gr_gate = jnp.stack([gs_f0[:f_shard].T, gs_f1gu[:f_shard].T])
    gr_up = jnp.stack([gs_f0[f_shard:2 * f_shard].T, gs_f1gu[f_shard:].T])
    gr_down = jnp.stack([gs_f0[2 * f_shard:], gs_f1d])

    def big(w, g, m, v, name):
        shape = w.shape
        two_d = (math.prod(shape[:-1]), shape[-1])
        rb = two_d[0] // 2
        out = _adamw(w.reshape(two_d), g.reshape(two_d), m.reshape(two_d), v.reshape(two_d), rb, name)
        return [o.reshape(shape) for o in out]

    small_w = [mix_norm, ffn_norm, a_v_gain, a_v_bias, a_w_s, a_b_s, final_norm]
    small_m = [m_mix_norm, m_ffn_norm, m_a_v_gain, m_a_v_bias, m_a_w_s, m_a_b_s, m_final_norm]
    small_v = [v_mix_norm, v_ffn_norm, v_a_v_gain, v_a_v_bias, v_a_w_s, v_a_b_s, v_final_norm]
    small_g = [gr_mix, gr_ffn, gr_gain, gr_bias, gr_ws, gr_bs, gr_final]
    sm_shapes = small_shapes[:-1]
    sm_out = _adamw(_pack_small(small_w, small_rows), _pack_small(small_g, small_rows), _pack_small(small_m, small_rows),
                    _pack_small(small_v, small_rows), small_rows, "adamw_small")
    sm_delta, sm_m, sm_v = [_unpack_small(o, sm_shapes) for o in sm_out]

    conv_out = _adamw(b_conv_w[0], gr_conv[0], m_b_conv_w[0], v_b_conv_w[0], CONV_W, "adamw_conv")
    conv_delta, conv_m, conv_v = [o[None] for o in conv_out]

    res = {
        "a_w_in": big(a_w_in, gr_a_in, m_a_w_in, v_a_w_in, "adamw_a_in"),
        "a_w_out": big(a_w_out, gr_a_out, m_a_w_out, v_a_w_out, "adamw_a_out"),
        "b_w_in": big(b_w_in, gr_b_in, m_b_w_in, v_b_w_in, "adamw_b_in"),
        "b_w_out": big(b_w_out, gr_b_out, m_b_w_out, v_b_w_out, "adamw_b_out"),
        "ffn_w_gate": big(ffn_w_gate, gr_gate, m_ffn_w_gate, v_ffn_w_gate, "adamw_gate"),
        "ffn_w_up": big(ffn_w_up, gr_up, m_ffn_w_up, v_ffn_w_up, "adamw_up"),
        "ffn_w_down": big(ffn_w_down, gr_down, m_ffn_w_down, v_ffn_w_down, "adamw_down"),
    }

    order = ["mix_norm", "ffn_norm", "a_w_in", "a_v_gain", "a_v_bias", "a_w_s", "a_b_s", "a_w_out", "b_w_in",
             "b_conv_w", "b_w_out", "ffn_w_gate", "ffn_w_up", "ffn_w_down", "final_norm"]
    small_names = ["mix_norm", "ffn_norm", "a_v_gain", "a_v_bias", "a_w_s", "a_b_s", "final_norm"]
    grads = {"a_w_in": gr_a_in, "a_w_out": gr_a_out, "b_w_in": gr_b_in, "b_w_out": gr_b_out, "ffn_w_gate": gr_gate,
             "ffn_w_up": gr_up, "ffn_w_down": gr_down, "b_conv_w": gr_conv}
    deltas, new_m, new_v = {}, {}, {}
    for k, name in enumerate(small_names):
        grads[name] = small_g[k]
        deltas[name], new_m[name], new_v[name] = sm_delta[k], sm_m[k], sm_v[k]
    deltas["b_conv_w"], new_m["b_conv_w"], new_v["b_conv_w"] = conv_delta, conv_m, conv_v
    for name, (dl, mm, vv) in res.items():
        deltas[name], new_m[name], new_v[name] = dl, mm, vv

    loss = lax.psum(loss_part[0, 0], ("x", "y", "c"))
    grad_x = dx0.reshape(bsz, seq, d)
    return (loss, grad_x, *[grads[n] for n in order], *[deltas[n] for n in order],
            *[new_m[n] for n in order], *[new_v[n] for n in order])
```

```python
import math

import jax
import jax.numpy as jnp
from jax import lax
from jax.experimental import pallas as pl
from jax.experimental.pallas import tpu as pltpu

F32 = jnp.float32
BF16 = jnp.bfloat16

N_DEV = 8
N_CHIP = 4
CHUNK = 128
HEADS = 16
GROUP = 128
CONV_W = 3
NORM_EPS = 1e-6
GELU_C = math.sqrt(2.0 / math.pi)
GELU_K = 0.044715

ADAM_LR = 0.001
ADAM_B1 = 0.9
ADAM_B2 = 0.999
ADAM_EPS = 1e-08
ADAM_WD = 0.01
ADAM_STEP = 10

LANES = 128
SUBLANES = 8
VMEM_LIMIT = 60 * 1024 * 1024
HALO = 16
MXU_WIDTH = 256
FFN_CHUNKS = 2

MESH = pl.DeviceIdType.MESH
ANY = pl.BlockSpec(memory_space=pl.ANY)


def _params(sequential=True):
    return pltpu.CompilerParams(
        dimension_semantics=("arbitrary",) if sequential else None,
        vmem_limit_bytes=VMEM_LIMIT)


def _nn(a, b):
    return jnp.dot(a, b, preferred_element_type=F32)


def _nt(a, b):
    return lax.dot_general(a, b, (((1,), (1,)), ((), ())), preferred_element_type=F32)


def _tn(a, b):
    return lax.dot_general(a, b, (((0,), (0,)), ((), ())), preferred_element_type=F32)


def _row_mean(a):
    return jnp.mean(a, axis=-1, keepdims=True)


def _col_sum(a):
    return jnp.sum(a, axis=0, keepdims=True)


def _rms_fwd(x, g):
    r = lax.rsqrt(_row_mean(x * x) + NORM_EPS)
    xhat = x * r
    return xhat * g, xhat, r


def _rms_bwd(dh, xhat, r, g):
    a = dh * g
    dx = r * (a - xhat * _row_mean(a * xhat))
    return dx, _col_sum(dh * xhat)


def _gelu_and_grad(x):
    x2 = x * x
    t = jnp.tanh(GELU_C * (x + GELU_K * x2 * x))
    half = 0.5 * (1.0 + t)
    d = half + 0.5 * x * (1.0 - t * t) * (GELU_C * (1.0 + 3.0 * GELU_K * x2))
    return x * half, d


def _sigmoid(x):
    return 1.0 / (1.0 + jnp.exp(-x))


def _row_spec(tm, width):
    return pl.BlockSpec((tm, width), lambda i: (i, 0))


def _const_spec(shape):
    nd = len(shape)
    return pl.BlockSpec(shape, lambda i: (0,) * nd)


def _load_group(parts, sems):
    @pl.when(pl.program_id(0) == 0)
    def _():
        copies = []
        for k, (gath_ref, first, n, dst) in enumerate(parts):
            for j in range(N_DEV):
                copies.append(pltpu.make_async_copy(gath_ref.at[j, pl.ds(first, n), :], dst.at[pl.ds(j * n, n), :],
                                                    sems.at[k * N_DEV + j]))
        for cp in copies:
            cp.start()
        for cp in copies:
            cp.wait()


def _hosting_call(body, name, n_steps, arrays, in_specs, out_specs, out_shape, scratch, hosted=()):
    n_in, n_out, n_scr = len(arrays), len(out_shape), len(scratch)
    h_arrays = [a for h in hosted for a in h.arrays]
    h_shapes = [s for h in hosted for s in h.out_shapes]
    h_sems = [s for h in hosted for s in h.sem_shapes]

    def full_body(*refs):
        pos = 0
        groups = []
        for n in (n_in, len(h_arrays), n_out, len(h_shapes), n_scr, len(h_sems)):
            groups.append(refs[pos:pos + n])
            pos += n
        own_in, h_in, own_out, h_out, own_scr, h_sem = groups
        per_host = []
        pi = po = ps = 0
        for h in hosted:
            ni, no, ns = len(h.arrays), len(h.out_shapes), len(h.sem_shapes)
            per_host.append((h, h_in[pi:pi + ni], h_out[po:po + no], h_sem[ps:ps + ns]))
            pi, po, ps = pi + ni, po + no, ps + ns
        for h, ins, outs, sems in per_host:
            h.begin(ins, outs, sems, n_steps)
        body(*own_in, *own_out, *own_scr)
        for h, ins, outs, sems in per_host:
            h.end(ins, outs, sems, n_steps)

    outs = pl.pallas_call(
        full_body, name=name, grid=(n_steps,),
        in_specs=list(in_specs) + [ANY] * len(h_arrays),
        out_specs=list(out_specs) + [ANY] * len(h_shapes),
        out_shape=list(out_shape) + h_shapes,
        scratch_shapes=list(scratch) + h_sems,
        compiler_params=_params(),
    )(*arrays, *h_arrays)
    return outs[:n_out], outs[n_out:]


def _my_index():
    return 4 * lax.axis_index("x") + 2 * lax.axis_index("y") + lax.axis_index("c")


class _Gather:
    def __init__(self, shard, out, send_sems, recv_sems, local_sem):
        self.shard, self.out = shard, out
        self.send_sems, self.recv_sems, self.local_sem = send_sems, recv_sems, local_sem
        x, y, c = lax.axis_index("x"), lax.axis_index("y"), lax.axis_index("c")
        self.c = c
        self.me, self.sibling = (x, y, c), (x, y, 1 - c)
        self.chips = [(1 - x, y), (x, 1 - y), (1 - x, 1 - y)]

    def _slot(self, dev):
        return self.out.at[4 * dev[0] + 2 * dev[1] + dev[2]]

    def _copy(self, k, block, to, src=None):
        return pltpu.make_async_remote_copy(
            src_ref=self._slot(block) if src is None else src, dst_ref=self._slot(block),
            send_sem=self.send_sems.at[k], recv_sem=self.recv_sems.at[k], device_id=to, device_id_type=MESH)

    def _local(self):
        return pltpu.make_async_copy(self.shard, self._slot(self.me), self.local_sem)

    def start(self):
        self._local().start()
        self._copy(0, self.me, self.sibling, src=self.shard).start()
        for j, chip in enumerate(self.chips):
            self._copy(1 + j, self.me, (*chip, self.c), src=self.shard).start()

    def forward(self):
        for j, chip in enumerate(self.chips):
            self._copy(1 + j, (*chip, self.c), self.me).wait_recv()
            self._copy(4 + j, (*chip, self.c), self.sibling).start()

    def finish(self):
        self._copy(0, self.sibling, self.me).wait_recv()
        for j, chip in enumerate(self.chips):
            self._copy(4 + j, (*chip, 1 - self.c), self.me).wait_recv()
        for k in range(N_DEV - 1):
            self._copy(k, self.me, self.sibling).wait_send()
        self._local().wait()


class _HostedGathers:
    def __init__(self, shards, mid_lead):
        n = len(shards)
        self.arrays = shards
        self.mid_lead = mid_lead
        self.out_shapes = [jax.ShapeDtypeStruct((N_DEV,) + s.shape, s.dtype) for s in shards]
        self.sem_shapes = [pltpu.SemaphoreType.DMA((n, N_DEV - 1)), pltpu.SemaphoreType.DMA((n, N_DEV - 1)),
                           pltpu.SemaphoreType.DMA((n,))]

    def _gathers(self, ins, outs, sems):
        return [_Gather(ins[a], outs[a], sems[0].at[a], sems[1].at[a], sems[2].at[a]) for a in range(len(ins))]

    def begin(self, ins, outs, sems, n_steps):
        i = pl.program_id(0)

        @pl.when(i == 0)
        def _():
            for g in self._gathers(ins, outs, sems):
                g.start()

        @pl.when(i == max(n_steps - 1 - self.mid_lead, 0))
        def _():
            for g in self._gathers(ins, outs, sems):
                g.forward()

    def end(self, ins, outs, sems, n_steps):
        @pl.when(pl.program_id(0) == n_steps - 1)
        def _():
            for g in self._gathers(ins, outs, sems):
                g.finish()


def _exchange(hosted, name):
    return _hosting_call(lambda: None, name, 1, [], [], [], [], [], hosted=hosted)[1]


class _ChipScatter:
    def __init__(self, pairsum, row0, land, send_sems, recv_sems, local_sem):
        self.pairsum, self.row0, self.land = pairsum, row0, land
        self.send_sems, self.recv_sems, self.local_sem = send_sems, recv_sems, local_sem
        x, y, c = lax.axis_index("x"), lax.axis_index("y"), lax.axis_index("c")
        self.c = c
        self.chip = 2 * x + y
        self.others = [(1 - x, y), (x, 1 - y), (1 - x, 1 - y)]

    def _src(self, chip):
        return self.pairsum.at[chip, pl.ds(self.row0, self.land.shape[1]), :]

    def _copy(self, k):
        ox, oy = self.others[k]
        return pltpu.make_async_remote_copy(
            src_ref=self._src(2 * ox + oy), dst_ref=self.land.at[self.chip],
            send_sem=self.send_sems.at[k], recv_sem=self.recv_sems.at[k], device_id=(ox, oy, self.c),
            device_id_type=MESH)

    def _arrival(self, k):
        ox, oy = self.others[k]
        return pltpu.make_async_remote_copy(
            src_ref=self._src(self.chip), dst_ref=self.land.at[2 * ox + oy],
            send_sem=self.send_sems.at[k], recv_sem=self.recv_sems.at[k], device_id=(ox, oy, self.c),
            device_id_type=MESH)

    def _local(self):
        return pltpu.make_async_copy(self._src(self.chip), self.land.at[self.chip], self.local_sem)

    def start(self):
        self._local().start()
        for k in range(N_CHIP - 1):
            self._copy(k).start()

    def finish(self):
        for k in range(N_CHIP - 1):
            self._arrival(k).wait_recv()
        for k in range(N_CHIP - 1):
            self._copy(k).wait_send()
        self._local().wait()


class _HostedChipScatter:
    def __init__(self, pairsum, row0=0, n=None):
        n = pairsum.shape[1] - row0 if n is None else n
        self.row0 = row0
        self.arrays = [pairsum]
        self.out_shapes = [jax.ShapeDtypeStruct((N_CHIP, n, pairsum.shape[2]), pairsum.dtype)]
        self.sem_shapes = [pltpu.SemaphoreType.DMA((N_CHIP - 1,)), pltpu.SemaphoreType.DMA((N_CHIP - 1,)),
                           pltpu.SemaphoreType.DMA(())]

    def begin(self, ins, outs, sems, n_steps):
        @pl.when(pl.program_id(0) == 0)
        def _():
            _ChipScatter(ins[0], self.row0, outs[0], *sems).start()

    def end(self, ins, outs, sems, n_steps):
        @pl.when(pl.program_id(0) == n_steps - 1)
        def _():
            _ChipScatter(ins[0], self.row0, outs[0], *sems).finish()


def _pair_reduce(arrays, entries, name):
    n_arr, n_ent = len(arrays), len(entries)
    cols = arrays[0].shape[1]
    offsets = []
    total = 0
    for _, _, n in entries:
        offsets.append(total)
        total += n

    def body(*refs):
        ins, out_ref = refs[:n_arr], refs[n_arr]
        rbuf, own, send_sems, recv_sems, own_sems = refs[n_arr + 1:]
        q = pl.program_id(0)
        x, y, c = lax.axis_index("x"), lax.axis_index("y"), lax.axis_index("c")

        def block(e, chip, core):
            ai, first, n = entries[e]
            return ins[ai].at[pl.ds(first + (2 * chip + core) * n, n), :]

        def to_sibling(e, chip):
            return pltpu.make_async_remote_copy(
                src_ref=block(e, chip, 1 - c), dst_ref=rbuf.at[chip, pl.ds(offsets[e], entries[e][2]), :],
                send_sem=send_sems.at[e, chip], recv_sem=recv_sems.at[e, chip], device_id=(x, y, 1 - c),
                device_id_type=MESH)

        @pl.when(q == 0)
        def _():
            for chip in range(N_CHIP):
                for e in range(n_ent):
                    to_sibling(e, chip).start()

        loads = [pltpu.make_async_copy(block(e, q, c), own.at[pl.ds(offsets[e], entries[e][2]), :], own_sems.at[e])
                 for e in range(n_ent)]
        for cp in loads:
            cp.start()
        for cp in loads:
            cp.wait()
        for e in range(n_ent):
            to_sibling(e, q).wait_recv()
        out_ref[...] = (own[...].astype(F32) + rbuf[q].astype(F32)).astype(out_ref.dtype)

        @pl.when(q == N_CHIP - 1)
        def _():
            for chip in range(N_CHIP):
                for e in range(n_ent):
                    to_sibling(e, chip).wait_send()

    return pl.pallas_call(
        body, name=name, grid=(N_CHIP,),
        in_specs=[ANY] * n_arr,
        out_specs=pl.BlockSpec((None, total, cols), lambda q: (q, 0, 0)),
        out_shape=jax.ShapeDtypeStruct((N_CHIP, total, cols), BF16),
        scratch_shapes=[pltpu.VMEM((N_CHIP, total, cols), BF16), pltpu.VMEM((total, cols), BF16),
                        pltpu.SemaphoreType.DMA((n_ent, N_CHIP)), pltpu.SemaphoreType.DMA((n_ent, N_CHIP)),
                        pltpu.SemaphoreType.DMA((n_ent,))],
        compiler_params=_params(),
    )(*arrays)


class _HostedScatterAll:
    def __init__(self, packed):
        n = packed.shape[0] // N_DEV
        self.n = n
        self.arrays = [packed]
        self.out_shapes = [jax.ShapeDtypeStruct((N_DEV, n, packed.shape[1]), packed.dtype)]
        self.sem_shapes = [pltpu.SemaphoreType.DMA((N_DEV - 1,)), pltpu.SemaphoreType.DMA((N_DEV - 1,)),
                           pltpu.SemaphoreType.DMA(())]

    def _copies(self, ins, outs, sems, with_arrivals):
        src, land = ins[0], outs[0]
        send_sems, recv_sems, local_sem = sems
        me = _my_index()

        def block(p):
            return src.at[pl.ds(p * self.n, self.n), :]

        local = pltpu.make_async_copy(block(me), land.at[me], local_sem)
        sends, arrivals = [], []
        for k in range(1, N_DEV):
            p = (me + k) % N_DEV
            q = (me + N_DEV - k) % N_DEV
            sends.append(pltpu.make_async_remote_copy(
                src_ref=block(p), dst_ref=land.at[me], send_sem=send_sems.at[k - 1], recv_sem=recv_sems.at[k - 1],
                device_id=(p // 4, (p // 2) % 2, p % 2), device_id_type=MESH))
            if with_arrivals:
                arrivals.append(pltpu.make_async_remote_copy(
                    src_ref=block(me), dst_ref=land.at[q], send_sem=send_sems.at[k - 1], recv_sem=recv_sems.at[k - 1],
                    device_id=(q // 4, (q // 2) % 2, q % 2), device_id_type=MESH))
        return local, sends, arrivals

    def begin(self, ins, outs, sems, n_steps):
        @pl.when(pl.program_id(0) == 0)
        def _():
            local, sends, _ = self._copies(ins, outs, sems, with_arrivals=False)
            local.start()
            for cp in sends:
                cp.start()

    def end(self, ins, outs, sems, n_steps):
        @pl.when(pl.program_id(0) == n_steps - 1)
        def _():
            local, sends, arrivals = self._copies(ins, outs, sems, with_arrivals=True)
            for cp in arrivals:
                cp.wait_recv()
            for cp in sends:
                cp.wait_send()
            local.wait()


def _tril_weights(ws_ref):
    r = lax.broadcasted_iota(jnp.int32, (CHUNK, CHUNK), 0)
    c = lax.broadcasted_iota(jnp.int32, (CHUNK, CHUNK), 1)
    return [jnp.where(r >= c, ws_ref[h], 0.0).astype(BF16) for h in range(HEADS)]


def _sgu_stats(zpre, gain, bias):
    e = zpre.shape[1] // 2
    z, dz = _gelu_and_grad(zpre)
    u, v = z[:, :e], z[:, e:]
    vc = v - _row_mean(v)
    rstd = lax.rsqrt(_row_mean(vc * vc) + NORM_EPS)
    vhat = vc * rstd
    return u, vhat, rstd, vhat * gain + bias, dz


def _spatial_fwd(wt, vn_bf, bfull_ref, sv_ref, tm):
    for ci in range(tm // CHUNK):
        rows = slice(ci * CHUNK, (ci + 1) * CHUNK)
        for h in range(HEADS):
            cols = slice(h * GROUP, (h + 1) * GROUP)
            sv_ref[rows, cols] = _nn(wt[h], vn_bf[rows, cols]) + bfull_ref[:, cols]


def _mixer_a_fwd(x, g, gath, gain, bias, ws, bfull, tm, hosted=()):
    t_tok, d = x.shape
    e = gain.shape[1]
    e2 = 2 * e
    n_in, n_out = e2 // N_DEV, e // N_DEV

    def body(x_ref, g_ref, gain_ref, bias_ref, ws_ref, bfull_ref, gath_ref,
             xo_ref, zpre_ref, y_ref, win_v, wout_v, sv_v, sems):
        _load_group([(gath_ref, 0, n_in, win_v), (gath_ref, n_in, n_out, wout_v)], sems)
        xv = x_ref[...]
        h = _rms_fwd(xv, g_ref[...])[0].astype(BF16)
        zpre = _nt(h, win_v[...])
        zpre_ref[...] = zpre.astype(BF16)
        u, _, _, vn, _ = _sgu_stats(zpre, gain_ref[...], bias_ref[...])
        _spatial_fwd(_tril_weights(ws_ref), vn.astype(BF16), bfull_ref, sv_v, tm)
        y = (u * sv_v[...]).astype(BF16)
        y_ref[...] = y
        xo_ref[...] = xv + _nn(y, wout_v[...])

    return _hosting_call(
        body, "mixer_a_fwd", t_tok // tm, [x, g, gain, bias, ws, bfull, gath],
        in_specs=[_row_spec(tm, d), _const_spec((1, d)), _const_spec((1, e)), _const_spec((1, e)),
                  _const_spec((HEADS, CHUNK, CHUNK)), _const_spec((CHUNK, e)), ANY],
        out_specs=[_row_spec(tm, d), _row_spec(tm, e2), _row_spec(tm, e)],
        out_shape=[jax.ShapeDtypeStruct((t_tok, d), F32), jax.ShapeDtypeStruct((t_tok, e2), BF16),
                   jax.ShapeDtypeStruct((t_tok, e), BF16)],
        scratch=[pltpu.VMEM((e2, d), BF16), pltpu.VMEM((e, d), BF16), pltpu.VMEM((tm, e), F32),
                 pltpu.SemaphoreType.DMA((2 * N_DEV,))],
        hosted=hosted)


def _mixer_a_bwd(dout, x, zpre, g, gath, gain, bias, ws, bfull, tm, hosted=()):
    t_tok, d = x.shape
    e = gain.shape[1]
    e2 = 2 * e
    n_in, n_out = e2 // N_DEV, e // N_DEV
    n_steps = t_tok // tm

    def body(dout_ref, x_ref, zpre_ref, g_ref, gain_ref, bias_ref, ws_ref, bfull_ref, gath_ref,
             dx_ref, dxb_ref, h_ref, dz_ref, dg_ref, dgain_ref, dbias_ref, dws_ref, dbs_ref,
             win_v, wout_v, sv_v, dvn_v, sems):
        i = pl.program_id(0)
        _load_group([(gath_ref, 0, n_in, win_v), (gath_ref, n_in, n_out, wout_v)], sems)

        @pl.when(i == 0)
        def _():
            dg_ref[...] = jnp.zeros_like(dg_ref)
            dgain_ref[...] = jnp.zeros_like(dgain_ref)
            dbias_ref[...] = jnp.zeros_like(dbias_ref)
            dws_ref[...] = jnp.zeros_like(dws_ref)
            dbs_ref[...] = jnp.zeros_like(dbs_ref)

        xv = x_ref[...]
        gv = g_ref[...]
        hv, xhat, r = _rms_fwd(xv, gv)
        h_ref[...] = hv.astype(BF16)
        gain_v = gain_ref[...]
        u, vhat, rstd, vn, gelu_d = _sgu_stats(zpre_ref[...].astype(F32), gain_v, bias_ref[...])
        vn_bf = vn.astype(BF16)
        wt = _tril_weights(ws_ref)
        _spatial_fwd(wt, vn_bf, bfull_ref, sv_v, tm)
        sv = sv_v[...]

        dov = dout_ref[...]
        dy = _nt(dov.astype(BF16), wout_v[...])
        du = dy * sv
        dsv = dy * u
        dsv_bf = dsv.astype(BF16)
        for ci in range(tm // CHUNK):
            rows = slice(ci * CHUNK, (ci + 1) * CHUNK)
            dbs_ref[...] += dsv[rows, :]
            for h in range(HEADS):
                cols = slice(h * GROUP, (h + 1) * GROUP)
                dvn_v[rows, cols] = _tn(wt[h], dsv_bf[rows, cols])
                dws_ref[h] += _nt(dsv_bf[rows, cols], vn_bf[rows, cols])
        dvn = dvn_v[...]
        dgain_ref[...] += _col_sum(dvn * vhat)
        dbias_ref[...] += _col_sum(dvn)
        dvhat = dvn * gain_v
        dv = rstd * (dvhat - _row_mean(dvhat) - vhat * _row_mean(dvhat * vhat))
        dzpre = (jnp.concatenate([du, dv], axis=1) * gelu_d).astype(BF16)
        dz_ref[...] = dzpre
        dh = _nn(dzpre, win_v[...])
        dxr, dg_row = _rms_bwd(dh, xhat, r, gv)
        dg_ref[...] += dg_row
        dx = dov + dxr
        dx_ref[...] = dx
        dxb_ref[...] = dx.astype(BF16)

        @pl.when(i == n_steps - 1)
        def _():
            rr = lax.broadcasted_iota(jnp.int32, (CHUNK, CHUNK), 0)
            cc = lax.broadcasted_iota(jnp.int32, (CHUNK, CHUNK), 1)
            for h in range(HEADS):
                dws_ref[h] = jnp.where(rr >= cc, dws_ref[h], 0.0)

    return _hosting_call(
        body, "mixer_a_bwd", n_steps, [dout, x, zpre, g, gain, bias, ws, bfull, gath],
        in_specs=[_row_spec(tm, d), _row_spec(tm, d), _row_spec(tm, e2), _const_spec((1, d)),
                  _const_spec((1, e)), _const_spec((1, e)), _const_spec((HEADS, CHUNK, CHUNK)),
                  _const_spec((CHUNK, e)), ANY],
        out_specs=[_row_spec(tm, d), _row_spec(tm, d), _row_spec(tm, d), _row_spec(tm, e2),
                   _const_spec((1, d)), _const_spec((1, e)), _const_spec((1, e)),
                   _const_spec((HEADS, CHUNK, CHUNK)), _const_spec((CHUNK, e))],
        out_shape=[jax.ShapeDtypeStruct((t_tok, d), F32), jax.ShapeDtypeStruct((t_tok, d), BF16),
                   jax.ShapeDtypeStruct((t_tok, d), BF16), jax.ShapeDtypeStruct((t_tok, e2), BF16),
                   jax.ShapeDtypeStruct((1, d), F32), jax.ShapeDtypeStruct((1, e), F32),
                   jax.ShapeDtypeStruct((1, e), F32), jax.ShapeDtypeStruct((HEADS, CHUNK, CHUNK), F32),
                   jax.ShapeDtypeStruct((CHUNK, e), F32)],
        scratch=[pltpu.VMEM((e2, d), BF16), pltpu.VMEM((e, d), BF16), pltpu.VMEM((tm, e), F32),
                 pltpu.VMEM((tm, e), F32), pltpu.SemaphoreType.DMA((2 * N_DEV,))],
        hosted=hosted)


def _ffn_fwd(x, g, srcs, nf, tm, name, hosted=()):
    t_tok, d = x.shape
    f = nf * N_DEV
    firsts = [first for _, first in srcs]

    def body(x_ref, g_ref, sg_ref, su_ref, sd_ref, xo_ref, gate_ref, up_ref, wg_v, wu_v, wd_v, sems):
        _load_group(
            [(sg_ref, firsts[0], nf, wg_v), (su_ref, firsts[1], nf, wu_v), (sd_ref, firsts[2], nf, wd_v)], sems)
        xv = x_ref[...]
        h = _rms_fwd(xv, g_ref[...])[0].astype(BF16)
        gate = _nt(h, wg_v[...])
        up = _nt(h, wu_v[...])
        gate_ref[...] = gate.astype(BF16)
        up_ref[...] = up.astype(BF16)
        act = (gate * _sigmoid(gate) * up).astype(BF16)
        xo_ref[...] = xv + _nn(act, wd_v[...])

    return _hosting_call(
        body, name, t_tok // tm, [x, g] + [arr for arr, _ in srcs],
        in_specs=[_row_spec(tm, d), _const_spec((1, d)), ANY, ANY, ANY],
        out_specs=[_row_spec(tm, d), _row_spec(tm, f), _row_spec(tm, f)],
        out_shape=[jax.ShapeDtypeStruct((t_tok, d), F32), jax.ShapeDtypeStruct((t_tok, f), BF16),
                   jax.ShapeDtypeStruct((t_tok, f), BF16)],
        scratch=[pltpu.VMEM((f, d), BF16), pltpu.VMEM((f, d), BF16), pltpu.VMEM((f, d), BF16),
                 pltpu.SemaphoreType.DMA((3 * N_DEV,))],
        hosted=hosted)


def _ffn_bwd(dout, x, gate, up, g, srcs, nf, tm, name, hosted=()):
    t_tok, d = x.shape
    f = nf * N_DEV
    firsts = [first for _, first in srcs]
    per_chunk = -(-f // (FFN_CHUNKS * MXU_WIDTH)) * MXU_WIDTH
    bounds = [min(ck * per_chunk, f) for ck in range(FFN_CHUNKS + 1)]

    def body(dout_ref, x_ref, gate_ref, up_ref, g_ref, sg_ref, su_ref, sd_ref,
             dx_ref, dxb_ref, h_ref, act_ref, dgu_ref, dg_ref, wg_v, wu_v, wd_v, sems):
        _load_group(
            [(sg_ref, firsts[0], nf, wg_v), (su_ref, firsts[1], nf, wu_v), (sd_ref, firsts[2], nf, wd_v)], sems)

        @pl.when(pl.program_id(0) == 0)
        def _():
            dg_ref[...] = jnp.zeros_like(dg_ref)

        xv = x_ref[...]
        gv = g_ref[...]
        hv, xhat, r = _rms_fwd(xv, gv)
        h_ref[...] = hv.astype(BF16)
        dov = dout_ref[...]
        dob = dov.astype(BF16)
        dh = None
        for ck in range(FFN_CHUNKS):
            cols = slice(bounds[ck], bounds[ck + 1])
            gate_v = gate_ref[:, cols].astype(F32)
            up_v = up_ref[:, cols].astype(F32)
            sig = _sigmoid(gate_v)
            silu = gate_v * sig
            act_ref[:, cols] = (silu * up_v).astype(BF16)
            dact = _nt(dob, wd_v[cols, :])
            dup = (dact * silu).astype(BF16)
            dgate = (dact * up_v * (sig * (1.0 + gate_v * (1.0 - sig)))).astype(BF16)
            dgu_ref[:, cols] = dgate
            dgu_ref[:, f + bounds[ck]:f + bounds[ck + 1]] = dup
            part = _nn(dgate, wg_v[cols, :]) + _nn(dup, wu_v[cols, :])
            dh = part if dh is None else dh + part
        dxr, dg_row = _rms_bwd(dh, xhat, r, gv)
        dg_ref[...] += dg_row
        dx = dov + dxr
        dx_ref[...] = dx
        dxb_ref[...] = dx.astype(BF16)

    return _hosting_call(
        body, name, t_tok // tm, [dout, x, gate, up, g] + [arr for arr, _ in srcs],
        in_specs=[_row_spec(tm, d), _row_spec(tm, d), _row_spec(tm, f), _row_spec(tm, f), _const_spec((1, d)),
                  ANY, ANY, ANY],
        out_specs=[_row_spec(tm, d), _row_spec(tm, d), _row_spec(tm, d), _row_spec(tm, f), _row_spec(tm, 2 * f),
                   _const_spec((1, d))],
        out_shape=[jax.ShapeDtypeStruct((t_tok, d), F32), jax.ShapeDtypeStruct((t_tok, d), BF16),
                   jax.ShapeDtypeStruct((t_tok, d), BF16), jax.ShapeDtypeStruct((t_tok, f), BF16),
                   jax.ShapeDtypeStruct((t_tok, 2 * f), BF16), jax.ShapeDtypeStruct((1, d), F32)],
        scratch=[pltpu.VMEM((f, d), BF16), pltpu.VMEM((f, d), BF16), pltpu.VMEM((f, d), BF16),
                 pltpu.SemaphoreType.DMA((3 * N_DEV,))],
        hosted=hosted)


def _shift_down(z, k, prev_rows):
    row = lax.broadcasted_iota(jnp.int32, z.shape, 0)
    out = pltpu.roll(z, k, 0)
    for j in range(k):
        out = jnp.where(row == j, prev_rows[j], out)
    return out


def _shift_up(z, k, next_rows):
    tm = z.shape[0]
    row = lax.broadcasted_iota(jnp.int32, z.shape, 0)
    out = pltpu.roll(z, tm - k, 0)
    for j in range(k):
        out = jnp.where(row == tm - k + j, next_rows[j], out)
    return out


def _mixer_b_fwd(x, g, gath, conv_w, tm, seq, hosted=()):
    t_tok, d = x.shape
    e = conv_w.shape[1]
    e3 = 3 * e
    n_in, n_out = e3 // N_DEV, e // N_DEV
    tiles_per_seq = seq // tm

    def body(x_ref, g_ref, cw_ref, gath_ref, xo_ref, p_ref, win_v, wout_v, tail_v, sems):
        i = pl.program_id(0)
        _load_group([(gath_ref, 0, n_in, win_v), (gath_ref, n_in, n_out, wout_v)], sems)

        @pl.when(i % tiles_per_seq == 0)
        def _():
            tail_v[...] = jnp.zeros_like(tail_v)

        xv = x_ref[...]
        h = _rms_fwd(xv, g_ref[...])[0].astype(BF16)
        p = _nt(h, win_v[...])
        p_ref[...] = p.astype(BF16)
        z = p[:, e:2 * e] * p[:, 2 * e:]
        prev = [tail_v[SUBLANES - 2:SUBLANES - 1, :], tail_v[SUBLANES - 1:SUBLANES, :]]
        conv = (cw_ref[2:3, :] * z + cw_ref[1:2, :] * _shift_down(z, 1, prev[1:])
                + cw_ref[0:1, :] * _shift_down(z, 2, prev))
        tail_v[...] = z[tm - SUBLANES:, :]
        y = (p[:, :e] * conv).astype(BF16)
        xo_ref[...] = xv + _nn(y, wout_v[...])

    return _hosting_call(
        body, "mixer_b_fwd", t_tok // tm, [x, g, conv_w, gath],
        in_specs=[_row_spec(tm, d), _const_spec((1, d)), _const_spec((SUBLANES, e)), ANY],
        out_specs=[_row_spec(tm, d), _row_spec(tm, e3)],
        out_shape=[jax.ShapeDtypeStruct((t_tok, d), F32), jax.ShapeDtypeStruct((t_tok, e3), BF16)],
        scratch=[pltpu.VMEM((e3, d), BF16), pltpu.VMEM((e, d), BF16), pltpu.VMEM((SUBLANES, e), F32),
                 pltpu.SemaphoreType.DMA((2 * N_DEV,))],
        hosted=hosted)


def _mixer_b_bwd(dout, x, p, g, gath, conv_w, tm, seq, hosted=()):
    t_tok, d = x.shape
    e = conv_w.shape[1]
    e3 = 3 * e
    n_in, n_out = e3 // N_DEV, e // N_DEV
    tiles_per_seq = seq // tm
    halo_per_tile = tm // HALO
    n_halo = t_tok // HALO

    def body(dout_ref, dnext_ref, x_ref, p_ref, pprev_ref, pnext_ref, g_ref, cw_ref, gath_ref,
             dx_ref, dxb_ref, h_ref, y_ref, dp_ref, dg_ref, dcw_ref, win_v, wout_v, sems):
        i = pl.program_id(0)
        _load_group([(gath_ref, 0, n_in, win_v), (gath_ref, n_in, n_out, wout_v)], sems)

        @pl.when(i == 0)
        def _():
            dg_ref[...] = jnp.zeros_like(dg_ref)
            dcw_ref[...] = jnp.zeros_like(dcw_ref)

        first = (i % tiles_per_seq == 0).astype(F32)
        last = (i % tiles_per_seq == tiles_per_seq - 1).astype(F32)
        xv = x_ref[...]
        gv = g_ref[...]
        hv, xhat, r = _rms_fwd(xv, gv)
        h_ref[...] = hv.astype(BF16)
        pv = p_ref[...].astype(F32)
        bg, cg, hx = pv[:, :e], pv[:, e:2 * e], pv[:, 2 * e:]
        z = cg * hx
        pprev = pprev_ref[...].astype(F32)
        zprev = pprev[:, e:2 * e] * pprev[:, 2 * e:] * (1.0 - first)
        prev = [zprev[HALO - 2:HALO - 1, :], zprev[HALO - 1:HALO, :]]
        zs1 = _shift_down(z, 1, prev[1:])
        zs2 = _shift_down(z, 2, prev)
        w0, w1, w2 = cw_ref[0:1, :], cw_ref[1:2, :], cw_ref[2:3, :]
        conv = w2 * z + w1 * zs1 + w0 * zs2
        y_ref[...] = (bg * conv).astype(BF16)

        dov = dout_ref[...]
        wout_bf = wout_v[...]
        dy = _nt(dov.astype(BF16), wout_bf)
        dconv = dy * bg
        dnext = _nt(dnext_ref[...].astype(BF16), wout_bf) * pnext_ref[:, :e].astype(F32) * (1.0 - last)
        nxt = [dnext[0:1, :], dnext[1:2, :]]
        dz = w2 * dconv + w1 * _shift_up(dconv, 1, nxt[:1]) + w0 * _shift_up(dconv, 2, nxt)
        dcw_ref[0:1, :] += _col_sum(dconv * zs2)
        dcw_ref[1:2, :] += _col_sum(dconv * zs1)
        dcw_ref[2:3, :] += _col_sum(dconv * z)
        dp = jnp.concatenate([dy * conv, dz * hx, dz * cg], axis=1).astype(BF16)
        dp_ref[...] = dp
        dh = _nn(dp, win_v[...])
        dxr, dg_row = _rms_bwd(dh, xhat, r, gv)
        dg_ref[...] += dg_row
        dx = dov + dxr
        dx_ref[...] = dx
        dxb_ref[...] = dx.astype(BF16)

    prev_spec = lambda w: pl.BlockSpec((HALO, w), lambda i: (jnp.maximum(i * halo_per_tile - 1, 0), 0))
    next_spec = lambda w: pl.BlockSpec((HALO, w), lambda i: (jnp.minimum((i + 1) * halo_per_tile, n_halo - 1), 0))
    return _hosting_call(
        body, "mixer_b_bwd", t_tok // tm, [dout, dout, x, p, p, p, g, conv_w, gath],
        in_specs=[_row_spec(tm, d), next_spec(d), _row_spec(tm, d), _row_spec(tm, e3), prev_spec(e3), next_spec(e3),
                  _const_spec((1, d)), _const_spec((SUBLANES, e)), ANY],
        out_specs=[_row_spec(tm, d), _row_spec(tm, d), _row_spec(tm, d), _row_spec(tm, e), _row_spec(tm, e3),
                   _const_spec((1, d)), _const_spec((SUBLANES, e))],
        out_shape=[jax.ShapeDtypeStruct((t_tok, d), F32), jax.ShapeDtypeStruct((t_tok, d), BF16),
                   jax.ShapeDtypeStruct((t_tok, d), BF16), jax.ShapeDtypeStruct((t_tok, e), BF16),
                   jax.ShapeDtypeStruct((t_tok, e3), BF16), jax.ShapeDtypeStruct((1, d), F32),
                   jax.ShapeDtypeStruct((SUBLANES, e), F32)],
        scratch=[pltpu.VMEM((e3, d), BF16), pltpu.VMEM((e, d), BF16), pltpu.SemaphoreType.DMA((2 * N_DEV,))],
        hosted=hosted)


def _loss_head(x, target, g, tm):
    t_tok, d = x.shape

    def body(x_ref, t_ref, g_ref, loss_ref, dx_ref, dxb_ref, dg_ref):
        @pl.when(pl.program_id(0) == 0)
        def _():
            loss_ref[...] = jnp.zeros_like(loss_ref)
            dg_ref[...] = jnp.zeros_like(dg_ref)

        gv = g_ref[...]
        y, xhat, r = _rms_fwd(x_ref[...], gv)
        err = y - t_ref[...]
        loss_ref[...] += 0.5 * jnp.sum(_row_mean(err * err), axis=0, keepdims=True)
        dxr, dg_row = _rms_bwd(err * (1.0 / d), xhat, r, gv)
        dg_ref[...] += dg_row
        dx_ref[...] = dxr
        dxb_ref[...] = dxr.astype(BF16)

    return pl.pallas_call(
        body, name="loss_head", grid=(t_tok // tm,),
        in_specs=[_row_spec(tm, d), _row_spec(tm, d), _const_spec((1, d))],
        out_specs=[_const_spec((1, 1)), _row_spec(tm, d), _row_spec(tm, d), _const_spec((1, d))],
        out_shape=[jax.ShapeDtypeStruct((1, 1), F32), jax.ShapeDtypeStruct((t_tok, d), F32),
                   jax.ShapeDtypeStruct((t_tok, d), BF16), jax.ShapeDtypeStruct((1, d), F32)],
        compiler_params=_params(),
    )(x, target, g)


def _wgrad(a, b, bm, name, hosted=()):
    t_tok, m = a.shape
    n = b.shape[1]

    def body(a_ref, b_ref, o_ref):
        o_ref[...] = _tn(a_ref[...], b_ref[...]).astype(o_ref.dtype)

    outs, h_outs = _hosting_call(
        body, name, m // bm, [a, b],
        in_specs=[pl.BlockSpec((t_tok, bm), lambda i: (0, i)), _const_spec((t_tok, n))],
        out_specs=[pl.BlockSpec((bm, n), lambda i: (i, 0))],
        out_shape=[jax.ShapeDtypeStruct((m, n), BF16)],
        scratch=[], hosted=hosted)
    return (outs[0], h_outs) if hosted else outs[0]


def _sum_slots(land, rb, name):
    n_slots, rows, cols = land.shape

    def body(l_ref, o_ref):
        acc = l_ref[0].astype(F32)
        for k in range(1, n_slots):
            acc = acc + l_ref[k].astype(F32)
        o_ref[...] = acc

    return pl.pallas_call(
        body, name=name, grid=(rows // rb,),
        in_specs=[pl.BlockSpec((n_slots, rb, cols), lambda i: (0, i, 0))],
        out_specs=pl.BlockSpec((rb, cols), lambda i: (i, 0)),
        out_shape=jax.ShapeDtypeStruct((rows, cols), F32),
        compiler_params=_params(sequential=False),
    )(land)


def _adamw(w, grad, m, v, rb, name):
    rows, cols = w.shape
    c1 = 1.0 / (1.0 - ADAM_B1 ** ADAM_STEP)
    c2 = 1.0 / (1.0 - ADAM_B2 ** ADAM_STEP)

    def body(w_ref, g_ref, m_ref, v_ref, d_ref, mo_ref, vo_ref):
        gv = g_ref[...]
        mn = ADAM_B1 * m_ref[...] + (1.0 - ADAM_B1) * gv
        vn = ADAM_B2 * v_ref[...] + (1.0 - ADAM_B2) * (gv * gv)
        mo_ref[...] = mn
        vo_ref[...] = vn
        d_ref[...] = -ADAM_LR * ((mn * c1) / (jnp.sqrt(vn * c2) + ADAM_EPS) + ADAM_WD * w_ref[...])

    spec = pl.BlockSpec((rb, cols), lambda i: (i, 0))
    shape = jax.ShapeDtypeStruct((rows, cols), F32)
    return pl.pallas_call(
        body, name=name, grid=(rows // rb,),
        in_specs=[spec] * 4, out_specs=[spec] * 3, out_shape=[shape] * 3,
        compiler_params=_params(sequential=False),
    )(w, grad, m, v)


def _split_bf16(a):
    hi = a.astype(BF16)
    rest = a - hi.astype(F32)
    mid = rest.astype(BF16)
    return hi, mid, (rest - mid.astype(F32)).astype(BF16)


def _reduce_adamw(lands, w, m, v, transpose, name):
    n_layers, rows_w, cols_w = w.shape
    c1 = 1.0 / (1.0 - ADAM_B1 ** ADAM_STEP)
    c2 = 1.0 / (1.0 - ADAM_B2 ** ADAM_STEP)
    if transpose:
        n, tiles = cols_w, rows_w // MXU_WIDTH
        blk = (MXU_WIDTH, n)
        land_specs = [pl.BlockSpec((N_CHIP, n, MXU_WIDTH), lambda i, b=first // n: (0, b, i % tiles))
                      for _, first in lands]
    else:
        n, tiles = rows_w, 2
        blk = (n // tiles, cols_w)
        land_specs = [pl.BlockSpec((N_CHIP,) + blk, lambda i, b=first // blk[0]: (0, b + i % tiles, 0))
                      for _, first in lands]
    for _, first in lands:
        assert first % (n if transpose else blk[0]) == 0

    def body(*refs):
        land_refs = refs[:n_layers]
        w_ref, m_ref, v_ref, g_ref, d_ref, mo_ref, vo_ref = refs[n_layers:]
        layer = pl.program_id(0) // tiles

        def total(ref):
            acc = ref[0].astype(F32)
            for q in range(1, N_CHIP):
                acc = acc + ref[q].astype(F32)
            return acc

        gv = total(land_refs[0])
        for k in range(1, n_layers):
            gv = jnp.where(layer == k, total(land_refs[k]), gv)
        if transpose:
            r = lax.broadcasted_iota(jnp.int32, (MXU_WIDTH, MXU_WIDTH), 0)
            c = lax.broadcasted_iota(jnp.int32, (MXU_WIDTH, MXU_WIDTH), 1)
            eye = (r == c).astype(BF16)
            hi, mid, lo = _split_bf16(gv)
            gv = _nt(eye, hi) + _nt(eye, mid) + _nt(eye, lo)
        g_ref[...] = gv
        mn = ADAM_B1 * m_ref[...] + (1.0 - ADAM_B1) * gv
        vn = ADAM_B2 * v_ref[...] + (1.0 - ADAM_B2) * (gv * gv)
        mo_ref[...] = mn
        vo_ref[...] = vn
        d_ref[...] = -ADAM_LR * ((mn * c1) / (jnp.sqrt(vn * c2) + ADAM_EPS) + ADAM_WD * w_ref[...])

    spec = pl.BlockSpec((None,) + blk, lambda i: (i // tiles, i % tiles, 0))
    shape = jax.ShapeDtypeStruct(w.shape, F32)
    return pl.pallas_call(
        body, name=name, grid=(n_layers * tiles,),
        in_specs=land_specs + [spec] * 3, out_specs=[spec] * 4, out_shape=[shape] * 4,
        compiler_params=_params(sequential=False),
    )(*[land for land, _ in lands], w, m, v)


def _pack_small(parts, rows):
    flat = jnp.concatenate([p.reshape(-1).astype(F32) for p in parts])
    return jnp.pad(flat, (0, rows * LANES - flat.shape[0])).reshape(rows, LANES)


def _unpack_small(packed, shapes):
    flat = packed.reshape(-1)
    out = []
    pos = 0
    for s in shapes:
        n = math.prod(s)
        out.append(flat[pos:pos + n].reshape(s))
        pos += n
    return out


def kernel(x, mix_norm, ffn_norm, a_w_in, a_v_gain, a_v_bias, a_w_s, a_b_s, a_w_out, b_w_in, b_conv_w, b_w_out, ffn_w_gate, ffn_w_up, ffn_w_down, final_norm, loss_target, m_mix_norm, m_ffn_norm, m_a_w_in, m_a_v_gain, m_a_v_bias, m_a_w_s, m_a_b_s, m_a_w_out, m_b_w_in, m_b_conv_w, m_b_w_out, m_ffn_w_gate, m_ffn_w_up, m_ffn_w_down, m_final_norm, v_mix_norm, v_ffn_norm, v_a_w_in, v_a_v_gain, v_a_v_bias, v_a_w_s, v_a_b_s, v_a_w_out, v_b_w_in, v_b_conv_w, v_b_w_out, v_ffn_w_gate, v_ffn_w_up, v_ffn_w_down, v_final_norm):
    bsz, seq, d = x.shape
    t_tok = bsz * seq
    me = _my_index()
    xt = x.reshape(t_tok, d)
    target = loss_target.reshape(t_tok, d)
    e_a = a_v_gain.shape[1]
    e_b = b_w_out.shape[1] * N_DEV
    n_layers = ffn_w_gate.shape[0]
    f_shard = ffn_w_gate.shape[2]
    f_full = f_shard * N_DEV

    conv_pad = jnp.pad(b_conv_w[0], ((0, SUBLANES - CONV_W), (0, 0)))
    sh_a = jnp.concatenate([a_w_in[0].T, a_w_out[0]]).astype(BF16)
    sh_b = jnp.concatenate([b_w_in[0].T, b_w_out[0]]).astype(BF16)
    sh_f0 = jnp.concatenate([ffn_w_gate[0].T, ffn_w_up[0].T, ffn_w_down[0]]).astype(BF16)
    sh_f1g = ffn_w_gate[1].T.astype(BF16)
    sh_f1ud = jnp.concatenate([ffn_w_up[1].T, ffn_w_down[1]]).astype(BF16)
    bfull = jnp.repeat(a_b_s[0].T, GROUP, axis=1)

    gath_a, conv_g = _exchange([_HostedGathers([sh_a, conv_pad], 0)], "gather_a")
    conv_full = jnp.pad(conv_g[:, :CONV_W, :].transpose(1, 0, 2).reshape(CONV_W, e_b), ((0, SUBLANES - CONV_W), (0, 0)))
    (x1, zpre, y_a), (gath_f0,) = _mixer_a_fwd(xt, mix_norm[0:1], gath_a, a_v_gain, a_v_bias, a_w_s[0], bfull, tm=256,
                                               hosted=[_HostedGathers([sh_f0], mid_lead=2)])
    srcs0 = [(gath_f0, 0), (gath_f0, f_shard), (gath_f0, 2 * f_shard)]
    (x2, gate0, up0), (gath_b, gath_f1g) = _ffn_fwd(x1, ffn_norm[0:1], srcs0, f_shard, tm=256, name="ffn_fwd0",
                                                    hosted=[_HostedGathers([sh_b, sh_f1g], mid_lead=2)])
    (x3, p_b), (gath_f1ud,) = _mixer_b_fwd(x2, mix_norm[1:2], gath_b, conv_full, tm=256, seq=seq,
                                           hosted=[_HostedGathers([sh_f1ud], mid_lead=2)])
    srcs1 = [(gath_f1g, 0), (gath_f1ud, 0), (gath_f1ud, f_shard)]
    (x4, gate1, up1), _ = _ffn_fwd(x3, ffn_norm[1:2], srcs1, f_shard, tm=256, name="ffn_fwd1")
    loss_part, dx4, dx4_bf, d_final = _loss_head(x4, target, final_norm.reshape(1, d), tm=512)

    ffn_entries = [(0, 0, f_shard), (0, f_full, f_shard), (1, 0, f_shard)]
    (dx3, dx3_bf, h_f1, act1, dgu1, d_fn1), _ = _ffn_bwd(dx4, x3, gate1, up1, ffn_norm[1:2], srcs1, f_shard, tm=256,
                                                         name="ffn_bwd1")
    g_down1 = _wgrad(act1, dx4_bf, 256, "wgrad_down1")
    g_gu1 = _wgrad(dgu1, h_f1, 512, "wgrad_gate_up1")
    ps_f1 = _pair_reduce([g_gu1, g_down1], ffn_entries, "pair_reduce_f1")
    (dx2, dx2_bf, h_b, y_b, dp_b, d_mn1, d_conv), (land_f1gu,) = _mixer_b_bwd(
        dx3, x2, p_b, mix_norm[1:2], gath_b, conv_full, tm=256, seq=seq,
        hosted=[_HostedChipScatter(ps_f1, 0, 2 * f_shard)])
    g_b_out = _wgrad(y_b, dx3_bf, 256, "wgrad_b_out")
    g_b_in = _wgrad(dp_b, h_b, 512, "wgrad_b_in")
    ps_b = _pair_reduce([g_b_in, g_b_out], [(0, 0, b_w_in.shape[2]), (1, 0, b_w_out.shape[1])], "pair_reduce_b")
    (dx1, dx1_bf, h_f0, act0, dgu0, d_fn0), (land_f1d, land_b) = _ffn_bwd(
        dx2, x1, gate0, up0, ffn_norm[0:1], srcs0, f_shard, tm=256, name="ffn_bwd0",
        hosted=[_HostedChipScatter(ps_f1, 2 * f_shard, f_shard), _HostedChipScatter(ps_b)])
    g_down0 = _wgrad(act0, dx2_bf, 256, "wgrad_down0")
    g_gu0 = _wgrad(dgu0, h_f0, 512, "wgrad_gate_up0")
    ps_f0 = _pair_reduce([g_gu0, g_down0], ffn_entries, "pair_reduce_f0")
    g_a_out = _wgrad(y_a, dx1_bf, 256, "wgrad_a_out")
    ps_ao = _pair_reduce([g_a_out], [(0, 0, a_w_out.shape[1])], "pair_reduce_a_out")
    (dx0, _, h_a, dz_a, d_mn0, d_gain, d_bias, d_ws, d_bs_acc), (land_f0, land_ao) = _mixer_a_bwd(
        dx1, xt, zpre, mix_norm[0:1], gath_a, a_v_gain, a_v_bias, a_w_s[0], bfull, tm=256,
        hosted=[_HostedChipScatter(ps_f0), _HostedChipScatter(ps_ao)])
    d_bs = d_bs_acc.reshape(CHUNK, HEADS, GROUP).sum(axis=2).T

    small_grads = [jnp.concatenate([d_mn0, d_mn1]), jnp.concatenate([d_fn0, d_fn1]), d_gain, d_bias, d_ws, d_bs,
                   d_final, d_conv[:CONV_W], loss_part]
    small_shapes = [(n_layers, d), (n_layers, d), (1, e_a), (1, e_a), (1, HEADS, CHUNK, CHUNK), (1, HEADS, CHUNK), (d,),
                    (CONV_W, e_b), ()]
    n_small = sum(math.prod(s) for s in small_shapes)
    blk_rows = -(-n_small // (N_DEV * LANES * SUBLANES)) * SUBLANES
    small_rows = blk_rows * N_DEV
    packed = _pack_small(small_grads, small_rows)
    g_a_in, (small_land,) = _wgrad(dz_a, h_a, 512, "wgrad_a_in", hosted=[_HostedScatterAll(packed)])
    ps_ai = _pair_reduce([g_a_in], [(0, 0, a_w_in.shape[2])], "pair_reduce_a_in")
    small_sum = _sum_slots(small_land, blk_rows, "sum_small")
    land_ai, small_gath = _exchange([_HostedChipScatter(ps_ai), _HostedGathers([small_sum], 0)], "tail_exchange")
    small_all = small_gath.reshape(small_rows, LANES)

    (gr_mix, gr_ffn, gr_gain, gr_bias, gr_ws, gr_bs, gr_final, gr_conv_full, loss) = _unpack_small(small_all, small_shapes)
    gr_conv = lax.dynamic_slice_in_dim(gr_conv_full, me * (e_b // N_DEV), e_b // N_DEV, axis=1)[None]

    small_w = [mix_norm, ffn_norm, a_v_gain, a_v_bias, a_w_s, a_b_s, final_norm]
    small_m = [m_mix_norm, m_ffn_norm, m_a_v_gain, m_a_v_bias, m_a_w_s, m_a_b_s, m_final_norm]
    small_v = [v_mix_norm, v_ffn_norm, v_a_v_gain, v_a_v_bias, v_a_w_s, v_a_b_s, v_final_norm]
    small_g = [gr_mix, gr_ffn, gr_gain, gr_bias, gr_ws, gr_bs, gr_final]
    sm_shapes = small_shapes[:len(small_w)]
    sm_out = _adamw(_pack_small(small_w, small_rows), _pack_small(small_g, small_rows), _pack_small(small_m, small_rows),
                    _pack_small(small_v, small_rows), small_rows, "adamw_small")
    sm_delta, sm_m, sm_v = [_unpack_small(o, sm_shapes) for o in sm_out]

    conv_out = _adamw(b_conv_w[0], gr_conv[0], m_b_conv_w[0], v_b_conv_w[0], CONV_W, "adamw_conv")
    conv_delta, conv_m, conv_v = [o[None] for o in conv_out]

    n_b_in = b_w_in.shape[2]
    res = {
        "a_w_in": _reduce_adamw([(land_ai, 0)], a_w_in, m_a_w_in, v_a_w_in, True, "adamw_a_in"),
        "a_w_out": _reduce_adamw([(land_ao, 0)], a_w_out, m_a_w_out, v_a_w_out, False, "adamw_a_out"),
        "b_w_in": _reduce_adamw([(land_b, 0)], b_w_in, m_b_w_in, v_b_w_in, True, "adamw_b_in"),
        "b_w_out": _reduce_adamw([(land_b, n_b_in)], b_w_out, m_b_w_out, v_b_w_out, False, "adamw_b_out"),
        "ffn_w_gate": _reduce_adamw([(land_f0, 0), (land_f1gu, 0)], ffn_w_gate, m_ffn_w_gate, v_ffn_w_gate, True,
                                    "adamw_gate"),
        "ffn_w_up": _reduce_adamw([(land_f0, f_shard), (land_f1gu, f_shard)], ffn_w_up, m_ffn_w_up, v_ffn_w_up, True,
                                  "adamw_up"),
        "ffn_w_down": _reduce_adamw([(land_f0, 2 * f_shard), (land_f1d, 0)], ffn_w_down, m_ffn_w_down, v_ffn_w_down,
                                    False, "adamw_down"),
    }

    order = ["mix_norm", "ffn_norm", "a_w_in", "a_v_gain", "a_v_bias", "a_w_s", "a_b_s", "a_w_out", "b_w_in",
             "b_conv_w", "b_w_out", "ffn_w_gate", "ffn_w_up", "ffn_w_down", "final_norm"]
    small_names = ["mix_norm", "ffn_norm", "a_v_gain", "a_v_bias", "a_w_s", "a_b_s", "final_norm"]
    grads = {"b_conv_w": gr_conv}
    deltas, new_m, new_v = {}, {}, {}
    for k, name in enumerate(small_names):
        grads[name] = small_g[k]
        deltas[name], new_m[name], new_v[name] = sm_delta[k], sm_m[k], sm_v[k]
    deltas["b_conv_w"], new_m["b_conv_w"], new_v["b_conv_w"] = conv_delta, conv_m, conv_v
    for name, (gg, dl, mm, vv) in res.items():
        grads[name], deltas[name], new_m[name], new_v[name] = gg, dl, mm, vv

    grad_x = dx0.reshape(bsz, seq, d)
    return (loss, grad_x, *[grads[n] for n in order], *[deltas[n] for n in order],
            *[new_m[n] for n in order], *[new_v[n] for n in order])
```

```python
import math

import jax
import jax.numpy as jnp
from jax import lax
from jax.experimental import pallas as pl
from jax.experimental.pallas import tpu as pltpu

F32 = jnp.float32
BF16 = jnp.bfloat16

N_DEV = 8
N_CHIP = 4
CHUNK = 128
HEADS = 16
GROUP = 128
CONV_W = 3
NORM_EPS = 1e-6
GELU_C = math.sqrt(2.0 / math.pi)
GELU_K = 0.044715

ADAM_LR = 0.001
ADAM_B1 = 0.9
ADAM_B2 = 0.999
ADAM_EPS = 1e-08
ADAM_WD = 0.01
ADAM_STEP = 10

LANES = 128
SUBLANES = 8
VMEM_LIMIT = 60 * 1024 * 1024
HALO = 16
MXU_WIDTH = 256
FFN_CHUNKS = 2

MESH = pl.DeviceIdType.MESH
ANY = pl.BlockSpec(memory_space=pl.ANY)


def _params(sequential=True):
    return pltpu.CompilerParams(
        dimension_semantics=("arbitrary",) if sequential else None,
        vmem_limit_bytes=VMEM_LIMIT)


def _nn(a, b):
    return jnp.dot(a, b, preferred_element_type=F32)


def _nt(a, b):
    return lax.dot_general(a, b, (((1,), (1,)), ((), ())), preferred_element_type=F32)


def _tn(a, b):
    return lax.dot_general(a, b, (((0,), (0,)), ((), ())), preferred_element_type=F32)


def _row_mean(a):
    return jnp.mean(a, axis=-1, keepdims=True)


def _col_sum(a):
    return jnp.sum(a, axis=0, keepdims=True)


def _rms_fwd(x, g):
    r = lax.rsqrt(_row_mean(x * x) + NORM_EPS)
    xhat = x * r
    return xhat * g, xhat, r


def _rms_bwd(dh, xhat, r, g):
    a = dh * g
    dx = r * (a - xhat * _row_mean(a * xhat))
    return dx, _col_sum(dh * xhat)


def _gelu_and_grad(x):
    x2 = x * x
    t = jnp.tanh(GELU_C * (x + GELU_K * x2 * x))
    half = 0.5 * (1.0 + t)
    d = half + 0.5 * x * (1.0 - t * t) * (GELU_C * (1.0 + 3.0 * GELU_K * x2))
    return x * half, d


def _sigmoid(x):
    return 1.0 / (1.0 + jnp.exp(-x))


def _row_spec(tm, width):
    return pl.BlockSpec((tm, width), lambda i: (i, 0))


def _const_spec(shape):
    nd = len(shape)
    return pl.BlockSpec(shape, lambda i: (0,) * nd)


def _load_group(parts, sems):
    @pl.when(pl.program_id(0) == 0)
    def _():
        copies = []
        for k, (gath_ref, first, n, dst) in enumerate(parts):
            for j in range(N_DEV):
                copies.append(pltpu.make_async_copy(gath_ref.at[j, pl.ds(first, n), :], dst.at[pl.ds(j * n, n), :],
                                                    sems.at[k * N_DEV + j]))
        for cp in copies:
            cp.start()
        for cp in copies:
            cp.wait()


def _hosting_call(body, name, n_steps, arrays, in_specs, out_specs, out_shape, scratch, hosted=()):
    n_in, n_out, n_scr = len(arrays), len(out_shape), len(scratch)
    h_arrays = [a for h in hosted for a in h.arrays]
    h_shapes = [s for h in hosted for s in h.out_shapes]
    h_sems = [s for h in hosted for s in h.sem_shapes]

    def full_body(*refs):
        pos = 0
        groups = []
        for n in (n_in, len(h_arrays), n_out, len(h_shapes), n_scr, len(h_sems)):
            groups.append(refs[pos:pos + n])
            pos += n
        own_in, h_in, own_out, h_out, own_scr, h_sem = groups
        per_host = []
        pi = po = ps = 0
        for h in hosted:
            ni, no, ns = len(h.arrays), len(h.out_shapes), len(h.sem_shapes)
            per_host.append((h, h_in[pi:pi + ni], h_out[po:po + no], h_sem[ps:ps + ns]))
            pi, po, ps = pi + ni, po + no, ps + ns
        for h, ins, outs, sems in per_host:
            h.begin(ins, outs, sems, n_steps)
        body(*own_in, *own_out, *own_scr)
        for h, ins, outs, sems in per_host:
            h.end(ins, outs, sems, n_steps)

    outs = pl.pallas_call(
        full_body, name=name, grid=(n_steps,),
        in_specs=list(in_specs) + [ANY] * len(h_arrays),
        out_specs=list(out_specs) + [ANY] * len(h_shapes),
        out_shape=list(out_shape) + h_shapes,
        scratch_shapes=list(scratch) + h_sems,
        compiler_params=_params(),
    )(*arrays, *h_arrays)
    return outs[:n_out], outs[n_out:]


def _my_index():
    return 4 * lax.axis_index("x") + 2 * lax.axis_index("y") + lax.axis_index("c")


GATHER_COPIES = 8


class _Gather:
    def __init__(self, shard, out, send_sems, recv_sems, local_sem):
        self.shard, self.out = shard, out
        self.send_sems, self.recv_sems, self.local_sem = send_sems, recv_sems, local_sem
        x, y, c = lax.axis_index("x"), lax.axis_index("y"), lax.axis_index("c")
        self.c = c
        self.me, self.sibling = (x, y, c), (x, y, 1 - c)
        self.xn, self.yn, self.dg = (1 - x, y), (x, 1 - y), (1 - x, 1 - y)
        self.n = shard.shape[0]
        self.half = self.n // 2
        rows_per_tile = SUBLANES * 4 // shard.dtype.itemsize
        self.relays = self.n % 2 == 0 and self.half % rows_per_tile == 0

    def _slot(self, dev, lo=0, hi=None):
        hi = self.n if hi is None else hi
        return self.out.at[4 * dev[0] + 2 * dev[1] + dev[2], pl.ds(lo, hi - lo), :]

    def _copy(self, k, block, to, src=None, lo=0, hi=None):
        return pltpu.make_async_remote_copy(
            src_ref=self._slot(block, lo, hi) if src is None else src, dst_ref=self._slot(block, lo, hi),
            send_sem=self.send_sems.at[k], recv_sem=self.recv_sems.at[k], device_id=to, device_id_type=MESH)

    def _local(self):
        return pltpu.make_async_copy(self.shard, self._slot(self.me), self.local_sem)

    def start(self):
        c = self.c
        self._local().start()
        self._copy(0, self.me, self.sibling, src=self.shard).start()
        self._copy(1, self.me, (*self.xn, c), src=self.shard).start()
        self._copy(2, self.me, (*self.yn, c), src=self.shard).start()
        if not self.relays:
            self._copy(3, self.me, (*self.dg, c), src=self.shard).start()

    def relay(self):
        c = self.c
        if self.relays:
            self._copy(1, (*self.xn, c), self.me).wait_recv()
            self._copy(3, (*self.xn, c), (*self.yn, c), hi=self.half).start()
            self._copy(2, (*self.yn, c), self.me).wait_recv()
            self._copy(4, (*self.yn, c), (*self.xn, c), lo=self.half).start()

    def forward(self):
        c = self.c
        if self.relays:
            self._copy(5, (*self.xn, c), self.sibling).start()
            self._copy(6, (*self.yn, c), self.sibling).start()
            self._copy(3, (*self.dg, c), self.me, hi=self.half).wait_recv()
            self._copy(4, (*self.dg, c), self.me, lo=self.half).wait_recv()
        else:
            self._copy(1, (*self.xn, c), self.me).wait_recv()
            self._copy(5, (*self.xn, c), self.sibling).start()
            self._copy(2, (*self.yn, c), self.me).wait_recv()
            self._copy(6, (*self.yn, c), self.sibling).start()
            self._copy(3, (*self.dg, c), self.me).wait_recv()
        self._copy(7, (*self.dg, c), self.sibling).start()

    def finish(self):
        c = self.c
        self._copy(0, self.sibling, self.me).wait_recv()
        for k, chip in ((5, self.xn), (6, self.yn), (7, self.dg)):
            self._copy(k, (*chip, 1 - c), self.me).wait_recv()
        for k in (0, 1, 2, 5, 6, 7):
            self._copy(k, self.me, self.sibling).wait_send()
        if self.relays:
            self._copy(3, self.me, self.sibling, hi=self.half).wait_send()
            self._copy(4, self.me, self.sibling, lo=self.half).wait_send()
        else:
            self._copy(3, self.me, self.sibling).wait_send()
        self._local().wait()


class _HostedGathers:
    def __init__(self, shards, mid_lead, relay_at=0.56):
        n = len(shards)
        self.arrays = shards
        self.mid_lead, self.relay_at = mid_lead, relay_at
        self.out_shapes = [jax.ShapeDtypeStruct((N_DEV,) + s.shape, s.dtype) for s in shards]
        self.sem_shapes = [pltpu.SemaphoreType.DMA((n, GATHER_COPIES)), pltpu.SemaphoreType.DMA((n, GATHER_COPIES)),
                           pltpu.SemaphoreType.DMA((n,))]

    def _gathers(self, ins, outs, sems):
        return [_Gather(ins[a], outs[a], sems[0].at[a], sems[1].at[a], sems[2].at[a]) for a in range(len(ins))]

    def begin(self, ins, outs, sems, n_steps):
        i = pl.program_id(0)
        forward_step = max(n_steps - 1 - self.mid_lead, 0)
        relay_step = min(int(self.relay_at * n_steps), forward_step)

        @pl.when(i == 0)
        def _():
            for g in self._gathers(ins, outs, sems):
                g.start()

        @pl.when(i == relay_step)
        def _():
            for g in self._gathers(ins, outs, sems):
                g.relay()

        @pl.when(i == forward_step)
        def _():
            for g in self._gathers(ins, outs, sems):
                g.forward()

    def end(self, ins, outs, sems, n_steps):
        @pl.when(pl.program_id(0) == n_steps - 1)
        def _():
            for g in self._gathers(ins, outs, sems):
                g.finish()


def _exchange(hosted, name):
    return _hosting_call(lambda: None, name, 1, [], [], [], [], [], hosted=hosted)[1]


class _ChipScatter:
    def __init__(self, pairsum, row0, land, send_sems, recv_sems, local_sem):
        self.pairsum, self.row0, self.land = pairsum, row0, land
        self.send_sems, self.recv_sems, self.local_sem = send_sems, recv_sems, local_sem
        x, y, c = lax.axis_index("x"), lax.axis_index("y"), lax.axis_index("c")
        self.c = c
        self.chip = 2 * x + y
        self.others = [(1 - x, y), (x, 1 - y), (1 - x, 1 - y)]

    def _src(self, chip):
        return self.pairsum.at[chip, pl.ds(self.row0, self.land.shape[1]), :]

    def _copy(self, k):
        ox, oy = self.others[k]
        return pltpu.make_async_remote_copy(
            src_ref=self._src(2 * ox + oy), dst_ref=self.land.at[self.chip],
            send_sem=self.send_sems.at[k], recv_sem=self.recv_sems.at[k], device_id=(ox, oy, self.c),
            device_id_type=MESH)

    def _arrival(self, k):
        ox, oy = self.others[k]
        return pltpu.make_async_remote_copy(
            src_ref=self._src(self.chip), dst_ref=self.land.at[2 * ox + oy],
            send_sem=self.send_sems.at[k], recv_sem=self.recv_sems.at[k], device_id=(ox, oy, self.c),
            device_id_type=MESH)

    def _local(self):
        return pltpu.make_async_copy(self._src(self.chip), self.land.at[self.chip], self.local_sem)

    def start(self):
        self._local().start()
        for k in range(N_CHIP - 1):
            self._copy(k).start()

    def finish(self):
        for k in range(N_CHIP - 1):
            self._arrival(k).wait_recv()
        for k in range(N_CHIP - 1):
            self._copy(k).wait_send()
        self._local().wait()


class _HostedChipScatter:
    def __init__(self, pairsum, row0=0, n=None):
        n = pairsum.shape[1] - row0 if n is None else n
        self.row0 = row0
        self.arrays = [pairsum]
        self.out_shapes = [jax.ShapeDtypeStruct((N_CHIP, n, pairsum.shape[2]), pairsum.dtype)]
        self.sem_shapes = [pltpu.SemaphoreType.DMA((N_CHIP - 1,)), pltpu.SemaphoreType.DMA((N_CHIP - 1,)),
                           pltpu.SemaphoreType.DMA(())]

    def begin(self, ins, outs, sems, n_steps):
        @pl.when(pl.program_id(0) == 0)
        def _():
            _ChipScatter(ins[0], self.row0, outs[0], *sems).start()

    def end(self, ins, outs, sems, n_steps):
        @pl.when(pl.program_id(0) == n_steps - 1)
        def _():
            _ChipScatter(ins[0], self.row0, outs[0], *sems).finish()


def _pair_reduce(arrays, entries, name):
    n_arr, n_ent = len(arrays), len(entries)
    cols = arrays[0].shape[1]
    offsets = []
    total = 0
    for _, _, n in entries:
        offsets.append(total)
        total += n

    def body(*refs):
        ins, out_ref = refs[:n_arr], refs[n_arr]
        rbuf, own, send_sems, recv_sems, own_sems = refs[n_arr + 1:]
        q = pl.program_id(0)
        x, y, c = lax.axis_index("x"), lax.axis_index("y"), lax.axis_index("c")

        def block(e, chip, core):
            ai, first, n = entries[e]
            return ins[ai].at[pl.ds(first + (2 * chip + core) * n, n), :]

        def to_sibling(e, chip):
            return pltpu.make_async_remote_copy(
                src_ref=block(e, chip, 1 - c), dst_ref=rbuf.at[chip, pl.ds(offsets[e], entries[e][2]), :],
                send_sem=send_sems.at[e, chip], recv_sem=recv_sems.at[e, chip], device_id=(x, y, 1 - c),
                device_id_type=MESH)

        @pl.when(q == 0)
        def _():
            for chip in range(N_CHIP):
                for e in range(n_ent):
                    to_sibling(e, chip).start()

        loads = [pltpu.make_async_copy(block(e, q, c), own.at[pl.ds(offsets[e], entries[e][2]), :], own_sems.at[e])
                 for e in range(n_ent)]
        for cp in loads:
            cp.start()
        for cp in loads:
            cp.wait()
        for e in range(n_ent):
            to_sibling(e, q).wait_recv()
        out_ref[...] = (own[...].astype(F32) + rbuf[q].astype(F32)).astype(out_ref.dtype)

        @pl.when(q == N_CHIP - 1)
        def _():
            for chip in range(N_CHIP):
                for e in range(n_ent):
                    to_sibling(e, chip).wait_send()

    return pl.pallas_call(
        body, name=name, grid=(N_CHIP,),
        in_specs=[ANY] * n_arr,
        out_specs=pl.BlockSpec((None, total, cols), lambda q: (q, 0, 0)),
        out_shape=jax.ShapeDtypeStruct((N_CHIP, total, cols), BF16),
        scratch_shapes=[pltpu.VMEM((N_CHIP, total, cols), BF16), pltpu.VMEM((total, cols), BF16),
                        pltpu.SemaphoreType.DMA((n_ent, N_CHIP)), pltpu.SemaphoreType.DMA((n_ent, N_CHIP)),
                        pltpu.SemaphoreType.DMA((n_ent,))],
        compiler_params=_params(),
    )(*arrays)


class _HostedScatterAll:
    def __init__(self, packed):
        n = packed.shape[0] // N_DEV
        self.n = n
        self.arrays = [packed]
        self.out_shapes = [jax.ShapeDtypeStruct((N_DEV, n, packed.shape[1]), packed.dtype)]
        self.sem_shapes = [pltpu.SemaphoreType.DMA((N_DEV - 1,)), pltpu.SemaphoreType.DMA((N_DEV - 1,)),
                           pltpu.SemaphoreType.DMA(())]

    def _copies(self, ins, outs, sems, with_arrivals):
        src, land = ins[0], outs[0]
        send_sems, recv_sems, local_sem = sems
        me = _my_index()

        def block(p):
            return src.at[pl.ds(p * self.n, self.n), :]

        local = pltpu.make_async_copy(block(me), land.at[me], local_sem)
        sends, arrivals = [], []
        for k in range(1, N_DEV):
            p = (me + k) % N_DEV
            q = (me + N_DEV - k) % N_DEV
            sends.append(pltpu.make_async_remote_copy(
                src_ref=block(p), dst_ref=land.at[me], send_sem=send_sems.at[k - 1], recv_sem=recv_sems.at[k - 1],
                device_id=(p // 4, (p // 2) % 2, p % 2), device_id_type=MESH))
            if with_arrivals:
                arrivals.append(pltpu.make_async_remote_copy(
                    src_ref=block(me), dst_ref=land.at[q], send_sem=send_sems.at[k - 1], recv_sem=recv_sems.at[k - 1],
                    device_id=(q // 4, (q // 2) % 2, q % 2), device_id_type=MESH))
        return local, sends, arrivals

    def begin(self, ins, outs, sems, n_steps):
        @pl.when(pl.program_id(0) == 0)
        def _():
            local, sends, _ = self._copies(ins, outs, sems, with_arrivals=False)
            local.start()
            for cp in sends:
                cp.start()

    def end(self, ins, outs, sems, n_steps):
        @pl.when(pl.program_id(0) == n_steps - 1)
        def _():
            local, sends, arrivals = self._copies(ins, outs, sems, with_arrivals=True)
            for cp in arrivals:
                cp.wait_recv()
            for cp in sends:
                cp.wait_send()
            local.wait()


def _tril_weights(ws_ref):
    r = lax.broadcasted_iota(jnp.int32, (CHUNK, CHUNK), 0)
    c = lax.broadcasted_iota(jnp.int32, (CHUNK, CHUNK), 1)
    return [jnp.where(r >= c, ws_ref[h], 0.0).astype(BF16) for h in range(HEADS)]


def _sgu_stats(zpre, gain, bias):
    e = zpre.shape[1] // 2
    z, dz = _gelu_and_grad(zpre)
    u, v = z[:, :e], z[:, e:]
    vc = v - _row_mean(v)
    rstd = lax.rsqrt(_row_mean(vc * vc) + NORM_EPS)
    vhat = vc * rstd
    return u, vhat, rstd, vhat * gain + bias, dz


def _spatial_fwd(wt, vn_bf, bfull_ref, sv_ref, tm):
    for ci in range(tm // CHUNK):
        rows = slice(ci * CHUNK, (ci + 1) * CHUNK)
        for h in range(HEADS):
            cols = slice(h * GROUP, (h + 1) * GROUP)
            sv_ref[rows, cols] = _nn(wt[h], vn_bf[rows, cols]) + bfull_ref[:, cols]


def _mixer_a_fwd(x, g, gath, gain, bias, ws, bfull, tm, hosted=()):
    t_tok, d = x.shape
    e = gain.shape[1]
    e2 = 2 * e
    n_in, n_out = e2 // N_DEV, e // N_DEV

    def body(x_ref, g_ref, gain_ref, bias_ref, ws_ref, bfull_ref, gath_ref,
             xo_ref, zpre_ref, y_ref, win_v, wout_v, sv_v, sems):
        _load_group([(gath_ref, 0, n_in, win_v), (gath_ref, n_in, n_out, wout_v)], sems)
        xv = x_ref[...]
        h = _rms_fwd(xv, g_ref[...])[0].astype(BF16)
        zpre = _nt(h, win_v[...])
        zpre_ref[...] = zpre.astype(BF16)
        u, _, _, vn, _ = _sgu_stats(zpre, gain_ref[...], bias_ref[...])
        _spatial_fwd(_tril_weights(ws_ref), vn.astype(BF16), bfull_ref, sv_v, tm)
        y = (u * sv_v[...]).astype(BF16)
        y_ref[...] = y
        xo_ref[...] = xv + _nn(y, wout_v[...])

    return _hosting_call(
        body, "mixer_a_fwd", t_tok // tm, [x, g, gain, bias, ws, bfull, gath],
        in_specs=[_row_spec(tm, d), _const_spec((1, d)), _const_spec((1, e)), _const_spec((1, e)),
                  _const_spec((HEADS, CHUNK, CHUNK)), _const_spec((CHUNK, e)), ANY],
        out_specs=[_row_spec(tm, d), _row_spec(tm, e2), _row_spec(tm, e)],
        out_shape=[jax.ShapeDtypeStruct((t_tok, d), F32), jax.ShapeDtypeStruct((t_tok, e2), BF16),
                   jax.ShapeDtypeStruct((t_tok, e), BF16)],
        scratch=[pltpu.VMEM((e2, d), BF16), pltpu.VMEM((e, d), BF16), pltpu.VMEM((tm, e), F32),
                 pltpu.SemaphoreType.DMA((2 * N_DEV,))],
        hosted=hosted)


def _mixer_a_bwd(dout, x, zpre, g, gath, gain, bias, ws, bfull, tm, hosted=()):
    t_tok, d = x.shape
    e = gain.shape[1]
    e2 = 2 * e
    n_in, n_out = e2 // N_DEV, e // N_DEV
    n_steps = t_tok // tm

    def body(dout_ref, x_ref, zpre_ref, g_ref, gain_ref, bias_ref, ws_ref, bfull_ref, gath_ref,
             dx_ref, dxb_ref, h_ref, dz_ref, dg_ref, dgain_ref, dbias_ref, dws_ref, dbs_ref,
             win_v, wout_v, sv_v, dvn_v, sems):
        i = pl.program_id(0)
        _load_group([(gath_ref, 0, n_in, win_v), (gath_ref, n_in, n_out, wout_v)], sems)

        @pl.when(i == 0)
        def _():
            dg_ref[...] = jnp.zeros_like(dg_ref)
            dgain_ref[...] = jnp.zeros_like(dgain_ref)
            dbias_ref[...] = jnp.zeros_like(dbias_ref)
            dws_ref[...] = jnp.zeros_like(dws_ref)
            dbs_ref[...] = jnp.zeros_like(dbs_ref)

        xv = x_ref[...]
        gv = g_ref[...]
        hv, xhat, r = _rms_fwd(xv, gv)
        h_ref[...] = hv.astype(BF16)
        gain_v = gain_ref[...]
        u, vhat, rstd, vn, gelu_d = _sgu_stats(zpre_ref[...].astype(F32), gain_v, bias_ref[...])
        vn_bf = vn.astype(BF16)
        wt = _tril_weights(ws_ref)
        _spatial_fwd(wt, vn_bf, bfull_ref, sv_v, tm)
        sv = sv_v[...]

        dov = dout_ref[...]
        dy = _nt(dov.astype(BF16), wout_v[...])
        du = dy * sv
        dsv = dy * u
        dsv_bf = dsv.astype(BF16)
        for ci in range(tm // CHUNK):
            rows = slice(ci * CHUNK, (ci + 1) * CHUNK)
            dbs_ref[...] += dsv[rows, :]
            for h in range(HEADS):
                cols = slice(h * GROUP, (h + 1) * GROUP)
                dvn_v[rows, cols] = _tn(wt[h], dsv_bf[rows, cols])
                dws_ref[h] += _nt(dsv_bf[rows, cols], vn_bf[rows, cols])
        dvn = dvn_v[...]
        dgain_ref[...] += _col_sum(dvn * vhat)
        dbias_ref[...] += _col_sum(dvn)
        dvhat = dvn * gain_v
        dv = rstd * (dvhat - _row_mean(dvhat) - vhat * _row_mean(dvhat * vhat))
        dzpre = (jnp.concatenate([du, dv], axis=1) * gelu_d).astype(BF16)
        dz_ref[...] = dzpre
        dh = _nn(dzpre, win_v[...])
        dxr, dg_row = _rms_bwd(dh, xhat, r, gv)
        dg_ref[...] += dg_row
        dx = dov + dxr
        dx_ref[...] = dx
        dxb_ref[...] = dx.astype(BF16)

        @pl.when(i == n_steps - 1)
        def _():
            rr = lax.broadcasted_iota(jnp.int32, (CHUNK, CHUNK), 0)
            cc = lax.broadcasted_iota(jnp.int32, (CHUNK, CHUNK), 1)
            for h in range(HEADS):
                dws_ref[h] = jnp.where(rr >= cc, dws_ref[h], 0.0)

    return _hosting_call(
        body, "mixer_a_bwd", n_steps, [dout, x, zpre, g, gain, bias, ws, bfull, gath],
        in_specs=[_row_spec(tm, d), _row_spec(tm, d), _row_spec(tm, e2), _const_spec((1, d)),
                  _const_spec((1, e)), _const_spec((1, e)), _const_spec((HEADS, CHUNK, CHUNK)),
                  _const_spec((CHUNK, e)), ANY],
        out_specs=[_row_spec(tm, d), _row_spec(tm, d), _row_spec(tm, d), _row_spec(tm, e2),
                   _const_spec((1, d)), _const_spec((1, e)), _const_spec((1, e)),
                   _const_spec((HEADS, CHUNK, CHUNK)), _const_spec((CHUNK, e))],
        out_shape=[jax.ShapeDtypeStruct((t_tok, d), F32), jax.ShapeDtypeStruct((t_tok, d), BF16),
                   jax.ShapeDtypeStruct((t_tok, d), BF16), jax.ShapeDtypeStruct((t_tok, e2), BF16),
                   jax.ShapeDtypeStruct((1, d), F32), jax.ShapeDtypeStruct((1, e), F32),
                   jax.ShapeDtypeStruct((1, e), F32), jax.ShapeDtypeStruct((HEADS, CHUNK, CHUNK), F32),
                   jax.ShapeDtypeStruct((CHUNK, e), F32)],
        scratch=[pltpu.VMEM((e2, d), BF16), pltpu.VMEM((e, d), BF16), pltpu.VMEM((tm, e), F32),
                 pltpu.VMEM((tm, e), F32), pltpu.SemaphoreType.DMA((2 * N_DEV,))],
        hosted=hosted)


def _ffn_fwd(x, g, srcs, nf, tm, name, hosted=()):
    t_tok, d = x.shape
    f = nf * N_DEV
    firsts = [first for _, first in srcs]

    def body(x_ref, g_ref, sg_ref, su_ref, sd_ref, xo_ref, gate_ref, up_ref, wg_v, wu_v, wd_v, sems):
        _load_group(
            [(sg_ref, firsts[0], nf, wg_v), (su_ref, firsts[1], nf, wu_v), (sd_ref, firsts[2], nf, wd_v)], sems)
        xv = x_ref[...]
        h = _rms_fwd(xv, g_ref[...])[0].astype(BF16)
        gate = _nt(h, wg_v[...])
        up = _nt(h, wu_v[...])
        gate_ref[...] = gate.astype(BF16)
        up_ref[...] = up.astype(BF16)
        act = (gate * _sigmoid(gate) * up).astype(BF16)
        xo_ref[...] = xv + _nn(act, wd_v[...])

    return _hosting_call(
        body, name, t_tok // tm, [x, g] + [arr for arr, _ in srcs],
        in_specs=[_row_spec(tm, d), _const_spec((1, d)), ANY, ANY, ANY],
        out_specs=[_row_spec(tm, d), _row_spec(tm, f), _row_spec(tm, f)],
        out_shape=[jax.ShapeDtypeStruct((t_tok, d), F32), jax.ShapeDtypeStruct((t_tok, f), BF16),
                   jax.ShapeDtypeStruct((t_tok, f), BF16)],
        scratch=[pltpu.VMEM((f, d), BF16), pltpu.VMEM((f, d), BF16), pltpu.VMEM((f, d), BF16),
                 pltpu.SemaphoreType.DMA((3 * N_DEV,))],
        hosted=hosted)


def _ffn_bwd(dout, x, gate, up, g, srcs, nf, tm, name, hosted=()):
    t_tok, d = x.shape
    f = nf * N_DEV
    firsts = [first for _, first in srcs]
    per_chunk = -(-f // (FFN_CHUNKS * MXU_WIDTH)) * MXU_WIDTH
    bounds = [min(ck * per_chunk, f) for ck in range(FFN_CHUNKS + 1)]

    def body(dout_ref, x_ref, gate_ref, up_ref, g_ref, sg_ref, su_ref, sd_ref,
             dx_ref, dxb_ref, h_ref, act_ref, dgu_ref, dg_ref, wg_v, wu_v, wd_v, sems):
        _load_group(
            [(sg_ref, firsts[0], nf, wg_v), (su_ref, firsts[1], nf, wu_v), (sd_ref, firsts[2], nf, wd_v)], sems)

        @pl.when(pl.program_id(0) == 0)
        def _():
            dg_ref[...] = jnp.zeros_like(dg_ref)

        xv = x_ref[...]
        gv = g_ref[...]
        hv, xhat, r = _rms_fwd(xv, gv)
        h_ref[...] = hv.astype(BF16)
        dov = dout_ref[...]
        dob = dov.astype(BF16)
        dh = None
        for ck in range(FFN_CHUNKS):
            cols = slice(bounds[ck], bounds[ck + 1])
            gate_v = gate_ref[:, cols].astype(F32)
            up_v = up_ref[:, cols].astype(F32)
            sig = _sigmoid(gate_v)
            silu = gate_v * sig
            act_ref[:, cols] = (silu * up_v).astype(BF16)
            dact = _nt(dob, wd_v[cols, :])
            dup = (dact * silu).astype(BF16)
            dgate = (dact * up_v * (sig * (1.0 + gate_v * (1.0 - sig)))).astype(BF16)
            dgu_ref[:, cols] = dgate
            dgu_ref[:, f + bounds[ck]:f + bounds[ck + 1]] = dup
            part = _nn(dgate, wg_v[cols, :]) + _nn(dup, wu_v[cols, :])
            dh = part if dh is None else dh + part
        dxr, dg_row = _rms_bwd(dh, xhat, r, gv)
        dg_ref[...] += dg_row
        dx = dov + dxr
        dx_ref[...] = dx
        dxb_ref[...] = dx.astype(BF16)

    return _hosting_call(
        body, name, t_tok // tm, [dout, x, gate, up, g] + [arr for arr, _ in srcs],
        in_specs=[_row_spec(tm, d), _row_spec(tm, d), _row_spec(tm, f), _row_spec(tm, f), _const_spec((1, d)),
                  ANY, ANY, ANY],
        out_specs=[_row_spec(tm, d), _row_spec(tm, d), _row_spec(tm, d), _row_spec(tm, f), _row_spec(tm, 2 * f),
                   _const_spec((1, d))],
        out_shape=[jax.ShapeDtypeStruct((t_tok, d), F32), jax.ShapeDtypeStruct((t_tok, d), BF16),
                   jax.ShapeDtypeStruct((t_tok, d), BF16), jax.ShapeDtypeStruct((t_tok, f), BF16),
                   jax.ShapeDtypeStruct((t_tok, 2 * f), BF16), jax.ShapeDtypeStruct((1, d), F32)],
        scratch=[pltpu.VMEM((f, d), BF16), pltpu.VMEM((f, d), BF16), pltpu.VMEM((f, d), BF16),
                 pltpu.SemaphoreType.DMA((3 * N_DEV,))],
        hosted=hosted)


def _shift_down(z, k, prev_rows):
    row = lax.broadcasted_iota(jnp.int32, z.shape, 0)
    out = pltpu.roll(z, k, 0)
    for j in range(k):
        out = jnp.where(row == j, prev_rows[j], out)
    return out


def _shift_up(z, k, next_rows):
    tm = z.shape[0]
    row = lax.broadcasted_iota(jnp.int32, z.shape, 0)
    out = pltpu.roll(z, tm - k, 0)
    for j in range(k):
        out = jnp.where(row == tm - k + j, next_rows[j], out)
    return out


def _mixer_b_fwd(x, g, gath, conv_w, tm, seq, hosted=()):
    t_tok, d = x.shape
    e = conv_w.shape[1]
    e3 = 3 * e
    n_in, n_out = e3 // N_DEV, e // N_DEV
    tiles_per_seq = seq // tm

    def body(x_ref, g_ref, cw_ref, gath_ref, xo_ref, p_ref, win_v, wout_v, tail_v, sems):
        i = pl.program_id(0)
        _load_group([(gath_ref, 0, n_in, win_v), (gath_ref, n_in, n_out, wout_v)], sems)

        @pl.when(i % tiles_per_seq == 0)
        def _():
            tail_v[...] = jnp.zeros_like(tail_v)

        xv = x_ref[...]
        h = _rms_fwd(xv, g_ref[...])[0].astype(BF16)
        p = _nt(h, win_v[...])
        p_ref[...] = p.astype(BF16)
        z = p[:, e:2 * e] * p[:, 2 * e:]
        prev = [tail_v[SUBLANES - 2:SUBLANES - 1, :], tail_v[SUBLANES - 1:SUBLANES, :]]
        conv = (cw_ref[2:3, :] * z + cw_ref[1:2, :] * _shift_down(z, 1, prev[1:])
                + cw_ref[0:1, :] * _shift_down(z, 2, prev))
        tail_v[...] = z[tm - SUBLANES:, :]
        y = (p[:, :e] * conv).astype(BF16)
        xo_ref[...] = xv + _nn(y, wout_v[...])

    return _hosting_call(
        body, "mixer_b_fwd", t_tok // tm, [x, g, conv_w, gath],
        in_specs=[_row_spec(tm, d), _const_spec((1, d)), _const_spec((SUBLANES, e)), ANY],
        out_specs=[_row_spec(tm, d), _row_spec(tm, e3)],
        out_shape=[jax.ShapeDtypeStruct((t_tok, d), F32), jax.ShapeDtypeStruct((t_tok, e3), BF16)],
        scratch=[pltpu.VMEM((e3, d), BF16), pltpu.VMEM((e, d), BF16), pltpu.VMEM((SUBLANES, e), F32),
                 pltpu.SemaphoreType.DMA((2 * N_DEV,))],
        hosted=hosted)


def _mixer_b_bwd(dout, x, p, g, gath, conv_w, tm, seq, hosted=()):
    t_tok, d = x.shape
    e = conv_w.shape[1]
    e3 = 3 * e
    n_in, n_out = e3 // N_DEV, e // N_DEV
    tiles_per_seq = seq // tm
    halo_per_tile = tm // HALO
    n_halo = t_tok // HALO

    def body(dout_ref, dnext_ref, x_ref, p_ref, pprev_ref, pnext_ref, g_ref, cw_ref, gath_ref,
             dx_ref, dxb_ref, h_ref, y_ref, dp_ref, dg_ref, dcw_ref, win_v, wout_v, sems):
        i = pl.program_id(0)
        _load_group([(gath_ref, 0, n_in, win_v), (gath_ref, n_in, n_out, wout_v)], sems)

        @pl.when(i == 0)
        def _():
            dg_ref[...] = jnp.zeros_like(dg_ref)
            dcw_ref[...] = jnp.zeros_like(dcw_ref)

        first = (i % tiles_per_seq == 0).astype(F32)
        last = (i % tiles_per_seq == tiles_per_seq - 1).astype(F32)
        xv = x_ref[...]
        gv = g_ref[...]
        hv, xhat, r = _rms_fwd(xv, gv)
        h_ref[...] = hv.astype(BF16)
        pv = p_ref[...].astype(F32)
        bg, cg, hx = pv[:, :e], pv[:, e:2 * e], pv[:, 2 * e:]
        z = cg * hx
        pprev = pprev_ref[...].astype(F32)
        zprev = pprev[:, e:2 * e] * pprev[:, 2 * e:] * (1.0 - first)
        prev = [zprev[HALO - 2:HALO - 1, :], zprev[HALO - 1:HALO, :]]
        zs1 = _shift_down(z, 1, prev[1:])
        zs2 = _shift_down(z, 2, prev)
        w0, w1, w2 = cw_ref[0:1, :], cw_ref[1:2, :], cw_ref[2:3, :]
        conv = w2 * z + w1 * zs1 + w0 * zs2
        y_ref[...] = (bg * conv).astype(BF16)

        dov = dout_ref[...]
        wout_bf = wout_v[...]
        dy = _nt(dov.astype(BF16), wout_bf)
        dconv = dy * bg
        dnext = _nt(dnext_ref[...].astype(BF16), wout_bf) * pnext_ref[:, :e].astype(F32) * (1.0 - last)
        nxt = [dnext[0:1, :], dnext[1:2, :]]
        dz = w2 * dconv + w1 * _shift_up(dconv, 1, nxt[:1]) + w0 * _shift_up(dconv, 2, nxt)
        dcw_ref[0:1, :] += _col_sum(dconv * zs2)
        dcw_ref[1:2, :] += _col_sum(dconv * zs1)
        dcw_ref[2:3, :] += _col_sum(dconv * z)
        dp = jnp.concatenate([dy * conv, dz * hx, dz * cg], axis=1).astype(BF16)
        dp_ref[...] = dp
        dh = _nn(dp, win_v[...])
        dxr, dg_row = _rms_bwd(dh, xhat, r, gv)
        dg_ref[...] += dg_row
        dx = dov + dxr
        dx_ref[...] = dx
        dxb_ref[...] = dx.astype(BF16)

    prev_spec = lambda w: pl.BlockSpec((HALO, w), lambda i: (jnp.maximum(i * halo_per_tile - 1, 0), 0))
    next_spec = lambda w: pl.BlockSpec((HALO, w), lambda i: (jnp.minimum((i + 1) * halo_per_tile, n_halo - 1), 0))
    return _hosting_call(
        body, "mixer_b_bwd", t_tok // tm, [dout, dout, x, p, p, p, g, conv_w, gath],
        in_specs=[_row_spec(tm, d), next_spec(d), _row_spec(tm, d), _row_spec(tm, e3), prev_spec(e3), next_spec(e3),
                  _const_spec((1, d)), _const_spec((SUBLANES, e)), ANY],
        out_specs=[_row_spec(tm, d), _row_spec(tm, d), _row_spec(tm, d), _row_spec(tm, e), _row_spec(tm, e3),
                   _const_spec((1, d)), _const_spec((SUBLANES, e))],
        out_shape=[jax.ShapeDtypeStruct((t_tok, d), F32), jax.ShapeDtypeStruct((t_tok, d), BF16),
                   jax.ShapeDtypeStruct((t_tok, d), BF16), jax.ShapeDtypeStruct((t_tok, e), BF16),
                   jax.ShapeDtypeStruct((t_tok, e3), BF16), jax.ShapeDtypeStruct((1, d), F32),
                   jax.ShapeDtypeStruct((SUBLANES, e), F32)],
        scratch=[pltpu.VMEM((e3, d), BF16), pltpu.VMEM((e, d), BF16), pltpu.SemaphoreType.DMA((2 * N_DEV,))],
        hosted=hosted)


def _loss_head(x, target, g, tm):
    t_tok, d = x.shape

    def body(x_ref, t_ref, g_ref, loss_ref, dx_ref, dxb_ref, dg_ref):
        @pl.when(pl.program_id(0) == 0)
        def _():
            loss_ref[...] = jnp.zeros_like(loss_ref)
            dg_ref[...] = jnp.zeros_like(dg_ref)

        gv = g_ref[...]
        y, xhat, r = _rms_fwd(x_ref[...], gv)
        err = y - t_ref[...]
        loss_ref[...] += 0.5 * jnp.sum(_row_mean(err * err), axis=0, keepdims=True)
        dxr, dg_row = _rms_bwd(err * (1.0 / d), xhat, r, gv)
        dg_ref[...] += dg_row
        dx_ref[...] = dxr
        dxb_ref[...] = dxr.astype(BF16)

    return pl.pallas_call(
        body, name="loss_head", grid=(t_tok // tm,),
        in_specs=[_row_spec(tm, d), _row_spec(tm, d), _const_spec((1, d))],
        out_specs=[_const_spec((1, 1)), _row_spec(tm, d), _row_spec(tm, d), _const_spec((1, d))],
        out_shape=[jax.ShapeDtypeStruct((1, 1), F32), jax.ShapeDtypeStruct((t_tok, d), F32),
                   jax.ShapeDtypeStruct((t_tok, d), BF16), jax.ShapeDtypeStruct((1, d), F32)],
        compiler_params=_params(),
    )(x, target, g)


def _wgrad(a, b, bm, name, hosted=()):
    t_tok, m = a.shape
    n = b.shape[1]

    def body(a_ref, b_ref, o_ref):
        o_ref[...] = _tn(a_ref[...], b_ref[...]).astype(o_ref.dtype)

    outs, h_outs = _hosting_call(
        body, name, m // bm, [a, b],
        in_specs=[pl.BlockSpec((t_tok, bm), lambda i: (0, i)), _const_spec((t_tok, n))],
        out_specs=[pl.BlockSpec((bm, n), lambda i: (i, 0))],
        out_shape=[jax.ShapeDtypeStruct((m, n), BF16)],
        scratch=[], hosted=hosted)
    return (outs[0], h_outs) if hosted else outs[0]


def _sum_slots(land, rb, name):
    n_slots, rows, cols = land.shape

    def body(l_ref, o_ref):
        acc = l_ref[0].astype(F32)
        for k in range(1, n_slots):
            acc = acc + l_ref[k].astype(F32)
        o_ref[...] = acc

    return pl.pallas_call(
        body, name=name, grid=(rows // rb,),
        in_specs=[pl.BlockSpec((n_slots, rb, cols), lambda i: (0, i, 0))],
        out_specs=pl.BlockSpec((rb, cols), lambda i: (i, 0)),
        out_shape=jax.ShapeDtypeStruct((rows, cols), F32),
        compiler_params=_params(sequential=False),
    )(land)


def _adamw(w, grad, m, v, rb, name):
    rows, cols = w.shape
    c1 = 1.0 / (1.0 - ADAM_B1 ** ADAM_STEP)
    c2 = 1.0 / (1.0 - ADAM_B2 ** ADAM_STEP)

    def body(w_ref, g_ref, m_ref, v_ref, d_ref, mo_ref, vo_ref):
        gv = g_ref[...]
        mn = ADAM_B1 * m_ref[...] + (1.0 - ADAM_B1) * gv
        vn = ADAM_B2 * v_ref[...] + (1.0 - ADAM_B2) * (gv * gv)
        mo_ref[...] = mn
        vo_ref[...] = vn
        d_ref[...] = -ADAM_LR * ((mn * c1) / (jnp.sqrt(vn * c2) + ADAM_EPS) + ADAM_WD * w_ref[...])

    spec = pl.BlockSpec((rb, cols), lambda i: (i, 0))
    shape = jax.ShapeDtypeStruct((rows, cols), F32)
    return pl.pallas_call(
        body, name=name, grid=(rows // rb,),
        in_specs=[spec] * 4, out_specs=[spec] * 3, out_shape=[shape] * 3,
        compiler_params=_params(sequential=False),
    )(w, grad, m, v)


def _split_bf16(a):
    hi = a.astype(BF16)
    rest = a - hi.astype(F32)
    mid = rest.astype(BF16)
    return hi, mid, (rest - mid.astype(F32)).astype(BF16)


def _reduce_adamw(lands, w, m, v, transpose, name):
    n_layers, rows_w, cols_w = w.shape
    c1 = 1.0 / (1.0 - ADAM_B1 ** ADAM_STEP)
    c2 = 1.0 / (1.0 - ADAM_B2 ** ADAM_STEP)
    if transpose:
        n, tiles = cols_w, rows_w // MXU_WIDTH
        blk = (MXU_WIDTH, n)
        land_specs = [pl.BlockSpec((N_CHIP, n, MXU_WIDTH), lambda i, b=first // n: (0, b, i % tiles))
                      for _, first in lands]
    else:
        n, tiles = rows_w, 2
        blk = (n // tiles, cols_w)
        land_specs = [pl.BlockSpec((N_CHIP,) + blk, lambda i, b=first // blk[0]: (0, b + i % tiles, 0))
                      for _, first in lands]
    for _, first in lands:
        assert first % (n if transpose else blk[0]) == 0

    def body(*refs):
        land_refs = refs[:n_layers]
        w_ref, m_ref, v_ref, g_ref, d_ref, mo_ref, vo_ref = refs[n_layers:]
        layer = pl.program_id(0) // tiles

        def total(ref):
            acc = ref[0].astype(F32)
            for q in range(1, N_CHIP):
                acc = acc + ref[q].astype(F32)
            return acc

        gv = total(land_refs[0])
        for k in range(1, n_layers):
            gv = jnp.where(layer == k, total(land_refs[k]), gv)
        if transpose:
            r = lax.broadcasted_iota(jnp.int32, (MXU_WIDTH, MXU_WIDTH), 0)
            c = lax.broadcasted_iota(jnp.int32, (MXU_WIDTH, MXU_WIDTH), 1)
            eye = (r == c).astype(BF16)
            hi, mid, lo = _split_bf16(gv)
            gv = _nt(eye, hi) + _nt(eye, mid) + _nt(eye, lo)
        g_ref[...] = gv
        mn = ADAM_B1 * m_ref[...] + (1.0 - ADAM_B1) * gv
        vn = ADAM_B2 * v_ref[...] + (1.0 - ADAM_B2) * (gv * gv)
        mo_ref[...] = mn
        vo_ref[...] = vn
        d_ref[...] = -ADAM_LR * ((mn * c1) / (jnp.sqrt(vn * c2) + ADAM_EPS) + ADAM_WD * w_ref[...])

    spec = pl.BlockSpec((None,) + blk, lambda i: (i // tiles, i % tiles, 0))
    shape = jax.ShapeDtypeStruct(w.shape, F32)
    return pl.pallas_call(
        body, name=name, grid=(n_layers * tiles,),
        in_specs=land_specs + [spec] * 3, out_specs=[spec] * 4, out_shape=[shape] * 4,
        compiler_params=_params(sequential=False),
    )(*[land for land, _ in lands], w, m, v)


def _pack_small(parts, rows):
    flat = jnp.concatenate([p.reshape(-1).astype(F32) for p in parts])
    return jnp.pad(flat, (0, rows * LANES - flat.shape[0])).reshape(rows, LANES)


def _unpack_small(packed, shapes):
    flat = packed.reshape(-1)
    out = []
    pos = 0
    for s in shapes:
        n = math.prod(s)
        out.append(flat[pos:pos + n].reshape(s))
        pos += n
    return out


def kernel(x, mix_norm, ffn_norm, a_w_in, a_v_gain, a_v_bias, a_w_s, a_b_s, a_w_out, b_w_in, b_conv_w, b_w_out, ffn_w_gate, ffn_w_up, ffn_w_down, final_norm, loss_target, m_mix_norm, m_ffn_norm, m_a_w_in, m_a_v_gain, m_a_v_bias, m_a_w_s, m_a_b_s, m_a_w_out, m_b_w_in, m_b_conv_w, m_b_w_out, m_ffn_w_gate, m_ffn_w_up, m_ffn_w_down, m_final_norm, v_mix_norm, v_ffn_norm, v_a_w_in, v_a_v_gain, v_a_v_bias, v_a_w_s, v_a_b_s, v_a_w_out, v_b_w_in, v_b_conv_w, v_b_w_out, v_ffn_w_gate, v_ffn_w_up, v_ffn_w_down, v_final_norm):
    bsz, seq, d = x.shape
    t_tok = bsz * seq
    me = _my_index()
    xt = x.reshape(t_tok, d)
    target = loss_target.reshape(t_tok, d)
    e_a = a_v_gain.shape[1]
    e_b = b_w_out.shape[1] * N_DEV
    n_layers = ffn_w_gate.shape[0]
    f_shard = ffn_w_gate.shape[2]
    f_full = f_shard * N_DEV

    conv_pad = jnp.pad(b_conv_w[0], ((0, SUBLANES - CONV_W), (0, 0)))
    sh_a = jnp.concatenate([a_w_in[0].T, a_w_out[0]]).astype(BF16)
    sh_b = jnp.concatenate([b_w_in[0].T, b_w_out[0]]).astype(BF16)
    sh_f0 = jnp.concatenate([ffn_w_gate[0].T, ffn_w_up[0].T, ffn_w_down[0]]).astype(BF16)
    sh_f1g = ffn_w_gate[1].T.astype(BF16)
    sh_f1ud = jnp.concatenate([ffn_w_up[1].T, ffn_w_down[1]]).astype(BF16)
    bfull = jnp.repeat(a_b_s[0].T, GROUP, axis=1)

    gath_a, conv_g = _exchange([_HostedGathers([sh_a, conv_pad], 0)], "gather_a")
    conv_full = jnp.pad(conv_g[:, :CONV_W, :].transpose(1, 0, 2).reshape(CONV_W, e_b), ((0, SUBLANES - CONV_W), (0, 0)))
    (x1, zpre, y_a), (gath_f0,) = _mixer_a_fwd(xt, mix_norm[0:1], gath_a, a_v_gain, a_v_bias, a_w_s[0], bfull, tm=256,
                                               hosted=[_HostedGathers([sh_f0], mid_lead=2)])
    srcs0 = [(gath_f0, 0), (gath_f0, f_shard), (gath_f0, 2 * f_shard)]
    (x2, gate0, up0), (gath_b, gath_f1g) = _ffn_fwd(x1, ffn_norm[0:1], srcs0, f_shard, tm=256, name="ffn_fwd0",
                                                    hosted=[_HostedGathers([sh_b, sh_f1g], mid_lead=2)])
    (x3, p_b), (gath_f1ud,) = _mixer_b_fwd(x2, mix_norm[1:2], gath_b, conv_full, tm=256, seq=seq,
                                           hosted=[_HostedGathers([sh_f1ud], mid_lead=2)])
    srcs1 = [(gath_f1g, 0), (gath_f1ud, 0), (gath_f1ud, f_shard)]
    (x4, gate1, up1), _ = _ffn_fwd(x3, ffn_norm[1:2], srcs1, f_shard, tm=256, name="ffn_fwd1")
    loss_part, dx4, dx4_bf, d_final = _loss_head(x4, target, final_norm.reshape(1, d), tm=512)

    ffn_entries = [(0, 0, f_shard), (0, f_full, f_shard), (1, 0, f_shard)]
    (dx3, dx3_bf, h_f1, act1, dgu1, d_fn1), _ = _ffn_bwd(dx4, x3, gate1, up1, ffn_norm[1:2], srcs1, f_shard, tm=256,
                                                         name="ffn_bwd1")
    g_down1 = _wgrad(act1, dx4_bf, 256, "wgrad_down1")
    g_gu1 = _wgrad(dgu1, h_f1, 512, "wgrad_gate_up1")
    ps_f1 = _pair_reduce([g_gu1, g_down1], ffn_entries, "pair_reduce_f1")
    (dx2, dx2_bf, h_b, y_b, dp_b, d_mn1, d_conv), (land_f1gu,) = _mixer_b_bwd(
        dx3, x2, p_b, mix_norm[1:2], gath_b, conv_full, tm=256, seq=seq,
        hosted=[_HostedChipScatter(ps_f1, 0, 2 * f_shard)])
    g_b_out = _wgrad(y_b, dx3_bf, 256, "wgrad_b_out")
    g_b_in = _wgrad(dp_b, h_b, 512, "wgrad_b_in")
    ps_b = _pair_reduce([g_b_in, g_b_out], [(0, 0, b_w_in.shape[2]), (1, 0, b_w_out.shape[1])], "pair_reduce_b")
    (dx1, dx1_bf, h_f0, act0, dgu0, d_fn0), (land_f1d, land_b) = _ffn_bwd(
        dx2, x1, gate0, up0, ffn_norm[0:1], srcs0, f_shard, tm=256, name="ffn_bwd0",
        hosted=[_HostedChipScatter(ps_f1, 2 * f_shard, f_shard), _HostedChipScatter(ps_b)])
    g_down0 = _wgrad(act0, dx2_bf, 256, "wgrad_down0")
    g_gu0 = _wgrad(dgu0, h_f0, 512, "wgrad_gate_up0")
    ps_f0 = _pair_reduce([g_gu0, g_down0], ffn_entries, "pair_reduce_f0")
    g_a_out = _wgrad(y_a, dx1_bf, 256, "wgrad_a_out")
    ps_ao = _pair_reduce([g_a_out], [(0, 0, a_w_out.shape[1])], "pair_reduce_a_out")
    (dx0, _, h_a, dz_a, d_mn0, d_gain, d_bias, d_ws, d_bs_acc), (land_f0, land_ao) = _mixer_a_bwd(
        dx1, xt, zpre, mix_norm[0:1], gath_a, a_v_gain, a_v_bias, a_w_s[0], bfull, tm=256,
        hosted=[_HostedChipScatter(ps_f0), _HostedChipScatter(ps_ao)])
    d_bs = d_bs_acc.reshape(CHUNK, HEADS, GROUP).sum(axis=2).T

    small_grads = [jnp.concatenate([d_mn0, d_mn1]), jnp.concatenate([d_fn0, d_fn1]), d_gain, d_bias, d_ws, d_bs,
                   d_final, d_conv[:CONV_W], loss_part]
    small_shapes = [(n_layers, d), (n_layers, d), (1, e_a), (1, e_a), (1, HEADS, CHUNK, CHUNK), (1, HEADS, CHUNK), (d,),
                    (CONV_W, e_b), ()]
    n_small = sum(math.prod(s) for s in small_shapes)
    blk_rows = -(-n_small // (N_DEV * LANES * SUBLANES)) * SUBLANES
    small_rows = blk_rows * N_DEV
    packed = _pack_small(small_grads, small_rows)
    g_a_in, (small_land,) = _wgrad(dz_a, h_a, 512, "wgrad_a_in", hosted=[_HostedScatterAll(packed)])
    ps_ai = _pair_reduce([g_a_in], [(0, 0, a_w_in.shape[2])], "pair_reduce_a_in")
    small_sum = _sum_slots(small_land, blk_rows, "sum_small")
    land_ai, small_gath = _exchange([_HostedChipScatter(ps_ai), _HostedGathers([small_sum], 0)], "tail_exchange")
    small_all = small_gath.reshape(small_rows, LANES)

    (gr_mix, gr_ffn, gr_gain, gr_bias, gr_ws, gr_bs, gr_final, gr_conv_full, loss) = _unpack_small(small_all, small_shapes)
    gr_conv = lax.dynamic_slice_in_dim(gr_conv_full, me * (e_b // N_DEV), e_b // N_DEV, axis=1)[None]

    small_w = [mix_norm, ffn_norm, a_v_gain, a_v_bias, a_w_s, a_b_s, final_norm]
    small_m = [m_mix_norm, m_ffn_norm, m_a_v_gain, m_a_v_bias, m_a_w_s, m_a_b_s, m_final_norm]
    small_v = [v_mix_norm, v_ffn_norm, v_a_v_gain, v_a_v_bias, v_a_w_s, v_a_b_s, v_final_norm]
    small_g = [gr_mix, gr_ffn, gr_gain, gr_bias, gr_ws, gr_bs, gr_final]
    sm_shapes = small_shapes[:len(small_w)]
    sm_out = _adamw(_pack_small(small_w, small_rows), _pack_small(small_g, small_rows), _pack_small(small_m, small_rows),
                    _pack_small(small_v, small_rows), small_rows, "adamw_small")
    sm_delta, sm_m, sm_v = [_unpack_small(o, sm_shapes) for o in sm_out]

    conv_out = _adamw(b_conv_w[0], gr_conv[0], m_b_conv_w[0], v_b_conv_w[0], CONV_W, "adamw_conv")
    conv_delta, conv_m, conv_v = [o[None] for o in conv_out]

    n_b_in = b_w_in.shape[2]
    res = {
        "a_w_in": _reduce_adamw([(land_ai, 0)], a_w_in, m_a_w_in, v_a_w_in, True, "adamw_a_in"),
        "a_w_out": _reduce_adamw([(land_ao, 0)], a_w_out, m_a_w_out, v_a_w_out, False, "adamw_a_out"),
        "b_w_in": _reduce_adamw([(land_b, 0)], b_w_in, m_b_w_in, v_b_w_in, True, "adamw_b_in"),
        "b_w_out": _reduce_adamw([(land_b, n_b_in)], b_w_out, m_b_w_out, v_b_w_out, False, "adamw_b_out"),
        "ffn_w_gate": _reduce_adamw([(land_f0, 0), (land_f1gu, 0)], ffn_w_gate, m_ffn_w_gate, v_ffn_w_gate, True,
                                    "adamw_gate"),
        "ffn_w_up": _reduce_adamw([(land_f0, f_shard), (land_f1gu, f_shard)], ffn_w_up, m_ffn_w_up, v_ffn_w_up, True,
                                  "adamw_up"),
        "ffn_w_down": _reduce_adamw([(land_f0, 2 * f_shard), (land_f1d, 0)], ffn_w_down, m_ffn_w_down, v_ffn_w_down,
                                    False, "adamw_down"),
    }

    order = ["mix_norm", "ffn_norm", "a_w_in", "a_v_gain", "a_v_bias", "a_w_s", "a_b_s", "a_w_out", "b_w_in",
             "b_conv_w", "b_w_out", "ffn_w_gate", "ffn_w_up", "ffn_w_down", "final_norm"]
    small_names = ["mix_norm", "ffn_norm", "a_v_gain", "a_v_bias", "a_w_s", "a_b_s", "final_norm"]
    grads = {"b_conv_w": gr_conv}
    deltas, new_m, new_v = {}, {}, {}
    for k, name in enumerate(small_names):
        grads[name] = small_g[k]
        deltas[name], new_m[name], new_v[name] = sm_delta[k], sm_m[k], sm_v[k]
    deltas["b_conv_w"], new_m["b_conv_w"], new_v["b_conv_w"] = conv_delta, conv_m, conv_v
    for name, (gg, dl, mm, vv) in res.items():
        grads[name], deltas[name], new_m[name], new_v[name] = gg, dl, mm, vv

    grad_x = dx0.reshape(bsz, seq, d)
    return (loss, grad_x, *[grads[n] for n in order], *[deltas[n] for n in order],
            *[new_m[n] for n in order], *[new_v[n] for n in order])
```

```python
import math

import jax
import jax.numpy as jnp
from jax import lax
from jax.experimental import pallas as pl
from jax.experimental.pallas import tpu as pltpu

F32 = jnp.float32
BF16 = jnp.bfloat16

N_DEV = 8
N_CHIP = 4
CHUNK = 128
HEADS = 16
GROUP = 128
CONV_W = 3
NORM_EPS = 1e-6
GELU_C = math.sqrt(2.0 / math.pi)
GELU_K = 0.044715

ADAM_LR = 0.001
ADAM_B1 = 0.9
ADAM_B2 = 0.999
ADAM_EPS = 1e-08
ADAM_WD = 0.01
ADAM_STEP = 10

LANES = 128
SUBLANES = 8
VMEM_LIMIT = 60 * 1024 * 1024
HALO = 16
MXU_WIDTH = 256
FFN_CHUNKS = 2

MESH = pl.DeviceIdType.MESH
ANY = pl.BlockSpec(memory_space=pl.ANY)


def _params(sequential=True):
    return pltpu.CompilerParams(
        dimension_semantics=("arbitrary",) if sequential else None,
        vmem_limit_bytes=VMEM_LIMIT)


def _nn(a, b):
    return jnp.dot(a, b, preferred_element_type=F32)


def _nt(a, b):
    return lax.dot_general(a, b, (((1,), (1,)), ((), ())), preferred_element_type=F32)


def _tn(a, b):
    return lax.dot_general(a, b, (((0,), (0,)), ((), ())), preferred_element_type=F32)


def _row_mean(a):
    return jnp.mean(a, axis=-1, keepdims=True)


def _col_sum(a):
    return jnp.sum(a, axis=0, keepdims=True)


def _rms_fwd(x, g):
    r = lax.rsqrt(_row_mean(x * x) + NORM_EPS)
    xhat = x * r
    return xhat * g, xhat, r


def _rms_bwd(dh, xhat, r, g):
    a = dh * g
    dx = r * (a - xhat * _row_mean(a * xhat))
    return dx, _col_sum(dh * xhat)


def _gelu_and_grad(x):
    x2 = x * x
    t = jnp.tanh(x * (GELU_C + (GELU_C * GELU_K) * x2))
    half = 0.5 * t + 0.5
    d = half + x * (0.5 - 0.5 * (t * t)) * (GELU_C + (3.0 * GELU_C * GELU_K) * x2)
    return x * half, d


def _sigmoid(x):
    return 1.0 / (1.0 + jnp.exp(-x))


def _row_spec(tm, width):
    return pl.BlockSpec((tm, width), lambda i: (i, 0))


def _const_spec(shape):
    nd = len(shape)
    return pl.BlockSpec(shape, lambda i: (0,) * nd)


def _load_group(parts, sems):
    @pl.when(pl.program_id(0) == 0)
    def _():
        copies = []
        for k, (gath_ref, first, n, dst) in enumerate(parts):
            for j in range(N_DEV):
                copies.append(pltpu.make_async_copy(gath_ref.at[j, pl.ds(first, n), :], dst.at[pl.ds(j * n, n), :],
                                                    sems.at[k * N_DEV + j]))
        for cp in copies:
            cp.start()
        for cp in copies:
            cp.wait()


def _hosting_call(body, name, n_steps, arrays, in_specs, out_specs, out_shape, scratch, hosted=()):
    n_in, n_out, n_scr = len(arrays), len(out_shape), len(scratch)
    h_arrays = [a for h in hosted for a in h.arrays]
    h_shapes = [s for h in hosted for s in h.out_shapes]
    h_sems = [s for h in hosted for s in h.sem_shapes]

    def full_body(*refs):
        pos = 0
        groups = []
        for n in (n_in, len(h_arrays), n_out, len(h_shapes), n_scr, len(h_sems)):
            groups.append(refs[pos:pos + n])
            pos += n
        own_in, h_in, own_out, h_out, own_scr, h_sem = groups
        per_host = []
        pi = po = ps = 0
        for h in hosted:
            ni, no, ns = len(h.arrays), len(h.out_shapes), len(h.sem_shapes)
            per_host.append((h, h_in[pi:pi + ni], h_out[po:po + no], h_sem[ps:ps + ns]))
            pi, po, ps = pi + ni, po + no, ps + ns
        for h, ins, outs, sems in per_host:
            h.begin(ins, outs, sems, n_steps)
        body(*own_in, *own_out, *own_scr)
        for h, ins, outs, sems in per_host:
            h.end(ins, outs, sems, n_steps)

    outs = pl.pallas_call(
        full_body, name=name, grid=(n_steps,),
        in_specs=list(in_specs) + [ANY] * len(h_arrays),
        out_specs=list(out_specs) + [ANY] * len(h_shapes),
        out_shape=list(out_shape) + h_shapes,
        scratch_shapes=list(scratch) + h_sems,
        compiler_params=_params(),
    )(*arrays, *h_arrays)
    return outs[:n_out], outs[n_out:]


def _my_index():
    return 4 * lax.axis_index("x") + 2 * lax.axis_index("y") + lax.axis_index("c")


GATHER_COPIES = 8


class _Gather:
    def __init__(self, shard, out, send_sems, recv_sems, local_sem):
        self.shard, self.out = shard, out
        self.send_sems, self.recv_sems, self.local_sem = send_sems, recv_sems, local_sem
        x, y, c = lax.axis_index("x"), lax.axis_index("y"), lax.axis_index("c")
        self.c = c
        self.me, self.sibling = (x, y, c), (x, y, 1 - c)
        self.xn, self.yn, self.dg = (1 - x, y), (x, 1 - y), (1 - x, 1 - y)
        self.n = shard.shape[0]
        self.half = self.n // 2
        rows_per_tile = SUBLANES * 4 // shard.dtype.itemsize
        self.relays = self.n % 2 == 0 and self.half % rows_per_tile == 0

    def _slot(self, dev, lo=0, hi=None):
        hi = self.n if hi is None else hi
        return self.out.at[4 * dev[0] + 2 * dev[1] + dev[2], pl.ds(lo, hi - lo), :]

    def _copy(self, k, block, to, src=None, lo=0, hi=None):
        return pltpu.make_async_remote_copy(
            src_ref=self._slot(block, lo, hi) if src is None else src, dst_ref=self._slot(block, lo, hi),
            send_sem=self.send_sems.at[k], recv_sem=self.recv_sems.at[k], device_id=to, device_id_type=MESH)

    def _local(self):
        return pltpu.make_async_copy(self.shard, self._slot(self.me), self.local_sem)

    def start(self):
        c = self.c
        self._local().start()
        self._copy(0, self.me, self.sibling, src=self.shard).start()
        self._copy(1, self.me, (*self.xn, c), src=self.shard).start()
        self._copy(2, self.me, (*self.yn, c), src=self.shard).start()
        if not self.relays:
            self._copy(3, self.me, (*self.dg, c), src=self.shard).start()

    def relay(self):
        c = self.c
        if self.relays:
            self._copy(1, (*self.xn, c), self.me).wait_recv()
            self._copy(3, (*self.xn, c), (*self.yn, c), hi=self.half).start()
            self._copy(2, (*self.yn, c), self.me).wait_recv()
            self._copy(4, (*self.yn, c), (*self.xn, c), lo=self.half).start()

    def forward(self):
        c = self.c
        if self.relays:
            self._copy(5, (*self.xn, c), self.sibling).start()
            self._copy(6, (*self.yn, c), self.sibling).start()
            self._copy(3, (*self.dg, c), self.me, hi=self.half).wait_recv()
            self._copy(4, (*self.dg, c), self.me, lo=self.half).wait_recv()
        else:
            self._copy(1, (*self.xn, c), self.me).wait_recv()
            self._copy(5, (*self.xn, c), self.sibling).start()
            self._copy(2, (*self.yn, c), self.me).wait_recv()
            self._copy(6, (*self.yn, c), self.sibling).start()
            self._copy(3, (*self.dg, c), self.me).wait_recv()
        self._copy(7, (*self.dg, c), self.sibling).start()

    def finish(self):
        c = self.c
        self._copy(0, self.sibling, self.me).wait_recv()
        for k, chip in ((5, self.xn), (6, self.yn), (7, self.dg)):
            self._copy(k, (*chip, 1 - c), self.me).wait_recv()
        for k in (0, 1, 2, 5, 6, 7):
            self._copy(k, self.me, self.sibling).wait_send()
        if self.relays:
            self._copy(3, self.me, self.sibling, hi=self.half).wait_send()
            self._copy(4, self.me, self.sibling, lo=self.half).wait_send()
        else:
            self._copy(3, self.me, self.sibling).wait_send()
        self._local().wait()


class _HostedGathers:
    def __init__(self, shards, mid_lead, relay_at=0.56):
        n = len(shards)
        self.arrays = shards
        self.mid_lead, self.relay_at = mid_lead, relay_at
        self.out_shapes = [jax.ShapeDtypeStruct((N_DEV,) + s.shape, s.dtype) for s in shards]
        self.sem_shapes = [pltpu.SemaphoreType.DMA((n, GATHER_COPIES)), pltpu.SemaphoreType.DMA((n, GATHER_COPIES)),
                           pltpu.SemaphoreType.DMA((n,))]

    def _gathers(self, ins, outs, sems):
        return [_Gather(ins[a], outs[a], sems[0].at[a], sems[1].at[a], sems[2].at[a]) for a in range(len(ins))]

    def begin(self, ins, outs, sems, n_steps):
        i = pl.program_id(0)
        forward_step = max(n_steps - 1 - self.mid_lead, 0)
        relay_step = min(int(self.relay_at * n_steps), forward_step)

        @pl.when(i == 0)
        def _():
            for g in self._gathers(ins, outs, sems):
                g.start()

        @pl.when(i == relay_step)
        def _():
            for g in self._gathers(ins, outs, sems):
                g.relay()

        @pl.when(i == forward_step)
        def _():
            for g in self._gathers(ins, outs, sems):
                g.forward()

    def end(self, ins, outs, sems, n_steps):
        @pl.when(pl.program_id(0) == n_steps - 1)
        def _():
            for g in self._gathers(ins, outs, sems):
                g.finish()


def _exchange(hosted, name):
    return _hosting_call(lambda: None, name, 1, [], [], [], [], [], hosted=hosted)[1]


class _ChipScatter:
    def __init__(self, pairsum, row0, land, send_sems, recv_sems, local_sem):
        self.pairsum, self.row0, self.land = pairsum, row0, land
        self.send_sems, self.recv_sems, self.local_sem = send_sems, recv_sems, local_sem
        x, y, c = lax.axis_index("x"), lax.axis_index("y"), lax.axis_index("c")
        self.c = c
        self.chip = 2 * x + y
        self.others = [(1 - x, y), (x, 1 - y), (1 - x, 1 - y)]

    def _src(self, chip):
        return self.pairsum.at[chip, pl.ds(self.row0, self.land.shape[1]), :]

    def _copy(self, k):
        ox, oy = self.others[k]
        return pltpu.make_async_remote_copy(
            src_ref=self._src(2 * ox + oy), dst_ref=self.land.at[self.chip],
            send_sem=self.send_sems.at[k], recv_sem=self.recv_sems.at[k], device_id=(ox, oy, self.c),
            device_id_type=MESH)

    def _arrival(self, k):
        ox, oy = self.others[k]
        return pltpu.make_async_remote_copy(
            src_ref=self._src(self.chip), dst_ref=self.land.at[2 * ox + oy],
            send_sem=self.send_sems.at[k], recv_sem=self.recv_sems.at[k], device_id=(ox, oy, self.c),
            device_id_type=MESH)

    def _local(self):
        return pltpu.make_async_copy(self._src(self.chip), self.land.at[self.chip], self.local_sem)

    def start(self):
        self._local().start()
        for k in range(N_CHIP - 1):
            self._copy(k).start()

    def finish(self):
        for k in range(N_CHIP - 1):
            self._arrival(k).wait_recv()
        for k in range(N_CHIP - 1):
            self._copy(k).wait_send()
        self._local().wait()


class _HostedChipScatter:
    def __init__(self, pairsum, row0=0, n=None):
        n = pairsum.shape[1] - row0 if n is None else n
        self.row0 = row0
        self.arrays = [pairsum]
        self.out_shapes = [jax.ShapeDtypeStruct((N_CHIP, n, pairsum.shape[2]), pairsum.dtype)]
        self.sem_shapes = [pltpu.SemaphoreType.DMA((N_CHIP - 1,)), pltpu.SemaphoreType.DMA((N_CHIP - 1,)),
                           pltpu.SemaphoreType.DMA(())]

    def begin(self, ins, outs, sems, n_steps):
        @pl.when(pl.program_id(0) == 0)
        def _():
            _ChipScatter(ins[0], self.row0, outs[0], *sems).start()

    def end(self, ins, outs, sems, n_steps):
        @pl.when(pl.program_id(0) == n_steps - 1)
        def _():
            _ChipScatter(ins[0], self.row0, outs[0], *sems).finish()


def _pair_reduce(arrays, entries, name):
    n_arr, n_ent = len(arrays), len(entries)
    cols = arrays[0].shape[1]
    offsets = []
    total = 0
    for _, _, n in entries:
        offsets.append(total)
        total += n

    def body(*refs):
        ins, out_ref = refs[:n_arr], refs[n_arr]
        rbuf, own, send_sems, recv_sems, own_sems = refs[n_arr + 1:]
        q = pl.program_id(0)
        x, y, c = lax.axis_index("x"), lax.axis_index("y"), lax.axis_index("c")

        def block(e, chip, core):
            ai, first, n = entries[e]
            return ins[ai].at[pl.ds(first + (2 * chip + core) * n, n), :]

        def to_sibling(e, chip):
            return pltpu.make_async_remote_copy(
                src_ref=block(e, chip, 1 - c), dst_ref=rbuf.at[chip, pl.ds(offsets[e], entries[e][2]), :],
                send_sem=send_sems.at[e, chip], recv_sem=recv_sems.at[e, chip], device_id=(x, y, 1 - c),
                device_id_type=MESH)

        @pl.when(q == 0)
        def _():
            for chip in range(N_CHIP):
                for e in range(n_ent):
                    to_sibling(e, chip).start()

        loads = [pltpu.make_async_copy(block(e, q, c), own.at[pl.ds(offsets[e], entries[e][2]), :], own_sems.at[e])
                 for e in range(n_ent)]
        for cp in loads:
            cp.start()
        for cp in loads:
            cp.wait()
        for e in range(n_ent):
            to_sibling(e, q).wait_recv()
        out_ref[...] = (own[...].astype(F32) + rbuf[q].astype(F32)).astype(out_ref.dtype)

        @pl.when(q == N_CHIP - 1)
        def _():
            for chip in range(N_CHIP):
                for e in range(n_ent):
                    to_sibling(e, chip).wait_send()

    return pl.pallas_call(
        body, name=name, grid=(N_CHIP,),
        in_specs=[ANY] * n_arr,
        out_specs=pl.BlockSpec((None, total, cols), lambda q: (q, 0, 0)),
        out_shape=jax.ShapeDtypeStruct((N_CHIP, total, cols), BF16),
        scratch_shapes=[pltpu.VMEM((N_CHIP, total, cols), BF16), pltpu.VMEM((total, cols), BF16),
                        pltpu.SemaphoreType.DMA((n_ent, N_CHIP)), pltpu.SemaphoreType.DMA((n_ent, N_CHIP)),
                        pltpu.SemaphoreType.DMA((n_ent,))],
        compiler_params=_params(),
    )(*arrays)


class _HostedScatterAll:
    def __init__(self, packed):
        n = packed.shape[0] // N_DEV
        self.n = n
        self.arrays = [packed]
        self.out_shapes = [jax.ShapeDtypeStruct((N_DEV, n, packed.shape[1]), packed.dtype)]
        self.sem_shapes = [pltpu.SemaphoreType.DMA((N_DEV - 1,)), pltpu.SemaphoreType.DMA((N_DEV - 1,)),
                           pltpu.SemaphoreType.DMA(())]

    def _copies(self, ins, outs, sems, with_arrivals):
        src, land = ins[0], outs[0]
        send_sems, recv_sems, local_sem = sems
        me = _my_index()

        def block(p):
            return src.at[pl.ds(p * self.n, self.n), :]

        local = pltpu.make_async_copy(block(me), land.at[me], local_sem)
        sends, arrivals = [], []
        for k in range(1, N_DEV):
            p = (me + k) % N_DEV
            q = (me + N_DEV - k) % N_DEV
            sends.append(pltpu.make_async_remote_copy(
                src_ref=block(p), dst_ref=land.at[me], send_sem=send_sems.at[k - 1], recv_sem=recv_sems.at[k - 1],
                device_id=(p // 4, (p // 2) % 2, p % 2), device_id_type=MESH))
            if with_arrivals:
                arrivals.append(pltpu.make_async_remote_copy(
                    src_ref=block(me), dst_ref=land.at[q], send_sem=send_sems.at[k - 1], recv_sem=recv_sems.at[k - 1],
                    device_id=(q // 4, (q // 2) % 2, q % 2), device_id_type=MESH))
        return local, sends, arrivals

    def begin(self, ins, outs, sems, n_steps):
        @pl.when(pl.program_id(0) == 0)
        def _():
            local, sends, _ = self._copies(ins, outs, sems, with_arrivals=False)
            local.start()
            for cp in sends:
                cp.start()

    def end(self, ins, outs, sems, n_steps):
        @pl.when(pl.program_id(0) == n_steps - 1)
        def _():
            local, sends, arrivals = self._copies(ins, outs, sems, with_arrivals=True)
            for cp in arrivals:
                cp.wait_recv()
            for cp in sends:
                cp.wait_send()
            local.wait()


def _tril_weights(ws_ref):
    r = lax.broadcasted_iota(jnp.int32, (CHUNK, CHUNK), 0)
    c = lax.broadcasted_iota(jnp.int32, (CHUNK, CHUNK), 1)
    return [jnp.where(r >= c, ws_ref[h], 0.0).astype(BF16) for h in range(HEADS)]


def _sgu_stats(zpre, gain, bias):
    e = zpre.shape[1] // 2
    z, dz = _gelu_and_grad(zpre)
    u, v = z[:, :e], z[:, e:]
    vc = v - _row_mean(v)
    rstd = lax.rsqrt(_row_mean(vc * vc) + NORM_EPS)
    vhat = vc * rstd
    return u, vhat, rstd, vhat * gain + bias, dz


def _spatial_fwd(wt, vn_bf, bfull_ref, sv_ref, tm):
    for ci in range(tm // CHUNK):
        rows = slice(ci * CHUNK, (ci + 1) * CHUNK)
        for h in range(HEADS):
            cols = slice(h * GROUP, (h + 1) * GROUP)
            sv_ref[rows, cols] = _nn(wt[h], vn_bf[rows, cols]) + bfull_ref[:, cols]


def _mixer_a_fwd(x, g, gath, gain, bias, ws, bfull, tm, hosted=()):
    t_tok, d = x.shape
    e = gain.shape[1]
    e2 = 2 * e
    n_in, n_out = e2 // N_DEV, e // N_DEV

    def body(x_ref, g_ref, gain_ref, bias_ref, ws_ref, bfull_ref, gath_ref,
             xo_ref, gd_ref, u_ref, vhat_ref, svo_ref, y_ref, rstd_ref, win_v, wout_v, sv_v, sems):
        _load_group([(gath_ref, 0, n_in, win_v), (gath_ref, n_in, n_out, wout_v)], sems)
        xv = x_ref[...]
        h = _rms_fwd(xv, g_ref[...])[0].astype(BF16)
        zpre = _nt(h, win_v[...])
        u, vhat, rstd, vn, gelu_d = _sgu_stats(zpre, gain_ref[...], bias_ref[...])
        gd_ref[...] = gelu_d.astype(BF16)
        u_ref[...] = u.astype(BF16)
        vhat_ref[...] = vhat.astype(BF16)
        rstd_ref[...] = rstd
        _spatial_fwd(_tril_weights(ws_ref), vn.astype(BF16), bfull_ref, sv_v, tm)
        sv = sv_v[...]
        svo_ref[...] = sv.astype(BF16)
        y = (u * sv).astype(BF16)
        y_ref[...] = y
        xo_ref[...] = xv + _nn(y, wout_v[...])

    return _hosting_call(
        body, "mixer_a_fwd", t_tok // tm, [x, g, gain, bias, ws, bfull, gath],
        in_specs=[_row_spec(tm, d), _const_spec((1, d)), _const_spec((1, e)), _const_spec((1, e)),
                  _const_spec((HEADS, CHUNK, CHUNK)), _const_spec((CHUNK, e)), ANY],
        out_specs=[_row_spec(tm, d), _row_spec(tm, e2), _row_spec(tm, e), _row_spec(tm, e), _row_spec(tm, e),
                   _row_spec(tm, e), _row_spec(tm, 1)],
        out_shape=[jax.ShapeDtypeStruct((t_tok, d), F32), jax.ShapeDtypeStruct((t_tok, e2), BF16),
                   jax.ShapeDtypeStruct((t_tok, e), BF16), jax.ShapeDtypeStruct((t_tok, e), BF16),
                   jax.ShapeDtypeStruct((t_tok, e), BF16), jax.ShapeDtypeStruct((t_tok, e), BF16),
                   jax.ShapeDtypeStruct((t_tok, 1), F32)],
        scratch=[pltpu.VMEM((e2, d), BF16), pltpu.VMEM((e, d), BF16), pltpu.VMEM((tm, e), F32),
                 pltpu.SemaphoreType.DMA((2 * N_DEV,))],
        hosted=hosted)


def _mixer_a_bwd(dout, x, gd, u_sav, vhat_sav, sv_sav, rstd_sav, g, gath, gain, bias, ws, tm, hosted=()):
    t_tok, d = x.shape
    e = gain.shape[1]
    e2 = 2 * e
    n_in, n_out = e2 // N_DEV, e // N_DEV
    n_steps = t_tok // tm

    def body(dout_ref, x_ref, gd_ref, u_ref, vhat_ref, sv_ref, rstd_ref, g_ref, gain_ref, bias_ref, ws_ref, gath_ref,
             dx_ref, dxb_ref, h_ref, dz_ref, dg_ref, dgain_ref, dbias_ref, dws_ref, dbso_ref,
             win_v, wout_v, dvn_v, dbs_ref, sems):
        i = pl.program_id(0)
        _load_group([(gath_ref, 0, n_in, win_v), (gath_ref, n_in, n_out, wout_v)], sems)

        @pl.when(i == 0)
        def _():
            dg_ref[...] = jnp.zeros_like(dg_ref)
            dgain_ref[...] = jnp.zeros_like(dgain_ref)
            dbias_ref[...] = jnp.zeros_like(dbias_ref)
            dws_ref[...] = jnp.zeros_like(dws_ref)
            dbs_ref[...] = jnp.zeros_like(dbs_ref)

        xv = x_ref[...]
        gv = g_ref[...]
        hv, xhat, r = _rms_fwd(xv, gv)
        h_ref[...] = hv.astype(BF16)
        gain_v = gain_ref[...]
        vhat = vhat_ref[...].astype(F32)
        vn_bf = (vhat * gain_v + bias_ref[...]).astype(BF16)
        wt = _tril_weights(ws_ref)

        dov = dout_ref[...]
        dy = _nt(dov.astype(BF16), wout_v[...])
        du = dy * sv_ref[...].astype(F32)
        dsv = dy * u_ref[...].astype(F32)
        dsv_bf = dsv.astype(BF16)
        for ci in range(tm // CHUNK):
            rows = slice(ci * CHUNK, (ci + 1) * CHUNK)
            dbs_ref[...] += dsv[rows, :]
            for h in range(HEADS):
                cols = slice(h * GROUP, (h + 1) * GROUP)
                dvn_v[rows, cols] = _tn(wt[h], dsv_bf[rows, cols])
                dws_ref[h] += _nt(dsv_bf[rows, cols], vn_bf[rows, cols])
        dvn = dvn_v[...]
        dgain_ref[...] += _col_sum(dvn * vhat)
        dbias_ref[...] += _col_sum(dvn)
        dvhat = dvn * gain_v
        dv = rstd_ref[...] * (dvhat - _row_mean(dvhat) - vhat * _row_mean(dvhat * vhat))
        dzpre = (jnp.concatenate([du, dv], axis=1) * gd_ref[...].astype(F32)).astype(BF16)
        dz_ref[...] = dzpre
        dh = _nn(dzpre, win_v[...])
        dxr, dg_row = _rms_bwd(dh, xhat, r, gv)
        dg_ref[...] += dg_row
        dx = dov + dxr
        dx_ref[...] = dx
        dxb_ref[...] = dx.astype(BF16)

        @pl.when(i == n_steps - 1)
        def _():
            rr = lax.broadcasted_iota(jnp.int32, (CHUNK, CHUNK), 0)
            cc = lax.broadcasted_iota(jnp.int32, (CHUNK, CHUNK), 1)
            for h in range(HEADS):
                dws_ref[h] = jnp.where(rr >= cc, dws_ref[h], 0.0)
                dbso_ref[h] = jnp.sum(dbs_ref[:, h * GROUP:(h + 1) * GROUP], axis=1, keepdims=True)

    return _hosting_call(
        body, "mixer_a_bwd", n_steps, [dout, x, gd, u_sav, vhat_sav, sv_sav, rstd_sav, g, gain, bias, ws, gath],
        in_specs=[_row_spec(tm, d), _row_spec(tm, d), _row_spec(tm, e2), _row_spec(tm, e), _row_spec(tm, e),
                  _row_spec(tm, e), _row_spec(tm, 1), _const_spec((1, d)),
                  _const_spec((1, e)), _const_spec((1, e)), _const_spec((HEADS, CHUNK, CHUNK)), ANY],
        out_specs=[_row_spec(tm, d), _row_spec(tm, d), _row_spec(tm, d), _row_spec(tm, e2),
                   _const_spec((1, d)), _const_spec((1, e)), _const_spec((1, e)),
                   _const_spec((HEADS, CHUNK, CHUNK)), _const_spec((HEADS, CHUNK, 1))],
        out_shape=[jax.ShapeDtypeStruct((t_tok, d), F32), jax.ShapeDtypeStruct((t_tok, d), BF16),
                   jax.ShapeDtypeStruct((t_tok, d), BF16), jax.ShapeDtypeStruct((t_tok, e2), BF16),
                   jax.ShapeDtypeStruct((1, d), F32), jax.ShapeDtypeStruct((1, e), F32),
                   jax.ShapeDtypeStruct((1, e), F32), jax.ShapeDtypeStruct((HEADS, CHUNK, CHUNK), F32),
                   jax.ShapeDtypeStruct((HEADS, CHUNK, 1), F32)],
        scratch=[pltpu.VMEM((e2, d), BF16), pltpu.VMEM((e, d), BF16), pltpu.VMEM((tm, e), F32),
                 pltpu.VMEM((CHUNK, e), F32), pltpu.SemaphoreType.DMA((2 * N_DEV,))],
        hosted=hosted)


def _ffn_fwd(x, g, srcs, nf, tm, name, hosted=()):
    t_tok, d = x.shape
    f = nf * N_DEV
    firsts = [first for _, first in srcs]

    def body(x_ref, g_ref, sg_ref, su_ref, sd_ref, xo_ref, gate_ref, up_ref, wg_v, wu_v, wd_v, sems):
        _load_group(
            [(sg_ref, firsts[0], nf, wg_v), (su_ref, firsts[1], nf, wu_v), (sd_ref, firsts[2], nf, wd_v)], sems)
        xv = x_ref[...]
        h = _rms_fwd(xv, g_ref[...])[0].astype(BF16)
        gate = _nt(h, wg_v[...])
        up = _nt(h, wu_v[...])
        gate_ref[...] = gate.astype(BF16)
        up_ref[...] = up.astype(BF16)
        act = (gate * _sigmoid(gate) * up).astype(BF16)
        xo_ref[...] = xv + _nn(act, wd_v[...])

    return _hosting_call(
        body, name, t_tok // tm, [x, g] + [arr for arr, _ in srcs],
        in_specs=[_row_spec(tm, d), _const_spec((1, d)), ANY, ANY, ANY],
        out_specs=[_row_spec(tm, d), _row_spec(tm, f), _row_spec(tm, f)],
        out_shape=[jax.ShapeDtypeStruct((t_tok, d), F32), jax.ShapeDtypeStruct((t_tok, f), BF16),
                   jax.ShapeDtypeStruct((t_tok, f), BF16)],
        scratch=[pltpu.VMEM((f, d), BF16), pltpu.VMEM((f, d), BF16), pltpu.VMEM((f, d), BF16),
                 pltpu.SemaphoreType.DMA((3 * N_DEV,))],
        hosted=hosted)


def _ffn_bwd(dout, x, gate, up, g, srcs, nf, tm, name, hosted=()):
    t_tok, d = x.shape
    f = nf * N_DEV
    firsts = [first for _, first in srcs]
    per_chunk = -(-f // (FFN_CHUNKS * MXU_WIDTH)) * MXU_WIDTH
    bounds = [min(ck * per_chunk, f) for ck in range(FFN_CHUNKS + 1)]

    def body(dout_ref, x_ref, gate_ref, up_ref, g_ref, sg_ref, su_ref, sd_ref,
             dx_ref, dxb_ref, h_ref, act_ref, dgu_ref, dg_ref, wg_v, wu_v, wd_v, sems):
        _load_group(
            [(sg_ref, firsts[0], nf, wg_v), (su_ref, firsts[1], nf, wu_v), (sd_ref, firsts[2], nf, wd_v)], sems)

        @pl.when(pl.program_id(0) == 0)
        def _():
            dg_ref[...] = jnp.zeros_like(dg_ref)

        xv = x_ref[...]
        gv = g_ref[...]
        hv, xhat, r = _rms_fwd(xv, gv)
        h_ref[...] = hv.astype(BF16)
        dov = dout_ref[...]
        dob = dov.astype(BF16)
        dh = None
        for ck in range(FFN_CHUNKS):
            cols = slice(bounds[ck], bounds[ck + 1])
            gate_v = gate_ref[:, cols].astype(F32)
            up_v = up_ref[:, cols].astype(F32)
            sig = _sigmoid(gate_v)
            silu = gate_v * sig
            act_ref[:, cols] = (silu * up_v).astype(BF16)
            dact = _nt(dob, wd_v[cols, :])
            dup = (dact * silu).astype(BF16)
            dgate = (dact * up_v * (sig * (1.0 + gate_v * (1.0 - sig)))).astype(BF16)
            dgu_ref[:, cols] = dgate
            dgu_ref[:, f + bounds[ck]:f + bounds[ck + 1]] = dup
            part = _nn(dgate, wg_v[cols, :]) + _nn(dup, wu_v[cols, :])
            dh = part if dh is None else dh + part
        dxr, dg_row = _rms_bwd(dh, xhat, r, gv)
        dg_ref[...] += dg_row
        dx = dov + dxr
        dx_ref[...] = dx
        dxb_ref[...] = dx.astype(BF16)

    return _hosting_call(
        body, name, t_tok // tm, [dout, x, gate, up, g] + [arr for arr, _ in srcs],
        in_specs=[_row_spec(tm, d), _row_spec(tm, d), _row_spec(tm, f), _row_spec(tm, f), _const_spec((1, d)),
                  ANY, ANY, ANY],
        out_specs=[_row_spec(tm, d), _row_spec(tm, d), _row_spec(tm, d), _row_spec(tm, f), _row_spec(tm, 2 * f),
                   _const_spec((1, d))],
        out_shape=[jax.ShapeDtypeStruct((t_tok, d), F32), jax.ShapeDtypeStruct((t_tok, d), BF16),
                   jax.ShapeDtypeStruct((t_tok, d), BF16), jax.ShapeDtypeStruct((t_tok, f), BF16),
                   jax.ShapeDtypeStruct((t_tok, 2 * f), BF16), jax.ShapeDtypeStruct((1, d), F32)],
        scratch=[pltpu.VMEM((f, d), BF16), pltpu.VMEM((f, d), BF16), pltpu.VMEM((f, d), BF16),
                 pltpu.SemaphoreType.DMA((3 * N_DEV,))],
        hosted=hosted)


def _shift_down(z, k, prev_rows):
    row = lax.broadcasted_iota(jnp.int32, z.shape, 0)
    out = pltpu.roll(z, k, 0)
    for j in range(k):
        out = jnp.where(row == j, prev_rows[j], out)
    return out


def _shift_up(z, k, next_rows):
    tm = z.shape[0]
    row = lax.broadcasted_iota(jnp.int32, z.shape, 0)
    out = pltpu.roll(z, tm - k, 0)
    for j in range(k):
        out = jnp.where(row == tm - k + j, next_rows[j], out)
    return out


def _mixer_b_fwd(x, g, gath, conv_w, tm, seq, hosted=()):
    t_tok, d = x.shape
    e = conv_w.shape[1]
    e3 = 3 * e
    n_in, n_out = e3 // N_DEV, e // N_DEV
    tiles_per_seq = seq // tm

    def body(x_ref, g_ref, cw_ref, gath_ref, xo_ref, p_ref, win_v, wout_v, tail_v, sems):
        i = pl.program_id(0)
        _load_group([(gath_ref, 0, n_in, win_v), (gath_ref, n_in, n_out, wout_v)], sems)

        @pl.when(i % tiles_per_seq == 0)
        def _():
            tail_v[...] = jnp.zeros_like(tail_v)

        xv = x_ref[...]
        h = _rms_fwd(xv, g_ref[...])[0].astype(BF16)
        p = _nt(h, win_v[...])
        p_ref[...] = p.astype(BF16)
        z = p[:, e:2 * e] * p[:, 2 * e:]
        prev = [tail_v[SUBLANES - 2:SUBLANES - 1, :], tail_v[SUBLANES - 1:SUBLANES, :]]
        conv = (cw_ref[2:3, :] * z + cw_ref[1:2, :] * _shift_down(z, 1, prev[1:])
                + cw_ref[0:1, :] * _shift_down(z, 2, prev))
        tail_v[...] = z[tm - SUBLANES:, :]
        y = (p[:, :e] * conv).astype(BF16)
        xo_ref[...] = xv + _nn(y, wout_v[...])

    return _hosting_call(
        body, "mixer_b_fwd", t_tok // tm, [x, g, conv_w, gath],
        in_specs=[_row_spec(tm, d), _const_spec((1, d)), _const_spec((SUBLANES, e)), ANY],
        out_specs=[_row_spec(tm, d), _row_spec(tm, e3)],
        out_shape=[jax.ShapeDtypeStruct((t_tok, d), F32), jax.ShapeDtypeStruct((t_tok, e3), BF16)],
        scratch=[pltpu.VMEM((e3, d), BF16), pltpu.VMEM((e, d), BF16), pltpu.VMEM((SUBLANES, e), F32),
                 pltpu.SemaphoreType.DMA((2 * N_DEV,))],
        hosted=hosted)


def _mixer_b_bwd(dout, x, p, g, gath, conv_w, tm, seq, hosted=()):
    t_tok, d = x.shape
    e = conv_w.shape[1]
    e3 = 3 * e
    n_in, n_out = e3 // N_DEV, e // N_DEV
    tiles_per_seq = seq // tm
    halo_per_tile = tm // HALO
    n_halo = t_tok // HALO

    def body(dout_ref, dnext_ref, x_ref, p_ref, pprev_ref, pnext_ref, g_ref, cw_ref, gath_ref,
             dx_ref, dxb_ref, h_ref, y_ref, dp_ref, dg_ref, dcw_ref, win_v, wout_v, sems):
        i = pl.program_id(0)
        _load_group([(gath_ref, 0, n_in, win_v), (gath_ref, n_in, n_out, wout_v)], sems)

        @pl.when(i == 0)
        def _():
            dg_ref[...] = jnp.zeros_like(dg_ref)
            dcw_ref[...] = jnp.zeros_like(dcw_ref)

        first = (i % tiles_per_seq == 0).astype(F32)
        last = (i % tiles_per_seq == tiles_per_seq - 1).astype(F32)
        xv = x_ref[...]
        gv = g_ref[...]
        hv, xhat, r = _rms_fwd(xv, gv)
        h_ref[...] = hv.astype(BF16)
        pv = p_ref[...].astype(F32)
        bg, cg, hx = pv[:, :e], pv[:, e:2 * e], pv[:, 2 * e:]
        z = cg * hx
        pprev = pprev_ref[...].astype(F32)
        zprev = pprev[:, e:2 * e] * pprev[:, 2 * e:] * (1.0 - first)
        prev = [zprev[HALO - 2:HALO - 1, :], zprev[HALO - 1:HALO, :]]
        zs1 = _shift_down(z, 1, prev[1:])
        zs2 = _shift_down(z, 2, prev)
        w0, w1, w2 = cw_ref[0:1, :], cw_ref[1:2, :], cw_ref[2:3, :]
        conv = w2 * z + w1 * zs1 + w0 * zs2
        y_ref[...] = (bg * conv).astype(BF16)

        dov = dout_ref[...]
        wout_bf = wout_v[...]
        dy = _nt(dov.astype(BF16), wout_bf)
        dconv = dy * bg
        dnext = _nt(dnext_ref[...].astype(BF16), wout_bf) * pnext_ref[:, :e].astype(F32) * (1.0 - last)
        nxt = [dnext[0:1, :], dnext[1:2, :]]
        dz = w2 * dconv + w1 * _shift_up(dconv, 1, nxt[:1]) + w0 * _shift_up(dconv, 2, nxt)
        dcw_ref[0:1, :] += _col_sum(dconv * zs2)
        dcw_ref[1:2, :] += _col_sum(dconv * zs1)
        dcw_ref[2:3, :] += _col_sum(dconv * z)
        dp = jnp.concatenate([dy * conv, dz * hx, dz * cg], axis=1).astype(BF16)
        dp_ref[...] = dp
        dh = _nn(dp, win_v[...])
        dxr, dg_row = _rms_bwd(dh, xhat, r, gv)
        dg_ref[...] += dg_row
        dx = dov + dxr
        dx_ref[...] = dx
        dxb_ref[...] = dx.astype(BF16)

    prev_spec = lambda w: pl.BlockSpec((HALO, w), lambda i: (jnp.maximum(i * halo_per_tile - 1, 0), 0))
    next_spec = lambda w: pl.BlockSpec((HALO, w), lambda i: (jnp.minimum((i + 1) * halo_per_tile, n_halo - 1), 0))
    return _hosting_call(
        body, "mixer_b_bwd", t_tok // tm, [dout, dout, x, p, p, p, g, conv_w, gath],
        in_specs=[_row_spec(tm, d), next_spec(d), _row_spec(tm, d), _row_spec(tm, e3), prev_spec(e3), next_spec(e3),
                  _const_spec((1, d)), _const_spec((SUBLANES, e)), ANY],
        out_specs=[_row_spec(tm, d), _row_spec(tm, d), _row_spec(tm, d), _row_spec(tm, e), _row_spec(tm, e3),
                   _const_spec((1, d)), _const_spec((SUBLANES, e))],
        out_shape=[jax.ShapeDtypeStruct((t_tok, d), F32), jax.ShapeDtypeStruct((t_tok, d), BF16),
                   jax.ShapeDtypeStruct((t_tok, d), BF16), jax.ShapeDtypeStruct((t_tok, e), BF16),
                   jax.ShapeDtypeStruct((t_tok, e3), BF16), jax.ShapeDtypeStruct((1, d), F32),
                   jax.ShapeDtypeStruct((SUBLANES, e), F32)],
        scratch=[pltpu.VMEM((e3, d), BF16), pltpu.VMEM((e, d), BF16), pltpu.SemaphoreType.DMA((2 * N_DEV,))],
        hosted=hosted)


def _loss_head(x, target, g, tm):
    t_tok, d = x.shape

    def body(x_ref, t_ref, g_ref, loss_ref, dx_ref, dxb_ref, dg_ref):
        @pl.when(pl.program_id(0) == 0)
        def _():
            loss_ref[...] = jnp.zeros_like(loss_ref)
            dg_ref[...] = jnp.zeros_like(dg_ref)

        gv = g_ref[...]
        y, xhat, r = _rms_fwd(x_ref[...], gv)
        err = y - t_ref[...]
        loss_ref[...] += 0.5 * jnp.sum(_row_mean(err * err), axis=0, keepdims=True)
        dxr, dg_row = _rms_bwd(err * (1.0 / d), xhat, r, gv)
        dg_ref[...] += dg_row
        dx_ref[...] = dxr
        dxb_ref[...] = dxr.astype(BF16)

    return pl.pallas_call(
        body, name="loss_head", grid=(t_tok // tm,),
        in_specs=[_row_spec(tm, d), _row_spec(tm, d), _const_spec((1, d))],
        out_specs=[_const_spec((1, 1)), _row_spec(tm, d), _row_spec(tm, d), _const_spec((1, d))],
        out_shape=[jax.ShapeDtypeStruct((1, 1), F32), jax.ShapeDtypeStruct((t_tok, d), F32),
                   jax.ShapeDtypeStruct((t_tok, d), BF16), jax.ShapeDtypeStruct((1, d), F32)],
        compiler_params=_params(),
    )(x, target, g)


def _wgrad(a, b, bm, name, hosted=()):
    t_tok, m = a.shape
    n = b.shape[1]

    def body(a_ref, b_ref, o_ref):
        o_ref[...] = _tn(a_ref[...], b_ref[...]).astype(o_ref.dtype)

    outs, h_outs = _hosting_call(
        body, name, m // bm, [a, b],
        in_specs=[pl.BlockSpec((t_tok, bm), lambda i: (0, i)), _const_spec((t_tok, n))],
        out_specs=[pl.BlockSpec((bm, n), lambda i: (i, 0))],
        out_shape=[jax.ShapeDtypeStruct((m, n), BF16)],
        scratch=[], hosted=hosted)
    return (outs[0], h_outs) if hosted else outs[0]


def _sum_slots(land, rb, name):
    n_slots, rows, cols = land.shape

    def body(l_ref, o_ref):
        acc = l_ref[0].astype(F32)
        for k in range(1, n_slots):
            acc = acc + l_ref[k].astype(F32)
        o_ref[...] = acc

    return pl.pallas_call(
        body, name=name, grid=(rows // rb,),
        in_specs=[pl.BlockSpec((n_slots, rb, cols), lambda i: (0, i, 0))],
        out_specs=pl.BlockSpec((rb, cols), lambda i: (i, 0)),
        out_shape=jax.ShapeDtypeStruct((rows, cols), F32),
        compiler_params=_params(sequential=False),
    )(land)


def _adamw(w, grad, m, v, rb, name):
    rows, cols = w.shape
    c1 = 1.0 / (1.0 - ADAM_B1 ** ADAM_STEP)
    c2 = 1.0 / (1.0 - ADAM_B2 ** ADAM_STEP)

    def body(w_ref, g_ref, m_ref, v_ref, d_ref, mo_ref, vo_ref):
        gv = g_ref[...]
        mn = ADAM_B1 * m_ref[...] + (1.0 - ADAM_B1) * gv
        vn = ADAM_B2 * v_ref[...] + (1.0 - ADAM_B2) * (gv * gv)
        mo_ref[...] = mn
        vo_ref[...] = vn
        d_ref[...] = -ADAM_LR * ((mn * c1) / (jnp.sqrt(vn * c2) + ADAM_EPS) + ADAM_WD * w_ref[...])

    spec = pl.BlockSpec((rb, cols), lambda i: (i, 0))
    shape = jax.ShapeDtypeStruct((rows, cols), F32)
    return pl.pallas_call(
        body, name=name, grid=(rows // rb,),
        in_specs=[spec] * 4, out_specs=[spec] * 3, out_shape=[shape] * 3,
        compiler_params=_params(sequential=False),
    )(w, grad, m, v)


def _split_bf16(a):
    hi = a.astype(BF16)
    rest = a - hi.astype(F32)
    mid = rest.astype(BF16)
    return hi, mid, (rest - mid.astype(F32)).astype(BF16)


def _reduce_adamw(lands, w, m, v, transpose, name):
    n_layers, rows_w, cols_w = w.shape
    c1 = 1.0 / (1.0 - ADAM_B1 ** ADAM_STEP)
    c2 = 1.0 / (1.0 - ADAM_B2 ** ADAM_STEP)
    if transpose:
        n, tiles = cols_w, rows_w // MXU_WIDTH
        blk = (MXU_WIDTH, n)
        land_specs = [pl.BlockSpec((N_CHIP, n, MXU_WIDTH), lambda i, b=first // n: (0, b, i % tiles))
                      for _, first in lands]
    else:
        n, tiles = rows_w, 2
        blk = (n // tiles, cols_w)
        land_specs = [pl.BlockSpec((N_CHIP,) + blk, lambda i, b=first // blk[0]: (0, b + i % tiles, 0))
                      for _, first in lands]
    for _, first in lands:
        assert first % (n if transpose else blk[0]) == 0

    def body(*refs):
        land_refs = refs[:n_layers]
        w_ref, m_ref, v_ref, g_ref, d_ref, mo_ref, vo_ref = refs[n_layers:]
        layer = pl.program_id(0) // tiles

        def total(ref):
            acc = ref[0].astype(F32)
            for q in range(1, N_CHIP):
                acc = acc + ref[q].astype(F32)
            return acc

        gv = total(land_refs[0])
        for k in range(1, n_layers):
            gv = jnp.where(layer == k, total(land_refs[k]), gv)
        if transpose:
            r = lax.broadcasted_iota(jnp.int32, (MXU_WIDTH, MXU_WIDTH), 0)
            c = lax.broadcasted_iota(jnp.int32, (MXU_WIDTH, MXU_WIDTH), 1)
            eye = (r == c).astype(BF16)
            hi, mid, lo = _split_bf16(gv)
            gv = _nt(eye, hi) + _nt(eye, mid) + _nt(eye, lo)
        g_ref[...] = gv
        mn = ADAM_B1 * m_ref[...] + (1.0 - ADAM_B1) * gv
        vn = ADAM_B2 * v_ref[...] + (1.0 - ADAM_B2) * (gv * gv)
        mo_ref[...] = mn
        vo_ref[...] = vn
        d_ref[...] = -ADAM_LR * ((mn * c1) / (jnp.sqrt(vn * c2) + ADAM_EPS) + ADAM_WD * w_ref[...])

    spec = pl.BlockSpec((None,) + blk, lambda i: (i // tiles, i % tiles, 0))
    shape = jax.ShapeDtypeStruct(w.shape, F32)
    return pl.pallas_call(
        body, name=name, grid=(n_layers * tiles,),
        in_specs=land_specs + [spec] * 3, out_specs=[spec] * 4, out_shape=[shape] * 4,
        compiler_params=_params(sequential=False),
    )(*[land for land, _ in lands], w, m, v)


def _pack_small(parts, rows):
    flat = jnp.concatenate([p.reshape(-1).astype(F32) for p in parts])
    return jnp.pad(flat, (0, rows * LANES - flat.shape[0])).reshape(rows, LANES)


def _unpack_small(packed, shapes):
    flat = packed.reshape(-1)
    out = []
    pos = 0
    for s in shapes:
        n = math.prod(s)
        out.append(flat[pos:pos + n].reshape(s))
        pos += n
    return out


def kernel(x, mix_norm, ffn_norm, a_w_in, a_v_gain, a_v_bias, a_w_s, a_b_s, a_w_out, b_w_in, b_conv_w, b_w_out, ffn_w_gate, ffn_w_up, ffn_w_down, final_norm, loss_target, m_mix_norm, m_ffn_norm, m_a_w_in, m_a_v_gain, m_a_v_bias, m_a_w_s, m_a_b_s, m_a_w_out, m_b_w_in, m_b_conv_w, m_b_w_out, m_ffn_w_gate, m_ffn_w_up, m_ffn_w_down, m_final_norm, v_mix_norm, v_ffn_norm, v_a_w_in, v_a_v_gain, v_a_v_bias, v_a_w_s, v_a_b_s, v_a_w_out, v_b_w_in, v_b_conv_w, v_b_w_out, v_ffn_w_gate, v_ffn_w_up, v_ffn_w_down, v_final_norm):
    bsz, seq, d = x.shape
    t_tok = bsz * seq
    me = _my_index()
    xt = x.reshape(t_tok, d)
    target = loss_target.reshape(t_tok, d)
    e_a = a_v_gain.shape[1]
    e_b = b_w_out.shape[1] * N_DEV
    n_layers = ffn_w_gate.shape[0]
    f_shard = ffn_w_gate.shape[2]
    f_full = f_shard * N_DEV

    conv_pad = jnp.pad(b_conv_w[0], ((0, SUBLANES - CONV_W), (0, 0)))
    sh_a = jnp.concatenate([a_w_in[0].T, a_w_out[0]]).astype(BF16)
    sh_b = jnp.concatenate([b_w_in[0].T, b_w_out[0]]).astype(BF16)
    sh_f0 = jnp.concatenate([ffn_w_gate[0].T, ffn_w_up[0].T, ffn_w_down[0]]).astype(BF16)
    sh_f1g = ffn_w_gate[1].T.astype(BF16)
    sh_f1ud = jnp.concatenate([ffn_w_up[1].T, ffn_w_down[1]]).astype(BF16)
    bfull = jnp.repeat(a_b_s[0].T, GROUP, axis=1)

    gath_a, conv_g = _exchange([_HostedGathers([sh_a, conv_pad], 0)], "gather_a")
    conv_full = jnp.pad(conv_g[:, :CONV_W, :].transpose(1, 0, 2).reshape(CONV_W, e_b), ((0, SUBLANES - CONV_W), (0, 0)))
    (x1, gd_a, u_a, vhat_a, sv_a, y_a, rstd_a), (gath_f0,) = _mixer_a_fwd(
        xt, mix_norm[0:1], gath_a, a_v_gain, a_v_bias, a_w_s[0], bfull, tm=256,
        hosted=[_HostedGathers([sh_f0], mid_lead=2)])
    srcs0 = [(gath_f0, 0), (gath_f0, f_shard), (gath_f0, 2 * f_shard)]
    (x2, gate0, up0), (gath_b, gath_f1g) = _ffn_fwd(x1, ffn_norm[0:1], srcs0, f_shard, tm=256, name="ffn_fwd0",
                                                    hosted=[_HostedGathers([sh_b, sh_f1g], mid_lead=2)])
    (x3, p_b), (gath_f1ud,) = _mixer_b_fwd(x2, mix_norm[1:2], gath_b, conv_full, tm=256, seq=seq,
                                           hosted=[_HostedGathers([sh_f1ud], mid_lead=2)])
    srcs1 = [(gath_f1g, 0), (gath_f1ud, 0), (gath_f1ud, f_shard)]
    (x4, gate1, up1), _ = _ffn_fwd(x3, ffn_norm[1:2], srcs1, f_shard, tm=256, name="ffn_fwd1")
    loss_part, dx4, dx4_bf, d_final = _loss_head(x4, target, final_norm.reshape(1, d), tm=512)

    ffn_entries = [(0, 0, f_shard), (0, f_full, f_shard), (1, 0, f_shard)]
    (dx3, dx3_bf, h_f1, act1, dgu1, d_fn1), _ = _ffn_bwd(dx4, x3, gate1, up1, ffn_norm[1:2], srcs1, f_shard, tm=256,
                                                         name="ffn_bwd1")
    g_down1 = _wgrad(act1, dx4_bf, 256, "wgrad_down1")
    g_gu1 = _wgrad(dgu1, h_f1, 512, "wgrad_gate_up1")
    ps_f1 = _pair_reduce([g_gu1, g_down1], ffn_entries, "pair_reduce_f1")
    (dx2, dx2_bf, h_b, y_b, dp_b, d_mn1, d_conv), (land_f1gu,) = _mixer_b_bwd(
        dx3, x2, p_b, mix_norm[1:2], gath_b, conv_full, tm=256, seq=seq,
        hosted=[_HostedChipScatter(ps_f1, 0, 2 * f_shard)])
    g_b_out = _wgrad(y_b, dx3_bf, 256, "wgrad_b_out")
    g_b_in = _wgrad(dp_b, h_b, 512, "wgrad_b_in")
    ps_b = _pair_reduce([g_b_in, g_b_out], [(0, 0, b_w_in.shape[2]), (1, 0, b_w_out.shape[1])], "pair_reduce_b")
    (dx1, dx1_bf, h_f0, act0, dgu0, d_fn0), (land_f1d, land_b) = _ffn_bwd(
        dx2, x1, gate0, up0, ffn_norm[0:1], srcs0, f_shard, tm=256, name="ffn_bwd0",
        hosted=[_HostedChipScatter(ps_f1, 2 * f_shard, f_shard), _HostedChipScatter(ps_b)])
    g_down0 = _wgrad(act0, dx2_bf, 256, "wgrad_down0")
    g_gu0 = _wgrad(dgu0, h_f0, 512, "wgrad_gate_up0")
    ps_f0 = _pair_reduce([g_gu0, g_down0], ffn_entries, "pair_reduce_f0")
    g_a_out = _wgrad(y_a, dx1_bf, 256, "wgrad_a_out")
    ps_ao = _pair_reduce([g_a_out], [(0, 0, a_w_out.shape[1])], "pair_reduce_a_out")
    (dx0, _, h_a, dz_a, d_mn0, d_gain, d_bias, d_ws, d_bs_acc), (land_f0, land_ao) = _mixer_a_bwd(
        dx1, xt, gd_a, u_a, vhat_a, sv_a, rstd_a, mix_norm[0:1], gath_a, a_v_gain, a_v_bias, a_w_s[0], tm=256,
        hosted=[_HostedChipScatter(ps_f0), _HostedChipScatter(ps_ao)])
    d_bs = d_bs_acc.reshape(HEADS, CHUNK)

    small_grads = [jnp.concatenate([d_mn0, d_mn1]), jnp.concatenate([d_fn0, d_fn1]), d_gain, d_bias, d_ws, d_bs,
                   d_final, d_conv[:CONV_W], loss_part]
    small_shapes = [(n_layers, d), (n_layers, d), (1, e_a), (1, e_a), (1, HEADS, CHUNK, CHUNK), (1, HEADS, CHUNK), (d,),
                    (CONV_W, e_b), ()]
    n_small = sum(math.prod(s) for s in small_shapes)
    blk_rows = -(-n_small // (N_DEV * LANES * SUBLANES)) * SUBLANES
    small_rows = blk_rows * N_DEV
    packed = _pack_small(small_grads, small_rows)
    g_a_in, (small_land,) = _wgrad(dz_a, h_a, 512, "wgrad_a_in", hosted=[_HostedScatterAll(packed)])
    ps_ai = _pair_reduce([g_a_in], [(0, 0, a_w_in.shape[2])], "pair_reduce_a_in")
    small_sum = _sum_slots(small_land, blk_rows, "sum_small")
    land_ai, small_gath = _exchange([_HostedChipScatter(ps_ai), _HostedGathers([small_sum], 0)], "tail_exchange")
    small_all = small_gath.reshape(small_rows, LANES)

    (gr_mix, gr_ffn, gr_gain, gr_bias, gr_ws, gr_bs, gr_final, gr_conv_full, loss) = _unpack_small(small_all, small_shapes)
    gr_conv = lax.dynamic_slice_in_dim(gr_conv_full, me * (e_b // N_DEV), e_b // N_DEV, axis=1)[None]

    small_w = [mix_norm, ffn_norm, a_v_gain, a_v_bias, a_w_s, a_b_s, final_norm]
    small_m = [m_mix_norm, m_ffn_norm, m_a_v_gain, m_a_v_bias, m_a_w_s, m_a_b_s, m_final_norm]
    small_v = [v_mix_norm, v_ffn_norm, v_a_v_gain, v_a_v_bias, v_a_w_s, v_a_b_s, v_final_norm]
    small_g = [gr_mix, gr_ffn, gr_gain, gr_bias, gr_ws, gr_bs, gr_final]
    sm_shapes = small_shapes[:len(small_w)]
    sm_out = _adamw(_pack_small(small_w, small_rows), _pack_small(small_g, small_rows), _pack_small(small_m, small_rows),
                    _pack_small(small_v, small_rows), small_rows, "adamw_small")
    sm_delta, sm_m, sm_v = [_unpack_small(o, sm_shapes) for o in sm_out]

    conv_out = _adamw(b_conv_w[0], gr_conv[0], m_b_conv_w[0], v_b_conv_w[0], CONV_W, "adamw_conv")
    conv_delta, conv_m, conv_v = [o[None] for o in conv_out]

    n_b_in = b_w_in.shape[2]
    res = {
        "a_w_in": _reduce_adamw([(land_ai, 0)], a_w_in, m_a_w_in, v_a_w_in, True, "adamw_a_in"),
        "a_w_out": _reduce_adamw([(land_ao, 0)], a_w_out, m_a_w_out, v_a_w_out, False, "adamw_a_out"),
        "b_w_in": _reduce_adamw([(land_b, 0)], b_w_in, m_b_w_in, v_b_w_in, True, "adamw_b_in"),
        "b_w_out": _reduce_adamw([(land_b, n_b_in)], b_w_out, m_b_w_out, v_b_w_out, False, "adamw_b_out"),
        "ffn_w_gate": _reduce_adamw([(land_f0, 0), (land_f1gu, 0)], ffn_w_gate, m_ffn_w_gate, v_ffn_w_gate, True,
                                    "adamw_gate"),
        "ffn_w_up": _reduce_adamw([(land_f0, f_shard), (land_f1gu, f_shard)], ffn_w_up, m_ffn_w_up, v_ffn_w_up, True,
                                  "adamw_up"),
        "ffn_w_down": _reduce_adamw([(land_f0, 2 * f_shard), (land_f1d, 0)], ffn_w_down, m_ffn_w_down, v_ffn_w_down,
                                    False, "adamw_down"),
    }

    order = ["mix_norm", "ffn_norm", "a_w_in", "a_v_gain", "a_v_bias", "a_w_s", "a_b_s", "a_w_out", "b_w_in",
             "b_conv_w", "b_w_out", "ffn_w_gate", "ffn_w_up", "ffn_w_down", "final_norm"]
    small_names = ["mix_norm", "ffn_norm", "a_v_gain", "a_v_bias", "a_w_s", "a_b_s", "final_norm"]
    grads = {"b_conv_w": gr_conv}
    deltas, new_m, new_v = {}, {}, {}
    for k, name in enumerate(small_names):
        grads[name] = small_g[k]
        deltas[name], new_m[name], new_v[name] = sm_delta[k], sm_m[k], sm_v[k]
    deltas["b_conv_w"], new_m["b_conv_w"], new_v["b_conv_w"] = conv_delta, conv_m, conv_v
    for name, (gg, dl, mm, vv) in res.items():
        grads[name], deltas[name], new_m[name], new_v[name] = gg, dl, mm, vv

    grad_x = dx0.reshape(bsz, seq, d)
    return (loss, grad_x, *[grads[n] for n in order], *[deltas[n] for n in order],
            *[new_m[n] for n in order], *[new_v[n] for n in order])
```

```python
import math

import jax
import jax.numpy as jnp
from jax import lax
from jax.experimental import pallas as pl
from jax.experimental.pallas import tpu as pltpu

F32 = jnp.float32
BF16 = jnp.bfloat16

N_DEV = 8
N_CHIP = 4
CHUNK = 128
HEADS = 16
GROUP = 128
CONV_W = 3
NORM_EPS = 1e-6
GELU_C = math.sqrt(2.0 / math.pi)
GELU_K = 0.044715

ADAM_LR = 0.001
ADAM_B1 = 0.9
ADAM_B2 = 0.999
ADAM_EPS = 1e-08
ADAM_WD = 0.01
ADAM_STEP = 10

LANES = 128
SUBLANES = 8
VMEM_LIMIT = 60 * 1024 * 1024
HALO = 16
MXU_WIDTH = 256
FFN_CHUNKS = 2

MESH = pl.DeviceIdType.MESH
ANY = pl.BlockSpec(memory_space=pl.ANY)


def _params(sequential=True):
    return pltpu.CompilerParams(
        dimension_semantics=("arbitrary",) if sequential else None,
        vmem_limit_bytes=VMEM_LIMIT)


def _nn(a, b):
    return jnp.dot(a, b, preferred_element_type=F32)


def _nt(a, b):
    return lax.dot_general(a, b, (((1,), (1,)), ((), ())), preferred_element_type=F32)


def _tn(a, b):
    return lax.dot_general(a, b, (((0,), (0,)), ((), ())), preferred_element_type=F32)


def _row_mean(a):
    return jnp.mean(a, axis=-1, keepdims=True)


def _col_sum(a):
    return jnp.sum(a, axis=0, keepdims=True)


def _rms_fwd(x, g):
    r = lax.rsqrt(_row_mean(x * x) + NORM_EPS)
    xhat = x * r
    return xhat * g, xhat, r


def _rms_bwd(dh, xhat, r, g):
    a = dh * g
    dx = r * (a - xhat * _row_mean(a * xhat))
    return dx, _col_sum(dh * xhat)


def _gelu_and_grad(x):
    x2 = x * x
    t = jnp.tanh(x * (GELU_C + (GELU_C * GELU_K) * x2))
    half = 0.5 * t + 0.5
    d = half + x * (0.5 - 0.5 * (t * t)) * (GELU_C + (3.0 * GELU_C * GELU_K) * x2)
    return x * half, d


def _sigmoid(x):
    return 1.0 / (1.0 + jnp.exp(-x))


def _row_spec(tm, width):
    return pl.BlockSpec((tm, width), lambda i: (i, 0))


def _const_spec(shape):
    nd = len(shape)
    return pl.BlockSpec(shape, lambda i: (0,) * nd)


def _load_group(parts, sems):
    @pl.when(pl.program_id(0) == 0)
    def _():
        copies = []
        for k, (gath_ref, first, n, dst) in enumerate(parts):
            for j in range(N_DEV):
                copies.append(pltpu.make_async_copy(gath_ref.at[j, pl.ds(first, n), :], dst.at[pl.ds(j * n, n), :],
                                                    sems.at[k * N_DEV + j]))
        for cp in copies:
            cp.start()
        for cp in copies:
            cp.wait()


def _hosting_call(body, name, n_steps, arrays, in_specs, out_specs, out_shape, scratch, hosted=()):
    n_in, n_out, n_scr = len(arrays), len(out_shape), len(scratch)
    h_arrays = [a for h in hosted for a in h.arrays]
    h_shapes = [s for h in hosted for s in h.out_shapes]
    h_sems = [s for h in hosted for s in h.sem_shapes]

    def full_body(*refs):
        pos = 0
        groups = []
        for n in (n_in, len(h_arrays), n_out, len(h_shapes), n_scr, len(h_sems)):
            groups.append(refs[pos:pos + n])
            pos += n
        own_in, h_in, own_out, h_out, own_scr, h_sem = groups
        per_host = []
        pi = po = ps = 0
        for h in hosted:
            ni, no, ns = len(h.arrays), len(h.out_shapes), len(h.sem_shapes)
            per_host.append((h, h_in[pi:pi + ni], h_out[po:po + no], h_sem[ps:ps + ns]))
            pi, po, ps = pi + ni, po + no, ps + ns
        for h, ins, outs, sems in per_host:
            h.begin(ins, outs, sems, n_steps)
        body(*own_in, *own_out, *own_scr)
        for h, ins, outs, sems in per_host:
            h.end(ins, outs, sems, n_steps)

    outs = pl.pallas_call(
        full_body, name=name, grid=(n_steps,),
        in_specs=list(in_specs) + [ANY] * len(h_arrays),
        out_specs=list(out_specs) + [ANY] * len(h_shapes),
        out_shape=list(out_shape) + h_shapes,
        scratch_shapes=list(scratch) + h_sems,
        compiler_params=_params(),
    )(*arrays, *h_arrays)
    return outs[:n_out], outs[n_out:]


def _my_index():
    return 4 * lax.axis_index("x") + 2 * lax.axis_index("y") + lax.axis_index("c")


GATHER_COPIES = 8


class _Gather:
    def __init__(self, shard, out, send_sems, recv_sems, local_sem):
        self.shard, self.out = shard, out
        self.send_sems, self.recv_sems, self.local_sem = send_sems, recv_sems, local_sem
        x, y, c = lax.axis_index("x"), lax.axis_index("y"), lax.axis_index("c")
        self.c = c
        self.me, self.sibling = (x, y, c), (x, y, 1 - c)
        self.xn, self.yn, self.dg = (1 - x, y), (x, 1 - y), (1 - x, 1 - y)
        self.n = shard.shape[0]
        self.half = self.n // 2
        rows_per_tile = SUBLANES * 4 // shard.dtype.itemsize
        self.relays = self.n % 2 == 0 and self.half % rows_per_tile == 0

    def _slot(self, dev, lo=0, hi=None):
        hi = self.n if hi is None else hi
        return self.out.at[4 * dev[0] + 2 * dev[1] + dev[2], pl.ds(lo, hi - lo), :]

    def _copy(self, k, block, to, src=None, lo=0, hi=None):
        return pltpu.make_async_remote_copy(
            src_ref=self._slot(block, lo, hi) if src is None else src, dst_ref=self._slot(block, lo, hi),
            send_sem=self.send_sems.at[k], recv_sem=self.recv_sems.at[k], device_id=to, device_id_type=MESH)

    def _local(self):
        return pltpu.make_async_copy(self.shard, self._slot(self.me), self.local_sem)

    def start(self):
        c = self.c
        self._local().start()
        self._copy(0, self.me, self.sibling, src=self.shard).start()
        self._copy(1, self.me, (*self.xn, c), src=self.shard).start()
        self._copy(2, self.me, (*self.yn, c), src=self.shard).start()
        if not self.relays:
            self._copy(3, self.me, (*self.dg, c), src=self.shard).start()

    def relay(self):
        c = self.c
        if self.relays:
            self._copy(1, (*self.xn, c), self.me).wait_recv()
            self._copy(3, (*self.xn, c), (*self.yn, c), hi=self.half).start()
            self._copy(2, (*self.yn, c), self.me).wait_recv()
            self._copy(4, (*self.yn, c), (*self.xn, c), lo=self.half).start()

    def forward(self):
        c = self.c
        if self.relays:
            self._copy(5, (*self.xn, c), self.sibling).start()
            self._copy(6, (*self.yn, c), self.sibling).start()
            self._copy(3, (*self.dg, c), self.me, hi=self.half).wait_recv()
            self._copy(4, (*self.dg, c), self.me, lo=self.half).wait_recv()
        else:
            self._copy(1, (*self.xn, c), self.me).wait_recv()
            self._copy(5, (*self.xn, c), self.sibling).start()
            self._copy(2, (*self.yn, c), self.me).wait_recv()
            self._copy(6, (*self.yn, c), self.sibling).start()
            self._copy(3, (*self.dg, c), self.me).wait_recv()
        self._copy(7, (*self.dg, c), self.sibling).start()

    def finish(self):
        c = self.c
        self._copy(0, self.sibling, self.me).wait_recv()
        for k, chip in ((5, self.xn), (6, self.yn), (7, self.dg)):
            self._copy(k, (*chip, 1 - c), self.me).wait_recv()
        for k in (0, 1, 2, 5, 6, 7):
            self._copy(k, self.me, self.sibling).wait_send()
        if self.relays:
            self._copy(3, self.me, self.sibling, hi=self.half).wait_send()
            self._copy(4, self.me, self.sibling, lo=self.half).wait_send()
        else:
            self._copy(3, self.me, self.sibling).wait_send()
        self._local().wait()


class _HostedGathers:
    def __init__(self, shards, mid_lead, relay_at=0.56):
        n = len(shards)
        self.arrays = shards
        self.mid_lead, self.relay_at = mid_lead, relay_at
        self.out_shapes = [jax.ShapeDtypeStruct((N_DEV,) + s.shape, s.dtype) for s in shards]
        self.sem_shapes = [pltpu.SemaphoreType.DMA((n, GATHER_COPIES)), pltpu.SemaphoreType.DMA((n, GATHER_COPIES)),
                           pltpu.SemaphoreType.DMA((n,))]

    def _gathers(self, ins, outs, sems):
        return [_Gather(ins[a], outs[a], sems[0].at[a], sems[1].at[a], sems[2].at[a]) for a in range(len(ins))]

    def begin(self, ins, outs, sems, n_steps):
        i = pl.program_id(0)
        forward_step = max(n_steps - 1 - self.mid_lead, 0)
        relay_step = min(int(self.relay_at * n_steps), forward_step)

        @pl.when(i == 0)
        def _():
            for g in self._gathers(ins, outs, sems):
                g.start()

        if n_steps == 1:
            return

        @pl.when(i == relay_step)
        def _():
            for g in self._gathers(ins, outs, sems):
                g.relay()

        @pl.when(i == forward_step)
        def _():
            for g in self._gathers(ins, outs, sems):
                g.forward()

    def end(self, ins, outs, sems, n_steps):
        @pl.when(pl.program_id(0) == n_steps - 1)
        def _():
            gathers = self._gathers(ins, outs, sems)
            if n_steps == 1:
                for g in gathers:
                    g.relay()
                for g in gathers:
                    g.forward()
            for g in gathers:
                g.finish()


def _exchange(hosted, name):
    return _hosting_call(lambda: None, name, 1, [], [], [], [], [], hosted=hosted)[1]


class _ChipScatter:
    def __init__(self, pairsum, row0, land, send_sems, recv_sems, local_sem):
        self.pairsum, self.row0, self.land = pairsum, row0, land
        self.send_sems, self.recv_sems, self.local_sem = send_sems, recv_sems, local_sem
        x, y, c = lax.axis_index("x"), lax.axis_index("y"), lax.axis_index("c")
        self.c = c
        self.chip = 2 * x + y
        self.others = [(1 - x, y), (x, 1 - y), (1 - x, 1 - y)]

    def _src(self, chip):
        return self.pairsum.at[chip, pl.ds(self.row0, self.land.shape[1]), :]

    def _copy(self, k):
        ox, oy = self.others[k]
        return pltpu.make_async_remote_copy(
            src_ref=self._src(2 * ox + oy), dst_ref=self.land.at[self.chip],
            send_sem=self.send_sems.at[k], recv_sem=self.recv_sems.at[k], device_id=(ox, oy, self.c),
            device_id_type=MESH)

    def _arrival(self, k):
        ox, oy = self.others[k]
        return pltpu.make_async_remote_copy(
            src_ref=self._src(self.chip), dst_ref=self.land.at[2 * ox + oy],
            send_sem=self.send_sems.at[k], recv_sem=self.recv_sems.at[k], device_id=(ox, oy, self.c),
            device_id_type=MESH)

    def _local(self):
        return pltpu.make_async_copy(self._src(self.chip), self.land.at[self.chip], self.local_sem)

    def start(self):
        self._local().start()
        for k in range(N_CHIP - 1):
            self._copy(k).start()

    def finish(self):
        for k in range(N_CHIP - 1):
            self._arrival(k).wait_recv()
        for k in range(N_CHIP - 1):
            self._copy(k).wait_send()
        self._local().wait()


class _HostedChipScatter:
    def __init__(self, pairsum, row0=0, n=None):
        n = pairsum.shape[1] - row0 if n is None else n
        self.row0 = row0
        self.arrays = [pairsum]
        self.out_shapes = [jax.ShapeDtypeStruct((N_CHIP, n, pairsum.shape[2]), pairsum.dtype)]
        self.sem_shapes = [pltpu.SemaphoreType.DMA((N_CHIP - 1,)), pltpu.SemaphoreType.DMA((N_CHIP - 1,)),
                           pltpu.SemaphoreType.DMA(())]

    def begin(self, ins, outs, sems, n_steps):
        @pl.when(pl.program_id(0) == 0)
        def _():
            _ChipScatter(ins[0], self.row0, outs[0], *sems).start()

    def end(self, ins, outs, sems, n_steps):
        @pl.when(pl.program_id(0) == n_steps - 1)
        def _():
            _ChipScatter(ins[0], self.row0, outs[0], *sems).finish()


def _pair_reduce(arrays, entries, name):
    n_arr, n_ent = len(arrays), len(entries)
    cols = arrays[0].shape[1]
    offsets = []
    total = 0
    for _, _, n in entries:
        offsets.append(total)
        total += n

    def body(*refs):
        ins, out_ref = refs[:n_arr], refs[n_arr]
        rbuf, own, send_sems, recv_sems, own_sems = refs[n_arr + 1:]
        q = pl.program_id(0)
        x, y, c = lax.axis_index("x"), lax.axis_index("y"), lax.axis_index("c")

        def block(e, chip, core):
            ai, first, n = entries[e]
            return ins[ai].at[pl.ds(first + (2 * chip + core) * n, n), :]

        def to_sibling(e, chip):
            return pltpu.make_async_remote_copy(
                src_ref=block(e, chip, 1 - c), dst_ref=rbuf.at[chip, pl.ds(offsets[e], entries[e][2]), :],
                send_sem=send_sems.at[e, chip], recv_sem=recv_sems.at[e, chip], device_id=(x, y, 1 - c),
                device_id_type=MESH)

        @pl.when(q == 0)
        def _():
            for chip in range(N_CHIP):
                for e in range(n_ent):
                    to_sibling(e, chip).start()

        loads = [pltpu.make_async_copy(block(e, q, c), own.at[pl.ds(offsets[e], entries[e][2]), :], own_sems.at[e])
                 for e in range(n_ent)]
        for cp in loads:
            cp.start()
        for cp in loads:
            cp.wait()
        for e in range(n_ent):
            to_sibling(e, q).wait_recv()
        out_ref[...] = (own[...].astype(F32) + rbuf[q].astype(F32)).astype(out_ref.dtype)

        @pl.when(q == N_CHIP - 1)
        def _():
            for chip in range(N_CHIP):
                for e in range(n_ent):
                    to_sibling(e, chip).wait_send()

    return pl.pallas_call(
        body, name=name, grid=(N_CHIP,),
        in_specs=[ANY] * n_arr,
        out_specs=pl.BlockSpec((None, total, cols), lambda q: (q, 0, 0)),
        out_shape=jax.ShapeDtypeStruct((N_CHIP, total, cols), BF16),
        scratch_shapes=[pltpu.VMEM((N_CHIP, total, cols), BF16), pltpu.VMEM((total, cols), BF16),
                        pltpu.SemaphoreType.DMA((n_ent, N_CHIP)), pltpu.SemaphoreType.DMA((n_ent, N_CHIP)),
                        pltpu.SemaphoreType.DMA((n_ent,))],
        compiler_params=_params(),
    )(*arrays)


class _HostedScatterAll:
    def __init__(self, packed):
        n = packed.shape[0] // N_DEV
        self.n = n
        self.arrays = [packed]
        self.out_shapes = [jax.ShapeDtypeStruct((N_DEV, n, packed.shape[1]), packed.dtype)]
        self.sem_shapes = [pltpu.SemaphoreType.DMA((N_DEV - 1,)), pltpu.SemaphoreType.DMA((N_DEV - 1,)),
                           pltpu.SemaphoreType.DMA(())]

    def _copies(self, ins, outs, sems, with_arrivals):
        src, land = ins[0], outs[0]
        send_sems, recv_sems, local_sem = sems
        me = _my_index()

        def block(p):
            return src.at[pl.ds(p * self.n, self.n), :]

        local = pltpu.make_async_copy(block(me), land.at[me], local_sem)
        sends, arrivals = [], []
        for k in range(1, N_DEV):
            p = (me + k) % N_DEV
            q = (me + N_DEV - k) % N_DEV
            sends.append(pltpu.make_async_remote_copy(
                src_ref=block(p), dst_ref=land.at[me], send_sem=send_sems.at[k - 1], recv_sem=recv_sems.at[k - 1],
                device_id=(p // 4, (p // 2) % 2, p % 2), device_id_type=MESH))
            if with_arrivals:
                arrivals.append(pltpu.make_async_remote_copy(
                    src_ref=block(me), dst_ref=land.at[q], send_sem=send_sems.at[k - 1], recv_sem=recv_sems.at[k - 1],
                    device_id=(q // 4, (q // 2) % 2, q % 2), device_id_type=MESH))
        return local, sends, arrivals

    def begin(self, ins, outs, sems, n_steps):
        @pl.when(pl.program_id(0) == 0)
        def _():
            local, sends, _ = self._copies(ins, outs, sems, with_arrivals=False)
            local.start()
            for cp in sends:
                cp.start()

    def end(self, ins, outs, sems, n_steps):
        @pl.when(pl.program_id(0) == n_steps - 1)
        def _():
            local, sends, arrivals = self._copies(ins, outs, sems, with_arrivals=True)
            for cp in arrivals:
                cp.wait_recv()
            for cp in sends:
                cp.wait_send()
            local.wait()


def _tril_weights(ws_ref):
    r = lax.broadcasted_iota(jnp.int32, (CHUNK, CHUNK), 0)
    c = lax.broadcasted_iota(jnp.int32, (CHUNK, CHUNK), 1)
    return [jnp.where(r >= c, ws_ref[h], 0.0).astype(BF16) for h in range(HEADS)]


def _sgu_stats(zpre, gain, bias):
    e = zpre.shape[1] // 2
    z, dz = _gelu_and_grad(zpre)
    u, v = z[:, :e], z[:, e:]
    vc = v - _row_mean(v)
    rstd = lax.rsqrt(_row_mean(vc * vc) + NORM_EPS)
    vhat = vc * rstd
    return u, vhat, rstd, vhat * gain + bias, dz


def _spatial_fwd(wt, vn_bf, bfull_ref, sv_ref, tm):
    for ci in range(tm // CHUNK):
        rows = slice(ci * CHUNK, (ci + 1) * CHUNK)
        for h in range(HEADS):
            cols = slice(h * GROUP, (h + 1) * GROUP)
            sv_ref[rows, cols] = _nn(wt[h], vn_bf[rows, cols]) + bfull_ref[:, cols]


def _mixer_a_fwd(x, g, gath, gain, bias, ws, bfull, tm, hosted=()):
    t_tok, d = x.shape
    e = gain.shape[1]
    e2 = 2 * e
    n_in, n_out = e2 // N_DEV, e // N_DEV

    def body(x_ref, g_ref, gain_ref, bias_ref, ws_ref, bfull_ref, gath_ref,
             xo_ref, gd_ref, u_ref, vhat_ref, svo_ref, y_ref, rstd_ref, win_v, wout_v, sv_v, sems):
        _load_group([(gath_ref, 0, n_in, win_v), (gath_ref, n_in, n_out, wout_v)], sems)
        xv = x_ref[...]
        h = _rms_fwd(xv, g_ref[...])[0].astype(BF16)
        zpre = _nt(h, win_v[...])
        u, vhat, rstd, vn, gelu_d = _sgu_stats(zpre, gain_ref[...], bias_ref[...])
        gd_ref[...] = gelu_d.astype(BF16)
        u_ref[...] = u.astype(BF16)
        vhat_ref[...] = vhat.astype(BF16)
        rstd_ref[...] = rstd
        _spatial_fwd(_tril_weights(ws_ref), vn.astype(BF16), bfull_ref, sv_v, tm)
        sv = sv_v[...]
        svo_ref[...] = sv.astype(BF16)
        y = (u * sv).astype(BF16)
        y_ref[...] = y
        xo_ref[...] = xv + _nn(y, wout_v[...])

    return _hosting_call(
        body, "mixer_a_fwd", t_tok // tm, [x, g, gain, bias, ws, bfull, gath],
        in_specs=[_row_spec(tm, d), _const_spec((1, d)), _const_spec((1, e)), _const_spec((1, e)),
                  _const_spec((HEADS, CHUNK, CHUNK)), _const_spec((CHUNK, e)), ANY],
        out_specs=[_row_spec(tm, d), _row_spec(tm, e2), _row_spec(tm, e), _row_spec(tm, e), _row_spec(tm, e),
                   _row_spec(tm, e), _row_spec(tm, 1)],
        out_shape=[jax.ShapeDtypeStruct((t_tok, d), F32), jax.ShapeDtypeStruct((t_tok, e2), BF16),
                   jax.ShapeDtypeStruct((t_tok, e), BF16), jax.ShapeDtypeStruct((t_tok, e), BF16),
                   jax.ShapeDtypeStruct((t_tok, e), BF16), jax.ShapeDtypeStruct((t_tok, e), BF16),
                   jax.ShapeDtypeStruct((t_tok, 1), F32)],
        scratch=[pltpu.VMEM((e2, d), BF16), pltpu.VMEM((e, d), BF16), pltpu.VMEM((tm, e), F32),
                 pltpu.SemaphoreType.DMA((2 * N_DEV,))],
        hosted=hosted)


def _mixer_a_bwd(dout, x, gd, u_sav, vhat_sav, sv_sav, rstd_sav, g, gath, gain, bias, ws, tm, hosted=()):
    t_tok, d = x.shape
    e = gain.shape[1]
    e2 = 2 * e
    n_in, n_out = e2 // N_DEV, e // N_DEV
    n_steps = t_tok // tm

    def body(dout_ref, x_ref, gd_ref, u_ref, vhat_ref, sv_ref, rstd_ref, g_ref, gain_ref, bias_ref, ws_ref, gath_ref,
             dx_ref, dxb_ref, h_ref, dz_ref, dg_ref, dgain_ref, dbias_ref, dws_ref, dbso_ref,
             win_v, wout_v, dvn_v, dbs_ref, sems):
        i = pl.program_id(0)
        _load_group([(gath_ref, 0, n_in, win_v), (gath_ref, n_in, n_out, wout_v)], sems)

        @pl.when(i == 0)
        def _():
            dg_ref[...] = jnp.zeros_like(dg_ref)
            dgain_ref[...] = jnp.zeros_like(dgain_ref)
            dbias_ref[...] = jnp.zeros_like(dbias_ref)
            dws_ref[...] = jnp.zeros_like(dws_ref)
            dbs_ref[...] = jnp.zeros_like(dbs_ref)

        xv = x_ref[...]
        gv = g_ref[...]
        hv, xhat, r = _rms_fwd(xv, gv)
        h_ref[...] = hv.astype(BF16)
        gain_v = gain_ref[...]
        vhat = vhat_ref[...].astype(F32)
        vn_bf = (vhat * gain_v + bias_ref[...]).astype(BF16)
        wt = _tril_weights(ws_ref)

        dov = dout_ref[...]
        dy = _nt(dov.astype(BF16), wout_v[...])
        du = dy * sv_ref[...].astype(F32)
        dsv = dy * u_ref[...].astype(F32)
        dsv_bf = dsv.astype(BF16)
        for ci in range(tm // CHUNK):
            rows = slice(ci * CHUNK, (ci + 1) * CHUNK)
            dbs_ref[...] += dsv[rows, :]
            for h in range(HEADS):
                cols = slice(h * GROUP, (h + 1) * GROUP)
                dvn_v[rows, cols] = _tn(wt[h], dsv_bf[rows, cols])
                dws_ref[h] += _nt(dsv_bf[rows, cols], vn_bf[rows, cols])
        dvn = dvn_v[...]
        dgain_ref[...] += _col_sum(dvn * vhat)
        dbias_ref[...] += _col_sum(dvn)
        dvhat = dvn * gain_v
        dv = rstd_ref[...] * (dvhat - _row_mean(dvhat) - vhat * _row_mean(dvhat * vhat))
        dzpre = (jnp.concatenate([du, dv], axis=1) * gd_ref[...].astype(F32)).astype(BF16)
        dz_ref[...] = dzpre
        dh = _nn(dzpre, win_v[...])
        dxr, dg_row = _rms_bwd(dh, xhat, r, gv)
        dg_ref[...] += dg_row
        dx = dov + dxr
        dx_ref[...] = dx
        dxb_ref[...] = dx.astype(BF16)

        @pl.when(i == n_steps - 1)
        def _():
            rr = lax.broadcasted_iota(jnp.int32, (CHUNK, CHUNK), 0)
            cc = lax.broadcasted_iota(jnp.int32, (CHUNK, CHUNK), 1)
            for h in range(HEADS):
                dws_ref[h] = jnp.where(rr >= cc, dws_ref[h], 0.0)
                dbso_ref[h] = jnp.sum(dbs_ref[:, h * GROUP:(h + 1) * GROUP], axis=1, keepdims=True)

    return _hosting_call(
        body, "mixer_a_bwd", n_steps, [dout, x, gd, u_sav, vhat_sav, sv_sav, rstd_sav, g, gain, bias, ws, gath],
        in_specs=[_row_spec(tm, d), _row_spec(tm, d), _row_spec(tm, e2), _row_spec(tm, e), _row_spec(tm, e),
                  _row_spec(tm, e), _row_spec(tm, 1), _const_spec((1, d)),
                  _const_spec((1, e)), _const_spec((1, e)), _const_spec((HEADS, CHUNK, CHUNK)), ANY],
        out_specs=[_row_spec(tm, d), _row_spec(tm, d), _row_spec(tm, d), _row_spec(tm, e2),
                   _const_spec((1, d)), _const_spec((1, e)), _const_spec((1, e)),
                   _const_spec((HEADS, CHUNK, CHUNK)), _const_spec((HEADS, CHUNK, 1))],
        out_shape=[jax.ShapeDtypeStruct((t_tok, d), F32), jax.ShapeDtypeStruct((t_tok, d), BF16),
                   jax.ShapeDtypeStruct((t_tok, d), BF16), jax.ShapeDtypeStruct((t_tok, e2), BF16),
                   jax.ShapeDtypeStruct((1, d), F32), jax.ShapeDtypeStruct((1, e), F32),
                   jax.ShapeDtypeStruct((1, e), F32), jax.ShapeDtypeStruct((HEADS, CHUNK, CHUNK), F32),
                   jax.ShapeDtypeStruct((HEADS, CHUNK, 1), F32)],
        scratch=[pltpu.VMEM((e2, d), BF16), pltpu.VMEM((e, d), BF16), pltpu.VMEM((tm, e), F32),
                 pltpu.VMEM((CHUNK, e), F32), pltpu.SemaphoreType.DMA((2 * N_DEV,))],
        hosted=hosted)


def _ffn_fwd(x, g, srcs, nf, tm, name, hosted=()):
    t_tok, d = x.shape
    f = nf * N_DEV
    firsts = [first for _, first in srcs]

    def body(x_ref, g_ref, sg_ref, su_ref, sd_ref, xo_ref, gate_ref, up_ref, wg_v, wu_v, wd_v, sems):
        _load_group(
            [(sg_ref, firsts[0], nf, wg_v), (su_ref, firsts[1], nf, wu_v), (sd_ref, firsts[2], nf, wd_v)], sems)
        xv = x_ref[...]
        h = _rms_fwd(xv, g_ref[...])[0].astype(BF16)
        gate = _nt(h, wg_v[...])
        up = _nt(h, wu_v[...])
        gate_ref[...] = gate.astype(BF16)
        up_ref[...] = up.astype(BF16)
        act = (gate * _sigmoid(gate) * up).astype(BF16)
        xo_ref[...] = xv + _nn(act, wd_v[...])

    return _hosting_call(
        body, name, t_tok // tm, [x, g] + [arr for arr, _ in srcs],
        in_specs=[_row_spec(tm, d), _const_spec((1, d)), ANY, ANY, ANY],
        out_specs=[_row_spec(tm, d), _row_spec(tm, f), _row_spec(tm, f)],
        out_shape=[jax.ShapeDtypeStruct((t_tok, d), F32), jax.ShapeDtypeStruct((t_tok, f), BF16),
                   jax.ShapeDtypeStruct((t_tok, f), BF16)],
        scratch=[pltpu.VMEM((f, d), BF16), pltpu.VMEM((f, d), BF16), pltpu.VMEM((f, d), BF16),
                 pltpu.SemaphoreType.DMA((3 * N_DEV,))],
        hosted=hosted)


def _ffn_bwd(dout, x, gate, up, g, srcs, nf, tm, name, hosted=()):
    t_tok, d = x.shape
    f = nf * N_DEV
    firsts = [first for _, first in srcs]
    per_chunk = -(-f // (FFN_CHUNKS * MXU_WIDTH)) * MXU_WIDTH
    bounds = [min(ck * per_chunk, f) for ck in range(FFN_CHUNKS + 1)]

    def body(dout_ref, x_ref, gate_ref, up_ref, g_ref, sg_ref, su_ref, sd_ref,
             dx_ref, dxb_ref, h_ref, act_ref, dgu_ref, dg_ref, wg_v, wu_v, wd_v, sems):
        _load_group(
            [(sg_ref, firsts[0], nf, wg_v), (su_ref, firsts[1], nf, wu_v), (sd_ref, firsts[2], nf, wd_v)], sems)

        @pl.when(pl.program_id(0) == 0)
        def _():
            dg_ref[...] = jnp.zeros_like(dg_ref)

        xv = x_ref[...]
        gv = g_ref[...]
        hv, xhat, r = _rms_fwd(xv, gv)
        h_ref[...] = hv.astype(BF16)
        dov = dout_ref[...]
        dob = dov.astype(BF16)
        dh = None
        for ck in range(FFN_CHUNKS):
            cols = slice(bounds[ck], bounds[ck + 1])
            gate_v = gate_ref[:, cols].astype(F32)
            up_v = up_ref[:, cols].astype(F32)
            sig = _sigmoid(gate_v)
            silu = gate_v * sig
            act_ref[:, cols] = (silu * up_v).astype(BF16)
            dact = _nt(dob, wd_v[cols, :])
            dup = (dact * silu).astype(BF16)
            dgate = (dact * up_v * (sig * (1.0 + gate_v * (1.0 - sig)))).astype(BF16)
            dgu_ref[:, cols] = dgate
            dgu_ref[:, f + bounds[ck]:f + bounds[ck + 1]] = dup
            part = _nn(dgate, wg_v[cols, :]) + _nn(dup, wu_v[cols, :])
            dh = part if dh is None else dh + part
        dxr, dg_row = _rms_bwd(dh, xhat, r, gv)
        dg_ref[...] += dg_row
        dx = dov + dxr
        dx_ref[...] = dx
        dxb_ref[...] = dx.astype(BF16)

    return _hosting_call(
        body, name, t_tok // tm, [dout, x, gate, up, g] + [arr for arr, _ in srcs],
        in_specs=[_row_spec(tm, d), _row_spec(tm, d), _row_spec(tm, f), _row_spec(tm, f), _const_spec((1, d)),
                  ANY, ANY, ANY],
        out_specs=[_row_spec(tm, d), _row_spec(tm, d), _row_spec(tm, d), _row_spec(tm, f), _row_spec(tm, 2 * f),
                   _const_spec((1, d))],
        out_shape=[jax.ShapeDtypeStruct((t_tok, d), F32), jax.ShapeDtypeStruct((t_tok, d), BF16),
                   jax.ShapeDtypeStruct((t_tok, d), BF16), jax.ShapeDtypeStruct((t_tok, f), BF16),
                   jax.ShapeDtypeStruct((t_tok, 2 * f), BF16), jax.ShapeDtypeStruct((1, d), F32)],
        scratch=[pltpu.VMEM((f, d), BF16), pltpu.VMEM((f, d), BF16), pltpu.VMEM((f, d), BF16),
                 pltpu.SemaphoreType.DMA((3 * N_DEV,))],
        hosted=hosted)


def _shift_down(z, k, prev_rows):
    row = lax.broadcasted_iota(jnp.int32, z.shape, 0)
    out = pltpu.roll(z, k, 0)
    for j in range(k):
        out = jnp.where(row == j, prev_rows[j], out)
    return out


def _shift_up(z, k, next_rows):
    tm = z.shape[0]
    row = lax.broadcasted_iota(jnp.int32, z.shape, 0)
    out = pltpu.roll(z, tm - k, 0)
    for j in range(k):
        out = jnp.where(row == tm - k + j, next_rows[j], out)
    return out


def _mixer_b_fwd(x, g, gath, conv_w, tm, seq, hosted=()):
    t_tok, d = x.shape
    e = conv_w.shape[1]
    e3 = 3 * e
    n_in, n_out = e3 // N_DEV, e // N_DEV
    tiles_per_seq = seq // tm

    def body(x_ref, g_ref, cw_ref, gath_ref, xo_ref, p_ref, win_v, wout_v, tail_v, sems):
        i = pl.program_id(0)
        _load_group([(gath_ref, 0, n_in, win_v), (gath_ref, n_in, n_out, wout_v)], sems)

        @pl.when(i % tiles_per_seq == 0)
        def _():
            tail_v[...] = jnp.zeros_like(tail_v)

        xv = x_ref[...]
        h = _rms_fwd(xv, g_ref[...])[0].astype(BF16)
        p = _nt(h, win_v[...])
        p_ref[...] = p.astype(BF16)
        z = p[:, e:2 * e] * p[:, 2 * e:]
        prev = [tail_v[SUBLANES - 2:SUBLANES - 1, :], tail_v[SUBLANES - 1:SUBLANES, :]]
        conv = (cw_ref[2:3, :] * z + cw_ref[1:2, :] * _shift_down(z, 1, prev[1:])
                + cw_ref[0:1, :] * _shift_down(z, 2, prev))
        tail_v[...] = z[tm - SUBLANES:, :]
        y = (p[:, :e] * conv).astype(BF16)
        xo_ref[...] = xv + _nn(y, wout_v[...])

    return _hosting_call(
        body, "mixer_b_fwd", t_tok // tm, [x, g, conv_w, gath],
        in_specs=[_row_spec(tm, d), _const_spec((1, d)), _const_spec((SUBLANES, e)), ANY],
        out_specs=[_row_spec(tm, d), _row_spec(tm, e3)],
        out_shape=[jax.ShapeDtypeStruct((t_tok, d), F32), jax.ShapeDtypeStruct((t_tok, e3), BF16)],
        scratch=[pltpu.VMEM((e3, d), BF16), pltpu.VMEM((e, d), BF16), pltpu.VMEM((SUBLANES, e), F32),
                 pltpu.SemaphoreType.DMA((2 * N_DEV,))],
        hosted=hosted)


def _mixer_b_bwd(dout, x, p, g, gath, conv_w, tm, seq, hosted=()):
    t_tok, d = x.shape
    e = conv_w.shape[1]
    e3 = 3 * e
    n_in, n_out = e3 // N_DEV, e // N_DEV
    tiles_per_seq = seq // tm
    halo_per_tile = tm // HALO
    n_halo = t_tok // HALO

    def body(dout_ref, dnext_ref, x_ref, p_ref, pprev_ref, pnext_ref, g_ref, cw_ref, gath_ref,
             dx_ref, dxb_ref, h_ref, y_ref, dp_ref, dg_ref, dcw_ref, win_v, wout_v, sems):
        i = pl.program_id(0)
        _load_group([(gath_ref, 0, n_in, win_v), (gath_ref, n_in, n_out, wout_v)], sems)

        @pl.when(i == 0)
        def _():
            dg_ref[...] = jnp.zeros_like(dg_ref)
            dcw_ref[...] = jnp.zeros_like(dcw_ref)

        first = (i % tiles_per_seq == 0).astype(F32)
        last = (i % tiles_per_seq == tiles_per_seq - 1).astype(F32)
        xv = x_ref[...]
        gv = g_ref[...]
        hv, xhat, r = _rms_fwd(xv, gv)
        h_ref[...] = hv.astype(BF16)
        pv = p_ref[...].astype(F32)
        bg, cg, hx = pv[:, :e], pv[:, e:2 * e], pv[:, 2 * e:]
        z = cg * hx
        pprev = pprev_ref[...].astype(F32)
        zprev = pprev[:, e:2 * e] * pprev[:, 2 * e:] * (1.0 - first)
        prev = [zprev[HALO - 2:HALO - 1, :], zprev[HALO - 1:HALO, :]]
        zs1 = _shift_down(z, 1, prev[1:])
        zs2 = _shift_down(z, 2, prev)
        w0, w1, w2 = cw_ref[0:1, :], cw_ref[1:2, :], cw_ref[2:3, :]
        conv = w2 * z + w1 * zs1 + w0 * zs2
        y_ref[...] = (bg * conv).astype(BF16)

        dov = dout_ref[...]
        wout_bf = wout_v[...]
        dy = _nt(dov.astype(BF16), wout_bf)
        dconv = dy * bg
        dnext = _nt(dnext_ref[...].astype(BF16), wout_bf) * pnext_ref[:, :e].astype(F32) * (1.0 - last)
        nxt = [dnext[0:1, :], dnext[1:2, :]]
        dz = w2 * dconv + w1 * _shift_up(dconv, 1, nxt[:1]) + w0 * _shift_up(dconv, 2, nxt)
        dcw_ref[0:1, :] += _col_sum(dconv * zs2)
        dcw_ref[1:2, :] += _col_sum(dconv * zs1)
        dcw_ref[2:3, :] += _col_sum(dconv * z)
        dp = jnp.concatenate([dy * conv, dz * hx, dz * cg], axis=1).astype(BF16)
        dp_ref[...] = dp
        dh = _nn(dp, win_v[...])
        dxr, dg_row = _rms_bwd(dh, xhat, r, gv)
        dg_ref[...] += dg_row
        dx = dov + dxr
        dx_ref[...] = dx
        dxb_ref[...] = dx.astype(BF16)

    prev_spec = lambda w: pl.BlockSpec((HALO, w), lambda i: (jnp.maximum(i * halo_per_tile - 1, 0), 0))
    next_spec = lambda w: pl.BlockSpec((HALO, w), lambda i: (jnp.minimum((i + 1) * halo_per_tile, n_halo - 1), 0))
    return _hosting_call(
        body, "mixer_b_bwd", t_tok // tm, [dout, dout, x, p, p, p, g, conv_w, gath],
        in_specs=[_row_spec(tm, d), next_spec(d), _row_spec(tm, d), _row_spec(tm, e3), prev_spec(e3), next_spec(e3),
                  _const_spec((1, d)), _const_spec((SUBLANES, e)), ANY],
        out_specs=[_row_spec(tm, d), _row_spec(tm, d), _row_spec(tm, d), _row_spec(tm, e), _row_spec(tm, e3),
                   _const_spec((1, d)), _const_spec((SUBLANES, e))],
        out_shape=[jax.ShapeDtypeStruct((t_tok, d), F32), jax.ShapeDtypeStruct((t_tok, d), BF16),
                   jax.ShapeDtypeStruct((t_tok, d), BF16), jax.ShapeDtypeStruct((t_tok, e), BF16),
                   jax.ShapeDtypeStruct((t_tok, e3), BF16), jax.ShapeDtypeStruct((1, d), F32),
                   jax.ShapeDtypeStruct((SUBLANES, e), F32)],
        scratch=[pltpu.VMEM((e3, d), BF16), pltpu.VMEM((e, d), BF16), pltpu.SemaphoreType.DMA((2 * N_DEV,))],
        hosted=hosted)


def _loss_head(x, target, g, tm):
    t_tok, d = x.shape

    def body(x_ref, t_ref, g_ref, loss_ref, dx_ref, dxb_ref, dg_ref):
        @pl.when(pl.program_id(0) == 0)
        def _():
            loss_ref[...] = jnp.zeros_like(loss_ref)
            dg_ref[...] = jnp.zeros_like(dg_ref)

        gv = g_ref[...]
        y, xhat, r = _rms_fwd(x_ref[...], gv)
        err = y - t_ref[...]
        loss_ref[...] += 0.5 * jnp.sum(_row_mean(err * err), axis=0, keepdims=True)
        dxr, dg_row = _rms_bwd(err * (1.0 / d), xhat, r, gv)
        dg_ref[...] += dg_row
        dx_ref[...] = dxr
        dxb_ref[...] = dxr.astype(BF16)

    return pl.pallas_call(
        body, name="loss_head", grid=(t_tok // tm,),
        in_specs=[_row_spec(tm, d), _row_spec(tm, d), _const_spec((1, d))],
        out_specs=[_const_spec((1, 1)), _row_spec(tm, d), _row_spec(tm, d), _const_spec((1, d))],
        out_shape=[jax.ShapeDtypeStruct((1, 1), F32), jax.ShapeDtypeStruct((t_tok, d), F32),
                   jax.ShapeDtypeStruct((t_tok, d), BF16), jax.ShapeDtypeStruct((1, d), F32)],
        compiler_params=_params(),
    )(x, target, g)


def _wgrad(a, b, bm, name, hosted=()):
    t_tok, m = a.shape
    n = b.shape[1]

    def body(a_ref, b_ref, o_ref):
        o_ref[...] = _tn(a_ref[...], b_ref[...]).astype(o_ref.dtype)

    outs, h_outs = _hosting_call(
        body, name, m // bm, [a, b],
        in_specs=[pl.BlockSpec((t_tok, bm), lambda i: (0, i)), _const_spec((t_tok, n))],
        out_specs=[pl.BlockSpec((bm, n), lambda i: (i, 0))],
        out_shape=[jax.ShapeDtypeStruct((m, n), BF16)],
        scratch=[], hosted=hosted)
    return (outs[0], h_outs) if hosted else outs[0]


def _sum_slots(land, rb, name):
    n_slots, rows, cols = land.shape

    def body(l_ref, o_ref):
        acc = l_ref[0].astype(F32)
        for k in range(1, n_slots):
            acc = acc + l_ref[k].astype(F32)
        o_ref[...] = acc

    return pl.pallas_call(
        body, name=name, grid=(rows // rb,),
        in_specs=[pl.BlockSpec((n_slots, rb, cols), lambda i: (0, i, 0))],
        out_specs=pl.BlockSpec((rb, cols), lambda i: (i, 0)),
        out_shape=jax.ShapeDtypeStruct((rows, cols), F32),
        compiler_params=_params(sequential=False),
    )(land)


def _adamw(w, grad, m, v, rb, name):
    rows, cols = w.shape
    c1 = 1.0 / (1.0 - ADAM_B1 ** ADAM_STEP)
    c2 = 1.0 / (1.0 - ADAM_B2 ** ADAM_STEP)

    def body(w_ref, g_ref, m_ref, v_ref, d_ref, mo_ref, vo_ref):
        gv = g_ref[...]
        mn = ADAM_B1 * m_ref[...] + (1.0 - ADAM_B1) * gv
        vn = ADAM_B2 * v_ref[...] + (1.0 - ADAM_B2) * (gv * gv)
        mo_ref[...] = mn
        vo_ref[...] = vn
        d_ref[...] = -ADAM_LR * ((mn * c1) / (jnp.sqrt(vn * c2) + ADAM_EPS) + ADAM_WD * w_ref[...])

    spec = pl.BlockSpec((rb, cols), lambda i: (i, 0))
    shape = jax.ShapeDtypeStruct((rows, cols), F32)
    return pl.pallas_call(
        body, name=name, grid=(rows // rb,),
        in_specs=[spec] * 4, out_specs=[spec] * 3, out_shape=[shape] * 3,
        compiler_params=_params(sequential=False),
    )(w, grad, m, v)


def _pack_shards(groups, name, hosted=()):
    flat = [part for group in groups for part, _ in group]
    rows = [[p.shape[1] if turn else p.shape[0] for p, turn in group] for group in groups]
    width = flat[0].shape[0] if groups[0][0][1] else flat[0].shape[1]

    def body(*refs):
        ins, outs = refs[:len(flat)], refs[len(flat):]
        k = 0
        for gi, group in enumerate(groups):
            off = 0
            for (_, turn), n in zip(group, rows[gi]):
                part = ins[k][...].astype(BF16)
                if turn:
                    r = lax.broadcasted_iota(jnp.int32, (n, n), 0)
                    c = lax.broadcasted_iota(jnp.int32, (n, n), 1)
                    part = _nt((r == c).astype(BF16), part).astype(BF16)
                outs[gi][off:off + n, :] = part
                off += n
                k += 1

    return _hosting_call(
        body, name, 1, flat,
        in_specs=[_const_spec(p.shape) for p in flat],
        out_specs=[_const_spec((sum(r), width)) for r in rows],
        out_shape=[jax.ShapeDtypeStruct((sum(r), width), BF16) for r in rows],
        scratch=[], hosted=hosted)


def _split_bf16(a):
    hi = a.astype(BF16)
    rest = a - hi.astype(F32)
    mid = rest.astype(BF16)
    return hi, mid, (rest - mid.astype(F32)).astype(BF16)


def _reduce_adamw(lands, w, m, v, transpose, name):
    n_layers, rows_w, cols_w = w.shape
    c1 = 1.0 / (1.0 - ADAM_B1 ** ADAM_STEP)
    c2 = 1.0 / (1.0 - ADAM_B2 ** ADAM_STEP)
    if transpose:
        n, tiles = cols_w, rows_w // MXU_WIDTH
        blk = (MXU_WIDTH, n)
        land_specs = [pl.BlockSpec((N_CHIP, n, MXU_WIDTH), lambda i, b=first // n: (0, b, i % tiles))
                      for _, first in lands]
    else:
        n, tiles = rows_w, 2
        blk = (n // tiles, cols_w)
        land_specs = [pl.BlockSpec((N_CHIP,) + blk, lambda i, b=first // blk[0]: (0, b + i % tiles, 0))
                      for _, first in lands]
    for _, first in lands:
        assert first % (n if transpose else blk[0]) == 0

    def body(*refs):
        land_refs = refs[:n_layers]
        w_ref, m_ref, v_ref, g_ref, d_ref, mo_ref, vo_ref = refs[n_layers:]
        layer = pl.program_id(0) // tiles

        def total(ref):
            acc = ref[0].astype(F32)
            for q in range(1, N_CHIP):
                acc = acc + ref[q].astype(F32)
            return acc

        gv = total(land_refs[0])
        for k in range(1, n_layers):
            gv = jnp.where(layer == k, total(land_refs[k]), gv)
        if transpose:
            r = lax.broadcasted_iota(jnp.int32, (MXU_WIDTH, MXU_WIDTH), 0)
            c = lax.broadcasted_iota(jnp.int32, (MXU_WIDTH, MXU_WIDTH), 1)
            eye = (r == c).astype(BF16)
            hi, mid, lo = _split_bf16(gv)
            gv = _nt(eye, hi) + _nt(eye, mid) + _nt(eye, lo)
        g_ref[...] = gv
        mn = ADAM_B1 * m_ref[...] + (1.0 - ADAM_B1) * gv
        vn = ADAM_B2 * v_ref[...] + (1.0 - ADAM_B2) * (gv * gv)
        mo_ref[...] = mn
        vo_ref[...] = vn
        d_ref[...] = -ADAM_LR * ((mn * c1) / (jnp.sqrt(vn * c2) + ADAM_EPS) + ADAM_WD * w_ref[...])

    spec = pl.BlockSpec((None,) + blk, lambda i: (i // tiles, i % tiles, 0))
    shape = jax.ShapeDtypeStruct(w.shape, F32)
    return pl.pallas_call(
        body, name=name, grid=(n_layers * tiles,),
        in_specs=land_specs + [spec] * 3, out_specs=[spec] * 4, out_shape=[shape] * 4,
        compiler_params=_params(sequential=False),
    )(*[land for land, _ in lands], w, m, v)


def _pack_small(parts, rows):
    flat = jnp.concatenate([p.reshape(-1).astype(F32) for p in parts])
    return jnp.pad(flat, (0, rows * LANES - flat.shape[0])).reshape(rows, LANES)


def _unpack_small(packed, shapes):
    flat = packed.reshape(-1)
    out = []
    pos = 0
    for s in shapes:
        n = math.prod(s)
        out.append(flat[pos:pos + n].reshape(s))
        pos += n
    return out


def kernel(x, mix_norm, ffn_norm, a_w_in, a_v_gain, a_v_bias, a_w_s, a_b_s, a_w_out, b_w_in, b_conv_w, b_w_out, ffn_w_gate, ffn_w_up, ffn_w_down, final_norm, loss_target, m_mix_norm, m_ffn_norm, m_a_w_in, m_a_v_gain, m_a_v_bias, m_a_w_s, m_a_b_s, m_a_w_out, m_b_w_in, m_b_conv_w, m_b_w_out, m_ffn_w_gate, m_ffn_w_up, m_ffn_w_down, m_final_norm, v_mix_norm, v_ffn_norm, v_a_w_in, v_a_v_gain, v_a_v_bias, v_a_w_s, v_a_b_s, v_a_w_out, v_b_w_in, v_b_conv_w, v_b_w_out, v_ffn_w_gate, v_ffn_w_up, v_ffn_w_down, v_final_norm):
    bsz, seq, d = x.shape
    t_tok = bsz * seq
    me = _my_index()
    xt = x.reshape(t_tok, d)
    target = loss_target.reshape(t_tok, d)
    e_a = a_v_gain.shape[1]
    e_b = b_w_out.shape[1] * N_DEV
    n_layers = ffn_w_gate.shape[0]
    f_shard = ffn_w_gate.shape[2]
    f_full = f_shard * N_DEV

    conv_pad = jnp.pad(b_conv_w[0], ((0, SUBLANES - CONV_W), (0, 0)))
    sh_a = jnp.concatenate([a_w_in[0].T, a_w_out[0]]).astype(BF16)
    bfull = jnp.repeat(a_b_s[0].T, GROUP, axis=1)

    (sh_b, sh_f0, sh_f1g, sh_f1ud), (gath_a, conv_g) = _pack_shards(
        [[(b_w_in[0], True), (b_w_out[0], False)],
         [(ffn_w_gate[0], True), (ffn_w_up[0], True), (ffn_w_down[0], False)],
         [(ffn_w_gate[1], True)],
         [(ffn_w_up[1], True), (ffn_w_down[1], False)]],
        "pack_shards", hosted=[_HostedGathers([sh_a, conv_pad], 0)])
    conv_full = jnp.pad(conv_g[:, :CONV_W, :].transpose(1, 0, 2).reshape(CONV_W, e_b), ((0, SUBLANES - CONV_W), (0, 0)))
    (x1, gd_a, u_a, vhat_a, sv_a, y_a, rstd_a), (gath_f0,) = _mixer_a_fwd(
        xt, mix_norm[0:1], gath_a, a_v_gain, a_v_bias, a_w_s[0], bfull, tm=256,
        hosted=[_HostedGathers([sh_f0], mid_lead=2)])
    srcs0 = [(gath_f0, 0), (gath_f0, f_shard), (gath_f0, 2 * f_shard)]
    (x2, gate0, up0), (gath_b, gath_f1g) = _ffn_fwd(x1, ffn_norm[0:1], srcs0, f_shard, tm=256, name="ffn_fwd0",
                                                    hosted=[_HostedGathers([sh_b, sh_f1g], mid_lead=2)])
    (x3, p_b), (gath_f1ud,) = _mixer_b_fwd(x2, mix_norm[1:2], gath_b, conv_full, tm=256, seq=seq,
                                           hosted=[_HostedGathers([sh_f1ud], mid_lead=2)])
    srcs1 = [(gath_f1g, 0), (gath_f1ud, 0), (gath_f1ud, f_shard)]
    (x4, gate1, up1), _ = _ffn_fwd(x3, ffn_norm[1:2], srcs1, f_shard, tm=256, name="ffn_fwd1")
    loss_part, dx4, dx4_bf, d_final = _loss_head(x4, target, final_norm.reshape(1, d), tm=512)

    ffn_entries = [(0, 0, f_shard), (0, f_full, f_shard), (1, 0, f_shard)]
    (dx3, dx3_bf, h_f1, act1, dgu1, d_fn1), _ = _ffn_bwd(dx4, x3, gate1, up1, ffn_norm[1:2], srcs1, f_shard, tm=256,
                                                         name="ffn_bwd1")
    g_down1 = _wgrad(act1, dx4_bf, 256, "wgrad_down1")
    g_gu1 = _wgrad(dgu1, h_f1, 512, "wgrad_gate_up1")
    ps_f1 = _pair_reduce([g_gu1, g_down1], ffn_entries, "pair_reduce_f1")
    (dx2, dx2_bf, h_b, y_b, dp_b, d_mn1, d_conv), (land_f1gu,) = _mixer_b_bwd(
        dx3, x2, p_b, mix_norm[1:2], gath_b, conv_full, tm=256, seq=seq,
        hosted=[_HostedChipScatter(ps_f1, 0, 2 * f_shard)])
    g_b_out = _wgrad(y_b, dx3_bf, 256, "wgrad_b_out")
    g_b_in = _wgrad(dp_b, h_b, 512, "wgrad_b_in")
    ps_b = _pair_reduce([g_b_in, g_b_out], [(0, 0, b_w_in.shape[2]), (1, 0, b_w_out.shape[1])], "pair_reduce_b")
    (dx1, dx1_bf, h_f0, act0, dgu0, d_fn0), (land_f1d, land_b) = _ffn_bwd(
        dx2, x1, gate0, up0, ffn_norm[0:1], srcs0, f_shard, tm=256, name="ffn_bwd0",
        hosted=[_HostedChipScatter(ps_f1, 2 * f_shard, f_shard), _HostedChipScatter(ps_b)])
    g_down0 = _wgrad(act0, dx2_bf, 256, "wgrad_down0")
    g_gu0 = _wgrad(dgu0, h_f0, 512, "wgrad_gate_up0")
    ps_f0 = _pair_reduce([g_gu0, g_down0], ffn_entries, "pair_reduce_f0")
    g_a_out = _wgrad(y_a, dx1_bf, 256, "wgrad_a_out")
    ps_ao = _pair_reduce([g_a_out], [(0, 0, a_w_out.shape[1])], "pair_reduce_a_out")
    (dx0, _, h_a, dz_a, d_mn0, d_gain, d_bias, d_ws, d_bs_acc), (land_f0, land_ao) = _mixer_a_bwd(
        dx1, xt, gd_a, u_a, vhat_a, sv_a, rstd_a, mix_norm[0:1], gath_a, a_v_gain, a_v_bias, a_w_s[0], tm=256,
        hosted=[_HostedChipScatter(ps_f0), _HostedChipScatter(ps_ao)])
    d_bs = d_bs_acc.reshape(HEADS, CHUNK)

    small_grads = [jnp.concatenate([d_mn0, d_mn1]), jnp.concatenate([d_fn0, d_fn1]), d_gain, d_bias, d_ws, d_bs,
                   d_final, d_conv[:CONV_W], loss_part]
    small_shapes = [(n_layers, d), (n_layers, d), (1, e_a), (1, e_a), (1, HEADS, CHUNK, CHUNK), (1, HEADS, CHUNK), (d,),
                    (CONV_W, e_b), ()]
    n_small = sum(math.prod(s) for s in small_shapes)
    blk_rows = -(-n_small // (N_DEV * LANES * SUBLANES)) * SUBLANES
    small_rows = blk_rows * N_DEV
    packed = _pack_small(small_grads, small_rows)
    g_a_in, (small_land,) = _wgrad(dz_a, h_a, 512, "wgrad_a_in", hosted=[_HostedScatterAll(packed)])
    ps_ai = _pair_reduce([g_a_in], [(0, 0, a_w_in.shape[2])], "pair_reduce_a_in")
    small_sum = _sum_slots(small_land, blk_rows, "sum_small")
    land_ai, small_gath = _exchange([_HostedChipScatter(ps_ai), _HostedGathers([small_sum], 0)], "tail_exchange")
    small_all = small_gath.reshape(small_rows, LANES)

    (gr_mix, gr_ffn, gr_gain, gr_bias, gr_ws, gr_bs, gr_final, gr_conv_full, loss) = _unpack_small(small_all, small_shapes)
    gr_conv = lax.dynamic_slice_in_dim(gr_conv_full, me * (e_b // N_DEV), e_b // N_DEV, axis=1)[None]

    small_w = [mix_norm, ffn_norm, a_v_gain, a_v_bias, a_w_s, a_b_s, final_norm]
    small_m = [m_mix_norm, m_ffn_norm, m_a_v_gain, m_a_v_bias, m_a_w_s, m_a_b_s, m_final_norm]
    small_v = [v_mix_norm, v_ffn_norm, v_a_v_gain, v_a_v_bias, v_a_w_s, v_a_b_s, v_final_norm]
    small_g = [gr_mix, gr_ffn, gr_gain, gr_bias, gr_ws, gr_bs, gr_final]
    sm_shapes = small_shapes[:len(small_w)]
    sm_out = _adamw(_pack_small(small_w, small_rows), _pack_small(small_g, small_rows), _pack_small(small_m, small_rows),
                    _pack_small(small_v, small_rows), small_rows, "adamw_small")
    sm_delta, sm_m, sm_v = [_unpack_small(o, sm_shapes) for o in sm_out]

    conv_out = _adamw(b_conv_w[0], gr_conv[0], m_b_conv_w[0], v_b_conv_w[0], CONV_W, "adamw_conv")
    conv_delta, conv_m, conv_v = [o[None] for o in conv_out]

    n_b_in = b_w_in.shape[2]
    res = {
        "a_w_in": _reduce_adamw([(land_ai, 0)], a_w_in, m_a_w_in, v_a_w_in, True, "adamw_a_in"),
        "a_w_out": _reduce_adamw([(land_ao, 0)], a_w_out, m_a_w_out, v_a_w_out, False, "adamw_a_out"),
        "b_w_in": _reduce_adamw([(land_b, 0)], b_w_in, m_b_w_in, v_b_w_in, True, "adamw_b_in"),
        "b_w_out": _reduce_adamw([(land_b, n_b_in)], b_w_out, m_b_w_out, v_b_w_out, False, "adamw_b_out"),
        "ffn_w_gate": _reduce_adamw([(land_f0, 0), (land_f1gu, 0)], ffn_w_gate, m_ffn_w_gate, v_ffn_w_gate, True,
                                    "adamw_gate"),
        "ffn_w_up": _reduce_adamw([(land_f0, f_shard), (land_f1gu, f_shard)], ffn_w_up, m_ffn_w_up, v_ffn_w_up, True,
                                  "adamw_up"),
        "ffn_w_down": _reduce_adamw([(land_f0, 2 * f_shard), (land_f1d, 0)], ffn_w_down, m_ffn_w_down, v_ffn_w_down,
                                    False, "adamw_down"),
    }

    order = ["mix_norm", "ffn_norm", "a_w_in", "a_v_gain", "a_v_bias", "a_w_s", "a_b_s", "a_w_out", "b_w_in",
             "b_conv_w", "b_w_out", "ffn_w_gate", "ffn_w_up", "ffn_w_down", "final_norm"]
    small_names = ["mix_norm", "ffn_norm", "a_v_gain", "a_v_bias", "a_w_s", "a_b_s", "final_norm"]
    grads = {"b_conv_w": gr_conv}
    deltas, new_m, new_v = {}, {}, {}
    for k, name in enumerate(small_names):
        grads[name] = small_g[k]
        deltas[name], new_m[name], new_v[name] = sm_delta[k], sm_m[k], sm_v[k]
    deltas["b_conv_w"], new_m["b_conv_w"], new_v["b_conv_w"] = conv_delta, conv_m, conv_v
    for name, (gg, dl, mm, vv) in res.items():
        grads[name], deltas[name], new_m[name], new_v[name] = gg, dl, mm, vv

    grad_x = dx0.reshape(bsz, seq, d)
    return (loss, grad_x, *[grads[n] for n in order], *[deltas[n] for n in order],
            *[new_m[n] for n in order], *[new_v[n] for n in order])
```

```python
import math

import jax
import jax.numpy as jnp
from jax import lax
from jax.experimental import pallas as pl
from jax.experimental.pallas import tpu as pltpu

F32 = jnp.float32
BF16 = jnp.bfloat16

N_DEV = 8
N_CHIP = 4
CHUNK = 128
HEADS = 16
GROUP = 128
CONV_W = 3
NORM_EPS = 1e-6
GELU_C = math.sqrt(2.0 / math.pi)
GELU_K = 0.044715

ADAM_LR = 0.001
ADAM_B1 = 0.9
ADAM_B2 = 0.999
ADAM_EPS = 1e-08
ADAM_WD = 0.01
ADAM_STEP = 10

LANES = 128
SUBLANES = 8
VMEM_LIMIT = 60 * 1024 * 1024
HALO = 16
MXU_WIDTH = 256
FFN_CHUNKS = 2

MESH = pl.DeviceIdType.MESH
ANY = pl.BlockSpec(memory_space=pl.ANY)


def _params(sequential=True):
    return pltpu.CompilerParams(
        dimension_semantics=("arbitrary",) if sequential else None,
        vmem_limit_bytes=VMEM_LIMIT)


def _nn(a, b):
    return jnp.dot(a, b, preferred_element_type=F32)


def _nt(a, b):
    return lax.dot_general(a, b, (((1,), (1,)), ((), ())), preferred_element_type=F32)


def _tn(a, b):
    return lax.dot_general(a, b, (((0,), (0,)), ((), ())), preferred_element_type=F32)


def _row_mean(a):
    return jnp.mean(a, axis=-1, keepdims=True)


def _col_sum(a):
    return jnp.sum(a, axis=0, keepdims=True)


def _rms_fwd(x, g):
    r = lax.rsqrt(_row_mean(x * x) + NORM_EPS)
    xhat = x * r
    return xhat * g, xhat, r


def _rms_bwd(dh, xhat, r, g):
    a = dh * g
    dx = r * (a - xhat * _row_mean(a * xhat))
    return dx, _col_sum(dh * xhat)


def _gelu_and_grad(x):
    x2 = x * x
    t = jnp.tanh(x * (GELU_C + (GELU_C * GELU_K) * x2))
    half = 0.5 * t + 0.5
    d = half + x * (0.5 - 0.5 * (t * t)) * (GELU_C + (3.0 * GELU_C * GELU_K) * x2)
    return x * half, d


def _sigmoid(x):
    return 1.0 / (1.0 + jnp.exp(-x))


def _row_spec(tm, width):
    return pl.BlockSpec((tm, width), lambda i: (i, 0))


def _const_spec(shape):
    nd = len(shape)
    return pl.BlockSpec(shape, lambda i: (0,) * nd)


def _load_group(parts, sems):
    @pl.when(pl.program_id(0) == 0)
    def _():
        copies = []
        for k, (gath_ref, first, n, dst) in enumerate(parts):
            for j in range(N_DEV):
                copies.append(pltpu.make_async_copy(gath_ref.at[j, pl.ds(first, n), :], dst.at[pl.ds(j * n, n), :],
                                                    sems.at[k * N_DEV + j]))
        for cp in copies:
            cp.start()
        for cp in copies:
            cp.wait()


def _hosting_call(body, name, n_steps, arrays, in_specs, out_specs, out_shape, scratch, hosted=()):
    n_in, n_out, n_scr = len(arrays), len(out_shape), len(scratch)
    h_arrays = [a for h in hosted for a in h.arrays]
    h_shapes = [s for h in hosted for s in h.out_shapes]
    h_sems = [s for h in hosted for s in h.sem_shapes]

    def full_body(*refs):
        pos = 0
        groups = []
        for n in (n_in, len(h_arrays), n_out, len(h_shapes), n_scr, len(h_sems)):
            groups.append(refs[pos:pos + n])
            pos += n
        own_in, h_in, own_out, h_out, own_scr, h_sem = groups
        per_host = []
        pi = po = ps = 0
        for h in hosted:
            ni, no, ns = len(h.arrays), len(h.out_shapes), len(h.sem_shapes)
            per_host.append((h, h_in[pi:pi + ni], h_out[po:po + no], h_sem[ps:ps + ns]))
            pi, po, ps = pi + ni, po + no, ps + ns
        for h, ins, outs, sems in per_host:
            h.begin(ins, outs, sems, n_steps)
        body(*own_in, *own_out, *own_scr)
        for h, ins, outs, sems in per_host:
            h.end(ins, outs, sems, n_steps)

    outs = pl.pallas_call(
        full_body, name=name, grid=(n_steps,),
        in_specs=list(in_specs) + [ANY] * len(h_arrays),
        out_specs=list(out_specs) + [ANY] * len(h_shapes),
        out_shape=list(out_shape) + h_shapes,
        scratch_shapes=list(scratch) + h_sems,
        compiler_params=_params(),
    )(*arrays, *h_arrays)
    return outs[:n_out], outs[n_out:]


def _my_index():
    return 4 * lax.axis_index("x") + 2 * lax.axis_index("y") + lax.axis_index("c")


GATHER_COPIES = 8


class _Gather:
    def __init__(self, shard, out, send_sems, recv_sems, local_sem):
        self.shard, self.out = shard, out
        self.send_sems, self.recv_sems, self.local_sem = send_sems, recv_sems, local_sem
        x, y, c = lax.axis_index("x"), lax.axis_index("y"), lax.axis_index("c")
        self.c = c
        self.me, self.sibling = (x, y, c), (x, y, 1 - c)
        self.xn, self.yn, self.dg = (1 - x, y), (x, 1 - y), (1 - x, 1 - y)
        self.n = shard.shape[0]
        self.half = self.n // 2
        rows_per_tile = SUBLANES * 4 // shard.dtype.itemsize
        self.relays = self.n % 2 == 0 and self.half % rows_per_tile == 0

    def _slot(self, dev, lo=0, hi=None):
        hi = self.n if hi is None else hi
        return self.out.at[4 * dev[0] + 2 * dev[1] + dev[2], pl.ds(lo, hi - lo), :]

    def _copy(self, k, block, to, src=None, lo=0, hi=None):
        return pltpu.make_async_remote_copy(
            src_ref=self._slot(block, lo, hi) if src is None else src, dst_ref=self._slot(block, lo, hi),
            send_sem=self.send_sems.at[k], recv_sem=self.recv_sems.at[k], device_id=to, device_id_type=MESH)

    def _local(self):
        return pltpu.make_async_copy(self.shard, self._slot(self.me), self.local_sem)

    def start(self):
        c = self.c
        self._local().start()
        self._copy(0, self.me, self.sibling, src=self.shard).start()
        self._copy(1, self.me, (*self.xn, c), src=self.shard).start()
        self._copy(2, self.me, (*self.yn, c), src=self.shard).start()
        if not self.relays:
            self._copy(3, self.me, (*self.dg, c), src=self.shard).start()

    def relay(self):
        c = self.c
        if self.relays:
            self._copy(1, (*self.xn, c), self.me).wait_recv()
            self._copy(3, (*self.xn, c), (*self.yn, c), hi=self.half).start()
            self._copy(2, (*self.yn, c), self.me).wait_recv()
            self._copy(4, (*self.yn, c), (*self.xn, c), lo=self.half).start()

    def forward(self):
        c = self.c
        if self.relays:
            self._copy(5, (*self.xn, c), self.sibling).start()
            self._copy(6, (*self.yn, c), self.sibling).start()
            self._copy(3, (*self.dg, c), self.me, hi=self.half).wait_recv()
            self._copy(4, (*self.dg, c), self.me, lo=self.half).wait_recv()
        else:
            self._copy(1, (*self.xn, c), self.me).wait_recv()
            self._copy(5, (*self.xn, c), self.sibling).start()
            self._copy(2, (*self.yn, c), self.me).wait_recv()
            self._copy(6, (*self.yn, c), self.sibling).start()
            self._copy(3, (*self.dg, c), self.me).wait_recv()
        self._copy(7, (*self.dg, c), self.sibling).start()

    def finish(self):
        c = self.c
        self._copy(0, self.sibling, self.me).wait_recv()
        for k, chip in ((5, self.xn), (6, self.yn), (7, self.dg)):
            self._copy(k, (*chip, 1 - c), self.me).wait_recv()
        for k in (0, 1, 2, 5, 6, 7):
            self._copy(k, self.me, self.sibling).wait_send()
        if self.relays:
            self._copy(3, self.me, self.sibling, hi=self.half).wait_send()
            self._copy(4, self.me, self.sibling, lo=self.half).wait_send()
        else:
            self._copy(3, self.me, self.sibling).wait_send()
        self._local().wait()


class _HostedGathers:
    def __init__(self, shards, mid_lead, relay_at=0.56):
        n = len(shards)
        self.arrays = shards
        self.mid_lead, self.relay_at = mid_lead, relay_at
        self.out_shapes = [jax.ShapeDtypeStruct((N_DEV,) + s.shape, s.dtype) for s in shards]
        self.sem_shapes = [pltpu.SemaphoreType.DMA((n, GATHER_COPIES)), pltpu.SemaphoreType.DMA((n, GATHER_COPIES)),
                           pltpu.SemaphoreType.DMA((n,))]

    def _gathers(self, ins, outs, sems):
        return [_Gather(ins[a], outs[a], sems[0].at[a], sems[1].at[a], sems[2].at[a]) for a in range(len(ins))]

    def begin(self, ins, outs, sems, n_steps):
        i = pl.program_id(0)
        forward_step = max(n_steps - 1 - self.mid_lead, 0)
        relay_step = min(int(self.relay_at * n_steps), forward_step)

        @pl.when(i == 0)
        def _():
            for g in self._gathers(ins, outs, sems):
                g.start()

        if n_steps == 1:
            return

        @pl.when(i == relay_step)
        def _():
            for g in self._gathers(ins, outs, sems):
                g.relay()

        @pl.when(i == forward_step)
        def _():
            for g in self._gathers(ins, outs, sems):
                g.forward()

    def end(self, ins, outs, sems, n_steps):
        @pl.when(pl.program_id(0) == n_steps - 1)
        def _():
            gathers = self._gathers(ins, outs, sems)
            if n_steps == 1:
                for g in gathers:
                    g.relay()
                for g in gathers:
                    g.forward()
            for g in gathers:
                g.finish()


def _exchange(hosted, name):
    return _hosting_call(lambda: None, name, 1, [], [], [], [], [], hosted=hosted)[1]


class _ChipScatter:
    def __init__(self, pairsum, row0, land, send_sems, recv_sems, local_sem):
        self.pairsum, self.row0, self.land = pairsum, row0, land
        self.send_sems, self.recv_sems, self.local_sem = send_sems, recv_sems, local_sem
        x, y, c = lax.axis_index("x"), lax.axis_index("y"), lax.axis_index("c")
        self.c = c
        self.chip = 2 * x + y
        self.others = [(1 - x, y), (x, 1 - y), (1 - x, 1 - y)]

    def _src(self, chip):
        return self.pairsum.at[chip, pl.ds(self.row0, self.land.shape[1]), :]

    def _copy(self, k):
        ox, oy = self.others[k]
        return pltpu.make_async_remote_copy(
            src_ref=self._src(2 * ox + oy), dst_ref=self.land.at[self.chip],
            send_sem=self.send_sems.at[k], recv_sem=self.recv_sems.at[k], device_id=(ox, oy, self.c),
            device_id_type=MESH)

    def _arrival(self, k):
        ox, oy = self.others[k]
        return pltpu.make_async_remote_copy(
            src_ref=self._src(self.chip), dst_ref=self.land.at[2 * ox + oy],
            send_sem=self.send_sems.at[k], recv_sem=self.recv_sems.at[k], device_id=(ox, oy, self.c),
            device_id_type=MESH)

    def _local(self):
        return pltpu.make_async_copy(self._src(self.chip), self.land.at[self.chip], self.local_sem)

    def start(self):
        self._local().start()
        for k in range(N_CHIP - 1):
            self._copy(k).start()

    def finish(self):
        for k in range(N_CHIP - 1):
            self._arrival(k).wait_recv()
        for k in range(N_CHIP - 1):
            self._copy(k).wait_send()
        self._local().wait()


class _HostedChipScatter:
    def __init__(self, pairsum, row0=0, n=None):
        n = pairsum.shape[1] - row0 if n is None else n
        self.row0 = row0
        self.arrays = [pairsum]
        self.out_shapes = [jax.ShapeDtypeStruct((N_CHIP, n, pairsum.shape[2]), pairsum.dtype)]
        self.sem_shapes = [pltpu.SemaphoreType.DMA((N_CHIP - 1,)), pltpu.SemaphoreType.DMA((N_CHIP - 1,)),
                           pltpu.SemaphoreType.DMA(())]

    def begin(self, ins, outs, sems, n_steps):
        @pl.when(pl.program_id(0) == 0)
        def _():
            _ChipScatter(ins[0], self.row0, outs[0], *sems).start()

    def end(self, ins, outs, sems, n_steps):
        @pl.when(pl.program_id(0) == n_steps - 1)
        def _():
            _ChipScatter(ins[0], self.row0, outs[0], *sems).finish()


def _pair_reduce(arrays, entries, name):
    n_arr, n_ent = len(arrays), len(entries)
    cols = arrays[0].shape[1]
    offsets = []
    total = 0
    for _, _, n in entries:
        offsets.append(total)
        total += n

    def body(*refs):
        ins, out_ref = refs[:n_arr], refs[n_arr]
        rbuf, own, send_sems, recv_sems, own_sems = refs[n_arr + 1:]
        q = pl.program_id(0)
        x, y, c = lax.axis_index("x"), lax.axis_index("y"), lax.axis_index("c")

        def block(e, chip, core):
            ai, first, n = entries[e]
            return ins[ai].at[pl.ds(first + (2 * chip + core) * n, n), :]

        def to_sibling(e, chip):
            return pltpu.make_async_remote_copy(
                src_ref=block(e, chip, 1 - c), dst_ref=rbuf.at[chip, pl.ds(offsets[e], entries[e][2]), :],
                send_sem=send_sems.at[e, chip], recv_sem=recv_sems.at[e, chip], device_id=(x, y, 1 - c),
                device_id_type=MESH)

        @pl.when(q == 0)
        def _():
            for chip in range(N_CHIP):
                for e in range(n_ent):
                    to_sibling(e, chip).start()

        loads = [pltpu.make_async_copy(block(e, q, c), own.at[pl.ds(offsets[e], entries[e][2]), :], own_sems.at[e])
                 for e in range(n_ent)]
        for cp in loads:
            cp.start()
        for cp in loads:
            cp.wait()
        for e in range(n_ent):
            to_sibling(e, q).wait_recv()
        out_ref[...] = (own[...].astype(F32) + rbuf[q].astype(F32)).astype(out_ref.dtype)

        @pl.when(q == N_CHIP - 1)
        def _():
            for chip in range(N_CHIP):
                for e in range(n_ent):
                    to_sibling(e, chip).wait_send()

    return pl.pallas_call(
        body, name=name, grid=(N_CHIP,),
        in_specs=[ANY] * n_arr,
        out_specs=pl.BlockSpec((None, total, cols), lambda q: (q, 0, 0)),
        out_shape=jax.ShapeDtypeStruct((N_CHIP, total, cols), BF16),
        scratch_shapes=[pltpu.VMEM((N_CHIP, total, cols), BF16), pltpu.VMEM((total, cols), BF16),
                        pltpu.SemaphoreType.DMA((n_ent, N_CHIP)), pltpu.SemaphoreType.DMA((n_ent, N_CHIP)),
                        pltpu.SemaphoreType.DMA((n_ent,))],
        compiler_params=_params(),
    )(*arrays)


class _HostedScatterAll:
    def __init__(self, packed):
        n = packed.shape[0] // N_DEV
        self.n = n
        self.arrays = [packed]
        self.out_shapes = [jax.ShapeDtypeStruct((N_DEV, n, packed.shape[1]), packed.dtype)]
        self.sem_shapes = [pltpu.SemaphoreType.DMA((N_DEV - 1,)), pltpu.SemaphoreType.DMA((N_DEV - 1,)),
                           pltpu.SemaphoreType.DMA(())]

    def _copies(self, ins, outs, sems, with_arrivals):
        src, land = ins[0], outs[0]
        send_sems, recv_sems, local_sem = sems
        me = _my_index()

        def block(p):
            return src.at[pl.ds(p * self.n, self.n), :]

        local = pltpu.make_async_copy(block(me), land.at[me], local_sem)
        sends, arrivals = [], []
        for k in range(1, N_DEV):
            p = (me + k) % N_DEV
            q = (me + N_DEV - k) % N_DEV
            sends.append(pltpu.make_async_remote_copy(
                src_ref=block(p), dst_ref=land.at[me], send_sem=send_sems.at[k - 1], recv_sem=recv_sems.at[k - 1],
                device_id=(p // 4, (p // 2) % 2, p % 2), device_id_type=MESH))
            if with_arrivals:
                arrivals.append(pltpu.make_async_remote_copy(
                    src_ref=block(me), dst_ref=land.at[q], send_sem=send_sems.at[k - 1], recv_sem=recv_sems.at[k - 1],
                    device_id=(q // 4, (q // 2) % 2, q % 2), device_id_type=MESH))
        return local, sends, arrivals

    def begin(self, ins, outs, sems, n_steps):
        @pl.when(pl.program_id(0) == 0)
        def _():
            local, sends, _ = self._copies(ins, outs, sems, with_arrivals=False)
            local.start()
            for cp in sends:
                cp.start()

    def end(self, ins, outs, sems, n_steps):
        @pl.when(pl.program_id(0) == n_steps - 1)
        def _():
            local, sends, arrivals = self._copies(ins, outs, sems, with_arrivals=True)
            for cp in arrivals:
                cp.wait_recv()
            for cp in sends:
                cp.wait_send()
            local.wait()


def _tril_weights(ws_ref):
    r = lax.broadcasted_iota(jnp.int32, (CHUNK, CHUNK), 0)
    c = lax.broadcasted_iota(jnp.int32, (CHUNK, CHUNK), 1)
    return [jnp.where(r >= c, ws_ref[h], 0.0).astype(BF16) for h in range(HEADS)]


def _sgu_stats(zpre, gain, bias):
    e = zpre.shape[1] // 2
    z, dz = _gelu_and_grad(zpre)
    u, v = z[:, :e], z[:, e:]
    vc = v - _row_mean(v)
    rstd = lax.rsqrt(_row_mean(vc * vc) + NORM_EPS)
    vhat = vc * rstd
    return u, vhat, rstd, vhat * gain + bias, dz


def _spatial_fwd(wt, vn_bf, bfull_ref, sv_ref, tm):
    for ci in range(tm // CHUNK):
        rows = slice(ci * CHUNK, (ci + 1) * CHUNK)
        for h in range(HEADS):
            cols = slice(h * GROUP, (h + 1) * GROUP)
            sv_ref[rows, cols] = _nn(wt[h], vn_bf[rows, cols]) + bfull_ref[:, cols]


def _mixer_a_fwd(x, g, gath, gain, bias, ws, bfull, tm, hosted=()):
    t_tok, d = x.shape
    e = gain.shape[1]
    e2 = 2 * e
    n_in, n_out = e2 // N_DEV, e // N_DEV

    def body(x_ref, g_ref, gain_ref, bias_ref, ws_ref, bfull_ref, gath_ref,
             xo_ref, gd_ref, u_ref, vhat_ref, svo_ref, y_ref, rstd_ref, win_v, wout_v, sv_v, sems):
        _load_group([(gath_ref, 0, n_in, win_v), (gath_ref, n_in, n_out, wout_v)], sems)
        xv = x_ref[...]
        h = _rms_fwd(xv, g_ref[...])[0].astype(BF16)
        zpre = _nt(h, win_v[...])
        u, vhat, rstd, vn, gelu_d = _sgu_stats(zpre, gain_ref[...], bias_ref[...])
        gd_ref[...] = gelu_d.astype(BF16)
        u_ref[...] = u.astype(BF16)
        vhat_ref[...] = vhat.astype(BF16)
        rstd_ref[...] = rstd
        _spatial_fwd(_tril_weights(ws_ref), vn.astype(BF16), bfull_ref, sv_v, tm)
        sv = sv_v[...]
        svo_ref[...] = sv.astype(BF16)
        y = (u * sv).astype(BF16)
        y_ref[...] = y
        xo_ref[...] = xv + _nn(y, wout_v[...])

    return _hosting_call(
        body, "mixer_a_fwd", t_tok // tm, [x, g, gain, bias, ws, bfull, gath],
        in_specs=[_row_spec(tm, d), _const_spec((1, d)), _const_spec((1, e)), _const_spec((1, e)),
                  _const_spec((HEADS, CHUNK, CHUNK)), _const_spec((CHUNK, e)), ANY],
        out_specs=[_row_spec(tm, d), _row_spec(tm, e2), _row_spec(tm, e), _row_spec(tm, e), _row_spec(tm, e),
                   _row_spec(tm, e), _row_spec(tm, 1)],
        out_shape=[jax.ShapeDtypeStruct((t_tok, d), F32), jax.ShapeDtypeStruct((t_tok, e2), BF16),
                   jax.ShapeDtypeStruct((t_tok, e), BF16), jax.ShapeDtypeStruct((t_tok, e), BF16),
                   jax.ShapeDtypeStruct((t_tok, e), BF16), jax.ShapeDtypeStruct((t_tok, e), BF16),
                   jax.ShapeDtypeStruct((t_tok, 1), F32)],
        scratch=[pltpu.VMEM((e2, d), BF16), pltpu.VMEM((e, d), BF16), pltpu.VMEM((tm, e), F32),
                 pltpu.SemaphoreType.DMA((2 * N_DEV,))],
        hosted=hosted)


def _mixer_a_bwd(dout, x, gd, u_sav, vhat_sav, sv_sav, rstd_sav, g, gath, gain, bias, ws, tm, hosted=()):
    t_tok, d = x.shape
    e = gain.shape[1]
    e2 = 2 * e
    n_in, n_out = e2 // N_DEV, e // N_DEV
    n_steps = t_tok // tm

    def body(dout_ref, x_ref, gd_ref, u_ref, vhat_ref, sv_ref, rstd_ref, g_ref, gain_ref, bias_ref, ws_ref, gath_ref,
             dx_ref, dxb_ref, h_ref, dz_ref, dg_ref, dgain_ref, dbias_ref, dws_ref, dbso_ref,
             win_v, wout_v, dvn_v, dbs_ref, sems):
        i = pl.program_id(0)
        _load_group([(gath_ref, 0, n_in, win_v), (gath_ref, n_in, n_out, wout_v)], sems)

        @pl.when(i == 0)
        def _():
            dg_ref[...] = jnp.zeros_like(dg_ref)
            dgain_ref[...] = jnp.zeros_like(dgain_ref)
            dbias_ref[...] = jnp.zeros_like(dbias_ref)
            dws_ref[...] = jnp.zeros_like(dws_ref)
            dbs_ref[...] = jnp.zeros_like(dbs_ref)

        xv = x_ref[...]
        gv = g_ref[...]
        hv, xhat, r = _rms_fwd(xv, gv)
        h_ref[...] = hv.astype(BF16)
        gain_v = gain_ref[...]
        vhat = vhat_ref[...].astype(F32)
        vn_bf = (vhat * gain_v + bias_ref[...]).astype(BF16)
        wt = _tril_weights(ws_ref)

        dov = dout_ref[...]
        dy = _nt(dov.astype(BF16), wout_v[...])
        du = dy * sv_ref[...].astype(F32)
        dsv = dy * u_ref[...].astype(F32)
        dsv_bf = dsv.astype(BF16)
        for ci in range(tm // CHUNK):
            rows = slice(ci * CHUNK, (ci + 1) * CHUNK)
            dbs_ref[...] += dsv[rows, :]
            for h in range(HEADS):
                cols = slice(h * GROUP, (h + 1) * GROUP)
                dvn_v[rows, cols] = _tn(wt[h], dsv_bf[rows, cols])
                dws_ref[h] += _nt(dsv_bf[rows, cols], vn_bf[rows, cols])
        dvn = dvn_v[...]
        dgain_ref[...] += _col_sum(dvn * vhat)
        dbias_ref[...] += _col_sum(dvn)
        dvhat = dvn * gain_v
        dv = rstd_ref[...] * (dvhat - _row_mean(dvhat) - vhat * _row_mean(dvhat * vhat))
        dzpre = (jnp.concatenate([du, dv], axis=1) * gd_ref[...].astype(F32)).astype(BF16)
        dz_ref[...] = dzpre
        dh = _nn(dzpre, win_v[...])
        dxr, dg_row = _rms_bwd(dh, xhat, r, gv)
        dg_ref[...] += dg_row
        dx = dov + dxr
        dx_ref[...] = dx
        dxb_ref[...] = dx.astype(BF16)

        @pl.when(i == n_steps - 1)
        def _():
            rr = lax.broadcasted_iota(jnp.int32, (CHUNK, CHUNK), 0)
            cc = lax.broadcasted_iota(jnp.int32, (CHUNK, CHUNK), 1)
            for h in range(HEADS):
                dws_ref[h] = jnp.where(rr >= cc, dws_ref[h], 0.0)
                dbso_ref[h] = jnp.sum(dbs_ref[:, h * GROUP:(h + 1) * GROUP], axis=1, keepdims=True)

    return _hosting_call(
        body, "mixer_a_bwd", n_steps, [dout, x, gd, u_sav, vhat_sav, sv_sav, rstd_sav, g, gain, bias, ws, gath],
        in_specs=[_row_spec(tm, d), _row_spec(tm, d), _row_spec(tm, e2), _row_spec(tm, e), _row_spec(tm, e),
                  _row_spec(tm, e), _row_spec(tm, 1), _const_spec((1, d)),
                  _const_spec((1, e)), _const_spec((1, e)), _const_spec((HEADS, CHUNK, CHUNK)), ANY],
        out_specs=[_row_spec(tm, d), _row_spec(tm, d), _row_spec(tm, d), _row_spec(tm, e2),
                   _const_spec((1, d)), _const_spec((1, e)), _const_spec((1, e)),
                   _const_spec((HEADS, CHUNK, CHUNK)), _const_spec((HEADS, CHUNK, 1))],
        out_shape=[jax.ShapeDtypeStruct((t_tok, d), F32), jax.ShapeDtypeStruct((t_tok, d), BF16),
                   jax.ShapeDtypeStruct((t_tok, d), BF16), jax.ShapeDtypeStruct((t_tok, e2), BF16),
                   jax.ShapeDtypeStruct((1, d), F32), jax.ShapeDtypeStruct((1, e), F32),
                   jax.ShapeDtypeStruct((1, e), F32), jax.ShapeDtypeStruct((HEADS, CHUNK, CHUNK), F32),
                   jax.ShapeDtypeStruct((HEADS, CHUNK, 1), F32)],
        scratch=[pltpu.VMEM((e2, d), BF16), pltpu.VMEM((e, d), BF16), pltpu.VMEM((tm, e), F32),
                 pltpu.VMEM((CHUNK, e), F32), pltpu.SemaphoreType.DMA((2 * N_DEV,))],
        hosted=hosted)


def _ffn_fwd(x, g, srcs, nf, tm, name, hosted=()):
    t_tok, d = x.shape
    f = nf * N_DEV
    firsts = [first for _, first in srcs]

    def body(x_ref, g_ref, sg_ref, su_ref, sd_ref, xo_ref, gate_ref, up_ref, wg_v, wu_v, wd_v, sems):
        _load_group(
            [(sg_ref, firsts[0], nf, wg_v), (su_ref, firsts[1], nf, wu_v), (sd_ref, firsts[2], nf, wd_v)], sems)
        xv = x_ref[...]
        h = _rms_fwd(xv, g_ref[...])[0].astype(BF16)
        gate = _nt(h, wg_v[...])
        up = _nt(h, wu_v[...])
        gate_ref[...] = gate.astype(BF16)
        up_ref[...] = up.astype(BF16)
        act = (gate * _sigmoid(gate) * up).astype(BF16)
        xo_ref[...] = xv + _nn(act, wd_v[...])

    return _hosting_call(
        body, name, t_tok // tm, [x, g] + [arr for arr, _ in srcs],
        in_specs=[_row_spec(tm, d), _const_spec((1, d)), ANY, ANY, ANY],
        out_specs=[_row_spec(tm, d), _row_spec(tm, f), _row_spec(tm, f)],
        out_shape=[jax.ShapeDtypeStruct((t_tok, d), F32), jax.ShapeDtypeStruct((t_tok, f), BF16),
                   jax.ShapeDtypeStruct((t_tok, f), BF16)],
        scratch=[pltpu.VMEM((f, d), BF16), pltpu.VMEM((f, d), BF16), pltpu.VMEM((f, d), BF16),
                 pltpu.SemaphoreType.DMA((3 * N_DEV,))],
        hosted=hosted)


def _ffn_bwd(dout, x, gate, up, g, srcs, nf, tm, name, hosted=()):
    t_tok, d = x.shape
    f = nf * N_DEV
    firsts = [first for _, first in srcs]
    per_chunk = -(-f // (FFN_CHUNKS * MXU_WIDTH)) * MXU_WIDTH
    bounds = [min(ck * per_chunk, f) for ck in range(FFN_CHUNKS + 1)]

    def body(dout_ref, x_ref, gate_ref, up_ref, g_ref, sg_ref, su_ref, sd_ref,
             dx_ref, dxb_ref, h_ref, act_ref, dgu_ref, dg_ref, wg_v, wu_v, wd_v, sems):
        _load_group(
            [(sg_ref, firsts[0], nf, wg_v), (su_ref, firsts[1], nf, wu_v), (sd_ref, firsts[2], nf, wd_v)], sems)

        @pl.when(pl.program_id(0) == 0)
        def _():
            dg_ref[...] = jnp.zeros_like(dg_ref)

        xv = x_ref[...]
        gv = g_ref[...]
        hv, xhat, r = _rms_fwd(xv, gv)
        h_ref[...] = hv.astype(BF16)
        dov = dout_ref[...]
        dob = dov.astype(BF16)
        dh = None
        for ck in range(FFN_CHUNKS):
            cols = slice(bounds[ck], bounds[ck + 1])
            gate_v = gate_ref[:, cols].astype(F32)
            up_v = up_ref[:, cols].astype(F32)
            sig = _sigmoid(gate_v)
            silu = gate_v * sig
            act_ref[:, cols] = (silu * up_v).astype(BF16)
            dact = _nt(dob, wd_v[cols, :])
            dup = (dact * silu).astype(BF16)
            dgate = (dact * up_v * (sig * (1.0 + gate_v * (1.0 - sig)))).astype(BF16)
            dgu_ref[:, cols] = dgate
            dgu_ref[:, f + bounds[ck]:f + bounds[ck + 1]] = dup
            part = _nn(dgate, wg_v[cols, :]) + _nn(dup, wu_v[cols, :])
            dh = part if dh is None else dh + part
        dxr, dg_row = _rms_bwd(dh, xhat, r, gv)
        dg_ref[...] += dg_row
        dx = dov + dxr
        dx_ref[...] = dx
        dxb_ref[...] = dx.astype(BF16)

    return _hosting_call(
        body, name, t_tok // tm, [dout, x, gate, up, g] + [arr for arr, _ in srcs],
        in_specs=[_row_spec(tm, d), _row_spec(tm, d), _row_spec(tm, f), _row_spec(tm, f), _const_spec((1, d)),
                  ANY, ANY, ANY],
        out_specs=[_row_spec(tm, d), _row_spec(tm, d), _row_spec(tm, d), _row_spec(tm, f), _row_spec(tm, 2 * f),
                   _const_spec((1, d))],
        out_shape=[jax.ShapeDtypeStruct((t_tok, d), F32), jax.ShapeDtypeStruct((t_tok, d), BF16),
                   jax.ShapeDtypeStruct((t_tok, d), BF16), jax.ShapeDtypeStruct((t_tok, f), BF16),
                   jax.ShapeDtypeStruct((t_tok, 2 * f), BF16), jax.ShapeDtypeStruct((1, d), F32)],
        scratch=[pltpu.VMEM((f, d), BF16), pltpu.VMEM((f, d), BF16), pltpu.VMEM((f, d), BF16),
                 pltpu.SemaphoreType.DMA((3 * N_DEV,))],
        hosted=hosted)


def _shift_down(z, k, prev_rows):
    row = lax.broadcasted_iota(jnp.int32, z.shape, 0)
    out = pltpu.roll(z, k, 0)
    for j in range(k):
        out = jnp.where(row == j, prev_rows[j], out)
    return out


def _shift_up(z, k, next_rows):
    tm = z.shape[0]
    row = lax.broadcasted_iota(jnp.int32, z.shape, 0)
    out = pltpu.roll(z, tm - k, 0)
    for j in range(k):
        out = jnp.where(row == tm - k + j, next_rows[j], out)
    return out


def _mixer_b_fwd(x, g, gath, conv_w, tm, seq, hosted=()):
    t_tok, d = x.shape
    e = conv_w.shape[1]
    e3 = 3 * e
    n_in, n_out = e3 // N_DEV, e // N_DEV
    tiles_per_seq = seq // tm

    def body(x_ref, g_ref, cw_ref, gath_ref, xo_ref, p_ref, win_v, wout_v, tail_v, sems):
        i = pl.program_id(0)
        _load_group([(gath_ref, 0, n_in, win_v), (gath_ref, n_in, n_out, wout_v)], sems)

        @pl.when(i % tiles_per_seq == 0)
        def _():
            tail_v[...] = jnp.zeros_like(tail_v)

        xv = x_ref[...]
        h = _rms_fwd(xv, g_ref[...])[0].astype(BF16)
        p = _nt(h, win_v[...])
        p_ref[...] = p.astype(BF16)
        z = p[:, e:2 * e] * p[:, 2 * e:]
        prev = [tail_v[SUBLANES - 2:SUBLANES - 1, :], tail_v[SUBLANES - 1:SUBLANES, :]]
        conv = (cw_ref[2:3, :] * z + cw_ref[1:2, :] * _shift_down(z, 1, prev[1:])
                + cw_ref[0:1, :] * _shift_down(z, 2, prev))
        tail_v[...] = z[tm - SUBLANES:, :]
        y = (p[:, :e] * conv).astype(BF16)
        xo_ref[...] = xv + _nn(y, wout_v[...])

    return _hosting_call(
        body, "mixer_b_fwd", t_tok // tm, [x, g, conv_w, gath],
        in_specs=[_row_spec(tm, d), _const_spec((1, d)), _const_spec((SUBLANES, e)), ANY],
        out_specs=[_row_spec(tm, d), _row_spec(tm, e3)],
        out_shape=[jax.ShapeDtypeStruct((t_tok, d), F32), jax.ShapeDtypeStruct((t_tok, e3), BF16)],
        scratch=[pltpu.VMEM((e3, d), BF16), pltpu.VMEM((e, d), BF16), pltpu.VMEM((SUBLANES, e), F32),
                 pltpu.SemaphoreType.DMA((2 * N_DEV,))],
        hosted=hosted)


def _mixer_b_bwd(dout, x, p, g, gath, conv_w, tm, seq, hosted=()):
    t_tok, d = x.shape
    e = conv_w.shape[1]
    e3 = 3 * e
    n_in, n_out = e3 // N_DEV, e // N_DEV
    tiles_per_seq = seq // tm
    halo_per_tile = tm // HALO
    n_halo = t_tok // HALO

    def body(dout_ref, dnext_ref, x_ref, p_ref, pprev_ref, pnext_ref, g_ref, cw_ref, gath_ref,
             dx_ref, dxb_ref, h_ref, y_ref, dp_ref, dg_ref, dcw_ref, win_v, wout_v, sems):
        i = pl.program_id(0)
        _load_group([(gath_ref, 0, n_in, win_v), (gath_ref, n_in, n_out, wout_v)], sems)

        @pl.when(i == 0)
        def _():
            dg_ref[...] = jnp.zeros_like(dg_ref)
            dcw_ref[...] = jnp.zeros_like(dcw_ref)

        first = (i % tiles_per_seq == 0).astype(F32)
        last = (i % tiles_per_seq == tiles_per_seq - 1).astype(F32)
        xv = x_ref[...]
        gv = g_ref[...]
        hv, xhat, r = _rms_fwd(xv, gv)
        h_ref[...] = hv.astype(BF16)
        pv = p_ref[...].astype(F32)
        bg, cg, hx = pv[:, :e], pv[:, e:2 * e], pv[:, 2 * e:]
        z = cg * hx
        pprev = pprev_ref[...].astype(F32)
        zprev = pprev[:, e:2 * e] * pprev[:, 2 * e:] * (1.0 - first)
        prev = [zprev[HALO - 2:HALO - 1, :], zprev[HALO - 1:HALO, :]]
        zs1 = _shift_down(z, 1, prev[1:])
        zs2 = _shift_down(z, 2, prev)
        w0, w1, w2 = cw_ref[0:1, :], cw_ref[1:2, :], cw_ref[2:3, :]
        conv = w2 * z + w1 * zs1 + w0 * zs2
        y_ref[...] = (bg * conv).astype(BF16)

        dov = dout_ref[...]
        wout_bf = wout_v[...]
        dy = _nt(dov.astype(BF16), wout_bf)
        dconv = dy * bg
        dnext = _nt(dnext_ref[...].astype(BF16), wout_bf) * pnext_ref[:, :e].astype(F32) * (1.0 - last)
        nxt = [dnext[0:1, :], dnext[1:2, :]]
        dz = w2 * dconv + w1 * _shift_up(dconv, 1, nxt[:1]) + w0 * _shift_up(dconv, 2, nxt)
        dcw_ref[0:1, :] += _col_sum(dconv * zs2)
        dcw_ref[1:2, :] += _col_sum(dconv * zs1)
        dcw_ref[2:3, :] += _col_sum(dconv * z)
        dp = jnp.concatenate([dy * conv, dz * hx, dz * cg], axis=1).astype(BF16)
        dp_ref[...] = dp
        dh = _nn(dp, win_v[...])
        dxr, dg_row = _rms_bwd(dh, xhat, r, gv)
        dg_ref[...] += dg_row
        dx = dov + dxr
        dx_ref[...] = dx
        dxb_ref[...] = dx.astype(BF16)

    prev_spec = lambda w: pl.BlockSpec((HALO, w), lambda i: (jnp.maximum(i * halo_per_tile - 1, 0), 0))
    next_spec = lambda w: pl.BlockSpec((HALO, w), lambda i: (jnp.minimum((i + 1) * halo_per_tile, n_halo - 1), 0))
    return _hosting_call(
        body, "mixer_b_bwd", t_tok // tm, [dout, dout, x, p, p, p, g, conv_w, gath],
        in_specs=[_row_spec(tm, d), next_spec(d), _row_spec(tm, d), _row_spec(tm, e3), prev_spec(e3), next_spec(e3),
                  _const_spec((1, d)), _const_spec((SUBLANES, e)), ANY],
        out_specs=[_row_spec(tm, d), _row_spec(tm, d), _row_spec(tm, d), _row_spec(tm, e), _row_spec(tm, e3),
                   _const_spec((1, d)), _const_spec((SUBLANES, e))],
        out_shape=[jax.ShapeDtypeStruct((t_tok, d), F32), jax.ShapeDtypeStruct((t_tok, d), BF16),
                   jax.ShapeDtypeStruct((t_tok, d), BF16), jax.ShapeDtypeStruct((t_tok, e), BF16),
                   jax.ShapeDtypeStruct((t_tok, e3), BF16), jax.ShapeDtypeStruct((1, d), F32),
                   jax.ShapeDtypeStruct((SUBLANES, e), F32)],
        scratch=[pltpu.VMEM((e3, d), BF16), pltpu.VMEM((e, d), BF16), pltpu.SemaphoreType.DMA((2 * N_DEV,))],
        hosted=hosted)


def _loss_head(x, target, g, tm):
    t_tok, d = x.shape

    def body(x_ref, t_ref, g_ref, loss_ref, dx_ref, dxb_ref, dg_ref):
        @pl.when(pl.program_id(0) == 0)
        def _():
            loss_ref[...] = jnp.zeros_like(loss_ref)
            dg_ref[...] = jnp.zeros_like(dg_ref)

        gv = g_ref[...]
        y, xhat, r = _rms_fwd(x_ref[...], gv)
        err = y - t_ref[...]
        loss_ref[...] += 0.5 * jnp.sum(_row_mean(err * err), axis=0, keepdims=True)
        dxr, dg_row = _rms_bwd(err * (1.0 / d), xhat, r, gv)
        dg_ref[...] += dg_row
        dx_ref[...] = dxr
        dxb_ref[...] = dxr.astype(BF16)

    return pl.pallas_call(
        body, name="loss_head", grid=(t_tok // tm,),
        in_specs=[_row_spec(tm, d), _row_spec(tm, d), _const_spec((1, d))],
        out_specs=[_const_spec((1, 1)), _row_spec(tm, d), _row_spec(tm, d), _const_spec((1, d))],
        out_shape=[jax.ShapeDtypeStruct((1, 1), F32), jax.ShapeDtypeStruct((t_tok, d), F32),
                   jax.ShapeDtypeStruct((t_tok, d), BF16), jax.ShapeDtypeStruct((1, d), F32)],
        compiler_params=_params(),
    )(x, target, g)


def _wgrad(a, b, bm, name, hosted=()):
    t_tok, m = a.shape
    n = b.shape[1]

    def body(a_ref, b_ref, o_ref):
        o_ref[...] = _tn(a_ref[...], b_ref[...]).astype(o_ref.dtype)

    outs, h_outs = _hosting_call(
        body, name, m // bm, [a, b],
        in_specs=[pl.BlockSpec((t_tok, bm), lambda i: (0, i)), _const_spec((t_tok, n))],
        out_specs=[pl.BlockSpec((bm, n), lambda i: (i, 0))],
        out_shape=[jax.ShapeDtypeStruct((m, n), BF16)],
        scratch=[], hosted=hosted)
    return (outs[0], h_outs) if hosted else outs[0]


def _sum_slots(land, rb, name):
    n_slots, rows, cols = land.shape

    def body(l_ref, o_ref):
        acc = l_ref[0].astype(F32)
        for k in range(1, n_slots):
            acc = acc + l_ref[k].astype(F32)
        o_ref[...] = acc

    return pl.pallas_call(
        body, name=name, grid=(rows // rb,),
        in_specs=[pl.BlockSpec((n_slots, rb, cols), lambda i: (0, i, 0))],
        out_specs=pl.BlockSpec((rb, cols), lambda i: (i, 0)),
        out_shape=jax.ShapeDtypeStruct((rows, cols), F32),
        compiler_params=_params(sequential=False),
    )(land)


def _adamw(w, grad, m, v, rb, name):
    rows, cols = w.shape
    c1 = 1.0 / (1.0 - ADAM_B1 ** ADAM_STEP)
    c2 = 1.0 / (1.0 - ADAM_B2 ** ADAM_STEP)

    def body(w_ref, g_ref, m_ref, v_ref, d_ref, mo_ref, vo_ref):
        gv = g_ref[...]
        mn = ADAM_B1 * m_ref[...] + (1.0 - ADAM_B1) * gv
        vn = ADAM_B2 * v_ref[...] + (1.0 - ADAM_B2) * (gv * gv)
        mo_ref[...] = mn
        vo_ref[...] = vn
        d_ref[...] = -ADAM_LR * ((mn * c1) / (jnp.sqrt(vn * c2) + ADAM_EPS) + ADAM_WD * w_ref[...])

    spec = pl.BlockSpec((rb, cols), lambda i: (i, 0))
    shape = jax.ShapeDtypeStruct((rows, cols), F32)
    return pl.pallas_call(
        body, name=name, grid=(rows // rb,),
        in_specs=[spec] * 4, out_specs=[spec] * 3, out_shape=[shape] * 3,
        compiler_params=_params(sequential=False),
    )(w, grad, m, v)


def _pack_shards(groups, name, hosted=()):
    flat = [(part, layer) for group in groups for part, layer, _ in group]
    rows = [[p.shape[2] if turn else p.shape[1] for p, _, turn in group] for group in groups]
    first, _, first_turn = groups[0][0]
    width = first.shape[1] if first_turn else first.shape[2]

    def body(*refs):
        ins, outs = refs[:len(flat)], refs[len(flat):]
        k = 0
        for gi, group in enumerate(groups):
            off = 0
            for (_, _, turn), n in zip(group, rows[gi]):
                part = ins[k][...].astype(BF16)
                if turn:
                    r = lax.broadcasted_iota(jnp.int32, (n, n), 0)
                    c = lax.broadcasted_iota(jnp.int32, (n, n), 1)
                    part = _nt((r == c).astype(BF16), part).astype(BF16)
                outs[gi][off:off + n, :] = part
                off += n
                k += 1

    return _hosting_call(
        body, name, 1, [p for p, _ in flat],
        in_specs=[pl.BlockSpec((None,) + p.shape[1:], lambda i, layer=layer: (layer, 0, 0)) for p, layer in flat],
        out_specs=[_const_spec((sum(r), width)) for r in rows],
        out_shape=[jax.ShapeDtypeStruct((sum(r), width), BF16) for r in rows],
        scratch=[], hosted=hosted)


def _split_bf16(a):
    hi = a.astype(BF16)
    rest = a - hi.astype(F32)
    mid = rest.astype(BF16)
    return hi, mid, (rest - mid.astype(F32)).astype(BF16)


def _reduce_adamw(lands, w, m, v, transpose, name):
    n_layers, rows_w, cols_w = w.shape
    c1 = 1.0 / (1.0 - ADAM_B1 ** ADAM_STEP)
    c2 = 1.0 / (1.0 - ADAM_B2 ** ADAM_STEP)
    if transpose:
        n, tiles = cols_w, rows_w // MXU_WIDTH
        blk = (MXU_WIDTH, n)
        land_specs = [pl.BlockSpec((N_CHIP, n, MXU_WIDTH), lambda i, b=first // n: (0, b, i % tiles))
                      for _, first in lands]
    else:
        n, tiles = rows_w, 2
        blk = (n // tiles, cols_w)
        land_specs = [pl.BlockSpec((N_CHIP,) + blk, lambda i, b=first // blk[0]: (0, b + i % tiles, 0))
                      for _, first in lands]
    for _, first in lands:
        assert first % (n if transpose else blk[0]) == 0

    def body(*refs):
        land_refs = refs[:n_layers]
        w_ref, m_ref, v_ref, g_ref, d_ref, mo_ref, vo_ref = refs[n_layers:]
        layer = pl.program_id(0) // tiles

        def total(ref):
            acc = ref[0].astype(F32)
            for q in range(1, N_CHIP):
                acc = acc + ref[q].astype(F32)
            return acc

        gv = total(land_refs[0])
        for k in range(1, n_layers):
            gv = jnp.where(layer == k, total(land_refs[k]), gv)
        if transpose:
            r = lax.broadcasted_iota(jnp.int32, (MXU_WIDTH, MXU_WIDTH), 0)
            c = lax.broadcasted_iota(jnp.int32, (MXU_WIDTH, MXU_WIDTH), 1)
            eye = (r == c).astype(BF16)
            hi, mid, lo = _split_bf16(gv)
            gv = _nt(eye, hi) + _nt(eye, mid) + _nt(eye, lo)
        g_ref[...] = gv
        mn = ADAM_B1 * m_ref[...] + (1.0 - ADAM_B1) * gv
        vn = ADAM_B2 * v_ref[...] + (1.0 - ADAM_B2) * (gv * gv)
        mo_ref[...] = mn
        vo_ref[...] = vn
        d_ref[...] = -ADAM_LR * ((mn * c1) / (jnp.sqrt(vn * c2) + ADAM_EPS) + ADAM_WD * w_ref[...])

    spec = pl.BlockSpec((None,) + blk, lambda i: (i // tiles, i % tiles, 0))
    shape = jax.ShapeDtypeStruct(w.shape, F32)
    return pl.pallas_call(
        body, name=name, grid=(n_layers * tiles,),
        in_specs=land_specs + [spec] * 3, out_specs=[spec] * 4, out_shape=[shape] * 4,
        compiler_params=_params(sequential=False),
    )(*[land for land, _ in lands], w, m, v)


def _pack_small(parts, rows):
    flat = jnp.concatenate([p.reshape(-1).astype(F32) for p in parts])
    return jnp.pad(flat, (0, rows * LANES - flat.shape[0])).reshape(rows, LANES)


def _unpack_small(packed, shapes):
    flat = packed.reshape(-1)
    out = []
    pos = 0
    for s in shapes:
        n = math.prod(s)
        out.append(flat[pos:pos + n].reshape(s))
        pos += n
    return out


def kernel(x, mix_norm, ffn_norm, a_w_in, a_v_gain, a_v_bias, a_w_s, a_b_s, a_w_out, b_w_in, b_conv_w, b_w_out, ffn_w_gate, ffn_w_up, ffn_w_down, final_norm, loss_target, m_mix_norm, m_ffn_norm, m_a_w_in, m_a_v_gain, m_a_v_bias, m_a_w_s, m_a_b_s, m_a_w_out, m_b_w_in, m_b_conv_w, m_b_w_out, m_ffn_w_gate, m_ffn_w_up, m_ffn_w_down, m_final_norm, v_mix_norm, v_ffn_norm, v_a_w_in, v_a_v_gain, v_a_v_bias, v_a_w_s, v_a_b_s, v_a_w_out, v_b_w_in, v_b_conv_w, v_b_w_out, v_ffn_w_gate, v_ffn_w_up, v_ffn_w_down, v_final_norm):
    bsz, seq, d = x.shape
    t_tok = bsz * seq
    me = _my_index()
    xt = x.reshape(t_tok, d)
    target = loss_target.reshape(t_tok, d)
    e_a = a_v_gain.shape[1]
    e_b = b_w_out.shape[1] * N_DEV
    n_layers = ffn_w_gate.shape[0]
    f_shard = ffn_w_gate.shape[2]
    f_full = f_shard * N_DEV

    conv_pad = jnp.pad(b_conv_w[0], ((0, SUBLANES - CONV_W), (0, 0)))
    sh_a = jnp.concatenate([a_w_in[0].T, a_w_out[0]]).astype(BF16)
    bfull = jnp.repeat(a_b_s[0].T, GROUP, axis=1)

    gate_t, up_t = ffn_w_gate.transpose(0, 2, 1), ffn_w_up.transpose(0, 2, 1)
    (sh_b, sh_f0, sh_f1g, sh_f1ud), (gath_a, conv_g) = _pack_shards(
        [[(b_w_in, 0, True), (b_w_out, 0, False)],
         [(gate_t, 0, False), (up_t, 0, False), (ffn_w_down, 0, False)],
         [(gate_t, 1, False)],
         [(up_t, 1, False), (ffn_w_down, 1, False)]],
        "pack_shards", hosted=[_HostedGathers([sh_a, conv_pad], 0)])
    conv_full = jnp.pad(conv_g[:, :CONV_W, :].transpose(1, 0, 2).reshape(CONV_W, e_b), ((0, SUBLANES - CONV_W), (0, 0)))
    (x1, gd_a, u_a, vhat_a, sv_a, y_a, rstd_a), (gath_f0,) = _mixer_a_fwd(
        xt, mix_norm[0:1], gath_a, a_v_gain, a_v_bias, a_w_s[0], bfull, tm=256,
        hosted=[_HostedGathers([sh_f0], mid_lead=2)])
    srcs0 = [(gath_f0, 0), (gath_f0, f_shard), (gath_f0, 2 * f_shard)]
    (x2, gate0, up0), (gath_b, gath_f1g) = _ffn_fwd(x1, ffn_norm[0:1], srcs0, f_shard, tm=256, name="ffn_fwd0",
                                                    hosted=[_HostedGathers([sh_b, sh_f1g], mid_lead=2)])
    (x3, p_b), (gath_f1ud,) = _mixer_b_fwd(x2, mix_norm[1:2], gath_b, conv_full, tm=256, seq=seq,
                                           hosted=[_HostedGathers([sh_f1ud], mid_lead=2)])
    srcs1 = [(gath_f1g, 0), (gath_f1ud, 0), (gath_f1ud, f_shard)]
    (x4, gate1, up1), _ = _ffn_fwd(x3, ffn_norm[1:2], srcs1, f_shard, tm=256, name="ffn_fwd1")
    loss_part, dx4, dx4_bf, d_final = _loss_head(x4, target, final_norm.reshape(1, d), tm=512)

    ffn_entries = [(0, 0, f_shard), (0, f_full, f_shard), (1, 0, f_shard)]
    (dx3, dx3_bf, h_f1, act1, dgu1, d_fn1), _ = _ffn_bwd(dx4, x3, gate1, up1, ffn_norm[1:2], srcs1, f_shard, tm=256,
                                                         name="ffn_bwd1")
    g_down1 = _wgrad(act1, dx4_bf, 256, "wgrad_down1")
    g_gu1 = _wgrad(dgu1, h_f1, 512, "wgrad_gate_up1")
    ps_f1 = _pair_reduce([g_gu1, g_down1], ffn_entries, "pair_reduce_f1")
    (dx2, dx2_bf, h_b, y_b, dp_b, d_mn1, d_conv), (land_f1gu,) = _mixer_b_bwd(
        dx3, x2, p_b, mix_norm[1:2], gath_b, conv_full, tm=256, seq=seq,
        hosted=[_HostedChipScatter(ps_f1, 0, 2 * f_shard)])
    g_b_out = _wgrad(y_b, dx3_bf, 256, "wgrad_b_out")
    g_b_in = _wgrad(dp_b, h_b, 512, "wgrad_b_in")
    ps_b = _pair_reduce([g_b_in, g_b_out], [(0, 0, b_w_in.shape[2]), (1, 0, b_w_out.shape[1])], "pair_reduce_b")
    (dx1, dx1_bf, h_f0, act0, dgu0, d_fn0), (land_f1d, land_b) = _ffn_bwd(
        dx2, x1, gate0, up0, ffn_norm[0:1], srcs0, f_shard, tm=256, name="ffn_bwd0",
        hosted=[_HostedChipScatter(ps_f1, 2 * f_shard, f_shard), _HostedChipScatter(ps_b)])
    g_down0 = _wgrad(act0, dx2_bf, 256, "wgrad_down0")
    g_gu0 = _wgrad(dgu0, h_f0, 512, "wgrad_gate_up0")
    ps_f0 = _pair_reduce([g_gu0, g_down0], ffn_entries, "pair_reduce_f0")
    g_a_out = _wgrad(y_a, dx1_bf, 256, "wgrad_a_out")
    ps_ao = _pair_reduce([g_a_out], [(0, 0, a_w_out.shape[1])], "pair_reduce_a_out")
    (dx0, _, h_a, dz_a, d_mn0, d_gain, d_bias, d_ws, d_bs_acc), (land_f0, land_ao) = _mixer_a_bwd(
        dx1, xt, gd_a, u_a, vhat_a, sv_a, rstd_a, mix_norm[0:1], gath_a, a_v_gain, a_v_bias, a_w_s[0], tm=256,
        hosted=[_HostedChipScatter(ps_f0), _HostedChipScatter(ps_ao)])
    d_bs = d_bs_acc.reshape(HEADS, CHUNK)

    small_grads = [jnp.concatenate([d_mn0, d_mn1]), jnp.concatenate([d_fn0, d_fn1]), d_gain, d_bias, d_ws, d_bs,
                   d_final, d_conv[:CONV_W], loss_part]
    small_shapes = [(n_layers, d), (n_layers, d), (1, e_a), (1, e_a), (1, HEADS, CHUNK, CHUNK), (1, HEADS, CHUNK), (d,),
                    (CONV_W, e_b), ()]
    n_small = sum(math.prod(s) for s in small_shapes)
    blk_rows = -(-n_small // (N_DEV * LANES * SUBLANES)) * SUBLANES
    small_rows = blk_rows * N_DEV
    packed = _pack_small(small_grads, small_rows)
    g_a_in, (small_land,) = _wgrad(dz_a, h_a, 512, "wgrad_a_in", hosted=[_HostedScatterAll(packed)])
    ps_ai = _pair_reduce([g_a_in], [(0, 0, a_w_in.shape[2])], "pair_reduce_a_in")
    small_sum = _sum_slots(small_land, blk_rows, "sum_small")
    land_ai, small_gath = _exchange([_HostedChipScatter(ps_ai), _HostedGathers([small_sum], 0)], "tail_exchange")
    small_all = small_gath.reshape(small_rows, LANES)

    (gr_mix, gr_ffn, gr_gain, gr_bias, gr_ws, gr_bs, gr_final, gr_conv_full, loss) = _unpack_small(small_all, small_shapes)
    gr_conv = lax.dynamic_slice_in_dim(gr_conv_full, me * (e_b // N_DEV), e_b // N_DEV, axis=1)[None]

    small_w = [mix_norm, ffn_norm, a_v_gain, a_v_bias, a_w_s, a_b_s, final_norm]
    small_m = [m_mix_norm, m_ffn_norm, m_a_v_gain, m_a_v_bias, m_a_w_s, m_a_b_s, m_final_norm]
    small_v = [v_mix_norm, v_ffn_norm, v_a_v_gain, v_a_v_bias, v_a_w_s, v_a_b_s, v_final_norm]
    small_g = [gr_mix, gr_ffn, gr_gain, gr_bias, gr_ws, gr_bs, gr_final]
    sm_shapes = small_shapes[:len(small_w)]
    sm_out = _adamw(_pack_small(small_w, small_rows), _pack_small(small_g, small_rows), _pack_small(small_m, small_rows),
                    _pack_small(small_v, small_rows), small_rows, "adamw_small")
    sm_delta, sm_m, sm_v = [_unpack_small(o, sm_shapes) for o in sm_out]

    conv_out = _adamw(b_conv_w[0], gr_conv[0], m_b_conv_w[0], v_b_conv_w[0], CONV_W, "adamw_conv")
    conv_delta, conv_m, conv_v = [o[None] for o in conv_out]

    n_b_in = b_w_in.shape[2]
    res = {
        "a_w_in": _reduce_adamw([(land_ai, 0)], a_w_in, m_a_w_in, v_a_w_in, True, "adamw_a_in"),
        "a_w_out": _reduce_adamw([(land_ao, 0)], a_w_out, m_a_w_out, v_a_w_out, False, "adamw_a_out"),
        "b_w_in": _reduce_adamw([(land_b, 0)], b_w_in, m_b_w_in, v_b_w_in, True, "adamw_b_in"),
        "b_w_out": _reduce_adamw([(land_b, n_b_in)], b_w_out, m_b_w_out, v_b_w_out, False, "adamw_b_out"),
        "ffn_w_gate": [o.transpose(0, 2, 1) for o in _reduce_adamw(
            [(land_f0, 0), (land_f1gu, 0)], gate_t, m_ffn_w_gate.transpose(0, 2, 1), v_ffn_w_gate.transpose(0, 2, 1),
            False, "adamw_gate")],
        "ffn_w_up": [o.transpose(0, 2, 1) for o in _reduce_adamw(
            [(land_f0, f_shard), (land_f1gu, f_shard)], up_t, m_ffn_w_up.transpose(0, 2, 1),
            v_ffn_w_up.transpose(0, 2, 1), False, "adamw_up")],
        "ffn_w_down": _reduce_adamw([(land_f0, 2 * f_shard), (land_f1d, 0)], ffn_w_down, m_ffn_w_down, v_ffn_w_down,
                                    False, "adamw_down"),
    }

    order = ["mix_norm", "ffn_norm", "a_w_in", "a_v_gain", "a_v_bias", "a_w_s", "a_b_s", "a_w_out", "b_w_in",
             "b_conv_w", "b_w_out", "ffn_w_gate", "ffn_w_up", "ffn_w_down", "final_norm"]
    small_names = ["mix_norm", "ffn_norm", "a_v_gain", "a_v_bias", "a_w_s", "a_b_s", "final_norm"]
    grads = {"b_conv_w": gr_conv}
    deltas, new_m, new_v = {}, {}, {}
    for k, name in enumerate(small_names):
        grads[name] = small_g[k]
        deltas[name], new_m[name], new_v[name] = sm_delta[k], sm_m[k], sm_v[k]
    deltas["b_conv_w"], new_m["b_conv_w"], new_v["b_conv_w"] = conv_delta, conv_m, conv_v
    for name, (gg, dl, mm, vv) in res.items():
        grads[name], deltas[name], new_m[name], new_v[name] = gg, dl, mm, vv

    grad_x = dx0.reshape(bsz, seq, d)
    return (loss, grad_x, *[grads[n] for n in order], *[deltas[n] for n in order],
            *[new_m[n] for n in order], *[new_v[n] for n in order])
```

```python
import math

import jax
import jax.numpy as jnp
from jax import lax
from jax.experimental import pallas as pl
from jax.experimental.pallas import tpu as pltpu

F32 = jnp.float32
BF16 = jnp.bfloat16

N_DEV = 8
N_CHIP = 4
CHUNK = 128
HEADS = 16
GROUP = 128
CONV_W = 3
NORM_EPS = 1e-6
GELU_C = math.sqrt(2.0 / math.pi)
GELU_K = 0.044715

ADAM_LR = 0.001
ADAM_B1 = 0.9
ADAM_B2 = 0.999
ADAM_EPS = 1e-08
ADAM_WD = 0.01
ADAM_STEP = 10

LANES = 128
SUBLANES = 8
VMEM_LIMIT = 60 * 1024 * 1024
HALO = 16
MXU_WIDTH = 256
FFN_CHUNKS = 2

MESH = pl.DeviceIdType.MESH
ANY = pl.BlockSpec(memory_space=pl.ANY)


def _params(sequential=True):
    return pltpu.CompilerParams(
        dimension_semantics=("arbitrary",) if sequential else None,
        vmem_limit_bytes=VMEM_LIMIT)


def _nn(a, b):
    return jnp.dot(a, b, preferred_element_type=F32)


def _nt(a, b):
    return lax.dot_general(a, b, (((1,), (1,)), ((), ())), preferred_element_type=F32)


def _tn(a, b):
    return lax.dot_general(a, b, (((0,), (0,)), ((), ())), preferred_element_type=F32)


def _row_mean(a):
    return jnp.mean(a, axis=-1, keepdims=True)


def _col_sum(a):
    return jnp.sum(a, axis=0, keepdims=True)


def _rms_fwd(x, g):
    r = lax.rsqrt(_row_mean(x * x) + NORM_EPS)
    xhat = x * r
    return xhat * g, xhat, r


def _rms_bwd(dh, xhat, r, g):
    a = dh * g
    dx = r * (a - xhat * _row_mean(a * xhat))
    return dx, _col_sum(dh * xhat)


def _gelu_and_grad(x):
    x2 = x * x
    t = jnp.tanh(x * (GELU_C + (GELU_C * GELU_K) * x2))
    half = 0.5 * t + 0.5
    d = half + x * (0.5 - 0.5 * (t * t)) * (GELU_C + (3.0 * GELU_C * GELU_K) * x2)
    return x * half, d


def _sigmoid(x):
    return 1.0 / (1.0 + jnp.exp(-x))


def _row_spec(tm, width):
    return pl.BlockSpec((tm, width), lambda i: (i, 0))


def _const_spec(shape):
    nd = len(shape)
    return pl.BlockSpec(shape, lambda i: (0,) * nd)


def _load_group(parts, sems):
    @pl.when(pl.program_id(0) == 0)
    def _():
        copies = []
        for k, (gath_ref, first, n, dst) in enumerate(parts):
            for j in range(N_DEV):
                copies.append(pltpu.make_async_copy(gath_ref.at[j, pl.ds(first, n), :], dst.at[pl.ds(j * n, n), :],
                                                    sems.at[k * N_DEV + j]))
        for cp in copies:
            cp.start()
        for cp in copies:
            cp.wait()


def _hosting_call(body, name, n_steps, arrays, in_specs, out_specs, out_shape, scratch, hosted=()):
    n_in, n_out, n_scr = len(arrays), len(out_shape), len(scratch)
    h_arrays = [a for h in hosted for a in h.arrays]
    h_shapes = [s for h in hosted for s in h.out_shapes]
    h_sems = [s for h in hosted for s in h.sem_shapes]

    def full_body(*refs):
        pos = 0
        groups = []
        for n in (n_in, len(h_arrays), n_out, len(h_shapes), n_scr, len(h_sems)):
            groups.append(refs[pos:pos + n])
            pos += n
        own_in, h_in, own_out, h_out, own_scr, h_sem = groups
        per_host = []
        pi = po = ps = 0
        for h in hosted:
            ni, no, ns = len(h.arrays), len(h.out_shapes), len(h.sem_shapes)
            per_host.append((h, h_in[pi:pi + ni], h_out[po:po + no], h_sem[ps:ps + ns]))
            pi, po, ps = pi + ni, po + no, ps + ns
        for h, ins, outs, sems in per_host:
            h.begin(ins, outs, sems, n_steps)
        body(*own_in, *own_out, *own_scr)
        for h, ins, outs, sems in per_host:
            h.end(ins, outs, sems, n_steps)

    outs = pl.pallas_call(
        full_body, name=name, grid=(n_steps,),
        in_specs=list(in_specs) + [ANY] * len(h_arrays),
        out_specs=list(out_specs) + [ANY] * len(h_shapes),
        out_shape=list(out_shape) + h_shapes,
        scratch_shapes=list(scratch) + h_sems,
        compiler_params=_params(),
    )(*arrays, *h_arrays)
    return outs[:n_out], outs[n_out:]


def _my_index():
    return 4 * lax.axis_index("x") + 2 * lax.axis_index("y") + lax.axis_index("c")


GATHER_COPIES = 8


class _Gather:
    def __init__(self, shard, out, send_sems, recv_sems, local_sem):
        self.shard, self.out = shard, out
        self.send_sems, self.recv_sems, self.local_sem = send_sems, recv_sems, local_sem
        x, y, c = lax.axis_index("x"), lax.axis_index("y"), lax.axis_index("c")
        self.c = c
        self.me, self.sibling = (x, y, c), (x, y, 1 - c)
        self.xn, self.yn, self.dg = (1 - x, y), (x, 1 - y), (1 - x, 1 - y)
        self.n = shard.shape[0]
        self.half = self.n // 2
        rows_per_tile = SUBLANES * 4 // shard.dtype.itemsize
        self.relays = self.n % 2 == 0 and self.half % rows_per_tile == 0

    def _slot(self, dev, lo=0, hi=None):
        hi = self.n if hi is None else hi
        return self.out.at[4 * dev[0] + 2 * dev[1] + dev[2], pl.ds(lo, hi - lo), :]

    def _copy(self, k, block, to, src=None, lo=0, hi=None):
        return pltpu.make_async_remote_copy(
            src_ref=self._slot(block, lo, hi) if src is None else src, dst_ref=self._slot(block, lo, hi),
            send_sem=self.send_sems.at[k], recv_sem=self.recv_sems.at[k], device_id=to, device_id_type=MESH)

    def _local(self):
        return pltpu.make_async_copy(self.shard, self._slot(self.me), self.local_sem)

    def start(self):
        c = self.c
        self._local().start()
        self._copy(0, self.me, self.sibling, src=self.shard).start()
        self._copy(1, self.me, (*self.xn, c), src=self.shard).start()
        self._copy(2, self.me, (*self.yn, c), src=self.shard).start()
        if not self.relays:
            self._copy(3, self.me, (*self.dg, c), src=self.shard).start()

    def relay(self):
        c = self.c
        if self.relays:
            self._copy(1, (*self.xn, c), self.me).wait_recv()
            self._copy(3, (*self.xn, c), (*self.yn, c), hi=self.half).start()
            self._copy(2, (*self.yn, c), self.me).wait_recv()
            self._copy(4, (*self.yn, c), (*self.xn, c), lo=self.half).start()

    def forward(self):
        c = self.c
        if self.relays:
            self._copy(5, (*self.xn, c), self.sibling).start()
            self._copy(6, (*self.yn, c), self.sibling).start()
            self._copy(3, (*self.dg, c), self.me, hi=self.half).wait_recv()
            self._copy(4, (*self.dg, c), self.me, lo=self.half).wait_recv()
        else:
            self._copy(1, (*self.xn, c), self.me).wait_recv()
            self._copy(5, (*self.xn, c), self.sibling).start()
            self._copy(2, (*self.yn, c), self.me).wait_recv()
            self._copy(6, (*self.yn, c), self.sibling).start()
            self._copy(3, (*self.dg, c), self.me).wait_recv()
        self._copy(7, (*self.dg, c), self.sibling).start()

    def finish(self):
        c = self.c
        self._copy(0, self.sibling, self.me).wait_recv()
        for k, chip in ((5, self.xn), (6, self.yn), (7, self.dg)):
            self._copy(k, (*chip, 1 - c), self.me).wait_recv()
        for k in (0, 1, 2, 5, 6, 7):
            self._copy(k, self.me, self.sibling).wait_send()
        if self.relays:
            self._copy(3, self.me, self.sibling, hi=self.half).wait_send()
            self._copy(4, self.me, self.sibling, lo=self.half).wait_send()
        else:
            self._copy(3, self.me, self.sibling).wait_send()
        self._local().wait()


class _HostedGathers:
    def __init__(self, shards, mid_lead, relay_at=0.56):
        n = len(shards)
        self.arrays = shards
        self.mid_lead, self.relay_at = mid_lead, relay_at
        self.out_shapes = [jax.ShapeDtypeStruct((N_DEV,) + s.shape, s.dtype) for s in shards]
        self.sem_shapes = [pltpu.SemaphoreType.DMA((n, GATHER_COPIES)), pltpu.SemaphoreType.DMA((n, GATHER_COPIES)),
                           pltpu.SemaphoreType.DMA((n,))]

    def _gathers(self, ins, outs, sems):
        return [_Gather(ins[a], outs[a], sems[0].at[a], sems[1].at[a], sems[2].at[a]) for a in range(len(ins))]

    def begin(self, ins, outs, sems, n_steps):
        i = pl.program_id(0)
        forward_step = max(n_steps - 1 - self.mid_lead, 0)
        relay_step = min(int(self.relay_at * n_steps), forward_step)

        @pl.when(i == 0)
        def _():
            for g in self._gathers(ins, outs, sems):
                g.start()

        if n_steps == 1:
            return

        @pl.when(i == relay_step)
        def _():
            for g in self._gathers(ins, outs, sems):
                g.relay()

        @pl.when(i == forward_step)
        def _():
            for g in self._gathers(ins, outs, sems):
                g.forward()

    def end(self, ins, outs, sems, n_steps):
        @pl.when(pl.program_id(0) == n_steps - 1)
        def _():
            gathers = self._gathers(ins, outs, sems)
            if n_steps == 1:
                for g in gathers:
                    g.relay()
                for g in gathers:
                    g.forward()
            for g in gathers:
                g.finish()


def _exchange(hosted, name):
    return _hosting_call(lambda: None, name, 1, [], [], [], [], [], hosted=hosted)[1]


class _ChipScatter:
    def __init__(self, pairsum, row0, land, send_sems, recv_sems, local_sem):
        self.pairsum, self.row0, self.land = pairsum, row0, land
        self.send_sems, self.recv_sems, self.local_sem = send_sems, recv_sems, local_sem
        x, y, c = lax.axis_index("x"), lax.axis_index("y"), lax.axis_index("c")
        self.c = c
        self.chip = 2 * x + y
        self.others = [(1 - x, y), (x, 1 - y), (1 - x, 1 - y)]

    def _src(self, chip):
        return self.pairsum.at[chip, pl.ds(self.row0, self.land.shape[1]), :]

    def _copy(self, k):
        ox, oy = self.others[k]
        return pltpu.make_async_remote_copy(
            src_ref=self._src(2 * ox + oy), dst_ref=self.land.at[self.chip],
            send_sem=self.send_sems.at[k], recv_sem=self.recv_sems.at[k], device_id=(ox, oy, self.c),
            device_id_type=MESH)

    def _arrival(self, k):
        ox, oy = self.others[k]
        return pltpu.make_async_remote_copy(
            src_ref=self._src(self.chip), dst_ref=self.land.at[2 * ox + oy],
            send_sem=self.send_sems.at[k], recv_sem=self.recv_sems.at[k], device_id=(ox, oy, self.c),
            device_id_type=MESH)

    def _local(self):
        return pltpu.make_async_copy(self._src(self.chip), self.land.at[self.chip], self.local_sem)

    def start(self):
        self._local().start()
        for k in range(N_CHIP - 1):
            self._copy(k).start()

    def finish(self):
        for k in range(N_CHIP - 1):
            self._arrival(k).wait_recv()
        for k in range(N_CHIP - 1):
            self._copy(k).wait_send()
        self._local().wait()


class _HostedChipScatter:
    def __init__(self, pairsum, row0=0, n=None):
        n = pairsum.shape[1] - row0 if n is None else n
        self.row0 = row0
        self.arrays = [pairsum]
        self.out_shapes = [jax.ShapeDtypeStruct((N_CHIP, n, pairsum.shape[2]), pairsum.dtype)]
        self.sem_shapes = [pltpu.SemaphoreType.DMA((N_CHIP - 1,)), pltpu.SemaphoreType.DMA((N_CHIP - 1,)),
                           pltpu.SemaphoreType.DMA(())]

    def begin(self, ins, outs, sems, n_steps):
        @pl.when(pl.program_id(0) == 0)
        def _():
            _ChipScatter(ins[0], self.row0, outs[0], *sems).start()

    def end(self, ins, outs, sems, n_steps):
        @pl.when(pl.program_id(0) == n_steps - 1)
        def _():
            _ChipScatter(ins[0], self.row0, outs[0], *sems).finish()


def _pair_reduce(arrays, entries, name):
    n_arr, n_ent = len(arrays), len(entries)
    cols = arrays[0].shape[1]
    offsets = []
    total = 0
    for _, _, n in entries:
        offsets.append(total)
        total += n

    def body(*refs):
        ins, out_ref = refs[:n_arr], refs[n_arr]
        rbuf, own, send_sems, recv_sems, own_sems = refs[n_arr + 1:]
        q = pl.program_id(0)
        x, y, c = lax.axis_index("x"), lax.axis_index("y"), lax.axis_index("c")

        def block(e, chip, core):
            ai, first, n = entries[e]
            return ins[ai].at[pl.ds(first + (2 * chip + core) * n, n), :]

        def to_sibling(e, chip):
            return pltpu.make_async_remote_copy(
                src_ref=block(e, chip, 1 - c), dst_ref=rbuf.at[chip, pl.ds(offsets[e], entries[e][2]), :],
                send_sem=send_sems.at[e, chip], recv_sem=recv_sems.at[e, chip], device_id=(x, y, 1 - c),
                device_id_type=MESH)

        @pl.when(q == 0)
        def _():
            for chip in range(N_CHIP):
                for e in range(n_ent):
                    to_sibling(e, chip).start()

        loads = [pltpu.make_async_copy(block(e, q, c), own.at[pl.ds(offsets[e], entries[e][2]), :], own_sems.at[e])
                 for e in range(n_ent)]
        for cp in loads:
            cp.start()
        for cp in loads:
            cp.wait()
        for e in range(n_ent):
            to_sibling(e, q).wait_recv()
        out_ref[...] = (own[...].astype(F32) + rbuf[q].astype(F32)).astype(out_ref.dtype)

        @pl.when(q == N_CHIP - 1)
        def _():
            for chip in range(N_CHIP):
                for e in range(n_ent):
                    to_sibling(e, chip).wait_send()

    return pl.pallas_call(
        body, name=name, grid=(N_CHIP,),
        in_specs=[ANY] * n_arr,
        out_specs=pl.BlockSpec((None, total, cols), lambda q: (q, 0, 0)),
        out_shape=jax.ShapeDtypeStruct((N_CHIP, total, cols), BF16),
        scratch_shapes=[pltpu.VMEM((N_CHIP, total, cols), BF16), pltpu.VMEM((total, cols), BF16),
                        pltpu.SemaphoreType.DMA((n_ent, N_CHIP)), pltpu.SemaphoreType.DMA((n_ent, N_CHIP)),
                        pltpu.SemaphoreType.DMA((n_ent,))],
        compiler_params=_params(),
    )(*arrays)


class _HostedScatterAll:
    def __init__(self, packed):
        n = packed.shape[0] // N_DEV
        self.n = n
        self.arrays = [packed]
        self.out_shapes = [jax.ShapeDtypeStruct((N_DEV, n, packed.shape[1]), packed.dtype)]
        self.sem_shapes = [pltpu.SemaphoreType.DMA((N_DEV - 1,)), pltpu.SemaphoreType.DMA((N_DEV - 1,)),
                           pltpu.SemaphoreType.DMA(())]

    def _copies(self, ins, outs, sems, with_arrivals):
        src, land = ins[0], outs[0]
        send_sems, recv_sems, local_sem = sems
        me = _my_index()

        def block(p):
            return src.at[pl.ds(p * self.n, self.n), :]

        local = pltpu.make_async_copy(block(me), land.at[me], local_sem)
        sends, arrivals = [], []
        for k in range(1, N_DEV):
            p = (me + k) % N_DEV
            q = (me + N_DEV - k) % N_DEV
            sends.append(pltpu.make_async_remote_copy(
                src_ref=block(p), dst_ref=land.at[me], send_sem=send_sems.at[k - 1], recv_sem=recv_sems.at[k - 1],
                device_id=(p // 4, (p // 2) % 2, p % 2), device_id_type=MESH))
            if with_arrivals:
                arrivals.append(pltpu.make_async_remote_copy(
                    src_ref=block(me), dst_ref=land.at[q], send_sem=send_sems.at[k - 1], recv_sem=recv_sems.at[k - 1],
                    device_id=(q // 4, (q // 2) % 2, q % 2), device_id_type=MESH))
        return local, sends, arrivals

    def begin(self, ins, outs, sems, n_steps):
        @pl.when(pl.program_id(0) == 0)
        def _():
            local, sends, _ = self._copies(ins, outs, sems, with_arrivals=False)
            local.start()
            for cp in sends:
                cp.start()

    def end(self, ins, outs, sems, n_steps):
        @pl.when(pl.program_id(0) == n_steps - 1)
        def _():
            local, sends, arrivals = self._copies(ins, outs, sems, with_arrivals=True)
            for cp in arrivals:
                cp.wait_recv()
            for cp in sends:
                cp.wait_send()
            local.wait()


def _tril_weights(ws_ref):
    r = lax.broadcasted_iota(jnp.int32, (CHUNK, CHUNK), 0)
    c = lax.broadcasted_iota(jnp.int32, (CHUNK, CHUNK), 1)
    return [jnp.where(r >= c, ws_ref[h], 0.0).astype(BF16) for h in range(HEADS)]


def _sgu_stats(zpre, gain, bias):
    e = zpre.shape[1] // 2
    z, dz = _gelu_and_grad(zpre)
    u, v = z[:, :e], z[:, e:]
    vc = v - _row_mean(v)
    rstd = lax.rsqrt(_row_mean(vc * vc) + NORM_EPS)
    vhat = vc * rstd
    return u, vhat, rstd, vhat * gain + bias, dz


def _spatial_fwd(wt, vn_bf, bfull_ref, sv_ref, tm):
    for ci in range(tm // CHUNK):
        rows = slice(ci * CHUNK, (ci + 1) * CHUNK)
        for h in range(HEADS):
            cols = slice(h * GROUP, (h + 1) * GROUP)
            sv_ref[rows, cols] = _nn(wt[h], vn_bf[rows, cols]) + bfull_ref[:, cols]


def _mixer_a_fwd(x, g, gath, gain, bias, ws, bfull, tm, hosted=()):
    t_tok, d = x.shape
    e = gain.shape[1]
    e2 = 2 * e
    n_in, n_out = e2 // N_DEV, e // N_DEV

    def body(x_ref, g_ref, gain_ref, bias_ref, ws_ref, bfull_ref, gath_ref,
             xo_ref, gd_ref, u_ref, vhat_ref, svo_ref, y_ref, rstd_ref, win_v, wout_v, sv_v, sems):
        _load_group([(gath_ref, 0, n_in, win_v), (gath_ref, n_in, n_out, wout_v)], sems)
        xv = x_ref[...]
        h = _rms_fwd(xv, g_ref[...])[0].astype(BF16)
        zpre = _nt(h, win_v[...])
        u, vhat, rstd, vn, gelu_d = _sgu_stats(zpre, gain_ref[...], bias_ref[...])
        gd_ref[...] = gelu_d.astype(BF16)
        u_ref[...] = u.astype(BF16)
        vhat_ref[...] = vhat.astype(BF16)
        rstd_ref[...] = rstd
        _spatial_fwd(_tril_weights(ws_ref), vn.astype(BF16), bfull_ref, sv_v, tm)
        sv = sv_v[...]
        svo_ref[...] = sv.astype(BF16)
        y = (u * sv).astype(BF16)
        y_ref[...] = y
        xo_ref[...] = xv + _nn(y, wout_v[...])

    return _hosting_call(
        body, "mixer_a_fwd", t_tok // tm, [x, g, gain, bias, ws, bfull, gath],
        in_specs=[_row_spec(tm, d), _const_spec((1, d)), _const_spec((1, e)), _const_spec((1, e)),
                  _const_spec((HEADS, CHUNK, CHUNK)), _const_spec((CHUNK, e)), ANY],
        out_specs=[_row_spec(tm, d), _row_spec(tm, e2), _row_spec(tm, e), _row_spec(tm, e), _row_spec(tm, e),
                   _row_spec(tm, e), _row_spec(tm, 1)],
        out_shape=[jax.ShapeDtypeStruct((t_tok, d), F32), jax.ShapeDtypeStruct((t_tok, e2), BF16),
                   jax.ShapeDtypeStruct((t_tok, e), BF16), jax.ShapeDtypeStruct((t_tok, e), BF16),
                   jax.ShapeDtypeStruct((t_tok, e), BF16), jax.ShapeDtypeStruct((t_tok, e), BF16),
                   jax.ShapeDtypeStruct((t_tok, 1), F32)],
        scratch=[pltpu.VMEM((e2, d), BF16), pltpu.VMEM((e, d), BF16), pltpu.VMEM((tm, e), F32),
                 pltpu.SemaphoreType.DMA((2 * N_DEV,))],
        hosted=hosted)


def _mixer_a_bwd(dout, x, gd, u_sav, vhat_sav, sv_sav, rstd_sav, g, gath, gain, bias, ws, tm, hosted=()):
    t_tok, d = x.shape
    e = gain.shape[1]
    e2 = 2 * e
    n_in, n_out = e2 // N_DEV, e // N_DEV
    n_steps = t_tok // tm

    def body(dout_ref, x_ref, gd_ref, u_ref, vhat_ref, sv_ref, rstd_ref, g_ref, gain_ref, bias_ref, ws_ref, gath_ref,
             dx_ref, dxb_ref, h_ref, dz_ref, dg_ref, dgain_ref, dbias_ref, dws_ref, dbso_ref,
             win_v, wout_v, dvn_v, dbs_ref, sems):
        i = pl.program_id(0)
        _load_group([(gath_ref, 0, n_in, win_v), (gath_ref, n_in, n_out, wout_v)], sems)

        @pl.when(i == 0)
        def _():
            dg_ref[...] = jnp.zeros_like(dg_ref)
            dgain_ref[...] = jnp.zeros_like(dgain_ref)
            dbias_ref[...] = jnp.zeros_like(dbias_ref)
            dws_ref[...] = jnp.zeros_like(dws_ref)
            dbs_ref[...] = jnp.zeros_like(dbs_ref)

        xv = x_ref[...]
        gv = g_ref[...]
        hv, xhat, r = _rms_fwd(xv, gv)
        h_ref[...] = hv.astype(BF16)
        gain_v = gain_ref[...]
        vhat = vhat_ref[...].astype(F32)
        vn_bf = (vhat * gain_v + bias_ref[...]).astype(BF16)
        wt = _tril_weights(ws_ref)

        dov = dout_ref[...]
        dy = _nt(dov.astype(BF16), wout_v[...])
        du = dy * sv_ref[...].astype(F32)
        dsv = dy * u_ref[...].astype(F32)
        dsv_bf = dsv.astype(BF16)
        for ci in range(tm // CHUNK):
            rows = slice(ci * CHUNK, (ci + 1) * CHUNK)
            dbs_ref[...] += dsv[rows, :]
            for h in range(HEADS):
                cols = slice(h * GROUP, (h + 1) * GROUP)
                dvn_v[rows, cols] = _tn(wt[h], dsv_bf[rows, cols])
                dws_ref[h] += _nt(dsv_bf[rows, cols], vn_bf[rows, cols])
        dvn = dvn_v[...]
        dgain_ref[...] += _col_sum(dvn * vhat)
        dbias_ref[...] += _col_sum(dvn)
        dvhat = dvn * gain_v
        dv = rstd_ref[...] * (dvhat - _row_mean(dvhat) - vhat * _row_mean(dvhat * vhat))
        dzpre = (jnp.concatenate([du, dv], axis=1) * gd_ref[...].astype(F32)).astype(BF16)
        dz_ref[...] = dzpre
        dh = _nn(dzpre, win_v[...])
        dxr, dg_row = _rms_bwd(dh, xhat, r, gv)
        dg_ref[...] += dg_row
        dx = dov + dxr
        dx_ref[...] = dx
        dxb_ref[...] = dx.astype(BF16)

        @pl.when(i == n_steps - 1)
        def _():
            rr = lax.broadcasted_iota(jnp.int32, (CHUNK, CHUNK), 0)
            cc = lax.broadcasted_iota(jnp.int32, (CHUNK, CHUNK), 1)
            for h in range(HEADS):
                dws_ref[h] = jnp.where(rr >= cc, dws_ref[h], 0.0)
                dbso_ref[h] = jnp.sum(dbs_ref[:, h * GROUP:(h + 1) * GROUP], axis=1, keepdims=True)

    return _hosting_call(
        body, "mixer_a_bwd", n_steps, [dout, x, gd, u_sav, vhat_sav, sv_sav, rstd_sav, g, gain, bias, ws, gath],
        in_specs=[_row_spec(tm, d), _row_spec(tm, d), _row_spec(tm, e2), _row_spec(tm, e), _row_spec(tm, e),
                  _row_spec(tm, e), _row_spec(tm, 1), _const_spec((1, d)),
                  _const_spec((1, e)), _const_spec((1, e)), _const_spec((HEADS, CHUNK, CHUNK)), ANY],
        out_specs=[_row_spec(tm, d), _row_spec(tm, d), _row_spec(tm, d), _row_spec(tm, e2),
                   _const_spec((1, d)), _const_spec((1, e)), _const_spec((1, e)),
                   _const_spec((HEADS, CHUNK, CHUNK)), _const_spec((HEADS, CHUNK, 1))],
        out_shape=[jax.ShapeDtypeStruct((t_tok, d), F32), jax.ShapeDtypeStruct((t_tok, d), BF16),
                   jax.ShapeDtypeStruct((t_tok, d), BF16), jax.ShapeDtypeStruct((t_tok, e2), BF16),
                   jax.ShapeDtypeStruct((1, d), F32), jax.ShapeDtypeStruct((1, e), F32),
                   jax.ShapeDtypeStruct((1, e), F32), jax.ShapeDtypeStruct((HEADS, CHUNK, CHUNK), F32),
                   jax.ShapeDtypeStruct((HEADS, CHUNK, 1), F32)],
        scratch=[pltpu.VMEM((e2, d), BF16), pltpu.VMEM((e, d), BF16), pltpu.VMEM((tm, e), F32),
                 pltpu.VMEM((CHUNK, e), F32), pltpu.SemaphoreType.DMA((2 * N_DEV,))],
        hosted=hosted)


def _ffn_fwd(x, g, srcs, nf, tm, name, hosted=(), head=None):
    t_tok, d = x.shape
    f = nf * N_DEV
    firsts = [first for _, first in srcs]
    n_head = 2 if head else 0

    def body(*refs):
        x_ref, g_ref, sg_ref, su_ref, sd_ref = refs[:5]
        gate_ref, up_ref, wg_v, wu_v, wd_v, sems = refs[-6:]
        _load_group(
            [(sg_ref, firsts[0], nf, wg_v), (su_ref, firsts[1], nf, wu_v), (sd_ref, firsts[2], nf, wd_v)], sems)
        if head:
            t_ref, gf_ref, loss_ref, dx_ref, dxb_ref, dgf_ref = refs[5:11]

            @pl.when(pl.program_id(0) == 0)
            def _():
                loss_ref[...] = jnp.zeros_like(loss_ref)
                dgf_ref[...] = jnp.zeros_like(dgf_ref)

        xv = x_ref[...]
        h = _rms_fwd(xv, g_ref[...])[0].astype(BF16)
        gate = _nt(h, wg_v[...])
        up = _nt(h, wu_v[...])
        gate_ref[...] = gate.astype(BF16)
        up_ref[...] = up.astype(BF16)
        act = (gate * _sigmoid(gate) * up).astype(BF16)
        xo = xv + _nn(act, wd_v[...])
        if head:
            gfv = gf_ref[...]
            y, xhat, r = _rms_fwd(xo, gfv)
            err = y - t_ref[...]
            loss_ref[...] += 0.5 * jnp.sum(_row_mean(err * err), axis=0, keepdims=True)
            dxr, dg_row = _rms_bwd(err * (1.0 / d), xhat, r, gfv)
            dgf_ref[...] += dg_row
            dx_ref[...] = dxr
            dxb_ref[...] = dxr.astype(BF16)
        else:
            refs[5][...] = xo

    act_specs = [_row_spec(tm, f), _row_spec(tm, f)]
    act_shapes = [jax.ShapeDtypeStruct((t_tok, f), BF16), jax.ShapeDtypeStruct((t_tok, f), BF16)]
    if head:
        out_specs = [_const_spec((1, 1)), _row_spec(tm, d), _row_spec(tm, d), _const_spec((1, d))]
        out_shape = [jax.ShapeDtypeStruct((1, 1), F32), jax.ShapeDtypeStruct((t_tok, d), F32),
                     jax.ShapeDtypeStruct((t_tok, d), BF16), jax.ShapeDtypeStruct((1, d), F32)]
    else:
        out_specs = [_row_spec(tm, d)]
        out_shape = [jax.ShapeDtypeStruct((t_tok, d), F32)]
    return _hosting_call(
        body, name, t_tok // tm, [x, g] + [arr for arr, _ in srcs] + list(head or ()),
        in_specs=[_row_spec(tm, d), _const_spec((1, d)), ANY, ANY, ANY] + [_row_spec(tm, d), _const_spec((1, d))][:n_head],
        out_specs=out_specs + act_specs, out_shape=out_shape + act_shapes,
        scratch=[pltpu.VMEM((f, d), BF16), pltpu.VMEM((f, d), BF16), pltpu.VMEM((f, d), BF16),
                 pltpu.SemaphoreType.DMA((3 * N_DEV,))],
        hosted=hosted)


def _ffn_bwd(dout, x, gate, up, g, srcs, nf, tm, name, hosted=()):
    t_tok, d = x.shape
    f = nf * N_DEV
    firsts = [first for _, first in srcs]
    per_chunk = -(-f // (FFN_CHUNKS * MXU_WIDTH)) * MXU_WIDTH
    bounds = [min(ck * per_chunk, f) for ck in range(FFN_CHUNKS + 1)]

    def body(dout_ref, x_ref, gate_ref, up_ref, g_ref, sg_ref, su_ref, sd_ref,
             dx_ref, dxb_ref, h_ref, act_ref, dgu_ref, dg_ref, wg_v, wu_v, wd_v, sems):
        _load_group(
            [(sg_ref, firsts[0], nf, wg_v), (su_ref, firsts[1], nf, wu_v), (sd_ref, firsts[2], nf, wd_v)], sems)

        @pl.when(pl.program_id(0) == 0)
        def _():
            dg_ref[...] = jnp.zeros_like(dg_ref)

        xv = x_ref[...]
        gv = g_ref[...]
        hv, xhat, r = _rms_fwd(xv, gv)
        h_ref[...] = hv.astype(BF16)
        dov = dout_ref[...]
        dob = dov.astype(BF16)
        dh = None
        for ck in range(FFN_CHUNKS):
            cols = slice(bounds[ck], bounds[ck + 1])
            gate_v = gate_ref[:, cols].astype(F32)
            up_v = up_ref[:, cols].astype(F32)
            sig = _sigmoid(gate_v)
            silu = gate_v * sig
            act_ref[:, cols] = (silu * up_v).astype(BF16)
            dact = _nt(dob, wd_v[cols, :])
            dup = (dact * silu).astype(BF16)
            dgate = (dact * up_v * (sig * (1.0 + gate_v * (1.0 - sig)))).astype(BF16)
            dgu_ref[:, cols] = dgate
            dgu_ref[:, f + bounds[ck]:f + bounds[ck + 1]] = dup
            part = _nn(dgate, wg_v[cols, :]) + _nn(dup, wu_v[cols, :])
            dh = part if dh is None else dh + part
        dxr, dg_row = _rms_bwd(dh, xhat, r, gv)
        dg_ref[...] += dg_row
        dx = dov + dxr
        dx_ref[...] = dx
        dxb_ref[...] = dx.astype(BF16)

    return _hosting_call(
        body, name, t_tok // tm, [dout, x, gate, up, g] + [arr for arr, _ in srcs],
        in_specs=[_row_spec(tm, d), _row_spec(tm, d), _row_spec(tm, f), _row_spec(tm, f), _const_spec((1, d)),
                  ANY, ANY, ANY],
        out_specs=[_row_spec(tm, d), _row_spec(tm, d), _row_spec(tm, d), _row_spec(tm, f), _row_spec(tm, 2 * f),
                   _const_spec((1, d))],
        out_shape=[jax.ShapeDtypeStruct((t_tok, d), F32), jax.ShapeDtypeStruct((t_tok, d), BF16),
                   jax.ShapeDtypeStruct((t_tok, d), BF16), jax.ShapeDtypeStruct((t_tok, f), BF16),
                   jax.ShapeDtypeStruct((t_tok, 2 * f), BF16), jax.ShapeDtypeStruct((1, d), F32)],
        scratch=[pltpu.VMEM((f, d), BF16), pltpu.VMEM((f, d), BF16), pltpu.VMEM((f, d), BF16),
                 pltpu.SemaphoreType.DMA((3 * N_DEV,))],
        hosted=hosted)


def _shift_down(z, k, prev_rows):
    row = lax.broadcasted_iota(jnp.int32, z.shape, 0)
    out = pltpu.roll(z, k, 0)
    for j in range(k):
        out = jnp.where(row == j, prev_rows[j], out)
    return out


def _shift_up(z, k, next_rows):
    tm = z.shape[0]
    row = lax.broadcasted_iota(jnp.int32, z.shape, 0)
    out = pltpu.roll(z, tm - k, 0)
    for j in range(k):
        out = jnp.where(row == tm - k + j, next_rows[j], out)
    return out


def _mixer_b_fwd(x, g, gath, conv_w, tm, seq, hosted=()):
    t_tok, d = x.shape
    e = conv_w.shape[1]
    e3 = 3 * e
    n_in, n_out = e3 // N_DEV, e // N_DEV
    tiles_per_seq = seq // tm

    def body(x_ref, g_ref, cw_ref, gath_ref, xo_ref, p_ref, win_v, wout_v, tail_v, sems):
        i = pl.program_id(0)
        _load_group([(gath_ref, 0, n_in, win_v), (gath_ref, n_in, n_out, wout_v)], sems)

        @pl.when(i % tiles_per_seq == 0)
        def _():
            tail_v[...] = jnp.zeros_like(tail_v)

        xv = x_ref[...]
        h = _rms_fwd(xv, g_ref[...])[0].astype(BF16)
        p = _nt(h, win_v[...])
        p_ref[...] = p.astype(BF16)
        z = p[:, e:2 * e] * p[:, 2 * e:]
        prev = [tail_v[SUBLANES - 2:SUBLANES - 1, :], tail_v[SUBLANES - 1:SUBLANES, :]]
        conv = (cw_ref[2:3, :] * z + cw_ref[1:2, :] * _shift_down(z, 1, prev[1:])
                + cw_ref[0:1, :] * _shift_down(z, 2, prev))
        tail_v[...] = z[tm - SUBLANES:, :]
        y = (p[:, :e] * conv).astype(BF16)
        xo_ref[...] = xv + _nn(y, wout_v[...])

    return _hosting_call(
        body, "mixer_b_fwd", t_tok // tm, [x, g, conv_w, gath],
        in_specs=[_row_spec(tm, d), _const_spec((1, d)), _const_spec((SUBLANES, e)), ANY],
        out_specs=[_row_spec(tm, d), _row_spec(tm, e3)],
        out_shape=[jax.ShapeDtypeStruct((t_tok, d), F32), jax.ShapeDtypeStruct((t_tok, e3), BF16)],
        scratch=[pltpu.VMEM((e3, d), BF16), pltpu.VMEM((e, d), BF16), pltpu.VMEM((SUBLANES, e), F32),
                 pltpu.SemaphoreType.DMA((2 * N_DEV,))],
        hosted=hosted)


def _mixer_b_bwd(dout, x, p, g, gath, conv_w, tm, seq, hosted=()):
    t_tok, d = x.shape
    e = conv_w.shape[1]
    e3 = 3 * e
    n_in, n_out = e3 // N_DEV, e // N_DEV
    tiles_per_seq = seq // tm
    halo_per_tile = tm // HALO
    n_halo = t_tok // HALO

    def body(dout_ref, dnext_ref, x_ref, p_ref, pprev_ref, pnext_ref, g_ref, cw_ref, gath_ref,
             dx_ref, dxb_ref, h_ref, y_ref, dp_ref, dg_ref, dcw_ref, win_v, wout_v, sems):
        i = pl.program_id(0)
        _load_group([(gath_ref, 0, n_in, win_v), (gath_ref, n_in, n_out, wout_v)], sems)

        @pl.when(i == 0)
        def _():
            dg_ref[...] = jnp.zeros_like(dg_ref)
            dcw_ref[...] = jnp.zeros_like(dcw_ref)

        first = (i % tiles_per_seq == 0).astype(F32)
        last = (i % tiles_per_seq == tiles_per_seq - 1).astype(F32)
        xv = x_ref[...]
        gv = g_ref[...]
        hv, xhat, r = _rms_fwd(xv, gv)
        h_ref[...] = hv.astype(BF16)
        pv = p_ref[...].astype(F32)
        bg, cg, hx = pv[:, :e], pv[:, e:2 * e], pv[:, 2 * e:]
        z = cg * hx
        pprev = pprev_ref[...].astype(F32)
        zprev = pprev[:, e:2 * e] * pprev[:, 2 * e:] * (1.0 - first)
        prev = [zprev[HALO - 2:HALO - 1, :], zprev[HALO - 1:HALO, :]]
        zs1 = _shift_down(z, 1, prev[1:])
        zs2 = _shift_down(z, 2, prev)
        w0, w1, w2 = cw_ref[0:1, :], cw_ref[1:2, :], cw_ref[2:3, :]
        conv = w2 * z + w1 * zs1 + w0 * zs2
        y_ref[...] = (bg * conv).astype(BF16)

        dov = dout_ref[...]
        wout_bf = wout_v[...]
        dy = _nt(dov.astype(BF16), wout_bf)
        dconv = dy * bg
        dnext = _nt(dnext_ref[...].astype(BF16), wout_bf) * pnext_ref[:, :e].astype(F32) * (1.0 - last)
        nxt = [dnext[0:1, :], dnext[1:2, :]]
        dz = w2 * dconv + w1 * _shift_up(dconv, 1, nxt[:1]) + w0 * _shift_up(dconv, 2, nxt)
        dcw_ref[0:1, :] += _col_sum(dconv * zs2)
        dcw_ref[1:2, :] += _col_sum(dconv * zs1)
        dcw_ref[2:3, :] += _col_sum(dconv * z)
        dp = jnp.concatenate([dy * conv, dz * hx, dz * cg], axis=1).astype(BF16)
        dp_ref[...] = dp
        dh = _nn(dp, win_v[...])
        dxr, dg_row = _rms_bwd(dh, xhat, r, gv)
        dg_ref[...] += dg_row
        dx = dov + dxr
        dx_ref[...] = dx
        dxb_ref[...] = dx.astype(BF16)

    prev_spec = lambda w: pl.BlockSpec((HALO, w), lambda i: (jnp.maximum(i * halo_per_tile - 1, 0), 0))
    next_spec = lambda w: pl.BlockSpec((HALO, w), lambda i: (jnp.minimum((i + 1) * halo_per_tile, n_halo - 1), 0))
    return _hosting_call(
        body, "mixer_b_bwd", t_tok // tm, [dout, dout, x, p, p, p, g, conv_w, gath],
        in_specs=[_row_spec(tm, d), next_spec(d), _row_spec(tm, d), _row_spec(tm, e3), prev_spec(e3), next_spec(e3),
                  _const_spec((1, d)), _const_spec((SUBLANES, e)), ANY],
        out_specs=[_row_spec(tm, d), _row_spec(tm, d), _row_spec(tm, d), _row_spec(tm, e), _row_spec(tm, e3),
                   _const_spec((1, d)), _const_spec((SUBLANES, e))],
        out_shape=[jax.ShapeDtypeStruct((t_tok, d), F32), jax.ShapeDtypeStruct((t_tok, d), BF16),
                   jax.ShapeDtypeStruct((t_tok, d), BF16), jax.ShapeDtypeStruct((t_tok, e), BF16),
                   jax.ShapeDtypeStruct((t_tok, e3), BF16), jax.ShapeDtypeStruct((1, d), F32),
                   jax.ShapeDtypeStruct((SUBLANES, e), F32)],
        scratch=[pltpu.VMEM((e3, d), BF16), pltpu.VMEM((e, d), BF16), pltpu.SemaphoreType.DMA((2 * N_DEV,))],
        hosted=hosted)


def _wgrad(a, b, bm, name, hosted=()):
    t_tok, m = a.shape
    n = b.shape[1]

    def body(a_ref, b_ref, o_ref):
        o_ref[...] = _tn(a_ref[...], b_ref[...]).astype(o_ref.dtype)

    outs, h_outs = _hosting_call(
        body, name, m // bm, [a, b],
        in_specs=[pl.BlockSpec((t_tok, bm), lambda i: (0, i)), _const_spec((t_tok, n))],
        out_specs=[pl.BlockSpec((bm, n), lambda i: (i, 0))],
        out_shape=[jax.ShapeDtypeStruct((m, n), BF16)],
        scratch=[], hosted=hosted)
    return (outs[0], h_outs) if hosted else outs[0]


def _sum_slots(land, rb, name):
    n_slots, rows, cols = land.shape

    def body(l_ref, o_ref):
        acc = l_ref[0].astype(F32)
        for k in range(1, n_slots):
            acc = acc + l_ref[k].astype(F32)
        o_ref[...] = acc

    return pl.pallas_call(
        body, name=name, grid=(rows // rb,),
        in_specs=[pl.BlockSpec((n_slots, rb, cols), lambda i: (0, i, 0))],
        out_specs=pl.BlockSpec((rb, cols), lambda i: (i, 0)),
        out_shape=jax.ShapeDtypeStruct((rows, cols), F32),
        compiler_params=_params(sequential=False),
    )(land)


def _adamw(w, grad, m, v, rb, name):
    rows, cols = w.shape
    c1 = 1.0 / (1.0 - ADAM_B1 ** ADAM_STEP)
    c2 = 1.0 / (1.0 - ADAM_B2 ** ADAM_STEP)

    def body(w_ref, g_ref, m_ref, v_ref, d_ref, mo_ref, vo_ref):
        gv = g_ref[...]
        mn = ADAM_B1 * m_ref[...] + (1.0 - ADAM_B1) * gv
        vn = ADAM_B2 * v_ref[...] + (1.0 - ADAM_B2) * (gv * gv)
        mo_ref[...] = mn
        vo_ref[...] = vn
        d_ref[...] = -ADAM_LR * ((mn * c1) / (jnp.sqrt(vn * c2) + ADAM_EPS) + ADAM_WD * w_ref[...])

    spec = pl.BlockSpec((rb, cols), lambda i: (i, 0))
    shape = jax.ShapeDtypeStruct((rows, cols), F32)
    return pl.pallas_call(
        body, name=name, grid=(rows // rb,),
        in_specs=[spec] * 4, out_specs=[spec] * 3, out_shape=[shape] * 3,
        compiler_params=_params(sequential=False),
    )(w, grad, m, v)


def _pack_shards(groups, name, hosted=()):
    flat = [(part, layer) for group in groups for part, layer, _ in group]
    rows = [[p.shape[2] if turn else p.shape[1] for p, _, turn in group] for group in groups]
    first, _, first_turn = groups[0][0]
    width = first.shape[1] if first_turn else first.shape[2]

    def body(*refs):
        ins, outs = refs[:len(flat)], refs[len(flat):]
        k = 0
        for gi, group in enumerate(groups):
            off = 0
            for (_, _, turn), n in zip(group, rows[gi]):
                part = ins[k][...].astype(BF16)
                if turn:
                    r = lax.broadcasted_iota(jnp.int32, (n, n), 0)
                    c = lax.broadcasted_iota(jnp.int32, (n, n), 1)
                    part = _nt((r == c).astype(BF16), part).astype(BF16)
                outs[gi][off:off + n, :] = part
                off += n
                k += 1

    return _hosting_call(
        body, name, 1, [p for p, _ in flat],
        in_specs=[pl.BlockSpec((None,) + p.shape[1:], lambda i, layer=layer: (layer, 0, 0)) for p, layer in flat],
        out_specs=[_const_spec((sum(r), width)) for r in rows],
        out_shape=[jax.ShapeDtypeStruct((sum(r), width), BF16) for r in rows],
        scratch=[], hosted=hosted)


def _split_bf16(a):
    hi = a.astype(BF16)
    rest = a - hi.astype(F32)
    mid = rest.astype(BF16)
    return hi, mid, (rest - mid.astype(F32)).astype(BF16)


def _reduce_adamw(lands, w, m, v, transpose, name, hosted=()):
    n_layers, rows_w, cols_w = w.shape
    c1 = 1.0 / (1.0 - ADAM_B1 ** ADAM_STEP)
    c2 = 1.0 / (1.0 - ADAM_B2 ** ADAM_STEP)
    flat = [piece for pieces in lands for piece in pieces]
    counts = [len(pieces) for pieces in lands]
    if transpose:
        tiles = rows_w // MXU_WIDTH
        blk = (MXU_WIDTH, cols_w)
        land_specs = [pl.BlockSpec((N_CHIP, n, MXU_WIDTH), lambda i, b=first // n: (0, b, i % tiles))
                      for _, first, n in flat]
        for _, first, n in flat:
            assert first % n == 0
    else:
        tiles = 2
        blk = (rows_w // tiles, cols_w)
        assert all(c == 1 for c in counts)
        land_specs = [pl.BlockSpec((N_CHIP,) + blk, lambda i, b=first // blk[0]: (0, b + i % tiles, 0))
                      for _, first, _ in flat]
        for _, first, _ in flat:
            assert first % blk[0] == 0

    def body(*refs):
        land_refs = refs[:len(flat)]
        w_ref, m_ref, v_ref, g_ref, d_ref, mo_ref, vo_ref = refs[len(flat):]
        layer = pl.program_id(0) // tiles

        def total(ref):
            acc = ref[0].astype(F32)
            for q in range(1, N_CHIP):
                acc = acc + ref[q].astype(F32)
            return acc

        def layer_sum(k):
            first = sum(counts[:k])
            parts = [total(land_refs[first + j]) for j in range(counts[k])]
            return parts[0] if len(parts) == 1 else jnp.concatenate(parts, axis=0)

        gv = layer_sum(0)
        for k in range(1, n_layers):
            gv = jnp.where(layer == k, layer_sum(k), gv)
        if transpose:
            r = lax.broadcasted_iota(jnp.int32, (MXU_WIDTH, MXU_WIDTH), 0)
            c = lax.broadcasted_iota(jnp.int32, (MXU_WIDTH, MXU_WIDTH), 1)
            eye = (r == c).astype(BF16)
            hi, mid, lo = _split_bf16(gv)
            gv = _nt(eye, hi) + _nt(eye, mid) + _nt(eye, lo)
        g_ref[...] = gv
        mn = ADAM_B1 * m_ref[...] + (1.0 - ADAM_B1) * gv
        vn = ADAM_B2 * v_ref[...] + (1.0 - ADAM_B2) * (gv * gv)
        mo_ref[...] = mn
        vo_ref[...] = vn
        d_ref[...] = -ADAM_LR * ((mn * c1) / (jnp.sqrt(vn * c2) + ADAM_EPS) + ADAM_WD * w_ref[...])

    spec = pl.BlockSpec((None,) + blk, lambda i: (i // tiles, i % tiles, 0))
    shape = jax.ShapeDtypeStruct(w.shape, F32)
    outs, h_outs = _hosting_call(
        body, name, n_layers * tiles, [land for land, _, _ in flat] + [w, m, v],
        in_specs=land_specs + [spec] * 3, out_specs=[spec] * 4, out_shape=[shape] * 4, scratch=[], hosted=hosted)
    return (outs, h_outs) if hosted else outs


def _pack_small(parts, rows):
    flat = jnp.concatenate([p.reshape(-1).astype(F32) for p in parts])
    return jnp.pad(flat, (0, rows * LANES - flat.shape[0])).reshape(rows, LANES)


def _unpack_small(packed, shapes):
    flat = packed.reshape(-1)
    out = []
    pos = 0
    for s in shapes:
        n = math.prod(s)
        out.append(flat[pos:pos + n].reshape(s))
        pos += n
    return out


def kernel(x, mix_norm, ffn_norm, a_w_in, a_v_gain, a_v_bias, a_w_s, a_b_s, a_w_out, b_w_in, b_conv_w, b_w_out, ffn_w_gate, ffn_w_up, ffn_w_down, final_norm, loss_target, m_mix_norm, m_ffn_norm, m_a_w_in, m_a_v_gain, m_a_v_bias, m_a_w_s, m_a_b_s, m_a_w_out, m_b_w_in, m_b_conv_w, m_b_w_out, m_ffn_w_gate, m_ffn_w_up, m_ffn_w_down, m_final_norm, v_mix_norm, v_ffn_norm, v_a_w_in, v_a_v_gain, v_a_v_bias, v_a_w_s, v_a_b_s, v_a_w_out, v_b_w_in, v_b_conv_w, v_b_w_out, v_ffn_w_gate, v_ffn_w_up, v_ffn_w_down, v_final_norm):
    bsz, seq, d = x.shape
    t_tok = bsz * seq
    me = _my_index()
    xt = x.reshape(t_tok, d)
    target = loss_target.reshape(t_tok, d)
    e_a = a_v_gain.shape[1]
    e_b = b_w_out.shape[1] * N_DEV
    n_layers = ffn_w_gate.shape[0]
    f_shard = ffn_w_gate.shape[2]
    f_full = f_shard * N_DEV

    conv_pad = jnp.pad(b_conv_w[0], ((0, SUBLANES - CONV_W), (0, 0)))
    sh_a = jnp.concatenate([a_w_in[0].T, a_w_out[0]]).astype(BF16)
    bfull = jnp.repeat(a_b_s[0].T, GROUP, axis=1)

    gate_t, up_t = ffn_w_gate.transpose(0, 2, 1), ffn_w_up.transpose(0, 2, 1)
    (sh_b, sh_f0, sh_f1g, sh_f1ud), (gath_a, conv_g) = _pack_shards(
        [[(b_w_in, 0, True), (b_w_out, 0, False)],
         [(gate_t, 0, False), (up_t, 0, False), (ffn_w_down, 0, False)],
         [(gate_t, 1, False)],
         [(up_t, 1, False), (ffn_w_down, 1, False)]],
        "pack_shards", hosted=[_HostedGathers([sh_a, conv_pad], 0)])
    conv_full = jnp.pad(conv_g[:, :CONV_W, :].transpose(1, 0, 2).reshape(CONV_W, e_b), ((0, SUBLANES - CONV_W), (0, 0)))
    (x1, gd_a, u_a, vhat_a, sv_a, y_a, rstd_a), (gath_f0,) = _mixer_a_fwd(
        xt, mix_norm[0:1], gath_a, a_v_gain, a_v_bias, a_w_s[0], bfull, tm=256,
        hosted=[_HostedGathers([sh_f0], mid_lead=2)])
    srcs0 = [(gath_f0, 0), (gath_f0, f_shard), (gath_f0, 2 * f_shard)]
    (x2, gate0, up0), (gath_b, gath_f1g) = _ffn_fwd(x1, ffn_norm[0:1], srcs0, f_shard, tm=256, name="ffn_fwd0",
                                                    hosted=[_HostedGathers([sh_b, sh_f1g], mid_lead=2)])
    (x3, p_b), (gath_f1ud,) = _mixer_b_fwd(x2, mix_norm[1:2], gath_b, conv_full, tm=256, seq=seq,
                                           hosted=[_HostedGathers([sh_f1ud], mid_lead=2)])
    srcs1 = [(gath_f1g, 0), (gath_f1ud, 0), (gath_f1ud, f_shard)]
    (loss_part, dx4, dx4_bf, d_final, gate1, up1), _ = _ffn_fwd(
        x3, ffn_norm[1:2], srcs1, f_shard, tm=256, name="ffn_fwd1", head=(target, final_norm.reshape(1, d)))

    ffn_entries = [(0, 0, f_shard), (0, f_full, f_shard), (1, 0, f_shard)]
    (dx3, dx3_bf, h_f1, act1, dgu1, d_fn1), _ = _ffn_bwd(dx4, x3, gate1, up1, ffn_norm[1:2], srcs1, f_shard, tm=256,
                                                         name="ffn_bwd1")
    g_down1 = _wgrad(act1, dx4_bf, 256, "wgrad_down1")
    g_gu1 = _wgrad(dgu1, h_f1, 512, "wgrad_gate_up1")
    ps_f1 = _pair_reduce([g_gu1, g_down1], ffn_entries, "pair_reduce_f1")
    (dx2, dx2_bf, h_b, y_b, dp_b, d_mn1, d_conv), (land_f1gu,) = _mixer_b_bwd(
        dx3, x2, p_b, mix_norm[1:2], gath_b, conv_full, tm=256, seq=seq,
        hosted=[_HostedChipScatter(ps_f1, 0, 2 * f_shard)])
    g_b_out = _wgrad(y_b, dx3_bf, 256, "wgrad_b_out")
    g_b_in = _wgrad(dp_b, h_b, 512, "wgrad_b_in")
    ps_b = _pair_reduce([g_b_in, g_b_out], [(0, 0, b_w_in.shape[2]), (1, 0, b_w_out.shape[1])], "pair_reduce_b")
    (dx1, dx1_bf, h_f0, act0, dgu0, d_fn0), (land_f1d, land_b) = _ffn_bwd(
        dx2, x1, gate0, up0, ffn_norm[0:1], srcs0, f_shard, tm=256, name="ffn_bwd0",
        hosted=[_HostedChipScatter(ps_f1, 2 * f_shard, f_shard), _HostedChipScatter(ps_b)])
    g_down0 = _wgrad(act0, dx2_bf, 256, "wgrad_down0")
    g_gu0 = _wgrad(dgu0, h_f0, 512, "wgrad_gate_up0")
    ps_f0 = _pair_reduce([g_gu0, g_down0], ffn_entries, "pair_reduce_f0")
    g_a_out = _wgrad(y_a, dx1_bf, 256, "wgrad_a_out")
    ps_ao = _pair_reduce([g_a_out], [(0, 0, a_w_out.shape[1])], "pair_reduce_a_out")
    (dx0, _, h_a, dz_a, d_mn0, d_gain, d_bias, d_ws, d_bs_acc), (land_f0, land_ao) = _mixer_a_bwd(
        dx1, xt, gd_a, u_a, vhat_a, sv_a, rstd_a, mix_norm[0:1], gath_a, a_v_gain, a_v_bias, a_w_s[0], tm=256,
        hosted=[_HostedChipScatter(ps_f0), _HostedChipScatter(ps_ao)])
    d_bs = d_bs_acc.reshape(HEADS, CHUNK)

    small_grads = [jnp.concatenate([d_mn0, d_mn1]), jnp.concatenate([d_fn0, d_fn1]), d_gain, d_bias, d_ws, d_bs,
                   d_final, d_conv[:CONV_W], loss_part]
    small_shapes = [(n_layers, d), (n_layers, d), (1, e_a), (1, e_a), (1, HEADS, CHUNK, CHUNK), (1, HEADS, CHUNK), (d,),
                    (CONV_W, e_b), ()]
    n_small = sum(math.prod(s) for s in small_shapes)
    blk_rows = -(-n_small // (N_DEV * LANES * SUBLANES)) * SUBLANES
    small_rows = blk_rows * N_DEV
    packed = _pack_small(small_grads, small_rows)
    g_a_in, (small_land,) = _wgrad(dz_a, h_a, 512, "wgrad_a_in", hosted=[_HostedScatterAll(packed)])
    ps_ai = _pair_reduce([g_a_in], [(0, 0, a_w_in.shape[2])], "pair_reduce_a_in")
    small_sum = _sum_slots(small_land, blk_rows, "sum_small")

    n_piece = a_w_in.shape[2] // N_CHIP
    n_b_in = b_w_in.shape[2]

    def piece(k):
        return _HostedChipScatter(ps_ai, k * n_piece, n_piece)

    gate_out, (land_ai0,) = _reduce_adamw(
        [[(land_f0, 0, f_shard)], [(land_f1gu, 0, f_shard)]], gate_t, m_ffn_w_gate.transpose(0, 2, 1),
        v_ffn_w_gate.transpose(0, 2, 1), False, "adamw_gate", hosted=[piece(0)])
    up_out, (land_ai1,) = _reduce_adamw(
        [[(land_f0, f_shard, f_shard)], [(land_f1gu, f_shard, f_shard)]], up_t, m_ffn_w_up.transpose(0, 2, 1),
        v_ffn_w_up.transpose(0, 2, 1), False, "adamw_up", hosted=[piece(1)])
    down_out, (land_ai2, small_gath) = _reduce_adamw(
        [[(land_f0, 2 * f_shard, f_shard)], [(land_f1d, 0, f_shard)]], ffn_w_down, m_ffn_w_down, v_ffn_w_down,
        False, "adamw_down", hosted=[piece(2), _HostedGathers([small_sum], 1)])
    b_in_out, (land_ai3,) = _reduce_adamw([[(land_b, 0, n_b_in)]], b_w_in, m_b_w_in, v_b_w_in, True, "adamw_b_in",
                                          hosted=[piece(3)])
    res = {
        "a_w_in": _reduce_adamw([[(land, 0, n_piece) for land in (land_ai0, land_ai1, land_ai2, land_ai3)]],
                                a_w_in, m_a_w_in, v_a_w_in, True, "adamw_a_in"),
        "a_w_out": _reduce_adamw([[(land_ao, 0, a_w_out.shape[1])]], a_w_out, m_a_w_out, v_a_w_out, False,
                                 "adamw_a_out"),
        "b_w_in": b_in_out,
        "b_w_out": _reduce_adamw([[(land_b, n_b_in, b_w_out.shape[1])]], b_w_out, m_b_w_out, v_b_w_out, False,
                                 "adamw_b_out"),
        "ffn_w_gate": [o.transpose(0, 2, 1) for o in gate_out],
        "ffn_w_up": [o.transpose(0, 2, 1) for o in up_out],
        "ffn_w_down": down_out,
    }
    small_all = small_gath.reshape(small_rows, LANES)

    (gr_mix, gr_ffn, gr_gain, gr_bias, gr_ws, gr_bs, gr_final, gr_conv_full, loss) = _unpack_small(small_all, small_shapes)
    gr_conv = lax.dynamic_slice_in_dim(gr_conv_full, me * (e_b // N_DEV), e_b // N_DEV, axis=1)[None]

    small_w =[mix_norm, ffn_norm, a_v_gain, a_v_bias, a_w_s, a_b_s, final_norm]
    small_m = [m_mix_norm, m_ffn_norm, m_a_v_gain, m_a_v_bias, m_a_w_s, m_a_b_s, m_final_norm]
    small_v = [v_mix_norm, v_ffn_norm, v_a_v_gain, v_a_v_bias, v_a_w_s, v_a_b_s, v_final_norm]
    small_g = [gr_mix, gr_ffn, gr_gain, gr_bias, gr_ws, gr_bs, gr_final]
    sm_shapes = small_shapes[:len(small_w)]
    sm_out = _adamw(_pack_small(small_w, small_rows), _pack_small(small_g, small_rows), _pack_small(small_m, small_rows),
                    _pack_small(small_v, small_rows), small_rows, "adamw_small")
    sm_delta, sm_m, sm_v = [_unpack_small(o, sm_shapes) for o in sm_out]

    conv_out = _adamw(b_conv_w[0], gr_conv[0], m_b_conv_w[0], v_b_conv_w[0], CONV_W, "adamw_conv")
    conv_delta, conv_m, conv_v = [o[None] for o in conv_out]

    order = ["mix_norm", "ffn_norm", "a_w_in", "a_v_gain", "a_v_bias", "a_w_s", "a_b_s", "a_w_out", "b_w_in",
             "b_conv_w", "b_w_out", "ffn_w_gate", "ffn_w_up", "ffn_w_down", "final_norm"]
    small_names = ["mix_norm", "ffn_norm", "a_v_gain", "a_v_bias", "a_w_s", "a_b_s", "final_norm"]
    grads = {"b_conv_w": gr_conv}
    deltas, new_m, new_v = {}, {}, {}
    for k, name in enumerate(small_names):
        grads[name] = small_g[k]
        deltas[name], new_m[name], new_v[name] = sm_delta[k], sm_m[k], sm_v[k]
    deltas["b_conv_w"], new_m["b_conv_w"], new_v["b_conv_w"] = conv_delta, conv_m, conv_v
    for name, (gg, dl, mm, vv) in res.items():
        grads[name], deltas[name], new_m[name], new_v[name] = gg, dl, mm, vv

    grad_x = dx0.reshape(bsz, seq, d)
    return (loss, grad_x, *[grads[n] for n in order], *[deltas[n] for n in order],
            *[new_m[n] for n in order], *[new_v[n] for n in order])
```

```python
import math

import jax
import jax.numpy as jnp
from jax import lax
from jax.experimental import pallas as pl
from jax.experimental.pallas import tpu as pltpu

F32 = jnp.float32
BF16 = jnp.bfloat16

N_DEV = 8
N_CHIP = 4
CHUNK = 128
HEADS = 16
GROUP = 128
CONV_W = 3
NORM_EPS = 1e-6
GELU_C = math.sqrt(2.0 / math.pi)
GELU_K = 0.044715

ADAM_LR = 0.001
ADAM_B1 = 0.9
ADAM_B2 = 0.999
ADAM_EPS = 1e-08
ADAM_WD = 0.01
ADAM_STEP = 10

LANES = 128
SUBLANES = 8
VMEM_LIMIT = 60 * 1024 * 1024
HALO = 16
MXU_WIDTH = 256
FFN_CHUNKS = 2

MESH = pl.DeviceIdType.MESH
ANY = pl.BlockSpec(memory_space=pl.ANY)


def _params(sequential=True):
    return pltpu.CompilerParams(
        dimension_semantics=("arbitrary",) if sequential else None,
        vmem_limit_bytes=VMEM_LIMIT)


def _nn(a, b):
    return jnp.dot(a, b, preferred_element_type=F32)


def _nt(a, b):
    return lax.dot_general(a, b, (((1,), (1,)), ((), ())), preferred_element_type=F32)


def _tn(a, b):
    return lax.dot_general(a, b, (((0,), (0,)), ((), ())), preferred_element_type=F32)


def _row_mean(a):
    return jnp.mean(a, axis=-1, keepdims=True)


def _col_sum(a):
    return jnp.sum(a, axis=0, keepdims=True)


def _rms_fwd(x, g):
    r = lax.rsqrt(_row_mean(x * x) + NORM_EPS)
    xhat = x * r
    return xhat * g, xhat, r


def _rms_bwd(dh, xhat, r, g):
    a = dh * g
    dx = r * (a - xhat * _row_mean(a * xhat))
    return dx, _col_sum(dh * xhat)


def _gelu_and_grad(x):
    x2 = x * x
    t = jnp.tanh(x * (GELU_C + (GELU_C * GELU_K) * x2))
    half = 0.5 * t + 0.5
    d = half + x * (0.5 - 0.5 * (t * t)) * (GELU_C + (3.0 * GELU_C * GELU_K) * x2)
    return x * half, d


def _sigmoid(x):
    return 1.0 / (1.0 + jnp.exp(-x))


def _row_spec(tm, width):
    return pl.BlockSpec((tm, width), lambda i: (i, 0))


def _const_spec(shape):
    nd = len(shape)
    return pl.BlockSpec(shape, lambda i: (0,) * nd)


def _load_group(parts, sems):
    @pl.when(pl.program_id(0) == 0)
    def _():
        copies = []
        for k, (gath_ref, first, n, dst) in enumerate(parts):
            for j in range(N_DEV):
                copies.append(pltpu.make_async_copy(gath_ref.at[j, pl.ds(first, n), :], dst.at[pl.ds(j * n, n), :],
                                                    sems.at[k * N_DEV + j]))
        for cp in copies:
            cp.start()
        for cp in copies:
            cp.wait()


def _hosting_call(body, name, n_steps, arrays, in_specs, out_specs, out_shape, scratch, hosted=()):
    n_in, n_out, n_scr = len(arrays), len(out_shape), len(scratch)
    h_arrays = [a for h in hosted for a in h.arrays]
    h_shapes = [s for h in hosted for s in h.out_shapes]
    h_sems = [s for h in hosted for s in h.sem_shapes]

    def full_body(*refs):
        pos = 0
        groups = []
        for n in (n_in, len(h_arrays), n_out, len(h_shapes), n_scr, len(h_sems)):
            groups.append(refs[pos:pos + n])
            pos += n
        own_in, h_in, own_out, h_out, own_scr, h_sem = groups
        per_host = []
        pi = po = ps = 0
        for h in hosted:
            ni, no, ns = len(h.arrays), len(h.out_shapes), len(h.sem_shapes)
            per_host.append((h, h_in[pi:pi + ni], h_out[po:po + no], h_sem[ps:ps + ns]))
            pi, po, ps = pi + ni, po + no, ps + ns
        for h, ins, outs, sems in per_host:
            h.begin(ins, outs, sems, n_steps)
        body(*own_in, *own_out, *own_scr)
        for h, ins, outs, sems in per_host:
            h.end(ins, outs, sems, n_steps)

    outs = pl.pallas_call(
        full_body, name=name, grid=(n_steps,),
        in_specs=list(in_specs) + [ANY] * len(h_arrays),
        out_specs=list(out_specs) + [ANY] * len(h_shapes),
        out_shape=list(out_shape) + h_shapes,
        scratch_shapes=list(scratch) + h_sems,
        compiler_params=_params(),
    )(*arrays, *h_arrays)
    return outs[:n_out], outs[n_out:]


def _my_index():
    return 4 * lax.axis_index("x") + 2 * lax.axis_index("y") + lax.axis_index("c")


GATHER_COPIES = 8


class _Gather:
    def __init__(self, shard, out, send_sems, recv_sems, local_sem):
        self.shard, self.out = shard, out
        self.send_sems, self.recv_sems, self.local_sem = send_sems, recv_sems, local_sem
        x, y, c = lax.axis_index("x"), lax.axis_index("y"), lax.axis_index("c")
        self.c = c
        self.me, self.sibling = (x, y, c), (x, y, 1 - c)
        self.xn, self.yn, self.dg = (1 - x, y), (x, 1 - y), (1 - x, 1 - y)
        self.n = shard.shape[0]
        self.half = self.n // 2
        rows_per_tile = SUBLANES * 4 // shard.dtype.itemsize
        self.relays = self.n % 2 == 0 and self.half % rows_per_tile == 0

    def _slot(self, dev, lo=0, hi=None):
        hi = self.n if hi is None else hi
        return self.out.at[4 * dev[0] + 2 * dev[1] + dev[2], pl.ds(lo, hi - lo), :]

    def _copy(self, k, block, to, src=None, lo=0, hi=None):
        return pltpu.make_async_remote_copy(
            src_ref=self._slot(block, lo, hi) if src is None else src, dst_ref=self._slot(block, lo, hi),
            send_sem=self.send_sems.at[k], recv_sem=self.recv_sems.at[k], device_id=to, device_id_type=MESH)

    def _local(self):
        return pltpu.make_async_copy(self.shard, self._slot(self.me), self.local_sem)

    def start(self):
        c = self.c
        self._local().start()
        self._copy(0, self.me, self.sibling, src=self.shard).start()
        self._copy(1, self.me, (*self.xn, c), src=self.shard).start()
        self._copy(2, self.me, (*self.yn, c), src=self.shard).start()
        if not self.relays:
            self._copy(3, self.me, (*self.dg, c), src=self.shard).start()

    def relay(self):
        c = self.c
        if self.relays:
            self._copy(1, (*self.xn, c), self.me).wait_recv()
            self._copy(3, (*self.xn, c), (*self.yn, c), hi=self.half).start()
            self._copy(2, (*self.yn, c), self.me).wait_recv()
            self._copy(4, (*self.yn, c), (*self.xn, c), lo=self.half).start()

    def forward(self):
        c = self.c
        if self.relays:
            self._copy(5, (*self.xn, c), self.sibling).start()
            self._copy(6, (*self.yn, c), self.sibling).start()
            self._copy(3, (*self.dg, c), self.me, hi=self.half).wait_recv()
            self._copy(4, (*self.dg, c), self.me, lo=self.half).wait_recv()
        else:
            self._copy(1, (*self.xn, c), self.me).wait_recv()
            self._copy(5, (*self.xn, c), self.sibling).start()
            self._copy(2, (*self.yn, c), self.me).wait_recv()
            self._copy(6, (*self.yn, c), self.sibling).start()
            self._copy(3, (*self.dg, c), self.me).wait_recv()
        self._copy(7, (*self.dg, c), self.sibling).start()

    def finish(self):
        c = self.c
        self._copy(0, self.sibling, self.me).wait_recv()
        for k, chip in ((5, self.xn), (6, self.yn), (7, self.dg)):
            self._copy(k, (*chip, 1 - c), self.me).wait_recv()
        for k in (0, 1, 2, 5, 6, 7):
            self._copy(k, self.me, self.sibling).wait_send()
        if self.relays:
            self._copy(3, self.me, self.sibling, hi=self.half).wait_send()
            self._copy(4, self.me, self.sibling, lo=self.half).wait_send()
        else:
            self._copy(3, self.me, self.sibling).wait_send()
        self._local().wait()


class _HostedGathers:
    def __init__(self, shards, mid_lead, relay_at=0.56):
        n = len(shards)
        self.arrays = shards
        self.mid_lead, self.relay_at = mid_lead, relay_at
        self.out_shapes = [jax.ShapeDtypeStruct((N_DEV,) + s.shape, s.dtype) for s in shards]
        self.sem_shapes = [pltpu.SemaphoreType.DMA((n, GATHER_COPIES)), pltpu.SemaphoreType.DMA((n, GATHER_COPIES)),
                           pltpu.SemaphoreType.DMA((n,))]

    def _gathers(self, ins, outs, sems):
        return [_Gather(ins[a], outs[a], sems[0].at[a], sems[1].at[a], sems[2].at[a]) for a in range(len(ins))]

    def begin(self, ins, outs, sems, n_steps):
        i = pl.program_id(0)
        forward_step = max(n_steps - 1 - self.mid_lead, 0)
        relay_step = min(int(self.relay_at * n_steps), forward_step)

        @pl.when(i == 0)
        def _():
            for g in self._gathers(ins, outs, sems):
                g.start()

        if n_steps == 1:
            return

        @pl.when(i == relay_step)
        def _():
            for g in self._gathers(ins, outs, sems):
                g.relay()

        @pl.when(i == forward_step)
        def _():
            for g in self._gathers(ins, outs, sems):
                g.forward()

    def end(self, ins, outs, sems, n_steps):
        @pl.when(pl.program_id(0) == n_steps - 1)
        def _():
            gathers = self._gathers(ins, outs, sems)
            if n_steps == 1:
                for g in gathers:
                    g.relay()
                for g in gathers:
                    g.forward()
            for g in gathers:
                g.finish()


def _exchange(hosted, name):
    return _hosting_call(lambda: None, name, 1, [], [], [], [], [], hosted=hosted)[1]


class _ChipScatter:
    def __init__(self, pairsum, row0, land, send_sems, recv_sems, local_sem):
        self.pairsum, self.row0, self.land = pairsum, row0, land
        self.send_sems, self.recv_sems, self.local_sem = send_sems, recv_sems, local_sem
        x, y, c = lax.axis_index("x"), lax.axis_index("y"), lax.axis_index("c")
        self.c = c
        self.chip = 2 * x + y
        self.others = [(1 - x, y), (x, 1 - y), (1 - x, 1 - y)]

    def _src(self, chip):
        return self.pairsum.at[chip, pl.ds(self.row0, self.land.shape[1]), :]

    def _copy(self, k):
        ox, oy = self.others[k]
        return pltpu.make_async_remote_copy(
            src_ref=self._src(2 * ox + oy), dst_ref=self.land.at[self.chip],
            send_sem=self.send_sems.at[k], recv_sem=self.recv_sems.at[k], device_id=(ox, oy, self.c),
            device_id_type=MESH)

    def _arrival(self, k):
        ox, oy = self.others[k]
        return pltpu.make_async_remote_copy(
            src_ref=self._src(self.chip), dst_ref=self.land.at[2 * ox + oy],
            send_sem=self.send_sems.at[k], recv_sem=self.recv_sems.at[k], device_id=(ox, oy, self.c),
            device_id_type=MESH)

    def _local(self):
        return pltpu.make_async_copy(self._src(self.chip), self.land.at[self.chip], self.local_sem)

    def start(self):
        self._local().start()
        for k in range(N_CHIP - 1):
            self._copy(k).start()

    def finish(self):
        for k in range(N_CHIP - 1):
            self._arrival(k).wait_recv()
        for k in range(N_CHIP - 1):
            self._copy(k).wait_send()
        self._local().wait()


class _HostedChipScatter:
    def __init__(self, pairsum, row0=0, n=None):
        n = pairsum.shape[1] - row0 if n is None else n
        self.row0 = row0
        self.arrays = [pairsum]
        self.out_shapes = [jax.ShapeDtypeStruct((N_CHIP, n, pairsum.shape[2]), pairsum.dtype)]
        self.sem_shapes = [pltpu.SemaphoreType.DMA((N_CHIP - 1,)), pltpu.SemaphoreType.DMA((N_CHIP - 1,)),
                           pltpu.SemaphoreType.DMA(())]

    def begin(self, ins, outs, sems, n_steps):
        @pl.when(pl.program_id(0) == 0)
        def _():
            _ChipScatter(ins[0], self.row0, outs[0], *sems).start()

    def end(self, ins, outs, sems, n_steps):
        @pl.when(pl.program_id(0) == n_steps - 1)
        def _():
            _ChipScatter(ins[0], self.row0, outs[0], *sems).finish()


def _pair_reduce(arrays, entries, name):
    n_arr, n_ent = len(arrays), len(entries)
    cols = arrays[0].shape[1]
    offsets = []
    total = 0
    for _, _, n in entries:
        offsets.append(total)
        total += n

    def body(*refs):
        ins, out_ref = refs[:n_arr], refs[n_arr]
        rbuf, own, send_sems, recv_sems, own_sems = refs[n_arr + 1:]
        q = pl.program_id(0)
        x, y, c = lax.axis_index("x"), lax.axis_index("y"), lax.axis_index("c")

        def block(e, chip, core):
            ai, first, n = entries[e]
            return ins[ai].at[pl.ds(first + (2 * chip + core) * n, n), :]

        def to_sibling(e, chip):
            return pltpu.make_async_remote_copy(
                src_ref=block(e, chip, 1 - c), dst_ref=rbuf.at[chip, pl.ds(offsets[e], entries[e][2]), :],
                send_sem=send_sems.at[e, chip], recv_sem=recv_sems.at[e, chip], device_id=(x, y, 1 - c),
                device_id_type=MESH)

        @pl.when(q == 0)
        def _():
            for chip in range(N_CHIP):
                for e in range(n_ent):
                    to_sibling(e, chip).start()

        loads = [pltpu.make_async_copy(block(e, q, c), own.at[pl.ds(offsets[e], entries[e][2]), :], own_sems.at[e])
                 for e in range(n_ent)]
        for cp in loads:
            cp.start()
        for cp in loads:
            cp.wait()
        for e in range(n_ent):
            to_sibling(e, q).wait_recv()
        out_ref[...] = (own[...].astype(F32) + rbuf[q].astype(F32)).astype(out_ref.dtype)

        @pl.when(q == N_CHIP - 1)
        def _():
            for chip in range(N_CHIP):
                for e in range(n_ent):
                    to_sibling(e, chip).wait_send()

    return pl.pallas_call(
        body, name=name, grid=(N_CHIP,),
        in_specs=[ANY] * n_arr,
        out_specs=pl.BlockSpec((None, total, cols), lambda q: (q, 0, 0)),
        out_shape=jax.ShapeDtypeStruct((N_CHIP, total, cols), BF16),
        scratch_shapes=[pltpu.VMEM((N_CHIP, total, cols), BF16), pltpu.VMEM((total, cols), BF16),
                        pltpu.SemaphoreType.DMA((n_ent, N_CHIP)), pltpu.SemaphoreType.DMA((n_ent, N_CHIP)),
                        pltpu.SemaphoreType.DMA((n_ent,))],
        compiler_params=_params(),
    )(*arrays)


class _HostedScatterAll:
    def __init__(self, packed):
        n = packed.shape[0] // N_DEV
        self.n = n
        self.arrays = [packed]
        self.out_shapes = [jax.ShapeDtypeStruct((N_DEV, n, packed.shape[1]), packed.dtype)]
        self.sem_shapes = [pltpu.SemaphoreType.DMA((N_DEV - 1,)), pltpu.SemaphoreType.DMA((N_DEV - 1,)),
                           pltpu.SemaphoreType.DMA(())]

    def _copies(self, ins, outs, sems, with_arrivals):
        src, land = ins[0], outs[0]
        send_sems, recv_sems, local_sem = sems
        me = _my_index()

        def block(p):
            return src.at[pl.ds(p * self.n, self.n), :]

        local = pltpu.make_async_copy(block(me), land.at[me], local_sem)
        sends, arrivals = [], []
        for k in range(1, N_DEV):
            p = (me + k) % N_DEV
            q = (me + N_DEV - k) % N_DEV
            sends.append(pltpu.make_async_remote_copy(
                src_ref=block(p), dst_ref=land.at[me], send_sem=send_sems.at[k - 1], recv_sem=recv_sems.at[k - 1],
                device_id=(p // 4, (p // 2) % 2, p % 2), device_id_type=MESH))
            if with_arrivals:
                arrivals.append(pltpu.make_async_remote_copy(
                    src_ref=block(me), dst_ref=land.at[q], send_sem=send_sems.at[k - 1], recv_sem=recv_sems.at[k - 1],
                    device_id=(q // 4, (q // 2) % 2, q % 2), device_id_type=MESH))
        return local, sends, arrivals

    def begin(self, ins, outs, sems, n_steps):
        @pl.when(pl.program_id(0) == 0)
        def _():
            local, sends, _ = self._copies(ins, outs, sems, with_arrivals=False)
            local.start()
            for cp in sends:
                cp.start()

    def end(self, ins, outs, sems, n_steps):
        @pl.when(pl.program_id(0) == n_steps - 1)
        def _():
            local, sends, arrivals = self._copies(ins, outs, sems, with_arrivals=True)
            for cp in arrivals:
                cp.wait_recv()
            for cp in sends:
                cp.wait_send()
            local.wait()


def _tril_weights(ws_ref):
    r = lax.broadcasted_iota(jnp.int32, (CHUNK, CHUNK), 0)
    c = lax.broadcasted_iota(jnp.int32, (CHUNK, CHUNK), 1)
    return [jnp.where(r >= c, ws_ref[h], 0.0).astype(BF16) for h in range(HEADS)]


def _sgu_stats(zpre, gain, bias):
    e = zpre.shape[1] // 2
    z, dz = _gelu_and_grad(zpre)
    u, v = z[:, :e], z[:, e:]
    vc = v - _row_mean(v)
    rstd = lax.rsqrt(_row_mean(vc * vc) + NORM_EPS)
    vhat = vc * rstd
    return u, vhat, rstd, vhat * gain + bias, dz


def _spatial_fwd(wt, vn_bf, bfull_ref, sv_ref, tm):
    for ci in range(tm // CHUNK):
        rows = slice(ci * CHUNK, (ci + 1) * CHUNK)
        for h in range(HEADS):
            cols = slice(h * GROUP, (h + 1) * GROUP)
            sv_ref[rows, cols] = _nn(wt[h], vn_bf[rows, cols]) + bfull_ref[:, cols]


def _mixer_a_fwd(x, g, gath, gain, bias, ws, bfull, tm, hosted=()):
    t_tok, d = x.shape
    e = gain.shape[1]
    e2 = 2 * e
    n_in, n_out = e2 // N_DEV, e // N_DEV

    def body(x_ref, g_ref, gain_ref, bias_ref, ws_ref, bfull_ref, gath_ref,
             xo_ref, gd_ref, u_ref, vhat_ref, svo_ref, y_ref, rstd_ref, win_v, wout_v, sv_v, sems):
        _load_group([(gath_ref, 0, n_in, win_v), (gath_ref, n_in, n_out, wout_v)], sems)
        xv = x_ref[...]
        h = _rms_fwd(xv, g_ref[...])[0].astype(BF16)
        zpre = _nt(h, win_v[...])
        u, vhat, rstd, vn, gelu_d = _sgu_stats(zpre, gain_ref[...], bias_ref[...])
        gd_ref[...] = gelu_d.astype(BF16)
        u_ref[...] = u.astype(BF16)
        vhat_ref[...] = vhat.astype(BF16)
        rstd_ref[...] = rstd
        _spatial_fwd(_tril_weights(ws_ref), vn.astype(BF16), bfull_ref, sv_v, tm)
        sv = sv_v[...]
        svo_ref[...] = sv.astype(BF16)
        y = (u * sv).astype(BF16)
        y_ref[...] = y
        xo_ref[...] = xv + _nn(y, wout_v[...])

    return _hosting_call(
        body, "mixer_a_fwd", t_tok // tm, [x, g, gain, bias, ws, bfull, gath],
        in_specs=[_row_spec(tm, d), _const_spec((1, d)), _const_spec((1, e)), _const_spec((1, e)),
                  _const_spec((HEADS, CHUNK, CHUNK)), _const_spec((CHUNK, e)), ANY],
        out_specs=[_row_spec(tm, d), _row_spec(tm, e2), _row_spec(tm, e), _row_spec(tm, e), _row_spec(tm, e),
                   _row_spec(tm, e), _row_spec(tm, 1)],
        out_shape=[jax.ShapeDtypeStruct((t_tok, d), F32), jax.ShapeDtypeStruct((t_tok, e2), BF16),
                   jax.ShapeDtypeStruct((t_tok, e), BF16), jax.ShapeDtypeStruct((t_tok, e), BF16),
                   jax.ShapeDtypeStruct((t_tok, e), BF16), jax.ShapeDtypeStruct((t_tok, e), BF16),
                   jax.ShapeDtypeStruct((t_tok, 1), F32)],
        scratch=[pltpu.VMEM((e2, d), BF16), pltpu.VMEM((e, d), BF16), pltpu.VMEM((tm, e), F32),
                 pltpu.SemaphoreType.DMA((2 * N_DEV,))],
        hosted=hosted)


def _mixer_a_bwd(dout, x, gd, u_sav, vhat_sav, sv_sav, rstd_sav, g, gath, gain, bias, ws, tm, hosted=()):
    t_tok, d = x.shape
    e = gain.shape[1]
    e2 = 2 * e
    n_in, n_out = e2 // N_DEV, e // N_DEV
    n_steps = t_tok // tm

    def body(dout_ref, x_ref, gd_ref, u_ref, vhat_ref, sv_ref, rstd_ref, g_ref, gain_ref, bias_ref, ws_ref, gath_ref,
             dx_ref, dxb_ref, h_ref, dz_ref, dg_ref, dgain_ref, dbias_ref, dws_ref, dbso_ref,
             win_v, wout_v, dvn_v, dbs_ref, sems):
        i = pl.program_id(0)
        _load_group([(gath_ref, 0, n_in, win_v), (gath_ref, n_in, n_out, wout_v)], sems)

        @pl.when(i == 0)
        def _():
            dg_ref[...] = jnp.zeros_like(dg_ref)
            dgain_ref[...] = jnp.zeros_like(dgain_ref)
            dbias_ref[...] = jnp.zeros_like(dbias_ref)
            dws_ref[...] = jnp.zeros_like(dws_ref)
            dbs_ref[...] = jnp.zeros_like(dbs_ref)

        xv = x_ref[...]
        gv = g_ref[...]
        hv, xhat, r = _rms_fwd(xv, gv)
        h_ref[...] = hv.astype(BF16)
        gain_v = gain_ref[...]
        vhat = vhat_ref[...].astype(F32)
        vn_bf = (vhat * gain_v + bias_ref[...]).astype(BF16)
        wt = _tril_weights(ws_ref)

        dov = dout_ref[...]
        dy = _nt(dov.astype(BF16), wout_v[...])
        du = dy * sv_ref[...].astype(F32)
        dsv = dy * u_ref[...].astype(F32)
        dsv_bf = dsv.astype(BF16)
        for ci in range(tm // CHUNK):
            rows = slice(ci * CHUNK, (ci + 1) * CHUNK)
            dbs_ref[...] += dsv[rows, :]
            for h in range(HEADS):
                cols = slice(h * GROUP, (h + 1) * GROUP)
                dvn_v[rows, cols] = _tn(wt[h], dsv_bf[rows, cols])
                dws_ref[h] += _nt(dsv_bf[rows, cols], vn_bf[rows, cols])
        dvn = dvn_v[...]
        dgain_ref[...] += _col_sum(dvn * vhat)
        dbias_ref[...] += _col_sum(dvn)
        dvhat = dvn * gain_v
        dv = rstd_ref[...] * (dvhat - _row_mean(dvhat) - vhat * _row_mean(dvhat * vhat))
        dzpre = (jnp.concatenate([du, dv], axis=1) * gd_ref[...].astype(F32)).astype(BF16)
        dz_ref[...] = dzpre
        dh = _nn(dzpre, win_v[...])
        dxr, dg_row = _rms_bwd(dh, xhat, r, gv)
        dg_ref[...] += dg_row
        dx = dov + dxr
        dx_ref[...] = dx
        dxb_ref[...] = dx.astype(BF16)

        @pl.when(i == n_steps - 1)
        def _():
            rr = lax.broadcasted_iota(jnp.int32, (CHUNK, CHUNK), 0)
            cc = lax.broadcasted_iota(jnp.int32, (CHUNK, CHUNK), 1)
            for h in range(HEADS):
                dws_ref[h] = jnp.where(rr >= cc, dws_ref[h], 0.0)
                dbso_ref[h] = jnp.sum(dbs_ref[:, h * GROUP:(h + 1) * GROUP], axis=1, keepdims=True)

    return _hosting_call(
        body, "mixer_a_bwd", n_steps, [dout, x, gd, u_sav, vhat_sav, sv_sav, rstd_sav, g, gain, bias, ws, gath],
        in_specs=[_row_spec(tm, d), _row_spec(tm, d), _row_spec(tm, e2), _row_spec(tm, e), _row_spec(tm, e),
                  _row_spec(tm, e), _row_spec(tm, 1), _const_spec((1, d)),
                  _const_spec((1, e)), _const_spec((1, e)), _const_spec((HEADS, CHUNK, CHUNK)), ANY],
        out_specs=[_row_spec(tm, d), _row_spec(tm, d), _row_spec(tm, d), _row_spec(tm, e2),
                   _const_spec((1, d)), _const_spec((1, e)), _const_spec((1, e)),
                   _const_spec((HEADS, CHUNK, CHUNK)), _const_spec((HEADS, CHUNK, 1))],
        out_shape=[jax.ShapeDtypeStruct((t_tok, d), F32), jax.ShapeDtypeStruct((t_tok, d), BF16),
                   jax.ShapeDtypeStruct((t_tok, d), BF16), jax.ShapeDtypeStruct((t_tok, e2), BF16),
                   jax.ShapeDtypeStruct((1, d), F32), jax.ShapeDtypeStruct((1, e), F32),
                   jax.ShapeDtypeStruct((1, e), F32), jax.ShapeDtypeStruct((HEADS, CHUNK, CHUNK), F32),
                   jax.ShapeDtypeStruct((HEADS, CHUNK, 1), F32)],
        scratch=[pltpu.VMEM((e2, d), BF16), pltpu.VMEM((e, d), BF16), pltpu.VMEM((tm, e), F32),
                 pltpu.VMEM((CHUNK, e), F32), pltpu.SemaphoreType.DMA((2 * N_DEV,))],
        hosted=hosted)


def _ffn_fwd(x, g, srcs, nf, tm, name, hosted=(), head=None):
    t_tok, d = x.shape
    f = nf * N_DEV
    firsts = [first for _, first in srcs]
    n_head = 2 if head else 0

    def body(*refs):
        x_ref, g_ref, sg_ref, su_ref, sd_ref = refs[:5]
        gate_ref, up_ref, wg_v, wu_v, wd_v, sems = refs[-6:]
        _load_group(
            [(sg_ref, firsts[0], nf, wg_v), (su_ref, firsts[1], nf, wu_v), (sd_ref, firsts[2], nf, wd_v)], sems)
        if head:
            t_ref, gf_ref, loss_ref, dx_ref, dxb_ref, dgf_ref = refs[5:11]

            @pl.when(pl.program_id(0) == 0)
            def _():
                loss_ref[...] = jnp.zeros_like(loss_ref)
                dgf_ref[...] = jnp.zeros_like(dgf_ref)

        xv = x_ref[...]
        h = _rms_fwd(xv, g_ref[...])[0].astype(BF16)
        gate = _nt(h, wg_v[...])
        up = _nt(h, wu_v[...])
        gate_ref[...] = gate.astype(BF16)
        up_ref[...] = up.astype(BF16)
        act = (gate * _sigmoid(gate) * up).astype(BF16)
        xo = xv + _nn(act, wd_v[...])
        if head:
            gfv = gf_ref[...]
            y, xhat, r = _rms_fwd(xo, gfv)
            err = y - t_ref[...]
            loss_ref[...] += 0.5 * jnp.sum(_row_mean(err * err), axis=0, keepdims=True)
            dxr, dg_row = _rms_bwd(err * (1.0 / d), xhat, r, gfv)
            dgf_ref[...] += dg_row
            dx_ref[...] = dxr
            dxb_ref[...] = dxr.astype(BF16)
        else:
            refs[5][...] = xo

    act_specs = [_row_spec(tm, f), _row_spec(tm, f)]
    act_shapes = [jax.ShapeDtypeStruct((t_tok, f), BF16), jax.ShapeDtypeStruct((t_tok, f), BF16)]
    if head:
        out_specs = [_const_spec((1, 1)), _row_spec(tm, d), _row_spec(tm, d), _const_spec((1, d))]
        out_shape = [jax.ShapeDtypeStruct((1, 1), F32), jax.ShapeDtypeStruct((t_tok, d), F32),
                     jax.ShapeDtypeStruct((t_tok, d), BF16), jax.ShapeDtypeStruct((1, d), F32)]
    else:
        out_specs = [_row_spec(tm, d)]
        out_shape = [jax.ShapeDtypeStruct((t_tok, d), F32)]
    return _hosting_call(
        body, name, t_tok // tm, [x, g] + [arr for arr, _ in srcs] + list(head or ()),
        in_specs=[_row_spec(tm, d), _const_spec((1, d)), ANY, ANY, ANY] + [_row_spec(tm, d), _const_spec((1, d))][:n_head],
        out_specs=out_specs + act_specs, out_shape=out_shape + act_shapes,
        scratch=[pltpu.VMEM((f, d), BF16), pltpu.VMEM((f, d), BF16), pltpu.VMEM((f, d), BF16),
                 pltpu.SemaphoreType.DMA((3 * N_DEV,))],
        hosted=hosted)


def _ffn_bwd(dout, x, gate, up, g, srcs, nf, tm, name, hosted=()):
    t_tok, d = x.shape
    f = nf * N_DEV
    firsts = [first for _, first in srcs]
    per_chunk = -(-f // (FFN_CHUNKS * MXU_WIDTH)) * MXU_WIDTH
    bounds = [min(ck * per_chunk, f) for ck in range(FFN_CHUNKS + 1)]

    def body(dout_ref, x_ref, gate_ref, up_ref, g_ref, sg_ref, su_ref, sd_ref,
             dx_ref, dxb_ref, h_ref, act_ref, dgu_ref, dg_ref, wg_v, wu_v, wd_v, sems):
        _load_group(
            [(sg_ref, firsts[0], nf, wg_v), (su_ref, firsts[1], nf, wu_v), (sd_ref, firsts[2], nf, wd_v)], sems)

        @pl.when(pl.program_id(0) == 0)
        def _():
            dg_ref[...] = jnp.zeros_like(dg_ref)

        xv = x_ref[...]
        gv = g_ref[...]
        hv, xhat, r = _rms_fwd(xv, gv)
        h_ref[...] = hv.astype(BF16)
        dov = dout_ref[...]
        dob = dov.astype(BF16)
        dh = None
        for ck in range(FFN_CHUNKS):
            cols = slice(bounds[ck], bounds[ck + 1])
            gate_v = gate_ref[:, cols].astype(F32)
            up_v = up_ref[:, cols].astype(F32)
            sig = _sigmoid(gate_v)
            silu = gate_v * sig
            act_ref[:, cols] = (silu * up_v).astype(BF16)
            dact = _nt(dob, wd_v[cols, :])
            dup = (dact * silu).astype(BF16)
            dgate = (dact * up_v * (sig * (1.0 + gate_v * (1.0 - sig)))).astype(BF16)
            dgu_ref[:, cols] = dgate
            dgu_ref[:, f + bounds[ck]:f + bounds[ck + 1]] = dup
            part = _nn(dgate, wg_v[cols, :]) + _nn(dup, wu_v[cols, :])
            dh = part if dh is None else dh + part
        dxr, dg_row = _rms_bwd(dh, xhat, r, gv)
        dg_ref[...] += dg_row
        dx = dov + dxr
        dx_ref[...] = dx
        dxb_ref[...] = dx.astype(BF16)

    return _hosting_call(
        body, name, t_tok // tm, [dout, x, gate, up, g] + [arr for arr, _ in srcs],
        in_specs=[_row_spec(tm, d), _row_spec(tm, d), _row_spec(tm, f), _row_spec(tm, f), _const_spec((1, d)),
                  ANY, ANY, ANY],
        out_specs=[_row_spec(tm, d), _row_spec(tm, d), _row_spec(tm, d), _row_spec(tm, f), _row_spec(tm, 2 * f),
                   _const_spec((1, d))],
        out_shape=[jax.ShapeDtypeStruct((t_tok, d), F32), jax.ShapeDtypeStruct((t_tok, d), BF16),
                   jax.ShapeDtypeStruct((t_tok, d), BF16), jax.ShapeDtypeStruct((t_tok, f), BF16),
                   jax.ShapeDtypeStruct((t_tok, 2 * f), BF16), jax.ShapeDtypeStruct((1, d), F32)],
        scratch=[pltpu.VMEM((f, d), BF16), pltpu.VMEM((f, d), BF16), pltpu.VMEM((f, d), BF16),
                 pltpu.SemaphoreType.DMA((3 * N_DEV,))],
        hosted=hosted)


def _shift_down(z, k, prev_rows):
    row = lax.broadcasted_iota(jnp.int32, z.shape, 0)
    out = pltpu.roll(z, k, 0)
    for j in range(k):
        out = jnp.where(row == j, prev_rows[j], out)
    return out


def _shift_up(z, k, next_rows):
    tm = z.shape[0]
    row = lax.broadcasted_iota(jnp.int32, z.shape, 0)
    out = pltpu.roll(z, tm - k, 0)
    for j in range(k):
        out = jnp.where(row == tm - k + j, next_rows[j], out)
    return out


def _mixer_b_fwd(x, g, gath, conv_w, tm, seq, hosted=()):
    t_tok, d = x.shape
    e = conv_w.shape[1]
    e3 = 3 * e
    n_in, n_out = e3 // N_DEV, e // N_DEV
    tiles_per_seq = seq // tm

    def body(x_ref, g_ref, cw_ref, gath_ref, xo_ref, p_ref, win_v, wout_v, tail_v, sems):
        i = pl.program_id(0)
        _load_group([(gath_ref, 0, n_in, win_v), (gath_ref, n_in, n_out, wout_v)], sems)

        @pl.when(i % tiles_per_seq == 0)
        def _():
            tail_v[...] = jnp.zeros_like(tail_v)

        xv = x_ref[...]
        h = _rms_fwd(xv, g_ref[...])[0].astype(BF16)
        p = _nt(h, win_v[...])
        p_ref[...] = p.astype(BF16)
        z = p[:, e:2 * e] * p[:, 2 * e:]
        prev = [tail_v[SUBLANES - 2:SUBLANES - 1, :], tail_v[SUBLANES - 1:SUBLANES, :]]
        conv = (cw_ref[2:3, :] * z + cw_ref[1:2, :] * _shift_down(z, 1, prev[1:])
                + cw_ref[0:1, :] * _shift_down(z, 2, prev))
        tail_v[...] = z[tm - SUBLANES:, :]
        y = (p[:, :e] * conv).astype(BF16)
        xo_ref[...] = xv + _nn(y, wout_v[...])

    return _hosting_call(
        body, "mixer_b_fwd", t_tok // tm, [x, g, conv_w, gath],
        in_specs=[_row_spec(tm, d), _const_spec((1, d)), _const_spec((SUBLANES, e)), ANY],
        out_specs=[_row_spec(tm, d), _row_spec(tm, e3)],
        out_shape=[jax.ShapeDtypeStruct((t_tok, d), F32), jax.ShapeDtypeStruct((t_tok, e3), BF16)],
        scratch=[pltpu.VMEM((e3, d), BF16), pltpu.VMEM((e, d), BF16), pltpu.VMEM((SUBLANES, e), F32),
                 pltpu.SemaphoreType.DMA((2 * N_DEV,))],
        hosted=hosted)


def _mixer_b_bwd(dout, x, p, g, gath, conv_w, tm, seq, hosted=()):
    t_tok, d = x.shape
    e = conv_w.shape[1]
    e3 = 3 * e
    n_in, n_out = e3 // N_DEV, e // N_DEV
    tiles_per_seq = seq // tm
    halo_per_tile = tm // HALO
    n_halo = t_tok // HALO

    def body(dout_ref, dnext_ref, x_ref, p_ref, pprev_ref, pnext_ref, g_ref, cw_ref, gath_ref,
             dx_ref, dxb_ref, h_ref, y_ref, dp_ref, dg_ref, dcw_ref, win_v, wout_v, sems):
        i = pl.program_id(0)
        _load_group([(gath_ref, 0, n_in, win_v), (gath_ref, n_in, n_out, wout_v)], sems)

        @pl.when(i == 0)
        def _():
            dg_ref[...] = jnp.zeros_like(dg_ref)
            dcw_ref[...] = jnp.zeros_like(dcw_ref)

        first = (i % tiles_per_seq == 0).astype(F32)
        last = (i % tiles_per_seq == tiles_per_seq - 1).astype(F32)
        xv = x_ref[...]
        gv = g_ref[...]
        hv, xhat, r = _rms_fwd(xv, gv)
        h_ref[...] = hv.astype(BF16)
        pv = p_ref[...].astype(F32)
        bg, cg, hx = pv[:, :e], pv[:, e:2 * e], pv[:, 2 * e:]
        z = cg * hx
        pprev = pprev_ref[...].astype(F32)
        zprev = pprev[:, e:2 * e] * pprev[:, 2 * e:] * (1.0 - first)
        prev = [zprev[HALO - 2:HALO - 1, :], zprev[HALO - 1:HALO, :]]
        zs1 = _shift_down(z, 1, prev[1:])
        zs2 = _shift_down(z, 2, prev)
        w0, w1, w2 = cw_ref[0:1, :], cw_ref[1:2, :], cw_ref[2:3, :]
        conv = w2 * z + w1 * zs1 + w0 * zs2
        y_ref[...] = (bg * conv).astype(BF16)

        dov = dout_ref[...]
        wout_bf = wout_v[...]
        dy = _nt(dov.astype(BF16), wout_bf)
        dconv = dy * bg
        dnext = _nt(dnext_ref[...].astype(BF16), wout_bf) * pnext_ref[:, :e].astype(F32) * (1.0 - last)
        nxt = [dnext[0:1, :], dnext[1:2, :]]
        dz = w2 * dconv + w1 * _shift_up(dconv, 1, nxt[:1]) + w0 * _shift_up(dconv, 2, nxt)
        dcw_ref[0:1, :] += _col_sum(dconv * zs2)
        dcw_ref[1:2, :] += _col_sum(dconv * zs1)
        dcw_ref[2:3, :] += _col_sum(dconv * z)
        dp = jnp.concatenate([dy * conv, dz * hx, dz * cg], axis=1).astype(BF16)
        dp_ref[...] = dp
        dh = _nn(dp, win_v[...])
        dxr, dg_row = _rms_bwd(dh, xhat, r, gv)
        dg_ref[...] += dg_row
        dx = dov + dxr
        dx_ref[...] = dx
        dxb_ref[...] = dx.astype(BF16)

    prev_spec = lambda w: pl.BlockSpec((HALO, w), lambda i: (jnp.maximum(i * halo_per_tile - 1, 0), 0))
    next_spec = lambda w: pl.BlockSpec((HALO, w), lambda i: (jnp.minimum((i + 1) * halo_per_tile, n_halo - 1), 0))
    return _hosting_call(
        body, "mixer_b_bwd", t_tok // tm, [dout, dout, x, p, p, p, g, conv_w, gath],
        in_specs=[_row_spec(tm, d), next_spec(d), _row_spec(tm, d), _row_spec(tm, e3), prev_spec(e3), next_spec(e3),
                  _const_spec((1, d)), _const_spec((SUBLANES, e)), ANY],
        out_specs=[_row_spec(tm, d), _row_spec(tm, d), _row_spec(tm, d), _row_spec(tm, e), _row_spec(tm, e3),
                   _const_spec((1, d)), _const_spec((SUBLANES, e))],
        out_shape=[jax.ShapeDtypeStruct((t_tok, d), F32), jax.ShapeDtypeStruct((t_tok, d), BF16),
                   jax.ShapeDtypeStruct((t_tok, d), BF16), jax.ShapeDtypeStruct((t_tok, e), BF16),
                   jax.ShapeDtypeStruct((t_tok, e3), BF16), jax.ShapeDtypeStruct((1, d), F32),
                   jax.ShapeDtypeStruct((SUBLANES, e), F32)],
        scratch=[pltpu.VMEM((e3, d), BF16), pltpu.VMEM((e, d), BF16), pltpu.SemaphoreType.DMA((2 * N_DEV,))],
        hosted=hosted)


def _wgrad(a, b, bm, name, hosted=()):
    t_tok, m = a.shape
    n = b.shape[1]

    def body(a_ref, b_ref, o_ref):
        o_ref[...] = _tn(a_ref[...], b_ref[...]).astype(o_ref.dtype)

    outs, h_outs = _hosting_call(
        body, name, m // bm, [a, b],
        in_specs=[pl.BlockSpec((t_tok, bm), lambda i: (0, i)), _const_spec((t_tok, n))],
        out_specs=[pl.BlockSpec((bm, n), lambda i: (i, 0))],
        out_shape=[jax.ShapeDtypeStruct((m, n), BF16)],
        scratch=[], hosted=hosted)
    return (outs[0], h_outs) if hosted else outs[0]


def _sum_slots(land, rb, name):
    n_slots, rows, cols = land.shape

    def body(l_ref, o_ref):
        acc = l_ref[0].astype(F32)
        for k in range(1, n_slots):
            acc = acc + l_ref[k].astype(F32)
        o_ref[...] = acc

    return pl.pallas_call(
        body, name=name, grid=(rows // rb,),
        in_specs=[pl.BlockSpec((n_slots, rb, cols), lambda i: (0, i, 0))],
        out_specs=pl.BlockSpec((rb, cols), lambda i: (i, 0)),
        out_shape=jax.ShapeDtypeStruct((rows, cols), F32),
        compiler_params=_params(sequential=False),
    )(land)


def _adamw(w, grad, m, v, rb, name):
    rows, cols = w.shape
    c1 = 1.0 / (1.0 - ADAM_B1 ** ADAM_STEP)
    c2 = 1.0 / (1.0 - ADAM_B2 ** ADAM_STEP)

    def body(w_ref, g_ref, m_ref, v_ref, d_ref, mo_ref, vo_ref):
        gv = g_ref[...]
        mn = ADAM_B1 * m_ref[...] + (1.0 - ADAM_B1) * gv
        vn = ADAM_B2 * v_ref[...] + (1.0 - ADAM_B2) * (gv * gv)
        mo_ref[...] = mn
        vo_ref[...] = vn
        d_ref[...] = -ADAM_LR * ((mn * c1) / (jnp.sqrt(vn * c2) + ADAM_EPS) + ADAM_WD * w_ref[...])

    spec = pl.BlockSpec((rb, cols), lambda i: (i, 0))
    shape = jax.ShapeDtypeStruct((rows, cols), F32)
    return pl.pallas_call(
        body, name=name, grid=(rows // rb,),
        in_specs=[spec] * 4, out_specs=[spec] * 3, out_shape=[shape] * 3,
        compiler_params=_params(sequential=False),
    )(w, grad, m, v)


def _pack_shards(groups, name, hosted=()):
    flat = [(part, layer) for group in groups for part, layer, _ in group]
    rows = [[p.shape[2] if turn else p.shape[1] for p, _, turn in group] for group in groups]
    first, _, first_turn = groups[0][0]
    width = first.shape[1] if first_turn else first.shape[2]

    def body(*refs):
        ins, outs = refs[:len(flat)], refs[len(flat):]
        k = 0
        for gi, group in enumerate(groups):
            off = 0
            for (_, _, turn), n in zip(group, rows[gi]):
                part = ins[k][...].astype(BF16)
                if turn:
                    r = lax.broadcasted_iota(jnp.int32, (n, n), 0)
                    c = lax.broadcasted_iota(jnp.int32, (n, n), 1)
                    part = _nt((r == c).astype(BF16), part).astype(BF16)
                outs[gi][off:off + n, :] = part
                off += n
                k += 1

    return _hosting_call(
        body, name, 1, [p for p, _ in flat],
        in_specs=[pl.BlockSpec((None,) + p.shape[1:], lambda i, layer=layer: (layer, 0, 0)) for p, layer in flat],
        out_specs=[_const_spec((sum(r), width)) for r in rows],
        out_shape=[jax.ShapeDtypeStruct((sum(r), width), BF16) for r in rows],
        scratch=[], hosted=hosted)


def _split_bf16(a):
    hi = a.astype(BF16)
    rest = a - hi.astype(F32)
    mid = rest.astype(BF16)
    return hi, mid, (rest - mid.astype(F32)).astype(BF16)


def _reduce_adamw(lands, w, m, v, transpose, name, hosted=()):
    n_layers, rows_w, cols_w = w.shape
    c1 = 1.0 / (1.0 - ADAM_B1 ** ADAM_STEP)
    c2 = 1.0 / (1.0 - ADAM_B2 ** ADAM_STEP)
    flat = [piece for pieces in lands for piece in pieces]
    counts = [len(pieces) for pieces in lands]
    if transpose:
        tiles = rows_w // MXU_WIDTH
        blk = (MXU_WIDTH, cols_w)
        land_specs = [pl.BlockSpec((N_CHIP, n, MXU_WIDTH), lambda i, b=first // n: (0, b, i % tiles))
                      for _, first, n in flat]
        for _, first, n in flat:
            assert first % n == 0
    else:
        tiles = 2
        blk = (rows_w // tiles, cols_w)
        assert all(c == 1 for c in counts)
        land_specs = [pl.BlockSpec((N_CHIP,) + blk, lambda i, b=first // blk[0]: (0, b + i % tiles, 0))
                      for _, first, _ in flat]
        for _, first, _ in flat:
            assert first % blk[0] == 0

    def body(*refs):
        land_refs = refs[:len(flat)]
        w_ref, m_ref, v_ref, g_ref, d_ref, mo_ref, vo_ref = refs[len(flat):]
        layer = pl.program_id(0) // tiles

        def total(ref):
            acc = ref[0].astype(F32)
            for q in range(1, N_CHIP):
                acc = acc + ref[q].astype(F32)
            return acc

        def layer_sum(k):
            first = sum(counts[:k])
            parts = [total(land_refs[first + j]) for j in range(counts[k])]
            return parts[0] if len(parts) == 1 else jnp.concatenate(parts, axis=0)

        gv = layer_sum(0)
        for k in range(1, n_layers):
            gv = jnp.where(layer == k, layer_sum(k), gv)
        if transpose:
            r = lax.broadcasted_iota(jnp.int32, (MXU_WIDTH, MXU_WIDTH), 0)
            c = lax.broadcasted_iota(jnp.int32, (MXU_WIDTH, MXU_WIDTH), 1)
            eye = (r == c).astype(BF16)
            hi, mid, lo = _split_bf16(gv)
            gv = _nt(eye, hi) + _nt(eye, mid) + _nt(eye, lo)
        g_ref[...] = gv
        mn = ADAM_B1 * m_ref[...] + (1.0 - ADAM_B1) * gv
        vn = ADAM_B2 * v_ref[...] + (1.0 - ADAM_B2) * (gv * gv)
        mo_ref[...] = mn
        vo_ref[...] = vn
        d_ref[...] = -ADAM_LR * ((mn * c1) / (jnp.sqrt(vn * c2) + ADAM_EPS) + ADAM_WD * w_ref[...])

    spec = pl.BlockSpec((None,) + blk, lambda i: (i // tiles, i % tiles, 0))
    shape = jax.ShapeDtypeStruct(w.shape, F32)
    outs, h_outs = _hosting_call(
        body, name, n_layers * tiles, [land for land, _, _ in flat] + [w, m, v],
        in_specs=land_specs + [spec] * 3, out_specs=[spec] * 4, out_shape=[shape] * 4, scratch=[], hosted=hosted)
    return (outs, h_outs) if hosted else outs


def _pack_small(parts, rows):
    flat = jnp.concatenate([p.reshape(-1).astype(F32) for p in parts])
    return jnp.pad(flat, (0, rows * LANES - flat.shape[0])).reshape(rows, LANES)


def _unpack_small(packed, shapes):
    flat = packed.reshape(-1)
    out = []
    pos = 0
    for s in shapes:
        n = math.prod(s)
        out.append(flat[pos:pos + n].reshape(s))
        pos += n
    return out


def kernel(x, mix_norm, ffn_norm, a_w_in, a_v_gain, a_v_bias, a_w_s, a_b_s, a_w_out, b_w_in, b_conv_w, b_w_out, ffn_w_gate, ffn_w_up, ffn_w_down, final_norm, loss_target, m_mix_norm, m_ffn_norm, m_a_w_in, m_a_v_gain, m_a_v_bias, m_a_w_s, m_a_b_s, m_a_w_out, m_b_w_in, m_b_conv_w, m_b_w_out, m_ffn_w_gate, m_ffn_w_up, m_ffn_w_down, m_final_norm, v_mix_norm, v_ffn_norm, v_a_w_in, v_a_v_gain, v_a_v_bias, v_a_w_s, v_a_b_s, v_a_w_out, v_b_w_in, v_b_conv_w, v_b_w_out, v_ffn_w_gate, v_ffn_w_up, v_ffn_w_down, v_final_norm):
    bsz, seq, d = x.shape
    t_tok = bsz * seq
    me = _my_index()
    xt = x.reshape(t_tok, d)
    target = loss_target.reshape(t_tok, d)
    e_a = a_v_gain.shape[1]
    e_b = b_w_out.shape[1] * N_DEV
    n_layers = ffn_w_gate.shape[0]
    f_shard = ffn_w_gate.shape[2]
    f_full = f_shard * N_DEV

    conv_pad = jnp.pad(b_conv_w[0], ((0, SUBLANES - CONV_W), (0, 0)))
    sh_a = jnp.concatenate([a_w_in[0].T, a_w_out[0]]).astype(BF16)
    bfull = jnp.repeat(a_b_s[0].T, GROUP, axis=1)

    gate_t, up_t = ffn_w_gate.transpose(0, 2, 1), ffn_w_up.transpose(0, 2, 1)
    (sh_b, sh_f0, sh_f1g, sh_f1ud), (gath_a, conv_g) = _pack_shards(
        [[(b_w_in, 0, True), (b_w_out, 0, False)],
         [(gate_t, 0, False), (up_t, 0, False), (ffn_w_down, 0, False)],
         [(gate_t, 1, False)],
         [(up_t, 1, False), (ffn_w_down, 1, False)]],
        "pack_shards", hosted=[_HostedGathers([sh_a, conv_pad], 0)])
    conv_full = jnp.pad(conv_g[:, :CONV_W, :].transpose(1, 0, 2).reshape(CONV_W, e_b), ((0, SUBLANES - CONV_W), (0, 0)))
    (x1, gd_a, u_a, vhat_a, sv_a, y_a, rstd_a), (gath_f0,) = _mixer_a_fwd(
        xt, mix_norm[0:1], gath_a, a_v_gain, a_v_bias, a_w_s[0], bfull, tm=256,
        hosted=[_HostedGathers([sh_f0], mid_lead=2)])
    srcs0 = [(gath_f0, 0), (gath_f0, f_shard), (gath_f0, 2 * f_shard)]
    (x2, gate0, up0), (gath_b, gath_f1g) = _ffn_fwd(x1, ffn_norm[0:1], srcs0, f_shard, tm=256, name="ffn_fwd0",
                                                    hosted=[_HostedGathers([sh_b, sh_f1g], mid_lead=2)])
    (x3, p_b), (gath_f1ud,) = _mixer_b_fwd(x2, mix_norm[1:2], gath_b, conv_full, tm=256, seq=seq,
                                           hosted=[_HostedGathers([sh_f1ud], mid_lead=2)])
    srcs1 = [(gath_f1g, 0), (gath_f1ud, 0), (gath_f1ud, f_shard)]
    (loss_part, dx4, dx4_bf, d_final, gate1, up1), _ = _ffn_fwd(
        x3, ffn_norm[1:2], srcs1, f_shard, tm=256, name="ffn_fwd1", head=(target, final_norm.reshape(1, d)))

    ffn_entries = [(0, 0, f_shard), (0, f_full, f_shard), (1, 0, f_shard)]
    (dx3, dx3_bf, h_f1, act1, dgu1, d_fn1), _ = _ffn_bwd(dx4, x3, gate1, up1, ffn_norm[1:2], srcs1, f_shard, tm=256,
                                                         name="ffn_bwd1")
    g_down1 = _wgrad(act1, dx4_bf, 256, "wgrad_down1")
    g_gu1 = _wgrad(dgu1, h_f1, 512, "wgrad_gate_up1")
    ps_f1 = _pair_reduce([g_gu1, g_down1], ffn_entries, "pair_reduce_f1")
    (dx2, dx2_bf, h_b, y_b, dp_b, d_mn1, d_conv), (land_f1gu,) = _mixer_b_bwd(
        dx3, x2, p_b, mix_norm[1:2], gath_b, conv_full, tm=256, seq=seq,
        hosted=[_HostedChipScatter(ps_f1, 0, 2 * f_shard)])
    g_b_out = _wgrad(y_b, dx3_bf, 256, "wgrad_b_out")
    g_b_in = _wgrad(dp_b, h_b, 512, "wgrad_b_in")
    ps_b = _pair_reduce([g_b_in, g_b_out], [(0, 0, b_w_in.shape[2]), (1, 0, b_w_out.shape[1])], "pair_reduce_b")
    (dx1, dx1_bf, h_f0, act0, dgu0, d_fn0), (land_f1d, land_b) = _ffn_bwd(
        dx2, x1, gate0, up0, ffn_norm[0:1], srcs0, f_shard, tm=256, name="ffn_bwd0",
        hosted=[_HostedChipScatter(ps_f1, 2 * f_shard, f_shard), _HostedChipScatter(ps_b)])
    g_down0 = _wgrad(act0, dx2_bf, 256, "wgrad_down0")
    g_gu0 = _wgrad(dgu0, h_f0, 512, "wgrad_gate_up0")
    ps_f0 = _pair_reduce([g_gu0, g_down0], ffn_entries, "pair_reduce_f0")
    g_a_out = _wgrad(y_a, dx1_bf, 256, "wgrad_a_out")
    ps_ao = _pair_reduce([g_a_out], [(0, 0, a_w_out.shape[1])], "pair_reduce_a_out")
    (dx0, _, h_a, dz_a, d_mn0, d_gain, d_bias, d_ws, d_bs_acc), (land_f0, land_ao) = _mixer_a_bwd(
        dx1, xt, gd_a, u_a, vhat_a, sv_a, rstd_a, mix_norm[0:1], gath_a, a_v_gain, a_v_bias, a_w_s[0], tm=256,
        hosted=[_HostedChipScatter(ps_f0), _HostedChipScatter(ps_ao)])
    d_bs = d_bs_acc.reshape(HEADS, CHUNK)

    small_grads = [jnp.concatenate([d_mn0, d_mn1]), jnp.concatenate([d_fn0, d_fn1]), d_gain, d_bias, d_ws, d_bs,
                   d_final, d_conv[:CONV_W], loss_part]
    small_shapes = [(n_layers, d), (n_layers, d), (1, e_a), (1, e_a), (1, HEADS, CHUNK, CHUNK), (1, HEADS, CHUNK), (d,),
                    (CONV_W, e_b), ()]
    n_small = sum(math.prod(s) for s in small_shapes)
    blk_rows = -(-n_small // (N_DEV * LANES * SUBLANES)) * SUBLANES
    small_rows = blk_rows * N_DEV
    packed = _pack_small(small_grads, small_rows)
    g_a_in, (small_land,) = _wgrad(dz_a, h_a, 512, "wgrad_a_in", hosted=[_HostedScatterAll(packed)])
    ps_ai = _pair_reduce([g_a_in], [(0, 0, a_w_in.shape[2])], "pair_reduce_a_in")
    small_sum = _sum_slots(small_land, blk_rows, "sum_small")

    land_ai, small_gath = _exchange([_HostedChipScatter(ps_ai), _HostedGathers([small_sum], 0)], "tail_exchange")
    small_all = small_gath.reshape(small_rows, LANES)

    n_b_in = b_w_in.shape[2]
    gate_out = _reduce_adamw([[(land_f0, 0, f_shard)], [(land_f1gu, 0, f_shard)]], gate_t,
                             m_ffn_w_gate.transpose(0, 2, 1), v_ffn_w_gate.transpose(0, 2, 1), False, "adamw_gate")
    up_out = _reduce_adamw([[(land_f0, f_shard, f_shard)], [(land_f1gu, f_shard, f_shard)]], up_t,
                           m_ffn_w_up.transpose(0, 2, 1), v_ffn_w_up.transpose(0, 2, 1), False, "adamw_up")
    res = {
        "a_w_in": _reduce_adamw([[(land_ai, 0, a_w_in.shape[2])]], a_w_in, m_a_w_in, v_a_w_in, True, "adamw_a_in"),
        "a_w_out": _reduce_adamw([[(land_ao, 0, a_w_out.shape[1])]], a_w_out, m_a_w_out, v_a_w_out, False,
                                 "adamw_a_out"),
        "b_w_in": _reduce_adamw([[(land_b, 0, n_b_in)]], b_w_in, m_b_w_in, v_b_w_in, True, "adamw_b_in"),
        "b_w_out": _reduce_adamw([[(land_b, n_b_in, b_w_out.shape[1])]], b_w_out, m_b_w_out, v_b_w_out, False,
                                 "adamw_b_out"),
        "ffn_w_gate": [o.transpose(0, 2, 1) for o in gate_out],
        "ffn_w_up": [o.transpose(0, 2, 1) for o in up_out],
        "ffn_w_down": _reduce_adamw([[(land_f0, 2 * f_shard, f_shard)], [(land_f1d, 0, f_shard)]], ffn_w_down,
                                    m_ffn_w_down, v_ffn_w_down, False, "adamw_down"),
    }

    (gr_mix, gr_ffn, gr_gain, gr_bias, gr_ws, gr_bs, gr_final, gr_conv_full, loss) = _unpack_small(small_all, small_shapes)
    gr_conv = lax.dynamic_slice_in_dim(gr_conv_full, me * (e_b // N_DEV), e_b // N_DEV, axis=1)[None]

    small_w =[mix_norm, ffn_norm, a_v_gain, a_v_bias, a_w_s, a_b_s, final_norm]
    small_m = [m_mix_norm, m_ffn_norm, m_a_v_gain, m_a_v_bias, m_a_w_s, m_a_b_s, m_final_norm]
    small_v = [v_mix_norm, v_ffn_norm, v_a_v_gain, v_a_v_bias, v_a_w_s, v_a_b_s, v_final_norm]
    small_g = [gr_mix, gr_ffn, gr_gain, gr_bias, gr_ws, gr_bs, gr_final]
    sm_shapes = small_shapes[:len(small_w)]
    sm_out = _adamw(_pack_small(small_w, small_rows), _pack_small(small_g, small_rows), _pack_small(small_m, small_rows),
                    _pack_small(small_v, small_rows), small_rows, "adamw_small")
    sm_delta, sm_m, sm_v = [_unpack_small(o, sm_shapes) for o in sm_out]

    conv_out = _adamw(b_conv_w[0], gr_conv[0], m_b_conv_w[0], v_b_conv_w[0], CONV_W, "adamw_conv")
    conv_delta, conv_m, conv_v = [o[None] for o in conv_out]

    order = ["mix_norm", "ffn_norm", "a_w_in", "a_v_gain", "a_v_bias", "a_w_s", "a_b_s", "a_w_out", "b_w_in",
             "b_conv_w", "b_w_out", "ffn_w_gate", "ffn_w_up", "ffn_w_down", "final_norm"]
    small_names = ["mix_norm", "ffn_norm", "a_v_gain", "a_v_bias", "a_w_s", "a_b_s", "final_norm"]
    grads = {"b_conv_w": gr_conv}
    deltas, new_m, new_v = {}, {}, {}
    for k, name in enumerate(small_names):
        grads[name] = small_g[k]
        deltas[name], new_m[name], new_v[name] = sm_delta[k], sm_m[k], sm_v[k]
    deltas["b_conv_w"], new_m["b_conv_w"], new_v["b_conv_w"] = conv_delta, conv_m, conv_v
    for name, (gg, dl, mm, vv) in res.items():
        grads[name], deltas[name], new_m[name], new_v[name] = gg, dl, mm, vv

    grad_x = dx0.reshape(bsz, seq, d)
    return (loss, grad_x, *[grads[n] for n in order], *[deltas[n] for n in order],
            *[new_m[n] for n in order], *[new_v[n] for n in order])
```

```python
import math

import jax
import jax.numpy as jnp
from jax import lax
from jax.experimental import pallas as pl
from jax.experimental.pallas import tpu as pltpu

F32 = jnp.float32
BF16 = jnp.bfloat16

N_DEV = 8
N_CHIP = 4
CHUNK = 128
HEADS = 16
GROUP = 128
CONV_W = 3
NORM_EPS = 1e-6
GELU_C = math.sqrt(2.0 / math.pi)
GELU_K = 0.044715

ADAM_LR = 0.001
ADAM_B1 = 0.9
ADAM_B2 = 0.999
ADAM_EPS = 1e-08
ADAM_WD = 0.01
ADAM_STEP = 10

LANES = 128
SUBLANES = 8
VMEM_LIMIT = 60 * 1024 * 1024
HALO = 16
MXU_WIDTH = 256
FFN_CHUNKS = 2

MESH = pl.DeviceIdType.MESH
ANY = pl.BlockSpec(memory_space=pl.ANY)


def _params(sequential=True):
    return pltpu.CompilerParams(
        dimension_semantics=("arbitrary",) if sequential else None,
        vmem_limit_bytes=VMEM_LIMIT)


def _nn(a, b):
    return jnp.dot(a, b, preferred_element_type=F32)


def _nt(a, b):
    return lax.dot_general(a, b, (((1,), (1,)), ((), ())), preferred_element_type=F32)


def _tn(a, b):
    return lax.dot_general(a, b, (((0,), (0,)), ((), ())), preferred_element_type=F32)


def _row_mean(a):
    return jnp.mean(a, axis=-1, keepdims=True)


def _col_sum(a):
    return jnp.sum(a, axis=0, keepdims=True)


def _rms_fwd(x, g):
    r = lax.rsqrt(_row_mean(x * x) + NORM_EPS)
    xhat = x * r
    return xhat * g, xhat, r


def _rms_bwd(dh, xhat, r, g):
    a = dh * g
    dx = r * (a - xhat * _row_mean(a * xhat))
    return dx, _col_sum(dh * xhat)


def _gelu_and_grad(x):
    x2 = x * x
    t = jnp.tanh(x * (GELU_C + (GELU_C * GELU_K) * x2))
    half = 0.5 * t + 0.5
    d = half + x * (0.5 - 0.5 * (t * t)) * (GELU_C + (3.0 * GELU_C * GELU_K) * x2)
    return x * half, d


def _sigmoid(x):
    return 1.0 / (1.0 + jnp.exp(-x))


def _row_spec(tm, width):
    return pl.BlockSpec((tm, width), lambda i: (i, 0))


def _const_spec(shape):
    nd = len(shape)
    return pl.BlockSpec(shape, lambda i: (0,) * nd)


def _load_group(parts, sems):
    @pl.when(pl.program_id(0) == 0)
    def _():
        copies = []
        for k, (gath_ref, first, n, dst) in enumerate(parts):
            for j in range(N_DEV):
                copies.append(pltpu.make_async_copy(gath_ref.at[j, pl.ds(first, n), :], dst.at[pl.ds(j * n, n), :],
                                                    sems.at[k * N_DEV + j]))
        for cp in copies:
            cp.start()
        for cp in copies:
            cp.wait()


def _hosting_call(body, name, n_steps, arrays, in_specs, out_specs, out_shape, scratch, hosted=()):
    n_in, n_out, n_scr = len(arrays), len(out_shape), len(scratch)
    h_arrays = [a for h in hosted for a in h.arrays]
    h_shapes = [s for h in hosted for s in h.out_shapes]
    h_sems = [s for h in hosted for s in h.sem_shapes]

    def full_body(*refs):
        pos = 0
        groups = []
        for n in (n_in, len(h_arrays), n_out, len(h_shapes), n_scr, len(h_sems)):
            groups.append(refs[pos:pos + n])
            pos += n
        own_in, h_in, own_out, h_out, own_scr, h_sem = groups
        per_host = []
        pi = po = ps = 0
        for h in hosted:
            ni, no, ns = len(h.arrays), len(h.out_shapes), len(h.sem_shapes)
            per_host.append((h, h_in[pi:pi + ni], h_out[po:po + no], h_sem[ps:ps + ns]))
            pi, po, ps = pi + ni, po + no, ps + ns
        for h, ins, outs, sems in per_host:
            h.begin(ins, outs, sems, n_steps)
        body(*own_in, *own_out, *own_scr)
        for h, ins, outs, sems in per_host:
            h.end(ins, outs, sems, n_steps)

    outs = pl.pallas_call(
        full_body, name=name, grid=(n_steps,),
        in_specs=list(in_specs) + [ANY] * len(h_arrays),
        out_specs=list(out_specs) + [ANY] * len(h_shapes),
        out_shape=list(out_shape) + h_shapes,
        scratch_shapes=list(scratch) + h_sems,
        compiler_params=_params(),
    )(*arrays, *h_arrays)
    return outs[:n_out], outs[n_out:]


def _my_index():
    return 4 * lax.axis_index("x") + 2 * lax.axis_index("y") + lax.axis_index("c")


GATHER_COPIES = 8


class _Gather:
    def __init__(self, shard, out, send_sems, recv_sems, local_sem):
        self.shard, self.out = shard, out
        self.send_sems, self.recv_sems, self.local_sem = send_sems, recv_sems, local_sem
        x, y, c = lax.axis_index("x"), lax.axis_index("y"), lax.axis_index("c")
        self.c = c
        self.me, self.sibling = (x, y, c), (x, y, 1 - c)
        self.xn, self.yn, self.dg = (1 - x, y), (x, 1 - y), (1 - x, 1 - y)
        self.n = shard.shape[0]
        self.half = self.n // 2
        rows_per_tile = SUBLANES * 4 // shard.dtype.itemsize
        self.relays = self.n % 2 == 0 and self.half % rows_per_tile == 0

    def _slot(self, dev, lo=0, hi=None):
        hi = self.n if hi is None else hi
        return self.out.at[4 * dev[0] + 2 * dev[1] + dev[2], pl.ds(lo, hi - lo), :]

    def _copy(self, k, block, to, src=None, lo=0, hi=None):
        return pltpu.make_async_remote_copy(
            src_ref=self._slot(block, lo, hi) if src is None else src, dst_ref=self._slot(block, lo, hi),
            send_sem=self.send_sems.at[k], recv_sem=self.recv_sems.at[k], device_id=to, device_id_type=MESH)

    def _local(self):
        return pltpu.make_async_copy(self.shard, self._slot(self.me), self.local_sem)

    def start(self):
        c = self.c
        self._local().start()
        self._copy(0, self.me, self.sibling, src=self.shard).start()
        self._copy(1, self.me, (*self.xn, c), src=self.shard).start()
        self._copy(2, self.me, (*self.yn, c), src=self.shard).start()
        if not self.relays:
            self._copy(3, self.me, (*self.dg, c), src=self.shard).start()

    def relay(self):
        c = self.c
        if self.relays:
            self._copy(1, (*self.xn, c), self.me).wait_recv()
            self._copy(3, (*self.xn, c), (*self.yn, c), hi=self.half).start()
            self._copy(2, (*self.yn, c), self.me).wait_recv()
            self._copy(4, (*self.yn, c), (*self.xn, c), lo=self.half).start()

    def forward(self):
        c = self.c
        if self.relays:
            self._copy(5, (*self.xn, c), self.sibling).start()
            self._copy(6, (*self.yn, c), self.sibling).start()
            self._copy(3, (*self.dg, c), self.me, hi=self.half).wait_recv()
            self._copy(4, (*self.dg, c), self.me, lo=self.half).wait_recv()
        else:
            self._copy(1, (*self.xn, c), self.me).wait_recv()
            self._copy(5, (*self.xn, c), self.sibling).start()
            self._copy(2, (*self.yn, c), self.me).wait_recv()
            self._copy(6, (*self.yn, c), self.sibling).start()
            self._copy(3, (*self.dg, c), self.me).wait_recv()
        self._copy(7, (*self.dg, c), self.sibling).start()

    def finish(self):
        c = self.c
        self._copy(0, self.sibling, self.me).wait_recv()
        for k, chip in ((5, self.xn), (6, self.yn), (7, self.dg)):
            self._copy(k, (*chip, 1 - c), self.me).wait_recv()
        for k in (0, 1, 2, 5, 6, 7):
            self._copy(k, self.me, self.sibling).wait_send()
        if self.relays:
            self._copy(3, self.me, self.sibling, hi=self.half).wait_send()
            self._copy(4, self.me, self.sibling, lo=self.half).wait_send()
        else:
            self._copy(3, self.me, self.sibling).wait_send()
        self._local().wait()


class _HostedGathers:
    def __init__(self, shards, mid_lead, relay_at=0.56):
        n = len(shards)
        self.arrays = shards
        self.mid_lead, self.relay_at = mid_lead, relay_at
        self.out_shapes = [jax.ShapeDtypeStruct((N_DEV,) + s.shape, s.dtype) for s in shards]
        self.sem_shapes = [pltpu.SemaphoreType.DMA((n, GATHER_COPIES)), pltpu.SemaphoreType.DMA((n, GATHER_COPIES)),
                           pltpu.SemaphoreType.DMA((n,))]

    def _gathers(self, ins, outs, sems):
        return [_Gather(ins[a], outs[a], sems[0].at[a], sems[1].at[a], sems[2].at[a]) for a in range(len(ins))]

    def begin(self, ins, outs, sems, n_steps):
        i = pl.program_id(0)
        forward_step = max(n_steps - 1 - self.mid_lead, 0)
        relay_step = min(int(self.relay_at * n_steps), forward_step)

        @pl.when(i == 0)
        def _():
            for g in self._gathers(ins, outs, sems):
                g.start()

        if n_steps == 1:
            return

        @pl.when(i == relay_step)
        def _():
            for g in self._gathers(ins, outs, sems):
                g.relay()

        @pl.when(i == forward_step)
        def _():
            for g in self._gathers(ins, outs, sems):
                g.forward()

    def end(self, ins, outs, sems, n_steps):
        @pl.when(pl.program_id(0) == n_steps - 1)
        def _():
            gathers = self._gathers(ins, outs, sems)
            if n_steps == 1:
                for g in gathers:
                    g.relay()
                for g in gathers:
                    g.forward()
            for g in gathers:
                g.finish()


def _exchange(hosted, name):
    return _hosting_call(lambda: None, name, 1, [], [], [], [], [], hosted=hosted)[1]


class _ChipScatter:
    def __init__(self, pairsum, row0, land, send_sems, recv_sems, local_sem):
        self.pairsum, self.row0, self.land = pairsum, row0, land
        self.send_sems, self.recv_sems, self.local_sem = send_sems, recv_sems, local_sem
        x, y, c = lax.axis_index("x"), lax.axis_index("y"), lax.axis_index("c")
        self.c = c
        self.chip = 2 * x + y
        self.others = [(1 - x, y), (x, 1 - y), (1 - x, 1 - y)]

    def _src(self, chip):
        return self.pairsum.at[chip, pl.ds(self.row0, self.land.shape[1]), :]

    def _copy(self, k):
        ox, oy = self.others[k]
        return pltpu.make_async_remote_copy(
            src_ref=self._src(2 * ox + oy), dst_ref=self.land.at[self.chip],
            send_sem=self.send_sems.at[k], recv_sem=self.recv_sems.at[k], device_id=(ox, oy, self.c),
            device_id_type=MESH)

    def _arrival(self, k):
        ox, oy = self.others[k]
        return pltpu.make_async_remote_copy(
            src_ref=self._src(self.chip), dst_ref=self.land.at[2 * ox + oy],
            send_sem=self.send_sems.at[k], recv_sem=self.recv_sems.at[k], device_id=(ox, oy, self.c),
            device_id_type=MESH)

    def _local(self):
        return pltpu.make_async_copy(self._src(self.chip), self.land.at[self.chip], self.local_sem)

    def start(self):
        self._local().start()
        for k in range(N_CHIP - 1):
            self._copy(k).start()

    def finish(self):
        for k in range(N_CHIP - 1):
            self._arrival(k).wait_recv()
        for k in range(N_CHIP - 1):
            self._copy(k).wait_send()
        self._local().wait()


class _HostedChipScatter:
    def __init__(self, pairsum, row0=0, n=None):
        n = pairsum.shape[1] - row0 if n is None else n
        self.row0 = row0
        self.arrays = [pairsum]
        self.out_shapes = [jax.ShapeDtypeStruct((N_CHIP, n, pairsum.shape[2]), pairsum.dtype)]
        self.sem_shapes = [pltpu.SemaphoreType.DMA((N_CHIP - 1,)), pltpu.SemaphoreType.DMA((N_CHIP - 1,)),
                           pltpu.SemaphoreType.DMA(())]

    def begin(self, ins, outs, sems, n_steps):
        @pl.when(pl.program_id(0) == 0)
        def _():
            _ChipScatter(ins[0], self.row0, outs[0], *sems).start()

    def end(self, ins, outs, sems, n_steps):
        @pl.when(pl.program_id(0) == n_steps - 1)
        def _():
            _ChipScatter(ins[0], self.row0, outs[0], *sems).finish()


def _pair_reduce(arrays, entries, name):
    n_arr, n_ent = len(arrays), len(entries)
    cols = arrays[0].shape[1]
    offsets = []
    total = 0
    for _, _, n in entries:
        offsets.append(total)
        total += n

    def body(*refs):
        ins, out_ref = refs[:n_arr], refs[n_arr]
        rbuf, own, send_sems, recv_sems, own_sems = refs[n_arr + 1:]
        q = pl.program_id(0)
        x, y, c = lax.axis_index("x"), lax.axis_index("y"), lax.axis_index("c")

        def block(e, chip, core):
            ai, first, n = entries[e]
            return ins[ai].at[pl.ds(first + (2 * chip + core) * n, n), :]

        def to_sibling(e, chip):
            return pltpu.make_async_remote_copy(
                src_ref=block(e, chip, 1 - c), dst_ref=rbuf.at[chip, pl.ds(offsets[e], entries[e][2]), :],
                send_sem=send_sems.at[e, chip], recv_sem=recv_sems.at[e, chip], device_id=(x, y, 1 - c),
                device_id_type=MESH)

        @pl.when(q == 0)
        def _():
            for chip in range(N_CHIP):
                for e in range(n_ent):
                    to_sibling(e, chip).start()

        loads = [pltpu.make_async_copy(block(e, q, c), own.at[pl.ds(offsets[e], entries[e][2]), :], own_sems.at[e])
                 for e in range(n_ent)]
        for cp in loads:
            cp.start()
        for cp in loads:
            cp.wait()
        for e in range(n_ent):
            to_sibling(e, q).wait_recv()
        out_ref[...] = (own[...].astype(F32) + rbuf[q].astype(F32)).astype(out_ref.dtype)

        @pl.when(q == N_CHIP - 1)
        def _():
            for chip in range(N_CHIP):
                for e in range(n_ent):
                    to_sibling(e, chip).wait_send()

    return pl.pallas_call(
        body, name=name, grid=(N_CHIP,),
        in_specs=[ANY] * n_arr,
        out_specs=pl.BlockSpec((None, total, cols), lambda q: (q, 0, 0)),
        out_shape=jax.ShapeDtypeStruct((N_CHIP, total, cols), BF16),
        scratch_shapes=[pltpu.VMEM((N_CHIP, total, cols), BF16), pltpu.VMEM((total, cols), BF16),
                        pltpu.SemaphoreType.DMA((n_ent, N_CHIP)), pltpu.SemaphoreType.DMA((n_ent, N_CHIP)),
                        pltpu.SemaphoreType.DMA((n_ent,))],
        compiler_params=_params(),
    )(*arrays)


class _HostedScatterAll:
    def __init__(self, packed):
        n = packed.shape[0] // N_DEV
        self.n = n
        self.arrays = [packed]
        self.out_shapes = [jax.ShapeDtypeStruct((N_DEV, n, packed.shape[1]), packed.dtype)]
        self.sem_shapes = [pltpu.SemaphoreType.DMA((N_DEV - 1,)), pltpu.SemaphoreType.DMA((N_DEV - 1,)),
                           pltpu.SemaphoreType.DMA(())]

    def _copies(self, ins, outs, sems, with_arrivals):
        src, land = ins[0], outs[0]
        send_sems, recv_sems, local_sem = sems
        me = _my_index()

        def block(p):
            return src.at[pl.ds(p * self.n, self.n), :]

        local = pltpu.make_async_copy(block(me), land.at[me], local_sem)
        sends, arrivals = [], []
        for k in range(1, N_DEV):
            p = (me + k) % N_DEV
            q = (me + N_DEV - k) % N_DEV
            sends.append(pltpu.make_async_remote_copy(
                src_ref=block(p), dst_ref=land.at[me], send_sem=send_sems.at[k - 1], recv_sem=recv_sems.at[k - 1],
                device_id=(p // 4, (p // 2) % 2, p % 2), device_id_type=MESH))
            if with_arrivals:
                arrivals.append(pltpu.make_async_remote_copy(
                    src_ref=block(me), dst_ref=land.at[q], send_sem=send_sems.at[k - 1], recv_sem=recv_sems.at[k - 1],
                    device_id=(q // 4, (q // 2) % 2, q % 2), device_id_type=MESH))
        return local, sends, arrivals

    def begin(self, ins, outs, sems, n_steps):
        @pl.when(pl.program_id(0) == 0)
        def _():
            local, sends, _ = self._copies(ins, outs, sems, with_arrivals=False)
            local.start()
            for cp in sends:
                cp.start()

    def end(self, ins, outs, sems, n_steps):
        @pl.when(pl.program_id(0) == n_steps - 1)
        def _():
            local, sends, arrivals = self._copies(ins, outs, sems, with_arrivals=True)
            for cp in arrivals:
                cp.wait_recv()
            for cp in sends:
                cp.wait_send()
            local.wait()


def _tril_weights(ws_ref):
    r = lax.broadcasted_iota(jnp.int32, (CHUNK, CHUNK), 0)
    c = lax.broadcasted_iota(jnp.int32, (CHUNK, CHUNK), 1)
    return [jnp.where(r >= c, ws_ref[h], 0.0).astype(BF16) for h in range(HEADS)]


def _sgu_stats(zpre, gain, bias):
    e = zpre.shape[1] // 2
    z, dz = _gelu_and_grad(zpre)
    u, v = z[:, :e], z[:, e:]
    vc = v - _row_mean(v)
    rstd = lax.rsqrt(_row_mean(vc * vc) + NORM_EPS)
    vhat = vc * rstd
    return u, vhat, rstd, vhat * gain + bias, dz


def _spatial_fwd(wt, vn_bf, bfull_ref, sv_ref, tm):
    for ci in range(tm // CHUNK):
        rows = slice(ci * CHUNK, (ci + 1) * CHUNK)
        for h in range(HEADS):
            cols = slice(h * GROUP, (h + 1) * GROUP)
            sv_ref[rows, cols] = _nn(wt[h], vn_bf[rows, cols]) + bfull_ref[:, cols]


def _mixer_a_fwd(x, g, gath, gain, bias, ws, bfull, tm, hosted=()):
    t_tok, d = x.shape
    e = gain.shape[1]
    e2 = 2 * e
    n_in, n_out = e2 // N_DEV, e // N_DEV

    def body(x_ref, g_ref, gain_ref, bias_ref, ws_ref, bfull_ref, gath_ref,
             xo_ref, gd_ref, u_ref, vhat_ref, svo_ref, y_ref, rstd_ref, win_v, wout_v, sv_v, sems):
        _load_group([(gath_ref, 0, n_in, win_v), (gath_ref, n_in, n_out, wout_v)], sems)
        xv = x_ref[...]
        h = _rms_fwd(xv, g_ref[...])[0].astype(BF16)
        zpre = _nt(h, win_v[...])
        u, vhat, rstd, vn, gelu_d = _sgu_stats(zpre, gain_ref[...], bias_ref[...])
        gd_ref[...] = gelu_d.astype(BF16)
        u_ref[...] = u.astype(BF16)
        vhat_ref[...] = vhat.astype(BF16)
        rstd_ref[...] = rstd
        _spatial_fwd(_tril_weights(ws_ref), vn.astype(BF16), bfull_ref, sv_v, tm)
        sv = sv_v[...]
        svo_ref[...] = sv.astype(BF16)
        y = (u * sv).astype(BF16)
        y_ref[...] = y
        xo_ref[...] = xv + _nn(y, wout_v[...])

    return _hosting_call(
        body, "mixer_a_fwd", t_tok // tm, [x, g, gain, bias, ws, bfull, gath],
        in_specs=[_row_spec(tm, d), _const_spec((1, d)), _const_spec((1, e)), _const_spec((1, e)),
                  _const_spec((HEADS, CHUNK, CHUNK)), _const_spec((CHUNK, e)), ANY],
        out_specs=[_row_spec(tm, d), _row_spec(tm, e2), _row_spec(tm, e), _row_spec(tm, e), _row_spec(tm, e),
                   _row_spec(tm, e), _row_spec(tm, 1)],
        out_shape=[jax.ShapeDtypeStruct((t_tok, d), F32), jax.ShapeDtypeStruct((t_tok, e2), BF16),
                   jax.ShapeDtypeStruct((t_tok, e), BF16), jax.ShapeDtypeStruct((t_tok, e), BF16),
                   jax.ShapeDtypeStruct((t_tok, e), BF16), jax.ShapeDtypeStruct((t_tok, e), BF16),
                   jax.ShapeDtypeStruct((t_tok, 1), F32)],
        scratch=[pltpu.VMEM((e2, d), BF16), pltpu.VMEM((e, d), BF16), pltpu.VMEM((tm, e), F32),
                 pltpu.SemaphoreType.DMA((2 * N_DEV,))],
        hosted=hosted)


def _mixer_a_bwd(dout, x, gd, u_sav, vhat_sav, sv_sav, rstd_sav, g, gath, gain, bias, ws, tm, hosted=()):
    t_tok, d = x.shape
    e = gain.shape[1]
    e2 = 2 * e
    n_in, n_out = e2 // N_DEV, e // N_DEV
    n_steps = t_tok // tm

    def body(dout_ref, x_ref, gd_ref, u_ref, vhat_ref, sv_ref, rstd_ref, g_ref, gain_ref, bias_ref, ws_ref, gath_ref,
             dx_ref, dxb_ref, h_ref, dz_ref, dg_ref, dgain_ref, dbias_ref, dws_ref, dbso_ref,
             win_v, wout_v, dvn_v, dbs_ref, sems):
        i = pl.program_id(0)
        _load_group([(gath_ref, 0, n_in, win_v), (gath_ref, n_in, n_out, wout_v)], sems)

        @pl.when(i == 0)
        def _():
            dg_ref[...] = jnp.zeros_like(dg_ref)
            dgain_ref[...] = jnp.zeros_like(dgain_ref)
            dbias_ref[...] = jnp.zeros_like(dbias_ref)
            dws_ref[...] = jnp.zeros_like(dws_ref)
            dbs_ref[...] = jnp.zeros_like(dbs_ref)

        xv = x_ref[...]
        gv = g_ref[...]
        hv, xhat, r = _rms_fwd(xv, gv)
        h_ref[...] = hv.astype(BF16)
        gain_v = gain_ref[...]
        vhat = vhat_ref[...].astype(F32)
        vn_bf = (vhat * gain_v + bias_ref[...]).astype(BF16)
        wt = _tril_weights(ws_ref)

        dov = dout_ref[...]
        dy = _nt(dov.astype(BF16), wout_v[...])
        du = dy * sv_ref[...].astype(F32)
        dsv = dy * u_ref[...].astype(F32)
        dsv_bf = dsv.astype(BF16)
        for ci in range(tm // CHUNK):
            rows = slice(ci * CHUNK, (ci + 1) * CHUNK)
            dbs_ref[...] += dsv[rows, :]
            for h in range(HEADS):
                cols = slice(h * GROUP, (h + 1) * GROUP)
                dvn_v[rows, cols] = _tn(wt[h], dsv_bf[rows, cols])
                dws_ref[h] += _nt(dsv_bf[rows, cols], vn_bf[rows, cols])
        dvn = dvn_v[...]
        dgain_ref[...] += _col_sum(dvn * vhat)
        dbias_ref[...] += _col_sum(dvn)
        dvhat = dvn * gain_v
        dv = rstd_ref[...] * (dvhat - _row_mean(dvhat) - vhat * _row_mean(dvhat * vhat))
        dzpre = (jnp.concatenate([du, dv], axis=1) * gd_ref[...].astype(F32)).astype(BF16)
        dz_ref[...] = dzpre
        dh = _nn(dzpre, win_v[...])
        dxr, dg_row = _rms_bwd(dh, xhat, r, gv)
        dg_ref[...] += dg_row
        dx = dov + dxr
        dx_ref[...] = dx
        dxb_ref[...] = dx.astype(BF16)

        @pl.when(i == n_steps - 1)
        def _():
            rr = lax.broadcasted_iota(jnp.int32, (CHUNK, CHUNK), 0)
            cc = lax.broadcasted_iota(jnp.int32, (CHUNK, CHUNK), 1)
            for h in range(HEADS):
                dws_ref[h] = jnp.where(rr >= cc, dws_ref[h], 0.0)
                dbso_ref[h] = jnp.sum(dbs_ref[:, h * GROUP:(h + 1) * GROUP], axis=1, keepdims=True)

    return _hosting_call(
        body, "mixer_a_bwd", n_steps, [dout, x, gd, u_sav, vhat_sav, sv_sav, rstd_sav, g, gain, bias, ws, gath],
        in_specs=[_row_spec(tm, d), _row_spec(tm, d), _row_spec(tm, e2), _row_spec(tm, e), _row_spec(tm, e),
                  _row_spec(tm, e), _row_spec(tm, 1), _const_spec((1, d)),
                  _const_spec((1, e)), _const_spec((1, e)), _const_spec((HEADS, CHUNK, CHUNK)), ANY],
        out_specs=[_row_spec(tm, d), _row_spec(tm, d), _row_spec(tm, d), _row_spec(tm, e2),
                   _const_spec((1, d)), _const_spec((1, e)), _const_spec((1, e)),
                   _const_spec((HEADS, CHUNK, CHUNK)), _const_spec((HEADS, CHUNK, 1))],
        out_shape=[jax.ShapeDtypeStruct((t_tok, d), F32), jax.ShapeDtypeStruct((t_tok, d), BF16),
                   jax.ShapeDtypeStruct((t_tok, d), BF16), jax.ShapeDtypeStruct((t_tok, e2), BF16),
                   jax.ShapeDtypeStruct((1, d), F32), jax.ShapeDtypeStruct((1, e), F32),
                   jax.ShapeDtypeStruct((1, e), F32), jax.ShapeDtypeStruct((HEADS, CHUNK, CHUNK), F32),
                   jax.ShapeDtypeStruct((HEADS, CHUNK, 1), F32)],
        scratch=[pltpu.VMEM((e2, d), BF16), pltpu.VMEM((e, d), BF16), pltpu.VMEM((tm, e), F32),
                 pltpu.VMEM((CHUNK, e), F32), pltpu.SemaphoreType.DMA((2 * N_DEV,))],
        hosted=hosted)


def _ffn_fwd(x, g, srcs, nf, tm, name, hosted=(), head=None):
    t_tok, d = x.shape
    f = nf * N_DEV
    firsts = [first for _, first in srcs]
    n_head = 2 if head else 0

    def body(*refs):
        x_ref, g_ref, sg_ref, su_ref, sd_ref = refs[:5]
        gate_ref, up_ref, wg_v, wu_v, wd_v, sems = refs[-6:]
        _load_group(
            [(sg_ref, firsts[0], nf, wg_v), (su_ref, firsts[1], nf, wu_v), (sd_ref, firsts[2], nf, wd_v)], sems)
        if head:
            t_ref, gf_ref, loss_ref, dx_ref, dxb_ref, dgf_ref = refs[5:11]

            @pl.when(pl.program_id(0) == 0)
            def _():
                loss_ref[...] = jnp.zeros_like(loss_ref)
                dgf_ref[...] = jnp.zeros_like(dgf_ref)

        xv = x_ref[...]
        h = _rms_fwd(xv, g_ref[...])[0].astype(BF16)
        gate = _nt(h, wg_v[...])
        up = _nt(h, wu_v[...])
        gate_ref[...] = gate.astype(BF16)
        up_ref[...] = up.astype(BF16)
        act = (gate * _sigmoid(gate) * up).astype(BF16)
        xo = xv + _nn(act, wd_v[...])
        if head:
            gfv = gf_ref[...]
            y, xhat, r = _rms_fwd(xo, gfv)
            err = y - t_ref[...]
            loss_ref[...] += 0.5 * jnp.sum(_row_mean(err * err), axis=0, keepdims=True)
            dxr, dg_row = _rms_bwd(err * (1.0 / d), xhat, r, gfv)
            dgf_ref[...] += dg_row
            dx_ref[...] = dxr
            dxb_ref[...] = dxr.astype(BF16)
        else:
            refs[5][...] = xo

    act_specs = [_row_spec(tm, f), _row_spec(tm, f)]
    act_shapes = [jax.ShapeDtypeStruct((t_tok, f), BF16), jax.ShapeDtypeStruct((t_tok, f), BF16)]
    if head:
        out_specs = [_const_spec((1, 1)), _row_spec(tm, d), _row_spec(tm, d), _const_spec((1, d))]
        out_shape = [jax.ShapeDtypeStruct((1, 1), F32), jax.ShapeDtypeStruct((t_tok, d), F32),
                     jax.ShapeDtypeStruct((t_tok, d), BF16), jax.ShapeDtypeStruct((1, d), F32)]
    else:
        out_specs = [_row_spec(tm, d)]
        out_shape = [jax.ShapeDtypeStruct((t_tok, d), F32)]
    return _hosting_call(
        body, name, t_tok // tm, [x, g] + [arr for arr, _ in srcs] + list(head or ()),
        in_specs=[_row_spec(tm, d), _const_spec((1, d)), ANY, ANY, ANY] + [_row_spec(tm, d), _const_spec((1, d))][:n_head],
        out_specs=out_specs + act_specs, out_shape=out_shape + act_shapes,
        scratch=[pltpu.VMEM((f, d), BF16), pltpu.VMEM((f, d), BF16), pltpu.VMEM((f, d), BF16),
                 pltpu.SemaphoreType.DMA((3 * N_DEV,))],
        hosted=hosted)


def _ffn_bwd(dout, x, gate, up, g, srcs, nf, tm, name, hosted=()):
    t_tok, d = x.shape
    f = nf * N_DEV
    firsts = [first for _, first in srcs]
    per_chunk = -(-f // (FFN_CHUNKS * MXU_WIDTH)) * MXU_WIDTH
    bounds = [min(ck * per_chunk, f) for ck in range(FFN_CHUNKS + 1)]

    def body(dout_ref, x_ref, gate_ref, up_ref, g_ref, sg_ref, su_ref, sd_ref,
             dx_ref, dxb_ref, h_ref, act_ref, dgu_ref, dg_ref, wg_v, wu_v, wd_v, sems):
        _load_group(
            [(sg_ref, firsts[0], nf, wg_v), (su_ref, firsts[1], nf, wu_v), (sd_ref, firsts[2], nf, wd_v)], sems)

        @pl.when(pl.program_id(0) == 0)
        def _():
            dg_ref[...] = jnp.zeros_like(dg_ref)

        xv = x_ref[...]
        gv = g_ref[...]
        hv, xhat, r = _rms_fwd(xv, gv)
        h_ref[...] = hv.astype(BF16)
        dov = dout_ref[...]
        dob = dov.astype(BF16)
        dh = None
        for ck in range(FFN_CHUNKS):
            cols = slice(bounds[ck], bounds[ck + 1])
            gate_v = gate_ref[:, cols].astype(F32)
            up_v = up_ref[:, cols].astype(F32)
            sig = _sigmoid(gate_v)
            silu = gate_v * sig
            act_ref[:, cols] = (silu * up_v).astype(BF16)
            dact = _nt(dob, wd_v[cols, :])
            dup = (dact * silu).astype(BF16)
            dgate = (dact * up_v * (sig * (1.0 + gate_v * (1.0 - sig)))).astype(BF16)
            dgu_ref[:, cols] = dgate
            dgu_ref[:, f + bounds[ck]:f + bounds[ck + 1]] = dup
            part = _nn(dgate, wg_v[cols, :]) + _nn(dup, wu_v[cols, :])
            dh = part if dh is None else dh + part
        dxr, dg_row = _rms_bwd(dh, xhat, r, gv)
        dg_ref[...] += dg_row
        dx = dov + dxr
        dx_ref[...] = dx
        dxb_ref[...] = dx.astype(BF16)

    return _hosting_call(
        body, name, t_tok // tm, [dout, x, gate, up, g] + [arr for arr, _ in srcs],
        in_specs=[_row_spec(tm, d), _row_spec(tm, d), _row_spec(tm, f), _row_spec(tm, f), _const_spec((1, d)),
                  ANY, ANY, ANY],
        out_specs=[_row_spec(tm, d), _row_spec(tm, d), _row_spec(tm, d), _row_spec(tm, f), _row_spec(tm, 2 * f),
                   _const_spec((1, d))],
        out_shape=[jax.ShapeDtypeStruct((t_tok, d), F32), jax.ShapeDtypeStruct((t_tok, d), BF16),
                   jax.ShapeDtypeStruct((t_tok, d), BF16), jax.ShapeDtypeStruct((t_tok, f), BF16),
                   jax.ShapeDtypeStruct((t_tok, 2 * f), BF16), jax.ShapeDtypeStruct((1, d), F32)],
        scratch=[pltpu.VMEM((f, d), BF16), pltpu.VMEM((f, d), BF16), pltpu.VMEM((f, d), BF16),
                 pltpu.SemaphoreType.DMA((3 * N_DEV,))],
        hosted=hosted)


def _shift_down(z, k, prev_rows):
    row = lax.broadcasted_iota(jnp.int32, z.shape, 0)
    out = pltpu.roll(z, k, 0)
    for j in range(k):
        out = jnp.where(row == j, prev_rows[j], out)
    return out


def _shift_up(z, k, next_rows):
    tm = z.shape[0]
    row = lax.broadcasted_iota(jnp.int32, z.shape, 0)
    out = pltpu.roll(z, tm - k, 0)
    for j in range(k):
        out = jnp.where(row == tm - k + j, next_rows[j], out)
    return out


def _mixer_b_fwd(x, g, gath, conv_w, tm, seq, hosted=()):
    t_tok, d = x.shape
    e = conv_w.shape[1]
    e3 = 3 * e
    n_in, n_out = e3 // N_DEV, e // N_DEV
    tiles_per_seq = seq // tm

    def body(x_ref, g_ref, cw_ref, gath_ref, xo_ref, p_ref, win_v, wout_v, tail_v, sems):
        i = pl.program_id(0)
        _load_group([(gath_ref, 0, n_in, win_v), (gath_ref, n_in, n_out, wout_v)], sems)

        @pl.when(i % tiles_per_seq == 0)
        def _():
            tail_v[...] = jnp.zeros_like(tail_v)

        xv = x_ref[...]
        h = _rms_fwd(xv, g_ref[...])[0].astype(BF16)
        p = _nt(h, win_v[...])
        p_ref[...] = p.astype(BF16)
        z = p[:, e:2 * e] * p[:, 2 * e:]
        prev = [tail_v[SUBLANES - 2:SUBLANES - 1, :], tail_v[SUBLANES - 1:SUBLANES, :]]
        conv = (cw_ref[2:3, :] * z + cw_ref[1:2, :] * _shift_down(z, 1, prev[1:])
                + cw_ref[0:1, :] * _shift_down(z, 2, prev))
        tail_v[...] = z[tm - SUBLANES:, :]
        y = (p[:, :e] * conv).astype(BF16)
        xo_ref[...] = xv + _nn(y, wout_v[...])

    return _hosting_call(
        body, "mixer_b_fwd", t_tok // tm, [x, g, conv_w, gath],
        in_specs=[_row_spec(tm, d), _const_spec((1, d)), _const_spec((SUBLANES, e)), ANY],
        out_specs=[_row_spec(tm, d), _row_spec(tm, e3)],
        out_shape=[jax.ShapeDtypeStruct((t_tok, d), F32), jax.ShapeDtypeStruct((t_tok, e3), BF16)],
        scratch=[pltpu.VMEM((e3, d), BF16), pltpu.VMEM((e, d), BF16), pltpu.VMEM((SUBLANES, e), F32),
                 pltpu.SemaphoreType.DMA((2 * N_DEV,))],
        hosted=hosted)


def _mixer_b_bwd(dout, x, p, g, gath, conv_w, tm, seq, hosted=()):
    t_tok, d = x.shape
    e = conv_w.shape[1]
    e3 = 3 * e
    n_in, n_out = e3 // N_DEV, e // N_DEV
    tiles_per_seq = seq // tm
    halo_per_tile = tm // HALO
    n_halo = t_tok // HALO

    def body(dout_ref, dnext_ref, x_ref, p_ref, pprev_ref, pnext_ref, g_ref, cw_ref, gath_ref,
             dx_ref, dxb_ref, h_ref, y_ref, dp_ref, dg_ref, dcw_ref, win_v, wout_v, sems):
        i = pl.program_id(0)
        _load_group([(gath_ref, 0, n_in, win_v), (gath_ref, n_in, n_out, wout_v)], sems)

        @pl.when(i == 0)
        def _():
            dg_ref[...] = jnp.zeros_like(dg_ref)
            dcw_ref[...] = jnp.zeros_like(dcw_ref)

        first = (i % tiles_per_seq == 0).astype(F32)
        last = (i % tiles_per_seq == tiles_per_seq - 1).astype(F32)
        xv = x_ref[...]
        gv = g_ref[...]
        hv, xhat, r = _rms_fwd(xv, gv)
        h_ref[...] = hv.astype(BF16)
        pv = p_ref[...].astype(F32)
        bg, cg, hx = pv[:, :e], pv[:, e:2 * e], pv[:, 2 * e:]
        z = cg * hx
        pprev = pprev_ref[...].astype(F32)
        zprev = pprev[:, e:2 * e] * pprev[:, 2 * e:] * (1.0 - first)
        prev = [zprev[HALO - 2:HALO - 1, :], zprev[HALO - 1:HALO, :]]
        zs1 = _shift_down(z, 1, prev[1:])
        zs2 = _shift_down(z, 2, prev)
        w0, w1, w2 = cw_ref[0:1, :], cw_ref[1:2, :], cw_ref[2:3, :]
        conv = w2 * z + w1 * zs1 + w0 * zs2
        y_ref[...] = (bg * conv).astype(BF16)

        dov = dout_ref[...]
        wout_bf = wout_v[...]
        dy = _nt(dov.astype(BF16), wout_bf)
        dconv = dy * bg
        dnext = _nt(dnext_ref[...].astype(BF16), wout_bf) * pnext_ref[:, :e].astype(F32) * (1.0 - last)
        nxt = [dnext[0:1, :], dnext[1:2, :]]
        dz = w2 * dconv + w1 * _shift_up(dconv, 1, nxt[:1]) + w0 * _shift_up(dconv, 2, nxt)
        dcw_ref[0:1, :] += _col_sum(dconv * zs2)
        dcw_ref[1:2, :] += _col_sum(dconv * zs1)
        dcw_ref[2:3, :] += _col_sum(dconv * z)
        dp = jnp.concatenate([dy * conv, dz * hx, dz * cg], axis=1).astype(BF16)
        dp_ref[...] = dp
        dh = _nn(dp, win_v[...])
        dxr, dg_row = _rms_bwd(dh, xhat, r, gv)
        dg_ref[...] += dg_row
        dx = dov + dxr
        dx_ref[...] = dx
        dxb_ref[...] = dx.astype(BF16)

    prev_spec = lambda w: pl.BlockSpec((HALO, w), lambda i: (jnp.maximum(i * halo_per_tile - 1, 0), 0))
    next_spec = lambda w: pl.BlockSpec((HALO, w), lambda i: (jnp.minimum((i + 1) * halo_per_tile, n_halo - 1), 0))
    return _hosting_call(
        body, "mixer_b_bwd", t_tok // tm, [dout, dout, x, p, p, p, g, conv_w, gath],
        in_specs=[_row_spec(tm, d), next_spec(d), _row_spec(tm, d), _row_spec(tm, e3), prev_spec(e3), next_spec(e3),
                  _const_spec((1, d)), _const_spec((SUBLANES, e)), ANY],
        out_specs=[_row_spec(tm, d), _row_spec(tm, d), _row_spec(tm, d), _row_spec(tm, e), _row_spec(tm, e3),
                   _const_spec((1, d)), _const_spec((SUBLANES, e))],
        out_shape=[jax.ShapeDtypeStruct((t_tok, d), F32), jax.ShapeDtypeStruct((t_tok, d), BF16),
                   jax.ShapeDtypeStruct((t_tok, d), BF16), jax.ShapeDtypeStruct((t_tok, e), BF16),
                   jax.ShapeDtypeStruct((t_tok, e3), BF16), jax.ShapeDtypeStruct((1, d), F32),
                   jax.ShapeDtypeStruct((SUBLANES, e), F32)],
        scratch=[pltpu.VMEM((e3, d), BF16), pltpu.VMEM((e, d), BF16), pltpu.SemaphoreType.DMA((2 * N_DEV,))],
        hosted=hosted)


def _wgrads(jobs, name, hosted=()):
    n_jobs = len(jobs)
    steps = [a.shape[1] // bm for a, _, bm in jobs]
    starts = [sum(steps[:k]) for k in range(n_jobs)]

    def body(*refs):
        a_refs, b_hbm, outs = refs[:n_jobs], refs[n_jobs:2 * n_jobs], refs[2 * n_jobs:3 * n_jobs]
        b_vmem, sems = refs[3 * n_jobs:4 * n_jobs], refs[4 * n_jobs]
        i = pl.program_id(0)
        for k in range(n_jobs):
            @pl.when(i == max(starts[k] - 1, 0))
            def _(k=k):
                pltpu.make_async_copy(b_hbm[k], b_vmem[k], sems.at[k]).start()

            @pl.when(i == starts[k])
            def _(k=k):
                pltpu.make_async_copy(b_hbm[k], b_vmem[k], sems.at[k]).wait()

            @pl.when((i >= starts[k]) & (i < starts[k] + steps[k]))
            def _(k=k):
                outs[k][...] = _tn(a_refs[k][...], b_vmem[k][...]).astype(BF16)

    def at(k):
        return lambda i: jnp.clip(i - starts[k], 0, steps[k] - 1)

    a_specs, out_specs, out_shape = [], [], []
    for k, (a, b, bm) in enumerate(jobs):
        t_tok, m = a.shape
        a_specs.append(pl.BlockSpec((t_tok, bm), lambda i, f=at(k): (0, f(i))))
        out_specs.append(pl.BlockSpec((bm, b.shape[1]), lambda i, f=at(k): (f(i), 0)))
        out_shape.append(jax.ShapeDtypeStruct((m, b.shape[1]), BF16))
    outs, h_outs = _hosting_call(
        body, name, sum(steps), [a for a, _, _ in jobs] + [b for _, b, _ in jobs],
        in_specs=a_specs + [ANY] * n_jobs, out_specs=out_specs, out_shape=out_shape,
        scratch=[pltpu.VMEM(b.shape, b.dtype) for _, b, _ in jobs] + [pltpu.SemaphoreType.DMA((n_jobs,))],
        hosted=hosted)
    return (outs, h_outs) if hosted else outs


def _sum_slots(land, rb, name):
    n_slots, rows, cols = land.shape

    def body(l_ref, o_ref):
        acc = l_ref[0].astype(F32)
        for k in range(1, n_slots):
            acc = acc + l_ref[k].astype(F32)
        o_ref[...] = acc

    return pl.pallas_call(
        body, name=name, grid=(rows // rb,),
        in_specs=[pl.BlockSpec((n_slots, rb, cols), lambda i: (0, i, 0))],
        out_specs=pl.BlockSpec((rb, cols), lambda i: (i, 0)),
        out_shape=jax.ShapeDtypeStruct((rows, cols), F32),
        compiler_params=_params(sequential=False),
    )(land)


def _adamw(w, grad, m, v, rb, name):
    rows, cols = w.shape
    c1 = 1.0 / (1.0 - ADAM_B1 ** ADAM_STEP)
    c2 = 1.0 / (1.0 - ADAM_B2 ** ADAM_STEP)

    def body(w_ref, g_ref, m_ref, v_ref, d_ref, mo_ref, vo_ref):
        gv = g_ref[...]
        mn = ADAM_B1 * m_ref[...] + (1.0 - ADAM_B1) * gv
        vn = ADAM_B2 * v_ref[...] + (1.0 - ADAM_B2) * (gv * gv)
        mo_ref[...] = mn
        vo_ref[...] = vn
        d_ref[...] = -ADAM_LR * ((mn * c1) / (jnp.sqrt(vn * c2) + ADAM_EPS) + ADAM_WD * w_ref[...])

    spec = pl.BlockSpec((rb, cols), lambda i: (i, 0))
    shape = jax.ShapeDtypeStruct((rows, cols), F32)
    return pl.pallas_call(
        body, name=name, grid=(rows // rb,),
        in_specs=[spec] * 4, out_specs=[spec] * 3, out_shape=[shape] * 3,
        compiler_params=_params(sequential=False),
    )(w, grad, m, v)


def _pack_shards(groups, name, hosted=()):
    flat = [(part, layer) for group in groups for part, layer, _ in group]
    rows = [[p.shape[2] if turn else p.shape[1] for p, _, turn in group] for group in groups]
    first, _, first_turn = groups[0][0]
    width = first.shape[1] if first_turn else first.shape[2]

    def body(*refs):
        ins, outs = refs[:len(flat)], refs[len(flat):]
        k = 0
        for gi, group in enumerate(groups):
            off = 0
            for (_, _, turn), n in zip(group, rows[gi]):
                part = ins[k][...].astype(BF16)
                if turn:
                    r = lax.broadcasted_iota(jnp.int32, (n, n), 0)
                    c = lax.broadcasted_iota(jnp.int32, (n, n), 1)
                    part = _nt((r == c).astype(BF16), part).astype(BF16)
                outs[gi][off:off + n, :] = part
                off += n
                k += 1

    return _hosting_call(
        body, name, 1, [p for p, _ in flat],
        in_specs=[pl.BlockSpec((None,) + p.shape[1:], lambda i, layer=layer: (layer, 0, 0)) for p, layer in flat],
        out_specs=[_const_spec((sum(r), width)) for r in rows],
        out_shape=[jax.ShapeDtypeStruct((sum(r), width), BF16) for r in rows],
        scratch=[], hosted=hosted)


def _split_bf16(a):
    hi = a.astype(BF16)
    rest = a - hi.astype(F32)
    mid = rest.astype(BF16)
    return hi, mid, (rest - mid.astype(F32)).astype(BF16)


def _reduce_adamw(lands, w, m, v, transpose, name, hosted=()):
    n_layers, rows_w, cols_w = w.shape
    c1 = 1.0 / (1.0 - ADAM_B1 ** ADAM_STEP)
    c2 = 1.0 / (1.0 - ADAM_B2 ** ADAM_STEP)
    flat = [piece for pieces in lands for piece in pieces]
    counts = [len(pieces) for pieces in lands]
    if transpose:
        tiles = rows_w // MXU_WIDTH
        blk = (MXU_WIDTH, cols_w)
        land_specs = [pl.BlockSpec((N_CHIP, n, MXU_WIDTH), lambda i, b=first // n: (0, b, i % tiles))
                      for _, first, n in flat]
        for _, first, n in flat:
            assert first % n == 0
    else:
        tiles = 2
        blk = (rows_w // tiles, cols_w)
        assert all(c == 1 for c in counts)
        land_specs = [pl.BlockSpec((N_CHIP,) + blk, lambda i, b=first // blk[0]: (0, b + i % tiles, 0))
                      for _, first, _ in flat]
        for _, first, _ in flat:
            assert first % blk[0] == 0

    def body(*refs):
        land_refs = refs[:len(flat)]
        w_ref, m_ref, v_ref, g_ref, d_ref, mo_ref, vo_ref = refs[len(flat):]
        layer = pl.program_id(0) // tiles

        def total(ref):
            acc = ref[0].astype(F32)
            for q in range(1, N_CHIP):
                acc = acc + ref[q].astype(F32)
            return acc

        def layer_sum(k):
            first = sum(counts[:k])
            parts = [total(land_refs[first + j]) for j in range(counts[k])]
            return parts[0] if len(parts) == 1 else jnp.concatenate(parts, axis=0)

        gv = layer_sum(0)
        for k in range(1, n_layers):
            gv = jnp.where(layer == k, layer_sum(k), gv)
        if transpose:
            r = lax.broadcasted_iota(jnp.int32, (MXU_WIDTH, MXU_WIDTH), 0)
            c = lax.broadcasted_iota(jnp.int32, (MXU_WIDTH, MXU_WIDTH), 1)
            eye = (r == c).astype(BF16)
            hi, mid, lo = _split_bf16(gv)
            gv = _nt(eye, hi) + _nt(eye, mid) + _nt(eye, lo)
        g_ref[...] = gv
        mn = ADAM_B1 * m_ref[...] + (1.0 - ADAM_B1) * gv
        vn = ADAM_B2 * v_ref[...] + (1.0 - ADAM_B2) * (gv * gv)
        mo_ref[...] = mn
        vo_ref[...] = vn
        d_ref[...] = -ADAM_LR * ((mn * c1) / (jnp.sqrt(vn * c2) + ADAM_EPS) + ADAM_WD * w_ref[...])

    spec = pl.BlockSpec((None,) + blk, lambda i: (i // tiles, i % tiles, 0))
    shape = jax.ShapeDtypeStruct(w.shape, F32)
    outs, h_outs = _hosting_call(
        body, name, n_layers * tiles, [land for land, _, _ in flat] + [w, m, v],
        in_specs=land_specs + [spec] * 3, out_specs=[spec] * 4, out_shape=[shape] * 4, scratch=[], hosted=hosted)
    return (outs, h_outs) if hosted else outs


def _pack_small(parts, rows):
    flat = jnp.concatenate([p.reshape(-1).astype(F32) for p in parts])
    return jnp.pad(flat, (0, rows * LANES - flat.shape[0])).reshape(rows, LANES)


def _unpack_small(packed, shapes):
    flat = packed.reshape(-1)
    out = []
    pos = 0
    for s in shapes:
        n = math.prod(s)
        out.append(flat[pos:pos + n].reshape(s))
        pos += n
    return out


def kernel(x, mix_norm, ffn_norm, a_w_in, a_v_gain, a_v_bias, a_w_s, a_b_s, a_w_out, b_w_in, b_conv_w, b_w_out, ffn_w_gate, ffn_w_up, ffn_w_down, final_norm, loss_target, m_mix_norm, m_ffn_norm, m_a_w_in, m_a_v_gain, m_a_v_bias, m_a_w_s, m_a_b_s, m_a_w_out, m_b_w_in, m_b_conv_w, m_b_w_out, m_ffn_w_gate, m_ffn_w_up, m_ffn_w_down, m_final_norm, v_mix_norm, v_ffn_norm, v_a_w_in, v_a_v_gain, v_a_v_bias, v_a_w_s, v_a_b_s, v_a_w_out, v_b_w_in, v_b_conv_w, v_b_w_out, v_ffn_w_gate, v_ffn_w_up, v_ffn_w_down, v_final_norm):
    bsz, seq, d = x.shape
    t_tok = bsz * seq
    me = _my_index()
    xt = x.reshape(t_tok, d)
    target = loss_target.reshape(t_tok, d)
    e_a = a_v_gain.shape[1]
    e_b = b_w_out.shape[1] * N_DEV
    n_layers = ffn_w_gate.shape[0]
    f_shard = ffn_w_gate.shape[2]
    f_full = f_shard * N_DEV

    conv_pad = jnp.pad(b_conv_w[0], ((0, SUBLANES - CONV_W), (0, 0)))
    sh_a = jnp.concatenate([a_w_in[0].T, a_w_out[0]]).astype(BF16)
    bfull = jnp.repeat(a_b_s[0].T, GROUP, axis=1)

    gate_t, up_t = ffn_w_gate.transpose(0, 2, 1), ffn_w_up.transpose(0, 2, 1)
    (sh_b, sh_f0, sh_f1g, sh_f1ud), (gath_a, conv_g) = _pack_shards(
        [[(b_w_in, 0, True), (b_w_out, 0, False)],
         [(gate_t, 0, False), (up_t, 0, False), (ffn_w_down, 0, False)],
         [(gate_t, 1, False)],
         [(up_t, 1, False), (ffn_w_down, 1, False)]],
        "pack_shards", hosted=[_HostedGathers([sh_a, conv_pad], 0)])
    conv_full = jnp.pad(conv_g[:, :CONV_W, :].transpose(1, 0, 2).reshape(CONV_W, e_b), ((0, SUBLANES - CONV_W), (0, 0)))
    (x1, gd_a, u_a, vhat_a, sv_a, y_a, rstd_a), (gath_f0,) = _mixer_a_fwd(
        xt, mix_norm[0:1], gath_a, a_v_gain, a_v_bias, a_w_s[0], bfull, tm=256,
        hosted=[_HostedGathers([sh_f0], mid_lead=2)])
    srcs0 = [(gath_f0, 0), (gath_f0, f_shard), (gath_f0, 2 * f_shard)]
    (x2, gate0, up0), (gath_b, gath_f1g) = _ffn_fwd(x1, ffn_norm[0:1], srcs0, f_shard, tm=256, name="ffn_fwd0",
                                                    hosted=[_HostedGathers([sh_b, sh_f1g], mid_lead=2)])
    (x3, p_b), (gath_f1ud,) = _mixer_b_fwd(x2, mix_norm[1:2], gath_b, conv_full, tm=256, seq=seq,
                                           hosted=[_HostedGathers([sh_f1ud], mid_lead=2)])
    srcs1 = [(gath_f1g, 0), (gath_f1ud, 0), (gath_f1ud, f_shard)]
    (loss_part, dx4, dx4_bf, d_final, gate1, up1), _ = _ffn_fwd(
        x3, ffn_norm[1:2], srcs1, f_shard, tm=256, name="ffn_fwd1", head=(target, final_norm.reshape(1, d)))

    ffn_entries = [(0, 0, f_shard), (0, f_full, f_shard), (1, 0, f_shard)]
    (dx3, dx3_bf, h_f1, act1, dgu1, d_fn1), _ = _ffn_bwd(dx4, x3, gate1, up1, ffn_norm[1:2], srcs1, f_shard, tm=256,
                                                         name="ffn_bwd1")
    g_down1, g_gu1 = _wgrads([(act1, dx4_bf, 256), (dgu1, h_f1, 512)], "wgrad_f1")
    ps_f1 = _pair_reduce([g_gu1, g_down1], ffn_entries, "pair_reduce_f1")
    (dx2, dx2_bf, h_b, y_b, dp_b, d_mn1, d_conv), (land_f1gu,) = _mixer_b_bwd(
        dx3, x2, p_b, mix_norm[1:2], gath_b, conv_full, tm=256, seq=seq,
        hosted=[_HostedChipScatter(ps_f1, 0, 2 * f_shard)])
    g_b_out, g_b_in = _wgrads([(y_b, dx3_bf, 256), (dp_b, h_b, 512)], "wgrad_b")
    ps_b = _pair_reduce([g_b_in, g_b_out], [(0, 0, b_w_in.shape[2]), (1, 0, b_w_out.shape[1])], "pair_reduce_b")
    (dx1, dx1_bf, h_f0, act0, dgu0, d_fn0), (land_f1d, land_b) = _ffn_bwd(
        dx2, x1, gate0, up0, ffn_norm[0:1], srcs0, f_shard, tm=256, name="ffn_bwd0",
        hosted=[_HostedChipScatter(ps_f1, 2 * f_shard, f_shard), _HostedChipScatter(ps_b)])
    g_down0, g_gu0, g_a_out = _wgrads([(act0, dx2_bf, 256), (dgu0, h_f0, 512), (y_a, dx1_bf, 256)], "wgrad_f0_a_out")
    n_ao = a_w_out.shape[1]
    ps_f0ao = _pair_reduce([g_gu0, g_down0, g_a_out], ffn_entries + [(2, 0, n_ao)], "pair_reduce_f0_a_out")
    (dx0, _, h_a, dz_a, d_mn0, d_gain, d_bias, d_ws, d_bs_acc), (land_f0, land_ao) = _mixer_a_bwd(
        dx1, xt, gd_a, u_a, vhat_a, sv_a, rstd_a, mix_norm[0:1], gath_a, a_v_gain, a_v_bias, a_w_s[0], tm=256,
        hosted=[_HostedChipScatter(ps_f0ao, 0, 3 * f_shard), _HostedChipScatter(ps_f0ao, 3 * f_shard, n_ao)])
    d_bs = d_bs_acc.reshape(HEADS, CHUNK)

    small_grads = [jnp.concatenate([d_mn0, d_mn1]), jnp.concatenate([d_fn0, d_fn1]), d_gain, d_bias, d_ws, d_bs,
                   d_final, d_conv[:CONV_W], loss_part]
    small_shapes = [(n_layers, d), (n_layers, d), (1, e_a), (1, e_a), (1, HEADS, CHUNK, CHUNK), (1, HEADS, CHUNK), (d,),
                    (CONV_W, e_b), ()]
    n_small = sum(math.prod(s) for s in small_shapes)
    blk_rows = -(-n_small // (N_DEV * LANES * SUBLANES)) * SUBLANES
    small_rows = blk_rows * N_DEV
    packed = _pack_small(small_grads, small_rows)
    (g_a_in,), (small_land,) = _wgrads([(dz_a, h_a, 512)], "wgrad_a_in", hosted=[_HostedScatterAll(packed)])
    ps_ai = _pair_reduce([g_a_in], [(0, 0, a_w_in.shape[2])], "pair_reduce_a_in")
    small_sum = _sum_slots(small_land, blk_rows, "sum_small")

    land_ai, small_gath = _exchange([_HostedChipScatter(ps_ai), _HostedGathers([small_sum], 0)], "tail_exchange")
    small_all = small_gath.reshape(small_rows, LANES)

    n_b_in = b_w_in.shape[2]
    gate_out = _reduce_adamw([[(land_f0, 0, f_shard)], [(land_f1gu, 0, f_shard)]], gate_t,
                             m_ffn_w_gate.transpose(0, 2, 1), v_ffn_w_gate.transpose(0, 2, 1), False, "adamw_gate")
    up_out = _reduce_adamw([[(land_f0, f_shard, f_shard)], [(land_f1gu, f_shard, f_shard)]], up_t,
                           m_ffn_w_up.transpose(0, 2, 1), v_ffn_w_up.transpose(0, 2, 1), False, "adamw_up")
    res = {
        "a_w_in": _reduce_adamw([[(land_ai, 0, a_w_in.shape[2])]], a_w_in, m_a_w_in, v_a_w_in, True, "adamw_a_in"),
        "a_w_out": _reduce_adamw([[(land_ao, 0, a_w_out.shape[1])]], a_w_out, m_a_w_out, v_a_w_out, False,
                                 "adamw_a_out"),
        "b_w_in": _reduce_adamw([[(land_b, 0, n_b_in)]], b_w_in, m_b_w_in, v_b_w_in, True, "adamw_b_in"),
        "b_w_out": _reduce_adamw([[(land_b, n_b_in, b_w_out.shape[1])]], b_w_out, m_b_w_out, v_b_w_out, False,
                                 "adamw_b_out"),
        "ffn_w_gate": [o.transpose(0, 2, 1) for o in gate_out],
        "ffn_w_up": [o.transpose(0, 2, 1) for o in up_out],
        "ffn_w_down": _reduce_adamw([[(land_f0, 2 * f_shard, f_shard)], [(land_f1d, 0, f_shard)]], ffn_w_down,
                                    m_ffn_w_down, v_ffn_w_down, False, "adamw_down"),
    }

    (gr_mix, gr_ffn, gr_gain, gr_bias, gr_ws, gr_bs, gr_final, gr_conv_full, loss) = _unpack_small(small_all, small_shapes)
    gr_conv = lax.dynamic_slice_in_dim(gr_conv_full, me * (e_b // N_DEV), e_b // N_DEV, axis=1)[None]

    small_w =[mix_norm, ffn_norm, a_v_gain, a_v_bias, a_w_s, a_b_s, final_norm]
    small_m = [m_mix_norm, m_ffn_norm, m_a_v_gain, m_a_v_bias, m_a_w_s, m_a_b_s, m_final_norm]
    small_v = [v_mix_norm, v_ffn_norm, v_a_v_gain, v_a_v_bias, v_a_w_s, v_a_b_s, v_final_norm]
    small_g = [gr_mix, gr_ffn, gr_gain, gr_bias, gr_ws, gr_bs, gr_final]
    sm_shapes = small_shapes[:len(small_w)]
    sm_out = _adamw(_pack_small(small_w, small_rows), _pack_small(small_g, small_rows), _pack_small(small_m, small_rows),
                    _pack_small(small_v, small_rows), small_rows, "adamw_small")
    sm_delta, sm_m, sm_v = [_unpack_small(o, sm_shapes) for o in sm_out]

    conv_out = _adamw(b_conv_w[0], gr_conv[0], m_b_conv_w[0], v_b_conv_w[0], CONV_W, "adamw_conv")
    conv_delta, conv_m, conv_v = [o[None] for o in conv_out]

    order = ["mix_norm", "ffn_norm", "a_w_in", "a_v_gain", "a_v_bias", "a_w_s", "a_b_s", "a_w_out", "b_w_in",
             "b_conv_w", "b_w_out", "ffn_w_gate", "ffn_w_up", "ffn_w_down", "final_norm"]
    small_names = ["mix_norm", "ffn_norm", "a_v_gain", "a_v_bias", "a_w_s", "a_b_s", "final_norm"]
    grads = {"b_conv_w": gr_conv}
    deltas, new_m, new_v = {}, {}, {}
    for k, name in enumerate(small_names):
        grads[name] = small_g[k]
        deltas[name], new_m[name], new_v[name] = sm_delta[k], sm_m[k], sm_v[k]
    deltas["b_conv_w"], new_m["b_conv_w"], new_v["b_conv_w"] = conv_delta, conv_m, conv_v
    for name, (gg, dl, mm, vv) in res.items():
        grads[name], deltas[name], new_m[name], new_v[name] = gg, dl, mm, vv

    grad_x = dx0.reshape(bsz, seq, d)
    return (loss, grad_x, *[grads[n] for n in order], *[deltas[n] for n in order],
            *[new_m[n] for n in order], *[new_v[n] for n in order])
```

```python
import math

import jax
import jax.numpy as jnp
from jax import lax
from jax.experimental import pallas as pl
from jax.experimental.pallas import tpu as pltpu

F32 = jnp.float32
BF16 = jnp.bfloat16

N_DEV = 8
N_CHIP = 4
CHUNK = 128
HEADS = 16
GROUP = 128
CONV_W = 3
NORM_EPS = 1e-6
GELU_C = math.sqrt(2.0 / math.pi)
GELU_K = 0.044715

ADAM_LR = 0.001
ADAM_B1 = 0.9
ADAM_B2 = 0.999
ADAM_EPS = 1e-08
ADAM_WD = 0.01
ADAM_STEP = 10

LANES = 128
SUBLANES = 8
VMEM_LIMIT = 60 * 1024 * 1024
HALO = 16
MXU_WIDTH = 256
FFN_CHUNKS = 2

MESH = pl.DeviceIdType.MESH
ANY = pl.BlockSpec(memory_space=pl.ANY)


def _params(sequential=True):
    return pltpu.CompilerParams(
        dimension_semantics=("arbitrary",) if sequential else None,
        vmem_limit_bytes=VMEM_LIMIT)


def _nn(a, b):
    return jnp.dot(a, b, preferred_element_type=F32)


def _nt(a, b):
    return lax.dot_general(a, b, (((1,), (1,)), ((), ())), preferred_element_type=F32)


def _tn(a, b):
    return lax.dot_general(a, b, (((0,), (0,)), ((), ())), preferred_element_type=F32)


def _row_mean(a):
    return jnp.mean(a, axis=-1, keepdims=True)


def _col_sum(a):
    return jnp.sum(a, axis=0, keepdims=True)


def _rms_fwd(x, g):
    r = lax.rsqrt(_row_mean(x * x) + NORM_EPS)
    xhat = x * r
    return xhat * g, xhat, r


def _rms_bwd(dh, xhat, r, g):
    a = dh * g
    dx = r * (a - xhat * _row_mean(a * xhat))
    return dx, _col_sum(dh * xhat)


def _gelu_and_grad(x):
    x2 = x * x
    t = jnp.tanh(x * (GELU_C + (GELU_C * GELU_K) * x2))
    half = 0.5 * t + 0.5
    d = half + x * (0.5 - 0.5 * (t * t)) * (GELU_C + (3.0 * GELU_C * GELU_K) * x2)
    return x * half, d


def _sigmoid(x):
    return 1.0 / (1.0 + jnp.exp(-x))


def _row_spec(tm, width):
    return pl.BlockSpec((tm, width), lambda i: (i, 0))


def _const_spec(shape):
    nd = len(shape)
    return pl.BlockSpec(shape, lambda i: (0,) * nd)


def _load_group(parts, sems):
    @pl.when(pl.program_id(0) == 0)
    def _():
        copies = []
        for k, (gath_ref, first, n, dst) in enumerate(parts):
            for j in range(N_DEV):
                copies.append(pltpu.make_async_copy(gath_ref.at[j, pl.ds(first, n), :], dst.at[pl.ds(j * n, n), :],
                                                    sems.at[k * N_DEV + j]))
        for cp in copies:
            cp.start()
        for cp in copies:
            cp.wait()


def _hosting_call(body, name, n_steps, arrays, in_specs, out_specs, out_shape, scratch, hosted=()):
    n_in, n_out, n_scr = len(arrays), len(out_shape), len(scratch)
    h_arrays = [a for h in hosted for a in h.arrays]
    h_shapes = [s for h in hosted for s in h.out_shapes]
    h_sems = [s for h in hosted for s in h.sem_shapes]

    def full_body(*refs):
        pos = 0
        groups = []
        for n in (n_in, len(h_arrays), n_out, len(h_shapes), n_scr, len(h_sems)):
            groups.append(refs[pos:pos + n])
            pos += n
        own_in, h_in, own_out, h_out, own_scr, h_sem = groups
        per_host = []
        pi = po = ps = 0
        for h in hosted:
            ni, no, ns = len(h.arrays), len(h.out_shapes), len(h.sem_shapes)
            per_host.append((h, h_in[pi:pi + ni], h_out[po:po + no], h_sem[ps:ps + ns]))
            pi, po, ps = pi + ni, po + no, ps + ns
        for h, ins, outs, sems in per_host:
            h.begin(ins, outs, sems, n_steps)
        body(*own_in, *own_out, *own_scr)
        for h, ins, outs, sems in per_host:
            h.end(ins, outs, sems, n_steps)

    outs = pl.pallas_call(
        full_body, name=name, grid=(n_steps,),
        in_specs=list(in_specs) + [ANY] * len(h_arrays),
        out_specs=list(out_specs) + [ANY] * len(h_shapes),
        out_shape=list(out_shape) + h_shapes,
        scratch_shapes=list(scratch) + h_sems,
        compiler_params=_params(),
    )(*arrays, *h_arrays)
    return outs[:n_out], outs[n_out:]


def _my_index():
    return 4 * lax.axis_index("x") + 2 * lax.axis_index("y") + lax.axis_index("c")


GATHER_COPIES = 8


class _Gather:
    def __init__(self, shard, out, send_sems, recv_sems, local_sem):
        self.shard, self.out = shard, out
        self.send_sems, self.recv_sems, self.local_sem = send_sems, recv_sems, local_sem
        x, y, c = lax.axis_index("x"), lax.axis_index("y"), lax.axis_index("c")
        self.c = c
        self.me, self.sibling = (x, y, c), (x, y, 1 - c)
        self.xn, self.yn, self.dg = (1 - x, y), (x, 1 - y), (1 - x, 1 - y)
        self.n = shard.shape[0]
        self.half = self.n // 2
        rows_per_tile = SUBLANES * 4 // shard.dtype.itemsize
        self.relays = self.n % 2 == 0 and self.half % rows_per_tile == 0

    def _slot(self, dev, lo=0, hi=None):
        hi = self.n if hi is None else hi
        return self.out.at[4 * dev[0] + 2 * dev[1] + dev[2], pl.ds(lo, hi - lo), :]

    def _copy(self, k, block, to, src=None, lo=0, hi=None):
        return pltpu.make_async_remote_copy(
            src_ref=self._slot(block, lo, hi) if src is None else src, dst_ref=self._slot(block, lo, hi),
            send_sem=self.send_sems.at[k], recv_sem=self.recv_sems.at[k], device_id=to, device_id_type=MESH)

    def _local(self):
        return pltpu.make_async_copy(self.shard, self._slot(self.me), self.local_sem)

    def start(self):
        c = self.c
        self._local().start()
        self._copy(0, self.me, self.sibling, src=self.shard).start()
        self._copy(1, self.me, (*self.xn, c), src=self.shard).start()
        self._copy(2, self.me, (*self.yn, c), src=self.shard).start()
        if not self.relays:
            self._copy(3, self.me, (*self.dg, c), src=self.shard).start()

    def relay(self):
        c = self.c
        if self.relays:
            self._copy(1, (*self.xn, c), self.me).wait_recv()
            self._copy(3, (*self.xn, c), (*self.yn, c), hi=self.half).start()
            self._copy(2, (*self.yn, c), self.me).wait_recv()
            self._copy(4, (*self.yn, c), (*self.xn, c), lo=self.half).start()

    def forward(self):
        c = self.c
        if self.relays:
            self._copy(5, (*self.xn, c), self.sibling).start()
            self._copy(6, (*self.yn, c), self.sibling).start()
            self._copy(3, (*self.dg, c), self.me, hi=self.half).wait_recv()
            self._copy(4, (*self.dg, c), self.me, lo=self.half).wait_recv()
        else:
            self._copy(1, (*self.xn, c), self.me).wait_recv()
            self._copy(5, (*self.xn, c), self.sibling).start()
            self._copy(2, (*self.yn, c), self.me).wait_recv()
            self._copy(6, (*self.yn, c), self.sibling).start()
            self._copy(3, (*self.dg, c), self.me).wait_recv()
        self._copy(7, (*self.dg, c), self.sibling).start()

    def finish(self):
        c = self.c
        self._copy(0, self.sibling, self.me).wait_recv()
        for k, chip in ((5, self.xn), (6, self.yn), (7, self.dg)):
            self._copy(k, (*chip, 1 - c), self.me).wait_recv()
        for k in (0, 1, 2, 5, 6, 7):
            self._copy(k, self.me, self.sibling).wait_send()
        if self.relays:
            self._copy(3, self.me, self.sibling, hi=self.half).wait_send()
            self._copy(4, self.me, self.sibling, lo=self.half).wait_send()
        else:
            self._copy(3, self.me, self.sibling).wait_send()
        self._local().wait()


class _HostedGathers:
    def __init__(self, shards, mid_lead, relay_at=0.56):
        n = len(shards)
        self.arrays = shards
        self.mid_lead, self.relay_at = mid_lead, relay_at
        self.out_shapes = [jax.ShapeDtypeStruct((N_DEV,) + s.shape, s.dtype) for s in shards]
        self.sem_shapes = [pltpu.SemaphoreType.DMA((n, GATHER_COPIES)), pltpu.SemaphoreType.DMA((n, GATHER_COPIES)),
                           pltpu.SemaphoreType.DMA((n,))]

    def _gathers(self, ins, outs, sems):
        return [_Gather(ins[a], outs[a], sems[0].at[a], sems[1].at[a], sems[2].at[a]) for a in range(len(ins))]

    def begin(self, ins, outs, sems, n_steps):
        i = pl.program_id(0)
        forward_step = max(n_steps - 1 - self.mid_lead, 0)
        relay_step = min(int(self.relay_at * n_steps), forward_step)

        @pl.when(i == 0)
        def _():
            for g in self._gathers(ins, outs, sems):
                g.start()

        if n_steps == 1:
            return

        @pl.when(i == relay_step)
        def _():
            for g in self._gathers(ins, outs, sems):
                g.relay()

        @pl.when(i == forward_step)
        def _():
            for g in self._gathers(ins, outs, sems):
                g.forward()

    def end(self, ins, outs, sems, n_steps):
        @pl.when(pl.program_id(0) == n_steps - 1)
        def _():
            gathers = self._gathers(ins, outs, sems)
            if n_steps == 1:
                for g in gathers:
                    g.relay()
                for g in gathers:
                    g.forward()
            for g in gathers:
                g.finish()


def _exchange(hosted, name):
    return _hosting_call(lambda: None, name, 1, [], [], [], [], [], hosted=hosted)[1]


class _ChipScatter:
    def __init__(self, pairsum, row0, land, send_sems, recv_sems, local_sem):
        self.pairsum, self.row0, self.land = pairsum, row0, land
        self.send_sems, self.recv_sems, self.local_sem = send_sems, recv_sems, local_sem
        x, y, c = lax.axis_index("x"), lax.axis_index("y"), lax.axis_index("c")
        self.c = c
        self.chip = 2 * x + y
        self.others = [(1 - x, y), (x, 1 - y), (1 - x, 1 - y)]

    def _src(self, chip):
        return self.pairsum.at[chip, pl.ds(self.row0, self.land.shape[1]), :]

    def _copy(self, k):
        ox, oy = self.others[k]
        return pltpu.make_async_remote_copy(
            src_ref=self._src(2 * ox + oy), dst_ref=self.land.at[self.chip],
            send_sem=self.send_sems.at[k], recv_sem=self.recv_sems.at[k], device_id=(ox, oy, self.c),
            device_id_type=MESH)

    def _arrival(self, k):
        ox, oy = self.others[k]
        return pltpu.make_async_remote_copy(
            src_ref=self._src(self.chip), dst_ref=self.land.at[2 * ox + oy],
            send_sem=self.send_sems.at[k], recv_sem=self.recv_sems.at[k], device_id=(ox, oy, self.c),
            device_id_type=MESH)

    def _local(self):
        return pltpu.make_async_copy(self._src(self.chip), self.land.at[self.chip], self.local_sem)

    def start(self):
        self._local().start()
        for k in range(N_CHIP - 1):
            self._copy(k).start()

    def finish(self):
        for k in range(N_CHIP - 1):
            self._arrival(k).wait_recv()
        for k in range(N_CHIP - 1):
            self._copy(k).wait_send()
        self._local().wait()


class _HostedChipScatter:
    def __init__(self, pairsum, row0=0, n=None):
        n = pairsum.shape[1] - row0 if n is None else n
        self.row0 = row0
        self.arrays = [pairsum]
        self.out_shapes = [jax.ShapeDtypeStruct((N_CHIP, n, pairsum.shape[2]), pairsum.dtype)]
        self.sem_shapes = [pltpu.SemaphoreType.DMA((N_CHIP - 1,)), pltpu.SemaphoreType.DMA((N_CHIP - 1,)),
                           pltpu.SemaphoreType.DMA(())]

    def begin(self, ins, outs, sems, n_steps):
        @pl.when(pl.program_id(0) == 0)
        def _():
            _ChipScatter(ins[0], self.row0, outs[0], *sems).start()

    def end(self, ins, outs, sems, n_steps):
        @pl.when(pl.program_id(0) == n_steps - 1)
        def _():
            _ChipScatter(ins[0], self.row0, outs[0], *sems).finish()


def _pair_reduce(arrays, entries, name):
    n_arr, n_ent = len(arrays), len(entries)
    cols = arrays[0].shape[1]
    offsets = []
    total = 0
    for _, _, n in entries:
        offsets.append(total)
        total += n

    def body(*refs):
        ins, out_ref = refs[:n_arr], refs[n_arr]
        rbuf, own, send_sems, recv_sems, own_sems = refs[n_arr + 1:]
        q = pl.program_id(0)
        x, y, c = lax.axis_index("x"), lax.axis_index("y"), lax.axis_index("c")

        def block(e, chip, core):
            ai, first, n = entries[e]
            return ins[ai].at[pl.ds(first + (2 * chip + core) * n, n), :]

        def to_sibling(e, chip):
            return pltpu.make_async_remote_copy(
                src_ref=block(e, chip, 1 - c), dst_ref=rbuf.at[chip, pl.ds(offsets[e], entries[e][2]), :],
                send_sem=send_sems.at[e, chip], recv_sem=recv_sems.at[e, chip], device_id=(x, y, 1 - c),
                device_id_type=MESH)

        @pl.when(q == 0)
        def _():
            for chip in range(N_CHIP):
                for e in range(n_ent):
                    to_sibling(e, chip).start()

        loads = [pltpu.make_async_copy(block(e, q, c), own.at[pl.ds(offsets[e], entries[e][2]), :], own_sems.at[e])
                 for e in range(n_ent)]
        for cp in loads:
            cp.start()
        for cp in loads:
            cp.wait()
        for e in range(n_ent):
            to_sibling(e, q).wait_recv()
        out_ref[...] = (own[...].astype(F32) + rbuf[q].astype(F32)).astype(out_ref.dtype)

        @pl.when(q == N_CHIP - 1)
        def _():
            for chip in range(N_CHIP):
                for e in range(n_ent):
                    to_sibling(e, chip).wait_send()

    return pl.pallas_call(
        body, name=name, grid=(N_CHIP,),
        in_specs=[ANY] * n_arr,
        out_specs=pl.BlockSpec((None, total, cols), lambda q: (q, 0, 0)),
        out_shape=jax.ShapeDtypeStruct((N_CHIP, total, cols), BF16),
        scratch_shapes=[pltpu.VMEM((N_CHIP, total, cols), BF16), pltpu.VMEM((total, cols), BF16),
                        pltpu.SemaphoreType.DMA((n_ent, N_CHIP)), pltpu.SemaphoreType.DMA((n_ent, N_CHIP)),
                        pltpu.SemaphoreType.DMA((n_ent,))],
        compiler_params=_params(),
    )(*arrays)


class _HostedScatterAll:
    def __init__(self, packed):
        n = packed.shape[0] // N_DEV
        self.n = n
        self.arrays = [packed]
        self.out_shapes = [jax.ShapeDtypeStruct((N_DEV, n, packed.shape[1]), packed.dtype)]
        self.sem_shapes = [pltpu.SemaphoreType.DMA((N_DEV - 1,)), pltpu.SemaphoreType.DMA((N_DEV - 1,)),
                           pltpu.SemaphoreType.DMA(())]

    def _copies(self, ins, outs, sems, with_arrivals):
        src, land = ins[0], outs[0]
        send_sems, recv_sems, local_sem = sems
        me = _my_index()

        def block(p):
            return src.at[pl.ds(p * self.n, self.n), :]

        local = pltpu.make_async_copy(block(me), land.at[me], local_sem)
        sends, arrivals = [], []
        for k in range(1, N_DEV):
            p = (me + k) % N_DEV
            q = (me + N_DEV - k) % N_DEV
            sends.append(pltpu.make_async_remote_copy(
                src_ref=block(p), dst_ref=land.at[me], send_sem=send_sems.at[k - 1], recv_sem=recv_sems.at[k - 1],
                device_id=(p // 4, (p // 2) % 2, p % 2), device_id_type=MESH))
            if with_arrivals:
                arrivals.append(pltpu.make_async_remote_copy(
                    src_ref=block(me), dst_ref=land.at[q], send_sem=send_sems.at[k - 1], recv_sem=recv_sems.at[k - 1],
                    device_id=(q // 4, (q // 2) % 2, q % 2), device_id_type=MESH))
        return local, sends, arrivals

    def begin(self, ins, outs, sems, n_steps):
        @pl.when(pl.program_id(0) == 0)
        def _():
            local, sends, _ = self._copies(ins, outs, sems, with_arrivals=False)
            local.start()
            for cp in sends:
                cp.start()

    def end(self, ins, outs, sems, n_steps):
        @pl.when(pl.program_id(0) == n_steps - 1)
        def _():
            local, sends, arrivals = self._copies(ins, outs, sems, with_arrivals=True)
            for cp in arrivals:
                cp.wait_recv()
            for cp in sends:
                cp.wait_send()
            local.wait()


def _tril_weights(ws_ref):
    r = lax.broadcasted_iota(jnp.int32, (CHUNK, CHUNK), 0)
    c = lax.broadcasted_iota(jnp.int32, (CHUNK, CHUNK), 1)
    return [jnp.where(r >= c, ws_ref[h], 0.0).astype(BF16) for h in range(HEADS)]


def _sgu_stats(zpre, gain, bias):
    e = zpre.shape[1] // 2
    z, dz = _gelu_and_grad(zpre)
    u, v = z[:, :e], z[:, e:]
    vc = v - _row_mean(v)
    rstd = lax.rsqrt(_row_mean(vc * vc) + NORM_EPS)
    vhat = vc * rstd
    return u, vhat, rstd, vhat * gain + bias, dz


def _spatial_fwd(wt, vn_bf, bfull_ref, sv_ref, tm):
    for ci in range(tm // CHUNK):
        rows = slice(ci * CHUNK, (ci + 1) * CHUNK)
        for h in range(HEADS):
            cols = slice(h * GROUP, (h + 1) * GROUP)
            sv_ref[rows, cols] = _nn(wt[h], vn_bf[rows, cols]) + bfull_ref[:, cols]


def _mixer_a_fwd(x, g, gath, gain, bias, ws, bfull, tm, hosted=()):
    t_tok, d = x.shape
    e = gain.shape[1]
    e2 = 2 * e
    n_in, n_out = e2 // N_DEV, e // N_DEV

    def body(x_ref, g_ref, gain_ref, bias_ref, ws_ref, bfull_ref, gath_ref,
             xo_ref, gd_ref, u_ref, vhat_ref, svo_ref, y_ref, rstd_ref, win_v, wout_v, sv_v, sems):
        _load_group([(gath_ref, 0, n_in, win_v), (gath_ref, n_in, n_out, wout_v)], sems)
        xv = x_ref[...]
        h = _rms_fwd(xv, g_ref[...])[0].astype(BF16)
        zpre = _nt(h, win_v[...])
        u, vhat, rstd, vn, gelu_d = _sgu_stats(zpre, gain_ref[...], bias_ref[...])
        gd_ref[...] = gelu_d.astype(BF16)
        u_ref[...] = u.astype(BF16)
        vhat_ref[...] = vhat.astype(BF16)
        rstd_ref[...] = rstd
        _spatial_fwd(_tril_weights(ws_ref), vn.astype(BF16), bfull_ref, sv_v, tm)
        sv = sv_v[...]
        svo_ref[...] = sv.astype(BF16)
        y = (u * sv).astype(BF16)
        y_ref[...] = y
        xo_ref[...] = xv + _nn(y, wout_v[...])

    return _hosting_call(
        body, "mixer_a_fwd", t_tok // tm, [x, g, gain, bias, ws, bfull, gath],
        in_specs=[_row_spec(tm, d), _const_spec((1, d)), _const_spec((1, e)), _const_spec((1, e)),
                  _const_spec((HEADS, CHUNK, CHUNK)), _const_spec((CHUNK, e)), ANY],
        out_specs=[_row_spec(tm, d), _row_spec(tm, e2), _row_spec(tm, e), _row_spec(tm, e), _row_spec(tm, e),
                   _row_spec(tm, e), _row_spec(tm, 1)],
        out_shape=[jax.ShapeDtypeStruct((t_tok, d), F32), jax.ShapeDtypeStruct((t_tok, e2), BF16),
                   jax.ShapeDtypeStruct((t_tok, e), BF16), jax.ShapeDtypeStruct((t_tok, e), BF16),
                   jax.ShapeDtypeStruct((t_tok, e), BF16), jax.ShapeDtypeStruct((t_tok, e), BF16),
                   jax.ShapeDtypeStruct((t_tok, 1), F32)],
        scratch=[pltpu.VMEM((e2, d), BF16), pltpu.VMEM((e, d), BF16), pltpu.VMEM((tm, e), F32),
                 pltpu.SemaphoreType.DMA((2 * N_DEV,))],
        hosted=hosted)


def _mixer_a_bwd(dout, x, gd, u_sav, vhat_sav, sv_sav, rstd_sav, g, gath, gain, bias, ws, tm, hosted=()):
    t_tok, d = x.shape
    e = gain.shape[1]
    e2 = 2 * e
    n_in, n_out = e2 // N_DEV, e // N_DEV
    n_steps = t_tok // tm

    def body(dout_ref, x_ref, gd_ref, u_ref, vhat_ref, sv_ref, rstd_ref, g_ref, gain_ref, bias_ref, ws_ref, gath_ref,
             dx_ref, dxb_ref, h_ref, dz_ref, dg_ref, dgain_ref, dbias_ref, dws_ref, dbso_ref,
             win_v, wout_v, dvn_v, dbs_ref, sems):
        i = pl.program_id(0)
        _load_group([(gath_ref, 0, n_in, win_v), (gath_ref, n_in, n_out, wout_v)], sems)

        @pl.when(i == 0)
        def _():
            dg_ref[...] = jnp.zeros_like(dg_ref)
            dgain_ref[...] = jnp.zeros_like(dgain_ref)
            dbias_ref[...] = jnp.zeros_like(dbias_ref)
            dws_ref[...] = jnp.zeros_like(dws_ref)
            dbs_ref[...] = jnp.zeros_like(dbs_ref)

        xv = x_ref[...]
        gv = g_ref[...]
        hv, xhat, r = _rms_fwd(xv, gv)
        h_ref[...] = hv.astype(BF16)
        gain_v = gain_ref[...]
        vhat = vhat_ref[...].astype(F32)
        vn_bf = (vhat * gain_v + bias_ref[...]).astype(BF16)
        wt = _tril_weights(ws_ref)

        dov = dout_ref[...]
        dy = _nt(dov.astype(BF16), wout_v[...])
        du = dy * sv_ref[...].astype(F32)
        dsv = dy * u_ref[...].astype(F32)
        dsv_bf = dsv.astype(BF16)
        for ci in range(tm // CHUNK):
            rows = slice(ci * CHUNK, (ci + 1) * CHUNK)
            dbs_ref[...] += dsv[rows, :]
            for h in range(HEADS):
                cols = slice(h * GROUP, (h + 1) * GROUP)
                dvn_v[rows, cols] = _tn(wt[h], dsv_bf[rows, cols])
                dws_ref[h] += _nt(dsv_bf[rows, cols], vn_bf[rows, cols])
        dvn = dvn_v[...]
        dgain_ref[...] += _col_sum(dvn * vhat)
        dbias_ref[...] += _col_sum(dvn)
        dvhat = dvn * gain_v
        dv = rstd_ref[...] * (dvhat - _row_mean(dvhat) - vhat * _row_mean(dvhat * vhat))
        dzpre = (jnp.concatenate([du, dv], axis=1) * gd_ref[...].astype(F32)).astype(BF16)
        dz_ref[...] = dzpre
        dh = _nn(dzpre, win_v[...])
        dxr, dg_row = _rms_bwd(dh, xhat, r, gv)
        dg_ref[...] += dg_row
        dx = dov + dxr
        dx_ref[...] = dx
        dxb_ref[...] = dx.astype(BF16)

        @pl.when(i == n_steps - 1)
        def _():
            rr = lax.broadcasted_iota(jnp.int32, (CHUNK, CHUNK), 0)
            cc = lax.broadcasted_iota(jnp.int32, (CHUNK, CHUNK), 1)
            for h in range(HEADS):
                dws_ref[h] = jnp.where(rr >= cc, dws_ref[h], 0.0)
                dbso_ref[h] = jnp.sum(dbs_ref[:, h * GROUP:(h + 1) * GROUP], axis=1, keepdims=True)

    return _hosting_call(
        body, "mixer_a_bwd", n_steps, [dout, x, gd, u_sav, vhat_sav, sv_sav, rstd_sav, g, gain, bias, ws, gath],
        in_specs=[_row_spec(tm, d), _row_spec(tm, d), _row_spec(tm, e2), _row_spec(tm, e), _row_spec(tm, e),
                  _row_spec(tm, e), _row_spec(tm, 1), _const_spec((1, d)),
                  _const_spec((1, e)), _const_spec((1, e)), _const_spec((HEADS, CHUNK, CHUNK)), ANY],
        out_specs=[_row_spec(tm, d), _row_spec(tm, d), _row_spec(tm, d), _row_spec(tm, e2),
                   _const_spec((1, d)), _const_spec((1, e)), _const_spec((1, e)),
                   _const_spec((HEADS, CHUNK, CHUNK)), _const_spec((HEADS, CHUNK, 1))],
        out_shape=[jax.ShapeDtypeStruct((t_tok, d), F32), jax.ShapeDtypeStruct((t_tok, d), BF16),
                   jax.ShapeDtypeStruct((t_tok, d), BF16), jax.ShapeDtypeStruct((t_tok, e2), BF16),
                   jax.ShapeDtypeStruct((1, d), F32), jax.ShapeDtypeStruct((1, e), F32),
                   jax.ShapeDtypeStruct((1, e), F32), jax.ShapeDtypeStruct((HEADS, CHUNK, CHUNK), F32),
                   jax.ShapeDtypeStruct((HEADS, CHUNK, 1), F32)],
        scratch=[pltpu.VMEM((e2, d), BF16), pltpu.VMEM((e, d), BF16), pltpu.VMEM((tm, e), F32),
                 pltpu.VMEM((CHUNK, e), F32), pltpu.SemaphoreType.DMA((2 * N_DEV,))],
        hosted=hosted)


def _ffn_fwd(x, g, srcs, nf, tm, name, hosted=(), head=None):
    t_tok, d = x.shape
    f = nf * N_DEV
    firsts = [first for _, first in srcs]
    n_head = 2 if head else 0

    def body(*refs):
        x_ref, g_ref, sg_ref, su_ref, sd_ref = refs[:5]
        gate_ref, up_ref, wg_v, wu_v, wd_v, sems = refs[-6:]
        _load_group(
            [(sg_ref, firsts[0], nf, wg_v), (su_ref, firsts[1], nf, wu_v), (sd_ref, firsts[2], nf, wd_v)], sems)
        if head:
            t_ref, gf_ref, loss_ref, dx_ref, dxb_ref, dgf_ref = refs[5:11]

            @pl.when(pl.program_id(0) == 0)
            def _():
                loss_ref[...] = jnp.zeros_like(loss_ref)
                dgf_ref[...] = jnp.zeros_like(dgf_ref)

        xv = x_ref[...]
        h = _rms_fwd(xv, g_ref[...])[0].astype(BF16)
        gate = _nt(h, wg_v[...])
        up = _nt(h, wu_v[...])
        gate_ref[...] = gate.astype(BF16)
        up_ref[...] = up.astype(BF16)
        act = (gate * _sigmoid(gate) * up).astype(BF16)
        xo = xv + _nn(act, wd_v[...])
        if head:
            gfv = gf_ref[...]
            y, xhat, r = _rms_fwd(xo, gfv)
            err = y - t_ref[...]
            loss_ref[...] += 0.5 * jnp.sum(_row_mean(err * err), axis=0, keepdims=True)
            dxr, dg_row = _rms_bwd(err * (1.0 / d), xhat, r, gfv)
            dgf_ref[...] += dg_row
            dx_ref[...] = dxr
            dxb_ref[...] = dxr.astype(BF16)
        else:
            refs[5][...] = xo

    act_specs = [_row_spec(tm, f), _row_spec(tm, f)]
    act_shapes = [jax.ShapeDtypeStruct((t_tok, f), BF16), jax.ShapeDtypeStruct((t_tok, f), BF16)]
    if head:
        out_specs = [_const_spec((1, 1)), _row_spec(tm, d), _row_spec(tm, d), _const_spec((1, d))]
        out_shape = [jax.ShapeDtypeStruct((1, 1), F32), jax.ShapeDtypeStruct((t_tok, d), F32),
                     jax.ShapeDtypeStruct((t_tok, d), BF16), jax.ShapeDtypeStruct((1, d), F32)]
    else:
        out_specs = [_row_spec(tm, d)]
        out_shape = [jax.ShapeDtypeStruct((t_tok, d), F32)]
    return _hosting_call(
        body, name, t_tok // tm, [x, g] + [arr for arr, _ in srcs] + list(head or ()),
        in_specs=[_row_spec(tm, d), _const_spec((1, d)), ANY, ANY, ANY] + [_row_spec(tm, d), _const_spec((1, d))][:n_head],
        out_specs=out_specs + act_specs, out_shape=out_shape + act_shapes,
        scratch=[pltpu.VMEM((f, d), BF16), pltpu.VMEM((f, d), BF16), pltpu.VMEM((f, d), BF16),
                 pltpu.SemaphoreType.DMA((3 * N_DEV,))],
        hosted=hosted)


def _ffn_bwd(dout, x, gate, up, g, srcs, nf, tm, name, hosted=()):
    t_tok, d = x.shape
    f = nf * N_DEV
    firsts = [first for _, first in srcs]
    per_chunk = -(-f // (FFN_CHUNKS * MXU_WIDTH)) * MXU_WIDTH
    bounds = [min(ck * per_chunk, f) for ck in range(FFN_CHUNKS + 1)]

    def body(dout_ref, x_ref, gate_ref, up_ref, g_ref, sg_ref, su_ref, sd_ref,
             dx_ref, dxb_ref, h_ref, act_ref, dgu_ref, dg_ref, wg_v, wu_v, wd_v, sems):
        _load_group(
            [(sg_ref, firsts[0], nf, wg_v), (su_ref, firsts[1], nf, wu_v), (sd_ref, firsts[2], nf, wd_v)], sems)

        @pl.when(pl.program_id(0) == 0)
        def _():
            dg_ref[...] = jnp.zeros_like(dg_ref)

        xv = x_ref[...]
        gv = g_ref[...]
        hv, xhat, r = _rms_fwd(xv, gv)
        h_ref[...] = hv.astype(BF16)
        dov = dout_ref[...]
        dob = dov.astype(BF16)
        dh = None
        for ck in range(FFN_CHUNKS):
            cols = slice(bounds[ck], bounds[ck + 1])
            gate_v = gate_ref[:, cols].astype(F32)
            up_v = up_ref[:, cols].astype(F32)
            sig = _sigmoid(gate_v)
            silu = gate_v * sig
            act_ref[:, cols] = (silu * up_v).astype(BF16)
            dact = _nt(dob, wd_v[cols, :])
            dup = (dact * silu).astype(BF16)
            dgate = (dact * up_v * (sig * (1.0 + gate_v * (1.0 - sig)))).astype(BF16)
            dgu_ref[:, cols] = dgate
            dgu_ref[:, f + bounds[ck]:f + bounds[ck + 1]] = dup
            part = _nn(dgate, wg_v[cols, :]) + _nn(dup, wu_v[cols, :])
            dh = part if dh is None else dh + part
        dxr, dg_row = _rms_bwd(dh, xhat, r, gv)
        dg_ref[...] += dg_row
        dx = dov + dxr
        dx_ref[...] = dx
        dxb_ref[...] = dx.astype(BF16)

    return _hosting_call(
        body, name, t_tok // tm, [dout, x, gate, up, g] + [arr for arr, _ in srcs],
        in_specs=[_row_spec(tm, d), _row_spec(tm, d), _row_spec(tm, f), _row_spec(tm, f), _const_spec((1, d)),
                  ANY, ANY, ANY],
        out_specs=[_row_spec(tm, d), _row_spec(tm, d), _row_spec(tm, d), _row_spec(tm, f), _row_spec(tm, 2 * f),
                   _const_spec((1, d))],
        out_shape=[jax.ShapeDtypeStruct((t_tok, d), F32), jax.ShapeDtypeStruct((t_tok, d), BF16),
                   jax.ShapeDtypeStruct((t_tok, d), BF16), jax.ShapeDtypeStruct((t_tok, f), BF16),
                   jax.ShapeDtypeStruct((t_tok, 2 * f), BF16), jax.ShapeDtypeStruct((1, d), F32)],
        scratch=[pltpu.VMEM((f, d), BF16), pltpu.VMEM((f, d), BF16), pltpu.VMEM((f, d), BF16),
                 pltpu.SemaphoreType.DMA((3 * N_DEV,))],
        hosted=hosted)


def _shift_down(z, k, prev_rows):
    row = lax.broadcasted_iota(jnp.int32, z.shape, 0)
    out = pltpu.roll(z, k, 0)
    for j in range(k):
        out = jnp.where(row == j, prev_rows[j], out)
    return out


def _shift_up(z, k, next_rows):
    tm = z.shape[0]
    row = lax.broadcasted_iota(jnp.int32, z.shape, 0)
    out = pltpu.roll(z, tm - k, 0)
    for j in range(k):
        out = jnp.where(row == tm - k + j, next_rows[j], out)
    return out


def _mixer_b_fwd(x, g, gath, conv_w, tm, seq, hosted=()):
    t_tok, d = x.shape
    e = conv_w.shape[1]
    e3 = 3 * e
    n_in, n_out = e3 // N_DEV, e // N_DEV
    tiles_per_seq = seq // tm

    def body(x_ref, g_ref, cw_ref, gath_ref, xo_ref, p_ref, win_v, wout_v, tail_v, sems):
        i = pl.program_id(0)
        _load_group([(gath_ref, 0, n_in, win_v), (gath_ref, n_in, n_out, wout_v)], sems)

        @pl.when(i % tiles_per_seq == 0)
        def _():
            tail_v[...] = jnp.zeros_like(tail_v)

        xv = x_ref[...]
        h = _rms_fwd(xv, g_ref[...])[0].astype(BF16)
        p = _nt(h, win_v[...])
        p_ref[...] = p.astype(BF16)
        z = p[:, e:2 * e] * p[:, 2 * e:]
        prev = [tail_v[SUBLANES - 2:SUBLANES - 1, :], tail_v[SUBLANES - 1:SUBLANES, :]]
        conv = (cw_ref[2:3, :] * z + cw_ref[1:2, :] * _shift_down(z, 1, prev[1:])
                + cw_ref[0:1, :] * _shift_down(z, 2, prev))
        tail_v[...] = z[tm - SUBLANES:, :]
        y = (p[:, :e] * conv).astype(BF16)
        xo_ref[...] = xv + _nn(y, wout_v[...])

    return _hosting_call(
        body, "mixer_b_fwd", t_tok // tm, [x, g, conv_w, gath],
        in_specs=[_row_spec(tm, d), _const_spec((1, d)), _const_spec((SUBLANES, e)), ANY],
        out_specs=[_row_spec(tm, d), _row_spec(tm, e3)],
        out_shape=[jax.ShapeDtypeStruct((t_tok, d), F32), jax.ShapeDtypeStruct((t_tok, e3), BF16)],
        scratch=[pltpu.VMEM((e3, d), BF16), pltpu.VMEM((e, d), BF16), pltpu.VMEM((SUBLANES, e), F32),
                 pltpu.SemaphoreType.DMA((2 * N_DEV,))],
        hosted=hosted)


def _mixer_b_bwd(dout, x, p, g, gath, conv_w, tm, seq, hosted=()):
    t_tok, d = x.shape
    e = conv_w.shape[1]
    e3 = 3 * e
    n_in, n_out = e3 // N_DEV, e // N_DEV
    tiles_per_seq = seq // tm
    halo_per_tile = tm // HALO
    n_halo = t_tok // HALO

    def body(dout_ref, dnext_ref, x_ref, p_ref, pprev_ref, pnext_ref, g_ref, cw_ref, gath_ref,
             dx_ref, dxb_ref, h_ref, y_ref, dp_ref, dg_ref, dcw_ref, win_v, wout_v, sems):
        i = pl.program_id(0)
        _load_group([(gath_ref, 0, n_in, win_v), (gath_ref, n_in, n_out, wout_v)], sems)

        @pl.when(i == 0)
        def _():
            dg_ref[...] = jnp.zeros_like(dg_ref)
            dcw_ref[...] = jnp.zeros_like(dcw_ref)

        first = (i % tiles_per_seq == 0).astype(F32)
        last = (i % tiles_per_seq == tiles_per_seq - 1).astype(F32)
        xv = x_ref[...]
        gv = g_ref[...]
        hv, xhat, r = _rms_fwd(xv, gv)
        h_ref[...] = hv.astype(BF16)
        pv = p_ref[...].astype(F32)
        bg, cg, hx = pv[:, :e], pv[:, e:2 * e], pv[:, 2 * e:]
        z = cg * hx
        pprev = pprev_ref[...].astype(F32)
        zprev = pprev[:, e:2 * e] * pprev[:, 2 * e:] * (1.0 - first)
        prev = [zprev[HALO - 2:HALO - 1, :], zprev[HALO - 1:HALO, :]]
        zs1 = _shift_down(z, 1, prev[1:])
        zs2 = _shift_down(z, 2, prev)
        w0, w1, w2 = cw_ref[0:1, :], cw_ref[1:2, :], cw_ref[2:3, :]
        conv = w2 * z + w1 * zs1 + w0 * zs2
        y_ref[...] = (bg * conv).astype(BF16)

        dov = dout_ref[...]
        wout_bf = wout_v[...]
        dy = _nt(dov.astype(BF16), wout_bf)
        dconv = dy * bg
        dnext = _nt(dnext_ref[...].astype(BF16), wout_bf) * pnext_ref[:, :e].astype(F32) * (1.0 - last)
        nxt = [dnext[0:1, :], dnext[1:2, :]]
        dz = w2 * dconv + w1 * _shift_up(dconv, 1, nxt[:1]) + w0 * _shift_up(dconv, 2, nxt)
        dcw_ref[0:1, :] += _col_sum(dconv * zs2)
        dcw_ref[1:2, :] += _col_sum(dconv * zs1)
        dcw_ref[2:3, :] += _col_sum(dconv * z)
        dp = jnp.concatenate([dy * conv, dz * hx, dz * cg], axis=1).astype(BF16)
        dp_ref[...] = dp
        dh = _nn(dp, win_v[...])
        dxr, dg_row = _rms_bwd(dh, xhat, r, gv)
        dg_ref[...] += dg_row
        dx = dov + dxr
        dx_ref[...] = dx
        dxb_ref[...] = dx.astype(BF16)

    prev_spec = lambda w: pl.BlockSpec((HALO, w), lambda i: (jnp.maximum(i * halo_per_tile - 1, 0), 0))
    next_spec = lambda w: pl.BlockSpec((HALO, w), lambda i: (jnp.minimum((i + 1) * halo_per_tile, n_halo - 1), 0))
    return _hosting_call(
        body, "mixer_b_bwd", t_tok // tm, [dout, dout, x, p, p, p, g, conv_w, gath],
        in_specs=[_row_spec(tm, d), next_spec(d), _row_spec(tm, d), _row_spec(tm, e3), prev_spec(e3), next_spec(e3),
                  _const_spec((1, d)), _const_spec((SUBLANES, e)), ANY],
        out_specs=[_row_spec(tm, d), _row_spec(tm, d), _row_spec(tm, d), _row_spec(tm, e), _row_spec(tm, e3),
                   _const_spec((1, d)), _const_spec((SUBLANES, e))],
        out_shape=[jax.ShapeDtypeStruct((t_tok, d), F32), jax.ShapeDtypeStruct((t_tok, d), BF16),
                   jax.ShapeDtypeStruct((t_tok, d), BF16), jax.ShapeDtypeStruct((t_tok, e), BF16),
                   jax.ShapeDtypeStruct((t_tok, e3), BF16), jax.ShapeDtypeStruct((1, d), F32),
                   jax.ShapeDtypeStruct((SUBLANES, e), F32)],
        scratch=[pltpu.VMEM((e3, d), BF16), pltpu.VMEM((e, d), BF16), pltpu.SemaphoreType.DMA((2 * N_DEV,))],
        hosted=hosted)


def _wgrad(a, b, bm, name, hosted=()):
    t_tok, m = a.shape
    n = b.shape[1]

    def body(a_ref, b_ref, o_ref):
        o_ref[...] = _tn(a_ref[...], b_ref[...]).astype(o_ref.dtype)

    outs, h_outs = _hosting_call(
        body, name, m // bm, [a, b],
        in_specs=[pl.BlockSpec((t_tok, bm), lambda i: (0, i)), _const_spec((t_tok, n))],
        out_specs=[pl.BlockSpec((bm, n), lambda i: (i, 0))],
        out_shape=[jax.ShapeDtypeStruct((m, n), BF16)],
        scratch=[], hosted=hosted)
    return (outs[0], h_outs) if hosted else outs[0]


def _sum_slots(land, rb, name):
    n_slots, rows, cols = land.shape

    def body(l_ref, o_ref):
        acc = l_ref[0].astype(F32)
        for k in range(1, n_slots):
            acc = acc + l_ref[k].astype(F32)
        o_ref[...] = acc

    return pl.pallas_call(
        body, name=name, grid=(rows // rb,),
        in_specs=[pl.BlockSpec((n_slots, rb, cols), lambda i: (0, i, 0))],
        out_specs=pl.BlockSpec((rb, cols), lambda i: (i, 0)),
        out_shape=jax.ShapeDtypeStruct((rows, cols), F32),
        compiler_params=_params(sequential=False),
    )(land)


def _adamw(w, grad, m, v, rb, name):
    rows, cols = w.shape
    c1 = 1.0 / (1.0 - ADAM_B1 ** ADAM_STEP)
    c2 = 1.0 / (1.0 - ADAM_B2 ** ADAM_STEP)

    def body(w_ref, g_ref, m_ref, v_ref, d_ref, mo_ref, vo_ref):
        gv = g_ref[...]
        mn = ADAM_B1 * m_ref[...] + (1.0 - ADAM_B1) * gv
        vn = ADAM_B2 * v_ref[...] + (1.0 - ADAM_B2) * (gv * gv)
        mo_ref[...] = mn
        vo_ref[...] = vn
        d_ref[...] = -ADAM_LR * ((mn * c1) / (jnp.sqrt(vn * c2) + ADAM_EPS) + ADAM_WD * w_ref[...])

    spec = pl.BlockSpec((rb, cols), lambda i: (i, 0))
    shape = jax.ShapeDtypeStruct((rows, cols), F32)
    return pl.pallas_call(
        body, name=name, grid=(rows // rb,),
        in_specs=[spec] * 4, out_specs=[spec] * 3, out_shape=[shape] * 3,
        compiler_params=_params(sequential=False),
    )(w, grad, m, v)


def _pack_shards(groups, name, hosted=()):
    flat = [(part, layer) for group in groups for part, layer, _ in group]
    rows = [[p.shape[2] if turn else p.shape[1] for p, _, turn in group] for group in groups]
    first, _, first_turn = groups[0][0]
    width = first.shape[1] if first_turn else first.shape[2]

    def body(*refs):
        ins, outs = refs[:len(flat)], refs[len(flat):]
        k = 0
        for gi, group in enumerate(groups):
            off = 0
            for (_, _, turn), n in zip(group, rows[gi]):
                part = ins[k][...].astype(BF16)
                if turn:
                    r = lax.broadcasted_iota(jnp.int32, (n, n), 0)
                    c = lax.broadcasted_iota(jnp.int32, (n, n), 1)
                    part = _nt((r == c).astype(BF16), part).astype(BF16)
                outs[gi][off:off + n, :] = part
                off += n
                k += 1

    return _hosting_call(
        body, name, 1, [p for p, _ in flat],
        in_specs=[pl.BlockSpec((None,) + p.shape[1:], lambda i, layer=layer: (layer, 0, 0)) for p, layer in flat],
        out_specs=[_const_spec((sum(r), width)) for r in rows],
        out_shape=[jax.ShapeDtypeStruct((sum(r), width), BF16) for r in rows],
        scratch=[], hosted=hosted)


def _split_bf16(a):
    hi = a.astype(BF16)
    rest = a - hi.astype(F32)
    mid = rest.astype(BF16)
    return hi, mid, (rest - mid.astype(F32)).astype(BF16)


def _reduce_adamw(lands, w, m, v, transpose, name, hosted=()):
    n_layers, rows_w, cols_w = w.shape
    c1 = 1.0 / (1.0 - ADAM_B1 ** ADAM_STEP)
    c2 = 1.0 / (1.0 - ADAM_B2 ** ADAM_STEP)
    flat = [piece for pieces in lands for piece in pieces]
    counts = [len(pieces) for pieces in lands]
    if transpose:
        tiles = rows_w // MXU_WIDTH
        blk = (MXU_WIDTH, cols_w)
        land_specs = [pl.BlockSpec((N_CHIP, n, MXU_WIDTH), lambda i, b=first // n: (0, b, i % tiles))
                      for _, first, n in flat]
        for _, first, n in flat:
            assert first % n == 0
    else:
        tiles = 2
        blk = (rows_w // tiles, cols_w)
        assert all(c == 1 for c in counts)
        land_specs = [pl.BlockSpec((N_CHIP,) + blk, lambda i, b=first // blk[0]: (0, b + i % tiles, 0))
                      for _, first, _ in flat]
        for _, first, _ in flat:
            assert first % blk[0] == 0

    def body(*refs):
        land_refs = refs[:len(flat)]
        w_ref, m_ref, v_ref, g_ref, d_ref, mo_ref, vo_ref = refs[len(flat):]
        layer = pl.program_id(0) // tiles

        def total(ref):
            acc = ref[0].astype(F32)
            for q in range(1, N_CHIP):
                acc = acc + ref[q].astype(F32)
            return acc

        def layer_sum(k):
            first = sum(counts[:k])
            parts = [total(land_refs[first + j]) for j in range(counts[k])]
            return parts[0] if len(parts) == 1 else jnp.concatenate(parts, axis=0)

        gv = layer_sum(0)
        for k in range(1, n_layers):
            gv = jnp.where(layer == k, layer_sum(k), gv)
        if transpose:
            r = lax.broadcasted_iota(jnp.int32, (MXU_WIDTH, MXU_WIDTH), 0)
            c = lax.broadcasted_iota(jnp.int32, (MXU_WIDTH, MXU_WIDTH), 1)
            eye = (r == c).astype(BF16)
            hi, mid, lo = _split_bf16(gv)
            gv = _nt(eye, hi) + _nt(eye, mid) + _nt(eye, lo)
        g_ref[...] = gv
        mn = ADAM_B1 * m_ref[...] + (1.0 - ADAM_B1) * gv
        vn = ADAM_B2 * v_ref[...] + (1.0 - ADAM_B2) * (gv * gv)
        mo_ref[...] = mn
        vo_ref[...] = vn
        d_ref[...] = -ADAM_LR * ((mn * c1) / (jnp.sqrt(vn * c2) + ADAM_EPS) + ADAM_WD * w_ref[...])

    spec = pl.BlockSpec((None,) + blk, lambda i: (i // tiles, i % tiles, 0))
    shape = jax.ShapeDtypeStruct(w.shape, F32)
    outs, h_outs = _hosting_call(
        body, name, n_layers * tiles, [land for land, _, _ in flat] + [w, m, v],
        in_specs=land_specs + [spec] * 3, out_specs=[spec] * 4, out_shape=[shape] * 4, scratch=[], hosted=hosted)
    return (outs, h_outs) if hosted else outs


def _pack_small(parts, rows):
    flat = jnp.concatenate([p.reshape(-1).astype(F32) for p in parts])
    return jnp.pad(flat, (0, rows * LANES - flat.shape[0])).reshape(rows, LANES)


def _unpack_small(packed, shapes):
    flat = packed.reshape(-1)
    out = []
    pos = 0
    for s in shapes:
        n = math.prod(s)
        out.append(flat[pos:pos + n].reshape(s))
        pos += n
    return out


def kernel(x, mix_norm, ffn_norm, a_w_in, a_v_gain, a_v_bias, a_w_s, a_b_s, a_w_out, b_w_in, b_conv_w, b_w_out, ffn_w_gate, ffn_w_up, ffn_w_down, final_norm, loss_target, m_mix_norm, m_ffn_norm, m_a_w_in, m_a_v_gain, m_a_v_bias, m_a_w_s, m_a_b_s, m_a_w_out, m_b_w_in, m_b_conv_w, m_b_w_out, m_ffn_w_gate, m_ffn_w_up, m_ffn_w_down, m_final_norm, v_mix_norm, v_ffn_norm, v_a_w_in, v_a_v_gain, v_a_v_bias, v_a_w_s, v_a_b_s, v_a_w_out, v_b_w_in, v_b_conv_w, v_b_w_out, v_ffn_w_gate, v_ffn_w_up, v_ffn_w_down, v_final_norm):
    bsz, seq, d = x.shape
    t_tok = bsz * seq
    me = _my_index()
    xt = x.reshape(t_tok, d)
    target = loss_target.reshape(t_tok, d)
    e_a = a_v_gain.shape[1]
    e_b = b_w_out.shape[1] * N_DEV
    n_layers = ffn_w_gate.shape[0]
    f_shard = ffn_w_gate.shape[2]
    f_full = f_shard * N_DEV

    conv_pad = jnp.pad(b_conv_w[0], ((0, SUBLANES - CONV_W), (0, 0)))
    sh_a = jnp.concatenate([a_w_in[0].T, a_w_out[0]]).astype(BF16)
    bfull = jnp.repeat(a_b_s[0].T, GROUP, axis=1)

    gate_t, up_t = ffn_w_gate.transpose(0, 2, 1), ffn_w_up.transpose(0, 2, 1)
    (sh_b, sh_f0, sh_f1g, sh_f1ud), (gath_a, conv_g) = _pack_shards(
        [[(b_w_in, 0, True), (b_w_out, 0, False)],
         [(gate_t, 0, False), (up_t, 0, False), (ffn_w_down, 0, False)],
         [(gate_t, 1, False)],
         [(up_t, 1, False), (ffn_w_down, 1, False)]],
        "pack_shards", hosted=[_HostedGathers([sh_a, conv_pad], 0)])
    conv_full = jnp.pad(conv_g[:, :CONV_W, :].transpose(1, 0, 2).reshape(CONV_W, e_b), ((0, SUBLANES - CONV_W), (0, 0)))
    (x1, gd_a, u_a, vhat_a, sv_a, y_a, rstd_a), (gath_f0,) = _mixer_a_fwd(
        xt, mix_norm[0:1], gath_a, a_v_gain, a_v_bias, a_w_s[0], bfull, tm=256,
        hosted=[_HostedGathers([sh_f0], mid_lead=2)])
    srcs0 = [(gath_f0, 0), (gath_f0, f_shard), (gath_f0, 2 * f_shard)]
    (x2, gate0, up0), (gath_b, gath_f1g) = _ffn_fwd(x1, ffn_norm[0:1], srcs0, f_shard, tm=512, name="ffn_fwd0",
                                                    hosted=[_HostedGathers([sh_b, sh_f1g], mid_lead=2)])
    (x3, p_b), (gath_f1ud,) = _mixer_b_fwd(x2, mix_norm[1:2], gath_b, conv_full, tm=512, seq=seq,
                                           hosted=[_HostedGathers([sh_f1ud], mid_lead=2)])
    srcs1 = [(gath_f1g, 0), (gath_f1ud, 0), (gath_f1ud, f_shard)]
    (loss_part, dx4, dx4_bf, d_final, gate1, up1), _ = _ffn_fwd(
        x3, ffn_norm[1:2], srcs1, f_shard, tm=512, name="ffn_fwd1", head=(target, final_norm.reshape(1, d)))

    ffn_entries = [(0, 0, f_shard), (0, f_full, f_shard), (1, 0, f_shard)]
    (dx3, dx3_bf, h_f1, act1, dgu1, d_fn1), _ = _ffn_bwd(dx4, x3, gate1, up1, ffn_norm[1:2], srcs1, f_shard, tm=256,
                                                         name="ffn_bwd1")
    g_down1 = _wgrad(act1, dx4_bf, 256, "wgrad_down1")
    g_gu1 = _wgrad(dgu1, h_f1, 512, "wgrad_gate_up1")
    ps_f1 = _pair_reduce([g_gu1, g_down1], ffn_entries, "pair_reduce_f1")
    (dx2, dx2_bf, h_b, y_b, dp_b, d_mn1, d_conv), (land_f1gu,) = _mixer_b_bwd(
        dx3, x2, p_b, mix_norm[1:2], gath_b, conv_full, tm=512, seq=seq,
        hosted=[_HostedChipScatter(ps_f1, 0, 2 * f_shard)])
    g_b_out = _wgrad(y_b, dx3_bf, 256, "wgrad_b_out")
    g_b_in = _wgrad(dp_b, h_b, 512, "wgrad_b_in")
    ps_b = _pair_reduce([g_b_in, g_b_out], [(0, 0, b_w_in.shape[2]), (1, 0, b_w_out.shape[1])], "pair_reduce_b")
    (dx1, dx1_bf, h_f0, act0, dgu0, d_fn0), (land_f1d, land_b) = _ffn_bwd(
        dx2, x1, gate0, up0, ffn_norm[0:1], srcs0, f_shard, tm=256, name="ffn_bwd0",
        hosted=[_HostedChipScatter(ps_f1, 2 * f_shard, f_shard), _HostedChipScatter(ps_b)])
    g_down0 = _wgrad(act0, dx2_bf, 256, "wgrad_down0")
    g_gu0 = _wgrad(dgu0, h_f0, 512, "wgrad_gate_up0")
    ps_f0 = _pair_reduce([g_gu0, g_down0], ffn_entries, "pair_reduce_f0")
    g_a_out = _wgrad(y_a, dx1_bf, 256, "wgrad_a_out")
    ps_ao = _pair_reduce([g_a_out], [(0, 0, a_w_out.shape[1])], "pair_reduce_a_out")
    (dx0, _, h_a, dz_a, d_mn0, d_gain, d_bias, d_ws, d_bs_acc), (land_f0, land_ao) = _mixer_a_bwd(
        dx1, xt, gd_a, u_a, vhat_a, sv_a, rstd_a, mix_norm[0:1], gath_a, a_v_gain, a_v_bias, a_w_s[0], tm=256,
        hosted=[_HostedChipScatter(ps_f0), _HostedChipScatter(ps_ao)])
    d_bs = d_bs_acc.reshape(HEADS, CHUNK)

    small_grads = [jnp.concatenate([d_mn0, d_mn1]), jnp.concatenate([d_fn0, d_fn1]), d_gain, d_bias, d_ws, d_bs,
                   d_final, d_conv[:CONV_W], loss_part]
    small_shapes = [(n_layers, d), (n_layers, d), (1, e_a), (1, e_a), (1, HEADS, CHUNK, CHUNK), (1, HEADS, CHUNK), (d,),
                    (CONV_W, e_b), ()]
    n_small = sum(math.prod(s) for s in small_shapes)
    blk_rows = -(-n_small // (N_DEV * LANES * SUBLANES)) * SUBLANES
    small_rows = blk_rows * N_DEV
    packed = _pack_small(small_grads, small_rows)
    g_a_in, (small_land,) = _wgrad(dz_a, h_a, 512, "wgrad_a_in", hosted=[_HostedScatterAll(packed)])
    ps_ai = _pair_reduce([g_a_in], [(0, 0, a_w_in.shape[2])], "pair_reduce_a_in")
    small_sum = _sum_slots(small_land, blk_rows, "sum_small")

    land_ai, small_gath = _exchange([_HostedChipScatter(ps_ai), _HostedGathers([small_sum], 0)], "tail_exchange")
    small_all = small_gath.reshape(small_rows, LANES)

    n_b_in = b_w_in.shape[2]
    gate_out = _reduce_adamw([[(land_f0, 0, f_shard)], [(land_f1gu, 0, f_shard)]], gate_t,
                             m_ffn_w_gate.transpose(0, 2, 1), v_ffn_w_gate.transpose(0, 2, 1), False, "adamw_gate")
    up_out = _reduce_adamw([[(land_f0, f_shard, f_shard)], [(land_f1gu, f_shard, f_shard)]], up_t,
                           m_ffn_w_up.transpose(0, 2, 1), v_ffn_w_up.transpose(0, 2, 1), False, "adamw_up")
    res = {
        "a_w_in": _reduce_adamw([[(land_ai, 0, a_w_in.shape[2])]], a_w_in, m_a_w_in, v_a_w_in, True, "adamw_a_in"),
        "a_w_out": _reduce_adamw([[(land_ao, 0, a_w_out.shape[1])]], a_w_out, m_a_w_out, v_a_w_out, False,
                                 "adamw_a_out"),
        "b_w_in": _reduce_adamw([[(land_b, 0, n_b_in)]], b_w_in, m_b_w_in, v_b_w_in, True, "adamw_b_in"),
        "b_w_out": _reduce_adamw([[(land_b, n_b_in, b_w_out.shape[1])]], b_w_out, m_b_w_out, v_b_w_out, False,
                                 "adamw_b_out"),
        "ffn_w_gate": [o.transpose(0, 2, 1) for o in gate_out],
        "ffn_w_up": [o.transpose(0, 2, 1) for o in up_out],
        "ffn_w_down": _reduce_adamw([[(land_f0, 2 * f_shard, f_shard)], [(land_f1d, 0, f_shard)]], ffn_w_down,
                                    m_ffn_w_down, v_ffn_w_down, False, "adamw_down"),
    }

    (gr_mix, gr_ffn, gr_gain, gr_bias, gr_ws, gr_bs, gr_final, gr_conv_full, loss) = _unpack_small(small_all, small_shapes)
    gr_conv = lax.dynamic_slice_in_dim(gr_conv_full, me * (e_b // N_DEV), e_b // N_DEV, axis=1)[None]

    small_w =[mix_norm, ffn_norm, a_v_gain, a_v_bias, a_w_s, a_b_s, final_norm]
    small_m = [m_mix_norm, m_ffn_norm, m_a_v_gain, m_a_v_bias, m_a_w_s, m_a_b_s, m_final_norm]
    small_v = [v_mix_norm, v_ffn_norm, v_a_v_gain, v_a_v_bias, v_a_w_s, v_a_b_s, v_final_norm]
    small_g = [gr_mix, gr_ffn, gr_gain, gr_bias, gr_ws, gr_bs, gr_final]
    sm_shapes = small_shapes[:len(small_w)]
    sm_out = _adamw(_pack_small(small_w, small_rows), _pack_small(small_g, small_rows), _pack_small(small_m, small_rows),
                    _pack_small(small_v, small_rows), small_rows, "adamw_small")
    sm_delta, sm_m, sm_v = [_unpack_small(o, sm_shapes) for o in sm_out]

    conv_out = _adamw(b_conv_w[0], gr_conv[0], m_b_conv_w[0], v_b_conv_w[0], CONV_W, "adamw_conv")
    conv_delta, conv_m, conv_v = [o[None] for o in conv_out]

    order = ["mix_norm", "ffn_norm", "a_w_in", "a_v_gain", "a_v_bias", "a_w_s", "a_b_s", "a_w_out", "b_w_in",
             "b_conv_w", "b_w_out", "ffn_w_gate", "ffn_w_up", "ffn_w_down", "final_norm"]
    small_names = ["mix_norm", "ffn_norm", "a_v_gain", "a_v_bias", "a_w_s", "a_b_s", "final_norm"]
    grads = {"b_conv_w": gr_conv}
    deltas, new_m, new_v = {}, {}, {}
    for k, name in enumerate(small_names):
        grads[name] = small_g[k]
        deltas[name], new_m[name], new_v[name] = sm_delta[k], sm_m[k], sm_v[k]
    deltas["b_conv_w"], new_m["b_conv_w"], new_v["b_conv_w"] = conv_delta, conv_m, conv_v
    for name, (gg, dl, mm, vv) in res.items():
        grads[name], deltas[name], new_m[name], new_v[name] = gg, dl, mm, vv

    grad_x = dx0.reshape(bsz, seq, d)
    return (loss, grad_x, *[grads[n] for n in order], *[deltas[n] for n in order],
            *[new_m[n] for n in order], *[new_v[n] for n in order])
```

```python
import math

import jax
import jax.numpy as jnp
from jax import lax
from jax.experimental import pallas as pl
from jax.experimental.pallas import tpu as pltpu

F32 = jnp.float32
BF16 = jnp.bfloat16

N_DEV = 8
N_CHIP = 4
CHUNK = 128
HEADS = 16
GROUP = 128
CONV_W = 3
NORM_EPS = 1e-6
GELU_C = math.sqrt(2.0 / math.pi)
GELU_K = 0.044715

ADAM_LR = 0.001
ADAM_B1 = 0.9
ADAM_B2 = 0.999
ADAM_EPS = 1e-08
ADAM_WD = 0.01
ADAM_STEP = 10

LANES = 128
SUBLANES = 8
VMEM_LIMIT = 60 * 1024 * 1024
HALO = 16
MXU_WIDTH = 256
FFN_CHUNKS = 2

MESH = pl.DeviceIdType.MESH
ANY = pl.BlockSpec(memory_space=pl.ANY)


def _params(sequential=True):
    return pltpu.CompilerParams(
        dimension_semantics=("arbitrary",) if sequential else None,
        vmem_limit_bytes=VMEM_LIMIT)


def _nn(a, b):
    return jnp.dot(a, b, preferred_element_type=F32)


def _nt(a, b):
    return lax.dot_general(a, b, (((1,), (1,)), ((), ())), preferred_element_type=F32)


def _tn(a, b):
    return lax.dot_general(a, b, (((0,), (0,)), ((), ())), preferred_element_type=F32)


def _row_mean(a):
    return jnp.mean(a, axis=-1, keepdims=True)


def _col_sum(a):
    return jnp.sum(a, axis=0, keepdims=True)


def _rms_fwd(x, g):
    r = lax.rsqrt(_row_mean(x * x) + NORM_EPS)
    xhat = x * r
    return xhat * g, xhat, r


def _rms_bwd(dh, xhat, r, g):
    a = dh * g
    dx = r * (a - xhat * _row_mean(a * xhat))
    return dx, _col_sum(dh * xhat)


def _gelu_and_grad(x):
    x2 = x * x
    t = jnp.tanh(x * (GELU_C + (GELU_C * GELU_K) * x2))
    half = 0.5 * t + 0.5
    d = half + x * (0.5 - 0.5 * (t * t)) * (GELU_C + (3.0 * GELU_C * GELU_K) * x2)
    return x * half, d


def _sigmoid(x):
    return 1.0 / (1.0 + jnp.exp(-x))


def _row_spec(tm, width):
    return pl.BlockSpec((tm, width), lambda i: (i, 0))


def _const_spec(shape):
    nd = len(shape)
    return pl.BlockSpec(shape, lambda i: (0,) * nd)


def _load_group(parts, sems):
    @pl.when(pl.program_id(0) == 0)
    def _():
        copies = []
        for k, (gath_ref, first, n, dst) in enumerate(parts):
            for j in range(N_DEV):
                copies.append(pltpu.make_async_copy(gath_ref.at[j, pl.ds(first, n), :], dst.at[pl.ds(j * n, n), :],
                                                    sems.at[k * N_DEV + j]))
        for cp in copies:
            cp.start()
        for cp in copies:
            cp.wait()


def _hosting_call(body, name, n_steps, arrays, in_specs, out_specs, out_shape, scratch, hosted=()):
    n_in, n_out, n_scr = len(arrays), len(out_shape), len(scratch)
    h_arrays = [a for h in hosted for a in h.arrays]
    h_shapes = [s for h in hosted for s in h.out_shapes]
    h_sems = [s for h in hosted for s in h.sem_shapes]

    def full_body(*refs):
        pos = 0
        groups = []
        for n in (n_in, len(h_arrays), n_out, len(h_shapes), n_scr, len(h_sems)):
            groups.append(refs[pos:pos + n])
            pos += n
        own_in, h_in, own_out, h_out, own_scr, h_sem = groups
        per_host = []
        pi = po = ps = 0
        for h in hosted:
            ni, no, ns = len(h.arrays), len(h.out_shapes), len(h.sem_shapes)
            per_host.append((h, h_in[pi:pi + ni], h_out[po:po + no], h_sem[ps:ps + ns]))
            pi, po, ps = pi + ni, po + no, ps + ns
        for h, ins, outs, sems in per_host:
            h.begin(ins, outs, sems, n_steps)
        body(*own_in, *own_out, *own_scr)
        for h, ins, outs, sems in per_host:
            h.end(ins, outs, sems, n_steps)

    outs = pl.pallas_call(
        full_body, name=name, grid=(n_steps,),
        in_specs=list(in_specs) + [ANY] * len(h_arrays),
        out_specs=list(out_specs) + [ANY] * len(h_shapes),
        out_shape=list(out_shape) + h_shapes,
        scratch_shapes=list(scratch) + h_sems,
        compiler_params=_params(),
    )(*arrays, *h_arrays)
    return outs[:n_out], outs[n_out:]


def _my_index():
    return 4 * lax.axis_index("x") + 2 * lax.axis_index("y") + lax.axis_index("c")


GATHER_COPIES = 8


class _Gather:
    def __init__(self, shard, out, send_sems, recv_sems, local_sem):
        self.shard, self.out = shard, out
        self.send_sems, self.recv_sems, self.local_sem = send_sems, recv_sems, local_sem
        x, y, c = lax.axis_index("x"), lax.axis_index("y"), lax.axis_index("c")
        self.c = c
        self.me, self.sibling = (x, y, c), (x, y, 1 - c)
        self.xn, self.yn, self.dg = (1 - x, y), (x, 1 - y), (1 - x, 1 - y)
        self.n = shard.shape[0]
        self.half = self.n // 2
        rows_per_tile = SUBLANES * 4 // shard.dtype.itemsize
        self.relays = self.n % 2 == 0 and self.half % rows_per_tile == 0

    def _slot(self, dev, lo=0, hi=None):
        hi = self.n if hi is None else hi
        return self.out.at[4 * dev[0] + 2 * dev[1] + dev[2], pl.ds(lo, hi - lo), :]

    def _copy(self, k, block, to, src=None, lo=0, hi=None):
        return pltpu.make_async_remote_copy(
            src_ref=self._slot(block, lo, hi) if src is None else src, dst_ref=self._slot(block, lo, hi),
            send_sem=self.send_sems.at[k], recv_sem=self.recv_sems.at[k], device_id=to, device_id_type=MESH)

    def _local(self):
        return pltpu.make_async_copy(self.shard, self._slot(self.me), self.local_sem)

    def start(self):
        c = self.c
        self._local().start()
        self._copy(0, self.me, self.sibling, src=self.shard).start()
        self._copy(1, self.me, (*self.xn, c), src=self.shard).start()
        self._copy(2, self.me, (*self.yn, c), src=self.shard).start()
        if not self.relays:
            self._copy(3, self.me, (*self.dg, c), src=self.shard).start()

    def relay(self):
        c = self.c
        if self.relays:
            self._copy(1, (*self.xn, c), self.me).wait_recv()
            self._copy(3, (*self.xn, c), (*self.yn, c), hi=self.half).start()
            self._copy(2, (*self.yn, c), self.me).wait_recv()
            self._copy(4, (*self.yn, c), (*self.xn, c), lo=self.half).start()

    def forward(self):
        c = self.c
        if self.relays:
            self._copy(5, (*self.xn, c), self.sibling).start()
            self._copy(6, (*self.yn, c), self.sibling).start()
            self._copy(3, (*self.dg, c), self.me, hi=self.half).wait_recv()
            self._copy(4, (*self.dg, c), self.me, lo=self.half).wait_recv()
        else:
            self._copy(1, (*self.xn, c), self.me).wait_recv()
            self._copy(5, (*self.xn, c), self.sibling).start()
            self._copy(2, (*self.yn, c), self.me).wait_recv()
            self._copy(6, (*self.yn, c), self.sibling).start()
            self._copy(3, (*self.dg, c), self.me).wait_recv()
        self._copy(7, (*self.dg, c), self.sibling).start()

    def finish(self):
        c = self.c
        self._copy(0, self.sibling, self.me).wait_recv()
        for k, chip in ((5, self.xn), (6, self.yn), (7, self.dg)):
            self._copy(k, (*chip, 1 - c), self.me).wait_recv()
        for k in (0, 1, 2, 5, 6, 7):
            self._copy(k, self.me, self.sibling).wait_send()
        if self.relays:
            self._copy(3, self.me, self.sibling, hi=self.half).wait_send()
            self._copy(4, self.me, self.sibling, lo=self.half).wait_send()
        else:
            self._copy(3, self.me, self.sibling).wait_send()
        self._local().wait()


class _HostedGathers:
    def __init__(self, shards, mid_lead, relay_at=0.56):
        n = len(shards)
        self.arrays = shards
        self.mid_lead, self.relay_at = mid_lead, relay_at
        self.out_shapes = [jax.ShapeDtypeStruct((N_DEV,) + s.shape, s.dtype) for s in shards]
        self.sem_shapes = [pltpu.SemaphoreType.DMA((n, GATHER_COPIES)), pltpu.SemaphoreType.DMA((n, GATHER_COPIES)),
                           pltpu.SemaphoreType.DMA((n,))]

    def _gathers(self, ins, outs, sems):
        return [_Gather(ins[a], outs[a], sems[0].at[a], sems[1].at[a], sems[2].at[a]) for a in range(len(ins))]

    def begin(self, ins, outs, sems, n_steps):
        i = pl.program_id(0)
        forward_step = max(n_steps - 1 - self.mid_lead, 0)
        relay_step = min(int(self.relay_at * n_steps), forward_step)

        @pl.when(i == 0)
        def _():
            for g in self._gathers(ins, outs, sems):
                g.start()

        if n_steps == 1:
            return

        @pl.when(i == relay_step)
        def _():
            for g in self._gathers(ins, outs, sems):
                g.relay()

        @pl.when(i == forward_step)
        def _():
            for g in self._gathers(ins, outs, sems):
                g.forward()

    def end(self, ins, outs, sems, n_steps):
        @pl.when(pl.program_id(0) == n_steps - 1)
        def _():
            gathers = self._gathers(ins, outs, sems)
            if n_steps == 1:
                for g in gathers:
                    g.relay()
                for g in gathers:
                    g.forward()
            for g in gathers:
                g.finish()


def _exchange(hosted, name):
    return _hosting_call(lambda: None, name, 1, [], [], [], [], [], hosted=hosted)[1]


class _ChipScatter:
    def __init__(self, pairsum, row0, land, send_sems, recv_sems, local_sem):
        self.pairsum, self.row0, self.land = pairsum, row0, land
        self.send_sems, self.recv_sems, self.local_sem = send_sems, recv_sems, local_sem
        x, y, c = lax.axis_index("x"), lax.axis_index("y"), lax.axis_index("c")
        self.c = c
        self.chip = 2 * x + y
        self.others = [(1 - x, y), (x, 1 - y), (1 - x, 1 - y)]

    def _src(self, chip):
        return self.pairsum.at[chip, pl.ds(self.row0, self.land.shape[1]), :]

    def _copy(self, k):
        ox, oy = self.others[k]
        return pltpu.make_async_remote_copy(
            src_ref=self._src(2 * ox + oy), dst_ref=self.land.at[self.chip],
            send_sem=self.send_sems.at[k], recv_sem=self.recv_sems.at[k], device_id=(ox, oy, self.c),
            device_id_type=MESH)

    def _arrival(self, k):
        ox, oy = self.others[k]
        return pltpu.make_async_remote_copy(
            src_ref=self._src(self.chip), dst_ref=self.land.at[2 * ox + oy],
            send_sem=self.send_sems.at[k], recv_sem=self.recv_sems.at[k], device_id=(ox, oy, self.c),
            device_id_type=MESH)

    def _local(self):
        return pltpu.make_async_copy(self._src(self.chip), self.land.at[self.chip], self.local_sem)

    def start(self):
        self._local().start()
        for k in range(N_CHIP - 1):
            self._copy(k).start()

    def finish(self):
        for k in range(N_CHIP - 1):
            self._arrival(k).wait_recv()
        for k in range(N_CHIP - 1):
            self._copy(k).wait_send()
        self._local().wait()


class _HostedChipScatter:
    def __init__(self, pairsum, row0=0, n=None):
        n = pairsum.shape[1] - row0 if n is None else n
        self.row0 = row0
        self.arrays = [pairsum]
        self.out_shapes = [jax.ShapeDtypeStruct((N_CHIP, n, pairsum.shape[2]), pairsum.dtype)]
        self.sem_shapes = [pltpu.SemaphoreType.DMA((N_CHIP - 1,)), pltpu.SemaphoreType.DMA((N_CHIP - 1,)),
                           pltpu.SemaphoreType.DMA(())]

    def begin(self, ins, outs, sems, n_steps):
        @pl.when(pl.program_id(0) == 0)
        def _():
            _ChipScatter(ins[0], self.row0, outs[0], *sems).start()

    def end(self, ins, outs, sems, n_steps):
        @pl.when(pl.program_id(0) == n_steps - 1)
        def _():
            _ChipScatter(ins[0], self.row0, outs[0], *sems).finish()


def _pair_reduce(arrays, entries, name):
    n_arr, n_ent = len(arrays), len(entries)
    cols = arrays[0].shape[1]
    offsets = []
    total = 0
    for _, _, n in entries:
        offsets.append(total)
        total += n

    def body(*refs):
        ins, out_ref = refs[:n_arr], refs[n_arr]
        rbuf, own, send_sems, recv_sems, own_sems = refs[n_arr + 1:]
        q = pl.program_id(0)
        x, y, c = lax.axis_index("x"), lax.axis_index("y"), lax.axis_index("c")

        def block(e, chip, core):
            ai, first, n = entries[e]
            return ins[ai].at[pl.ds(first + (2 * chip + core) * n, n), :]

        def to_sibling(e, chip):
            return pltpu.make_async_remote_copy(
                src_ref=block(e, chip, 1 - c), dst_ref=rbuf.at[chip, pl.ds(offsets[e], entries[e][2]), :],
                send_sem=send_sems.at[e, chip], recv_sem=recv_sems.at[e, chip], device_id=(x, y, 1 - c),
                device_id_type=MESH)

        @pl.when(q == 0)
        def _():
            for chip in range(N_CHIP):
                for e in range(n_ent):
                    to_sibling(e, chip).start()

        loads = [pltpu.make_async_copy(block(e, q, c), own.at[pl.ds(offsets[e], entries[e][2]), :], own_sems.at[e])
                 for e in range(n_ent)]
        for cp in loads:
            cp.start()
        for cp in loads:
            cp.wait()
        for e in range(n_ent):
            to_sibling(e, q).wait_recv()
        out_ref[...] = (own[...].astype(F32) + rbuf[q].astype(F32)).astype(out_ref.dtype)

        @pl.when(q == N_CHIP - 1)
        def _():
            for chip in range(N_CHIP):
                for e in range(n_ent):
                    to_sibling(e, chip).wait_send()

    return pl.pallas_call(
        body, name=name, grid=(N_CHIP,),
        in_specs=[ANY] * n_arr,
        out_specs=pl.BlockSpec((None, total, cols), lambda q: (q, 0, 0)),
        out_shape=jax.ShapeDtypeStruct((N_CHIP, total, cols), BF16),
        scratch_shapes=[pltpu.VMEM((N_CHIP, total, cols), BF16), pltpu.VMEM((total, cols), BF16),
                        pltpu.SemaphoreType.DMA((n_ent, N_CHIP)), pltpu.SemaphoreType.DMA((n_ent, N_CHIP)),
                        pltpu.SemaphoreType.DMA((n_ent,))],
        compiler_params=_params(),
    )(*arrays)


class _HostedScatterAll:
    def __init__(self, packed):
        n = packed.shape[0] // N_DEV
        self.n = n
        self.arrays = [packed]
        self.out_shapes = [jax.ShapeDtypeStruct((N_DEV, n, packed.shape[1]), packed.dtype)]
        self.sem_shapes = [pltpu.SemaphoreType.DMA((N_DEV - 1,)), pltpu.SemaphoreType.DMA((N_DEV - 1,)),
                           pltpu.SemaphoreType.DMA(())]

    def _copies(self, ins, outs, sems, with_arrivals):
        src, land = ins[0], outs[0]
        send_sems, recv_sems, local_sem = sems
        me = _my_index()

        def block(p):
            return src.at[pl.ds(p * self.n, self.n), :]

        local = pltpu.make_async_copy(block(me), land.at[me], local_sem)
        sends, arrivals = [], []
        for k in range(1, N_DEV):
            p = (me + k) % N_DEV
            q = (me + N_DEV - k) % N_DEV
            sends.append(pltpu.make_async_remote_copy(
                src_ref=block(p), dst_ref=land.at[me], send_sem=send_sems.at[k - 1], recv_sem=recv_sems.at[k - 1],
                device_id=(p // 4, (p // 2) % 2, p % 2), device_id_type=MESH))
            if with_arrivals:
                arrivals.append(pltpu.make_async_remote_copy(
                    src_ref=block(me), dst_ref=land.at[q], send_sem=send_sems.at[k - 1], recv_sem=recv_sems.at[k - 1],
                    device_id=(q // 4, (q // 2) % 2, q % 2), device_id_type=MESH))
        return local, sends, arrivals

    def begin(self, ins, outs, sems, n_steps):
        @pl.when(pl.program_id(0) == 0)
        def _():
            local, sends, _ = self._copies(ins, outs, sems, with_arrivals=False)
            local.start()
            for cp in sends:
                cp.start()

    def end(self, ins, outs, sems, n_steps):
        @pl.when(pl.program_id(0) == n_steps - 1)
        def _():
            local, sends, arrivals = self._copies(ins, outs, sems, with_arrivals=True)
            for cp in arrivals:
                cp.wait_recv()
            for cp in sends:
                cp.wait_send()
            local.wait()


def _tril_weights(ws_ref):
    r = lax.broadcasted_iota(jnp.int32, (CHUNK, CHUNK), 0)
    c = lax.broadcasted_iota(jnp.int32, (CHUNK, CHUNK), 1)
    return [jnp.where(r >= c, ws_ref[h], 0.0).astype(BF16) for h in range(HEADS)]


def _sgu_stats(zpre, gain, bias):
    e = zpre.shape[1] // 2
    z, dz = _gelu_and_grad(zpre)
    u, v = z[:, :e], z[:, e:]
    vc = v - _row_mean(v)
    rstd = lax.rsqrt(_row_mean(vc * vc) + NORM_EPS)
    vhat = vc * rstd
    return u, vhat, rstd, vhat * gain + bias, dz


def _spatial_fwd(wt, vn_bf, bfull_ref, sv_ref, tm):
    for ci in range(tm // CHUNK):
        rows = slice(ci * CHUNK, (ci + 1) * CHUNK)
        for h in range(HEADS):
            cols = slice(h * GROUP, (h + 1) * GROUP)
            sv_ref[rows, cols] = _nn(wt[h], vn_bf[rows, cols]) + bfull_ref[:, cols]


def _mixer_a_fwd(x, g, gath, gain, bias, ws, bfull, tm, hosted=()):
    t_tok, d = x.shape
    e = gain.shape[1]
    e2 = 2 * e
    n_in, n_out = e2 // N_DEV, e // N_DEV

    def body(x_ref, g_ref, gain_ref, bias_ref, ws_ref, bfull_ref, gath_ref,
             xo_ref, gd_ref, u_ref, vhat_ref, svo_ref, y_ref, rstd_ref, win_v, wout_v, sv_v, sems):
        _load_group([(gath_ref, 0, n_in, win_v), (gath_ref, n_in, n_out, wout_v)], sems)
        xv = x_ref[...]
        h = _rms_fwd(xv, g_ref[...])[0].astype(BF16)
        zpre = _nt(h, win_v[...])
        u, vhat, rstd, vn, gelu_d = _sgu_stats(zpre, gain_ref[...], bias_ref[...])
        gd_ref[...] = gelu_d.astype(BF16)
        u_ref[...] = u.astype(BF16)
        vhat_ref[...] = vhat.astype(BF16)
        rstd_ref[...] = rstd
        _spatial_fwd(_tril_weights(ws_ref), vn.astype(BF16), bfull_ref, sv_v, tm)
        sv = sv_v[...]
        svo_ref[...] = sv.astype(BF16)
        y = (u * sv).astype(BF16)
        y_ref[...] = y
        xo_ref[...] = xv + _nn(y, wout_v[...])

    return _hosting_call(
        body, "mixer_a_fwd", t_tok // tm, [x, g, gain, bias, ws, bfull, gath],
        in_specs=[_row_spec(tm, d), _const_spec((1, d)), _const_spec((1, e)), _const_spec((1, e)),
                  _const_spec((HEADS, CHUNK, CHUNK)), _const_spec((CHUNK, e)), ANY],
        out_specs=[_row_spec(tm, d), _row_spec(tm, e2), _row_spec(tm, e), _row_spec(tm, e), _row_spec(tm, e),
                   _row_spec(tm, e), _row_spec(tm, 1)],
        out_shape=[jax.ShapeDtypeStruct((t_tok, d), F32), jax.ShapeDtypeStruct((t_tok, e2), BF16),
                   jax.ShapeDtypeStruct((t_tok, e), BF16), jax.ShapeDtypeStruct((t_tok, e), BF16),
                   jax.ShapeDtypeStruct((t_tok, e), BF16), jax.ShapeDtypeStruct((t_tok, e), BF16),
                   jax.ShapeDtypeStruct((t_tok, 1), F32)],
        scratch=[pltpu.VMEM((e2, d), BF16), pltpu.VMEM((e, d), BF16), pltpu.VMEM((tm, e), F32),
                 pltpu.SemaphoreType.DMA((2 * N_DEV,))],
        hosted=hosted)


def _mixer_a_bwd(dout, x, gd, u_sav, vhat_sav, sv_sav, rstd_sav, g, gath, gain, bias, ws, tm, hosted=()):
    t_tok, d = x.shape
    e = gain.shape[1]
    e2 = 2 * e
    n_in, n_out = e2 // N_DEV, e // N_DEV
    n_steps = t_tok // tm

    def body(dout_ref, x_ref, gd_ref, u_ref, vhat_ref, sv_ref, rstd_ref, g_ref, gain_ref, bias_ref, ws_ref, gath_ref,
             dx_ref, dxb_ref, h_ref, dz_ref, dg_ref, dgain_ref, dbias_ref, dws_ref, dbso_ref,
             win_v, wout_v, dvn_v, dbs_ref, sems):
        i = pl.program_id(0)
        _load_group([(gath_ref, 0, n_in, win_v), (gath_ref, n_in, n_out, wout_v)], sems)

        @pl.when(i == 0)
        def _():
            dg_ref[...] = jnp.zeros_like(dg_ref)
            dgain_ref[...] = jnp.zeros_like(dgain_ref)
            dbias_ref[...] = jnp.zeros_like(dbias_ref)
            dws_ref[...] = jnp.zeros_like(dws_ref)
            dbs_ref[...] = jnp.zeros_like(dbs_ref)

        xv = x_ref[...]
        gv = g_ref[...]
        hv, xhat, r = _rms_fwd(xv, gv)
        h_ref[...] = hv.astype(BF16)
        gain_v = gain_ref[...]
        vhat = vhat_ref[...].astype(F32)
        vn_bf = (vhat * gain_v + bias_ref[...]).astype(BF16)
        wt = _tril_weights(ws_ref)

        dov = dout_ref[...]
        dy = _nt(dov.astype(BF16), wout_v[...])
        du = dy * sv_ref[...].astype(F32)
        dsv = dy * u_ref[...].astype(F32)
        dsv_bf = dsv.astype(BF16)
        for ci in range(tm // CHUNK):
            rows = slice(ci * CHUNK, (ci + 1) * CHUNK)
            dbs_ref[...] += dsv[rows, :]
            for h in range(HEADS):
                cols = slice(h * GROUP, (h + 1) * GROUP)
                dvn_v[rows, cols] = _tn(wt[h], dsv_bf[rows, cols])
                dws_ref[h] += _nt(dsv_bf[rows, cols], vn_bf[rows, cols])
        dvn = dvn_v[...]
        dgain_ref[...] += _col_sum(dvn * vhat)
        dbias_ref[...] += _col_sum(dvn)
        dvhat = dvn * gain_v
        dv = rstd_ref[...] * (dvhat - _row_mean(dvhat) - vhat * _row_mean(dvhat * vhat))
        dzpre = (jnp.concatenate([du, dv], axis=1) * gd_ref[...].astype(F32)).astype(BF16)
        dz_ref[...] = dzpre
        dh = _nn(dzpre, win_v[...])
        dxr, dg_row = _rms_bwd(dh, xhat, r, gv)
        dg_ref[...] += dg_row
        dx = dov + dxr
        dx_ref[...] = dx
        dxb_ref[...] = dx.astype(BF16)

        @pl.when(i == n_steps - 1)
        def _():
            rr = lax.broadcasted_iota(jnp.int32, (CHUNK, CHUNK), 0)
            cc = lax.broadcasted_iota(jnp.int32, (CHUNK, CHUNK), 1)
            for h in range(HEADS):
                dws_ref[h] = jnp.where(rr >= cc, dws_ref[h], 0.0)
                dbso_ref[h] = jnp.sum(dbs_ref[:, h * GROUP:(h + 1) * GROUP], axis=1, keepdims=True)

    return _hosting_call(
        body, "mixer_a_bwd", n_steps, [dout, x, gd, u_sav, vhat_sav, sv_sav, rstd_sav, g, gain, bias, ws, gath],
        in_specs=[_row_spec(tm, d), _row_spec(tm, d), _row_spec(tm, e2), _row_spec(tm, e), _row_spec(tm, e),
                  _row_spec(tm, e), _row_spec(tm, 1), _const_spec((1, d)),
                  _const_spec((1, e)), _const_spec((1, e)), _const_spec((HEADS, CHUNK, CHUNK)), ANY],
        out_specs=[_row_spec(tm, d), _row_spec(tm, d), _row_spec(tm, d), _row_spec(tm, e2),
                   _const_spec((1, d)), _const_spec((1, e)), _const_spec((1, e)),
                   _const_spec((HEADS, CHUNK, CHUNK)), _const_spec((HEADS, CHUNK, 1))],
        out_shape=[jax.ShapeDtypeStruct((t_tok, d), F32), jax.ShapeDtypeStruct((t_tok, d), BF16),
                   jax.ShapeDtypeStruct((t_tok, d), BF16), jax.ShapeDtypeStruct((t_tok, e2), BF16),
                   jax.ShapeDtypeStruct((1, d), F32), jax.ShapeDtypeStruct((1, e), F32),
                   jax.ShapeDtypeStruct((1, e), F32), jax.ShapeDtypeStruct((HEADS, CHUNK, CHUNK), F32),
                   jax.ShapeDtypeStruct((HEADS, CHUNK, 1), F32)],
        scratch=[pltpu.VMEM((e2, d), BF16), pltpu.VMEM((e, d), BF16), pltpu.VMEM((tm, e), F32),
                 pltpu.VMEM((CHUNK, e), F32), pltpu.SemaphoreType.DMA((2 * N_DEV,))],
        hosted=hosted)


def _ffn_fwd(x, g, srcs, nf, tm, name, hosted=(), head=None):
    t_tok, d = x.shape
    f = nf * N_DEV
    firsts = [first for _, first in srcs]
    n_head = 2 if head else 0

    def body(*refs):
        x_ref, g_ref, sg_ref, su_ref, sd_ref = refs[:5]
        gate_ref, up_ref, wg_v, wu_v, wd_v, sems = refs[-6:]
        _load_group(
            [(sg_ref, firsts[0], nf, wg_v), (su_ref, firsts[1], nf, wu_v), (sd_ref, firsts[2], nf, wd_v)], sems)
        if head:
            t_ref, gf_ref, loss_ref, dx_ref, dxb_ref, dgf_ref = refs[5:11]

            @pl.when(pl.program_id(0) == 0)
            def _():
                loss_ref[...] = jnp.zeros_like(loss_ref)
                dgf_ref[...] = jnp.zeros_like(dgf_ref)

        xv = x_ref[...]
        h = _rms_fwd(xv, g_ref[...])[0].astype(BF16)
        gate = _nt(h, wg_v[...])
        up = _nt(h, wu_v[...])
        gate_ref[...] = gate.astype(BF16)
        up_ref[...] = up.astype(BF16)
        act = (gate * _sigmoid(gate) * up).astype(BF16)
        xo = xv + _nn(act, wd_v[...])
        if head:
            gfv = gf_ref[...]
            y, xhat, r = _rms_fwd(xo, gfv)
            err = y - t_ref[...]
            loss_ref[...] += 0.5 * jnp.sum(_row_mean(err * err), axis=0, keepdims=True)
            dxr, dg_row = _rms_bwd(err * (1.0 / d), xhat, r, gfv)
            dgf_ref[...] += dg_row
            dx_ref[...] = dxr
            dxb_ref[...] = dxr.astype(BF16)
        else:
            refs[5][...] = xo

    act_specs = [_row_spec(tm, f), _row_spec(tm, f)]
    act_shapes = [jax.ShapeDtypeStruct((t_tok, f), BF16), jax.ShapeDtypeStruct((t_tok, f), BF16)]
    if head:
        out_specs = [_const_spec((1, 1)), _row_spec(tm, d), _row_spec(tm, d), _const_spec((1, d))]
        out_shape = [jax.ShapeDtypeStruct((1, 1), F32), jax.ShapeDtypeStruct((t_tok, d), F32),
                     jax.ShapeDtypeStruct((t_tok, d), BF16), jax.ShapeDtypeStruct((1, d), F32)]
    else:
        out_specs = [_row_spec(tm, d)]
        out_shape = [jax.ShapeDtypeStruct((t_tok, d), F32)]
    return _hosting_call(
        body, name, t_tok // tm, [x, g] + [arr for arr, _ in srcs] + list(head or ()),
        in_specs=[_row_spec(tm, d), _const_spec((1, d)), ANY, ANY, ANY] + [_row_spec(tm, d), _const_spec((1, d))][:n_head],
        out_specs=out_specs + act_specs, out_shape=out_shape + act_shapes,
        scratch=[pltpu.VMEM((f, d), BF16), pltpu.VMEM((f, d), BF16), pltpu.VMEM((f, d), BF16),
                 pltpu.SemaphoreType.DMA((3 * N_DEV,))],
        hosted=hosted)


def _ffn_bwd(dout, x, gate, up, g, srcs, nf, tm, name, hosted=()):
    t_tok, d = x.shape
    f = nf * N_DEV
    firsts = [first for _, first in srcs]
    per_chunk = -(-f // (FFN_CHUNKS * MXU_WIDTH)) * MXU_WIDTH
    bounds = [min(ck * per_chunk, f) for ck in range(FFN_CHUNKS + 1)]

    def body(dout_ref, x_ref, gate_ref, up_ref, g_ref, sg_ref, su_ref, sd_ref,
             dx_ref, dxb_ref, h_ref, act_ref, dgu_ref, dg_ref, wg_v, wu_v, wd_v, sems):
        _load_group(
            [(sg_ref, firsts[0], nf, wg_v), (su_ref, firsts[1], nf, wu_v), (sd_ref, firsts[2], nf, wd_v)], sems)

        @pl.when(pl.program_id(0) == 0)
        def _():
            dg_ref[...] = jnp.zeros_like(dg_ref)

        xv = x_ref[...]
        gv = g_ref[...]
        hv, xhat, r = _rms_fwd(xv, gv)
        h_ref[...] = hv.astype(BF16)
        dov = dout_ref[...]
        dob = dov.astype(BF16)
        dh = None
        for ck in range(FFN_CHUNKS):
            cols = slice(bounds[ck], bounds[ck + 1])
            gate_v = gate_ref[:, cols].astype(F32)
            up_v = up_ref[:, cols].astype(F32)
            sig = _sigmoid(gate_v)
            silu = gate_v * sig
            act_ref[:, cols] = (silu * up_v).astype(BF16)
            dact = _nt(dob, wd_v[cols, :])
            dup = (dact * silu).astype(BF16)
            dgate = (dact * up_v * (sig * (1.0 + gate_v * (1.0 - sig)))).astype(BF16)
            dgu_ref[:, cols] = dgate
            dgu_ref[:, f + bounds[ck]:f + bounds[ck + 1]] = dup
            part = _nn(dgate, wg_v[cols, :]) + _nn(dup, wu_v[cols, :])
            dh = part if dh is None else dh + part
        dxr, dg_row = _rms_bwd(dh, xhat, r, gv)
        dg_ref[...] += dg_row
        dx = dov + dxr
        dx_ref[...] = dx
        dxb_ref[...] = dx.astype(BF16)

    return _hosting_call(
        body, name, t_tok // tm, [dout, x, gate, up, g] + [arr for arr, _ in srcs],
        in_specs=[_row_spec(tm, d), _row_spec(tm, d), _row_spec(tm, f), _row_spec(tm, f), _const_spec((1, d)),
                  ANY, ANY, ANY],
        out_specs=[_row_spec(tm, d), _row_spec(tm, d), _row_spec(tm, d), _row_spec(tm, f), _row_spec(tm, 2 * f),
                   _const_spec((1, d))],
        out_shape=[jax.ShapeDtypeStruct((t_tok, d), F32), jax.ShapeDtypeStruct((t_tok, d), BF16),
                   jax.ShapeDtypeStruct((t_tok, d), BF16), jax.ShapeDtypeStruct((t_tok, f), BF16),
                   jax.ShapeDtypeStruct((t_tok, 2 * f), BF16), jax.ShapeDtypeStruct((1, d), F32)],
        scratch=[pltpu.VMEM((f, d), BF16), pltpu.VMEM((f, d), BF16), pltpu.VMEM((f, d), BF16),
                 pltpu.SemaphoreType.DMA((3 * N_DEV,))],
        hosted=hosted)


def _shift_down(z, k, prev_rows):
    row = lax.broadcasted_iota(jnp.int32, z.shape, 0)
    out = pltpu.roll(z, k, 0)
    for j in range(k):
        out = jnp.where(row == j, prev_rows[j], out)
    return out


def _shift_up(z, k, next_rows):
    tm = z.shape[0]
    row = lax.broadcasted_iota(jnp.int32, z.shape, 0)
    out = pltpu.roll(z, tm - k, 0)
    for j in range(k):
        out = jnp.where(row == tm - k + j, next_rows[j], out)
    return out


def _mixer_b_fwd(x, g, gath, conv_w, tm, seq, hosted=()):
    t_tok, d = x.shape
    e = conv_w.shape[1]
    e3 = 3 * e
    n_in, n_out = e3 // N_DEV, e // N_DEV
    tiles_per_seq = seq // tm

    def body(x_ref, g_ref, cw_ref, gath_ref, xo_ref, p_ref, win_v, wout_v, tail_v, sems):
        i = pl.program_id(0)
        _load_group([(gath_ref, 0, n_in, win_v), (gath_ref, n_in, n_out, wout_v)], sems)

        @pl.when(i % tiles_per_seq == 0)
        def _():
            tail_v[...] = jnp.zeros_like(tail_v)

        xv = x_ref[...]
        h = _rms_fwd(xv, g_ref[...])[0].astype(BF16)
        p = _nt(h, win_v[...])
        p_ref[...] = p.astype(BF16)
        z = p[:, e:2 * e] * p[:, 2 * e:]
        prev = [tail_v[SUBLANES - 2:SUBLANES - 1, :], tail_v[SUBLANES - 1:SUBLANES, :]]
        conv = (cw_ref[2:3, :] * z + cw_ref[1:2, :] * _shift_down(z, 1, prev[1:])
                + cw_ref[0:1, :] * _shift_down(z, 2, prev))
        tail_v[...] = z[tm - SUBLANES:, :]
        y = (p[:, :e] * conv).astype(BF16)
        xo_ref[...] = xv + _nn(y, wout_v[...])

    return _hosting_call(
        body, "mixer_b_fwd", t_tok // tm, [x, g, conv_w, gath],
        in_specs=[_row_spec(tm, d), _const_spec((1, d)), _const_spec((SUBLANES, e)), ANY],
        out_specs=[_row_spec(tm, d), _row_spec(tm, e3)],
        out_shape=[jax.ShapeDtypeStruct((t_tok, d), F32), jax.ShapeDtypeStruct((t_tok, e3), BF16)],
        scratch=[pltpu.VMEM((e3, d), BF16), pltpu.VMEM((e, d), BF16), pltpu.VMEM((SUBLANES, e), F32),
                 pltpu.SemaphoreType.DMA((2 * N_DEV,))],
        hosted=hosted)


def _mixer_b_bwd(dout, x, p, g, gath, conv_w, tm, seq, hosted=()):
    t_tok, d = x.shape
    e = conv_w.shape[1]
    e3 = 3 * e
    n_in, n_out = e3 // N_DEV, e // N_DEV
    tiles_per_seq = seq // tm
    halo_per_tile = tm // HALO
    n_halo = t_tok // HALO

    def body(dout_ref, dnext_ref, x_ref, p_ref, pprev_ref, pnext_ref, g_ref, cw_ref, gath_ref,
             dx_ref, dxb_ref, h_ref, y_ref, dp_ref, dg_ref, dcw_ref, win_v, wout_v, sems):
        i = pl.program_id(0)
        _load_group([(gath_ref, 0, n_in, win_v), (gath_ref, n_in, n_out, wout_v)], sems)

        @pl.when(i == 0)
        def _():
            dg_ref[...] = jnp.zeros_like(dg_ref)
            dcw_ref[...] = jnp.zeros_like(dcw_ref)

        first = (i % tiles_per_seq == 0).astype(F32)
        last = (i % tiles_per_seq == tiles_per_seq - 1).astype(F32)
        xv = x_ref[...]
        gv = g_ref[...]
        hv, xhat, r = _rms_fwd(xv, gv)
        h_ref[...] = hv.astype(BF16)
        pv = p_ref[...].astype(F32)
        bg, cg, hx = pv[:, :e], pv[:, e:2 * e], pv[:, 2 * e:]
        z = cg * hx
        pprev = pprev_ref[...].astype(F32)
        zprev = pprev[:, e:2 * e] * pprev[:, 2 * e:] * (1.0 - first)
        prev = [zprev[HALO - 2:HALO - 1, :], zprev[HALO - 1:HALO, :]]
        zs1 = _shift_down(z, 1, prev[1:])
        zs2 = _shift_down(z, 2, prev)
        w0, w1, w2 = cw_ref[0:1, :], cw_ref[1:2, :], cw_ref[2:3, :]
        conv = w2 * z + w1 * zs1 + w0 * zs2
        y_ref[...] = (bg * conv).astype(BF16)

        dov = dout_ref[...]
        wout_bf = wout_v[...]
        dy = _nt(dov.astype(BF16), wout_bf)
        dconv = dy * bg
        dnext = _nt(dnext_ref[...].astype(BF16), wout_bf) * pnext_ref[:, :e].astype(F32) * (1.0 - last)
        nxt = [dnext[0:1, :], dnext[1:2, :]]
        dz = w2 * dconv + w1 * _shift_up(dconv, 1, nxt[:1]) + w0 * _shift_up(dconv, 2, nxt)
        dcw_ref[0:1, :] += _col_sum(dconv * zs2)
        dcw_ref[1:2, :] += _col_sum(dconv * zs1)
        dcw_ref[2:3, :] += _col_sum(dconv * z)
        dp = jnp.concatenate([dy * conv, dz * hx, dz * cg], axis=1).astype(BF16)
        dp_ref[...] = dp
        dh = _nn(dp, win_v[...])
        dxr, dg_row = _rms_bwd(dh, xhat, r, gv)
        dg_ref[...] += dg_row
        dx = dov + dxr
        dx_ref[...] = dx
        dxb_ref[...] = dx.astype(BF16)

    prev_spec = lambda w: pl.BlockSpec((HALO, w), lambda i: (jnp.maximum(i * halo_per_tile - 1, 0), 0))
    next_spec = lambda w: pl.BlockSpec((HALO, w), lambda i: (jnp.minimum((i + 1) * halo_per_tile, n_halo - 1), 0))
    return _hosting_call(
        body, "mixer_b_bwd", t_tok // tm, [dout, dout, x, p, p, p, g, conv_w, gath],
        in_specs=[_row_spec(tm, d), next_spec(d), _row_spec(tm, d), _row_spec(tm, e3), prev_spec(e3), next_spec(e3),
                  _const_spec((1, d)), _const_spec((SUBLANES, e)), ANY],
        out_specs=[_row_spec(tm, d), _row_spec(tm, d), _row_spec(tm, d), _row_spec(tm, e), _row_spec(tm, e3),
                   _const_spec((1, d)), _const_spec((SUBLANES, e))],
        out_shape=[jax.ShapeDtypeStruct((t_tok, d), F32), jax.ShapeDtypeStruct((t_tok, d), BF16),
                   jax.ShapeDtypeStruct((t_tok, d), BF16), jax.ShapeDtypeStruct((t_tok, e), BF16),
                   jax.ShapeDtypeStruct((t_tok, e3), BF16), jax.ShapeDtypeStruct((1, d), F32),
                   jax.ShapeDtypeStruct((SUBLANES, e), F32)],
        scratch=[pltpu.VMEM((e3, d), BF16), pltpu.VMEM((e, d), BF16), pltpu.SemaphoreType.DMA((2 * N_DEV,))],
        hosted=hosted)


def _wgrad(a, b, bm, name, hosted=()):
    t_tok, m = a.shape
    n = b.shape[1]

    def body(a_ref, b_ref, o_ref):
        o_ref[...] = _tn(a_ref[...], b_ref[...]).astype(o_ref.dtype)

    outs, h_outs = _hosting_call(
        body, name, m // bm, [a, b],
        in_specs=[pl.BlockSpec((t_tok, bm), lambda i: (0, i)), _const_spec((t_tok, n))],
        out_specs=[pl.BlockSpec((bm, n), lambda i: (i, 0))],
        out_shape=[jax.ShapeDtypeStruct((m, n), BF16)],
        scratch=[], hosted=hosted)
    return (outs[0], h_outs) if hosted else outs[0]


def _sum_slots(land, rb, name):
    n_slots, rows, cols = land.shape

    def body(l_ref, o_ref):
        acc = l_ref[0].astype(F32)
        for k in range(1, n_slots):
            acc = acc + l_ref[k].astype(F32)
        o_ref[...] = acc

    return pl.pallas_call(
        body, name=name, grid=(rows // rb,),
        in_specs=[pl.BlockSpec((n_slots, rb, cols), lambda i: (0, i, 0))],
        out_specs=pl.BlockSpec((rb, cols), lambda i: (i, 0)),
        out_shape=jax.ShapeDtypeStruct((rows, cols), F32),
        compiler_params=_params(sequential=False),
    )(land)


def _adamw(w, grad, m, v, rb, name):
    rows, cols = w.shape
    c1 = 1.0 / (1.0 - ADAM_B1 ** ADAM_STEP)
    c2 = 1.0 / (1.0 - ADAM_B2 ** ADAM_STEP)

    def body(w_ref, g_ref, m_ref, v_ref, d_ref, mo_ref, vo_ref):
        gv = g_ref[...]
        mn = ADAM_B1 * m_ref[...] + (1.0 - ADAM_B1) * gv
        vn = ADAM_B2 * v_ref[...] + (1.0 - ADAM_B2) * (gv * gv)
        mo_ref[...] = mn
        vo_ref[...] = vn
        d_ref[...] = -ADAM_LR * ((mn * c1) / (jnp.sqrt(vn * c2) + ADAM_EPS) + ADAM_WD * w_ref[...])

    spec = pl.BlockSpec((rb, cols), lambda i: (i, 0))
    shape = jax.ShapeDtypeStruct((rows, cols), F32)
    return pl.pallas_call(
        body, name=name, grid=(rows // rb,),
        in_specs=[spec] * 4, out_specs=[spec] * 3, out_shape=[shape] * 3,
        compiler_params=_params(sequential=False),
    )(w, grad, m, v)


def _pack_shards(groups, name, hosted=()):
    flat = [(part, layer) for group in groups for part, layer, _ in group]
    rows = [[p.shape[2] if turn else p.shape[1] for p, _, turn in group] for group in groups]
    first, _, first_turn = groups[0][0]
    width = first.shape[1] if first_turn else first.shape[2]

    def body(*refs):
        ins, outs = refs[:len(flat)], refs[len(flat):]
        k = 0
        for gi, group in enumerate(groups):
            off = 0
            for (_, _, turn), n in zip(group, rows[gi]):
                part = ins[k][...].astype(BF16)
                if turn:
                    r = lax.broadcasted_iota(jnp.int32, (n, n), 0)
                    c = lax.broadcasted_iota(jnp.int32, (n, n), 1)
                    part = _nt((r == c).astype(BF16), part).astype(BF16)
                outs[gi][off:off + n, :] = part
                off += n
                k += 1

    return _hosting_call(
        body, name, 1, [p for p, _ in flat],
        in_specs=[pl.BlockSpec((None,) + p.shape[1:], lambda i, layer=layer: (layer, 0, 0)) for p, layer in flat],
        out_specs=[_const_spec((sum(r), width)) for r in rows],
        out_shape=[jax.ShapeDtypeStruct((sum(r), width), BF16) for r in rows],
        scratch=[], hosted=hosted)


def _split_bf16(a):
    hi = a.astype(BF16)
    rest = a - hi.astype(F32)
    mid = rest.astype(BF16)
    return hi, mid, (rest - mid.astype(F32)).astype(BF16)


def _reduce_adamw(lands, w, m, v, transpose, name, hosted=()):
    n_layers, rows_w, cols_w = w.shape
    c1 = 1.0 / (1.0 - ADAM_B1 ** ADAM_STEP)
    c2 = 1.0 / (1.0 - ADAM_B2 ** ADAM_STEP)
    flat = [piece for pieces in lands for piece in pieces]
    counts = [len(pieces) for pieces in lands]
    if transpose:
        tiles = rows_w // MXU_WIDTH
        blk = (MXU_WIDTH, cols_w)
        land_specs = [pl.BlockSpec((N_CHIP, n, MXU_WIDTH), lambda i, b=first // n: (0, b, i % tiles))
                      for _, first, n in flat]
        for _, first, n in flat:
            assert first % n == 0
    else:
        tiles = 2
        blk = (rows_w // tiles, cols_w)
        assert all(c == 1 for c in counts)
        land_specs = [pl.BlockSpec((N_CHIP,) + blk, lambda i, b=first // blk[0]: (0, b + i % tiles, 0))
                      for _, first, _ in flat]
        for _, first, _ in flat:
            assert first % blk[0] == 0

    def body(*refs):
        land_refs = refs[:len(flat)]
        w_ref, m_ref, v_ref, g_ref, d_ref, mo_ref, vo_ref = refs[len(flat):]
        layer = pl.program_id(0) // tiles

        def total(ref):
            acc = ref[0].astype(F32)
            for q in range(1, N_CHIP):
                acc = acc + ref[q].astype(F32)
            return acc

        def layer_sum(k):
            first = sum(counts[:k])
            parts = [total(land_refs[first + j]) for j in range(counts[k])]
            return parts[0] if len(parts) == 1 else jnp.concatenate(parts, axis=0)

        gv = layer_sum(0)
        for k in range(1, n_layers):
            gv = jnp.where(layer == k, layer_sum(k), gv)
        if transpose:
            r = lax.broadcasted_iota(jnp.int32, (MXU_WIDTH, MXU_WIDTH), 0)
            c = lax.broadcasted_iota(jnp.int32, (MXU_WIDTH, MXU_WIDTH), 1)
            eye = (r == c).astype(BF16)
            hi, mid, lo = _split_bf16(gv)
            gv = _nt(eye, hi) + _nt(eye, mid) + _nt(eye, lo)
        g_ref[...] = gv
        mn = ADAM_B1 * m_ref[...] + (1.0 - ADAM_B1) * gv
        vn = ADAM_B2 * v_ref[...] + (1.0 - ADAM_B2) * (gv * gv)
        mo_ref[...] = mn
        vo_ref[...] = vn
        d_ref[...] = -ADAM_LR * ((mn * c1) / (jnp.sqrt(vn * c2) + ADAM_EPS) + ADAM_WD * w_ref[...])

    spec = pl.BlockSpec((None,) + blk, lambda i: (i // tiles, i % tiles, 0))
    shape = jax.ShapeDtypeStruct(w.shape, F32)
    outs, h_outs = _hosting_call(
        body, name, n_layers * tiles, [land for land, _, _ in flat] + [w, m, v],
        in_specs=land_specs + [spec] * 3, out_specs=[spec] * 4, out_shape=[shape] * 4, scratch=[], hosted=hosted)
    return (outs, h_outs) if hosted else outs


def _pack_small(parts, rows):
    flat = jnp.concatenate([p.reshape(-1).astype(F32) for p in parts])
    return jnp.pad(flat, (0, rows * LANES - flat.shape[0])).reshape(rows, LANES)


def _unpack_small(packed, shapes):
    flat = packed.reshape(-1)
    out = []
    pos = 0
    for s in shapes:
        n = math.prod(s)
        out.append(flat[pos:pos + n].reshape(s))
        pos += n
    return out


def kernel(x, mix_norm, ffn_norm, a_w_in, a_v_gain, a_v_bias, a_w_s, a_b_s, a_w_out, b_w_in, b_conv_w, b_w_out, ffn_w_gate, ffn_w_up, ffn_w_down, final_norm, loss_target, m_mix_norm, m_ffn_norm, m_a_w_in, m_a_v_gain, m_a_v_bias, m_a_w_s, m_a_b_s, m_a_w_out, m_b_w_in, m_b_conv_w, m_b_w_out, m_ffn_w_gate, m_ffn_w_up, m_ffn_w_down, m_final_norm, v_mix_norm, v_ffn_norm, v_a_w_in, v_a_v_gain, v_a_v_bias, v_a_w_s, v_a_b_s, v_a_w_out, v_b_w_in, v_b_conv_w, v_b_w_out, v_ffn_w_gate, v_ffn_w_up, v_ffn_w_down, v_final_norm):
    bsz, seq, d = x.shape
    t_tok = bsz * seq
    me = _my_index()
    xt = x.reshape(t_tok, d)
    target = loss_target.reshape(t_tok, d)
    e_a = a_v_gain.shape[1]
    e_b = b_w_out.shape[1] * N_DEV
    n_layers = ffn_w_gate.shape[0]
    f_shard = ffn_w_gate.shape[2]
    f_full = f_shard * N_DEV

    conv_pad = jnp.pad(b_conv_w[0], ((0, SUBLANES - CONV_W), (0, 0)))
    sh_a = jnp.concatenate([a_w_in[0].T, a_w_out[0]]).astype(BF16)
    bfull = jnp.repeat(a_b_s[0].T, GROUP, axis=1)

    gate_t, up_t = ffn_w_gate.transpose(0, 2, 1), ffn_w_up.transpose(0, 2, 1)
    (sh_b, sh_f0, sh_f1g, sh_f1ud), (gath_a, conv_g) = _pack_shards(
        [[(b_w_in, 0, True), (b_w_out, 0, False)],
         [(gate_t, 0, False), (up_t, 0, False), (ffn_w_down, 0, False)],
         [(gate_t, 1, False)],
         [(up_t, 1, False), (ffn_w_down, 1, False)]],
        "pack_shards", hosted=[_HostedGathers([sh_a, conv_pad], 0)])
    conv_full = jnp.pad(conv_g[:, :CONV_W, :].transpose(1, 0, 2).reshape(CONV_W, e_b), ((0, SUBLANES - CONV_W), (0, 0)))
    (x1, gd_a, u_a, vhat_a, sv_a, y_a, rstd_a), (gath_f0,) = _mixer_a_fwd(
        xt, mix_norm[0:1], gath_a, a_v_gain, a_v_bias, a_w_s[0], bfull, tm=256,
        hosted=[_HostedGathers([sh_f0], mid_lead=2)])
    srcs0 = [(gath_f0, 0), (gath_f0, f_shard), (gath_f0, 2 * f_shard)]
    (x2, gate0, up0), (gath_b, gath_f1g) = _ffn_fwd(x1, ffn_norm[0:1], srcs0, f_shard, tm=256, name="ffn_fwd0",
                                                    hosted=[_HostedGathers([sh_b, sh_f1g], mid_lead=2)])
    (x3, p_b), (gath_f1ud,) = _mixer_b_fwd(x2, mix_norm[1:2], gath_b, conv_full, tm=256, seq=seq,
                                           hosted=[_HostedGathers([sh_f1ud], mid_lead=2)])
    srcs1 = [(gath_f1g, 0), (gath_f1ud, 0), (gath_f1ud, f_shard)]
    (loss_part, dx4, dx4_bf, d_final, gate1, up1), _ = _ffn_fwd(
        x3, ffn_norm[1:2], srcs1, f_shard, tm=256, name="ffn_fwd1", head=(target, final_norm.reshape(1, d)))

    ffn_entries = [(0, 0, f_shard), (0, f_full, f_shard), (1, 0, f_shard)]
    (dx3, dx3_bf, h_f1, act1, dgu1, d_fn1), _ = _ffn_bwd(dx4, x3, gate1, up1, ffn_norm[1:2], srcs1, f_shard, tm=256,
                                                         name="ffn_bwd1")
    g_down1 = _wgrad(act1, dx4_bf, 256, "wgrad_down1")
    g_gu1 = _wgrad(dgu1, h_f1, 512, "wgrad_gate_up1")
    ps_f1 = _pair_reduce([g_gu1, g_down1], ffn_entries, "pair_reduce_f1")
    (dx2, dx2_bf, h_b, y_b, dp_b, d_mn1, d_conv), (land_f1gu,) = _mixer_b_bwd(
        dx3, x2, p_b, mix_norm[1:2], gath_b, conv_full, tm=512, seq=seq,
        hosted=[_HostedChipScatter(ps_f1, 0, 2 * f_shard)])
    g_b_out = _wgrad(y_b, dx3_bf, 256, "wgrad_b_out")
    g_b_in = _wgrad(dp_b, h_b, 512, "wgrad_b_in")
    ps_b = _pair_reduce([g_b_in, g_b_out], [(0, 0, b_w_in.shape[2]), (1, 0, b_w_out.shape[1])], "pair_reduce_b")
    (dx1, dx1_bf, h_f0, act0, dgu0, d_fn0), (land_f1d, land_b) = _ffn_bwd(
        dx2, x1, gate0, up0, ffn_norm[0:1], srcs0, f_shard, tm=256, name="ffn_bwd0",
        hosted=[_HostedChipScatter(ps_f1, 2 * f_shard, f_shard), _HostedChipScatter(ps_b)])
    g_down0 = _wgrad(act0, dx2_bf, 256, "wgrad_down0")
    g_gu0 = _wgrad(dgu0, h_f0, 512, "wgrad_gate_up0")
    g_a_out = _wgrad(y_a, dx1_bf, 256, "wgrad_a_out")
    n_ao = a_w_out.shape[1]
    ps_f0ao = _pair_reduce([g_gu0, g_down0, g_a_out], ffn_entries + [(2, 0, n_ao)], "pair_reduce_f0_a_out")
    (dx0, _, h_a, dz_a, d_mn0, d_gain, d_bias, d_ws, d_bs_acc), (land_f0, land_ao) = _mixer_a_bwd(
        dx1, xt, gd_a, u_a, vhat_a, sv_a, rstd_a, mix_norm[0:1], gath_a, a_v_gain, a_v_bias, a_w_s[0], tm=256,
        hosted=[_HostedChipScatter(ps_f0ao, 0, 3 * f_shard), _HostedChipScatter(ps_f0ao, 3 * f_shard, n_ao)])
    d_bs = d_bs_acc.reshape(HEADS, CHUNK)

    small_grads = [jnp.concatenate([d_mn0, d_mn1]), jnp.concatenate([d_fn0, d_fn1]), d_gain, d_bias, d_ws, d_bs,
                   d_final, d_conv[:CONV_W], loss_part]
    small_shapes = [(n_layers, d), (n_layers, d), (1, e_a), (1, e_a), (1, HEADS, CHUNK, CHUNK), (1, HEADS, CHUNK), (d,),
                    (CONV_W, e_b), ()]
    n_small = sum(math.prod(s) for s in small_shapes)
    blk_rows = -(-n_small // (N_DEV * LANES * SUBLANES)) * SUBLANES
    small_rows = blk_rows * N_DEV
    packed = _pack_small(small_grads, small_rows)
    g_a_in, (small_land,) = _wgrad(dz_a, h_a, 512, "wgrad_a_in", hosted=[_HostedScatterAll(packed)])
    ps_ai = _pair_reduce([g_a_in], [(0, 0, a_w_in.shape[2])], "pair_reduce_a_in")
    small_sum = _sum_slots(small_land, blk_rows, "sum_small")

    land_ai, small_gath = _exchange([_HostedChipScatter(ps_ai), _HostedGathers([small_sum], 0)], "tail_exchange")
    small_all = small_gath.reshape(small_rows, LANES)

    n_b_in = b_w_in.shape[2]
    gate_out = _reduce_adamw([[(land_f0, 0, f_shard)], [(land_f1gu, 0, f_shard)]], gate_t,
                             m_ffn_w_gate.transpose(0, 2, 1), v_ffn_w_gate.transpose(0, 2, 1), False, "adamw_gate")
    up_out = _reduce_adamw([[(land_f0, f_shard, f_shard)], [(land_f1gu, f_shard, f_shard)]], up_t,
                           m_ffn_w_up.transpose(0, 2, 1), v_ffn_w_up.transpose(0, 2, 1), False, "adamw_up")
    res = {
        "a_w_in": _reduce_adamw([[(land_ai, 0, a_w_in.shape[2])]], a_w_in, m_a_w_in, v_a_w_in, True, "adamw_a_in"),
        "a_w_out": _reduce_adamw([[(land_ao, 0, a_w_out.shape[1])]], a_w_out, m_a_w_out, v_a_w_out, False,
                                 "adamw_a_out"),
        "b_w_in": _reduce_adamw([[(land_b, 0, n_b_in)]], b_w_in, m_b_w_in, v_b_w_in, True, "adamw_b_in"),
        "b_w_out": _reduce_adamw([[(land_b, n_b_in, b_w_out.shape[1])]], b_w_out, m_b_w_out, v_b_w_out, False,
                                 "adamw_b_out"),
        "ffn_w_gate": [o.transpose(0, 2, 1) for o in gate_out],
        "ffn_w_up": [o.transpose(0, 2, 1) for o in up_out],
        "ffn_w_down": _reduce_adamw([[(land_f0, 2 * f_shard, f_shard)], [(land_f1d, 0, f_shard)]], ffn_w_down,
                                    m_ffn_w_down, v_ffn_w_down, False, "adamw_down"),
    }

    (gr_mix, gr_ffn, gr_gain, gr_bias, gr_ws, gr_bs, gr_final, gr_conv_full, loss) = _unpack_small(small_all, small_shapes)
    gr_conv = lax.dynamic_slice_in_dim(gr_conv_full, me * (e_b // N_DEV), e_b // N_DEV, axis=1)[None]

    small_w =[mix_norm, ffn_norm, a_v_gain, a_v_bias, a_w_s, a_b_s, final_norm]
    small_m = [m_mix_norm, m_ffn_norm, m_a_v_gain, m_a_v_bias, m_a_w_s, m_a_b_s, m_final_norm]
    small_v = [v_mix_norm, v_ffn_norm, v_a_v_gain, v_a_v_bias, v_a_w_s, v_a_b_s, v_final_norm]
    small_g = [gr_mix, gr_ffn, gr_gain, gr_bias, gr_ws, gr_bs, gr_final]
    sm_shapes = small_shapes[:len(small_w)]
    sm_out = _adamw(_pack_small(small_w, small_rows), _pack_small(small_g, small_rows), _pack_small(small_m, small_rows),
                    _pack_small(small_v, small_rows), small_rows, "adamw_small")
    sm_delta, sm_m, sm_v = [_unpack_small(o, sm_shapes) for o in sm_out]

    conv_out = _adamw(b_conv_w[0], gr_conv[0], m_b_conv_w[0], v_b_conv_w[0], CONV_W, "adamw_conv")
    conv_delta, conv_m, conv_v = [o[None] for o in conv_out]

    order = ["mix_norm", "ffn_norm", "a_w_in", "a_v_gain", "a_v_bias", "a_w_s", "a_b_s", "a_w_out", "b_w_in",
             "b_conv_w", "b_w_out", "ffn_w_gate", "ffn_w_up", "ffn_w_down", "final_norm"]
    small_names = ["mix_norm", "ffn_norm", "a_v_gain", "a_v_bias", "a_w_s", "a_b_s", "final_norm"]
    grads = {"b_conv_w": gr_conv}
    deltas, new_m, new_v = {}, {}, {}
    for k, name in enumerate(small_names):
        grads[name] = small_g[k]
        deltas[name], new_m[name], new_v[name] = sm_delta[k], sm_m[k], sm_v[k]
    deltas["b_conv_w"], new_m["b_conv_w"], new_v["b_conv_w"] = conv_delta, conv_m, conv_v
    for name, (gg, dl, mm, vv) in res.items():
        grads[name], deltas[name], new_m[name], new_v[name] = gg, dl, mm, vv

    grad_x = dx0.reshape(bsz, seq, d)
    return (loss, grad_x, *[grads[n] for n in order], *[deltas[n] for n in order],
            *[new_m[n] for n in order], *[new_v[n] for n in order])
```

```python
import math

import jax
import jax.numpy as jnp
from jax import lax
from jax.experimental import pallas as pl
from jax.experimental.pallas import tpu as pltpu

F32 = jnp.float32
BF16 = jnp.bfloat16

N_DEV = 8
N_CHIP = 4
CHUNK = 128
HEADS = 16
GROUP = 128
CONV_W = 3
NORM_EPS = 1e-6
GELU_C = math.sqrt(2.0 / math.pi)
GELU_K = 0.044715

ADAM_LR = 0.001
ADAM_B1 = 0.9
ADAM_B2 = 0.999
ADAM_EPS = 1e-08
ADAM_WD = 0.01
ADAM_STEP = 10

LANES = 128
SUBLANES = 8
VMEM_LIMIT = 60 * 1024 * 1024
HALO = 16
MXU_WIDTH = 256
FFN_CHUNKS = 2
TOKEN_TILE = 256
TOKEN_TILE_WIDE = 512
WGRAD_ROWS = 256
WGRAD_ROWS_WIDE = 512
GATHER_RELAY_AT = 0.56
GATHER_FORWARD_LEAD = 2

MESH = pl.DeviceIdType.MESH
ANY = pl.BlockSpec(memory_space=pl.ANY)


def _params(sequential=True):
    return pltpu.CompilerParams(
        dimension_semantics=("arbitrary",) if sequential else None,
        vmem_limit_bytes=VMEM_LIMIT)


def _nn(a, b):
    return jnp.dot(a, b, preferred_element_type=F32)


def _nt(a, b):
    return lax.dot_general(a, b, (((1,), (1,)), ((), ())), preferred_element_type=F32)


def _tn(a, b):
    return lax.dot_general(a, b, (((0,), (0,)), ((), ())), preferred_element_type=F32)


def _row_mean(a):
    return jnp.mean(a, axis=-1, keepdims=True)


def _col_sum(a):
    return jnp.sum(a, axis=0, keepdims=True)


def _rms_fwd(x, g):
    r = lax.rsqrt(_row_mean(x * x) + NORM_EPS)
    xhat = x * r
    return xhat * g, xhat, r


def _rms_bwd(dh, xhat, r, g):
    a = dh * g
    dx = r * (a - xhat * _row_mean(a * xhat))
    return dx, _col_sum(dh * xhat)


def _gelu_and_grad(x):
    x2 = x * x
    t = jnp.tanh(x * (GELU_C + (GELU_C * GELU_K) * x2))
    half = 0.5 * t + 0.5
    d = half + x * (0.5 - 0.5 * (t * t)) * (GELU_C + (3.0 * GELU_C * GELU_K) * x2)
    return x * half, d


def _sigmoid(x):
    return 1.0 / (1.0 + jnp.exp(-x))


def _row_spec(tm, width):
    return pl.BlockSpec((tm, width), lambda i: (i, 0))


def _const_spec(shape):
    nd = len(shape)
    return pl.BlockSpec(shape, lambda i: (0,) * nd)


def _load_group(parts, sems):
    @pl.when(pl.program_id(0) == 0)
    def _():
        copies = []
        for k, (gath_ref, first, n, dst) in enumerate(parts):
            for j in range(N_DEV):
                copies.append(pltpu.make_async_copy(gath_ref.at[j, pl.ds(first, n), :], dst.at[pl.ds(j * n, n), :],
                                                    sems.at[k * N_DEV + j]))
        for cp in copies:
            cp.start()
        for cp in copies:
            cp.wait()


def _hosting_call(body, name, n_steps, arrays, in_specs, out_specs, out_shape, scratch, hosted=()):
    n_in, n_out, n_scr = len(arrays), len(out_shape), len(scratch)
    h_arrays = [a for h in hosted for a in h.arrays]
    h_shapes = [s for h in hosted for s in h.out_shapes]
    h_sems = [s for h in hosted for s in h.sem_shapes]

    def full_body(*refs):
        pos = 0
        groups = []
        for n in (n_in, len(h_arrays), n_out, len(h_shapes), n_scr, len(h_sems)):
            groups.append(refs[pos:pos + n])
            pos += n
        own_in, h_in, own_out, h_out, own_scr, h_sem = groups
        per_host = []
        pi = po = ps = 0
        for h in hosted:
            ni, no, ns = len(h.arrays), len(h.out_shapes), len(h.sem_shapes)
            per_host.append((h, h_in[pi:pi + ni], h_out[po:po + no], h_sem[ps:ps + ns]))
            pi, po, ps = pi + ni, po + no, ps + ns
        for h, ins, outs, sems in per_host:
            h.begin(ins, outs, sems, n_steps)
        body(*own_in, *own_out, *own_scr)
        for h, ins, outs, sems in per_host:
            h.end(ins, outs, sems, n_steps)

    outs = pl.pallas_call(
        full_body, name=name, grid=(n_steps,),
        in_specs=list(in_specs) + [ANY] * len(h_arrays),
        out_specs=list(out_specs) + [ANY] * len(h_shapes),
        out_shape=list(out_shape) + h_shapes,
        scratch_shapes=list(scratch) + h_sems,
        compiler_params=_params(),
    )(*arrays, *h_arrays)
    return outs[:n_out], outs[n_out:]


def _my_index():
    return 4 * lax.axis_index("x") + 2 * lax.axis_index("y") + lax.axis_index("c")


PAIR_COLLECTIVE_ID = 0
GATHER_COPIES = 8


class _Gather:
    def __init__(self, shard, out, send_sems, recv_sems, local_sem):
        self.shard, self.out = shard, out
        self.send_sems, self.recv_sems, self.local_sem = send_sems, recv_sems, local_sem
        x, y, c = lax.axis_index("x"), lax.axis_index("y"), lax.axis_index("c")
        self.c = c
        self.me, self.sibling = (x, y, c), (x, y, 1 - c)
        self.xn, self.yn, self.dg = (1 - x, y), (x, 1 - y), (1 - x, 1 - y)
        self.n = shard.shape[0]
        self.half = self.n // 2
        rows_per_tile = SUBLANES * 4 // shard.dtype.itemsize
        self.relays = self.n % 2 == 0 and self.half % rows_per_tile == 0

    def _slot(self, dev, lo=0, hi=None):
        hi = self.n if hi is None else hi
        return self.out.at[4 * dev[0] + 2 * dev[1] + dev[2], pl.ds(lo, hi - lo), :]

    def _copy(self, k, block, to, src=None, lo=0, hi=None):
        return pltpu.make_async_remote_copy(
            src_ref=self._slot(block, lo, hi) if src is None else src, dst_ref=self._slot(block, lo, hi),
            send_sem=self.send_sems.at[k], recv_sem=self.recv_sems.at[k], device_id=to, device_id_type=MESH)

    def _local(self):
        return pltpu.make_async_copy(self.shard, self._slot(self.me), self.local_sem)

    def start(self):
        c = self.c
        self._local().start()
        self._copy(0, self.me, self.sibling, src=self.shard).start()
        self._copy(1, self.me, (*self.xn, c), src=self.shard).start()
        self._copy(2, self.me, (*self.yn, c), src=self.shard).start()
        if not self.relays:
            self._copy(3, self.me, (*self.dg, c), src=self.shard).start()

    def relay(self):
        c = self.c
        if self.relays:
            self._copy(1, (*self.xn, c), self.me).wait_recv()
            self._copy(3, (*self.xn, c), (*self.yn, c), hi=self.half).start()
            self._copy(2, (*self.yn, c), self.me).wait_recv()
            self._copy(4, (*self.yn, c), (*self.xn, c), lo=self.half).start()

    def forward(self):
        c = self.c
        if self.relays:
            self._copy(5, (*self.xn, c), self.sibling).start()
            self._copy(6, (*self.yn, c), self.sibling).start()
            self._copy(3, (*self.dg, c), self.me, hi=self.half).wait_recv()
            self._copy(4, (*self.dg, c), self.me, lo=self.half).wait_recv()
        else:
            self._copy(1, (*self.xn, c), self.me).wait_recv()
            self._copy(5, (*self.xn, c), self.sibling).start()
            self._copy(2, (*self.yn, c), self.me).wait_recv()
            self._copy(6, (*self.yn, c), self.sibling).start()
            self._copy(3, (*self.dg, c), self.me).wait_recv()
        self._copy(7, (*self.dg, c), self.sibling).start()

    def finish(self):
        c = self.c
        self._copy(0, self.sibling, self.me).wait_recv()
        for k, chip in ((5, self.xn), (6, self.yn), (7, self.dg)):
            self._copy(k, (*chip, 1 - c), self.me).wait_recv()
        for k in (0, 1, 2, 5, 6, 7):
            self._copy(k, self.me, self.sibling).wait_send()
        if self.relays:
            self._copy(3, self.me, self.sibling, hi=self.half).wait_send()
            self._copy(4, self.me, self.sibling, lo=self.half).wait_send()
        else:
            self._copy(3, self.me, self.sibling).wait_send()
        self._local().wait()


class _HostedGathers:
    def __init__(self, shards, mid_lead=GATHER_FORWARD_LEAD, relay_at=GATHER_RELAY_AT):
        n = len(shards)
        self.arrays = shards
        self.mid_lead, self.relay_at = mid_lead, relay_at
        self.out_shapes = [jax.ShapeDtypeStruct((N_DEV,) + s.shape, s.dtype) for s in shards]
        self.sem_shapes = [pltpu.SemaphoreType.DMA((n, GATHER_COPIES)), pltpu.SemaphoreType.DMA((n, GATHER_COPIES)),
                           pltpu.SemaphoreType.DMA((n,))]

    def _gathers(self, ins, outs, sems):
        return [_Gather(ins[a], outs[a], sems[0].at[a], sems[1].at[a], sems[2].at[a]) for a in range(len(ins))]

    def begin(self, ins, outs, sems, n_steps):
        i = pl.program_id(0)
        forward_step = max(n_steps - 1 - self.mid_lead, 0)
        relay_step = min(int(self.relay_at * n_steps), forward_step)

        @pl.when(i == 0)
        def _():
            for g in self._gathers(ins, outs, sems):
                g.start()

        if n_steps == 1:
            return

        @pl.when(i == relay_step)
        def _():
            for g in self._gathers(ins, outs, sems):
                g.relay()

        @pl.when(i == forward_step)
        def _():
            for g in self._gathers(ins, outs, sems):
                g.forward()

    def end(self, ins, outs, sems, n_steps):
        @pl.when(pl.program_id(0) == n_steps - 1)
        def _():
            gathers = self._gathers(ins, outs, sems)
            if n_steps == 1:
                for g in gathers:
                    g.relay()
                for g in gathers:
                    g.forward()
            for g in gathers:
                g.finish()


def _exchange(hosted, name):
    return _hosting_call(lambda: None, name, 1, [], [], [], [], [], hosted=hosted)[1]


class _ChipScatter:
    def __init__(self, pairsum, row0, land, send_sems, recv_sems, local_sem):
        self.pairsum, self.row0, self.land = pairsum, row0, land
        self.send_sems, self.recv_sems, self.local_sem = send_sems, recv_sems, local_sem
        x, y, c = lax.axis_index("x"), lax.axis_index("y"), lax.axis_index("c")
        self.c = c
        self.chip = 2 * x + y
        self.others = [(1 - x, y), (x, 1 - y), (1 - x, 1 - y)]

    def _src(self, chip):
        return self.pairsum.at[chip, pl.ds(self.row0, self.land.shape[1]), :]

    def _copy(self, k):
        ox, oy = self.others[k]
        return pltpu.make_async_remote_copy(
            src_ref=self._src(2 * ox + oy), dst_ref=self.land.at[self.chip],
            send_sem=self.send_sems.at[k], recv_sem=self.recv_sems.at[k], device_id=(ox, oy, self.c),
            device_id_type=MESH)

    def _arrival(self, k):
        ox, oy = self.others[k]
        return pltpu.make_async_remote_copy(
            src_ref=self._src(self.chip), dst_ref=self.land.at[2 * ox + oy],
            send_sem=self.send_sems.at[k], recv_sem=self.recv_sems.at[k], device_id=(ox, oy, self.c),
            device_id_type=MESH)

    def _local(self):
        return pltpu.make_async_copy(self._src(self.chip), self.land.at[self.chip], self.local_sem)

    def start(self):
        self._local().start()
        for k in range(N_CHIP - 1):
            self._copy(k).start()

    def finish(self):
        for k in range(N_CHIP - 1):
            self._arrival(k).wait_recv()
        for k in range(N_CHIP - 1):
            self._copy(k).wait_send()
        self._local().wait()


class _HostedChipScatter:
    def __init__(self, pairsum, row0=0, n=None):
        n = pairsum.shape[1] - row0 if n is None else n
        self.row0 = row0
        self.arrays = [pairsum]
        self.out_shapes = [jax.ShapeDtypeStruct((N_CHIP, n, pairsum.shape[2]), pairsum.dtype)]
        self.sem_shapes = [pltpu.SemaphoreType.DMA((N_CHIP - 1,)), pltpu.SemaphoreType.DMA((N_CHIP - 1,)),
                           pltpu.SemaphoreType.DMA(())]

    def begin(self, ins, outs, sems, n_steps):
        @pl.when(pl.program_id(0) == 0)
        def _():
            _ChipScatter(ins[0], self.row0, outs[0], *sems).start()

    def end(self, ins, outs, sems, n_steps):
        @pl.when(pl.program_id(0) == n_steps - 1)
        def _():
            _ChipScatter(ins[0], self.row0, outs[0], *sems).finish()


def _pair_reduce(arrays, entries, name):
    n_arr, n_ent = len(arrays), len(entries)
    cols = arrays[0].shape[1]
    offsets = []
    total = 0
    for _, _, n in entries:
        offsets.append(total)
        total += n

    def body(*refs):
        ins, out_ref = refs[:n_arr], refs[n_arr]
        rbuf, own, send_sems, recv_sems, own_sems = refs[n_arr + 1:]
        q = pl.program_id(0)
        x, y, c = lax.axis_index("x"), lax.axis_index("y"), lax.axis_index("c")

        def block(e, chip, core):
            ai, first, n = entries[e]
            return ins[ai].at[pl.ds(first + (2 * chip + core) * n, n), :]

        def to_sibling(e, chip):
            return pltpu.make_async_remote_copy(
                src_ref=block(e, chip, 1 - c), dst_ref=rbuf.at[chip, pl.ds(offsets[e], entries[e][2]), :],
                send_sem=send_sems.at[e, chip], recv_sem=recv_sems.at[e, chip], device_id=(x, y, 1 - c),
                device_id_type=MESH)

        @pl.when(q == 0)
        def _():
            barrier = pltpu.get_barrier_semaphore()
            pl.semaphore_signal(barrier, inc=1, device_id=(x, y, 1 - c), device_id_type=MESH)
            pl.semaphore_wait(barrier, 1)
            for chip in range(N_CHIP):
                for e in range(n_ent):
                    to_sibling(e, chip).start()

        loads = [pltpu.make_async_copy(block(e, q, c), own.at[pl.ds(offsets[e], entries[e][2]), :], own_sems.at[e])
                 for e in range(n_ent)]
        for cp in loads:
            cp.start()
        for cp in loads:
            cp.wait()
        for e in range(n_ent):
            to_sibling(e, q).wait_recv()
        out_ref[...] = (own[...].astype(F32) + rbuf[q].astype(F32)).astype(out_ref.dtype)

        @pl.when(q == N_CHIP - 1)
        def _():
            for chip in range(N_CHIP):
                for e in range(n_ent):
                    to_sibling(e, chip).wait_send()

    return pl.pallas_call(
        body, name=name, grid=(N_CHIP,),
        in_specs=[ANY] * n_arr,
        out_specs=pl.BlockSpec((None, total, cols), lambda q: (q, 0, 0)),
        out_shape=jax.ShapeDtypeStruct((N_CHIP, total, cols), BF16),
        scratch_shapes=[pltpu.VMEM((N_CHIP, total, cols), BF16), pltpu.VMEM((total, cols), BF16),
                        pltpu.SemaphoreType.DMA((n_ent, N_CHIP)), pltpu.SemaphoreType.DMA((n_ent, N_CHIP)),
                        pltpu.SemaphoreType.DMA((n_ent,))],
        compiler_params=pltpu.CompilerParams(dimension_semantics=("arbitrary",), vmem_limit_bytes=VMEM_LIMIT,
                                             collective_id=PAIR_COLLECTIVE_ID),
    )(*arrays)


class _HostedScatterAll:
    def __init__(self, packed):
        n = packed.shape[0] // N_DEV
        self.n = n
        self.arrays = [packed]
        self.out_shapes = [jax.ShapeDtypeStruct((N_DEV, n, packed.shape[1]), packed.dtype)]
        self.sem_shapes = [pltpu.SemaphoreType.DMA((N_DEV - 1,)), pltpu.SemaphoreType.DMA((N_DEV - 1,)),
                           pltpu.SemaphoreType.DMA(())]

    def _copies(self, ins, outs, sems, with_arrivals):
        src, land = ins[0], outs[0]
        send_sems, recv_sems, local_sem = sems
        me = _my_index()

        def block(p):
            return src.at[pl.ds(p * self.n, self.n), :]

        local = pltpu.make_async_copy(block(me), land.at[me], local_sem)
        sends, arrivals = [], []
        for k in range(1, N_DEV):
            p = (me + k) % N_DEV
            q = (me + N_DEV - k) % N_DEV
            sends.append(pltpu.make_async_remote_copy(
                src_ref=block(p), dst_ref=land.at[me], send_sem=send_sems.at[k - 1], recv_sem=recv_sems.at[k - 1],
                device_id=(p // 4, (p // 2) % 2, p % 2), device_id_type=MESH))
            if with_arrivals:
                arrivals.append(pltpu.make_async_remote_copy(
                    src_ref=block(me), dst_ref=land.at[q], send_sem=send_sems.at[k - 1], recv_sem=recv_sems.at[k - 1],
                    device_id=(q // 4, (q // 2) % 2, q % 2), device_id_type=MESH))
        return local, sends, arrivals

    def begin(self, ins, outs, sems, n_steps):
        @pl.when(pl.program_id(0) == 0)
        def _():
            local, sends, _ = self._copies(ins, outs, sems, with_arrivals=False)
            local.start()
            for cp in sends:
                cp.start()

    def end(self, ins, outs, sems, n_steps):
        @pl.when(pl.program_id(0) == n_steps - 1)
        def _():
            local, sends, arrivals = self._copies(ins, outs, sems, with_arrivals=True)
            for cp in arrivals:
                cp.wait_recv()
            for cp in sends:
                cp.wait_send()
            local.wait()


def _tril_weights(ws_ref):
    r = lax.broadcasted_iota(jnp.int32, (CHUNK, CHUNK), 0)
    c = lax.broadcasted_iota(jnp.int32, (CHUNK, CHUNK), 1)
    return [jnp.where(r >= c, ws_ref[h], 0.0).astype(BF16) for h in range(HEADS)]


def _sgu_stats(zpre, gain, bias):
    e = zpre.shape[1] // 2
    z, dz = _gelu_and_grad(zpre)
    u, v = z[:, :e], z[:, e:]
    vc = v - _row_mean(v)
    rstd = lax.rsqrt(_row_mean(vc * vc) + NORM_EPS)
    vhat = vc * rstd
    return u, vhat, rstd, vhat * gain + bias, dz


def _spatial_fwd(wt, vn_bf, bfull_ref, sv_ref, tm):
    for ci in range(tm // CHUNK):
        rows = slice(ci * CHUNK, (ci + 1) * CHUNK)
        for h in range(HEADS):
            cols = slice(h * GROUP, (h + 1) * GROUP)
            sv_ref[rows, cols] = _nn(wt[h], vn_bf[rows, cols]) + bfull_ref[:, cols]


def _mixer_a_fwd(x, g, gath, gain, bias, ws, bfull, tm, hosted=()):
    t_tok, d = x.shape
    e = gain.shape[1]
    e2 = 2 * e
    n_in, n_out = e2 // N_DEV, e // N_DEV

    def body(x_ref, g_ref, gain_ref, bias_ref, ws_ref, bfull_ref, gath_ref,
             xo_ref, gd_ref, u_ref, vhat_ref, svo_ref, y_ref, rstd_ref, win_v, wout_v, sv_v, sems):
        _load_group([(gath_ref, 0, n_in, win_v), (gath_ref, n_in, n_out, wout_v)], sems)
        xv = x_ref[...]
        h = _rms_fwd(xv, g_ref[...])[0].astype(BF16)
        zpre = _nt(h, win_v[...])
        u, vhat, rstd, vn, gelu_d = _sgu_stats(zpre, gain_ref[...], bias_ref[...])
        gd_ref[...] = gelu_d.astype(BF16)
        u_ref[...] = u.astype(BF16)
        vhat_ref[...] = vhat.astype(BF16)
        rstd_ref[...] = rstd
        _spatial_fwd(_tril_weights(ws_ref), vn.astype(BF16), bfull_ref, sv_v, tm)
        sv = sv_v[...]
        svo_ref[...] = sv.astype(BF16)
        y = (u * sv).astype(BF16)
        y_ref[...] = y
        xo_ref[...] = xv + _nn(y, wout_v[...])

    return _hosting_call(
        body, "mixer_a_fwd", t_tok // tm, [x, g, gain, bias, ws, bfull, gath],
        in_specs=[_row_spec(tm, d), _const_spec((1, d)), _const_spec((1, e)), _const_spec((1, e)),
                  _const_spec((HEADS, CHUNK, CHUNK)), _const_spec((CHUNK, e)), ANY],
        out_specs=[_row_spec(tm, d), _row_spec(tm, e2), _row_spec(tm, e), _row_spec(tm, e), _row_spec(tm, e),
                   _row_spec(tm, e), _row_spec(tm, 1)],
        out_shape=[jax.ShapeDtypeStruct((t_tok, d), F32), jax.ShapeDtypeStruct((t_tok, e2), BF16),
                   jax.ShapeDtypeStruct((t_tok, e), BF16), jax.ShapeDtypeStruct((t_tok, e), BF16),
                   jax.ShapeDtypeStruct((t_tok, e), BF16), jax.ShapeDtypeStruct((t_tok, e), BF16),
                   jax.ShapeDtypeStruct((t_tok, 1), F32)],
        scratch=[pltpu.VMEM((e2, d), BF16), pltpu.VMEM((e, d), BF16), pltpu.VMEM((tm, e), F32),
                 pltpu.SemaphoreType.DMA((2 * N_DEV,))],
        hosted=hosted)


def _mixer_a_bwd(dout, x, gd, u_sav, vhat_sav, sv_sav, rstd_sav, g, gath, gain, bias, ws, tm, hosted=()):
    t_tok, d = x.shape
    e = gain.shape[1]
    e2 = 2 * e
    n_in, n_out = e2 // N_DEV, e // N_DEV
    n_steps = t_tok // tm

    def body(dout_ref, x_ref, gd_ref, u_ref, vhat_ref, sv_ref, rstd_ref, g_ref, gain_ref, bias_ref, ws_ref, gath_ref,
             dx_ref, dxb_ref, h_ref, dz_ref, dg_ref, dgain_ref, dbias_ref, dws_ref, dbso_ref,
             win_v, wout_v, dvn_v, dbs_ref, sems):
        i = pl.program_id(0)
        _load_group([(gath_ref, 0, n_in, win_v), (gath_ref, n_in, n_out, wout_v)], sems)

        @pl.when(i == 0)
        def _():
            dg_ref[...] = jnp.zeros_like(dg_ref)
            dgain_ref[...] = jnp.zeros_like(dgain_ref)
            dbias_ref[...] = jnp.zeros_like(dbias_ref)
            dws_ref[...] = jnp.zeros_like(dws_ref)
            dbs_ref[...] = jnp.zeros_like(dbs_ref)

        xv = x_ref[...]
        gv = g_ref[...]
        hv, xhat, r = _rms_fwd(xv, gv)
        h_ref[...] = hv.astype(BF16)
        gain_v = gain_ref[...]
        vhat = vhat_ref[...].astype(F32)
        vn_bf = (vhat * gain_v + bias_ref[...]).astype(BF16)
        wt = _tril_weights(ws_ref)

        dov = dout_ref[...]
        dy = _nt(dov.astype(BF16), wout_v[...])
        du = dy * sv_ref[...].astype(F32)
        dsv = dy * u_ref[...].astype(F32)
        dsv_bf = dsv.astype(BF16)
        for ci in range(tm // CHUNK):
            rows = slice(ci * CHUNK, (ci + 1) * CHUNK)
            dbs_ref[...] += dsv[rows, :]
            for h in range(HEADS):
                cols = slice(h * GROUP, (h + 1) * GROUP)
                dvn_v[rows, cols] = _tn(wt[h], dsv_bf[rows, cols])
                dws_ref[h] += _nt(dsv_bf[rows, cols], vn_bf[rows, cols])
        dvn = dvn_v[...]
        dgain_ref[...] += _col_sum(dvn * vhat)
        dbias_ref[...] += _col_sum(dvn)
        dvhat = dvn * gain_v
        dv = rstd_ref[...] * (dvhat - _row_mean(dvhat) - vhat * _row_mean(dvhat * vhat))
        dzpre = (jnp.concatenate([du, dv], axis=1) * gd_ref[...].astype(F32)).astype(BF16)
        dz_ref[...] = dzpre
        dh = _nn(dzpre, win_v[...])
        dxr, dg_row = _rms_bwd(dh, xhat, r, gv)
        dg_ref[...] += dg_row
        dx = dov + dxr
        dx_ref[...] = dx
        dxb_ref[...] = dx.astype(BF16)

        @pl.when(i == n_steps - 1)
        def _():
            rr = lax.broadcasted_iota(jnp.int32, (CHUNK, CHUNK), 0)
            cc = lax.broadcasted_iota(jnp.int32, (CHUNK, CHUNK), 1)
            for h in range(HEADS):
                dws_ref[h] = jnp.where(rr >= cc, dws_ref[h], 0.0)
                dbso_ref[h] = jnp.sum(dbs_ref[:, h * GROUP:(h + 1) * GROUP], axis=1, keepdims=True)

    return _hosting_call(
        body, "mixer_a_bwd", n_steps, [dout, x, gd, u_sav, vhat_sav, sv_sav, rstd_sav, g, gain, bias, ws, gath],
        in_specs=[_row_spec(tm, d), _row_spec(tm, d), _row_spec(tm, e2), _row_spec(tm, e), _row_spec(tm, e),
                  _row_spec(tm, e), _row_spec(tm, 1), _const_spec((1, d)),
                  _const_spec((1, e)), _const_spec((1, e)), _const_spec((HEADS, CHUNK, CHUNK)), ANY],
        out_specs=[_row_spec(tm, d), _row_spec(tm, d), _row_spec(tm, d), _row_spec(tm, e2),
                   _const_spec((1, d)), _const_spec((1, e)), _const_spec((1, e)),
                   _const_spec((HEADS, CHUNK, CHUNK)), _const_spec((HEADS, CHUNK, 1))],
        out_shape=[jax.ShapeDtypeStruct((t_tok, d), F32), jax.ShapeDtypeStruct((t_tok, d), BF16),
                   jax.ShapeDtypeStruct((t_tok, d), BF16), jax.ShapeDtypeStruct((t_tok, e2), BF16),
                   jax.ShapeDtypeStruct((1, d), F32), jax.ShapeDtypeStruct((1, e), F32),
                   jax.ShapeDtypeStruct((1, e), F32), jax.ShapeDtypeStruct((HEADS, CHUNK, CHUNK), F32),
                   jax.ShapeDtypeStruct((HEADS, CHUNK, 1), F32)],
        scratch=[pltpu.VMEM((e2, d), BF16), pltpu.VMEM((e, d), BF16), pltpu.VMEM((tm, e), F32),
                 pltpu.VMEM((CHUNK, e), F32), pltpu.SemaphoreType.DMA((2 * N_DEV,))],
        hosted=hosted)


def _ffn_fwd(x, g, srcs, nf, tm, name, hosted=(), head=None):
    t_tok, d = x.shape
    f = nf * N_DEV
    firsts = [first for _, first in srcs]
    n_head = 2 if head else 0

    def body(*refs):
        x_ref, g_ref, sg_ref, su_ref, sd_ref = refs[:5]
        gate_ref, up_ref, wg_v, wu_v, wd_v, sems = refs[-6:]
        _load_group(
            [(sg_ref, firsts[0], nf, wg_v), (su_ref, firsts[1], nf, wu_v), (sd_ref, firsts[2], nf, wd_v)], sems)
        if head:
            t_ref, gf_ref, loss_ref, dx_ref, dxb_ref, dgf_ref = refs[5:11]

            @pl.when(pl.program_id(0) == 0)
            def _():
                loss_ref[...] = jnp.zeros_like(loss_ref)
                dgf_ref[...] = jnp.zeros_like(dgf_ref)

        xv = x_ref[...]
        h = _rms_fwd(xv, g_ref[...])[0].astype(BF16)
        gate = _nt(h, wg_v[...])
        up = _nt(h, wu_v[...])
        gate_ref[...] = gate.astype(BF16)
        up_ref[...] = up.astype(BF16)
        act = (gate * _sigmoid(gate) * up).astype(BF16)
        xo = xv + _nn(act, wd_v[...])
        if head:
            gfv = gf_ref[...]
            y, xhat, r = _rms_fwd(xo, gfv)
            err = y - t_ref[...]
            loss_ref[...] += 0.5 * jnp.sum(_row_mean(err * err), axis=0, keepdims=True)
            dxr, dg_row = _rms_bwd(err * (1.0 / d), xhat, r, gfv)
            dgf_ref[...] += dg_row
            dx_ref[...] = dxr
            dxb_ref[...] = dxr.astype(BF16)
        else:
            refs[5][...] = xo

    act_specs = [_row_spec(tm, f), _row_spec(tm, f)]
    act_shapes = [jax.ShapeDtypeStruct((t_tok, f), BF16), jax.ShapeDtypeStruct((t_tok, f), BF16)]
    if head:
        out_specs = [_const_spec((1, 1)), _row_spec(tm, d), _row_spec(tm, d), _const_spec((1, d))]
        out_shape = [jax.ShapeDtypeStruct((1, 1), F32), jax.ShapeDtypeStruct((t_tok, d), F32),
                     jax.ShapeDtypeStruct((t_tok, d), BF16), jax.ShapeDtypeStruct((1, d), F32)]
    else:
        out_specs = [_row_spec(tm, d)]
        out_shape = [jax.ShapeDtypeStruct((t_tok, d), F32)]
    return _hosting_call(
        body, name, t_tok // tm, [x, g] + [arr for arr, _ in srcs] + list(head or ()),
        in_specs=[_row_spec(tm, d), _const_spec((1, d)), ANY, ANY, ANY] + [_row_spec(tm, d), _const_spec((1, d))][:n_head],
        out_specs=out_specs + act_specs, out_shape=out_shape + act_shapes,
        scratch=[pltpu.VMEM((f, d), BF16), pltpu.VMEM((f, d), BF16), pltpu.VMEM((f, d), BF16),
                 pltpu.SemaphoreType.DMA((3 * N_DEV,))],
        hosted=hosted)


def _ffn_bwd(dout, x, gate, up, g, srcs, nf, tm, name, hosted=()):
    t_tok, d = x.shape
    f = nf * N_DEV
    firsts = [first for _, first in srcs]
    per_chunk = -(-f // (FFN_CHUNKS * MXU_WIDTH)) * MXU_WIDTH
    bounds = [min(ck * per_chunk, f) for ck in range(FFN_CHUNKS + 1)]

    def body(dout_ref, x_ref, gate_ref, up_ref, g_ref, sg_ref, su_ref, sd_ref,
             dx_ref, dxb_ref, h_ref, act_ref, dgu_ref, dg_ref, wg_v, wu_v, wd_v, sems):
        _load_group(
            [(sg_ref, firsts[0], nf, wg_v), (su_ref, firsts[1], nf, wu_v), (sd_ref, firsts[2], nf, wd_v)], sems)

        @pl.when(pl.program_id(0) == 0)
        def _():
            dg_ref[...] = jnp.zeros_like(dg_ref)

        xv = x_ref[...]
        gv = g_ref[...]
        hv, xhat, r = _rms_fwd(xv, gv)
        h_ref[...] = hv.astype(BF16)
        dov = dout_ref[...]
        dob = dov.astype(BF16)
        dh = None
        for ck in range(FFN_CHUNKS):
            cols = slice(bounds[ck], bounds[ck + 1])
            gate_v = gate_ref[:, cols].astype(F32)
            up_v = up_ref[:, cols].astype(F32)
            sig = _sigmoid(gate_v)
            silu = gate_v * sig
            act_ref[:, cols] = (silu * up_v).astype(BF16)
            dact = _nt(dob, wd_v[cols, :])
            dup = (dact * silu).astype(BF16)
            dgate = (dact * up_v * (sig * (1.0 + gate_v * (1.0 - sig)))).astype(BF16)
            dgu_ref[:, cols] = dgate
            dgu_ref[:, f + bounds[ck]:f + bounds[ck + 1]] = dup
            part = _nn(dgate, wg_v[cols, :]) + _nn(dup, wu_v[cols, :])
            dh = part if dh is None else dh + part
        dxr, dg_row = _rms_bwd(dh, xhat, r, gv)
        dg_ref[...] += dg_row
        dx = dov + dxr
        dx_ref[...] = dx
        dxb_ref[...] = dx.astype(BF16)

    return _hosting_call(
        body, name, t_tok // tm, [dout, x, gate, up, g] + [arr for arr, _ in srcs],
        in_specs=[_row_spec(tm, d), _row_spec(tm, d), _row_spec(tm, f), _row_spec(tm, f), _const_spec((1, d)),
                  ANY, ANY, ANY],
        out_specs=[_row_spec(tm, d), _row_spec(tm, d), _row_spec(tm, d), _row_spec(tm, f), _row_spec(tm, 2 * f),
                   _const_spec((1, d))],
        out_shape=[jax.ShapeDtypeStruct((t_tok, d), F32), jax.ShapeDtypeStruct((t_tok, d), BF16),
                   jax.ShapeDtypeStruct((t_tok, d), BF16), jax.ShapeDtypeStruct((t_tok, f), BF16),
                   jax.ShapeDtypeStruct((t_tok, 2 * f), BF16), jax.ShapeDtypeStruct((1, d), F32)],
        scratch=[pltpu.VMEM((f, d), BF16), pltpu.VMEM((f, d), BF16), pltpu.VMEM((f, d), BF16),
                 pltpu.SemaphoreType.DMA((3 * N_DEV,))],
        hosted=hosted)


def _shift_down(z, k, prev_rows):
    row = lax.broadcasted_iota(jnp.int32, z.shape, 0)
    out = pltpu.roll(z, k, 0)
    for j in range(k):
        out = jnp.where(row == j, prev_rows[j], out)
    return out


def _shift_up(z, k, next_rows):
    tm = z.shape[0]
    row = lax.broadcasted_iota(jnp.int32, z.shape, 0)
    out = pltpu.roll(z, tm - k, 0)
    for j in range(k):
        out = jnp.where(row == tm - k + j, next_rows[j], out)
    return out


def _mixer_b_fwd(x, g, gath, conv_w, tm, seq, hosted=()):
    t_tok, d = x.shape
    e = conv_w.shape[1]
    e3 = 3 * e
    n_in, n_out = e3 // N_DEV, e // N_DEV
    tiles_per_seq = seq // tm

    def body(x_ref, g_ref, cw_ref, gath_ref, xo_ref, p_ref, win_v, wout_v, tail_v, sems):
        i = pl.program_id(0)
        _load_group([(gath_ref, 0, n_in, win_v), (gath_ref, n_in, n_out, wout_v)], sems)

        @pl.when(i % tiles_per_seq == 0)
        def _():
            tail_v[...] = jnp.zeros_like(tail_v)

        xv = x_ref[...]
        h = _rms_fwd(xv, g_ref[...])[0].astype(BF16)
        p = _nt(h, win_v[...])
        p_ref[...] = p.astype(BF16)
        z = p[:, e:2 * e] * p[:, 2 * e:]
        prev = [tail_v[SUBLANES - 2:SUBLANES - 1, :], tail_v[SUBLANES - 1:SUBLANES, :]]
        conv = (cw_ref[2:3, :] * z + cw_ref[1:2, :] * _shift_down(z, 1, prev[1:])
                + cw_ref[0:1, :] * _shift_down(z, 2, prev))
        tail_v[...] = z[tm - SUBLANES:, :]
        y = (p[:, :e] * conv).astype(BF16)
        xo_ref[...] = xv + _nn(y, wout_v[...])

    return _hosting_call(
        body, "mixer_b_fwd", t_tok // tm, [x, g, conv_w, gath],
        in_specs=[_row_spec(tm, d), _const_spec((1, d)), _const_spec((SUBLANES, e)), ANY],
        out_specs=[_row_spec(tm, d), _row_spec(tm, e3)],
        out_shape=[jax.ShapeDtypeStruct((t_tok, d), F32), jax.ShapeDtypeStruct((t_tok, e3), BF16)],
        scratch=[pltpu.VMEM((e3, d), BF16), pltpu.VMEM((e, d), BF16), pltpu.VMEM((SUBLANES, e), F32),
                 pltpu.SemaphoreType.DMA((2 * N_DEV,))],
        hosted=hosted)


def _mixer_b_bwd(dout, x, p, g, gath, conv_w, tm, seq, hosted=()):
    t_tok, d = x.shape
    e = conv_w.shape[1]
    e3 = 3 * e
    n_in, n_out = e3 // N_DEV, e // N_DEV
    tiles_per_seq = seq // tm
    halo_per_tile = tm // HALO
    n_halo = t_tok // HALO

    def body(dout_ref, dnext_ref, x_ref, p_ref, pprev_ref, pnext_ref, g_ref, cw_ref, gath_ref,
             dx_ref, dxb_ref, h_ref, y_ref, dp_ref, dg_ref, dcw_ref, win_v, wout_v, sems):
        i = pl.program_id(0)
        _load_group([(gath_ref, 0, n_in, win_v), (gath_ref, n_in, n_out, wout_v)], sems)

        @pl.when(i == 0)
        def _():
            dg_ref[...] = jnp.zeros_like(dg_ref)
            dcw_ref[...] = jnp.zeros_like(dcw_ref)

        first = (i % tiles_per_seq == 0).astype(F32)
        last = (i % tiles_per_seq == tiles_per_seq - 1).astype(F32)
        xv = x_ref[...]
        gv = g_ref[...]
        hv, xhat, r = _rms_fwd(xv, gv)
        h_ref[...] = hv.astype(BF16)
        pv = p_ref[...].astype(F32)
        bg, cg, hx = pv[:, :e], pv[:, e:2 * e], pv[:, 2 * e:]
        z = cg * hx
        pprev = pprev_ref[...].astype(F32)
        zprev = pprev[:, e:2 * e] * pprev[:, 2 * e:] * (1.0 - first)
        prev = [zprev[HALO - 2:HALO - 1, :], zprev[HALO - 1:HALO, :]]
        zs1 = _shift_down(z, 1, prev[1:])
        zs2 = _shift_down(z, 2, prev)
        w0, w1, w2 = cw_ref[0:1, :], cw_ref[1:2, :], cw_ref[2:3, :]
        conv = w2 * z + w1 * zs1 + w0 * zs2
        y_ref[...] = (bg * conv).astype(BF16)

        dov = dout_ref[...]
        wout_bf = wout_v[...]
        dy = _nt(dov.astype(BF16), wout_bf)
        dconv = dy * bg
        dnext = _nt(dnext_ref[...].astype(BF16), wout_bf) * pnext_ref[:, :e].astype(F32) * (1.0 - last)
        nxt = [dnext[0:1, :], dnext[1:2, :]]
        dz = w2 * dconv + w1 * _shift_up(dconv, 1, nxt[:1]) + w0 * _shift_up(dconv, 2, nxt)
        dcw_ref[0:1, :] += _col_sum(dconv * zs2)
        dcw_ref[1:2, :] += _col_sum(dconv * zs1)
        dcw_ref[2:3, :] += _col_sum(dconv * z)
        dp = jnp.concatenate([dy * conv, dz * hx, dz * cg], axis=1).astype(BF16)
        dp_ref[...] = dp
        dh = _nn(dp, win_v[...])
        dxr, dg_row = _rms_bwd(dh, xhat, r, gv)
        dg_ref[...] += dg_row
        dx = dov + dxr
        dx_ref[...] = dx
        dxb_ref[...] = dx.astype(BF16)

    prev_spec = lambda w: pl.BlockSpec((HALO, w), lambda i: (jnp.maximum(i * halo_per_tile - 1, 0), 0))
    next_spec = lambda w: pl.BlockSpec((HALO, w), lambda i: (jnp.minimum((i + 1) * halo_per_tile, n_halo - 1), 0))
    return _hosting_call(
        body, "mixer_b_bwd", t_tok // tm, [dout, dout, x, p, p, p, g, conv_w, gath],
        in_specs=[_row_spec(tm, d), next_spec(d), _row_spec(tm, d), _row_spec(tm, e3), prev_spec(e3), next_spec(e3),
                  _const_spec((1, d)), _const_spec((SUBLANES, e)), ANY],
        out_specs=[_row_spec(tm, d), _row_spec(tm, d), _row_spec(tm, d), _row_spec(tm, e), _row_spec(tm, e3),
                   _const_spec((1, d)), _const_spec((SUBLANES, e))],
        out_shape=[jax.ShapeDtypeStruct((t_tok, d), F32), jax.ShapeDtypeStruct((t_tok, d), BF16),
                   jax.ShapeDtypeStruct((t_tok, d), BF16), jax.ShapeDtypeStruct((t_tok, e), BF16),
                   jax.ShapeDtypeStruct((t_tok, e3), BF16), jax.ShapeDtypeStruct((1, d), F32),
                   jax.ShapeDtypeStruct((SUBLANES, e), F32)],
        scratch=[pltpu.VMEM((e3, d), BF16), pltpu.VMEM((e, d), BF16), pltpu.SemaphoreType.DMA((2 * N_DEV,))],
        hosted=hosted)


def _wgrad(a, b, bm, name, hosted=()):
    t_tok, m = a.shape
    n = b.shape[1]

    def body(a_ref, b_ref, o_ref):
        o_ref[...] = _tn(a_ref[...], b_ref[...]).astype(o_ref.dtype)

    outs, h_outs = _hosting_call(
        body, name, m // bm, [a, b],
        in_specs=[pl.BlockSpec((t_tok, bm), lambda i: (0, i)), _const_spec((t_tok, n))],
        out_specs=[pl.BlockSpec((bm, n), lambda i: (i, 0))],
        out_shape=[jax.ShapeDtypeStruct((m, n), BF16)],
        scratch=[], hosted=hosted)
    return (outs[0], h_outs) if hosted else outs[0]


def _sum_slots(land, rb, name):
    n_slots, rows, cols = land.shape

    def body(l_ref, o_ref):
        acc = l_ref[0].astype(F32)
        for k in range(1, n_slots):
            acc = acc + l_ref[k].astype(F32)
        o_ref[...] = acc

    return pl.pallas_call(
        body, name=name, grid=(rows // rb,),
        in_specs=[pl.BlockSpec((n_slots, rb, cols), lambda i: (0, i, 0))],
        out_specs=pl.BlockSpec((rb, cols), lambda i: (i, 0)),
        out_shape=jax.ShapeDtypeStruct((rows, cols), F32),
        compiler_params=_params(sequential=False),
    )(land)


def _adamw(w, grad, m, v, rb, name):
    rows, cols = w.shape
    c1 = 1.0 / (1.0 - ADAM_B1 ** ADAM_STEP)
    c2 = 1.0 / (1.0 - ADAM_B2 ** ADAM_STEP)

    def body(w_ref, g_ref, m_ref, v_ref, d_ref, mo_ref, vo_ref):
        gv = g_ref[...]
        mn = ADAM_B1 * m_ref[...] + (1.0 - ADAM_B1) * gv
        vn = ADAM_B2 * v_ref[...] + (1.0 - ADAM_B2) * (gv * gv)
        mo_ref[...] = mn
        vo_ref[...] = vn
        d_ref[...] = -ADAM_LR * ((mn * c1) / (jnp.sqrt(vn * c2) + ADAM_EPS) + ADAM_WD * w_ref[...])

    spec = pl.BlockSpec((rb, cols), lambda i: (i, 0))
    shape = jax.ShapeDtypeStruct((rows, cols), F32)
    return pl.pallas_call(
        body, name=name, grid=(rows // rb,),
        in_specs=[spec] * 4, out_specs=[spec] * 3, out_shape=[shape] * 3,
        compiler_params=_params(sequential=False),
    )(w, grad, m, v)


def _pack_shards(groups, name, hosted=()):
    flat = [(part, layer) for group in groups for part, layer, _ in group]
    rows = [[p.shape[2] if turn else p.shape[1] for p, _, turn in group] for group in groups]
    first, _, first_turn = groups[0][0]
    width = first.shape[1] if first_turn else first.shape[2]

    def body(*refs):
        ins, outs = refs[:len(flat)], refs[len(flat):]
        k = 0
        for gi, group in enumerate(groups):
            off = 0
            for (_, _, turn), n in zip(group, rows[gi]):
                part = ins[k][...].astype(BF16)
                if turn:
                    r = lax.broadcasted_iota(jnp.int32, (n, n), 0)
                    c = lax.broadcasted_iota(jnp.int32, (n, n), 1)
                    part = _nt((r == c).astype(BF16), part).astype(BF16)
                outs[gi][off:off + n, :] = part
                off += n
                k += 1

    return _hosting_call(
        body, name, 1, [p for p, _ in flat],
        in_specs=[pl.BlockSpec((None,) + p.shape[1:], lambda i, layer=layer: (layer, 0, 0)) for p, layer in flat],
        out_specs=[_const_spec((sum(r), width)) for r in rows],
        out_shape=[jax.ShapeDtypeStruct((sum(r), width), BF16) for r in rows],
        scratch=[], hosted=hosted)


def _split_bf16(a):
    hi = a.astype(BF16)
    rest = a - hi.astype(F32)
    mid = rest.astype(BF16)
    return hi, mid, (rest - mid.astype(F32)).astype(BF16)


def _reduce_adamw(lands, w, m, v, transpose, name, hosted=()):
    n_layers, rows_w, cols_w = w.shape
    c1 = 1.0 / (1.0 - ADAM_B1 ** ADAM_STEP)
    c2 = 1.0 / (1.0 - ADAM_B2 ** ADAM_STEP)
    flat = [piece for pieces in lands for piece in pieces]
    counts = [len(pieces) for pieces in lands]
    if transpose:
        tiles = rows_w // MXU_WIDTH
        blk = (MXU_WIDTH, cols_w)
        land_specs = [pl.BlockSpec((N_CHIP, n, MXU_WIDTH), lambda i, b=first // n: (0, b, i % tiles))
                      for _, first, n in flat]
        for _, first, n in flat:
            assert first % n == 0
    else:
        tiles = 2
        blk = (rows_w // tiles, cols_w)
        assert all(c == 1 for c in counts)
        land_specs = [pl.BlockSpec((N_CHIP,) + blk, lambda i, b=first // blk[0]: (0, b + i % tiles, 0))
                      for _, first, _ in flat]
        for _, first, _ in flat:
            assert first % blk[0] == 0

    def body(*refs):
        land_refs = refs[:len(flat)]
        w_ref, m_ref, v_ref, g_ref, d_ref, mo_ref, vo_ref = refs[len(flat):]
        layer = pl.program_id(0) // tiles

        def total(ref):
            acc = ref[0].astype(F32)
            for q in range(1, N_CHIP):
                acc = acc + ref[q].astype(F32)
            return acc

        def layer_sum(k):
            first = sum(counts[:k])
            parts = [total(land_refs[first + j]) for j in range(counts[k])]
            return parts[0] if len(parts) == 1 else jnp.concatenate(parts, axis=0)

        gv = layer_sum(0)
        for k in range(1, n_layers):
            gv = jnp.where(layer == k, layer_sum(k), gv)
        if transpose:
            r = lax.broadcasted_iota(jnp.int32, (MXU_WIDTH, MXU_WIDTH), 0)
            c = lax.broadcasted_iota(jnp.int32, (MXU_WIDTH, MXU_WIDTH), 1)
            eye = (r == c).astype(BF16)
            hi, mid, lo = _split_bf16(gv)
            gv = _nt(eye, hi) + _nt(eye, mid) + _nt(eye, lo)
        g_ref[...] = gv
        mn = ADAM_B1 * m_ref[...] + (1.0 - ADAM_B1) * gv
        vn = ADAM_B2 * v_ref[...] + (1.0 - ADAM_B2) * (gv * gv)
        mo_ref[...] = mn
        vo_ref[...] = vn
        d_ref[...] = -ADAM_LR * ((mn * c1) / (jnp.sqrt(vn * c2) + ADAM_EPS) + ADAM_WD * w_ref[...])

    spec = pl.BlockSpec((None,) + blk, lambda i: (i // tiles, i % tiles, 0))
    shape = jax.ShapeDtypeStruct(w.shape, F32)
    outs, h_outs = _hosting_call(
        body, name, n_layers * tiles, [land for land, _, _ in flat] + [w, m, v],
        in_specs=land_specs + [spec] * 3, out_specs=[spec] * 4, out_shape=[shape] * 4, scratch=[], hosted=hosted)
    return (outs, h_outs) if hosted else outs


def _pack_small(parts, rows):
    flat = jnp.concatenate([p.reshape(-1).astype(F32) for p in parts])
    return jnp.pad(flat, (0, rows * LANES - flat.shape[0])).reshape(rows, LANES)


def _unpack_small(packed, shapes):
    flat = packed.reshape(-1)
    out = []
    pos = 0
    for s in shapes:
        n = math.prod(s)
        out.append(flat[pos:pos + n].reshape(s))
        pos += n
    return out


def kernel(x, mix_norm, ffn_norm, a_w_in, a_v_gain, a_v_bias, a_w_s, a_b_s, a_w_out, b_w_in, b_conv_w, b_w_out, ffn_w_gate, ffn_w_up, ffn_w_down, final_norm, loss_target, m_mix_norm, m_ffn_norm, m_a_w_in, m_a_v_gain, m_a_v_bias, m_a_w_s, m_a_b_s, m_a_w_out, m_b_w_in, m_b_conv_w, m_b_w_out, m_ffn_w_gate, m_ffn_w_up, m_ffn_w_down, m_final_norm, v_mix_norm, v_ffn_norm, v_a_w_in, v_a_v_gain, v_a_v_bias, v_a_w_s, v_a_b_s, v_a_w_out, v_b_w_in, v_b_conv_w, v_b_w_out, v_ffn_w_gate, v_ffn_w_up, v_ffn_w_down, v_final_norm):
    bsz, seq, d = x.shape
    t_tok = bsz * seq
    me = _my_index()
    xt = x.reshape(t_tok, d)
    target = loss_target.reshape(t_tok, d)
    e_a = a_v_gain.shape[1]
    e_b = b_w_out.shape[1] * N_DEV
    n_layers = ffn_w_gate.shape[0]
    f_shard = ffn_w_gate.shape[2]
    f_full = f_shard * N_DEV

    conv_pad = jnp.pad(b_conv_w[0], ((0, SUBLANES - CONV_W), (0, 0)))
    sh_a = jnp.concatenate([a_w_in[0].T, a_w_out[0]]).astype(BF16)
    bfull = jnp.repeat(a_b_s[0].T, GROUP, axis=1)

    gate_t, up_t = ffn_w_gate.transpose(0, 2, 1), ffn_w_up.transpose(0, 2, 1)
    (sh_b, sh_f0, sh_f1g, sh_f1ud), (gath_a, conv_g) = _pack_shards(
        [[(b_w_in, 0, True), (b_w_out, 0, False)],
         [(gate_t, 0, False), (up_t, 0, False), (ffn_w_down, 0, False)],
         [(gate_t, 1, False)],
         [(up_t, 1, False), (ffn_w_down, 1, False)]],
        "pack_shards", hosted=[_HostedGathers([sh_a, conv_pad])])
    conv_full = jnp.pad(conv_g[:, :CONV_W, :].transpose(1, 0, 2).reshape(CONV_W, e_b), ((0, SUBLANES - CONV_W), (0, 0)))
    (x1, gd_a, u_a, vhat_a, sv_a, y_a, rstd_a), (gath_f0,) = _mixer_a_fwd(
        xt, mix_norm[0:1], gath_a, a_v_gain, a_v_bias, a_w_s[0], bfull, tm=TOKEN_TILE,
        hosted=[_HostedGathers([sh_f0])])
    srcs0 = [(gath_f0, 0), (gath_f0, f_shard), (gath_f0, 2 * f_shard)]
    (x2, gate0, up0), (gath_b, gath_f1g) = _ffn_fwd(x1, ffn_norm[0:1], srcs0, f_shard, tm=TOKEN_TILE, name="ffn_fwd0",
                                                    hosted=[_HostedGathers([sh_b, sh_f1g])])
    (x3, p_b), (gath_f1ud,) = _mixer_b_fwd(x2, mix_norm[1:2], gath_b, conv_full, tm=TOKEN_TILE, seq=seq,
                                           hosted=[_HostedGathers([sh_f1ud])])
    srcs1 = [(gath_f1g, 0), (gath_f1ud, 0), (gath_f1ud, f_shard)]
    (loss_part, dx4, dx4_bf, d_final, gate1, up1), _ = _ffn_fwd(
        x3, ffn_norm[1:2], srcs1, f_shard, tm=TOKEN_TILE, name="ffn_fwd1", head=(target, final_norm.reshape(1, d)))

    ffn_entries = [(0, 0, f_shard), (0, f_full, f_shard), (1, 0, f_shard)]
    (dx3, dx3_bf, h_f1, act1, dgu1, d_fn1), _ = _ffn_bwd(dx4, x3, gate1, up1, ffn_norm[1:2], srcs1, f_shard, tm=TOKEN_TILE,
                                                         name="ffn_bwd1")
    g_down1 = _wgrad(act1, dx4_bf, WGRAD_ROWS, "wgrad_down1")
    g_gu1 = _wgrad(dgu1, h_f1, WGRAD_ROWS_WIDE, "wgrad_gate_up1")
    ps_f1 = _pair_reduce([g_gu1, g_down1], ffn_entries, "pair_reduce_f1")
    (dx2, dx2_bf, h_b, y_b, dp_b, d_mn1, d_conv), (land_f1gu,) = _mixer_b_bwd(
        dx3, x2, p_b, mix_norm[1:2], gath_b, conv_full, tm=TOKEN_TILE_WIDE, seq=seq,
        hosted=[_HostedChipScatter(ps_f1, 0, 2 * f_shard)])
    g_b_out = _wgrad(y_b, dx3_bf, WGRAD_ROWS, "wgrad_b_out")
    g_b_in = _wgrad(dp_b, h_b, WGRAD_ROWS_WIDE, "wgrad_b_in")
    ps_b = _pair_reduce([g_b_in, g_b_out], [(0, 0, b_w_in.shape[2]), (1, 0, b_w_out.shape[1])], "pair_reduce_b")
    (dx1, dx1_bf, h_f0, act0, dgu0, d_fn0), (land_f1d, land_b) = _ffn_bwd(
        dx2, x1, gate0, up0, ffn_norm[0:1], srcs0, f_shard, tm=TOKEN_TILE, name="ffn_bwd0",
        hosted=[_HostedChipScatter(ps_f1, 2 * f_shard, f_shard), _HostedChipScatter(ps_b)])
    g_down0 = _wgrad(act0, dx2_bf, WGRAD_ROWS, "wgrad_down0")
    g_gu0 = _wgrad(dgu0, h_f0, WGRAD_ROWS_WIDE, "wgrad_gate_up0")
    g_a_out = _wgrad(y_a, dx1_bf, WGRAD_ROWS, "wgrad_a_out")
    n_ao = a_w_out.shape[1]
    ps_f0ao = _pair_reduce([g_gu0, g_down0, g_a_out], ffn_entries + [(2, 0, n_ao)], "pair_reduce_f0_a_out")
    (dx0, _, h_a, dz_a, d_mn0, d_gain, d_bias, d_ws, d_bs_acc), (land_f0, land_ao) = _mixer_a_bwd(
        dx1, xt, gd_a, u_a, vhat_a, sv_a, rstd_a, mix_norm[0:1], gath_a, a_v_gain, a_v_bias, a_w_s[0], tm=TOKEN_TILE,
        hosted=[_HostedChipScatter(ps_f0ao, 0, 3 * f_shard), _HostedChipScatter(ps_f0ao, 3 * f_shard, n_ao)])
    d_bs = d_bs_acc.reshape(HEADS, CHUNK)

    small_grads = [jnp.concatenate([d_mn0, d_mn1]), jnp.concatenate([d_fn0, d_fn1]), d_gain, d_bias, d_ws, d_bs,
                   d_final, d_conv[:CONV_W], loss_part]
    small_shapes = [(n_layers, d), (n_layers, d), (1, e_a), (1, e_a), (1, HEADS, CHUNK, CHUNK), (1, HEADS, CHUNK), (d,),
                    (CONV_W, e_b), ()]
    n_small = sum(math.prod(s) for s in small_shapes)
    blk_rows = -(-n_small // (N_DEV * LANES * SUBLANES)) * SUBLANES
    small_rows = blk_rows * N_DEV
    packed = _pack_small(small_grads, small_rows)
    g_a_in, (small_land,) = _wgrad(dz_a, h_a, WGRAD_ROWS_WIDE, "wgrad_a_in", hosted=[_HostedScatterAll(packed)])
    ps_ai = _pair_reduce([g_a_in], [(0, 0, a_w_in.shape[2])], "pair_reduce_a_in")
    small_sum = _sum_slots(small_land, blk_rows, "sum_small")

    land_ai, small_gath = _exchange([_HostedChipScatter(ps_ai), _HostedGathers([small_sum])], "tail_exchange")
    small_all = small_gath.reshape(small_rows, LANES)

    n_b_in = b_w_in.shape[2]
    gate_out = _reduce_adamw([[(land_f0, 0, f_shard)], [(land_f1gu, 0, f_shard)]], gate_t,
                             m_ffn_w_gate.transpose(0, 2, 1), v_ffn_w_gate.transpose(0, 2, 1), False, "adamw_gate")
    up_out = _reduce_adamw([[(land_f0, f_shard, f_shard)], [(land_f1gu, f_shard, f_shard)]], up_t,
                           m_ffn_w_up.transpose(0, 2, 1), v_ffn_w_up.transpose(0, 2, 1), False, "adamw_up")
    res = {
        "a_w_in": _reduce_adamw([[(land_ai, 0, a_w_in.shape[2])]], a_w_in, m_a_w_in, v_a_w_in, True, "adamw_a_in"),
        "a_w_out": _reduce_adamw([[(land_ao, 0, a_w_out.shape[1])]], a_w_out, m_a_w_out, v_a_w_out, False,
                                 "adamw_a_out"),
        "b_w_in": _reduce_adamw([[(land_b, 0, n_b_in)]], b_w_in, m_b_w_in, v_b_w_in, True, "adamw_b_in"),
        "b_w_out": _reduce_adamw([[(land_b, n_b_in, b_w_out.shape[1])]], b_w_out, m_b_w_out, v_b_w_out, False,
                                 "adamw_b_out"),
        "ffn_w_gate": [o.transpose(0, 2, 1) for o in gate_out],
        "ffn_w_up": [o.transpose(0, 2, 1) for o in up_out],
        "ffn_w_down": _reduce_adamw([[(land_f0, 2 * f_shard, f_shard)], [(land_f1d, 0, f_shard)]], ffn_w_down,
                                    m_ffn_w_down, v_ffn_w_down, False, "adamw_down"),
    }

    (gr_mix, gr_ffn, gr_gain, gr_bias, gr_ws, gr_bs, gr_final, gr_conv_full, loss) = _unpack_small(small_all, small_shapes)
    gr_conv = lax.dynamic_slice_in_dim(gr_conv_full, me * (e_b // N_DEV), e_b // N_DEV, axis=1)[None]

    small_w =[mix_norm, ffn_norm, a_v_gain, a_v_bias, a_w_s, a_b_s, final_norm]
    small_m = [m_mix_norm, m_ffn_norm, m_a_v_gain, m_a_v_bias, m_a_w_s, m_a_b_s, m_final_norm]
    small_v = [v_mix_norm, v_ffn_norm, v_a_v_gain, v_a_v_bias, v_a_w_s, v_a_b_s, v_final_norm]
    small_g = [gr_mix, gr_ffn, gr_gain, gr_bias, gr_ws, gr_bs, gr_final]
    sm_shapes = small_shapes[:len(small_w)]
    sm_out = _adamw(_pack_small(small_w, small_rows), _pack_small(small_g, small_rows), _pack_small(small_m, small_rows),
                    _pack_small(small_v, small_rows), small_rows, "adamw_small")
    sm_delta, sm_m, sm_v = [_unpack_small(o, sm_shapes) for o in sm_out]

    conv_out = _adamw(b_conv_w[0], gr_conv[0], m_b_conv_w[0], v_b_conv_w[0], CONV_W, "adamw_conv")
    conv_delta, conv_m, conv_v = [o[None] for o in conv_out]

    order = ["mix_norm", "ffn_norm", "a_w_in", "a_v_gain", "a_v_bias", "a_w_s", "a_b_s", "a_w_out", "b_w_in",
             "b_conv_w", "b_w_out", "ffn_w_gate", "ffn_w_up", "ffn_w_down", "final_norm"]
    small_names = ["mix_norm", "ffn_norm", "a_v_gain", "a_v_bias", "a_w_s", "a_b_s", "final_norm"]
    grads = {"b_conv_w": gr_conv}
    deltas, new_m, new_v = {}, {}, {}
    for k, name in enumerate(small_names):
        grads[name] = small_g[k]
        deltas[name], new_m[name], new_v[name] = sm_delta[k], sm_m[k], sm_v[k]
    deltas["b_conv_w"], new_m["b_conv_w"], new_v["b_conv_w"] = conv_delta, conv_m, conv_v
    for name, (gg, dl, mm, vv) in res.items():
        grads[name], deltas[name], new_m[name], new_v[name] = gg, dl, mm, vv

    grad_x = dx0.reshape(bsz, seq, d)
    return (loss, grad_x, *[grads[n] for n in order], *[deltas[n] for n in order],
            *[new_m[n] for n in order], *[new_v[n] for n in order])
```

```python
import math

import jax
import jax.numpy as jnp
from jax import lax
from jax.experimental import pallas as pl
from jax.experimental.pallas import tpu as pltpu

F32 = jnp.float32
BF16 = jnp.bfloat16

N_DEV = 8
N_CHIP = 4
CHUNK = 128
HEADS = 16
GROUP = 128
CONV_W = 3
NORM_EPS = 1e-6
GELU_C = math.sqrt(2.0 / math.pi)
GELU_K = 0.044715

ADAM_LR = 0.001
ADAM_B1 = 0.9
ADAM_B2 = 0.999
ADAM_EPS = 1e-08
ADAM_WD = 0.01
ADAM_STEP = 10

LANES = 128
SUBLANES = 8
VMEM_LIMIT = 60 * 1024 * 1024
HALO = 16
MXU_WIDTH = 256
FFN_CHUNKS = 2
TOKEN_TILE = 256
TOKEN_TILE_WIDE = 512
WGRAD_ROWS = 256
WGRAD_ROWS_WIDE = 512
GATHER_RELAY_AT = 0.56
GATHER_FORWARD_LEAD = 2

MESH = pl.DeviceIdType.MESH
ANY = pl.BlockSpec(memory_space=pl.ANY)

PEER_FLIPS = {"sibling": (0, 0, 1), "x": (1, 0, 0), "y": (0, 1, 0), "diagonal": (1, 1, 0),
              "x_other": (1, 0, 1), "y_other": (0, 1, 1), "diagonal_other": (1, 1, 1)}
COLLECTIVE_IDS = {frozenset(["sibling"]): 0,
                  frozenset(["sibling", "x", "y"]): 1,
                  frozenset(["sibling", "x", "y", "diagonal"]): 2,
                  frozenset(["x", "y", "diagonal"]): 3,
                  frozenset(PEER_FLIPS): 4}


def _params(sequential=True):
    return pltpu.CompilerParams(
        dimension_semantics=("arbitrary",) if sequential else None,
        vmem_limit_bytes=VMEM_LIMIT)


def _nn(a, b):
    return jnp.dot(a, b, preferred_element_type=F32)


def _nt(a, b):
    return lax.dot_general(a, b, (((1,), (1,)), ((), ())), preferred_element_type=F32)


def _tn(a, b):
    return lax.dot_general(a, b, (((0,), (0,)), ((), ())), preferred_element_type=F32)


def _row_mean(a):
    return jnp.mean(a, axis=-1, keepdims=True)


def _col_sum(a):
    return jnp.sum(a, axis=0, keepdims=True)


def _rms_fwd(x, g):
    r = lax.rsqrt(_row_mean(x * x) + NORM_EPS)
    xhat = x * r
    return xhat * g, xhat, r


def _rms_bwd(dh, xhat, r, g):
    a = dh * g
    dx = r * (a - xhat * _row_mean(a * xhat))
    return dx, _col_sum(dh * xhat)


def _gelu_and_grad(x):
    x2 = x * x
    t = jnp.tanh(x * (GELU_C + (GELU_C * GELU_K) * x2))
    half = 0.5 * t + 0.5
    d = half + x * (0.5 - 0.5 * (t * t)) * (GELU_C + (3.0 * GELU_C * GELU_K) * x2)
    return x * half, d


def _sigmoid(x):
    return 1.0 / (1.0 + jnp.exp(-x))


def _row_spec(tm, width):
    return pl.BlockSpec((tm, width), lambda i: (i, 0))


def _const_spec(shape):
    nd = len(shape)
    return pl.BlockSpec(shape, lambda i: (0,) * nd)


def _load_group(parts, sems):
    @pl.when(pl.program_id(0) == 0)
    def _():
        copies = []
        for k, (gath_ref, first, n, dst) in enumerate(parts):
            for j in range(N_DEV):
                copies.append(pltpu.make_async_copy(gath_ref.at[j, pl.ds(first, n), :], dst.at[pl.ds(j * n, n), :],
                                                    sems.at[k * N_DEV + j]))
        for cp in copies:
            cp.start()
        for cp in copies:
            cp.wait()


def _hosting_call(body, name, n_steps, arrays, in_specs, out_specs, out_shape, scratch, hosted=()):
    n_in, n_out, n_scr = len(arrays), len(out_shape), len(scratch)
    h_arrays = [a for h in hosted for a in h.arrays]
    h_shapes = [s for h in hosted for s in h.out_shapes]
    h_sems = [s for h in hosted for s in h.sem_shapes]
    peers = sorted(set().union(*[h.peers for h in hosted])) if hosted else []

    def handshake():
        @pl.when(pl.program_id(0) == 0)
        def _():
            x, y, c = lax.axis_index("x"), lax.axis_index("y"), lax.axis_index("c")
            barrier = pltpu.get_barrier_semaphore()
            for p in peers:
                fx, fy, fc = PEER_FLIPS[p]
                peer = (1 - x if fx else x, 1 - y if fy else y, 1 - c if fc else c)
                pl.semaphore_signal(barrier, inc=1, device_id=peer, device_id_type=MESH)
            pl.semaphore_wait(barrier, len(peers))

    def full_body(*refs):
        pos = 0
        groups = []
        for n in (n_in, len(h_arrays), n_out, len(h_shapes), n_scr, len(h_sems)):
            groups.append(refs[pos:pos + n])
            pos += n
        own_in, h_in, own_out, h_out, own_scr, h_sem = groups
        per_host = []
        pi = po = ps = 0
        for h in hosted:
            ni, no, ns = len(h.arrays), len(h.out_shapes), len(h.sem_shapes)
            per_host.append((h, h_in[pi:pi + ni], h_out[po:po + no], h_sem[ps:ps + ns]))
            pi, po, ps = pi + ni, po + no, ps + ns
        if hosted:
            handshake()
        for h, ins, outs, sems in per_host:
            h.begin(ins, outs, sems, n_steps)
        body(*own_in, *own_out, *own_scr)
        for h, ins, outs, sems in per_host:
            h.end(ins, outs, sems, n_steps)

    outs = pl.pallas_call(
        full_body, name=name, grid=(n_steps,),
        in_specs=list(in_specs) + [ANY] * len(h_arrays),
        out_specs=list(out_specs) + [ANY] * len(h_shapes),
        out_shape=list(out_shape) + h_shapes,
        scratch_shapes=list(scratch) + h_sems,
        compiler_params=pltpu.CompilerParams(
            dimension_semantics=("arbitrary",), vmem_limit_bytes=VMEM_LIMIT,
            collective_id=COLLECTIVE_IDS[frozenset(peers)] if hosted else None),
    )(*arrays, *h_arrays)
    return outs[:n_out], outs[n_out:]


def _my_index():
    return 4 * lax.axis_index("x") + 2 * lax.axis_index("y") + lax.axis_index("c")


GATHER_COPIES = 8


def _gather_relays(n_rows, dtype):
    rows_per_tile = SUBLANES * 4 // jnp.dtype(dtype).itemsize
    return n_rows % 2 == 0 and (n_rows // 2) % rows_per_tile == 0


class _Gather:
    def __init__(self, shard, out, send_sems, recv_sems, local_sem):
        self.shard, self.out = shard, out
        self.send_sems, self.recv_sems, self.local_sem = send_sems, recv_sems, local_sem
        x, y, c = lax.axis_index("x"), lax.axis_index("y"), lax.axis_index("c")
        self.c = c
        self.me, self.sibling = (x, y, c), (x, y, 1 - c)
        self.xn, self.yn, self.dg = (1 - x, y), (x, 1 - y), (1 - x, 1 - y)
        self.n = shard.shape[0]
        self.half = self.n // 2
        self.relays = _gather_relays(self.n, shard.dtype)

    def _slot(self, dev, lo=0, hi=None):
        hi = self.n if hi is None else hi
        return self.out.at[4 * dev[0] + 2 * dev[1] + dev[2], pl.ds(lo, hi - lo), :]

    def _copy(self, k, block, to, src=None, lo=0, hi=None):
        return pltpu.make_async_remote_copy(
            src_ref=self._slot(block, lo, hi) if src is None else src, dst_ref=self._slot(block, lo, hi),
            send_sem=self.send_sems.at[k], recv_sem=self.recv_sems.at[k], device_id=to, device_id_type=MESH)

    def _local(self):
        return pltpu.make_async_copy(self.shard, self._slot(self.me), self.local_sem)

    def start(self):
        c = self.c
        self._local().start()
        self._copy(0, self.me, self.sibling, src=self.shard).start()
        self._copy(1, self.me, (*self.xn, c), src=self.shard).start()
        self._copy(2, self.me, (*self.yn, c), src=self.shard).start()
        if not self.relays:
            self._copy(3, self.me, (*self.dg, c), src=self.shard).start()

    def relay(self):
        c = self.c
        if self.relays:
            self._copy(1, (*self.xn, c), self.me).wait_recv()
            self._copy(3, (*self.xn, c), (*self.yn, c), hi=self.half).start()
            self._copy(2, (*self.yn, c), self.me).wait_recv()
            self._copy(4, (*self.yn, c), (*self.xn, c), lo=self.half).start()

    def forward(self):
        c = self.c
        if self.relays:
            self._copy(5, (*self.xn, c), self.sibling).start()
            self._copy(6, (*self.yn, c), self.sibling).start()
            self._copy(3, (*self.dg, c), self.me, hi=self.half).wait_recv()
            self._copy(4, (*self.dg, c), self.me, lo=self.half).wait_recv()
        else:
            self._copy(1, (*self.xn, c), self.me).wait_recv()
            self._copy(5, (*self.xn, c), self.sibling).start()
            self._copy(2, (*self.yn, c), self.me).wait_recv()
            self._copy(6, (*self.yn, c), self.sibling).start()
            self._copy(3, (*self.dg, c), self.me).wait_recv()
        self._copy(7, (*self.dg, c), self.sibling).start()

    def finish(self):
        c = self.c
        self._copy(0, self.sibling, self.me).wait_recv()
        for k, chip in ((5, self.xn), (6, self.yn), (7, self.dg)):
            self._copy(k, (*chip, 1 - c), self.me).wait_recv()
        for k in (0, 1, 2, 5, 6, 7):
            self._copy(k, self.me, self.sibling).wait_send()
        if self.relays:
            self._copy(3, self.me, self.sibling, hi=self.half).wait_send()
            self._copy(4, self.me, self.sibling, lo=self.half).wait_send()
        else:
            self._copy(3, self.me, self.sibling).wait_send()
        self._local().wait()


class _HostedGathers:
    def __init__(self, shards, mid_lead=GATHER_FORWARD_LEAD, relay_at=GATHER_RELAY_AT):
        n = len(shards)
        self.arrays = shards
        self.mid_lead, self.relay_at = mid_lead, relay_at
        self.peers = {"sibling", "x", "y"}
        if not all(_gather_relays(s.shape[0], s.dtype) for s in shards):
            self.peers.add("diagonal")
        self.out_shapes = [jax.ShapeDtypeStruct((N_DEV,) + s.shape, s.dtype) for s in shards]
        self.sem_shapes = [pltpu.SemaphoreType.DMA((n, GATHER_COPIES)), pltpu.SemaphoreType.DMA((n, GATHER_COPIES)),
                           pltpu.SemaphoreType.DMA((n,))]

    def _gathers(self, ins, outs, sems):
        return [_Gather(ins[a], outs[a], sems[0].at[a], sems[1].at[a], sems[2].at[a]) for a in range(len(ins))]

    def begin(self, ins, outs, sems, n_steps):
        i = pl.program_id(0)
        forward_step = max(n_steps - 1 - self.mid_lead, 0)
        relay_step = min(int(self.relay_at * n_steps), forward_step)

        @pl.when(i == 0)
        def _():
            for g in self._gathers(ins, outs, sems):
                g.start()

        if n_steps == 1:
            return

        @pl.when(i == relay_step)
        def _():
            for g in self._gathers(ins, outs, sems):
                g.relay()

        @pl.when(i == forward_step)
        def _():
            for g in self._gathers(ins, outs, sems):
                g.forward()

    def end(self, ins, outs, sems, n_steps):
        @pl.when(pl.program_id(0) == n_steps - 1)
        def _():
            gathers = self._gathers(ins, outs, sems)
            if n_steps == 1:
                for g in gathers:
                    g.relay()
                for g in gathers:
                    g.forward()
            for g in gathers:
                g.finish()


def _exchange(hosted, name):
    return _hosting_call(lambda: None, name, 1, [], [], [], [], [], hosted=hosted)[1]


class _ChipScatter:
    def __init__(self, pairsum, row0, land, send_sems, recv_sems, local_sem):
        self.pairsum, self.row0, self.land = pairsum, row0, land
        self.send_sems, self.recv_sems, self.local_sem = send_sems, recv_sems, local_sem
        x, y, c = lax.axis_index("x"), lax.axis_index("y"), lax.axis_index("c")
        self.c = c
        self.chip = 2 * x + y
        self.others = [(1 - x, y), (x, 1 - y), (1 - x, 1 - y)]

    def _src(self, chip):
        return self.pairsum.at[chip, pl.ds(self.row0, self.land.shape[1]), :]

    def _copy(self, k):
        ox, oy = self.others[k]
        return pltpu.make_async_remote_copy(
            src_ref=self._src(2 * ox + oy), dst_ref=self.land.at[self.chip],
            send_sem=self.send_sems.at[k], recv_sem=self.recv_sems.at[k], device_id=(ox, oy, self.c),
            device_id_type=MESH)

    def _arrival(self, k):
        ox, oy = self.others[k]
        return pltpu.make_async_remote_copy(
            src_ref=self._src(self.chip), dst_ref=self.land.at[2 * ox + oy],
            send_sem=self.send_sems.at[k], recv_sem=self.recv_sems.at[k], device_id=(ox, oy, self.c),
            device_id_type=MESH)

    def _local(self):
        return pltpu.make_async_copy(self._src(self.chip), self.land.at[self.chip], self.local_sem)

    def start(self):
        self._local().start()
        for k in range(N_CHIP - 1):
            self._copy(k).start()

    def finish(self):
        for k in range(N_CHIP - 1):
            self._arrival(k).wait_recv()
        for k in range(N_CHIP - 1):
            self._copy(k).wait_send()
        self._local().wait()


class _HostedChipScatter:
    def __init__(self, pairsum, row0=0, n=None):
        n = pairsum.shape[1] - row0 if n is None else n
        self.row0 = row0
        self.peers = {"x", "y", "diagonal"}
        self.arrays = [pairsum]
        self.out_shapes = [jax.ShapeDtypeStruct((N_CHIP, n, pairsum.shape[2]), pairsum.dtype)]
        self.sem_shapes = [pltpu.SemaphoreType.DMA((N_CHIP - 1,)), pltpu.SemaphoreType.DMA((N_CHIP - 1,)),
                           pltpu.SemaphoreType.DMA(())]

    def begin(self, ins, outs, sems, n_steps):
        @pl.when(pl.program_id(0) == 0)
        def _():
            _ChipScatter(ins[0], self.row0, outs[0], *sems).start()

    def end(self, ins, outs, sems, n_steps):
        @pl.when(pl.program_id(0) == n_steps - 1)
        def _():
            _ChipScatter(ins[0], self.row0, outs[0], *sems).finish()


def _pair_reduce(arrays, entries, name):
    n_arr, n_ent = len(arrays), len(entries)
    cols = arrays[0].shape[1]
    offsets = []
    total = 0
    for _, _, n in entries:
        offsets.append(total)
        total += n

    def body(*refs):
        ins, out_ref = refs[:n_arr], refs[n_arr]
        rbuf, own, send_sems, recv_sems, own_sems = refs[n_arr + 1:]
        q = pl.program_id(0)
        x, y, c = lax.axis_index("x"), lax.axis_index("y"), lax.axis_index("c")

        def block(e, chip, core):
            ai, first, n = entries[e]
            return ins[ai].at[pl.ds(first + (2 * chip + core) * n, n), :]

        def to_sibling(e, chip):
            return pltpu.make_async_remote_copy(
                src_ref=block(e, chip, 1 - c), dst_ref=rbuf.at[chip, pl.ds(offsets[e], entries[e][2]), :],
                send_sem=send_sems.at[e, chip], recv_sem=recv_sems.at[e, chip], device_id=(x, y, 1 - c),
                device_id_type=MESH)

        @pl.when(q == 0)
        def _():
            barrier = pltpu.get_barrier_semaphore()
            pl.semaphore_signal(barrier, inc=1, device_id=(x, y, 1 - c), device_id_type=MESH)
            pl.semaphore_wait(barrier, 1)
            for chip in range(N_CHIP):
                for e in range(n_ent):
                    to_sibling(e, chip).start()

        loads = [pltpu.make_async_copy(block(e, q, c), own.at[pl.ds(offsets[e], entries[e][2]), :], own_sems.at[e])
                 for e in range(n_ent)]
        for cp in loads:
            cp.start()
        for cp in loads:
            cp.wait()
        for e in range(n_ent):
            to_sibling(e, q).wait_recv()
        out_ref[...] = (own[...].astype(F32) + rbuf[q].astype(F32)).astype(out_ref.dtype)

        @pl.when(q == N_CHIP - 1)
        def _():
            for chip in range(N_CHIP):
                for e in range(n_ent):
                    to_sibling(e, chip).wait_send()

    return pl.pallas_call(
        body, name=name, grid=(N_CHIP,),
        in_specs=[ANY] * n_arr,
        out_specs=pl.BlockSpec((None, total, cols), lambda q: (q, 0, 0)),
        out_shape=jax.ShapeDtypeStruct((N_CHIP, total, cols), BF16),
        scratch_shapes=[pltpu.VMEM((N_CHIP, total, cols), BF16), pltpu.VMEM((total, cols), BF16),
                        pltpu.SemaphoreType.DMA((n_ent, N_CHIP)), pltpu.SemaphoreType.DMA((n_ent, N_CHIP)),
                        pltpu.SemaphoreType.DMA((n_ent,))],
        compiler_params=pltpu.CompilerParams(dimension_semantics=("arbitrary",), vmem_limit_bytes=VMEM_LIMIT,
                                             collective_id=COLLECTIVE_IDS[frozenset(["sibling"])]),
    )(*arrays)


class _HostedScatterAll:
    def __init__(self, packed):
        n = packed.shape[0] // N_DEV
        self.n = n
        self.peers = set(PEER_FLIPS)
        self.arrays = [packed]
        self.out_shapes = [jax.ShapeDtypeStruct((N_DEV, n, packed.shape[1]), packed.dtype)]
        self.sem_shapes = [pltpu.SemaphoreType.DMA((N_DEV - 1,)), pltpu.SemaphoreType.DMA((N_DEV - 1,)),
                           pltpu.SemaphoreType.DMA(())]

    def _copies(self, ins, outs, sems, with_arrivals):
        src, land = ins[0], outs[0]
        send_sems, recv_sems, local_sem = sems
        me = _my_index()

        def block(p):
            return src.at[pl.ds(p * self.n, self.n), :]

        local = pltpu.make_async_copy(block(me), land.at[me], local_sem)
        sends, arrivals = [], []
        for k in range(1, N_DEV):
            p = (me + k) % N_DEV
            q = (me + N_DEV - k) % N_DEV
            sends.append(pltpu.make_async_remote_copy(
                src_ref=block(p), dst_ref=land.at[me], send_sem=send_sems.at[k - 1], recv_sem=recv_sems.at[k - 1],
                device_id=(p // 4, (p // 2) % 2, p % 2), device_id_type=MESH))
            if with_arrivals:
                arrivals.append(pltpu.make_async_remote_copy(
                    src_ref=block(me), dst_ref=land.at[q], send_sem=send_sems.at[k - 1], recv_sem=recv_sems.at[k - 1],
                    device_id=(q // 4, (q // 2) % 2, q % 2), device_id_type=MESH))
        return local, sends, arrivals

    def begin(self, ins, outs, sems, n_steps):
        @pl.when(pl.program_id(0) == 0)
        def _():
            local, sends, _ = self._copies(ins, outs, sems, with_arrivals=False)
            local.start()
            for cp in sends:
                cp.start()

    def end(self, ins, outs, sems, n_steps):
        @pl.when(pl.program_id(0) == n_steps - 1)
        def _():
            local, sends, arrivals = self._copies(ins, outs, sems, with_arrivals=True)
            for cp in arrivals:
                cp.wait_recv()
            for cp in sends:
                cp.wait_send()
            local.wait()


def _tril_weights(ws_ref):
    r = lax.broadcasted_iota(jnp.int32, (CHUNK, CHUNK), 0)
    c = lax.broadcasted_iota(jnp.int32, (CHUNK, CHUNK), 1)
    return [jnp.where(r >= c, ws_ref[h], 0.0).astype(BF16) for h in range(HEADS)]


def _sgu_stats(zpre, gain, bias):
    e = zpre.shape[1] // 2
    z, dz = _gelu_and_grad(zpre)
    u, v = z[:, :e], z[:, e:]
    vc = v - _row_mean(v)
    rstd = lax.rsqrt(_row_mean(vc * vc) + NORM_EPS)
    vhat = vc * rstd
    return u, vhat, rstd, vhat * gain + bias, dz


def _spatial_fwd(wt, vn_bf, bfull_ref, sv_ref, tm):
    for ci in range(tm // CHUNK):
        rows = slice(ci * CHUNK, (ci + 1) * CHUNK)
        for h in range(HEADS):
            cols = slice(h * GROUP, (h + 1) * GROUP)
            sv_ref[rows, cols] = _nn(wt[h], vn_bf[rows, cols]) + bfull_ref[:, cols]


def _mixer_a_fwd(x, g, gath, gain, bias, ws, bfull, tm, hosted=()):
    t_tok, d = x.shape
    e = gain.shape[1]
    e2 = 2 * e
    n_in, n_out = e2 // N_DEV, e // N_DEV

    def body(x_ref, g_ref, gain_ref, bias_ref, ws_ref, bfull_ref, gath_ref,
             xo_ref, gd_ref, u_ref, vhat_ref, svo_ref, y_ref, rstd_ref, win_v, wout_v, sv_v, sems):
        _load_group([(gath_ref, 0, n_in, win_v), (gath_ref, n_in, n_out, wout_v)], sems)
        xv = x_ref[...]
        h = _rms_fwd(xv, g_ref[...])[0].astype(BF16)
        zpre = _nt(h, win_v[...])
        u, vhat, rstd, vn, gelu_d = _sgu_stats(zpre, gain_ref[...], bias_ref[...])
        gd_ref[...] = gelu_d.astype(BF16)
        u_ref[...] = u.astype(BF16)
        vhat_ref[...] = vhat.astype(BF16)
        rstd_ref[...] = rstd
        _spatial_fwd(_tril_weights(ws_ref), vn.astype(BF16), bfull_ref, sv_v, tm)
        sv = sv_v[...]
        svo_ref[...] = sv.astype(BF16)
        y = (u * sv).astype(BF16)
        y_ref[...] = y
        xo_ref[...] = xv + _nn(y, wout_v[...])

    return _hosting_call(
        body, "mixer_a_fwd", t_tok // tm, [x, g, gain, bias, ws, bfull, gath],
        in_specs=[_row_spec(tm, d), _const_spec((1, d)), _const_spec((1, e)), _const_spec((1, e)),
                  _const_spec((HEADS, CHUNK, CHUNK)), _const_spec((CHUNK, e)), ANY],
        out_specs=[_row_spec(tm, d), _row_spec(tm, e2), _row_spec(tm, e), _row_spec(tm, e), _row_spec(tm, e),
                   _row_spec(tm, e), _row_spec(tm, 1)],
        out_shape=[jax.ShapeDtypeStruct((t_tok, d), F32), jax.ShapeDtypeStruct((t_tok, e2), BF16),
                   jax.ShapeDtypeStruct((t_tok, e), BF16), jax.ShapeDtypeStruct((t_tok, e), BF16),
                   jax.ShapeDtypeStruct((t_tok, e), BF16), jax.ShapeDtypeStruct((t_tok, e), BF16),
                   jax.ShapeDtypeStruct((t_tok, 1), F32)],
        scratch=[pltpu.VMEM((e2, d), BF16), pltpu.VMEM((e, d), BF16), pltpu.VMEM((tm, e), F32),
                 pltpu.SemaphoreType.DMA((2 * N_DEV,))],
        hosted=hosted)


def _mixer_a_bwd(dout, x, gd, u_sav, vhat_sav, sv_sav, rstd_sav, g, gath, gain, bias, ws, tm, hosted=()):
    t_tok, d = x.shape
    e = gain.shape[1]
    e2 = 2 * e
    n_in, n_out = e2 // N_DEV, e // N_DEV
    n_steps = t_tok // tm

    def body(dout_ref, x_ref, gd_ref, u_ref, vhat_ref, sv_ref, rstd_ref, g_ref, gain_ref, bias_ref, ws_ref, gath_ref,
             dx_ref, dxb_ref, h_ref, dz_ref, dg_ref, dgain_ref, dbias_ref, dws_ref, dbso_ref,
             win_v, wout_v, dvn_v, dbs_ref, sems):
        i = pl.program_id(0)
        _load_group([(gath_ref, 0, n_in, win_v), (gath_ref, n_in, n_out, wout_v)], sems)

        @pl.when(i == 0)
        def _():
            dg_ref[...] = jnp.zeros_like(dg_ref)
            dgain_ref[...] = jnp.zeros_like(dgain_ref)
            dbias_ref[...] = jnp.zeros_like(dbias_ref)
            dws_ref[...] = jnp.zeros_like(dws_ref)
            dbs_ref[...] = jnp.zeros_like(dbs_ref)

        xv = x_ref[...]
        gv = g_ref[...]
        hv, xhat, r = _rms_fwd(xv, gv)
        h_ref[...] = hv.astype(BF16)
        gain_v = gain_ref[...]
        vhat = vhat_ref[...].astype(F32)
        vn_bf = (vhat * gain_v + bias_ref[...]).astype(BF16)
        wt = _tril_weights(ws_ref)

        dov = dout_ref[...]
        dy = _nt(dov.astype(BF16), wout_v[...])
        du = dy * sv_ref[...].astype(F32)
        dsv = dy * u_ref[...].astype(F32)
        dsv_bf = dsv.astype(BF16)
        for ci in range(tm // CHUNK):
            rows = slice(ci * CHUNK, (ci + 1) * CHUNK)
            dbs_ref[...] += dsv[rows, :]
            for h in range(HEADS):
                cols = slice(h * GROUP, (h + 1) * GROUP)
                dvn_v[rows, cols] = _tn(wt[h], dsv_bf[rows, cols])
                dws_ref[h] += _nt(dsv_bf[rows, cols], vn_bf[rows, cols])
        dvn = dvn_v[...]
        dgain_ref[...] += _col_sum(dvn * vhat)
        dbias_ref[...] += _col_sum(dvn)
        dvhat = dvn * gain_v
        dv = rstd_ref[...] * (dvhat - _row_mean(dvhat) - vhat * _row_mean(dvhat * vhat))
        dzpre = (jnp.concatenate([du, dv], axis=1) * gd_ref[...].astype(F32)).astype(BF16)
        dz_ref[...] = dzpre
        dh = _nn(dzpre, win_v[...])
        dxr, dg_row = _rms_bwd(dh, xhat, r, gv)
        dg_ref[...] += dg_row
        dx = dov + dxr
        dx_ref[...] = dx
        dxb_ref[...] = dx.astype(BF16)

        @pl.when(i == n_steps - 1)
        def _():
            rr = lax.broadcasted_iota(jnp.int32, (CHUNK, CHUNK), 0)
            cc = lax.broadcasted_iota(jnp.int32, (CHUNK, CHUNK), 1)
            for h in range(HEADS):
                dws_ref[h] = jnp.where(rr >= cc, dws_ref[h], 0.0)
                dbso_ref[h] = jnp.sum(dbs_ref[:, h * GROUP:(h + 1) * GROUP], axis=1, keepdims=True)

    return _hosting_call(
        body, "mixer_a_bwd", n_steps, [dout, x, gd, u_sav, vhat_sav, sv_sav, rstd_sav, g, gain, bias, ws, gath],
        in_specs=[_row_spec(tm, d), _row_spec(tm, d), _row_spec(tm, e2), _row_spec(tm, e), _row_spec(tm, e),
                  _row_spec(tm, e), _row_spec(tm, 1), _const_spec((1, d)),
                  _const_spec((1, e)), _const_spec((1, e)), _const_spec((HEADS, CHUNK, CHUNK)), ANY],
        out_specs=[_row_spec(tm, d), _row_spec(tm, d), _row_spec(tm, d), _row_spec(tm, e2),
                   _const_spec((1, d)), _const_spec((1, e)), _const_spec((1, e)),
                   _const_spec((HEADS, CHUNK, CHUNK)), _const_spec((HEADS, CHUNK, 1))],
        out_shape=[jax.ShapeDtypeStruct((t_tok, d), F32), jax.ShapeDtypeStruct((t_tok, d), BF16),
                   jax.ShapeDtypeStruct((t_tok, d), BF16), jax.ShapeDtypeStruct((t_tok, e2), BF16),
                   jax.ShapeDtypeStruct((1, d), F32), jax.ShapeDtypeStruct((1, e), F32),
                   jax.ShapeDtypeStruct((1, e), F32), jax.ShapeDtypeStruct((HEADS, CHUNK, CHUNK), F32),
                   jax.ShapeDtypeStruct((HEADS, CHUNK, 1), F32)],
        scratch=[pltpu.VMEM((e2, d), BF16), pltpu.VMEM((e, d), BF16), pltpu.VMEM((tm, e), F32),
                 pltpu.VMEM((CHUNK, e), F32), pltpu.SemaphoreType.DMA((2 * N_DEV,))],
        hosted=hosted)


def _ffn_fwd(x, g, srcs, nf, tm, name, hosted=(), head=None):
    t_tok, d = x.shape
    f = nf * N_DEV
    firsts = [first for _, first in srcs]
    n_head = 2 if head else 0

    def body(*refs):
        x_ref, g_ref, sg_ref, su_ref, sd_ref = refs[:5]
        gate_ref, up_ref, wg_v, wu_v, wd_v, sems = refs[-6:]
        _load_group(
            [(sg_ref, firsts[0], nf, wg_v), (su_ref, firsts[1], nf, wu_v), (sd_ref, firsts[2], nf, wd_v)], sems)
        if head:
            t_ref, gf_ref, loss_ref, dx_ref, dxb_ref, dgf_ref = refs[5:11]

            @pl.when(pl.program_id(0) == 0)
            def _():
                loss_ref[...] = jnp.zeros_like(loss_ref)
                dgf_ref[...] = jnp.zeros_like(dgf_ref)

        xv = x_ref[...]
        h = _rms_fwd(xv, g_ref[...])[0].astype(BF16)
        gate = _nt(h, wg_v[...])
        up = _nt(h, wu_v[...])
        gate_ref[...] = gate.astype(BF16)
        up_ref[...] = up.astype(BF16)
        act = (gate * _sigmoid(gate) * up).astype(BF16)
        xo = xv + _nn(act, wd_v[...])
        if head:
            gfv = gf_ref[...]
            y, xhat, r = _rms_fwd(xo, gfv)
            err = y - t_ref[...]
            loss_ref[...] += 0.5 * jnp.sum(_row_mean(err * err), axis=0, keepdims=True)
            dxr, dg_row = _rms_bwd(err * (1.0 / d), xhat, r, gfv)
            dgf_ref[...] += dg_row
            dx_ref[...] = dxr
            dxb_ref[...] = dxr.astype(BF16)
        else:
            refs[5][...] = xo

    act_specs = [_row_spec(tm, f), _row_spec(tm, f)]
    act_shapes = [jax.ShapeDtypeStruct((t_tok, f), BF16), jax.ShapeDtypeStruct((t_tok, f), BF16)]
    if head:
        out_specs = [_const_spec((1, 1)), _row_spec(tm, d), _row_spec(tm, d), _const_spec((1, d))]
        out_shape = [jax.ShapeDtypeStruct((1, 1), F32), jax.ShapeDtypeStruct((t_tok, d), F32),
                     jax.ShapeDtypeStruct((t_tok, d), BF16), jax.ShapeDtypeStruct((1, d), F32)]
    else:
        out_specs = [_row_spec(tm, d)]
        out_shape = [jax.ShapeDtypeStruct((t_tok, d), F32)]
    return _hosting_call(
        body, name, t_tok // tm, [x, g] + [arr for arr, _ in srcs] + list(head or ()),
        in_specs=[_row_spec(tm, d), _const_spec((1, d)), ANY, ANY, ANY] + [_row_spec(tm, d), _const_spec((1, d))][:n_head],
        out_specs=out_specs + act_specs, out_shape=out_shape + act_shapes,
        scratch=[pltpu.VMEM((f, d), BF16), pltpu.VMEM((f, d), BF16), pltpu.VMEM((f, d), BF16),
                 pltpu.SemaphoreType.DMA((3 * N_DEV,))],
        hosted=hosted)


def _ffn_bwd(dout, x, gate, up, g, srcs, nf, tm, name, hosted=()):
    t_tok, d = x.shape
    f = nf * N_DEV
    firsts = [first for _, first in srcs]
    per_chunk = -(-f // (FFN_CHUNKS * MXU_WIDTH)) * MXU_WIDTH
    bounds = [min(ck * per_chunk, f) for ck in range(FFN_CHUNKS + 1)]

    def body(dout_ref, x_ref, gate_ref, up_ref, g_ref, sg_ref, su_ref, sd_ref,
             dx_ref, dxb_ref, h_ref, act_ref, dgu_ref, dg_ref, wg_v, wu_v, wd_v, sems):
        _load_group(
            [(sg_ref, firsts[0], nf, wg_v), (su_ref, firsts[1], nf, wu_v), (sd_ref, firsts[2], nf, wd_v)], sems)

        @pl.when(pl.program_id(0) == 0)
        def _():
            dg_ref[...] = jnp.zeros_like(dg_ref)

        xv = x_ref[...]
        gv = g_ref[...]
        hv, xhat, r = _rms_fwd(xv, gv)
        h_ref[...] = hv.astype(BF16)
        dov = dout_ref[...]
        dob = dov.astype(BF16)
        dh = None
        for ck in range(FFN_CHUNKS):
            cols = slice(bounds[ck], bounds[ck + 1])
            gate_v = gate_ref[:, cols].astype(F32)
            up_v = up_ref[:, cols].astype(F32)
            sig = _sigmoid(gate_v)
            silu = gate_v * sig
            act_ref[:, cols] = (silu * up_v).astype(BF16)
            dact = _nt(dob, wd_v[cols, :])
            dup = (dact * silu).astype(BF16)
            dgate = (dact * up_v * (sig * (1.0 + gate_v * (1.0 - sig)))).astype(BF16)
            dgu_ref[:, cols] = dgate
            dgu_ref[:, f + bounds[ck]:f + bounds[ck + 1]] = dup
            part = _nn(dgate, wg_v[cols, :]) + _nn(dup, wu_v[cols, :])
            dh = part if dh is None else dh + part
        dxr, dg_row = _rms_bwd(dh, xhat, r, gv)
        dg_ref[...] += dg_row
        dx = dov + dxr
        dx_ref[...] = dx
        dxb_ref[...] = dx.astype(BF16)

    return _hosting_call(
        body, name, t_tok // tm, [dout, x, gate, up, g] + [arr for arr, _ in srcs],
        in_specs=[_row_spec(tm, d), _row_spec(tm, d), _row_spec(tm, f), _row_spec(tm, f), _const_spec((1, d)),
                  ANY, ANY, ANY],
        out_specs=[_row_spec(tm, d), _row_spec(tm, d), _row_spec(tm, d), _row_spec(tm, f), _row_spec(tm, 2 * f),
                   _const_spec((1, d))],
        out_shape=[jax.ShapeDtypeStruct((t_tok, d), F32), jax.ShapeDtypeStruct((t_tok, d), BF16),
                   jax.ShapeDtypeStruct((t_tok, d), BF16), jax.ShapeDtypeStruct((t_tok, f), BF16),
                   jax.ShapeDtypeStruct((t_tok, 2 * f), BF16), jax.ShapeDtypeStruct((1, d), F32)],
        scratch=[pltpu.VMEM((f, d), BF16), pltpu.VMEM((f, d), BF16), pltpu.VMEM((f, d), BF16),
                 pltpu.SemaphoreType.DMA((3 * N_DEV,))],
        hosted=hosted)


def _shift_down(z, k, prev_rows):
    row = lax.broadcasted_iota(jnp.int32, z.shape, 0)
    out = pltpu.roll(z, k, 0)
    for j in range(k):
        out = jnp.where(row == j, prev_rows[j], out)
    return out


def _shift_up(z, k, next_rows):
    tm = z.shape[0]
    row = lax.broadcasted_iota(jnp.int32, z.shape, 0)
    out = pltpu.roll(z, tm - k, 0)
    for j in range(k):
        out = jnp.where(row == tm - k + j, next_rows[j], out)
    return out


def _mixer_b_fwd(x, g, gath, conv_w, tm, seq, hosted=()):
    t_tok, d = x.shape
    e = conv_w.shape[1]
    e3 = 3 * e
    n_in, n_out = e3 // N_DEV, e // N_DEV
    tiles_per_seq = seq // tm

    def body(x_ref, g_ref, cw_ref, gath_ref, xo_ref, p_ref, win_v, wout_v, tail_v, sems):
        i = pl.program_id(0)
        _load_group([(gath_ref, 0, n_in, win_v), (gath_ref, n_in, n_out, wout_v)], sems)

        @pl.when(i % tiles_per_seq == 0)
        def _():
            tail_v[...] = jnp.zeros_like(tail_v)

        xv = x_ref[...]
        h = _rms_fwd(xv, g_ref[...])[0].astype(BF16)
        p = _nt(h, win_v[...])
        p_ref[...] = p.astype(BF16)
        z = p[:, e:2 * e] * p[:, 2 * e:]
        prev = [tail_v[SUBLANES - 2:SUBLANES - 1, :], tail_v[SUBLANES - 1:SUBLANES, :]]
        conv = (cw_ref[2:3, :] * z + cw_ref[1:2, :] * _shift_down(z, 1, prev[1:])
                + cw_ref[0:1, :] * _shift_down(z, 2, prev))
        tail_v[...] = z[tm - SUBLANES:, :]
        y = (p[:, :e] * conv).astype(BF16)
        xo_ref[...] = xv + _nn(y, wout_v[...])

    return _hosting_call(
        body, "mixer_b_fwd", t_tok // tm, [x, g, conv_w, gath],
        in_specs=[_row_spec(tm, d), _const_spec((1, d)), _const_spec((SUBLANES, e)), ANY],
        out_specs=[_row_spec(tm, d), _row_spec(tm, e3)],
        out_shape=[jax.ShapeDtypeStruct((t_tok, d), F32), jax.ShapeDtypeStruct((t_tok, e3), BF16)],
        scratch=[pltpu.VMEM((e3, d), BF16), pltpu.VMEM((e, d), BF16), pltpu.VMEM((SUBLANES, e), F32),
                 pltpu.SemaphoreType.DMA((2 * N_DEV,))],
        hosted=hosted)


def _mixer_b_bwd(dout, x, p, g, gath, conv_w, tm, seq, hosted=()):
    t_tok, d = x.shape
    e = conv_w.shape[1]
    e3 = 3 * e
    n_in, n_out = e3 // N_DEV, e // N_DEV
    tiles_per_seq = seq // tm
    halo_per_tile = tm // HALO
    n_halo = t_tok // HALO

    def body(dout_ref, dnext_ref, x_ref, p_ref, pprev_ref, pnext_ref, g_ref, cw_ref, gath_ref,
             dx_ref, dxb_ref, h_ref, y_ref, dp_ref, dg_ref, dcw_ref, win_v, wout_v, sems):
        i = pl.program_id(0)
        _load_group([(gath_ref, 0, n_in, win_v), (gath_ref, n_in, n_out, wout_v)], sems)

        @pl.when(i == 0)
        def _():
            dg_ref[...] = jnp.zeros_like(dg_ref)
            dcw_ref[...] = jnp.zeros_like(dcw_ref)

        first = (i % tiles_per_seq == 0).astype(F32)
        last = (i % tiles_per_seq == tiles_per_seq - 1).astype(F32)
        xv = x_ref[...]
        gv = g_ref[...]
        hv, xhat, r = _rms_fwd(xv, gv)
        h_ref[...] = hv.astype(BF16)
        pv = p_ref[...].astype(F32)
        bg, cg, hx = pv[:, :e], pv[:, e:2 * e], pv[:, 2 * e:]
        z = cg * hx
        pprev = pprev_ref[...].astype(F32)
        zprev = pprev[:, e:2 * e] * pprev[:, 2 * e:] * (1.0 - first)
        prev = [zprev[HALO - 2:HALO - 1, :], zprev[HALO - 1:HALO, :]]
        zs1 = _shift_down(z, 1, prev[1:])
        zs2 = _shift_down(z, 2, prev)
        w0, w1, w2 = cw_ref[0:1, :], cw_ref[1:2, :], cw_ref[2:3, :]
        conv = w2 * z + w1 * zs1 + w0 * zs2
        y_ref[...] = (bg * conv).astype(BF16)

        dov = dout_ref[...]
        wout_bf = wout_v[...]
        dy = _nt(dov.astype(BF16), wout_bf)
        dconv = dy * bg
        dnext = _nt(dnext_ref[...].astype(BF16), wout_bf) * pnext_ref[:, :e].astype(F32) * (1.0 - last)
        nxt = [dnext[0:1, :], dnext[1:2, :]]
        dz = w2 * dconv + w1 * _shift_up(dconv, 1, nxt[:1]) + w0 * _shift_up(dconv, 2, nxt)
        dcw_ref[0:1, :] += _col_sum(dconv * zs2)
        dcw_ref[1:2, :] += _col_sum(dconv * zs1)
        dcw_ref[2:3, :] += _col_sum(dconv * z)
        dp = jnp.concatenate([dy * conv, dz * hx, dz * cg], axis=1).astype(BF16)
        dp_ref[...] = dp
        dh = _nn(dp, win_v[...])
        dxr, dg_row = _rms_bwd(dh, xhat, r, gv)
        dg_ref[...] += dg_row
        dx = dov + dxr
        dx_ref[...] = dx
        dxb_ref[...] = dx.astype(BF16)

    prev_spec = lambda w: pl.BlockSpec((HALO, w), lambda i: (jnp.maximum(i * halo_per_tile - 1, 0), 0))
    next_spec = lambda w: pl.BlockSpec((HALO, w), lambda i: (jnp.minimum((i + 1) * halo_per_tile, n_halo - 1), 0))
    return _hosting_call(
        body, "mixer_b_bwd", t_tok // tm, [dout, dout, x, p, p, p, g, conv_w, gath],
        in_specs=[_row_spec(tm, d), next_spec(d), _row_spec(tm, d), _row_spec(tm, e3), prev_spec(e3), next_spec(e3),
                  _const_spec((1, d)), _const_spec((SUBLANES, e)), ANY],
        out_specs=[_row_spec(tm, d), _row_spec(tm, d), _row_spec(tm, d), _row_spec(tm, e), _row_spec(tm, e3),
                   _const_spec((1, d)), _const_spec((SUBLANES, e))],
        out_shape=[jax.ShapeDtypeStruct((t_tok, d), F32), jax.ShapeDtypeStruct((t_tok, d), BF16),
                   jax.ShapeDtypeStruct((t_tok, d), BF16), jax.ShapeDtypeStruct((t_tok, e), BF16),
                   jax.ShapeDtypeStruct((t_tok, e3), BF16), jax.ShapeDtypeStruct((1, d), F32),
                   jax.ShapeDtypeStruct((SUBLANES, e), F32)],
        scratch=[pltpu.VMEM((e3, d), BF16), pltpu.VMEM((e, d), BF16), pltpu.SemaphoreType.DMA((2 * N_DEV,))],
        hosted=hosted)


def _wgrad(a, b, bm, name, hosted=()):
    t_tok, m = a.shape
    n = b.shape[1]

    def body(a_ref, b_ref, o_ref):
        o_ref[...] = _tn(a_ref[...], b_ref[...]).astype(o_ref.dtype)

    outs, h_outs = _hosting_call(
        body, name, m // bm, [a, b],
        in_specs=[pl.BlockSpec((t_tok, bm), lambda i: (0, i)), _const_spec((t_tok, n))],
        out_specs=[pl.BlockSpec((bm, n), lambda i: (i, 0))],
        out_shape=[jax.ShapeDtypeStruct((m, n), BF16)],
        scratch=[], hosted=hosted)
    return (outs[0], h_outs) if hosted else outs[0]


def _sum_slots(land, rb, name):
    n_slots, rows, cols = land.shape

    def body(l_ref, o_ref):
        acc = l_ref[0].astype(F32)
        for k in range(1, n_slots):
            acc = acc + l_ref[k].astype(F32)
        o_ref[...] = acc

    return pl.pallas_call(
        body, name=name, grid=(rows // rb,),
        in_specs=[pl.BlockSpec((n_slots, rb, cols), lambda i: (0, i, 0))],
        out_specs=pl.BlockSpec((rb, cols), lambda i: (i, 0)),
        out_shape=jax.ShapeDtypeStruct((rows, cols), F32),
        compiler_params=_params(sequential=False),
    )(land)


def _adamw(w, grad, m, v, rb, name):
    rows, cols = w.shape
    c1 = 1.0 / (1.0 - ADAM_B1 ** ADAM_STEP)
    c2 = 1.0 / (1.0 - ADAM_B2 ** ADAM_STEP)

    def body(w_ref, g_ref, m_ref, v_ref, d_ref, mo_ref, vo_ref):
        gv = g_ref[...]
        mn = ADAM_B1 * m_ref[...] + (1.0 - ADAM_B1) * gv
        vn = ADAM_B2 * v_ref[...] + (1.0 - ADAM_B2) * (gv * gv)
        mo_ref[...] = mn
        vo_ref[...] = vn
        d_ref[...] = -ADAM_LR * ((mn * c1) / (jnp.sqrt(vn * c2) + ADAM_EPS) + ADAM_WD * w_ref[...])

    spec = pl.BlockSpec((rb, cols), lambda i: (i, 0))
    shape = jax.ShapeDtypeStruct((rows, cols), F32)
    return pl.pallas_call(
        body, name=name, grid=(rows // rb,),
        in_specs=[spec] * 4, out_specs=[spec] * 3, out_shape=[shape] * 3,
        compiler_params=_params(sequential=False),
    )(w, grad, m, v)


def _pack_shards(groups, name, hosted=()):
    flat = [(part, layer) for group in groups for part, layer, _ in group]
    rows = [[p.shape[2] if turn else p.shape[1] for p, _, turn in group] for group in groups]
    first, _, first_turn = groups[0][0]
    width = first.shape[1] if first_turn else first.shape[2]

    def body(*refs):
        ins, outs = refs[:len(flat)], refs[len(flat):]
        k = 0
        for gi, group in enumerate(groups):
            off = 0
            for (_, _, turn), n in zip(group, rows[gi]):
                part = ins[k][...].astype(BF16)
                if turn:
                    r = lax.broadcasted_iota(jnp.int32, (n, n), 0)
                    c = lax.broadcasted_iota(jnp.int32, (n, n), 1)
                    part = _nt((r == c).astype(BF16), part).astype(BF16)
                outs[gi][off:off + n, :] = part
                off += n
                k += 1

    return _hosting_call(
        body, name, 1, [p for p, _ in flat],
        in_specs=[pl.BlockSpec((None,) + p.shape[1:], lambda i, layer=layer: (layer, 0, 0)) for p, layer in flat],
        out_specs=[_const_spec((sum(r), width)) for r in rows],
        out_shape=[jax.ShapeDtypeStruct((sum(r), width), BF16) for r in rows],
        scratch=[], hosted=hosted)


def _split_bf16(a):
    hi = a.astype(BF16)
    rest = a - hi.astype(F32)
    mid = rest.astype(BF16)
    return hi, mid, (rest - mid.astype(F32)).astype(BF16)


def _reduce_adamw(lands, w, m, v, transpose, name, hosted=()):
    n_layers, rows_w, cols_w = w.shape
    c1 = 1.0 / (1.0 - ADAM_B1 ** ADAM_STEP)
    c2 = 1.0 / (1.0 - ADAM_B2 ** ADAM_STEP)
    flat = [piece for pieces in lands for piece in pieces]
    counts = [len(pieces) for pieces in lands]
    if transpose:
        tiles = rows_w // MXU_WIDTH
        blk = (MXU_WIDTH, cols_w)
        land_specs = [pl.BlockSpec((N_CHIP, n, MXU_WIDTH), lambda i, b=first // n: (0, b, i % tiles))
                      for _, first, n in flat]
        for _, first, n in flat:
            assert first % n == 0
    else:
        tiles = 2
        blk = (rows_w // tiles, cols_w)
        assert all(c == 1 for c in counts)
        land_specs = [pl.BlockSpec((N_CHIP,) + blk, lambda i, b=first // blk[0]: (0, b + i % tiles, 0))
                      for _, first, _ in flat]
        for _, first, _ in flat:
            assert first % blk[0] == 0

    def body(*refs):
        land_refs = refs[:len(flat)]
        w_ref, m_ref, v_ref, g_ref, d_ref, mo_ref, vo_ref = refs[len(flat):]
        layer = pl.program_id(0) // tiles

        def total(ref):
            acc = ref[0].astype(F32)
            for q in range(1, N_CHIP):
                acc = acc + ref[q].astype(F32)
            return acc

        def layer_sum(k):
            first = sum(counts[:k])
            parts = [total(land_refs[first + j]) for j in range(counts[k])]
            return parts[0] if len(parts) == 1 else jnp.concatenate(parts, axis=0)

        gv = layer_sum(0)
        for k in range(1, n_layers):
            gv = jnp.where(layer == k, layer_sum(k), gv)
        if transpose:
            r = lax.broadcasted_iota(jnp.int32, (MXU_WIDTH, MXU_WIDTH), 0)
            c = lax.broadcasted_iota(jnp.int32, (MXU_WIDTH, MXU_WIDTH), 1)
            eye = (r == c).astype(BF16)
            hi, mid, lo = _split_bf16(gv)
            gv = _nt(eye, hi) + _nt(eye, mid) + _nt(eye, lo)
        g_ref[...] = gv
        mn = ADAM_B1 * m_ref[...] + (1.0 - ADAM_B1) * gv
        vn = ADAM_B2 * v_ref[...] + (1.0 - ADAM_B2) * (gv * gv)
        mo_ref[...] = mn
        vo_ref[...] = vn
        d_ref[...] = -ADAM_LR * ((mn * c1) / (jnp.sqrt(vn * c2) + ADAM_EPS) + ADAM_WD * w_ref[...])

    spec = pl.BlockSpec((None,) + blk, lambda i: (i // tiles, i % tiles, 0))
    shape = jax.ShapeDtypeStruct(w.shape, F32)
    outs, h_outs = _hosting_call(
        body, name, n_layers * tiles, [land for land, _, _ in flat] + [w, m, v],
        in_specs=land_specs + [spec] * 3, out_specs=[spec] * 4, out_shape=[shape] * 4, scratch=[], hosted=hosted)
    return (outs, h_outs) if hosted else outs


def _pack_small(parts, rows):
    flat = jnp.concatenate([p.reshape(-1).astype(F32) for p in parts])
    return jnp.pad(flat, (0, rows * LANES - flat.shape[0])).reshape(rows, LANES)


def _unpack_small(packed, shapes):
    flat = packed.reshape(-1)
    out = []
    pos = 0
    for s in shapes:
        n = math.prod(s)
        out.append(flat[pos:pos + n].reshape(s))
        pos += n
    return out


def kernel(x, mix_norm, ffn_norm, a_w_in, a_v_gain, a_v_bias, a_w_s, a_b_s, a_w_out, b_w_in, b_conv_w, b_w_out, ffn_w_gate, ffn_w_up, ffn_w_down, final_norm, loss_target, m_mix_norm, m_ffn_norm, m_a_w_in, m_a_v_gain, m_a_v_bias, m_a_w_s, m_a_b_s, m_a_w_out, m_b_w_in, m_b_conv_w, m_b_w_out, m_ffn_w_gate, m_ffn_w_up, m_ffn_w_down, m_final_norm, v_mix_norm, v_ffn_norm, v_a_w_in, v_a_v_gain, v_a_v_bias, v_a_w_s, v_a_b_s, v_a_w_out, v_b_w_in, v_b_conv_w, v_b_w_out, v_ffn_w_gate, v_ffn_w_up, v_ffn_w_down, v_final_norm):
    bsz, seq, d = x.shape
    t_tok = bsz * seq
    me = _my_index()
    xt = x.reshape(t_tok, d)
    target = loss_target.reshape(t_tok, d)
    e_a = a_v_gain.shape[1]
    e_b = b_w_out.shape[1] * N_DEV
    n_layers = ffn_w_gate.shape[0]
    f_shard = ffn_w_gate.shape[2]
    f_full = f_shard * N_DEV

    conv_pad = jnp.pad(b_conv_w[0], ((0, SUBLANES - CONV_W), (0, 0)))
    sh_a = jnp.concatenate([a_w_in[0].T, a_w_out[0]]).astype(BF16)
    bfull = jnp.repeat(a_b_s[0].T, GROUP, axis=1)

    gate_t, up_t = ffn_w_gate.transpose(0, 2, 1), ffn_w_up.transpose(0, 2, 1)
    (sh_b, sh_f0, sh_f1g, sh_f1ud), (gath_a, conv_g) = _pack_shards(
        [[(b_w_in, 0, True), (b_w_out, 0, False)],
         [(gate_t, 0, False), (up_t, 0, False), (ffn_w_down, 0, False)],
         [(gate_t, 1, False)],
         [(up_t, 1, False), (ffn_w_down, 1, False)]],
        "pack_shards", hosted=[_HostedGathers([sh_a, conv_pad])])
    conv_full = jnp.pad(conv_g[:, :CONV_W, :].transpose(1, 0, 2).reshape(CONV_W, e_b), ((0, SUBLANES - CONV_W), (0, 0)))
    (x1, gd_a, u_a, vhat_a, sv_a, y_a, rstd_a), (gath_f0,) = _mixer_a_fwd(
        xt, mix_norm[0:1], gath_a, a_v_gain, a_v_bias, a_w_s[0], bfull, tm=TOKEN_TILE,
        hosted=[_HostedGathers([sh_f0])])
    srcs0 = [(gath_f0, 0), (gath_f0, f_shard), (gath_f0, 2 * f_shard)]
    (x2, gate0, up0), (gath_b, gath_f1g) = _ffn_fwd(x1, ffn_norm[0:1], srcs0, f_shard, tm=TOKEN_TILE, name="ffn_fwd0",
                                                    hosted=[_HostedGathers([sh_b, sh_f1g])])
    (x3, p_b), (gath_f1ud,) = _mixer_b_fwd(x2, mix_norm[1:2], gath_b, conv_full, tm=TOKEN_TILE, seq=seq,
                                           hosted=[_HostedGathers([sh_f1ud])])
    srcs1 = [(gath_f1g, 0), (gath_f1ud, 0), (gath_f1ud, f_shard)]
    (loss_part, dx4, dx4_bf, d_final, gate1, up1), _ = _ffn_fwd(
        x3, ffn_norm[1:2], srcs1, f_shard, tm=TOKEN_TILE, name="ffn_fwd1", head=(target, final_norm.reshape(1, d)))

    ffn_entries = [(0, 0, f_shard), (0, f_full, f_shard), (1, 0, f_shard)]
    (dx3, dx3_bf, h_f1, act1, dgu1, d_fn1), _ = _ffn_bwd(dx4, x3, gate1, up1, ffn_norm[1:2], srcs1, f_shard, tm=TOKEN_TILE,
                                                         name="ffn_bwd1")
    g_down1 = _wgrad(act1, dx4_bf, WGRAD_ROWS, "wgrad_down1")
    g_gu1 = _wgrad(dgu1, h_f1, WGRAD_ROWS_WIDE, "wgrad_gate_up1")
    ps_f1 = _pair_reduce([g_gu1, g_down1], ffn_entries, "pair_reduce_f1")
    (dx2, dx2_bf, h_b, y_b, dp_b, d_mn1, d_conv), (land_f1gu,) = _mixer_b_bwd(
        dx3, x2, p_b, mix_norm[1:2], gath_b, conv_full, tm=TOKEN_TILE_WIDE, seq=seq,
        hosted=[_HostedChipScatter(ps_f1, 0, 2 * f_shard)])
    g_b_out = _wgrad(y_b, dx3_bf, WGRAD_ROWS, "wgrad_b_out")
    g_b_in = _wgrad(dp_b, h_b, WGRAD_ROWS_WIDE, "wgrad_b_in")
    ps_b = _pair_reduce([g_b_in, g_b_out], [(0, 0, b_w_in.shape[2]), (1, 0, b_w_out.shape[1])], "pair_reduce_b")
    (dx1, dx1_bf, h_f0, act0, dgu0, d_fn0), (land_f1d, land_b) = _ffn_bwd(
        dx2, x1, gate0, up0, ffn_norm[0:1], srcs0, f_shard, tm=TOKEN_TILE, name="ffn_bwd0",
        hosted=[_HostedChipScatter(ps_f1, 2 * f_shard, f_shard), _HostedChipScatter(ps_b)])
    g_down0 = _wgrad(act0, dx2_bf, WGRAD_ROWS, "wgrad_down0")
    g_gu0 = _wgrad(dgu0, h_f0, WGRAD_ROWS_WIDE, "wgrad_gate_up0")
    g_a_out = _wgrad(y_a, dx1_bf, WGRAD_ROWS, "wgrad_a_out")
    n_ao = a_w_out.shape[1]
    ps_f0ao = _pair_reduce([g_gu0, g_down0, g_a_out], ffn_entries + [(2, 0, n_ao)], "pair_reduce_f0_a_out")
    (dx0, _, h_a, dz_a, d_mn0, d_gain, d_bias, d_ws, d_bs_acc), (land_f0, land_ao) = _mixer_a_bwd(
        dx1, xt, gd_a, u_a, vhat_a, sv_a, rstd_a, mix_norm[0:1], gath_a, a_v_gain, a_v_bias, a_w_s[0], tm=TOKEN_TILE,
        hosted=[_HostedChipScatter(ps_f0ao, 0, 3 * f_shard), _HostedChipScatter(ps_f0ao, 3 * f_shard, n_ao)])
    d_bs = d_bs_acc.reshape(HEADS, CHUNK)

    small_grads = [jnp.concatenate([d_mn0, d_mn1]), jnp.concatenate([d_fn0, d_fn1]), d_gain, d_bias, d_ws, d_bs,
                   d_final, d_conv[:CONV_W], loss_part]
    small_shapes = [(n_layers, d), (n_layers, d), (1, e_a), (1, e_a), (1, HEADS, CHUNK, CHUNK), (1, HEADS, CHUNK), (d,),
                    (CONV_W, e_b), ()]
    n_small = sum(math.prod(s) for s in small_shapes)
    blk_rows = -(-n_small // (N_DEV * LANES * SUBLANES)) * SUBLANES
    small_rows = blk_rows * N_DEV
    packed = _pack_small(small_grads, small_rows)
    g_a_in, (small_land,) = _wgrad(dz_a, h_a, WGRAD_ROWS_WIDE, "wgrad_a_in", hosted=[_HostedScatterAll(packed)])
    ps_ai = _pair_reduce([g_a_in], [(0, 0, a_w_in.shape[2])], "pair_reduce_a_in")
    small_sum = _sum_slots(small_land, blk_rows, "sum_small")

    land_ai, small_gath = _exchange([_HostedChipScatter(ps_ai), _HostedGathers([small_sum])], "tail_exchange")
    small_all = small_gath.reshape(small_rows, LANES)

    n_b_in = b_w_in.shape[2]
    gate_out = _reduce_adamw([[(land_f0, 0, f_shard)], [(land_f1gu, 0, f_shard)]], gate_t,
                             m_ffn_w_gate.transpose(0, 2, 1), v_ffn_w_gate.transpose(0, 2, 1), False, "adamw_gate")
    up_out = _reduce_adamw([[(land_f0, f_shard, f_shard)], [(land_f1gu, f_shard, f_shard)]], up_t,
                           m_ffn_w_up.transpose(0, 2, 1), v_ffn_w_up.transpose(0, 2, 1), False, "adamw_up")
    res = {
        "a_w_in": _reduce_adamw([[(land_ai, 0, a_w_in.shape[2])]], a_w_in, m_a_w_in, v_a_w_in, True, "adamw_a_in"),
        "a_w_out": _reduce_adamw([[(land_ao, 0, a_w_out.shape[1])]], a_w_out, m_a_w_out, v_a_w_out, False,
                                 "adamw_a_out"),
        "b_w_in": _reduce_adamw([[(land_b, 0, n_b_in)]], b_w_in, m_b_w_in, v_b_w_in, True, "adamw_b_in"),
        "b_w_out": _reduce_adamw([[(land_b, n_b_in, b_w_out.shape[1])]], b_w_out, m_b_w_out, v_b_w_out, False,
                                 "adamw_b_out"),
        "ffn_w_gate": [o.transpose(0, 2, 1) for o in gate_out],
        "ffn_w_up": [o.transpose(0, 2, 1) for o in up_out],
        "ffn_w_down": _reduce_adamw([[(land_f0, 2 * f_shard, f_shard)], [(land_f1d, 0, f_shard)]], ffn_w_down,
                                    m_ffn_w_down, v_ffn_w_down, False, "adamw_down"),
    }

    (gr_mix, gr_ffn, gr_gain, gr_bias, gr_ws, gr_bs, gr_final, gr_conv_full, loss) = _unpack_small(small_all, small_shapes)
    gr_conv = lax.dynamic_slice_in_dim(gr_conv_full, me * (e_b // N_DEV), e_b // N_DEV, axis=1)[None]

    small_w =[mix_norm, ffn_norm, a_v_gain, a_v_bias, a_w_s, a_b_s, final_norm]
    small_m = [m_mix_norm, m_ffn_norm, m_a_v_gain, m_a_v_bias, m_a_w_s, m_a_b_s, m_final_norm]
    small_v = [v_mix_norm, v_ffn_norm, v_a_v_gain, v_a_v_bias, v_a_w_s, v_a_b_s, v_final_norm]
    small_g = [gr_mix, gr_ffn, gr_gain, gr_bias, gr_ws, gr_bs, gr_final]
    sm_shapes = small_shapes[:len(small_w)]
    sm_out = _adamw(_pack_small(small_w, small_rows), _pack_small(small_g, small_rows), _pack_small(small_m, small_rows),
                    _pack_small(small_v, small_rows), small_rows, "adamw_small")
    sm_delta, sm_m, sm_v = [_unpack_small(o, sm_shapes) for o in sm_out]

    conv_out = _adamw(b_conv_w[0], gr_conv[0], m_b_conv_w[0], v_b_conv_w[0], CONV_W, "adamw_conv")
    conv_delta, conv_m, conv_v = [o[None] for o in conv_out]

    order = ["mix_norm", "ffn_norm", "a_w_in", "a_v_gain", "a_v_bias", "a_w_s", "a_b_s", "a_w_out", "b_w_in",
             "b_conv_w", "b_w_out", "ffn_w_gate", "ffn_w_up", "ffn_w_down", "final_norm"]
    small_names = ["mix_norm", "ffn_norm", "a_v_gain", "a_v_bias", "a_w_s", "a_b_s", "final_norm"]
    grads = {"b_conv_w": gr_conv}
    deltas, new_m, new_v = {}, {}, {}
    for k, name in enumerate(small_names):
        grads[name] = small_g[k]
        deltas[name], new_m[name], new_v[name] = sm_delta[k], sm_m[k], sm_v[k]
    deltas["b_conv_w"], new_m["b_conv_w"], new_v["b_conv_w"] = conv_delta, conv_m, conv_v
    for name, (gg, dl, mm, vv) in res.items():
        grads[name], deltas[name], new_m[name], new_v[name] = gg, dl, mm, vv

    grad_x = dx0.reshape(bsz, seq, d)
    return (loss, grad_x, *[grads[n] for n in order], *[deltas[n] for n in order],
            *[new_m[n] for n in order], *[new_v[n] for n in order])
```

```python
import math

import jax
import jax.numpy as jnp
from jax import lax
from jax.experimental import pallas as pl
from jax.experimental.pallas import tpu as pltpu

F32 = jnp.float32
BF16 = jnp.bfloat16

N_DEV = 8
N_CHIP = 4
CHUNK = 128
HEADS = 16
GROUP = 128
CONV_W = 3
NORM_EPS = 1e-6
GELU_C = math.sqrt(2.0 / math.pi)
GELU_K = 0.044715

ADAM_LR = 0.001
ADAM_B1 = 0.9
ADAM_B2 = 0.999
ADAM_EPS = 1e-08
ADAM_WD = 0.01
ADAM_STEP = 10

LANES = 128
SUBLANES = 8
VMEM_LIMIT = 60 * 1024 * 1024
HALO = 16
MXU_WIDTH = 256
FFN_CHUNKS = 2
TOKEN_TILE = 256
TOKEN_TILE_WIDE = 512
WGRAD_ROWS = 256
WGRAD_ROWS_WIDE = 512
GATHER_RELAY_AT = 0.56
GATHER_FORWARD_LEAD = 2

MESH = pl.DeviceIdType.MESH
ANY = pl.BlockSpec(memory_space=pl.ANY)

PEER_FLIPS = {"sibling": (0, 0, 1), "x": (1, 0, 0), "y": (0, 1, 0), "diagonal": (1, 1, 0),
              "x_other": (1, 0, 1), "y_other": (0, 1, 1), "diagonal_other": (1, 1, 1)}
COLLECTIVE_IDS = {frozenset(["sibling"]): 0,
                  frozenset(["sibling", "x", "y"]): 1,
                  frozenset(["sibling", "x", "y", "diagonal"]): 2,
                  frozenset(["x", "y", "diagonal"]): 3,
                  frozenset(PEER_FLIPS): 4}


def _params(sequential=True):
    return pltpu.CompilerParams(
        dimension_semantics=("arbitrary",) if sequential else None,
        vmem_limit_bytes=VMEM_LIMIT)


def _nn(a, b):
    return jnp.dot(a, b, preferred_element_type=F32)


def _nt(a, b):
    return lax.dot_general(a, b, (((1,), (1,)), ((), ())), preferred_element_type=F32)


def _tn(a, b):
    return lax.dot_general(a, b, (((0,), (0,)), ((), ())), preferred_element_type=F32)


def _row_mean(a):
    return jnp.mean(a, axis=-1, keepdims=True)


def _col_sum(a):
    return jnp.sum(a, axis=0, keepdims=True)


def _rms_fwd(x, g):
    r = lax.rsqrt(_row_mean(x * x) + NORM_EPS)
    xhat = x * r
    return xhat * g, xhat, r


def _rms_bwd(dh, xhat, r, g):
    a = dh * g
    dx = r * (a - xhat * _row_mean(a * xhat))
    return dx, _col_sum(dh * xhat)


def _gelu_and_grad(x):
    x2 = x * x
    t = jnp.tanh(x * (GELU_C + (GELU_C * GELU_K) * x2))
    half = 0.5 * t + 0.5
    d = half + x * (0.5 - 0.5 * (t * t)) * (GELU_C + (3.0 * GELU_C * GELU_K) * x2)
    return x * half, d


def _sigmoid(x):
    return 1.0 / (1.0 + jnp.exp(-x))


def _row_spec(tm, width):
    return pl.BlockSpec((tm, width), lambda i: (i, 0))


def _const_spec(shape):
    nd = len(shape)
    return pl.BlockSpec(shape, lambda i: (0,) * nd)


def _load_group(parts, sems):
    @pl.when(pl.program_id(0) == 0)
    def _():
        copies = []
        for k, (gath_ref, first, n, dst) in enumerate(parts):
            for j in range(N_DEV):
                copies.append(pltpu.make_async_copy(gath_ref.at[j, pl.ds(first, n), :], dst.at[pl.ds(j * n, n), :],
                                                    sems.at[k * N_DEV + j]))
        for cp in copies:
            cp.start()
        for cp in copies:
            cp.wait()


def _hosting_call(body, name, n_steps, arrays, in_specs, out_specs, out_shape, scratch, hosted=()):
    n_in, n_out, n_scr = len(arrays), len(out_shape), len(scratch)
    h_arrays = [a for h in hosted for a in h.arrays]
    h_shapes = [s for h in hosted for s in h.out_shapes]
    h_sems = [s for h in hosted for s in h.sem_shapes]
    peers = sorted(set().union(*[h.peers for h in hosted])) if hosted else []

    def handshake():
        @pl.when(pl.program_id(0) == 0)
        def _():
            x, y, c = lax.axis_index("x"), lax.axis_index("y"), lax.axis_index("c")
            barrier = pltpu.get_barrier_semaphore()
            for p in peers:
                fx, fy, fc = PEER_FLIPS[p]
                peer = (1 - x if fx else x, 1 - y if fy else y, 1 - c if fc else c)
                pl.semaphore_signal(barrier, inc=1, device_id=peer, device_id_type=MESH)
            pl.semaphore_wait(barrier, len(peers))

    def full_body(*refs):
        pos = 0
        groups = []
        for n in (n_in, len(h_arrays), n_out, len(h_shapes), n_scr, len(h_sems)):
            groups.append(refs[pos:pos + n])
            pos += n
        own_in, h_in, own_out, h_out, own_scr, h_sem = groups
        per_host = []
        pi = po = ps = 0
        for h in hosted:
            ni, no, ns = len(h.arrays), len(h.out_shapes), len(h.sem_shapes)
            per_host.append((h, h_in[pi:pi + ni], h_out[po:po + no], h_sem[ps:ps + ns]))
            pi, po, ps = pi + ni, po + no, ps + ns
        if hosted:
            handshake()
        for h, ins, outs, sems in per_host:
            h.begin(ins, outs, sems, n_steps)
        body(*own_in, *own_out, *own_scr)
        for h, ins, outs, sems in per_host:
            h.end(ins, outs, sems, n_steps)

    outs = pl.pallas_call(
        full_body, name=name, grid=(n_steps,),
        in_specs=list(in_specs) + [ANY] * len(h_arrays),
        out_specs=list(out_specs) + [ANY] * len(h_shapes),
        out_shape=list(out_shape) + h_shapes,
        scratch_shapes=list(scratch) + h_sems,
        compiler_params=pltpu.CompilerParams(
            dimension_semantics=("arbitrary",), vmem_limit_bytes=VMEM_LIMIT,
            collective_id=COLLECTIVE_IDS[frozenset(peers)] if hosted else None),
    )(*arrays, *h_arrays)
    return outs[:n_out], outs[n_out:]


def _my_index():
    return 4 * lax.axis_index("x") + 2 * lax.axis_index("y") + lax.axis_index("c")


GATHER_COPIES = 8


def _gather_relays(n_rows, dtype):
    rows_per_tile = SUBLANES * 4 // jnp.dtype(dtype).itemsize
    return n_rows % 2 == 0 and (n_rows // 2) % rows_per_tile == 0


class _Gather:
    def __init__(self, shard, out, send_sems, recv_sems, local_sem):
        self.shard, self.out = shard, out
        self.send_sems, self.recv_sems, self.local_sem = send_sems, recv_sems, local_sem
        x, y, c = lax.axis_index("x"), lax.axis_index("y"), lax.axis_index("c")
        self.c = c
        self.me, self.sibling = (x, y, c), (x, y, 1 - c)
        self.xn, self.yn, self.dg = (1 - x, y), (x, 1 - y), (1 - x, 1 - y)
        self.n = shard.shape[0]
        self.half = self.n // 2
        self.relays = _gather_relays(self.n, shard.dtype)

    def _slot(self, dev, lo=0, hi=None):
        hi = self.n if hi is None else hi
        return self.out.at[4 * dev[0] + 2 * dev[1] + dev[2], pl.ds(lo, hi - lo), :]

    def _copy(self, k, block, to, src=None, lo=0, hi=None):
        return pltpu.make_async_remote_copy(
            src_ref=self._slot(block, lo, hi) if src is None else src, dst_ref=self._slot(block, lo, hi),
            send_sem=self.send_sems.at[k], recv_sem=self.recv_sems.at[k], device_id=to, device_id_type=MESH)

    def _local(self):
        return pltpu.make_async_copy(self.shard, self._slot(self.me), self.local_sem)

    def start(self):
        c = self.c
        self._local().start()
        self._copy(0, self.me, self.sibling, src=self.shard).start()
        self._copy(1, self.me, (*self.xn, c), src=self.shard).start()
        self._copy(2, self.me, (*self.yn, c), src=self.shard).start()
        if not self.relays:
            self._copy(3, self.me, (*self.dg, c), src=self.shard).start()

    def relay(self):
        c = self.c
        if self.relays:
            self._copy(1, (*self.xn, c), self.me).wait_recv()
            self._copy(3, (*self.xn, c), (*self.yn, c), hi=self.half).start()
            self._copy(2, (*self.yn, c), self.me).wait_recv()
            self._copy(4, (*self.yn, c), (*self.xn, c), lo=self.half).start()

    def forward(self):
        c = self.c
        if self.relays:
            self._copy(5, (*self.xn, c), self.sibling).start()
            self._copy(6, (*self.yn, c), self.sibling).start()
            self._copy(3, (*self.dg, c), self.me, hi=self.half).wait_recv()
            self._copy(4, (*self.dg, c), self.me, lo=self.half).wait_recv()
        else:
            self._copy(1, (*self.xn, c), self.me).wait_recv()
            self._copy(5, (*self.xn, c), self.sibling).start()
            self._copy(2, (*self.yn, c), self.me).wait_recv()
            self._copy(6, (*self.yn, c), self.sibling).start()
            self._copy(3, (*self.dg, c), self.me).wait_recv()
        self._copy(7, (*self.dg, c), self.sibling).start()

    def finish(self):
        c = self.c
        self._copy(0, self.sibling, self.me).wait_recv()
        for k, chip in ((5, self.xn), (6, self.yn), (7, self.dg)):
            self._copy(k, (*chip, 1 - c), self.me).wait_recv()
        for k in (0, 1, 2, 5, 6, 7):
            self._copy(k, self.me, self.sibling).wait_send()
        if self.relays:
            self._copy(3, self.me, self.sibling, hi=self.half).wait_send()
            self._copy(4, self.me, self.sibling, lo=self.half).wait_send()
        else:
            self._copy(3, self.me, self.sibling).wait_send()
        self._local().wait()


class _HostedGathers:
    def __init__(self, shards, mid_lead=GATHER_FORWARD_LEAD, relay_at=GATHER_RELAY_AT):
        n = len(shards)
        self.arrays = shards
        self.mid_lead, self.relay_at = mid_lead, relay_at
        self.peers = {"sibling", "x", "y"}
        if not all(_gather_relays(s.shape[0], s.dtype) for s in shards):
            self.peers.add("diagonal")
        self.out_shapes = [jax.ShapeDtypeStruct((N_DEV,) + s.shape, s.dtype) for s in shards]
        self.sem_shapes = [pltpu.SemaphoreType.DMA((n, GATHER_COPIES)), pltpu.SemaphoreType.DMA((n, GATHER_COPIES)),
                           pltpu.SemaphoreType.DMA((n,))]

    def _gathers(self, ins, outs, sems):
        return [_Gather(ins[a], outs[a], sems[0].at[a], sems[1].at[a], sems[2].at[a]) for a in range(len(ins))]

    def begin(self, ins, outs, sems, n_steps):
        i = pl.program_id(0)
        forward_step = max(n_steps - 1 - self.mid_lead, 0)
        relay_step = min(int(self.relay_at * n_steps), forward_step)

        @pl.when(i == 0)
        def _():
            for g in self._gathers(ins, outs, sems):
                g.start()

        if n_steps == 1:
            return

        @pl.when(i == relay_step)
        def _():
            for g in self._gathers(ins, outs, sems):
                g.relay()

        @pl.when(i == forward_step)
        def _():
            for g in self._gathers(ins, outs, sems):
                g.forward()

    def end(self, ins, outs, sems, n_steps):
        @pl.when(pl.program_id(0) == n_steps - 1)
        def _():
            gathers = self._gathers(ins, outs, sems)
            if n_steps == 1:
                for g in gathers:
                    g.relay()
                for g in gathers:
                    g.forward()
            for g in gathers:
                g.finish()


def _exchange(hosted, name):
    return _hosting_call(lambda: None, name, 1, [], [], [], [], [], hosted=hosted)[1]


class _ChipScatter:
    def __init__(self, pairsum, row0, land, send_sems, recv_sems, local_sem):
        self.pairsum, self.row0, self.land = pairsum, row0, land
        self.send_sems, self.recv_sems, self.local_sem = send_sems, recv_sems, local_sem
        x, y, c = lax.axis_index("x"), lax.axis_index("y"), lax.axis_index("c")
        self.c = c
        self.chip = 2 * x + y
        self.others = [(1 - x, y), (x, 1 - y), (1 - x, 1 - y)]

    def _src(self, chip):
        return self.pairsum.at[chip, pl.ds(self.row0, self.land.shape[1]), :]

    def _copy(self, k):
        ox, oy = self.others[k]
        return pltpu.make_async_remote_copy(
            src_ref=self._src(2 * ox + oy), dst_ref=self.land.at[self.chip],
            send_sem=self.send_sems.at[k], recv_sem=self.recv_sems.at[k], device_id=(ox, oy, self.c),
            device_id_type=MESH)

    def _arrival(self, k):
        ox, oy = self.others[k]
        return pltpu.make_async_remote_copy(
            src_ref=self._src(self.chip), dst_ref=self.land.at[2 * ox + oy],
            send_sem=self.send_sems.at[k], recv_sem=self.recv_sems.at[k], device_id=(ox, oy, self.c),
            device_id_type=MESH)

    def _local(self):
        return pltpu.make_async_copy(self._src(self.chip), self.land.at[self.chip], self.local_sem)

    def start(self):
        self._local().start()
        for k in range(N_CHIP - 1):
            self._copy(k).start()

    def finish(self):
        for k in range(N_CHIP - 1):
            self._arrival(k).wait_recv()
        for k in range(N_CHIP - 1):
            self._copy(k).wait_send()
        self._local().wait()


class _HostedChipScatter:
    def __init__(self, pairsum, row0=0, n=None):
        n = pairsum.shape[1] - row0 if n is None else n
        self.row0 = row0
        self.peers = {"x", "y", "diagonal"}
        self.arrays = [pairsum]
        self.out_shapes = [jax.ShapeDtypeStruct((N_CHIP, n, pairsum.shape[2]), pairsum.dtype)]
        self.sem_shapes = [pltpu.SemaphoreType.DMA((N_CHIP - 1,)), pltpu.SemaphoreType.DMA((N_CHIP - 1,)),
                           pltpu.SemaphoreType.DMA(())]

    def begin(self, ins, outs, sems, n_steps):
        @pl.when(pl.program_id(0) == 0)
        def _():
            _ChipScatter(ins[0], self.row0, outs[0], *sems).start()

    def end(self, ins, outs, sems, n_steps):
        @pl.when(pl.program_id(0) == n_steps - 1)
        def _():
            _ChipScatter(ins[0], self.row0, outs[0], *sems).finish()


def _pair_reduce(arrays, entries, name):
    n_arr, n_ent = len(arrays), len(entries)
    cols = arrays[0].shape[1]
    offsets = []
    total = 0
    for _, _, n in entries:
        offsets.append(total)
        total += n

    def body(*refs):
        ins, out_ref = refs[:n_arr], refs[n_arr]
        rbuf, own, send_sems, recv_sems, own_sems = refs[n_arr + 1:]
        q = pl.program_id(0)
        x, y, c = lax.axis_index("x"), lax.axis_index("y"), lax.axis_index("c")

        def block(e, chip, core):
            ai, first, n = entries[e]
            return ins[ai].at[pl.ds(first + (2 * chip + core) * n, n), :]

        def to_sibling(e, chip):
            return pltpu.make_async_remote_copy(
                src_ref=block(e, chip, 1 - c), dst_ref=rbuf.at[chip, pl.ds(offsets[e], entries[e][2]), :],
                send_sem=send_sems.at[e, chip], recv_sem=recv_sems.at[e, chip], device_id=(x, y, 1 - c),
                device_id_type=MESH)

        @pl.when(q == 0)
        def _():
            barrier = pltpu.get_barrier_semaphore()
            pl.semaphore_signal(barrier, inc=1, device_id=(x, y, 1 - c), device_id_type=MESH)
            pl.semaphore_wait(barrier, 1)
            for chip in range(N_CHIP):
                for e in range(n_ent):
                    to_sibling(e, chip).start()

        loads = [pltpu.make_async_copy(block(e, q, c), own.at[pl.ds(offsets[e], entries[e][2]), :], own_sems.at[e])
                 for e in range(n_ent)]
        for cp in loads:
            cp.start()
        for cp in loads:
            cp.wait()
        for e in range(n_ent):
            to_sibling(e, q).wait_recv()
        out_ref[...] = (own[...].astype(F32) + rbuf[q].astype(F32)).astype(out_ref.dtype)

        @pl.when(q == N_CHIP - 1)
        def _():
            for chip in range(N_CHIP):
                for e in range(n_ent):
                    to_sibling(e, chip).wait_send()

    return pl.pallas_call(
        body, name=name, grid=(N_CHIP,),
        in_specs=[ANY] * n_arr,
        out_specs=pl.BlockSpec((None, total, cols), lambda q: (q, 0, 0)),
        out_shape=jax.ShapeDtypeStruct((N_CHIP, total, cols), BF16),
        scratch_shapes=[pltpu.VMEM((N_CHIP, total, cols), BF16), pltpu.VMEM((total, cols), BF16),
                        pltpu.SemaphoreType.DMA((n_ent, N_CHIP)), pltpu.SemaphoreType.DMA((n_ent, N_CHIP)),
                        pltpu.SemaphoreType.DMA((n_ent,))],
        compiler_params=pltpu.CompilerParams(dimension_semantics=("arbitrary",), vmem_limit_bytes=VMEM_LIMIT,
                                             collective_id=COLLECTIVE_IDS[frozenset(["sibling"])]),
    )(*arrays)


class _HostedScatterAll:
    def __init__(self, packed):
        n = packed.shape[0] // N_DEV
        self.n = n
        self.peers = set(PEER_FLIPS)
        self.arrays = [packed]
        self.out_shapes = [jax.ShapeDtypeStruct((N_DEV, n, packed.shape[1]), packed.dtype)]
        self.sem_shapes = [pltpu.SemaphoreType.DMA((N_DEV - 1,)), pltpu.SemaphoreType.DMA((N_DEV - 1,)),
                           pltpu.SemaphoreType.DMA(())]

    def _copies(self, ins, outs, sems, with_arrivals):
        src, land = ins[0], outs[0]
        send_sems, recv_sems, local_sem = sems
        me = _my_index()

        def block(p):
            return src.at[pl.ds(p * self.n, self.n), :]

        local = pltpu.make_async_copy(block(me), land.at[me], local_sem)
        sends, arrivals = [], []
        for k in range(1, N_DEV):
            p = (me + k) % N_DEV
            q = (me + N_DEV - k) % N_DEV
            sends.append(pltpu.make_async_remote_copy(
                src_ref=block(p), dst_ref=land.at[me], send_sem=send_sems.at[k - 1], recv_sem=recv_sems.at[k - 1],
                device_id=(p // 4, (p // 2) % 2, p % 2), device_id_type=MESH))
            if with_arrivals:
                arrivals.append(pltpu.make_async_remote_copy(
                    src_ref=block(me), dst_ref=land.at[q], send_sem=send_sems.at[k - 1], recv_sem=recv_sems.at[k - 1],
                    device_id=(q // 4, (q // 2) % 2, q % 2), device_id_type=MESH))
        return local, sends, arrivals

    def begin(self, ins, outs, sems, n_steps):
        @pl.when(pl.program_id(0) == 0)
        def _():
            local, sends, _ = self._copies(ins, outs, sems, with_arrivals=False)
            local.start()
            for cp in sends:
                cp.start()

    def end(self, ins, outs, sems, n_steps):
        @pl.when(pl.program_id(0) == n_steps - 1)
        def _():
            local, sends, arrivals = self._copies(ins, outs, sems, with_arrivals=True)
            for cp in arrivals:
                cp.wait_recv()
            for cp in sends:
                cp.wait_send()
            local.wait()


def _tril_weights(ws_ref):
    r = lax.broadcasted_iota(jnp.int32, (CHUNK, CHUNK), 0)
    c = lax.broadcasted_iota(jnp.int32, (CHUNK, CHUNK), 1)
    return [jnp.where(r >= c, ws_ref[h], 0.0).astype(BF16) for h in range(HEADS)]


def _sgu_stats(zpre, gain, bias):
    e = zpre.shape[1] // 2
    z, dz = _gelu_and_grad(zpre)
    u, v = z[:, :e], z[:, e:]
    vc = v - _row_mean(v)
    rstd = lax.rsqrt(_row_mean(vc * vc) + NORM_EPS)
    vhat = vc * rstd
    return u, vhat, rstd, vhat * gain + bias, dz


def _spatial_fwd(wt, vn_bf, bfull_ref, sv_ref, tm):
    for ci in range(tm // CHUNK):
        rows = slice(ci * CHUNK, (ci + 1) * CHUNK)
        for h in range(HEADS):
            cols = slice(h * GROUP, (h + 1) * GROUP)
            sv_ref[rows, cols] = _nn(wt[h], vn_bf[rows, cols]) + bfull_ref[:, cols]


def _mixer_a_fwd(x, g, gath, gain, bias, ws, bfull, tm, hosted=()):
    t_tok, d = x.shape
    e = gain.shape[1]
    e2 = 2 * e
    n_in, n_out = e2 // N_DEV, e // N_DEV

    def body(x_ref, g_ref, gain_ref, bias_ref, ws_ref, bfull_ref, gath_ref,
             xo_ref, gd_ref, u_ref, vhat_ref, svo_ref, y_ref, rstd_ref, win_v, wout_v, sv_v, sems):
        _load_group([(gath_ref, 0, n_in, win_v), (gath_ref, n_in, n_out, wout_v)], sems)
        xv = x_ref[...]
        h = _rms_fwd(xv, g_ref[...])[0].astype(BF16)
        zpre = _nt(h, win_v[...])
        u, vhat, rstd, vn, gelu_d = _sgu_stats(zpre, gain_ref[...], bias_ref[...])
        gd_ref[...] = gelu_d.astype(BF16)
        u_ref[...] = u.astype(BF16)
        vhat_ref[...] = vhat.astype(BF16)
        rstd_ref[...] = rstd
        _spatial_fwd(_tril_weights(ws_ref), vn.astype(BF16), bfull_ref, sv_v, tm)
        sv = sv_v[...]
        svo_ref[...] = sv.astype(BF16)
        y = (u * sv).astype(BF16)
        y_ref[...] = y
        xo_ref[...] = xv + _nn(y, wout_v[...])

    return _hosting_call(
        body, "mixer_a_fwd", t_tok // tm, [x, g, gain, bias, ws, bfull, gath],
        in_specs=[_row_spec(tm, d), _const_spec((1, d)), _const_spec((1, e)), _const_spec((1, e)),
                  _const_spec((HEADS, CHUNK, CHUNK)), _const_spec((CHUNK, e)), ANY],
        out_specs=[_row_spec(tm, d), _row_spec(tm, e2), _row_spec(tm, e), _row_spec(tm, e), _row_spec(tm, e),
                   _row_spec(tm, e), _row_spec(tm, 1)],
        out_shape=[jax.ShapeDtypeStruct((t_tok, d), F32), jax.ShapeDtypeStruct((t_tok, e2), BF16),
                   jax.ShapeDtypeStruct((t_tok, e), BF16), jax.ShapeDtypeStruct((t_tok, e), BF16),
                   jax.ShapeDtypeStruct((t_tok, e), BF16), jax.ShapeDtypeStruct((t_tok, e), BF16),
                   jax.ShapeDtypeStruct((t_tok, 1), F32)],
        scratch=[pltpu.VMEM((e2, d), BF16), pltpu.VMEM((e, d), BF16), pltpu.VMEM((tm, e), F32),
                 pltpu.SemaphoreType.DMA((2 * N_DEV,))],
        hosted=hosted)


def _mixer_a_bwd(dout, x, gd, u_sav, vhat_sav, sv_sav, rstd_sav, g, gath, gain, bias, ws, tm, hosted=()):
    t_tok, d = x.shape
    e = gain.shape[1]
    e2 = 2 * e
    n_in, n_out = e2 // N_DEV, e // N_DEV
    n_steps = t_tok // tm

    def body(dout_ref, x_ref, gd_ref, u_ref, vhat_ref, sv_ref, rstd_ref, g_ref, gain_ref, bias_ref, ws_ref, gath_ref,
             dx_ref, dxb_ref, h_ref, dz_ref, dg_ref, dgain_ref, dbias_ref, dws_ref, dbso_ref,
             win_v, wout_v, dvn_v, dbs_ref, sems):
        i = pl.program_id(0)
        _load_group([(gath_ref, 0, n_in, win_v), (gath_ref, n_in, n_out, wout_v)], sems)

        @pl.when(i == 0)
        def _():
            dg_ref[...] = jnp.zeros_like(dg_ref)
            dgain_ref[...] = jnp.zeros_like(dgain_ref)
            dbias_ref[...] = jnp.zeros_like(dbias_ref)
            dws_ref[...] = jnp.zeros_like(dws_ref)
            dbs_ref[...] = jnp.zeros_like(dbs_ref)

        xv = x_ref[...]
        gv = g_ref[...]
        hv, xhat, r = _rms_fwd(xv, gv)
        h_ref[...] = hv.astype(BF16)
        gain_v = gain_ref[...]
        vhat = vhat_ref[...].astype(F32)
        vn_bf = (vhat * gain_v + bias_ref[...]).astype(BF16)
        wt = _tril_weights(ws_ref)

        dov = dout_ref[...]
        dy = _nt(dov.astype(BF16), wout_v[...])
        du = dy * sv_ref[...].astype(F32)
        dsv = dy * u_ref[...].astype(F32)
        dsv_bf = dsv.astype(BF16)
        for ci in range(tm // CHUNK):
            rows = slice(ci * CHUNK, (ci + 1) * CHUNK)
            dbs_ref[...] += dsv[rows, :]
            for h in range(HEADS):
                cols = slice(h * GROUP, (h + 1) * GROUP)
                dvn_v[rows, cols] = _tn(wt[h], dsv_bf[rows, cols])
                dws_ref[h] += _nt(dsv_bf[rows, cols], vn_bf[rows, cols])
        dvn = dvn_v[...]
        dgain_ref[...] += _col_sum(dvn * vhat)
        dbias_ref[...] += _col_sum(dvn)
        dvhat = dvn * gain_v
        dv = rstd_ref[...] * (dvhat - _row_mean(dvhat) - vhat * _row_mean(dvhat * vhat))
        dzpre = (jnp.concatenate([du, dv], axis=1) * gd_ref[...].astype(F32)).astype(BF16)
        dz_ref[...] = dzpre
        dh = _nn(dzpre, win_v[...])
        dxr, dg_row = _rms_bwd(dh, xhat, r, gv)
        dg_ref[...] += dg_row
        dx = dov + dxr
        dx_ref[...] = dx
        dxb_ref[...] = dx.astype(BF16)

        @pl.when(i == n_steps - 1)
        def _():
            rr = lax.broadcasted_iota(jnp.int32, (CHUNK, CHUNK), 0)
            cc = lax.broadcasted_iota(jnp.int32, (CHUNK, CHUNK), 1)
            for h in range(HEADS):
                dws_ref[h] = jnp.where(rr >= cc, dws_ref[h], 0.0)
                dbso_ref[h] = jnp.sum(dbs_ref[:, h * GROUP:(h + 1) * GROUP], axis=1, keepdims=True)

    return _hosting_call(
        body, "mixer_a_bwd", n_steps, [dout, x, gd, u_sav, vhat_sav, sv_sav, rstd_sav, g, gain, bias, ws, gath],
        in_specs=[_row_spec(tm, d), _row_spec(tm, d), _row_spec(tm, e2), _row_spec(tm, e), _row_spec(tm, e),
                  _row_spec(tm, e), _row_spec(tm, 1), _const_spec((1, d)),
                  _const_spec((1, e)), _const_spec((1, e)), _const_spec((HEADS, CHUNK, CHUNK)), ANY],
        out_specs=[_row_spec(tm, d), _row_spec(tm, d), _row_spec(tm, d), _row_spec(tm, e2),
                   _const_spec((1, d)), _const_spec((1, e)), _const_spec((1, e)),
                   _const_spec((HEADS, CHUNK, CHUNK)), _const_spec((HEADS, CHUNK, 1))],
        out_shape=[jax.ShapeDtypeStruct((t_tok, d), F32), jax.ShapeDtypeStruct((t_tok, d), BF16),
                   jax.ShapeDtypeStruct((t_tok, d), BF16), jax.ShapeDtypeStruct((t_tok, e2), BF16),
                   jax.ShapeDtypeStruct((1, d), F32), jax.ShapeDtypeStruct((1, e), F32),
                   jax.ShapeDtypeStruct((1, e), F32), jax.ShapeDtypeStruct((HEADS, CHUNK, CHUNK), F32),
                   jax.ShapeDtypeStruct((HEADS, CHUNK, 1), F32)],
        scratch=[pltpu.VMEM((e2, d), BF16), pltpu.VMEM((e, d), BF16), pltpu.VMEM((tm, e), F32),
                 pltpu.VMEM((CHUNK, e), F32), pltpu.SemaphoreType.DMA((2 * N_DEV,))],
        hosted=hosted)


def _ffn_fwd(x, g, srcs, nf, tm, name, hosted=(), head=None):
    t_tok, d = x.shape
    f = nf * N_DEV
    firsts = [first for _, first in srcs]
    n_head = 2 if head else 0

    def body(*refs):
        x_ref, g_ref, sg_ref, su_ref, sd_ref = refs[:5]
        gate_ref, up_ref, wg_v, wu_v, wd_v, sems = refs[-6:]
        _load_group(
            [(sg_ref, firsts[0], nf, wg_v), (su_ref, firsts[1], nf, wu_v), (sd_ref, firsts[2], nf, wd_v)], sems)
        if head:
            t_ref, gf_ref, loss_ref, dx_ref, dxb_ref, dgf_ref = refs[5:11]

            @pl.when(pl.program_id(0) == 0)
            def _():
                loss_ref[...] = jnp.zeros_like(loss_ref)
                dgf_ref[...] = jnp.zeros_like(dgf_ref)

        xv = x_ref[...]
        h = _rms_fwd(xv, g_ref[...])[0].astype(BF16)
        gate = _nt(h, wg_v[...])
        up = _nt(h, wu_v[...])
        gate_ref[...] = gate.astype(BF16)
        up_ref[...] = up.astype(BF16)
        act = (gate * _sigmoid(gate) * up).astype(BF16)
        xo = xv + _nn(act, wd_v[...])
        if head:
            gfv = gf_ref[...]
            y, xhat, r = _rms_fwd(xo, gfv)
            err = y - t_ref[...]
            loss_ref[...] += 0.5 * jnp.sum(_row_mean(err * err), axis=0, keepdims=True)
            dxr, dg_row = _rms_bwd(err * (1.0 / d), xhat, r, gfv)
            dgf_ref[...] += dg_row
            dx_ref[...] = dxr
            dxb_ref[...] = dxr.astype(BF16)
        else:
            refs[5][...] = xo

    act_specs = [_row_spec(tm, f), _row_spec(tm, f)]
    act_shapes = [jax.ShapeDtypeStruct((t_tok, f), BF16), jax.ShapeDtypeStruct((t_tok, f), BF16)]
    if head:
        out_specs = [_const_spec((1, 1)), _row_spec(tm, d), _row_spec(tm, d), _const_spec((1, d))]
        out_shape = [jax.ShapeDtypeStruct((1, 1), F32), jax.ShapeDtypeStruct((t_tok, d), F32),
                     jax.ShapeDtypeStruct((t_tok, d), BF16), jax.ShapeDtypeStruct((1, d), F32)]
    else:
        out_specs = [_row_spec(tm, d)]
        out_shape = [jax.ShapeDtypeStruct((t_tok, d), F32)]
    return _hosting_call(
        body, name, t_tok // tm, [x, g] + [arr for arr, _ in srcs] + list(head or ()),
        in_specs=[_row_spec(tm, d), _const_spec((1, d)), ANY, ANY, ANY] + [_row_spec(tm, d), _const_spec((1, d))][:n_head],
        out_specs=out_specs + act_specs, out_shape=out_shape + act_shapes,
        scratch=[pltpu.VMEM((f, d), BF16), pltpu.VMEM((f, d), BF16), pltpu.VMEM((f, d), BF16),
                 pltpu.SemaphoreType.DMA((3 * N_DEV,))],
        hosted=hosted)


def _ffn_bwd(dout, x, gate, up, g, srcs, nf, tm, name, hosted=()):
    t_tok, d = x.shape
    f = nf * N_DEV
    firsts = [first for _, first in srcs]
    per_chunk = -(-f // (FFN_CHUNKS * MXU_WIDTH)) * MXU_WIDTH
    bounds = [min(ck * per_chunk, f) for ck in range(FFN_CHUNKS + 1)]

    def body(dout_ref, x_ref, gate_ref, up_ref, g_ref, sg_ref, su_ref, sd_ref,
             dx_ref, dxb_ref, h_ref, act_ref, dgu_ref, dg_ref, wg_v, wu_v, wd_v, sems):
        _load_group(
            [(sg_ref, firsts[0], nf, wg_v), (su_ref, firsts[1], nf, wu_v), (sd_ref, firsts[2], nf, wd_v)], sems)

        @pl.when(pl.program_id(0) == 0)
        def _():
            dg_ref[...] = jnp.zeros_like(dg_ref)

        xv = x_ref[...]
        gv = g_ref[...]
        hv, xhat, r = _rms_fwd(xv, gv)
        h_ref[...] = hv.astype(BF16)
        dov = dout_ref[...]
        dob = dov.astype(BF16)
        dh = None
        for ck in range(FFN_CHUNKS):
            cols = slice(bounds[ck], bounds[ck + 1])
            gate_v = gate_ref[:, cols].astype(F32)
            up_v = up_ref[:, cols].astype(F32)
            sig = _sigmoid(gate_v)
            silu = gate_v * sig
            act_ref[:, cols] = (silu * up_v).astype(BF16)
            dact = _nt(dob, wd_v[cols, :])
            dup = (dact * silu).astype(BF16)
            dgate = (dact * up_v * (sig * (1.0 + gate_v * (1.0 - sig)))).astype(BF16)
            dgu_ref[:, cols] = dgate
            dgu_ref[:, f + bounds[ck]:f + bounds[ck + 1]] = dup
            part = _nn(dgate, wg_v[cols, :]) + _nn(dup, wu_v[cols, :])
            dh = part if dh is None else dh + part
        dxr, dg_row = _rms_bwd(dh, xhat, r, gv)
        dg_ref[...] += dg_row
        dx = dov + dxr
        dx_ref[...] = dx
        dxb_ref[...] = dx.astype(BF16)

    return _hosting_call(
        body, name, t_tok // tm, [dout, x, gate, up, g] + [arr for arr, _ in srcs],
        in_specs=[_row_spec(tm, d), _row_spec(tm, d), _row_spec(tm, f), _row_spec(tm, f), _const_spec((1, d)),
                  ANY, ANY, ANY],
        out_specs=[_row_spec(tm, d), _row_spec(tm, d), _row_spec(tm, d), _row_spec(tm, f), _row_spec(tm, 2 * f),
                   _const_spec((1, d))],
        out_shape=[jax.ShapeDtypeStruct((t_tok, d), F32), jax.ShapeDtypeStruct((t_tok, d), BF16),
                   jax.ShapeDtypeStruct((t_tok, d), BF16), jax.ShapeDtypeStruct((t_tok, f), BF16),
                   jax.ShapeDtypeStruct((t_tok, 2 * f), BF16), jax.ShapeDtypeStruct((1, d), F32)],
        scratch=[pltpu.VMEM((f, d), BF16), pltpu.VMEM((f, d), BF16), pltpu.VMEM((f, d), BF16),
                 pltpu.SemaphoreType.DMA((3 * N_DEV,))],
        hosted=hosted)


def _shift_down(z, k, prev_rows):
    row = lax.broadcasted_iota(jnp.int32, z.shape, 0)
    out = pltpu.roll(z, k, 0)
    for j in range(k):
        out = jnp.where(row == j, prev_rows[j], out)
    return out


def _shift_up(z, k, next_rows):
    tm = z.shape[0]
    row = lax.broadcasted_iota(jnp.int32, z.shape, 0)
    out = pltpu.roll(z, tm - k, 0)
    for j in range(k):
        out = jnp.where(row == tm - k + j, next_rows[j], out)
    return out


def _mixer_b_fwd(x, g, gath, conv_w, tm, seq, hosted=()):
    t_tok, d = x.shape
    e = conv_w.shape[1]
    e3 = 3 * e
    n_in, n_out = e3 // N_DEV, e // N_DEV
    tiles_per_seq = seq // tm

    def body(x_ref, g_ref, cw_ref, gath_ref, xo_ref, p_ref, win_v, wout_v, tail_v, sems):
        i = pl.program_id(0)
        _load_group([(gath_ref, 0, n_in, win_v), (gath_ref, n_in, n_out, wout_v)], sems)

        @pl.when(i % tiles_per_seq == 0)
        def _():
            tail_v[...] = jnp.zeros_like(tail_v)

        xv = x_ref[...]
        h = _rms_fwd(xv, g_ref[...])[0].astype(BF16)
        p = _nt(h, win_v[...])
        p_ref[...] = p.astype(BF16)
        z = p[:, e:2 * e] * p[:, 2 * e:]
        prev = [tail_v[SUBLANES - 2:SUBLANES - 1, :], tail_v[SUBLANES - 1:SUBLANES, :]]
        conv = (cw_ref[2:3, :] * z + cw_ref[1:2, :] * _shift_down(z, 1, prev[1:])
                + cw_ref[0:1, :] * _shift_down(z, 2, prev))
        tail_v[...] = z[tm - SUBLANES:, :]
        y = (p[:, :e] * conv).astype(BF16)
        xo_ref[...] = xv + _nn(y, wout_v[...])

    return _hosting_call(
        body, "mixer_b_fwd", t_tok // tm, [x, g, conv_w, gath],
        in_specs=[_row_spec(tm, d), _const_spec((1, d)), _const_spec((SUBLANES, e)), ANY],
        out_specs=[_row_spec(tm, d), _row_spec(tm, e3)],
        out_shape=[jax.ShapeDtypeStruct((t_tok, d), F32), jax.ShapeDtypeStruct((t_tok, e3), BF16)],
        scratch=[pltpu.VMEM((e3, d), BF16), pltpu.VMEM((e, d), BF16), pltpu.VMEM((SUBLANES, e), F32),
                 pltpu.SemaphoreType.DMA((2 * N_DEV,))],
        hosted=hosted)


def _mixer_b_bwd(dout, x, p, g, gath, conv_w, tm, seq, hosted=()):
    t_tok, d = x.shape
    e = conv_w.shape[1]
    e3 = 3 * e
    n_in, n_out = e3 // N_DEV, e // N_DEV
    tiles_per_seq = seq // tm
    halo_per_tile = tm // HALO
    n_halo = t_tok // HALO

    def body(dout_ref, dnext_ref, x_ref, p_ref, pprev_ref, pnext_ref, g_ref, cw_ref, gath_ref,
             dx_ref, dxb_ref, h_ref, y_ref, dp_ref, dg_ref, dcw_ref, win_v, wout_v, sems):
        i = pl.program_id(0)
        _load_group([(gath_ref, 0, n_in, win_v), (gath_ref, n_in, n_out, wout_v)], sems)

        @pl.when(i == 0)
        def _():
            dg_ref[...] = jnp.zeros_like(dg_ref)
            dcw_ref[...] = jnp.zeros_like(dcw_ref)

        first = (i % tiles_per_seq == 0).astype(F32)
        last = (i % tiles_per_seq == tiles_per_seq - 1).astype(F32)
        xv = x_ref[...]
        gv = g_ref[...]
        hv, xhat, r = _rms_fwd(xv, gv)
        h_ref[...] = hv.astype(BF16)
        pv = p_ref[...].astype(F32)
        bg, cg, hx = pv[:, :e], pv[:, e:2 * e], pv[:, 2 * e:]
        z = cg * hx
        pprev = pprev_ref[...].astype(F32)
        zprev = pprev[:, e:2 * e] * pprev[:, 2 * e:] * (1.0 - first)
        prev = [zprev[HALO - 2:HALO - 1, :], zprev[HALO - 1:HALO, :]]
        zs1 = _shift_down(z, 1, prev[1:])
        zs2 = _shift_down(z, 2, prev)
        w0, w1, w2 = cw_ref[0:1, :], cw_ref[1:2, :], cw_ref[2:3, :]
        conv = w2 * z + w1 * zs1 + w0 * zs2
        y_ref[...] = (bg * conv).astype(BF16)

        dov = dout_ref[...]
        wout_bf = wout_v[...]
        dy = _nt(dov.astype(BF16), wout_bf)
        dconv = dy * bg
        dnext = _nt(dnext_ref[...].astype(BF16), wout_bf) * pnext_ref[:, :e].astype(F32) * (1.0 - last)
        nxt = [dnext[0:1, :], dnext[1:2, :]]
        dz = w2 * dconv + w1 * _shift_up(dconv, 1, nxt[:1]) + w0 * _shift_up(dconv, 2, nxt)
        dcw_ref[0:1, :] += _col_sum(dconv * zs2)
        dcw_ref[1:2, :] += _col_sum(dconv * zs1)
        dcw_ref[2:3, :] += _col_sum(dconv * z)
        dp = jnp.concatenate([dy * conv, dz * hx, dz * cg], axis=1).astype(BF16)
        dp_ref[...] = dp
        dh = _nn(dp, win_v[...])
        dxr, dg_row = _rms_bwd(dh, xhat, r, gv)
        dg_ref[...] += dg_row
        dx = dov + dxr
        dx_ref[...] = dx
        dxb_ref[...] = dx.astype(BF16)

    prev_spec = lambda w: pl.BlockSpec((HALO, w), lambda i: (jnp.maximum(i * halo_per_tile - 1, 0), 0))
    next_spec = lambda w: pl.BlockSpec((HALO, w), lambda i: (jnp.minimum((i + 1) * halo_per_tile, n_halo - 1), 0))
    return _hosting_call(
        body, "mixer_b_bwd", t_tok // tm, [dout, dout, x, p, p, p, g, conv_w, gath],
        in_specs=[_row_spec(tm, d), next_spec(d), _row_spec(tm, d), _row_spec(tm, e3), prev_spec(e3), next_spec(e3),
                  _const_spec((1, d)), _const_spec((SUBLANES, e)), ANY],
        out_specs=[_row_spec(tm, d), _row_spec(tm, d), _row_spec(tm, d), _row_spec(tm, e), _row_spec(tm, e3),
                   _const_spec((1, d)), _const_spec((SUBLANES, e))],
        out_shape=[jax.ShapeDtypeStruct((t_tok, d), F32), jax.ShapeDtypeStruct((t_tok, d), BF16),
                   jax.ShapeDtypeStruct((t_tok, d), BF16), jax.ShapeDtypeStruct((t_tok, e), BF16),
                   jax.ShapeDtypeStruct((t_tok, e3), BF16), jax.ShapeDtypeStruct((1, d), F32),
                   jax.ShapeDtypeStruct((SUBLANES, e), F32)],
        scratch=[pltpu.VMEM((e3, d), BF16), pltpu.VMEM((e, d), BF16), pltpu.SemaphoreType.DMA((2 * N_DEV,))],
        hosted=hosted)


def _wgrad(a, b, bm, name, hosted=(), part=(1, 0)):
    t_tok, m = a.shape
    n = b.shape[1]
    every, first = part

    def body(a_ref, b_ref, o_ref):
        o_ref[...] = _tn(a_ref[...], b_ref[...]).astype(o_ref.dtype)

    outs, h_outs = _hosting_call(
        body, name, m // (bm * every), [a, b],
        in_specs=[pl.BlockSpec((t_tok, bm), lambda i: (0, every * i + first)), _const_spec((t_tok, n))],
        out_specs=[pl.BlockSpec((bm, n), lambda i: (i, 0))],
        out_shape=[jax.ShapeDtypeStruct((m // every, n), BF16)],
        scratch=[], hosted=hosted)
    return (outs[0], h_outs) if hosted else outs[0]


def _sum_slots(land, rb, name):
    n_slots, rows, cols = land.shape

    def body(l_ref, o_ref):
        acc = l_ref[0].astype(F32)
        for k in range(1, n_slots):
            acc = acc + l_ref[k].astype(F32)
        o_ref[...] = acc

    return pl.pallas_call(
        body, name=name, grid=(rows // rb,),
        in_specs=[pl.BlockSpec((n_slots, rb, cols), lambda i: (0, i, 0))],
        out_specs=pl.BlockSpec((rb, cols), lambda i: (i, 0)),
        out_shape=jax.ShapeDtypeStruct((rows, cols), F32),
        compiler_params=_params(sequential=False),
    )(land)


def _adamw(w, grad, m, v, rb, name):
    rows, cols = w.shape
    c1 = 1.0 / (1.0 - ADAM_B1 ** ADAM_STEP)
    c2 = 1.0 / (1.0 - ADAM_B2 ** ADAM_STEP)

    def body(w_ref, g_ref, m_ref, v_ref, d_ref, mo_ref, vo_ref):
        gv = g_ref[...]
        mn = ADAM_B1 * m_ref[...] + (1.0 - ADAM_B1) * gv
        vn = ADAM_B2 * v_ref[...] + (1.0 - ADAM_B2) * (gv * gv)
        mo_ref[...] = mn
        vo_ref[...] = vn
        d_ref[...] = -ADAM_LR * ((mn * c1) / (jnp.sqrt(vn * c2) + ADAM_EPS) + ADAM_WD * w_ref[...])

    spec = pl.BlockSpec((rb, cols), lambda i: (i, 0))
    shape = jax.ShapeDtypeStruct((rows, cols), F32)
    return pl.pallas_call(
        body, name=name, grid=(rows // rb,),
        in_specs=[spec] * 4, out_specs=[spec] * 3, out_shape=[shape] * 3,
        compiler_params=_params(sequential=False),
    )(w, grad, m, v)


def _pack_shards(groups, name, hosted=()):
    flat = [(part, layer) for group in groups for part, layer, _ in group]
    rows = [[p.shape[2] if turn else p.shape[1] for p, _, turn in group] for group in groups]
    first, _, first_turn = groups[0][0]
    width = first.shape[1] if first_turn else first.shape[2]

    def body(*refs):
        ins, outs = refs[:len(flat)], refs[len(flat):]
        k = 0
        for gi, group in enumerate(groups):
            off = 0
            for (_, _, turn), n in zip(group, rows[gi]):
                part = ins[k][...].astype(BF16)
                if turn:
                    r = lax.broadcasted_iota(jnp.int32, (n, n), 0)
                    c = lax.broadcasted_iota(jnp.int32, (n, n), 1)
                    part = _nt((r == c).astype(BF16), part).astype(BF16)
                outs[gi][off:off + n, :] = part
                off += n
                k += 1

    return _hosting_call(
        body, name, 1, [p for p, _ in flat],
        in_specs=[pl.BlockSpec((None,) + p.shape[1:], lambda i, layer=layer: (layer, 0, 0)) for p, layer in flat],
        out_specs=[_const_spec((sum(r), width)) for r in rows],
        out_shape=[jax.ShapeDtypeStruct((sum(r), width), BF16) for r in rows],
        scratch=[], hosted=hosted)


def _split_bf16(a):
    hi = a.astype(BF16)
    rest = a - hi.astype(F32)
    mid = rest.astype(BF16)
    return hi, mid, (rest - mid.astype(F32)).astype(BF16)


def _reduce_adamw(lands, w, m, v, transpose, name, hosted=()):
    n_layers, rows_w, cols_w = w.shape
    c1 = 1.0 / (1.0 - ADAM_B1 ** ADAM_STEP)
    c2 = 1.0 / (1.0 - ADAM_B2 ** ADAM_STEP)
    flat = [piece for pieces in lands for piece in pieces]
    counts = [len(pieces) for pieces in lands]
    if transpose:
        tiles = rows_w // MXU_WIDTH
        blk = (MXU_WIDTH, cols_w)
        land_specs = [pl.BlockSpec((N_CHIP, n, MXU_WIDTH), lambda i, b=first // n: (0, b, i % tiles))
                      for _, first, n in flat]
        for _, first, n in flat:
            assert first % n == 0
    else:
        tiles = 2
        blk = (rows_w // tiles, cols_w)
        assert all(c == 1 for c in counts)
        land_specs = [pl.BlockSpec((N_CHIP,) + blk, lambda i, b=first // blk[0]: (0, b + i % tiles, 0))
                      for _, first, _ in flat]
        for _, first, _ in flat:
            assert first % blk[0] == 0

    def body(*refs):
        land_refs = refs[:len(flat)]
        w_ref, m_ref, v_ref, g_ref, d_ref, mo_ref, vo_ref = refs[len(flat):]
        layer = pl.program_id(0) // tiles

        def total(ref):
            acc = ref[0].astype(F32)
            for q in range(1, N_CHIP):
                acc = acc + ref[q].astype(F32)
            return acc

        def layer_sum(k):
            first = sum(counts[:k])
            parts = [total(land_refs[first + j]) for j in range(counts[k])]
            return parts[0] if len(parts) == 1 else jnp.concatenate(parts, axis=0)

        gv = layer_sum(0)
        for k in range(1, n_layers):
            gv = jnp.where(layer == k, layer_sum(k), gv)
        if transpose:
            r = lax.broadcasted_iota(jnp.int32, (MXU_WIDTH, MXU_WIDTH), 0)
            c = lax.broadcasted_iota(jnp.int32, (MXU_WIDTH, MXU_WIDTH), 1)
            eye = (r == c).astype(BF16)
            hi, mid, lo = _split_bf16(gv)
            gv = _nt(eye, hi) + _nt(eye, mid) + _nt(eye, lo)
        g_ref[...] = gv
        mn = ADAM_B1 * m_ref[...] + (1.0 - ADAM_B1) * gv
        vn = ADAM_B2 * v_ref[...] + (1.0 - ADAM_B2) * (gv * gv)
        mo_ref[...] = mn
        vo_ref[...] = vn
        d_ref[...] = -ADAM_LR * ((mn * c1) / (jnp.sqrt(vn * c2) + ADAM_EPS) + ADAM_WD * w_ref[...])

    spec = pl.BlockSpec((None,) + blk, lambda i: (i // tiles, i % tiles, 0))
    shape = jax.ShapeDtypeStruct(w.shape, F32)
    outs, h_outs = _hosting_call(
        body, name, n_layers * tiles, [land for land, _, _ in flat] + [w, m, v],
        in_specs=land_specs + [spec] * 3, out_specs=[spec] * 4, out_shape=[shape] * 4, scratch=[], hosted=hosted)
    return (outs, h_outs) if hosted else outs


def _pack_small(parts, rows):
    flat = jnp.concatenate([p.reshape(-1).astype(F32) for p in parts])
    return jnp.pad(flat, (0, rows * LANES - flat.shape[0])).reshape(rows, LANES)


def _unpack_small(packed, shapes):
    flat = packed.reshape(-1)
    out = []
    pos = 0
    for s in shapes:
        n = math.prod(s)
        out.append(flat[pos:pos + n].reshape(s))
        pos += n
    return out


def kernel(x, mix_norm, ffn_norm, a_w_in, a_v_gain, a_v_bias, a_w_s, a_b_s, a_w_out, b_w_in, b_conv_w, b_w_out, ffn_w_gate, ffn_w_up, ffn_w_down, final_norm, loss_target, m_mix_norm, m_ffn_norm, m_a_w_in, m_a_v_gain, m_a_v_bias, m_a_w_s, m_a_b_s, m_a_w_out, m_b_w_in, m_b_conv_w, m_b_w_out, m_ffn_w_gate, m_ffn_w_up, m_ffn_w_down, m_final_norm, v_mix_norm, v_ffn_norm, v_a_w_in, v_a_v_gain, v_a_v_bias, v_a_w_s, v_a_b_s, v_a_w_out, v_b_w_in, v_b_conv_w, v_b_w_out, v_ffn_w_gate, v_ffn_w_up, v_ffn_w_down, v_final_norm):
    bsz, seq, d = x.shape
    t_tok = bsz * seq
    me = _my_index()
    xt = x.reshape(t_tok, d)
    target = loss_target.reshape(t_tok, d)
    e_a = a_v_gain.shape[1]
    e_b = b_w_out.shape[1] * N_DEV
    n_layers = ffn_w_gate.shape[0]
    f_shard = ffn_w_gate.shape[2]
    f_full = f_shard * N_DEV

    conv_pad = jnp.pad(b_conv_w[0], ((0, SUBLANES - CONV_W), (0, 0)))
    sh_a = jnp.concatenate([a_w_in[0].T, a_w_out[0]]).astype(BF16)
    bfull = jnp.repeat(a_b_s[0].T, GROUP, axis=1)

    gate_t, up_t = ffn_w_gate.transpose(0, 2, 1), ffn_w_up.transpose(0, 2, 1)
    (sh_b, sh_f0, sh_f1g, sh_f1ud), (gath_a, conv_g) = _pack_shards(
        [[(b_w_in, 0, True), (b_w_out, 0, False)],
         [(gate_t, 0, False), (up_t, 0, False), (ffn_w_down, 0, False)],
         [(gate_t, 1, False)],
         [(up_t, 1, False), (ffn_w_down, 1, False)]],
        "pack_shards", hosted=[_HostedGathers([sh_a, conv_pad])])
    conv_full = jnp.pad(conv_g[:, :CONV_W, :].transpose(1, 0, 2).reshape(CONV_W, e_b), ((0, SUBLANES - CONV_W), (0, 0)))
    (x1, gd_a, u_a, vhat_a, sv_a, y_a, rstd_a), (gath_f0,) = _mixer_a_fwd(
        xt, mix_norm[0:1], gath_a, a_v_gain, a_v_bias, a_w_s[0], bfull, tm=TOKEN_TILE,
        hosted=[_HostedGathers([sh_f0])])
    srcs0 = [(gath_f0, 0), (gath_f0, f_shard), (gath_f0, 2 * f_shard)]
    (x2, gate0, up0), (gath_b, gath_f1g) = _ffn_fwd(x1, ffn_norm[0:1], srcs0, f_shard, tm=TOKEN_TILE, name="ffn_fwd0",
                                                    hosted=[_HostedGathers([sh_b, sh_f1g])])
    (x3, p_b), (gath_f1ud,) = _mixer_b_fwd(x2, mix_norm[1:2], gath_b, conv_full, tm=TOKEN_TILE, seq=seq,
                                           hosted=[_HostedGathers([sh_f1ud])])
    srcs1 = [(gath_f1g, 0), (gath_f1ud, 0), (gath_f1ud, f_shard)]
    (loss_part, dx4, dx4_bf, d_final, gate1, up1), _ = _ffn_fwd(
        x3, ffn_norm[1:2], srcs1, f_shard, tm=TOKEN_TILE, name="ffn_fwd1", head=(target, final_norm.reshape(1, d)))

    ffn_entries = [(0, 0, f_shard), (0, f_full, f_shard), (1, 0, f_shard)]
    (dx3, dx3_bf, h_f1, act1, dgu1, d_fn1), _ = _ffn_bwd(dx4, x3, gate1, up1, ffn_norm[1:2], srcs1, f_shard, tm=TOKEN_TILE,
                                                         name="ffn_bwd1")
    g_down1 = _wgrad(act1, dx4_bf, WGRAD_ROWS, "wgrad_down1")
    g_gu1 = _wgrad(dgu1, h_f1, WGRAD_ROWS_WIDE, "wgrad_gate_up1")
    ps_f1 = _pair_reduce([g_gu1, g_down1], ffn_entries, "pair_reduce_f1")
    (dx2, dx2_bf, h_b, y_b, dp_b, d_mn1, d_conv), (land_f1gu,) = _mixer_b_bwd(
        dx3, x2, p_b, mix_norm[1:2], gath_b, conv_full, tm=TOKEN_TILE_WIDE, seq=seq,
        hosted=[_HostedChipScatter(ps_f1, 0, 2 * f_shard)])
    g_b_out = _wgrad(y_b, dx3_bf, WGRAD_ROWS, "wgrad_b_out")
    g_b_in = _wgrad(dp_b, h_b, WGRAD_ROWS_WIDE, "wgrad_b_in")
    ps_b = _pair_reduce([g_b_in, g_b_out], [(0, 0, b_w_in.shape[2]), (1, 0, b_w_out.shape[1])], "pair_reduce_b")
    (dx1, dx1_bf, h_f0, act0, dgu0, d_fn0), (land_f1d, land_b) = _ffn_bwd(
        dx2, x1, gate0, up0, ffn_norm[0:1], srcs0, f_shard, tm=TOKEN_TILE, name="ffn_bwd0",
        hosted=[_HostedChipScatter(ps_f1, 2 * f_shard, f_shard), _HostedChipScatter(ps_b)])
    g_down0 = _wgrad(act0, dx2_bf, WGRAD_ROWS, "wgrad_down0")
    g_gu0 = _wgrad(dgu0, h_f0, WGRAD_ROWS_WIDE, "wgrad_gate_up0")
    g_a_out = _wgrad(y_a, dx1_bf, WGRAD_ROWS, "wgrad_a_out")
    n_ao = a_w_out.shape[1]
    ps_f0ao = _pair_reduce([g_gu0, g_down0, g_a_out], ffn_entries + [(2, 0, n_ao)], "pair_reduce_f0_a_out")
    (dx0, _, h_a, dz_a, d_mn0, d_gain, d_bias, d_ws, d_bs_acc), (land_f0, land_ao) = _mixer_a_bwd(
        dx1, xt, gd_a, u_a, vhat_a, sv_a, rstd_a, mix_norm[0:1], gath_a, a_v_gain, a_v_bias, a_w_s[0], tm=TOKEN_TILE,
        hosted=[_HostedChipScatter(ps_f0ao, 0, 3 * f_shard), _HostedChipScatter(ps_f0ao, 3 * f_shard, n_ao)])
    d_bs = d_bs_acc.reshape(HEADS, CHUNK)

    small_grads = [jnp.concatenate([d_mn0, d_mn1]), jnp.concatenate([d_fn0, d_fn1]), d_gain, d_bias, d_ws, d_bs,
                   d_final, d_conv[:CONV_W], loss_part]
    small_shapes = [(n_layers, d), (n_layers, d), (1, e_a), (1, e_a), (1, HEADS, CHUNK, CHUNK), (1, HEADS, CHUNK), (d,),
                    (CONV_W, e_b), ()]
    n_small = sum(math.prod(s) for s in small_shapes)
    blk_rows = -(-n_small // (N_DEV * LANES * SUBLANES)) * SUBLANES
    small_rows = blk_rows * N_DEV
    packed = _pack_small(small_grads, small_rows)
    n_half = a_w_in.shape[2] // 2
    g_ai0, (small_land,) = _wgrad(dz_a, h_a, n_half, "wgrad_a_in0", hosted=[_HostedScatterAll(packed)], part=(2, 0))
    ps_ai0 = _pair_reduce([g_ai0], [(0, 0, n_half)], "pair_reduce_a_in0")
    small_sum = _sum_slots(small_land, blk_rows, "sum_small")
    g_ai1, (land_ai0,) = _wgrad(dz_a, h_a, n_half, "wgrad_a_in1", hosted=[_HostedChipScatter(ps_ai0)], part=(2, 1))
    ps_ai1 = _pair_reduce([g_ai1], [(0, 0, n_half)], "pair_reduce_a_in1")
    land_ai1, small_gath = _exchange([_HostedChipScatter(ps_ai1), _HostedGathers([small_sum])], "tail_exchange")
    small_all = small_gath.reshape(small_rows, LANES)

    n_b_in = b_w_in.shape[2]
    gate_out = _reduce_adamw([[(land_f0, 0, f_shard)], [(land_f1gu, 0, f_shard)]], gate_t,
                             m_ffn_w_gate.transpose(0, 2, 1), v_ffn_w_gate.transpose(0, 2, 1), False, "adamw_gate")
    up_out = _reduce_adamw([[(land_f0, f_shard, f_shard)], [(land_f1gu, f_shard, f_shard)]], up_t,
                           m_ffn_w_up.transpose(0, 2, 1), v_ffn_w_up.transpose(0, 2, 1), False, "adamw_up")
    res = {
        "a_w_in": _reduce_adamw([[(land_ai0, 0, n_half), (land_ai1, 0, n_half)]], a_w_in, m_a_w_in, v_a_w_in, True,
                                "adamw_a_in"),
        "a_w_out": _reduce_adamw([[(land_ao, 0, a_w_out.shape[1])]], a_w_out, m_a_w_out, v_a_w_out, False,
                                 "adamw_a_out"),
        "b_w_in": _reduce_adamw([[(land_b, 0, n_b_in)]], b_w_in, m_b_w_in, v_b_w_in, True, "adamw_b_in"),
        "b_w_out": _reduce_adamw([[(land_b, n_b_in, b_w_out.shape[1])]], b_w_out, m_b_w_out, v_b_w_out, False,
                                 "adamw_b_out"),
        "ffn_w_gate": [o.transpose(0, 2, 1) for o in gate_out],
        "ffn_w_up": [o.transpose(0, 2, 1) for o in up_out],
        "ffn_w_down": _reduce_adamw([[(land_f0, 2 * f_shard, f_shard)], [(land_f1d, 0, f_shard)]], ffn_w_down,
                                    m_ffn_w_down, v_ffn_w_down, False, "adamw_down"),
    }

    (gr_mix, gr_ffn, gr_gain, gr_bias, gr_ws, gr_bs, gr_final, gr_conv_full, loss) = _unpack_small(small_all, small_shapes)
    gr_conv = lax.dynamic_slice_in_dim(gr_conv_full, me * (e_b // N_DEV), e_b // N_DEV, axis=1)[None]

    small_w =[mix_norm, ffn_norm, a_v_gain, a_v_bias, a_w_s, a_b_s, final_norm]
    small_m = [m_mix_norm, m_ffn_norm, m_a_v_gain, m_a_v_bias, m_a_w_s, m_a_b_s, m_final_norm]
    small_v = [v_mix_norm, v_ffn_norm, v_a_v_gain, v_a_v_bias, v_a_w_s, v_a_b_s, v_final_norm]
    small_g = [gr_mix, gr_ffn, gr_gain, gr_bias, gr_ws, gr_bs, gr_final]
    sm_shapes = small_shapes[:len(small_w)]
    sm_out = _adamw(_pack_small(small_w, small_rows), _pack_small(small_g, small_rows), _pack_small(small_m, small_rows),
                    _pack_small(small_v, small_rows), small_rows, "adamw_small")
    sm_delta, sm_m, sm_v = [_unpack_small(o, sm_shapes) for o in sm_out]

    conv_out = _adamw(b_conv_w[0], gr_conv[0], m_b_conv_w[0], v_b_conv_w[0], CONV_W, "adamw_conv")
    conv_delta, conv_m, conv_v = [o[None] for o in conv_out]

    order = ["mix_norm", "ffn_norm", "a_w_in", "a_v_gain", "a_v_bias", "a_w_s", "a_b_s", "a_w_out", "b_w_in",
             "b_conv_w", "b_w_out", "ffn_w_gate", "ffn_w_up", "ffn_w_down", "final_norm"]
    small_names = ["mix_norm", "ffn_norm", "a_v_gain", "a_v_bias", "a_w_s", "a_b_s", "final_norm"]
    grads = {"b_conv_w": gr_conv}
    deltas, new_m, new_v = {}, {}, {}
    for k, name in enumerate(small_names):
        grads[name] = small_g[k]
        deltas[name], new_m[name], new_v[name] = sm_delta[k], sm_m[k], sm_v[k]
    deltas["b_conv_w"], new_m["b_conv_w"], new_v["b_conv_w"] = conv_delta, conv_m, conv_v
    for name, (gg, dl, mm, vv) in res.items():
        grads[name], deltas[name], new_m[name], new_v[name] = gg, dl, mm, vv

    grad_x = dx0.reshape(bsz, seq, d)
    return (loss, grad_x, *[grads[n] for n in order], *[deltas[n] for n in order],
            *[new_m[n] for n in order], *[new_v[n] for n in order])
```

```python
import math

import jax
import jax.numpy as jnp
from jax import lax
from jax.experimental import pallas as pl
from jax.experimental.pallas import tpu as pltpu

F32 = jnp.float32
BF16 = jnp.bfloat16

N_DEV = 8
N_CHIP = 4
CHUNK = 128
HEADS = 16
GROUP = 128
CONV_W = 3
NORM_EPS = 1e-6
GELU_C = math.sqrt(2.0 / math.pi)
GELU_K = 0.044715

ADAM_LR = 0.001
ADAM_B1 = 0.9
ADAM_B2 = 0.999
ADAM_EPS = 1e-08
ADAM_WD = 0.01
ADAM_STEP = 10

LANES = 128
SUBLANES = 8
VMEM_LIMIT = 60 * 1024 * 1024
HALO = 16
MXU_WIDTH = 256
FFN_CHUNKS = 2
TOKEN_TILE = 256
TOKEN_TILE_WIDE = 512
WGRAD_ROWS = 256
WGRAD_ROWS_WIDE = 512
GATHER_RELAY_AT = 0.56
GATHER_FORWARD_LEAD = 2

MESH = pl.DeviceIdType.MESH
ANY = pl.BlockSpec(memory_space=pl.ANY)

PEER_FLIPS = {"sibling": (0, 0, 1), "x": (1, 0, 0), "y": (0, 1, 0), "diagonal": (1, 1, 0),
              "x_other": (1, 0, 1), "y_other": (0, 1, 1), "diagonal_other": (1, 1, 1)}
COLLECTIVE_IDS = {frozenset(["sibling"]): 0,
                  frozenset(["sibling", "x", "y"]): 1,
                  frozenset(["sibling", "x", "y", "diagonal"]): 2,
                  frozenset(["x", "y", "diagonal"]): 3,
                  frozenset(PEER_FLIPS): 4}


def _params(sequential=True):
    return pltpu.CompilerParams(
        dimension_semantics=("arbitrary",) if sequential else None,
        vmem_limit_bytes=VMEM_LIMIT)


def _nn(a, b):
    return jnp.dot(a, b, preferred_element_type=F32)


def _nt(a, b):
    return lax.dot_general(a, b, (((1,), (1,)), ((), ())), preferred_element_type=F32)


def _tn(a, b):
    return lax.dot_general(a, b, (((0,), (0,)), ((), ())), preferred_element_type=F32)


def _row_mean(a):
    return jnp.mean(a, axis=-1, keepdims=True)


def _col_sum(a):
    return jnp.sum(a, axis=0, keepdims=True)


def _rms_fwd(x, g):
    r = lax.rsqrt(_row_mean(x * x) + NORM_EPS)
    xhat = x * r
    return xhat * g, xhat, r


def _rms_bwd(dh, xhat, r, g):
    a = dh * g
    dx = r * (a - xhat * _row_mean(a * xhat))
    return dx, _col_sum(dh * xhat)


def _gelu_and_grad(x):
    x2 = x * x
    t = jnp.tanh(x * (GELU_C + (GELU_C * GELU_K) * x2))
    half = 0.5 * t + 0.5
    d = half + x * (0.5 - 0.5 * (t * t)) * (GELU_C + (3.0 * GELU_C * GELU_K) * x2)
    return x * half, d


def _sigmoid(x):
    return 1.0 / (1.0 + jnp.exp(-x))


def _row_spec(tm, width):
    return pl.BlockSpec((tm, width), lambda i: (i, 0))


def _const_spec(shape):
    nd = len(shape)
    return pl.BlockSpec(shape, lambda i: (0,) * nd)


def _load_group(parts, sems):
    @pl.when(pl.program_id(0) == 0)
    def _():
        copies = []
        for k, (gath_ref, first, n, dst) in enumerate(parts):
            for j in range(N_DEV):
                copies.append(pltpu.make_async_copy(gath_ref.at[j, pl.ds(first, n), :], dst.at[pl.ds(j * n, n), :],
                                                    sems.at[k * N_DEV + j]))
        for cp in copies:
            cp.start()
        for cp in copies:
            cp.wait()


def _hosting_call(body, name, n_steps, arrays, in_specs, out_specs, out_shape, scratch, hosted=()):
    n_in, n_out, n_scr = len(arrays), len(out_shape), len(scratch)
    h_arrays = [a for h in hosted for a in h.arrays]
    h_shapes = [s for h in hosted for s in h.out_shapes]
    h_sems = [s for h in hosted for s in h.sem_shapes]
    peers = sorted(set().union(*[h.peers for h in hosted])) if hosted else []

    def handshake():
        @pl.when(pl.program_id(0) == 0)
        def _():
            x, y, c = lax.axis_index("x"), lax.axis_index("y"), lax.axis_index("c")
            barrier = pltpu.get_barrier_semaphore()
            for p in peers:
                fx, fy, fc = PEER_FLIPS[p]
                peer = (1 - x if fx else x, 1 - y if fy else y, 1 - c if fc else c)
                pl.semaphore_signal(barrier, inc=1, device_id=peer, device_id_type=MESH)
            pl.semaphore_wait(barrier, len(peers))

    def full_body(*refs):
        pos = 0
        groups = []
        for n in (n_in, len(h_arrays), n_out, len(h_shapes), n_scr, len(h_sems)):
            groups.append(refs[pos:pos + n])
            pos += n
        own_in, h_in, own_out, h_out, own_scr, h_sem = groups
        per_host = []
        pi = po = ps = 0
        for h in hosted:
            ni, no, ns = len(h.arrays), len(h.out_shapes), len(h.sem_shapes)
            per_host.append((h, h_in[pi:pi + ni], h_out[po:po + no], h_sem[ps:ps + ns]))
            pi, po, ps = pi + ni, po + no, ps + ns
        if hosted:
            handshake()
        for h, ins, outs, sems in per_host:
            h.begin(ins, outs, sems, n_steps)
        body(*own_in, *own_out, *own_scr)
        for h, ins, outs, sems in per_host:
            h.end(ins, outs, sems, n_steps)

    outs = pl.pallas_call(
        full_body, name=name, grid=(n_steps,),
        in_specs=list(in_specs) + [ANY] * len(h_arrays),
        out_specs=list(out_specs) + [ANY] * len(h_shapes),
        out_shape=list(out_shape) + h_shapes,
        scratch_shapes=list(scratch) + h_sems,
        compiler_params=pltpu.CompilerParams(
            dimension_semantics=("arbitrary",), vmem_limit_bytes=VMEM_LIMIT,
            collective_id=COLLECTIVE_IDS[frozenset(peers)] if hosted else None),
    )(*arrays, *h_arrays)
    return outs[:n_out], outs[n_out:]


def _my_index():
    return 4 * lax.axis_index("x") + 2 * lax.axis_index("y") + lax.axis_index("c")


GATHER_COPIES = 8


def _gather_relays(n_rows, dtype):
    rows_per_tile = SUBLANES * 4 // jnp.dtype(dtype).itemsize
    return n_rows % 2 == 0 and (n_rows // 2) % rows_per_tile == 0


class _Gather:
    def __init__(self, shard, out, send_sems, recv_sems, local_sem):
        self.shard, self.out = shard, out
        self.send_sems, self.recv_sems, self.local_sem = send_sems, recv_sems, local_sem
        x, y, c = lax.axis_index("x"), lax.axis_index("y"), lax.axis_index("c")
        self.c = c
        self.me, self.sibling = (x, y, c), (x, y, 1 - c)
        self.xn, self.yn, self.dg = (1 - x, y), (x, 1 - y), (1 - x, 1 - y)
        self.n = shard.shape[0]
        self.half = self.n // 2
        self.relays = _gather_relays(self.n, shard.dtype)

    def _slot(self, dev, lo=0, hi=None):
        hi = self.n if hi is None else hi
        return self.out.at[4 * dev[0] + 2 * dev[1] + dev[2], pl.ds(lo, hi - lo), :]

    def _copy(self, k, block, to, src=None, lo=0, hi=None):
        return pltpu.make_async_remote_copy(
            src_ref=self._slot(block, lo, hi) if src is None else src, dst_ref=self._slot(block, lo, hi),
            send_sem=self.send_sems.at[k], recv_sem=self.recv_sems.at[k], device_id=to, device_id_type=MESH)

    def _local(self):
        return pltpu.make_async_copy(self.shard, self._slot(self.me), self.local_sem)

    def start(self):
        c = self.c
        self._local().start()
        self._copy(0, self.me, self.sibling, src=self.shard).start()
        self._copy(1, self.me, (*self.xn, c), src=self.shard).start()
        self._copy(2, self.me, (*self.yn, c), src=self.shard).start()
        if not self.relays:
            self._copy(3, self.me, (*self.dg, c), src=self.shard).start()

    def relay(self):
        c = self.c
        if self.relays:
            self._copy(1, (*self.xn, c), self.me).wait_recv()
            self._copy(3, (*self.xn, c), (*self.yn, c), hi=self.half).start()
            self._copy(2, (*self.yn, c), self.me).wait_recv()
            self._copy(4, (*self.yn, c), (*self.xn, c), lo=self.half).start()

    def forward(self):
        c = self.c
        if self.relays:
            self._copy(5, (*self.xn, c), self.sibling).start()
            self._copy(6, (*self.yn, c), self.sibling).start()
            self._copy(3, (*self.dg, c), self.me, hi=self.half).wait_recv()
            self._copy(4, (*self.dg, c), self.me, lo=self.half).wait_recv()
        else:
            self._copy(1, (*self.xn, c), self.me).wait_recv()
            self._copy(5, (*self.xn, c), self.sibling).start()
            self._copy(2, (*self.yn, c), self.me).wait_recv()
            self._copy(6, (*self.yn, c), self.sibling).start()
            self._copy(3, (*self.dg, c), self.me).wait_recv()
        self._copy(7, (*self.dg, c), self.sibling).start()

    def finish(self):
        c = self.c
        self._copy(0, self.sibling, self.me).wait_recv()
        for k, chip in ((5, self.xn), (6, self.yn), (7, self.dg)):
            self._copy(k, (*chip, 1 - c), self.me).wait_recv()
        for k in (0, 1, 2, 5, 6, 7):
            self._copy(k, self.me, self.sibling).wait_send()
        if self.relays:
            self._copy(3, self.me, self.sibling, hi=self.half).wait_send()
            self._copy(4, self.me, self.sibling, lo=self.half).wait_send()
        else:
            self._copy(3, self.me, self.sibling).wait_send()
        self._local().wait()


class _HostedGathers:
    def __init__(self, shards, mid_lead=GATHER_FORWARD_LEAD, relay_at=GATHER_RELAY_AT):
        n = len(shards)
        self.arrays = shards
        self.mid_lead, self.relay_at = mid_lead, relay_at
        self.peers = {"sibling", "x", "y"}
        if not all(_gather_relays(s.shape[0], s.dtype) for s in shards):
            self.peers.add("diagonal")
        self.out_shapes = [jax.ShapeDtypeStruct((N_DEV,) + s.shape, s.dtype) for s in shards]
        self.sem_shapes = [pltpu.SemaphoreType.DMA((n, GATHER_COPIES)), pltpu.SemaphoreType.DMA((n, GATHER_COPIES)),
                           pltpu.SemaphoreType.DMA((n,))]

    def _gathers(self, ins, outs, sems):
        return [_Gather(ins[a], outs[a], sems[0].at[a], sems[1].at[a], sems[2].at[a]) for a in range(len(ins))]

    def begin(self, ins, outs, sems, n_steps):
        i = pl.program_id(0)
        forward_step = max(n_steps - 1 - self.mid_lead, 0)
        relay_step = min(int(self.relay_at * n_steps), forward_step)

        @pl.when(i == 0)
        def _():
            for g in self._gathers(ins, outs, sems):
                g.start()

        if n_steps == 1:
            return

        @pl.when(i == relay_step)
        def _():
            for g in self._gathers(ins, outs, sems):
                g.relay()

        @pl.when(i == forward_step)
        def _():
            for g in self._gathers(ins, outs, sems):
                g.forward()

    def end(self, ins, outs, sems, n_steps):
        @pl.when(pl.program_id(0) == n_steps - 1)
        def _():
            gathers = self._gathers(ins, outs, sems)
            if n_steps == 1:
                for g in gathers:
                    g.relay()
                for g in gathers:
                    g.forward()
            for g in gathers:
                g.finish()


def _exchange(hosted, name):
    return _hosting_call(lambda: None, name, 1, [], [], [], [], [], hosted=hosted)[1]


class _ChipScatter:
    def __init__(self, pairsum, row0, land, send_sems, recv_sems, local_sem):
        self.pairsum, self.row0, self.land = pairsum, row0, land
        self.send_sems, self.recv_sems, self.local_sem = send_sems, recv_sems, local_sem
        x, y, c = lax.axis_index("x"), lax.axis_index("y"), lax.axis_index("c")
        self.c = c
        self.chip = 2 * x + y
        self.others = [(1 - x, y), (x, 1 - y), (1 - x, 1 - y)]

    def _src(self, chip):
        return self.pairsum.at[chip, pl.ds(self.row0, self.land.shape[1]), :]

    def _copy(self, k):
        ox, oy = self.others[k]
        return pltpu.make_async_remote_copy(
            src_ref=self._src(2 * ox + oy), dst_ref=self.land.at[self.chip],
            send_sem=self.send_sems.at[k], recv_sem=self.recv_sems.at[k], device_id=(ox, oy, self.c),
            device_id_type=MESH)

    def _arrival(self, k):
        ox, oy = self.others[k]
        return pltpu.make_async_remote_copy(
            src_ref=self._src(self.chip), dst_ref=self.land.at[2 * ox + oy],
            send_sem=self.send_sems.at[k], recv_sem=self.recv_sems.at[k], device_id=(ox, oy, self.c),
            device_id_type=MESH)

    def _local(self):
        return pltpu.make_async_copy(self._src(self.chip), self.land.at[self.chip], self.local_sem)

    def start(self):
        self._local().start()
        for k in range(N_CHIP - 1):
            self._copy(k).start()

    def finish(self):
        for k in range(N_CHIP - 1):
            self._arrival(k).wait_recv()
        for k in range(N_CHIP - 1):
            self._copy(k).wait_send()
        self._local().wait()


class _HostedChipScatter:
    def __init__(self, pairsum, row0=0, n=None):
        n = pairsum.shape[1] - row0 if n is None else n
        self.row0 = row0
        self.peers = {"x", "y", "diagonal"}
        self.arrays = [pairsum]
        self.out_shapes = [jax.ShapeDtypeStruct((N_CHIP, n, pairsum.shape[2]), pairsum.dtype)]
        self.sem_shapes = [pltpu.SemaphoreType.DMA((N_CHIP - 1,)), pltpu.SemaphoreType.DMA((N_CHIP - 1,)),
                           pltpu.SemaphoreType.DMA(())]

    def begin(self, ins, outs, sems, n_steps):
        @pl.when(pl.program_id(0) == 0)
        def _():
            _ChipScatter(ins[0], self.row0, outs[0], *sems).start()

    def end(self, ins, outs, sems, n_steps):
        @pl.when(pl.program_id(0) == n_steps - 1)
        def _():
            _ChipScatter(ins[0], self.row0, outs[0], *sems).finish()


def _pair_reduce(arrays, entries, name):
    n_arr, n_ent = len(arrays), len(entries)
    cols = arrays[0].shape[1]
    offsets = []
    total = 0
    for _, _, n in entries:
        offsets.append(total)
        total += n

    def body(*refs):
        ins, out_ref = refs[:n_arr], refs[n_arr]
        rbuf, own, send_sems, recv_sems, own_sems = refs[n_arr + 1:]
        q = pl.program_id(0)
        x, y, c = lax.axis_index("x"), lax.axis_index("y"), lax.axis_index("c")

        def block(e, chip, core):
            ai, first, n = entries[e]
            return ins[ai].at[pl.ds(first + (2 * chip + core) * n, n), :]

        def to_sibling(e, chip):
            return pltpu.make_async_remote_copy(
                src_ref=block(e, chip, 1 - c), dst_ref=rbuf.at[chip, pl.ds(offsets[e], entries[e][2]), :],
                send_sem=send_sems.at[e, chip], recv_sem=recv_sems.at[e, chip], device_id=(x, y, 1 - c),
                device_id_type=MESH)

        def own_block(e, chip):
            return pltpu.make_async_copy(block(e, chip, c), own.at[chip, pl.ds(offsets[e], entries[e][2]), :],
                                         own_sems.at[e, chip])

        @pl.when(q == 0)
        def _():
            barrier = pltpu.get_barrier_semaphore()
            pl.semaphore_signal(barrier, inc=1, device_id=(x, y, 1 - c), device_id_type=MESH)
            pl.semaphore_wait(barrier, 1)
            for chip in range(N_CHIP):
                for e in range(n_ent):
                    to_sibling(e, chip).start()
            for chip in range(N_CHIP):
                for e in range(n_ent):
                    own_block(e, chip).start()

        for e in range(n_ent):
            own_block(e, q).wait()
            to_sibling(e, q).wait_recv()
        out_ref[...] = (own[q].astype(F32) + rbuf[q].astype(F32)).astype(out_ref.dtype)

        @pl.when(q == N_CHIP - 1)
        def _():
            for chip in range(N_CHIP):
                for e in range(n_ent):
                    to_sibling(e, chip).wait_send()

    return pl.pallas_call(
        body, name=name, grid=(N_CHIP,),
        in_specs=[ANY] * n_arr,
        out_specs=pl.BlockSpec((None, total, cols), lambda q: (q, 0, 0)),
        out_shape=jax.ShapeDtypeStruct((N_CHIP, total, cols), BF16),
        scratch_shapes=[pltpu.VMEM((N_CHIP, total, cols), BF16), pltpu.VMEM((N_CHIP, total, cols), BF16),
                        pltpu.SemaphoreType.DMA((n_ent, N_CHIP)), pltpu.SemaphoreType.DMA((n_ent, N_CHIP)),
                        pltpu.SemaphoreType.DMA((n_ent, N_CHIP))],
        compiler_params=pltpu.CompilerParams(dimension_semantics=("arbitrary",), vmem_limit_bytes=VMEM_LIMIT,
                                             collective_id=COLLECTIVE_IDS[frozenset(["sibling"])]),
    )(*arrays)


class _HostedScatterAll:
    def __init__(self, packed):
        n = packed.shape[0] // N_DEV
        self.n = n
        self.peers = set(PEER_FLIPS)
        self.arrays = [packed]
        self.out_shapes = [jax.ShapeDtypeStruct((N_DEV, n, packed.shape[1]), packed.dtype)]
        self.sem_shapes = [pltpu.SemaphoreType.DMA((N_DEV - 1,)), pltpu.SemaphoreType.DMA((N_DEV - 1,)),
                           pltpu.SemaphoreType.DMA(())]

    def _copies(self, ins, outs, sems, with_arrivals):
        src, land = ins[0], outs[0]
        send_sems, recv_sems, local_sem = sems
        me = _my_index()

        def block(p):
            return src.at[pl.ds(p * self.n, self.n), :]

        local = pltpu.make_async_copy(block(me), land.at[me], local_sem)
        sends, arrivals = [], []
        for k in range(1, N_DEV):
            p = (me + k) % N_DEV
            q = (me + N_DEV - k) % N_DEV
            sends.append(pltpu.make_async_remote_copy(
                src_ref=block(p), dst_ref=land.at[me], send_sem=send_sems.at[k - 1], recv_sem=recv_sems.at[k - 1],
                device_id=(p // 4, (p // 2) % 2, p % 2), device_id_type=MESH))
            if with_arrivals:
                arrivals.append(pltpu.make_async_remote_copy(
                    src_ref=block(me), dst_ref=land.at[q], send_sem=send_sems.at[k - 1], recv_sem=recv_sems.at[k - 1],
                    device_id=(q // 4, (q // 2) % 2, q % 2), device_id_type=MESH))
        return local, sends, arrivals

    def begin(self, ins, outs, sems, n_steps):
        @pl.when(pl.program_id(0) == 0)
        def _():
            local, sends, _ = self._copies(ins, outs, sems, with_arrivals=False)
            local.start()
            for cp in sends:
                cp.start()

    def end(self, ins, outs, sems, n_steps):
        @pl.when(pl.program_id(0) == n_steps - 1)
        def _():
            local, sends, arrivals = self._copies(ins, outs, sems, with_arrivals=True)
            for cp in arrivals:
                cp.wait_recv()
            for cp in sends:
                cp.wait_send()
            local.wait()


def _tril_weights(ws_ref):
    r = lax.broadcasted_iota(jnp.int32, (CHUNK, CHUNK), 0)
    c = lax.broadcasted_iota(jnp.int32, (CHUNK, CHUNK), 1)
    return [jnp.where(r >= c, ws_ref[h], 0.0).astype(BF16) for h in range(HEADS)]


def _sgu_stats(zpre, gain, bias):
    e = zpre.shape[1] // 2
    z, dz = _gelu_and_grad(zpre)
    u, v = z[:, :e], z[:, e:]
    vc = v - _row_mean(v)
    rstd = lax.rsqrt(_row_mean(vc * vc) + NORM_EPS)
    vhat = vc * rstd
    return u, vhat, rstd, vhat * gain + bias, dz


def _spatial_fwd(wt, vn_bf, bfull_ref, sv_ref, tm):
    for ci in range(tm // CHUNK):
        rows = slice(ci * CHUNK, (ci + 1) * CHUNK)
        for h in range(HEADS):
            cols = slice(h * GROUP, (h + 1) * GROUP)
            sv_ref[rows, cols] = _nn(wt[h], vn_bf[rows, cols]) + bfull_ref[:, cols]


def _mixer_a_fwd(x, g, gath, gain, bias, ws, bfull, tm, hosted=()):
    t_tok, d = x.shape
    e = gain.shape[1]
    e2 = 2 * e
    n_in, n_out = e2 // N_DEV, e // N_DEV

    def body(x_ref, g_ref, gain_ref, bias_ref, ws_ref, bfull_ref, gath_ref,
             xo_ref, gd_ref, u_ref, vhat_ref, svo_ref, y_ref, rstd_ref, win_v, wout_v, sv_v, sems):
        _load_group([(gath_ref, 0, n_in, win_v), (gath_ref, n_in, n_out, wout_v)], sems)
        xv = x_ref[...]
        h = _rms_fwd(xv, g_ref[...])[0].astype(BF16)
        zpre = _nt(h, win_v[...])
        u, vhat, rstd, vn, gelu_d = _sgu_stats(zpre, gain_ref[...], bias_ref[...])
        gd_ref[...] = gelu_d.astype(BF16)
        u_ref[...] = u.astype(BF16)
        vhat_ref[...] = vhat.astype(BF16)
        rstd_ref[...] = rstd
        _spatial_fwd(_tril_weights(ws_ref), vn.astype(BF16), bfull_ref, sv_v, tm)
        sv = sv_v[...]
        svo_ref[...] = sv.astype(BF16)
        y = (u * sv).astype(BF16)
        y_ref[...] = y
        xo_ref[...] = xv + _nn(y, wout_v[...])

    return _hosting_call(
        body, "mixer_a_fwd", t_tok // tm, [x, g, gain, bias, ws, bfull, gath],
        in_specs=[_row_spec(tm, d), _const_spec((1, d)), _const_spec((1, e)), _const_spec((1, e)),
                  _const_spec((HEADS, CHUNK, CHUNK)), _const_spec((CHUNK, e)), ANY],
        out_specs=[_row_spec(tm, d), _row_spec(tm, e2), _row_spec(tm, e), _row_spec(tm, e), _row_spec(tm, e),
                   _row_spec(tm, e), _row_spec(tm, 1)],
        out_shape=[jax.ShapeDtypeStruct((t_tok, d), F32), jax.ShapeDtypeStruct((t_tok, e2), BF16),
                   jax.ShapeDtypeStruct((t_tok, e), BF16), jax.ShapeDtypeStruct((t_tok, e), BF16),
                   jax.ShapeDtypeStruct((t_tok, e), BF16), jax.ShapeDtypeStruct((t_tok, e), BF16),
                   jax.ShapeDtypeStruct((t_tok, 1), F32)],
        scratch=[pltpu.VMEM((e2, d), BF16), pltpu.VMEM((e, d), BF16), pltpu.VMEM((tm, e), F32),
                 pltpu.SemaphoreType.DMA((2 * N_DEV,))],
        hosted=hosted)


def _mixer_a_bwd(dout, x, gd, u_sav, vhat_sav, sv_sav, rstd_sav, g, gath, gain, bias, ws, tm, hosted=()):
    t_tok, d = x.shape
    e = gain.shape[1]
    e2 = 2 * e
    n_in, n_out = e2 // N_DEV, e // N_DEV
    n_steps = t_tok // tm

    def body(dout_ref, x_ref, gd_ref, u_ref, vhat_ref, sv_ref, rstd_ref, g_ref, gain_ref, bias_ref, ws_ref, gath_ref,
             dx_ref, dxb_ref, h_ref, dz_ref, dg_ref, dgain_ref, dbias_ref, dws_ref, dbso_ref,
             win_v, wout_v, dvn_v, dbs_ref, sems):
        i = pl.program_id(0)
        _load_group([(gath_ref, 0, n_in, win_v), (gath_ref, n_in, n_out, wout_v)], sems)

        @pl.when(i == 0)
        def _():
            dg_ref[...] = jnp.zeros_like(dg_ref)
            dgain_ref[...] = jnp.zeros_like(dgain_ref)
            dbias_ref[...] = jnp.zeros_like(dbias_ref)
            dws_ref[...] = jnp.zeros_like(dws_ref)
            dbs_ref[...] = jnp.zeros_like(dbs_ref)

        xv = x_ref[...]
        gv = g_ref[...]
        hv, xhat, r = _rms_fwd(xv, gv)
        h_ref[...] = hv.astype(BF16)
        gain_v = gain_ref[...]
        vhat = vhat_ref[...].astype(F32)
        vn_bf = (vhat * gain_v + bias_ref[...]).astype(BF16)
        wt = _tril_weights(ws_ref)

        dov = dout_ref[...]
        dy = _nt(dov.astype(BF16), wout_v[...])
        du = dy * sv_ref[...].astype(F32)
        dsv = dy * u_ref[...].astype(F32)
        dsv_bf = dsv.astype(BF16)
        for ci in range(tm // CHUNK):
            rows = slice(ci * CHUNK, (ci + 1) * CHUNK)
            dbs_ref[...] += dsv[rows, :]
            for h in range(HEADS):
                cols = slice(h * GROUP, (h + 1) * GROUP)
                dvn_v[rows, cols] = _tn(wt[h], dsv_bf[rows, cols])
                dws_ref[h] += _nt(dsv_bf[rows, cols], vn_bf[rows, cols])
        dvn = dvn_v[...]
        dgain_ref[...] += _col_sum(dvn * vhat)
        dbias_ref[...] += _col_sum(dvn)
        dvhat = dvn * gain_v
        dv = rstd_ref[...] * (dvhat - _row_mean(dvhat) - vhat * _row_mean(dvhat * vhat))
        dzpre = (jnp.concatenate([du, dv], axis=1) * gd_ref[...].astype(F32)).astype(BF16)
        dz_ref[...] = dzpre
        dh = _nn(dzpre, win_v[...])
        dxr, dg_row = _rms_bwd(dh, xhat, r, gv)
        dg_ref[...] += dg_row
        dx = dov + dxr
        dx_ref[...] = dx
        dxb_ref[...] = dx.astype(BF16)

        @pl.when(i == n_steps - 1)
        def _():
            rr = lax.broadcasted_iota(jnp.int32, (CHUNK, CHUNK), 0)
            cc = lax.broadcasted_iota(jnp.int32, (CHUNK, CHUNK), 1)
            for h in range(HEADS):
                dws_ref[h] = jnp.where(rr >= cc, dws_ref[h], 0.0)
                dbso_ref[h] = jnp.sum(dbs_ref[:, h * GROUP:(h + 1) * GROUP], axis=1, keepdims=True)

    return _hosting_call(
        body, "mixer_a_bwd", n_steps, [dout, x, gd, u_sav, vhat_sav, sv_sav, rstd_sav, g, gain, bias, ws, gath],
        in_specs=[_row_spec(tm, d), _row_spec(tm, d), _row_spec(tm, e2), _row_spec(tm, e), _row_spec(tm, e),
                  _row_spec(tm, e), _row_spec(tm, 1), _const_spec((1, d)),
                  _const_spec((1, e)), _const_spec((1, e)), _const_spec((HEADS, CHUNK, CHUNK)), ANY],
        out_specs=[_row_spec(tm, d), _row_spec(tm, d), _row_spec(tm, d), _row_spec(tm, e2),
                   _const_spec((1, d)), _const_spec((1, e)), _const_spec((1, e)),
                   _const_spec((HEADS, CHUNK, CHUNK)), _const_spec((HEADS, CHUNK, 1))],
        out_shape=[jax.ShapeDtypeStruct((t_tok, d), F32), jax.ShapeDtypeStruct((t_tok, d), BF16),
                   jax.ShapeDtypeStruct((t_tok, d), BF16), jax.ShapeDtypeStruct((t_tok, e2), BF16),
                   jax.ShapeDtypeStruct((1, d), F32), jax.ShapeDtypeStruct((1, e), F32),
                   jax.ShapeDtypeStruct((1, e), F32), jax.ShapeDtypeStruct((HEADS, CHUNK, CHUNK), F32),
                   jax.ShapeDtypeStruct((HEADS, CHUNK, 1), F32)],
        scratch=[pltpu.VMEM((e2, d), BF16), pltpu.VMEM((e, d), BF16), pltpu.VMEM((tm, e), F32),
                 pltpu.VMEM((CHUNK, e), F32), pltpu.SemaphoreType.DMA((2 * N_DEV,))],
        hosted=hosted)


def _ffn_fwd(x, g, srcs, nf, tm, name, hosted=(), head=None):
    t_tok, d = x.shape
    f = nf * N_DEV
    firsts = [first for _, first in srcs]
    n_head = 2 if head else 0

    def body(*refs):
        x_ref, g_ref, sg_ref, su_ref, sd_ref = refs[:5]
        gate_ref, up_ref, wg_v, wu_v, wd_v, sems = refs[-6:]
        _load_group(
            [(sg_ref, firsts[0], nf, wg_v), (su_ref, firsts[1], nf, wu_v), (sd_ref, firsts[2], nf, wd_v)], sems)
        if head:
            t_ref, gf_ref, loss_ref, dx_ref, dxb_ref, dgf_ref = refs[5:11]

            @pl.when(pl.program_id(0) == 0)
            def _():
                loss_ref[...] = jnp.zeros_like(loss_ref)
                dgf_ref[...] = jnp.zeros_like(dgf_ref)

        xv = x_ref[...]
        h = _rms_fwd(xv, g_ref[...])[0].astype(BF16)
        gate = _nt(h, wg_v[...])
        up = _nt(h, wu_v[...])
        gate_ref[...] = gate.astype(BF16)
        up_ref[...] = up.astype(BF16)
        act = (gate * _sigmoid(gate) * up).astype(BF16)
        xo = xv + _nn(act, wd_v[...])
        if head:
            gfv = gf_ref[...]
            y, xhat, r = _rms_fwd(xo, gfv)
            err = y - t_ref[...]
            loss_ref[...] += 0.5 * jnp.sum(_row_mean(err * err), axis=0, keepdims=True)
            dxr, dg_row = _rms_bwd(err * (1.0 / d), xhat, r, gfv)
            dgf_ref[...] += dg_row
            dx_ref[...] = dxr
            dxb_ref[...] = dxr.astype(BF16)
        else:
            refs[5][...] = xo

    act_specs = [_row_spec(tm, f), _row_spec(tm, f)]
    act_shapes = [jax.ShapeDtypeStruct((t_tok, f), BF16), jax.ShapeDtypeStruct((t_tok, f), BF16)]
    if head:
        out_specs = [_const_spec((1, 1)), _row_spec(tm, d), _row_spec(tm, d), _const_spec((1, d))]
        out_shape = [jax.ShapeDtypeStruct((1, 1), F32), jax.ShapeDtypeStruct((t_tok, d), F32),
                     jax.ShapeDtypeStruct((t_tok, d), BF16), jax.ShapeDtypeStruct((1, d), F32)]
    else:
        out_specs = [_row_spec(tm, d)]
        out_shape = [jax.ShapeDtypeStruct((t_tok, d), F32)]
    return _hosting_call(
        body, name, t_tok // tm, [x, g] + [arr for arr, _ in srcs] + list(head or ()),
        in_specs=[_row_spec(tm, d), _const_spec((1, d)), ANY, ANY, ANY] + [_row_spec(tm, d), _const_spec((1, d))][:n_head],
        out_specs=out_specs + act_specs, out_shape=out_shape + act_shapes,
        scratch=[pltpu.VMEM((f, d), BF16), pltpu.VMEM((f, d), BF16), pltpu.VMEM((f, d), BF16),
                 pltpu.SemaphoreType.DMA((3 * N_DEV,))],
        hosted=hosted)


def _ffn_bwd(dout, x, gate, up, g, srcs, nf, tm, name, hosted=()):
    t_tok, d = x.shape
    f = nf * N_DEV
    firsts = [first for _, first in srcs]
    per_chunk = -(-f // (FFN_CHUNKS * MXU_WIDTH)) * MXU_WIDTH
    bounds = [min(ck * per_chunk, f) for ck in range(FFN_CHUNKS + 1)]

    def body(dout_ref, x_ref, gate_ref, up_ref, g_ref, sg_ref, su_ref, sd_ref,
             dx_ref, dxb_ref, h_ref, act_ref, dgu_ref, dg_ref, wg_v, wu_v, wd_v, sems):
        _load_group(
            [(sg_ref, firsts[0], nf, wg_v), (su_ref, firsts[1], nf, wu_v), (sd_ref, firsts[2], nf, wd_v)], sems)

        @pl.when(pl.program_id(0) == 0)
        def _():
            dg_ref[...] = jnp.zeros_like(dg_ref)

        xv = x_ref[...]
        gv = g_ref[...]
        hv, xhat, r = _rms_fwd(xv, gv)
        h_ref[...] = hv.astype(BF16)
        dov = dout_ref[...]
        dob = dov.astype(BF16)
        dh = None
        for ck in range(FFN_CHUNKS):
            cols = slice(bounds[ck], bounds[ck + 1])
            gate_v = gate_ref[:, cols].astype(F32)
            up_v = up_ref[:, cols].astype(F32)
            sig = _sigmoid(gate_v)
            silu = gate_v * sig
            act_ref[:, cols] = (silu * up_v).astype(BF16)
            dact = _nt(dob, wd_v[cols, :])
            dup = (dact * silu).astype(BF16)
            dgate = (dact * up_v * (sig * (1.0 + gate_v * (1.0 - sig)))).astype(BF16)
            dgu_ref[:, cols] = dgate
            dgu_ref[:, f + bounds[ck]:f + bounds[ck + 1]] = dup
            part = _nn(dgate, wg_v[cols, :]) + _nn(dup, wu_v[cols, :])
            dh = part if dh is None else dh + part
        dxr, dg_row = _rms_bwd(dh, xhat, r, gv)
        dg_ref[...] += dg_row
        dx = dov + dxr
        dx_ref[...] = dx
        dxb_ref[...] = dx.astype(BF16)

    return _hosting_call(
        body, name, t_tok // tm, [dout, x, gate, up, g] + [arr for arr, _ in srcs],
        in_specs=[_row_spec(tm, d), _row_spec(tm, d), _row_spec(tm, f), _row_spec(tm, f), _const_spec((1, d)),
                  ANY, ANY, ANY],
        out_specs=[_row_spec(tm, d), _row_spec(tm, d), _row_spec(tm, d), _row_spec(tm, f), _row_spec(tm, 2 * f),
                   _const_spec((1, d))],
        out_shape=[jax.ShapeDtypeStruct((t_tok, d), F32), jax.ShapeDtypeStruct((t_tok, d), BF16),
                   jax.ShapeDtypeStruct((t_tok, d), BF16), jax.ShapeDtypeStruct((t_tok, f), BF16),
                   jax.ShapeDtypeStruct((t_tok, 2 * f), BF16), jax.ShapeDtypeStruct((1, d), F32)],
        scratch=[pltpu.VMEM((f, d), BF16), pltpu.VMEM((f, d), BF16), pltpu.VMEM((f, d), BF16),
                 pltpu.SemaphoreType.DMA((3 * N_DEV,))],
        hosted=hosted)


def _shift_down(z, k, prev_rows):
    row = lax.broadcasted_iota(jnp.int32, z.shape, 0)
    out = pltpu.roll(z, k, 0)
    for j in range(k):
        out = jnp.where(row == j, prev_rows[j], out)
    return out


def _shift_up(z, k, next_rows):
    tm = z.shape[0]
    row = lax.broadcasted_iota(jnp.int32, z.shape, 0)
    out = pltpu.roll(z, tm - k, 0)
    for j in range(k):
        out = jnp.where(row == tm - k + j, next_rows[j], out)
    return out


def _mixer_b_fwd(x, g, gath, conv_w, tm, seq, hosted=()):
    t_tok, d = x.shape
    e = conv_w.shape[1]
    e3 = 3 * e
    n_in, n_out = e3 // N_DEV, e // N_DEV
    tiles_per_seq = seq // tm

    def body(x_ref, g_ref, cw_ref, gath_ref, xo_ref, p_ref, win_v, wout_v, tail_v, sems):
        i = pl.program_id(0)
        _load_group([(gath_ref, 0, n_in, win_v), (gath_ref, n_in, n_out, wout_v)], sems)

        @pl.when(i % tiles_per_seq == 0)
        def _():
            tail_v[...] = jnp.zeros_like(tail_v)

        xv = x_ref[...]
        h = _rms_fwd(xv, g_ref[...])[0].astype(BF16)
        p = _nt(h, win_v[...])
        p_ref[...] = p.astype(BF16)
        z = p[:, e:2 * e] * p[:, 2 * e:]
        prev = [tail_v[SUBLANES - 2:SUBLANES - 1, :], tail_v[SUBLANES - 1:SUBLANES, :]]
        conv = (cw_ref[2:3, :] * z + cw_ref[1:2, :] * _shift_down(z, 1, prev[1:])
                + cw_ref[0:1, :] * _shift_down(z, 2, prev))
        tail_v[...] = z[tm - SUBLANES:, :]
        y = (p[:, :e] * conv).astype(BF16)
        xo_ref[...] = xv + _nn(y, wout_v[...])

    return _hosting_call(
        body, "mixer_b_fwd", t_tok // tm, [x, g, conv_w, gath],
        in_specs=[_row_spec(tm, d), _const_spec((1, d)), _const_spec((SUBLANES, e)), ANY],
        out_specs=[_row_spec(tm, d), _row_spec(tm, e3)],
        out_shape=[jax.ShapeDtypeStruct((t_tok, d), F32), jax.ShapeDtypeStruct((t_tok, e3), BF16)],
        scratch=[pltpu.VMEM((e3, d), BF16), pltpu.VMEM((e, d), BF16), pltpu.VMEM((SUBLANES, e), F32),
                 pltpu.SemaphoreType.DMA((2 * N_DEV,))],
        hosted=hosted)


def _mixer_b_bwd(dout, x, p, g, gath, conv_w, tm, seq, hosted=()):
    t_tok, d = x.shape
    e = conv_w.shape[1]
    e3 = 3 * e
    n_in, n_out = e3 // N_DEV, e // N_DEV
    tiles_per_seq = seq // tm
    halo_per_tile = tm // HALO
    n_halo = t_tok // HALO

    def body(dout_ref, dnext_ref, x_ref, p_ref, pprev_ref, pnext_ref, g_ref, cw_ref, gath_ref,
             dx_ref, dxb_ref, h_ref, y_ref, dp_ref, dg_ref, dcw_ref, win_v, wout_v, sems):
        i = pl.program_id(0)
        _load_group([(gath_ref, 0, n_in, win_v), (gath_ref, n_in, n_out, wout_v)], sems)

        @pl.when(i == 0)
        def _():
            dg_ref[...] = jnp.zeros_like(dg_ref)
            dcw_ref[...] = jnp.zeros_like(dcw_ref)

        first = (i % tiles_per_seq == 0).astype(F32)
        last = (i % tiles_per_seq == tiles_per_seq - 1).astype(F32)
        xv = x_ref[...]
        gv = g_ref[...]
        hv, xhat, r = _rms_fwd(xv, gv)
        h_ref[...] = hv.astype(BF16)
        pv = p_ref[...].astype(F32)
        bg, cg, hx = pv[:, :e], pv[:, e:2 * e], pv[:, 2 * e:]
        z = cg * hx
        pprev = pprev_ref[...].astype(F32)
        zprev = pprev[:, e:2 * e] * pprev[:, 2 * e:] * (1.0 - first)
        prev = [zprev[HALO - 2:HALO - 1, :], zprev[HALO - 1:HALO, :]]
        zs1 = _shift_down(z, 1, prev[1:])
        zs2 = _shift_down(z, 2, prev)
        w0, w1, w2 = cw_ref[0:1, :], cw_ref[1:2, :], cw_ref[2:3, :]
        conv = w2 * z + w1 * zs1 + w0 * zs2
        y_ref[...] = (bg * conv).astype(BF16)

        dov = dout_ref[...]
        wout_bf = wout_v[...]
        dy = _nt(dov.astype(BF16), wout_bf)
        dconv = dy * bg
        dnext = _nt(dnext_ref[...].astype(BF16), wout_bf) * pnext_ref[:, :e].astype(F32) * (1.0 - last)
        nxt = [dnext[0:1, :], dnext[1:2, :]]
        dz = w2 * dconv + w1 * _shift_up(dconv, 1, nxt[:1]) + w0 * _shift_up(dconv, 2, nxt)
        dcw_ref[0:1, :] += _col_sum(dconv * zs2)
        dcw_ref[1:2, :] += _col_sum(dconv * zs1)
        dcw_ref[2:3, :] += _col_sum(dconv * z)
        dp = jnp.concatenate([dy * conv, dz * hx, dz * cg], axis=1).astype(BF16)
        dp_ref[...] = dp
        dh = _nn(dp, win_v[...])
        dxr, dg_row = _rms_bwd(dh, xhat, r, gv)
        dg_ref[...] += dg_row
        dx = dov + dxr
        dx_ref[...] = dx
        dxb_ref[...] = dx.astype(BF16)

    prev_spec = lambda w: pl.BlockSpec((HALO, w), lambda i: (jnp.maximum(i * halo_per_tile - 1, 0), 0))
    next_spec = lambda w: pl.BlockSpec((HALO, w), lambda i: (jnp.minimum((i + 1) * halo_per_tile, n_halo - 1), 0))
    return _hosting_call(
        body, "mixer_b_bwd", t_tok // tm, [dout, dout, x, p, p, p, g, conv_w, gath],
        in_specs=[_row_spec(tm, d), next_spec(d), _row_spec(tm, d), _row_spec(tm, e3), prev_spec(e3), next_spec(e3),
                  _const_spec((1, d)), _const_spec((SUBLANES, e)), ANY],
        out_specs=[_row_spec(tm, d), _row_spec(tm, d), _row_spec(tm, d), _row_spec(tm, e), _row_spec(tm, e3),
                   _const_spec((1, d)), _const_spec((SUBLANES, e))],
        out_shape=[jax.ShapeDtypeStruct((t_tok, d), F32), jax.ShapeDtypeStruct((t_tok, d), BF16),
                   jax.ShapeDtypeStruct((t_tok, d), BF16), jax.ShapeDtypeStruct((t_tok, e), BF16),
                   jax.ShapeDtypeStruct((t_tok, e3), BF16), jax.ShapeDtypeStruct((1, d), F32),
                   jax.ShapeDtypeStruct((SUBLANES, e), F32)],
        scratch=[pltpu.VMEM((e3, d), BF16), pltpu.VMEM((e, d), BF16), pltpu.SemaphoreType.DMA((2 * N_DEV,))],
        hosted=hosted)


def _wgrad(a, b, bm, name, hosted=(), part=(1, 0)):
    t_tok, m = a.shape
    n = b.shape[1]
    every, first = part

    def body(a_ref, b_ref, o_ref):
        o_ref[...] = _tn(a_ref[...], b_ref[...]).astype(o_ref.dtype)

    outs, h_outs = _hosting_call(
        body, name, m // (bm * every), [a, b],
        in_specs=[pl.BlockSpec((t_tok, bm), lambda i: (0, every * i + first)), _const_spec((t_tok, n))],
        out_specs=[pl.BlockSpec((bm, n), lambda i: (i, 0))],
        out_shape=[jax.ShapeDtypeStruct((m // every, n), BF16)],
        scratch=[], hosted=hosted)
    return (outs[0], h_outs) if hosted else outs[0]


def _sum_slots(land, rb, name):
    n_slots, rows, cols = land.shape

    def body(l_ref, o_ref):
        acc = l_ref[0].astype(F32)
        for k in range(1, n_slots):
            acc = acc + l_ref[k].astype(F32)
        o_ref[...] = acc

    return pl.pallas_call(
        body, name=name, grid=(rows // rb,),
        in_specs=[pl.BlockSpec((n_slots, rb, cols), lambda i: (0, i, 0))],
        out_specs=pl.BlockSpec((rb, cols), lambda i: (i, 0)),
        out_shape=jax.ShapeDtypeStruct((rows, cols), F32),
        compiler_params=_params(sequential=False),
    )(land)


def _adamw(w, grad, m, v, rb, name):
    rows, cols = w.shape
    c1 = 1.0 / (1.0 - ADAM_B1 ** ADAM_STEP)
    c2 = 1.0 / (1.0 - ADAM_B2 ** ADAM_STEP)

    def body(w_ref, g_ref, m_ref, v_ref, d_ref, mo_ref, vo_ref):
        gv = g_ref[...]
        mn = ADAM_B1 * m_ref[...] + (1.0 - ADAM_B1) * gv
        vn = ADAM_B2 * v_ref[...] + (1.0 - ADAM_B2) * (gv * gv)
        mo_ref[...] = mn
        vo_ref[...] = vn
        d_ref[...] = -ADAM_LR * ((mn * c1) / (jnp.sqrt(vn * c2) + ADAM_EPS) + ADAM_WD * w_ref[...])

    spec = pl.BlockSpec((rb, cols), lambda i: (i, 0))
    shape = jax.ShapeDtypeStruct((rows, cols), F32)
    return pl.pallas_call(
        body, name=name, grid=(rows // rb,),
        in_specs=[spec] * 4, out_specs=[spec] * 3, out_shape=[shape] * 3,
        compiler_params=_params(sequential=False),
    )(w, grad, m, v)


def _pack_shards(groups, name, hosted=()):
    flat = [(part, layer) for group in groups for part, layer, _ in group]
    rows = [[p.shape[2] if turn else p.shape[1] for p, _, turn in group] for group in groups]
    first, _, first_turn = groups[0][0]
    width = first.shape[1] if first_turn else first.shape[2]

    def body(*refs):
        ins, outs = refs[:len(flat)], refs[len(flat):]
        k = 0
        for gi, group in enumerate(groups):
            off = 0
            for (_, _, turn), n in zip(group, rows[gi]):
                part = ins[k][...].astype(BF16)
                if turn:
                    r = lax.broadcasted_iota(jnp.int32, (n, n), 0)
                    c = lax.broadcasted_iota(jnp.int32, (n, n), 1)
                    part = _nt((r == c).astype(BF16), part).astype(BF16)
                outs[gi][off:off + n, :] = part
                off += n
                k += 1

    return _hosting_call(
        body, name, 1, [p for p, _ in flat],
        in_specs=[pl.BlockSpec((None,) + p.shape[1:], lambda i, layer=layer: (layer, 0, 0)) for p, layer in flat],
        out_specs=[_const_spec((sum(r), width)) for r in rows],
        out_shape=[jax.ShapeDtypeStruct((sum(r), width), BF16) for r in rows],
        scratch=[], hosted=hosted)


def _split_bf16(a):
    hi = a.astype(BF16)
    rest = a - hi.astype(F32)
    mid = rest.astype(BF16)
    return hi, mid, (rest - mid.astype(F32)).astype(BF16)


def _reduce_adamw(lands, w, m, v, transpose, name, hosted=()):
    n_layers, rows_w, cols_w = w.shape
    c1 = 1.0 / (1.0 - ADAM_B1 ** ADAM_STEP)
    c2 = 1.0 / (1.0 - ADAM_B2 ** ADAM_STEP)
    flat = [piece for pieces in lands for piece in pieces]
    counts = [len(pieces) for pieces in lands]
    if transpose:
        tiles = rows_w // MXU_WIDTH
        blk = (MXU_WIDTH, cols_w)
        land_specs = [pl.BlockSpec((N_CHIP, n, MXU_WIDTH), lambda i, b=first // n: (0, b, i % tiles))
                      for _, first, n in flat]
        for _, first, n in flat:
            assert first % n == 0
    else:
        tiles = 2
        blk = (rows_w // tiles, cols_w)
        assert all(c == 1 for c in counts)
        land_specs = [pl.BlockSpec((N_CHIP,) + blk, lambda i, b=first // blk[0]: (0, b + i % tiles, 0))
                      for _, first, _ in flat]
        for _, first, _ in flat:
            assert first % blk[0] == 0

    def body(*refs):
        land_refs = refs[:len(flat)]
        w_ref, m_ref, v_ref, g_ref, d_ref, mo_ref, vo_ref = refs[len(flat):]
        layer = pl.program_id(0) // tiles

        def total(ref):
            acc = ref[0].astype(F32)
            for q in range(1, N_CHIP):
                acc = acc + ref[q].astype(F32)
            return acc

        def layer_sum(k):
            first = sum(counts[:k])
            parts = [total(land_refs[first + j]) for j in range(counts[k])]
            return parts[0] if len(parts) == 1 else jnp.concatenate(parts, axis=0)

        gv = layer_sum(0)
        for k in range(1, n_layers):
            gv = jnp.where(layer == k, layer_sum(k), gv)
        if transpose:
            r = lax.broadcasted_iota(jnp.int32, (MXU_WIDTH, MXU_WIDTH), 0)
            c = lax.broadcasted_iota(jnp.int32, (MXU_WIDTH, MXU_WIDTH), 1)
            eye = (r == c).astype(BF16)
            hi, mid, lo = _split_bf16(gv)
            gv = _nt(eye, hi) + _nt(eye, mid) + _nt(eye, lo)
        g_ref[...] = gv
        mn = ADAM_B1 * m_ref[...] + (1.0 - ADAM_B1) * gv
        vn = ADAM_B2 * v_ref[...] + (1.0 - ADAM_B2) * (gv * gv)
        mo_ref[...] = mn
        vo_ref[...] = vn
        d_ref[...] = -ADAM_LR * ((mn * c1) / (jnp.sqrt(vn * c2) + ADAM_EPS) + ADAM_WD * w_ref[...])

    spec = pl.BlockSpec((None,) + blk, lambda i: (i // tiles, i % tiles, 0))
    shape = jax.ShapeDtypeStruct(w.shape, F32)
    outs, h_outs = _hosting_call(
        body, name, n_layers * tiles, [land for land, _, _ in flat] + [w, m, v],
        in_specs=land_specs + [spec] * 3, out_specs=[spec] * 4, out_shape=[shape] * 4, scratch=[], hosted=hosted)
    return (outs, h_outs) if hosted else outs


def _pack_small(parts, rows):
    flat = jnp.concatenate([p.reshape(-1).astype(F32) for p in parts])
    return jnp.pad(flat, (0, rows * LANES - flat.shape[0])).reshape(rows, LANES)


def _unpack_small(packed, shapes):
    flat = packed.reshape(-1)
    out = []
    pos = 0
    for s in shapes:
        n = math.prod(s)
        out.append(flat[pos:pos + n].reshape(s))
        pos += n
    return out


def kernel(x, mix_norm, ffn_norm, a_w_in, a_v_gain, a_v_bias, a_w_s, a_b_s, a_w_out, b_w_in, b_conv_w, b_w_out, ffn_w_gate, ffn_w_up, ffn_w_down, final_norm, loss_target, m_mix_norm, m_ffn_norm, m_a_w_in, m_a_v_gain, m_a_v_bias, m_a_w_s, m_a_b_s, m_a_w_out, m_b_w_in, m_b_conv_w, m_b_w_out, m_ffn_w_gate, m_ffn_w_up, m_ffn_w_down, m_final_norm, v_mix_norm, v_ffn_norm, v_a_w_in, v_a_v_gain, v_a_v_bias, v_a_w_s, v_a_b_s, v_a_w_out, v_b_w_in, v_b_conv_w, v_b_w_out, v_ffn_w_gate, v_ffn_w_up, v_ffn_w_down, v_final_norm):
    bsz, seq, d = x.shape
    t_tok = bsz * seq
    me = _my_index()
    xt = x.reshape(t_tok, d)
    target = loss_target.reshape(t_tok, d)
    e_a = a_v_gain.shape[1]
    e_b = b_w_out.shape[1] * N_DEV
    n_layers = ffn_w_gate.shape[0]
    f_shard = ffn_w_gate.shape[2]
    f_full = f_shard * N_DEV

    conv_pad = jnp.pad(b_conv_w[0], ((0, SUBLANES - CONV_W), (0, 0)))
    sh_a = jnp.concatenate([a_w_in[0].T, a_w_out[0]]).astype(BF16)
    bfull = jnp.repeat(a_b_s[0].T, GROUP, axis=1)

    gate_t, up_t = ffn_w_gate.transpose(0, 2, 1), ffn_w_up.transpose(0, 2, 1)
    (sh_b, sh_f0, sh_f1g, sh_f1ud), (gath_a, conv_g) = _pack_shards(
        [[(b_w_in, 0, True), (b_w_out, 0, False)],
         [(gate_t, 0, False), (up_t, 0, False), (ffn_w_down, 0, False)],
         [(gate_t, 1, False)],
         [(up_t, 1, False), (ffn_w_down, 1, False)]],
        "pack_shards", hosted=[_HostedGathers([sh_a, conv_pad])])
    conv_full = jnp.pad(conv_g[:, :CONV_W, :].transpose(1, 0, 2).reshape(CONV_W, e_b), ((0, SUBLANES - CONV_W), (0, 0)))
    (x1, gd_a, u_a, vhat_a, sv_a, y_a, rstd_a), (gath_f0,) = _mixer_a_fwd(
        xt, mix_norm[0:1], gath_a, a_v_gain, a_v_bias, a_w_s[0], bfull, tm=TOKEN_TILE,
        hosted=[_HostedGathers([sh_f0])])
    srcs0 = [(gath_f0, 0), (gath_f0, f_shard), (gath_f0, 2 * f_shard)]
    (x2, gate0, up0), (gath_b, gath_f1g) = _ffn_fwd(x1, ffn_norm[0:1], srcs0, f_shard, tm=TOKEN_TILE, name="ffn_fwd0",
                                                    hosted=[_HostedGathers([sh_b, sh_f1g])])
    (x3, p_b), (gath_f1ud,) = _mixer_b_fwd(x2, mix_norm[1:2], gath_b, conv_full, tm=TOKEN_TILE, seq=seq,
                                           hosted=[_HostedGathers([sh_f1ud])])
    srcs1 = [(gath_f1g, 0), (gath_f1ud, 0), (gath_f1ud, f_shard)]
    (loss_part, dx4, dx4_bf, d_final, gate1, up1), _ = _ffn_fwd(
        x3, ffn_norm[1:2], srcs1, f_shard, tm=TOKEN_TILE, name="ffn_fwd1", head=(target, final_norm.reshape(1, d)))

    ffn_entries = [(0, 0, f_shard), (0, f_full, f_shard), (1, 0, f_shard)]
    (dx3, dx3_bf, h_f1, act1, dgu1, d_fn1), _ = _ffn_bwd(dx4, x3, gate1, up1, ffn_norm[1:2], srcs1, f_shard, tm=TOKEN_TILE,
                                                         name="ffn_bwd1")
    g_down1 = _wgrad(act1, dx4_bf, WGRAD_ROWS, "wgrad_down1")
    g_gu1 = _wgrad(dgu1, h_f1, WGRAD_ROWS_WIDE, "wgrad_gate_up1")
    ps_f1 = _pair_reduce([g_gu1, g_down1], ffn_entries, "pair_reduce_f1")
    (dx2, dx2_bf, h_b, y_b, dp_b, d_mn1, d_conv), (land_f1gu,) = _mixer_b_bwd(
        dx3, x2, p_b, mix_norm[1:2], gath_b, conv_full, tm=TOKEN_TILE_WIDE, seq=seq,
        hosted=[_HostedChipScatter(ps_f1, 0, 2 * f_shard)])
    g_b_out = _wgrad(y_b, dx3_bf, WGRAD_ROWS, "wgrad_b_out")
    g_b_in = _wgrad(dp_b, h_b, WGRAD_ROWS_WIDE, "wgrad_b_in")
    ps_b = _pair_reduce([g_b_in, g_b_out], [(0, 0, b_w_in.shape[2]), (1, 0, b_w_out.shape[1])], "pair_reduce_b")
    (dx1, dx1_bf, h_f0, act0, dgu0, d_fn0), (land_f1d, land_b) = _ffn_bwd(
        dx2, x1, gate0, up0, ffn_norm[0:1], srcs0, f_shard, tm=TOKEN_TILE, name="ffn_bwd0",
        hosted=[_HostedChipScatter(ps_f1, 2 * f_shard, f_shard), _HostedChipScatter(ps_b)])
    g_down0 = _wgrad(act0, dx2_bf, WGRAD_ROWS, "wgrad_down0")
    g_gu0 = _wgrad(dgu0, h_f0, WGRAD_ROWS_WIDE, "wgrad_gate_up0")
    g_a_out = _wgrad(y_a, dx1_bf, WGRAD_ROWS, "wgrad_a_out")
    n_ao = a_w_out.shape[1]
    ps_f0ao = _pair_reduce([g_gu0, g_down0, g_a_out], ffn_entries + [(2, 0, n_ao)], "pair_reduce_f0_a_out")
    (dx0, _, h_a, dz_a, d_mn0, d_gain, d_bias, d_ws, d_bs_acc), (land_f0, land_ao) = _mixer_a_bwd(
        dx1, xt, gd_a, u_a, vhat_a, sv_a, rstd_a, mix_norm[0:1], gath_a, a_v_gain, a_v_bias, a_w_s[0], tm=TOKEN_TILE,
        hosted=[_HostedChipScatter(ps_f0ao, 0, 3 * f_shard), _HostedChipScatter(ps_f0ao, 3 * f_shard, n_ao)])
    d_bs = d_bs_acc.reshape(HEADS, CHUNK)

    small_grads = [jnp.concatenate([d_mn0, d_mn1]), jnp.concatenate([d_fn0, d_fn1]), d_gain, d_bias, d_ws, d_bs,
                   d_final, d_conv[:CONV_W], loss_part]
    small_shapes = [(n_layers, d), (n_layers, d), (1, e_a), (1, e_a), (1, HEADS, CHUNK, CHUNK), (1, HEADS, CHUNK), (d,),
                    (CONV_W, e_b), ()]
    n_small = sum(math.prod(s) for s in small_shapes)
    blk_rows = -(-n_small // (N_DEV * LANES * SUBLANES)) * SUBLANES
    small_rows = blk_rows * N_DEV
    packed = _pack_small(small_grads, small_rows)
    n_half = a_w_in.shape[2] // 2
    g_ai0, (small_land,) = _wgrad(dz_a, h_a, n_half, "wgrad_a_in0", hosted=[_HostedScatterAll(packed)], part=(2, 0))
    ps_ai0 = _pair_reduce([g_ai0], [(0, 0, n_half)], "pair_reduce_a_in0")
    small_sum = _sum_slots(small_land, blk_rows, "sum_small")
    g_ai1, (land_ai0,) = _wgrad(dz_a, h_a, n_half, "wgrad_a_in1", hosted=[_HostedChipScatter(ps_ai0)], part=(2, 1))
    ps_ai1 = _pair_reduce([g_ai1], [(0, 0, n_half)], "pair_reduce_a_in1")
    land_ai1, small_gath = _exchange([_HostedChipScatter(ps_ai1), _HostedGathers([small_sum])], "tail_exchange")
    small_all = small_gath.reshape(small_rows, LANES)

    n_b_in = b_w_in.shape[2]
    gate_out = _reduce_adamw([[(land_f0, 0, f_shard)], [(land_f1gu, 0, f_shard)]], gate_t,
                             m_ffn_w_gate.transpose(0, 2, 1), v_ffn_w_gate.transpose(0, 2, 1), False, "adamw_gate")
    up_out = _reduce_adamw([[(land_f0, f_shard, f_shard)], [(land_f1gu, f_shard, f_shard)]], up_t,
                           m_ffn_w_up.transpose(0, 2, 1), v_ffn_w_up.transpose(0, 2, 1), False, "adamw_up")
    res = {
        "a_w_in": _reduce_adamw([[(land_ai0, 0, n_half), (land_ai1, 0, n_half)]], a_w_in, m_a_w_in, v_a_w_in, True,
                                "adamw_a_in"),
        "a_w_out": _reduce_adamw([[(land_ao, 0, a_w_out.shape[1])]], a_w_out, m_a_w_out, v_a_w_out, False,
                                 "adamw_a_out"),
        "b_w_in": _reduce_adamw([[(land_b, 0, n_b_in)]], b_w_in, m_b_w_in, v_b_w_in, True, "adamw_b_in"),
        "b_w_out": _reduce_adamw([[(land_b, n_b_in, b_w_out.shape[1])]], b_w_out, m_b_w_out, v_b_w_out, False,
                                 "adamw_b_out"),
        "ffn_w_gate": [o.transpose(0, 2, 1) for o in gate_out],
        "ffn_w_up": [o.transpose(0, 2, 1) for o in up_out],
        "ffn_w_down": _reduce_adamw([[(land_f0, 2 * f_shard, f_shard)], [(land_f1d, 0, f_shard)]], ffn_w_down,
                                    m_ffn_w_down, v_ffn_w_down, False, "adamw_down"),
    }

    (gr_mix, gr_ffn, gr_gain, gr_bias, gr_ws, gr_bs, gr_final, gr_conv_full, loss) = _unpack_small(small_all, small_shapes)
    gr_conv = lax.dynamic_slice_in_dim(gr_conv_full, me * (e_b // N_DEV), e_b // N_DEV, axis=1)[None]

    small_w =[mix_norm, ffn_norm, a_v_gain, a_v_bias, a_w_s, a_b_s, final_norm]
    small_m = [m_mix_norm, m_ffn_norm, m_a_v_gain, m_a_v_bias, m_a_w_s, m_a_b_s, m_final_norm]
    small_v = [v_mix_norm, v_ffn_norm, v_a_v_gain, v_a_v_bias, v_a_w_s, v_a_b_s, v_final_norm]
    small_g = [gr_mix, gr_ffn, gr_gain, gr_bias, gr_ws, gr_bs, gr_final]
    sm_shapes = small_shapes[:len(small_w)]
    sm_out = _adamw(_pack_small(small_w, small_rows), _pack_small(small_g, small_rows), _pack_small(small_m, small_rows),
                    _pack_small(small_v, small_rows), small_rows, "adamw_small")
    sm_delta, sm_m, sm_v = [_unpack_small(o, sm_shapes) for o in sm_out]

    conv_out = _adamw(b_conv_w[0], gr_conv[0], m_b_conv_w[0], v_b_conv_w[0], CONV_W, "adamw_conv")
    conv_delta, conv_m, conv_v = [o[None] for o in conv_out]

    order = ["mix_norm", "ffn_norm", "a_w_in", "a_v_gain", "a_v_bias", "a_w_s", "a_b_s", "a_w_out", "b_w_in",
             "b_conv_w", "b_w_out", "ffn_w_gate", "ffn_w_up", "ffn_w_down", "final_norm"]
    small_names = ["mix_norm", "ffn_norm", "a_v_gain", "a_v_bias", "a_w_s", "a_b_s", "final_norm"]
    grads = {"b_conv_w": gr_conv}
    deltas, new_m, new_v = {}, {}, {}
    for k, name in enumerate(small_names):
        grads[name] = small_g[k]
        deltas[name], new_m[name], new_v[name] = sm_delta[k], sm_m[k], sm_v[k]
    deltas["b_conv_w"], new_m["b_conv_w"], new_v["b_conv_w"] = conv_delta, conv_m, conv_v
    for name, (gg, dl, mm, vv) in res.items():
        grads[name], deltas[name], new_m[name], new_v[name] = gg, dl, mm, vv

    grad_x = dx0.reshape(bsz, seq, d)
    return (loss, grad_x, *[grads[n] for n in order], *[deltas[n] for n in order],
            *[new_m[n] for n in order], *[new_v[n] for n in order])
```

```python
import math

import jax
import jax.numpy as jnp
from jax import lax
from jax.experimental import pallas as pl
from jax.experimental.pallas import tpu as pltpu

F32 = jnp.float32
BF16 = jnp.bfloat16

N_DEV = 8
N_CHIP = 4
CHUNK = 128
HEADS = 16
GROUP = 128
CONV_W = 3
NORM_EPS = 1e-6
GELU_C = math.sqrt(2.0 / math.pi)
GELU_K = 0.044715

ADAM_LR = 0.001
ADAM_B1 = 0.9
ADAM_B2 = 0.999
ADAM_EPS = 1e-08
ADAM_WD = 0.01
ADAM_STEP = 10

LANES = 128
SUBLANES = 8
VMEM_LIMIT = 60 * 1024 * 1024
HALO = 16
MXU_WIDTH = 256
FFN_CHUNKS = 2
TOKEN_TILE = 256
TOKEN_TILE_WIDE = 512
WGRAD_ROWS = 256
WGRAD_ROWS_WIDE = 512
GATHER_RELAY_AT = 0.56
PAIR_ONE_STEP_ROWS = 512
GATHER_FORWARD_LEAD = 2

MESH = pl.DeviceIdType.MESH
ANY = pl.BlockSpec(memory_space=pl.ANY)

PEER_FLIPS = {"sibling": (0, 0, 1), "x": (1, 0, 0), "y": (0, 1, 0), "diagonal": (1, 1, 0),
              "x_other": (1, 0, 1), "y_other": (0, 1, 1), "diagonal_other": (1, 1, 1)}
COLLECTIVE_IDS = {frozenset(["sibling"]): 0,
                  frozenset(["sibling", "x", "y"]): 1,
                  frozenset(["sibling", "x", "y", "diagonal"]): 2,
                  frozenset(["x", "y", "diagonal"]): 3,
                  frozenset(PEER_FLIPS): 4}


def _params(sequential=True):
    return pltpu.CompilerParams(
        dimension_semantics=("arbitrary",) if sequential else None,
        vmem_limit_bytes=VMEM_LIMIT)


def _nn(a, b):
    return jnp.dot(a, b, preferred_element_type=F32)


def _nt(a, b):
    return lax.dot_general(a, b, (((1,), (1,)), ((), ())), preferred_element_type=F32)


def _tn(a, b):
    return lax.dot_general(a, b, (((0,), (0,)), ((), ())), preferred_element_type=F32)


def _row_mean(a):
    return jnp.mean(a, axis=-1, keepdims=True)


def _col_sum(a):
    return jnp.sum(a, axis=0, keepdims=True)


def _rms_fwd(x, g):
    r = lax.rsqrt(_row_mean(x * x) + NORM_EPS)
    xhat = x * r
    return xhat * g, xhat, r


def _rms_bwd(dh, xhat, r, g):
    a = dh * g
    dx = r * (a - xhat * _row_mean(a * xhat))
    return dx, _col_sum(dh * xhat)


def _gelu_and_grad(x):
    x2 = x * x
    t = jnp.tanh(x * (GELU_C + (GELU_C * GELU_K) * x2))
    half = 0.5 * t + 0.5
    d = half + x * (0.5 - 0.5 * (t * t)) * (GELU_C + (3.0 * GELU_C * GELU_K) * x2)
    return x * half, d


def _sigmoid(x):
    return 1.0 / (1.0 + jnp.exp(-x))


def _row_spec(tm, width):
    return pl.BlockSpec((tm, width), lambda i: (i, 0))


def _const_spec(shape):
    nd = len(shape)
    return pl.BlockSpec(shape, lambda i: (0,) * nd)


def _load_group(parts, sems):
    @pl.when(pl.program_id(0) == 0)
    def _():
        copies = []
        for k, (gath_ref, first, n, dst) in enumerate(parts):
            for j in range(N_DEV):
                copies.append(pltpu.make_async_copy(gath_ref.at[j, pl.ds(first, n), :], dst.at[pl.ds(j * n, n), :],
                                                    sems.at[k * N_DEV + j]))
        for cp in copies:
            cp.start()
        for cp in copies:
            cp.wait()


def _hosting_call(body, name, n_steps, arrays, in_specs, out_specs, out_shape, scratch, hosted=()):
    n_in, n_out, n_scr = len(arrays), len(out_shape), len(scratch)
    h_arrays = [a for h in hosted for a in h.arrays]
    h_shapes = [s for h in hosted for s in h.out_shapes]
    h_sems = [s for h in hosted for s in h.sem_shapes]
    peers = sorted(set().union(*[h.peers for h in hosted])) if hosted else []

    def handshake():
        @pl.when(pl.program_id(0) == 0)
        def _():
            x, y, c = lax.axis_index("x"), lax.axis_index("y"), lax.axis_index("c")
            barrier = pltpu.get_barrier_semaphore()
            for p in peers:
                fx, fy, fc = PEER_FLIPS[p]
                peer = (1 - x if fx else x, 1 - y if fy else y, 1 - c if fc else c)
                pl.semaphore_signal(barrier, inc=1, device_id=peer, device_id_type=MESH)
            pl.semaphore_wait(barrier, len(peers))

    def full_body(*refs):
        pos = 0
        groups = []
        for n in (n_in, len(h_arrays), n_out, len(h_shapes), n_scr, len(h_sems)):
            groups.append(refs[pos:pos + n])
            pos += n
        own_in, h_in, own_out, h_out, own_scr, h_sem = groups
        per_host = []
        pi = po = ps = 0
        for h in hosted:
            ni, no, ns = len(h.arrays), len(h.out_shapes), len(h.sem_shapes)
            per_host.append((h, h_in[pi:pi + ni], h_out[po:po + no], h_sem[ps:ps + ns]))
            pi, po, ps = pi + ni, po + no, ps + ns
        if hosted:
            handshake()
        for h, ins, outs, sems in per_host:
            h.begin(ins, outs, sems, n_steps)
        body(*own_in, *own_out, *own_scr)
        for h, ins, outs, sems in per_host:
            h.end(ins, outs, sems, n_steps)

    outs = pl.pallas_call(
        full_body, name=name, grid=(n_steps,),
        in_specs=list(in_specs) + [ANY] * len(h_arrays),
        out_specs=list(out_specs) + [ANY] * len(h_shapes),
        out_shape=list(out_shape) + h_shapes,
        scratch_shapes=list(scratch) + h_sems,
        compiler_params=pltpu.CompilerParams(
            dimension_semantics=("arbitrary",), vmem_limit_bytes=VMEM_LIMIT,
            collective_id=COLLECTIVE_IDS[frozenset(peers)] if hosted else None),
    )(*arrays, *h_arrays)
    return outs[:n_out], outs[n_out:]


def _my_index():
    return 4 * lax.axis_index("x") + 2 * lax.axis_index("y") + lax.axis_index("c")


GATHER_COPIES = 8


def _gather_relays(n_rows, dtype):
    rows_per_tile = SUBLANES * 4 // jnp.dtype(dtype).itemsize
    return n_rows % 2 == 0 and (n_rows // 2) % rows_per_tile == 0


class _Gather:
    def __init__(self, shard, out, send_sems, recv_sems, local_sem):
        self.shard, self.out = shard, out
        self.send_sems, self.recv_sems, self.local_sem = send_sems, recv_sems, local_sem
        x, y, c = lax.axis_index("x"), lax.axis_index("y"), lax.axis_index("c")
        self.c = c
        self.me, self.sibling = (x, y, c), (x, y, 1 - c)
        self.xn, self.yn, self.dg = (1 - x, y), (x, 1 - y), (1 - x, 1 - y)
        self.n = shard.shape[0]
        self.half = self.n // 2
        self.relays = _gather_relays(self.n, shard.dtype)

    def _slot(self, dev, lo=0, hi=None):
        hi = self.n if hi is None else hi
        return self.out.at[4 * dev[0] + 2 * dev[1] + dev[2], pl.ds(lo, hi - lo), :]

    def _copy(self, k, block, to, src=None, lo=0, hi=None):
        return pltpu.make_async_remote_copy(
            src_ref=self._slot(block, lo, hi) if src is None else src, dst_ref=self._slot(block, lo, hi),
            send_sem=self.send_sems.at[k], recv_sem=self.recv_sems.at[k], device_id=to, device_id_type=MESH)

    def _local(self):
        return pltpu.make_async_copy(self.shard, self._slot(self.me), self.local_sem)

    def start(self):
        c = self.c
        self._local().start()
        self._copy(0, self.me, self.sibling, src=self.shard).start()
        self._copy(1, self.me, (*self.xn, c), src=self.shard).start()
        self._copy(2, self.me, (*self.yn, c), src=self.shard).start()
        if not self.relays:
            self._copy(3, self.me, (*self.dg, c), src=self.shard).start()

    def relay(self):
        c = self.c
        if self.relays:
            self._copy(1, (*self.xn, c), self.me).wait_recv()
            self._copy(3, (*self.xn, c), (*self.yn, c), hi=self.half).start()
            self._copy(2, (*self.yn, c), self.me).wait_recv()
            self._copy(4, (*self.yn, c), (*self.xn, c), lo=self.half).start()

    def forward(self):
        c = self.c
        if self.relays:
            self._copy(5, (*self.xn, c), self.sibling).start()
            self._copy(6, (*self.yn, c), self.sibling).start()
            self._copy(3, (*self.dg, c), self.me, hi=self.half).wait_recv()
            self._copy(4, (*self.dg, c), self.me, lo=self.half).wait_recv()
        else:
            self._copy(1, (*self.xn, c), self.me).wait_recv()
            self._copy(5, (*self.xn, c), self.sibling).start()
            self._copy(2, (*self.yn, c), self.me).wait_recv()
            self._copy(6, (*self.yn, c), self.sibling).start()
            self._copy(3, (*self.dg, c), self.me).wait_recv()
        self._copy(7, (*self.dg, c), self.sibling).start()

    def finish(self):
        c = self.c
        self._copy(0, self.sibling, self.me).wait_recv()
        for k, chip in ((5, self.xn), (6, self.yn), (7, self.dg)):
            self._copy(k, (*chip, 1 - c), self.me).wait_recv()
        for k in (0, 1, 2, 5, 6, 7):
            self._copy(k, self.me, self.sibling).wait_send()
        if self.relays:
            self._copy(3, self.me, self.sibling, hi=self.half).wait_send()
            self._copy(4, self.me, self.sibling, lo=self.half).wait_send()
        else:
            self._copy(3, self.me, self.sibling).wait_send()
        self._local().wait()


class _HostedGathers:
    def __init__(self, shards, mid_lead=GATHER_FORWARD_LEAD, relay_at=GATHER_RELAY_AT):
        n = len(shards)
        self.arrays = shards
        self.mid_lead, self.relay_at = mid_lead, relay_at
        self.peers = {"sibling", "x", "y"}
        if not all(_gather_relays(s.shape[0], s.dtype) for s in shards):
            self.peers.add("diagonal")
        self.out_shapes = [jax.ShapeDtypeStruct((N_DEV,) + s.shape, s.dtype) for s in shards]
        self.sem_shapes = [pltpu.SemaphoreType.DMA((n, GATHER_COPIES)), pltpu.SemaphoreType.DMA((n, GATHER_COPIES)),
                           pltpu.SemaphoreType.DMA((n,))]

    def _gathers(self, ins, outs, sems):
        return [_Gather(ins[a], outs[a], sems[0].at[a], sems[1].at[a], sems[2].at[a]) for a in range(len(ins))]

    def begin(self, ins, outs, sems, n_steps):
        i = pl.program_id(0)
        forward_step = max(n_steps - 1 - self.mid_lead, 0)
        relay_step = min(int(self.relay_at * n_steps), forward_step)

        @pl.when(i == 0)
        def _():
            for g in self._gathers(ins, outs, sems):
                g.start()

        if n_steps == 1:
            return

        @pl.when(i == relay_step)
        def _():
            for g in self._gathers(ins, outs, sems):
                g.relay()

        @pl.when(i == forward_step)
        def _():
            for g in self._gathers(ins, outs, sems):
                g.forward()

    def end(self, ins, outs, sems, n_steps):
        @pl.when(pl.program_id(0) == n_steps - 1)
        def _():
            gathers = self._gathers(ins, outs, sems)
            if n_steps == 1:
                for g in gathers:
                    g.relay()
                for g in gathers:
                    g.forward()
            for g in gathers:
                g.finish()


def _exchange(hosted, name):
    return _hosting_call(lambda: None, name, 1, [], [], [], [], [], hosted=hosted)[1]


class _ChipScatter:
    def __init__(self, pairsum, row0, land, send_sems, recv_sems, local_sem):
        self.pairsum, self.row0, self.land = pairsum, row0, land
        self.send_sems, self.recv_sems, self.local_sem = send_sems, recv_sems, local_sem
        x, y, c = lax.axis_index("x"), lax.axis_index("y"), lax.axis_index("c")
        self.c = c
        self.chip = 2 * x + y
        self.others = [(1 - x, y), (x, 1 - y), (1 - x, 1 - y)]

    def _src(self, chip):
        return self.pairsum.at[chip, pl.ds(self.row0, self.land.shape[1]), :]

    def _copy(self, k):
        ox, oy = self.others[k]
        return pltpu.make_async_remote_copy(
            src_ref=self._src(2 * ox + oy), dst_ref=self.land.at[self.chip],
            send_sem=self.send_sems.at[k], recv_sem=self.recv_sems.at[k], device_id=(ox, oy, self.c),
            device_id_type=MESH)

    def _arrival(self, k):
        ox, oy = self.others[k]
        return pltpu.make_async_remote_copy(
            src_ref=self._src(self.chip), dst_ref=self.land.at[2 * ox + oy],
            send_sem=self.send_sems.at[k], recv_sem=self.recv_sems.at[k], device_id=(ox, oy, self.c),
            device_id_type=MESH)

    def _local(self):
        return pltpu.make_async_copy(self._src(self.chip), self.land.at[self.chip], self.local_sem)

    def start(self):
        self._local().start()
        for k in range(N_CHIP - 1):
            self._copy(k).start()

    def finish(self):
        for k in range(N_CHIP - 1):
            self._arrival(k).wait_recv()
        for k in range(N_CHIP - 1):
            self._copy(k).wait_send()
        self._local().wait()


class _HostedChipScatter:
    def __init__(self, pairsum, row0=0, n=None):
        n = pairsum.shape[1] - row0 if n is None else n
        self.row0 = row0
        self.peers = {"x", "y", "diagonal"}
        self.arrays = [pairsum]
        self.out_shapes = [jax.ShapeDtypeStruct((N_CHIP, n, pairsum.shape[2]), pairsum.dtype)]
        self.sem_shapes = [pltpu.SemaphoreType.DMA((N_CHIP - 1,)), pltpu.SemaphoreType.DMA((N_CHIP - 1,)),
                           pltpu.SemaphoreType.DMA(())]

    def begin(self, ins, outs, sems, n_steps):
        @pl.when(pl.program_id(0) == 0)
        def _():
            _ChipScatter(ins[0], self.row0, outs[0], *sems).start()

    def end(self, ins, outs, sems, n_steps):
        @pl.when(pl.program_id(0) == n_steps - 1)
        def _():
            _ChipScatter(ins[0], self.row0, outs[0], *sems).finish()


def _pair_reduce(arrays, entries, name):
    n_arr, n_ent = len(arrays), len(entries)
    cols = arrays[0].shape[1]
    offsets = []
    total = 0
    for _, _, n in entries:
        offsets.append(total)
        total += n
    steps = 1 if total <= PAIR_ONE_STEP_ROWS else N_CHIP
    per_step = N_CHIP // steps

    def body(*refs):
        ins, out_ref = refs[:n_arr], refs[n_arr]
        rbuf, own, send_sems, recv_sems, own_sems = refs[n_arr + 1:]
        q = pl.program_id(0)
        x, y, c = lax.axis_index("x"), lax.axis_index("y"), lax.axis_index("c")

        def block(e, chip, core):
            ai, first, n = entries[e]
            return ins[ai].at[pl.ds(first + (2 * chip + core) * n, n), :]

        def to_sibling(e, chip):
            return pltpu.make_async_remote_copy(
                src_ref=block(e, chip, 1 - c), dst_ref=rbuf.at[chip, pl.ds(offsets[e], entries[e][2]), :],
                send_sem=send_sems.at[e, chip], recv_sem=recv_sems.at[e, chip], device_id=(x, y, 1 - c),
                device_id_type=MESH)

        def own_block(e, chip):
            return pltpu.make_async_copy(block(e, chip, c), own.at[chip, pl.ds(offsets[e], entries[e][2]), :],
                                         own_sems.at[e, chip])

        @pl.when(q == 0)
        def _():
            barrier = pltpu.get_barrier_semaphore()
            pl.semaphore_signal(barrier, inc=1, device_id=(x, y, 1 - c), device_id_type=MESH)
            pl.semaphore_wait(barrier, 1)
            for chip in range(N_CHIP):
                for e in range(n_ent):
                    to_sibling(e, chip).start()
            for chip in range(N_CHIP):
                for e in range(n_ent):
                    own_block(e, chip).start()

        for j in range(per_step):
            chip = q * per_step + j
            for e in range(n_ent):
                own_block(e, chip).wait()
                to_sibling(e, chip).wait_recv()
            out_ref[j] = (own[chip].astype(F32) + rbuf[chip].astype(F32)).astype(out_ref.dtype)

        @pl.when(q == steps - 1)
        def _():
            for chip in range(N_CHIP):
                for e in range(n_ent):
                    to_sibling(e, chip).wait_send()

    return pl.pallas_call(
        body, name=name, grid=(steps,),
        in_specs=[ANY] * n_arr,
        out_specs=pl.BlockSpec((per_step, total, cols), lambda q: (q, 0, 0)),
        out_shape=jax.ShapeDtypeStruct((N_CHIP, total, cols), BF16),
        scratch_shapes=[pltpu.VMEM((N_CHIP, total, cols), BF16), pltpu.VMEM((N_CHIP, total, cols), BF16),
                        pltpu.SemaphoreType.DMA((n_ent, N_CHIP)), pltpu.SemaphoreType.DMA((n_ent, N_CHIP)),
                        pltpu.SemaphoreType.DMA((n_ent, N_CHIP))],
        compiler_params=pltpu.CompilerParams(dimension_semantics=("arbitrary",), vmem_limit_bytes=VMEM_LIMIT,
                                             collective_id=COLLECTIVE_IDS[frozenset(["sibling"])]),
    )(*arrays)


class _HostedScatterAll:
    def __init__(self, packed):
        n = packed.shape[0] // N_DEV
        self.n = n
        self.peers = set(PEER_FLIPS)
        self.arrays = [packed]
        self.out_shapes = [jax.ShapeDtypeStruct((N_DEV, n, packed.shape[1]), packed.dtype)]
        self.sem_shapes = [pltpu.SemaphoreType.DMA((N_DEV - 1,)), pltpu.SemaphoreType.DMA((N_DEV - 1,)),
                           pltpu.SemaphoreType.DMA(())]

    def _copies(self, ins, outs, sems, with_arrivals):
        src, land = ins[0], outs[0]
        send_sems, recv_sems, local_sem = sems
        me = _my_index()

        def block(p):
            return src.at[pl.ds(p * self.n, self.n), :]

        local = pltpu.make_async_copy(block(me), land.at[me], local_sem)
        sends, arrivals = [], []
        for k in range(1, N_DEV):
            p = (me + k) % N_DEV
            q = (me + N_DEV - k) % N_DEV
            sends.append(pltpu.make_async_remote_copy(
                src_ref=block(p), dst_ref=land.at[me], send_sem=send_sems.at[k - 1], recv_sem=recv_sems.at[k - 1],
                device_id=(p // 4, (p // 2) % 2, p % 2), device_id_type=MESH))
            if with_arrivals:
                arrivals.append(pltpu.make_async_remote_copy(
                    src_ref=block(me), dst_ref=land.at[q], send_sem=send_sems.at[k - 1], recv_sem=recv_sems.at[k - 1],
                    device_id=(q // 4, (q // 2) % 2, q % 2), device_id_type=MESH))
        return local, sends, arrivals

    def begin(self, ins, outs, sems, n_steps):
        @pl.when(pl.program_id(0) == 0)
        def _():
            local, sends, _ = self._copies(ins, outs, sems, with_arrivals=False)
            local.start()
            for cp in sends:
                cp.start()

    def end(self, ins, outs, sems, n_steps):
        @pl.when(pl.program_id(0) == n_steps - 1)
        def _():
            local, sends, arrivals = self._copies(ins, outs, sems, with_arrivals=True)
            for cp in arrivals:
                cp.wait_recv()
            for cp in sends:
                cp.wait_send()
            local.wait()


def _tril_weights(ws_ref):
    r = lax.broadcasted_iota(jnp.int32, (CHUNK, CHUNK), 0)
    c = lax.broadcasted_iota(jnp.int32, (CHUNK, CHUNK), 1)
    return [jnp.where(r >= c, ws_ref[h], 0.0).astype(BF16) for h in range(HEADS)]


def _sgu_stats(zpre, gain, bias):
    e = zpre.shape[1] // 2
    z, dz = _gelu_and_grad(zpre)
    u, v = z[:, :e], z[:, e:]
    vc = v - _row_mean(v)
    rstd = lax.rsqrt(_row_mean(vc * vc) + NORM_EPS)
    vhat = vc * rstd
    return u, vhat, rstd, vhat * gain + bias, dz


def _spatial_fwd(wt, vn_bf, bfull_ref, sv_ref, tm):
    for ci in range(tm // CHUNK):
        rows = slice(ci * CHUNK, (ci + 1) * CHUNK)
        for h in range(HEADS):
            cols = slice(h * GROUP, (h + 1) * GROUP)
            sv_ref[rows, cols] = _nn(wt[h], vn_bf[rows, cols]) + bfull_ref[:, cols]


def _mixer_a_fwd(x, g, gath, gain, bias, ws, bfull, tm, hosted=()):
    t_tok, d = x.shape
    e = gain.shape[1]
    e2 = 2 * e
    n_in, n_out = e2 // N_DEV, e // N_DEV

    def body(x_ref, g_ref, gain_ref, bias_ref, ws_ref, bfull_ref, gath_ref,
             xo_ref, gd_ref, u_ref, vhat_ref, svo_ref, y_ref, rstd_ref, win_v, wout_v, sv_v, sems):
        _load_group([(gath_ref, 0, n_in, win_v), (gath_ref, n_in, n_out, wout_v)], sems)
        xv = x_ref[...]
        h = _rms_fwd(xv, g_ref[...])[0].astype(BF16)
        zpre = _nt(h, win_v[...])
        u, vhat, rstd, vn, gelu_d = _sgu_stats(zpre, gain_ref[...], bias_ref[...])
        gd_ref[...] = gelu_d.astype(BF16)
        u_ref[...] = u.astype(BF16)
        vhat_ref[...] = vhat.astype(BF16)
        rstd_ref[...] = rstd
        _spatial_fwd(_tril_weights(ws_ref), vn.astype(BF16), bfull_ref, sv_v, tm)
        sv = sv_v[...]
        svo_ref[...] = sv.astype(BF16)
        y = (u * sv).astype(BF16)
        y_ref[...] = y
        xo_ref[...] = xv + _nn(y, wout_v[...])

    return _hosting_call(
        body, "mixer_a_fwd", t_tok // tm, [x, g, gain, bias, ws, bfull, gath],
        in_specs=[_row_spec(tm, d), _const_spec((1, d)), _const_spec((1, e)), _const_spec((1, e)),
                  _const_spec((HEADS, CHUNK, CHUNK)), _const_spec((CHUNK, e)), ANY],
        out_specs=[_row_spec(tm, d), _row_spec(tm, e2), _row_spec(tm, e), _row_spec(tm, e), _row_spec(tm, e),
                   _row_spec(tm, e), _row_spec(tm, 1)],
        out_shape=[jax.ShapeDtypeStruct((t_tok, d), F32), jax.ShapeDtypeStruct((t_tok, e2), BF16),
                   jax.ShapeDtypeStruct((t_tok, e), BF16), jax.ShapeDtypeStruct((t_tok, e), BF16),
                   jax.ShapeDtypeStruct((t_tok, e), BF16), jax.ShapeDtypeStruct((t_tok, e), BF16),
                   jax.ShapeDtypeStruct((t_tok, 1), F32)],
        scratch=[pltpu.VMEM((e2, d), BF16), pltpu.VMEM((e, d), BF16), pltpu.VMEM((tm, e), F32),
                 pltpu.SemaphoreType.DMA((2 * N_DEV,))],
        hosted=hosted)


def _mixer_a_bwd(dout, x, gd, u_sav, vhat_sav, sv_sav, rstd_sav, g, gath, gain, bias, ws, tm, hosted=()):
    t_tok, d = x.shape
    e = gain.shape[1]
    e2 = 2 * e
    n_in, n_out = e2 // N_DEV, e // N_DEV
    n_steps = t_tok // tm

    def body(dout_ref, x_ref, gd_ref, u_ref, vhat_ref, sv_ref, rstd_ref, g_ref, gain_ref, bias_ref, ws_ref, gath_ref,
             dx_ref, dxb_ref, h_ref, dz_ref, dg_ref, dgain_ref, dbias_ref, dws_ref, dbso_ref,
             win_v, wout_v, dvn_v, dbs_ref, sems):
        i = pl.program_id(0)
        _load_group([(gath_ref, 0, n_in, win_v), (gath_ref, n_in, n_out, wout_v)], sems)

        @pl.when(i == 0)
        def _():
            dg_ref[...] = jnp.zeros_like(dg_ref)
            dgain_ref[...] = jnp.zeros_like(dgain_ref)
            dbias_ref[...] = jnp.zeros_like(dbias_ref)
            dws_ref[...] = jnp.zeros_like(dws_ref)
            dbs_ref[...] = jnp.zeros_like(dbs_ref)

        xv = x_ref[...]
        gv = g_ref[...]
        hv, xhat, r = _rms_fwd(xv, gv)
        h_ref[...] = hv.astype(BF16)
        gain_v = gain_ref[...]
        vhat = vhat_ref[...].astype(F32)
        vn_bf = (vhat * gain_v + bias_ref[...]).astype(BF16)
        wt = _tril_weights(ws_ref)

        dov = dout_ref[...]
        dy = _nt(dov.astype(BF16), wout_v[...])
        du = dy * sv_ref[...].astype(F32)
        dsv = dy * u_ref[...].astype(F32)
        dsv_bf = dsv.astype(BF16)
        for ci in range(tm // CHUNK):
            rows = slice(ci * CHUNK, (ci + 1) * CHUNK)
            dbs_ref[...] += dsv[rows, :]
            for h in range(HEADS):
                cols = slice(h * GROUP, (h + 1) * GROUP)
                dvn_v[rows, cols] = _tn(wt[h], dsv_bf[rows, cols])
                dws_ref[h] += _nt(dsv_bf[rows, cols], vn_bf[rows, cols])
        dvn = dvn_v[...]
        dgain_ref[...] += _col_sum(dvn * vhat)
        dbias_ref[...] += _col_sum(dvn)
        dvhat = dvn * gain_v
        dv = rstd_ref[...] * (dvhat - _row_mean(dvhat) - vhat * _row_mean(dvhat * vhat))
        dzpre = (jnp.concatenate([du, dv], axis=1) * gd_ref[...].astype(F32)).astype(BF16)
        dz_ref[...] = dzpre
        dh = _nn(dzpre, win_v[...])
        dxr, dg_row = _rms_bwd(dh, xhat, r, gv)
        dg_ref[...] += dg_row
        dx = dov + dxr
        dx_ref[...] = dx
        dxb_ref[...] = dx.astype(BF16)

        @pl.when(i == n_steps - 1)
        def _():
            rr = lax.broadcasted_iota(jnp.int32, (CHUNK, CHUNK), 0)
            cc = lax.broadcasted_iota(jnp.int32, (CHUNK, CHUNK), 1)
            for h in range(HEADS):
                dws_ref[h] = jnp.where(rr >= cc, dws_ref[h], 0.0)
                dbso_ref[h] = jnp.sum(dbs_ref[:, h * GROUP:(h + 1) * GROUP], axis=1, keepdims=True)

    return _hosting_call(
        body, "mixer_a_bwd", n_steps, [dout, x, gd, u_sav, vhat_sav, sv_sav, rstd_sav, g, gain, bias, ws, gath],
        in_specs=[_row_spec(tm, d), _row_spec(tm, d), _row_spec(tm, e2), _row_spec(tm, e), _row_spec(tm, e),
                  _row_spec(tm, e), _row_spec(tm, 1), _const_spec((1, d)),
                  _const_spec((1, e)), _const_spec((1, e)), _const_spec((HEADS, CHUNK, CHUNK)), ANY],
        out_specs=[_row_spec(tm, d), _row_spec(tm, d), _row_spec(tm, d), _row_spec(tm, e2),
                   _const_spec((1, d)), _const_spec((1, e)), _const_spec((1, e)),
                   _const_spec((HEADS, CHUNK, CHUNK)), _const_spec((HEADS, CHUNK, 1))],
        out_shape=[jax.ShapeDtypeStruct((t_tok, d), F32), jax.ShapeDtypeStruct((t_tok, d), BF16),
                   jax.ShapeDtypeStruct((t_tok, d), BF16), jax.ShapeDtypeStruct((t_tok, e2), BF16),
                   jax.ShapeDtypeStruct((1, d), F32), jax.ShapeDtypeStruct((1, e), F32),
                   jax.ShapeDtypeStruct((1, e), F32), jax.ShapeDtypeStruct((HEADS, CHUNK, CHUNK), F32),
                   jax.ShapeDtypeStruct((HEADS, CHUNK, 1), F32)],
        scratch=[pltpu.VMEM((e2, d), BF16), pltpu.VMEM((e, d), BF16), pltpu.VMEM((tm, e), F32),
                 pltpu.VMEM((CHUNK, e), F32), pltpu.SemaphoreType.DMA((2 * N_DEV,))],
        hosted=hosted)


def _ffn_fwd(x, g, srcs, nf, tm, name, hosted=(), head=None):
    t_tok, d = x.shape
    f = nf * N_DEV
    firsts = [first for _, first in srcs]
    n_head = 2 if head else 0

    def body(*refs):
        x_ref, g_ref, sg_ref, su_ref, sd_ref = refs[:5]
        gate_ref, up_ref, wg_v, wu_v, wd_v, sems = refs[-6:]
        _load_group(
            [(sg_ref, firsts[0], nf, wg_v), (su_ref, firsts[1], nf, wu_v), (sd_ref, firsts[2], nf, wd_v)], sems)
        if head:
            t_ref, gf_ref, loss_ref, dx_ref, dxb_ref, dgf_ref = refs[5:11]

            @pl.when(pl.program_id(0) == 0)
            def _():
                loss_ref[...] = jnp.zeros_like(loss_ref)
                dgf_ref[...] = jnp.zeros_like(dgf_ref)

        xv = x_ref[...]
        h = _rms_fwd(xv, g_ref[...])[0].astype(BF16)
        gate = _nt(h, wg_v[...])
        up = _nt(h, wu_v[...])
        gate_ref[...] = gate.astype(BF16)
        up_ref[...] = up.astype(BF16)
        act = (gate * _sigmoid(gate) * up).astype(BF16)
        xo = xv + _nn(act, wd_v[...])
        if head:
            gfv = gf_ref[...]
            y, xhat, r = _rms_fwd(xo, gfv)
            err = y - t_ref[...]
            loss_ref[...] += 0.5 * jnp.sum(_row_mean(err * err), axis=0, keepdims=True)
            dxr, dg_row = _rms_bwd(err * (1.0 / d), xhat, r, gfv)
            dgf_ref[...] += dg_row
            dx_ref[...] = dxr
            dxb_ref[...] = dxr.astype(BF16)
        else:
            refs[5][...] = xo

    act_specs = [_row_spec(tm, f), _row_spec(tm, f)]
    act_shapes = [jax.ShapeDtypeStruct((t_tok, f), BF16), jax.ShapeDtypeStruct((t_tok, f), BF16)]
    if head:
        out_specs = [_const_spec((1, 1)), _row_spec(tm, d), _row_spec(tm, d), _const_spec((1, d))]
        out_shape = [jax.ShapeDtypeStruct((1, 1), F32), jax.ShapeDtypeStruct((t_tok, d), F32),
                     jax.ShapeDtypeStruct((t_tok, d), BF16), jax.ShapeDtypeStruct((1, d), F32)]
    else:
        out_specs = [_row_spec(tm, d)]
        out_shape = [jax.ShapeDtypeStruct((t_tok, d), F32)]
    return _hosting_call(
        body, name, t_tok // tm, [x, g] + [arr for arr, _ in srcs] + list(head or ()),
        in_specs=[_row_spec(tm, d), _const_spec((1, d)), ANY, ANY, ANY] + [_row_spec(tm, d), _const_spec((1, d))][:n_head],
        out_specs=out_specs + act_specs, out_shape=out_shape + act_shapes,
        scratch=[pltpu.VMEM((f, d), BF16), pltpu.VMEM((f, d), BF16), pltpu.VMEM((f, d), BF16),
                 pltpu.SemaphoreType.DMA((3 * N_DEV,))],
        hosted=hosted)


def _ffn_bwd(dout, x, gate, up, g, srcs, nf, tm, name, hosted=()):
    t_tok, d = x.shape
    f = nf * N_DEV
    firsts = [first for _, first in srcs]
    per_chunk = -(-f // (FFN_CHUNKS * MXU_WIDTH)) * MXU_WIDTH
    bounds = [min(ck * per_chunk, f) for ck in range(FFN_CHUNKS + 1)]

    def body(dout_ref, x_ref, gate_ref, up_ref, g_ref, sg_ref, su_ref, sd_ref,
             dx_ref, dxb_ref, h_ref, act_ref, dgu_ref, dg_ref, wg_v, wu_v, wd_v, sems):
        _load_group(
            [(sg_ref, firsts[0], nf, wg_v), (su_ref, firsts[1], nf, wu_v), (sd_ref, firsts[2], nf, wd_v)], sems)

        @pl.when(pl.program_id(0) == 0)
        def _():
            dg_ref[...] = jnp.zeros_like(dg_ref)

        xv = x_ref[...]
        gv = g_ref[...]
        hv, xhat, r = _rms_fwd(xv, gv)
        h_ref[...] = hv.astype(BF16)
        dov = dout_ref[...]
        dob = dov.astype(BF16)
        dh = None
        for ck in range(FFN_CHUNKS):
            cols = slice(bounds[ck], bounds[ck + 1])
            gate_v = gate_ref[:, cols].astype(F32)
            up_v = up_ref[:, cols].astype(F32)
            sig = _sigmoid(gate_v)
            silu = gate_v * sig
            act_ref[:, cols] = (silu * up_v).astype(BF16)
            dact = _nt(dob, wd_v[cols, :])
            dup = (dact * silu).astype(BF16)
            dgate = (dact * up_v * (sig * (1.0 + gate_v * (1.0 - sig)))).astype(BF16)
            dgu_ref[:, cols] = dgate
            dgu_ref[:, f + bounds[ck]:f + bounds[ck + 1]] = dup
            part = _nn(dgate, wg_v[cols, :]) + _nn(dup, wu_v[cols, :])
            dh = part if dh is None else dh + part
        dxr, dg_row = _rms_bwd(dh, xhat, r, gv)
        dg_ref[...] += dg_row
        dx = dov + dxr
        dx_ref[...] = dx
        dxb_ref[...] = dx.astype(BF16)

    return _hosting_call(
        body, name, t_tok // tm, [dout, x, gate, up, g] + [arr for arr, _ in srcs],
        in_specs=[_row_spec(tm, d), _row_spec(tm, d), _row_spec(tm, f), _row_spec(tm, f), _const_spec((1, d)),
                  ANY, ANY, ANY],
        out_specs=[_row_spec(tm, d), _row_spec(tm, d), _row_spec(tm, d), _row_spec(tm, f), _row_spec(tm, 2 * f),
                   _const_spec((1, d))],
        out_shape=[jax.ShapeDtypeStruct((t_tok, d), F32), jax.ShapeDtypeStruct((t_tok, d), BF16),
                   jax.ShapeDtypeStruct((t_tok, d), BF16), jax.ShapeDtypeStruct((t_tok, f), BF16),
                   jax.ShapeDtypeStruct((t_tok, 2 * f), BF16), jax.ShapeDtypeStruct((1, d), F32)],
        scratch=[pltpu.VMEM((f, d), BF16), pltpu.VMEM((f, d), BF16), pltpu.VMEM((f, d), BF16),
                 pltpu.SemaphoreType.DMA((3 * N_DEV,))],
        hosted=hosted)


def _shift_down(z, k, prev_rows):
    row = lax.broadcasted_iota(jnp.int32, z.shape, 0)
    out = pltpu.roll(z, k, 0)
    for j in range(k):
        out = jnp.where(row == j, prev_rows[j], out)
    return out


def _shift_up(z, k, next_rows):
    tm = z.shape[0]
    row = lax.broadcasted_iota(jnp.int32, z.shape, 0)
    out = pltpu.roll(z, tm - k, 0)
    for j in range(k):
        out = jnp.where(row == tm - k + j, next_rows[j], out)
    return out


def _mixer_b_fwd(x, g, gath, conv_w, tm, seq, hosted=()):
    t_tok, d = x.shape
    e = conv_w.shape[1]
    e3 = 3 * e
    n_in, n_out = e3 // N_DEV, e // N_DEV
    tiles_per_seq = seq // tm

    def body(x_ref, g_ref, cw_ref, gath_ref, xo_ref, p_ref, win_v, wout_v, tail_v, sems):
        i = pl.program_id(0)
        _load_group([(gath_ref, 0, n_in, win_v), (gath_ref, n_in, n_out, wout_v)], sems)

        @pl.when(i % tiles_per_seq == 0)
        def _():
            tail_v[...] = jnp.zeros_like(tail_v)

        xv = x_ref[...]
        h = _rms_fwd(xv, g_ref[...])[0].astype(BF16)
        p = _nt(h, win_v[...])
        p_ref[...] = p.astype(BF16)
        z = p[:, e:2 * e] * p[:, 2 * e:]
        prev = [tail_v[SUBLANES - 2:SUBLANES - 1, :], tail_v[SUBLANES - 1:SUBLANES, :]]
        conv = (cw_ref[2:3, :] * z + cw_ref[1:2, :] * _shift_down(z, 1, prev[1:])
                + cw_ref[0:1, :] * _shift_down(z, 2, prev))
        tail_v[...] = z[tm - SUBLANES:, :]
        y = (p[:, :e] * conv).astype(BF16)
        xo_ref[...] = xv + _nn(y, wout_v[...])

    return _hosting_call(
        body, "mixer_b_fwd", t_tok // tm, [x, g, conv_w, gath],
        in_specs=[_row_spec(tm, d), _const_spec((1, d)), _const_spec((SUBLANES, e)), ANY],
        out_specs=[_row_spec(tm, d), _row_spec(tm, e3)],
        out_shape=[jax.ShapeDtypeStruct((t_tok, d), F32), jax.ShapeDtypeStruct((t_tok, e3), BF16)],
        scratch=[pltpu.VMEM((e3, d), BF16), pltpu.VMEM((e, d), BF16), pltpu.VMEM((SUBLANES, e), F32),
                 pltpu.SemaphoreType.DMA((2 * N_DEV,))],
        hosted=hosted)


def _mixer_b_bwd(dout, x, p, g, gath, conv_w, tm, seq, hosted=()):
    t_tok, d = x.shape
    e = conv_w.shape[1]
    e3 = 3 * e
    n_in, n_out = e3 // N_DEV, e // N_DEV
    tiles_per_seq = seq // tm
    halo_per_tile = tm // HALO
    n_halo = t_tok // HALO

    def body(dout_ref, dnext_ref, x_ref, p_ref, pprev_ref, pnext_ref, g_ref, cw_ref, gath_ref,
             dx_ref, dxb_ref, h_ref, y_ref, dp_ref, dg_ref, dcw_ref, win_v, wout_v, sems):
        i = pl.program_id(0)
        _load_group([(gath_ref, 0, n_in, win_v), (gath_ref, n_in, n_out, wout_v)], sems)

        @pl.when(i == 0)
        def _():
            dg_ref[...] = jnp.zeros_like(dg_ref)
            dcw_ref[...] = jnp.zeros_like(dcw_ref)

        first = (i % tiles_per_seq == 0).astype(F32)
        last = (i % tiles_per_seq == tiles_per_seq - 1).astype(F32)
        xv = x_ref[...]
        gv = g_ref[...]
        hv, xhat, r = _rms_fwd(xv, gv)
        h_ref[...] = hv.astype(BF16)
        pv = p_ref[...].astype(F32)
        bg, cg, hx = pv[:, :e], pv[:, e:2 * e], pv[:, 2 * e:]
        z = cg * hx
        pprev = pprev_ref[...].astype(F32)
        zprev = pprev[:, e:2 * e] * pprev[:, 2 * e:] * (1.0 - first)
        prev = [zprev[HALO - 2:HALO - 1, :], zprev[HALO - 1:HALO, :]]
        zs1 = _shift_down(z, 1, prev[1:])
        zs2 = _shift_down(z, 2, prev)
        w0, w1, w2 = cw_ref[0:1, :], cw_ref[1:2, :], cw_ref[2:3, :]
        conv = w2 * z + w1 * zs1 + w0 * zs2
        y_ref[...] = (bg * conv).astype(BF16)

        dov = dout_ref[...]
        wout_bf = wout_v[...]
        dy = _nt(dov.astype(BF16), wout_bf)
        dconv = dy * bg
        dnext = _nt(dnext_ref[...].astype(BF16), wout_bf) * pnext_ref[:, :e].astype(F32) * (1.0 - last)
        nxt = [dnext[0:1, :], dnext[1:2, :]]
        dz = w2 * dconv + w1 * _shift_up(dconv, 1, nxt[:1]) + w0 * _shift_up(dconv, 2, nxt)
        dcw_ref[0:1, :] += _col_sum(dconv * zs2)
        dcw_ref[1:2, :] += _col_sum(dconv * zs1)
        dcw_ref[2:3, :] += _col_sum(dconv * z)
        dp = jnp.concatenate([dy * conv, dz * hx, dz * cg], axis=1).astype(BF16)
        dp_ref[...] = dp
        dh = _nn(dp, win_v[...])
        dxr, dg_row = _rms_bwd(dh, xhat, r, gv)
        dg_ref[...] += dg_row
        dx = dov + dxr
        dx_ref[...] = dx
        dxb_ref[...] = dx.astype(BF16)

    prev_spec = lambda w: pl.BlockSpec((HALO, w), lambda i: (jnp.maximum(i * halo_per_tile - 1, 0), 0))
    next_spec = lambda w: pl.BlockSpec((HALO, w), lambda i: (jnp.minimum((i + 1) * halo_per_tile, n_halo - 1), 0))
    return _hosting_call(
        body, "mixer_b_bwd", t_tok // tm, [dout, dout, x, p, p, p, g, conv_w, gath],
        in_specs=[_row_spec(tm, d), next_spec(d), _row_spec(tm, d), _row_spec(tm, e3), prev_spec(e3), next_spec(e3),
                  _const_spec((1, d)), _const_spec((SUBLANES, e)), ANY],
        out_specs=[_row_spec(tm, d), _row_spec(tm, d), _row_spec(tm, d), _row_spec(tm, e), _row_spec(tm, e3),
                   _const_spec((1, d)), _const_spec((SUBLANES, e))],
        out_shape=[jax.ShapeDtypeStruct((t_tok, d), F32), jax.ShapeDtypeStruct((t_tok, d), BF16),
                   jax.ShapeDtypeStruct((t_tok, d), BF16), jax.ShapeDtypeStruct((t_tok, e), BF16),
                   jax.ShapeDtypeStruct((t_tok, e3), BF16), jax.ShapeDtypeStruct((1, d), F32),
                   jax.ShapeDtypeStruct((SUBLANES, e), F32)],
        scratch=[pltpu.VMEM((e3, d), BF16), pltpu.VMEM((e, d), BF16), pltpu.SemaphoreType.DMA((2 * N_DEV,))],
        hosted=hosted)


def _wgrad(a, b, bm, name, hosted=(), part=(1, 0)):
    t_tok, m = a.shape
    n = b.shape[1]
    every, first = part

    def body(a_ref, b_ref, o_ref):
        o_ref[...] = _tn(a_ref[...], b_ref[...]).astype(o_ref.dtype)

    outs, h_outs = _hosting_call(
        body, name, m // (bm * every), [a, b],
        in_specs=[pl.BlockSpec((t_tok, bm), lambda i: (0, every * i + first)), _const_spec((t_tok, n))],
        out_specs=[pl.BlockSpec((bm, n), lambda i: (i, 0))],
        out_shape=[jax.ShapeDtypeStruct((m // every, n), BF16)],
        scratch=[], hosted=hosted)
    return (outs[0], h_outs) if hosted else outs[0]


def _sum_slots(land, rb, name):
    n_slots, rows, cols = land.shape

    def body(l_ref, o_ref):
        acc = l_ref[0].astype(F32)
        for k in range(1, n_slots):
            acc = acc + l_ref[k].astype(F32)
        o_ref[...] = acc

    return pl.pallas_call(
        body, name=name, grid=(rows // rb,),
        in_specs=[pl.BlockSpec((n_slots, rb, cols), lambda i: (0, i, 0))],
        out_specs=pl.BlockSpec((rb, cols), lambda i: (i, 0)),
        out_shape=jax.ShapeDtypeStruct((rows, cols), F32),
        compiler_params=_params(sequential=False),
    )(land)


def _adamw(w, grad, m, v, rb, name):
    rows, cols = w.shape
    c1 = 1.0 / (1.0 - ADAM_B1 ** ADAM_STEP)
    c2 = 1.0 / (1.0 - ADAM_B2 ** ADAM_STEP)

    def body(w_ref, g_ref, m_ref, v_ref, d_ref, mo_ref, vo_ref):
        gv = g_ref[...]
        mn = ADAM_B1 * m_ref[...] + (1.0 - ADAM_B1) * gv
        vn = ADAM_B2 * v_ref[...] + (1.0 - ADAM_B2) * (gv * gv)
        mo_ref[...] = mn
        vo_ref[...] = vn
        d_ref[...] = -ADAM_LR * ((mn * c1) / (jnp.sqrt(vn * c2) + ADAM_EPS) + ADAM_WD * w_ref[...])

    spec = pl.BlockSpec((rb, cols), lambda i: (i, 0))
    shape = jax.ShapeDtypeStruct((rows, cols), F32)
    return pl.pallas_call(
        body, name=name, grid=(rows // rb,),
        in_specs=[spec] * 4, out_specs=[spec] * 3, out_shape=[shape] * 3,
        compiler_params=_params(sequential=False),
    )(w, grad, m, v)


def _pack_shards(groups, name, hosted=()):
    flat = [(part, layer) for group in groups for part, layer, _ in group]
    rows = [[p.shape[2] if turn else p.shape[1] for p, _, turn in group] for group in groups]
    first, _, first_turn = groups[0][0]
    width = first.shape[1] if first_turn else first.shape[2]

    def body(*refs):
        ins, outs = refs[:len(flat)], refs[len(flat):]
        k = 0
        for gi, group in enumerate(groups):
            off = 0
            for (_, _, turn), n in zip(group, rows[gi]):
                part = ins[k][...].astype(BF16)
                if turn:
                    r = lax.broadcasted_iota(jnp.int32, (n, n), 0)
                    c = lax.broadcasted_iota(jnp.int32, (n, n), 1)
                    part = _nt((r == c).astype(BF16), part).astype(BF16)
                outs[gi][off:off + n, :] = part
                off += n
                k += 1

    return _hosting_call(
        body, name, 1, [p for p, _ in flat],
        in_specs=[pl.BlockSpec((None,) + p.shape[1:], lambda i, layer=layer: (layer, 0, 0)) for p, layer in flat],
        out_specs=[_const_spec((sum(r), width)) for r in rows],
        out_shape=[jax.ShapeDtypeStruct((sum(r), width), BF16) for r in rows],
        scratch=[], hosted=hosted)


def _split_bf16(a):
    hi = a.astype(BF16)
    rest = a - hi.astype(F32)
    mid = rest.astype(BF16)
    return hi, mid, (rest - mid.astype(F32)).astype(BF16)


def _reduce_adamw(lands, w, m, v, transpose, name, hosted=()):
    n_layers, rows_w, cols_w = w.shape
    c1 = 1.0 / (1.0 - ADAM_B1 ** ADAM_STEP)
    c2 = 1.0 / (1.0 - ADAM_B2 ** ADAM_STEP)
    flat = [piece for pieces in lands for piece in pieces]
    counts = [len(pieces) for pieces in lands]
    if transpose:
        tiles = rows_w // MXU_WIDTH
        blk = (MXU_WIDTH, cols_w)
        land_specs = [pl.BlockSpec((N_CHIP, n, MXU_WIDTH), lambda i, b=first // n: (0, b, i % tiles))
                      for _, first, n in flat]
        for _, first, n in flat:
            assert first % n == 0
    else:
        tiles = 2
        blk = (rows_w // tiles, cols_w)
        assert all(c == 1 for c in counts)
        land_specs = [pl.BlockSpec((N_CHIP,) + blk, lambda i, b=first // blk[0]: (0, b + i % tiles, 0))
                      for _, first, _ in flat]
        for _, first, _ in flat:
            assert first % blk[0] == 0

    def body(*refs):
        land_refs = refs[:len(flat)]
        w_ref, m_ref, v_ref, g_ref, d_ref, mo_ref, vo_ref = refs[len(flat):]
        layer = pl.program_id(0) // tiles

        def total(ref):
            acc = ref[0].astype(F32)
            for q in range(1, N_CHIP):
                acc = acc + ref[q].astype(F32)
            return acc

        def layer_sum(k):
            first = sum(counts[:k])
            parts = [total(land_refs[first + j]) for j in range(counts[k])]
            return parts[0] if len(parts) == 1 else jnp.concatenate(parts, axis=0)

        gv = layer_sum(0)
        for k in range(1, n_layers):
            gv = jnp.where(layer == k, layer_sum(k), gv)
        if transpose:
            r = lax.broadcasted_iota(jnp.int32, (MXU_WIDTH, MXU_WIDTH), 0)
            c = lax.broadcasted_iota(jnp.int32, (MXU_WIDTH, MXU_WIDTH), 1)
            eye = (r == c).astype(BF16)
            hi, mid, lo = _split_bf16(gv)
            gv = _nt(eye, hi) + _nt(eye, mid) + _nt(eye, lo)
        g_ref[...] = gv
        mn = ADAM_B1 * m_ref[...] + (1.0 - ADAM_B1) * gv
        vn = ADAM_B2 * v_ref[...] + (1.0 - ADAM_B2) * (gv * gv)
        mo_ref[...] = mn
        vo_ref[...] = vn
        d_ref[...] = -ADAM_LR * ((mn * c1) / (jnp.sqrt(vn * c2) + ADAM_EPS) + ADAM_WD * w_ref[...])

    spec = pl.BlockSpec((None,) + blk, lambda i: (i // tiles, i % tiles, 0))
    shape = jax.ShapeDtypeStruct(w.shape, F32)
    outs, h_outs = _hosting_call(
        body, name, n_layers * tiles, [land for land, _, _ in flat] + [w, m, v],
        in_specs=land_specs + [spec] * 3, out_specs=[spec] * 4, out_shape=[shape] * 4, scratch=[], hosted=hosted)
    return (outs, h_outs) if hosted else outs


def _pack_small(parts, rows):
    flat = jnp.concatenate([p.reshape(-1).astype(F32) for p in parts])
    return jnp.pad(flat, (0, rows * LANES - flat.shape[0])).reshape(rows, LANES)


def _unpack_small(packed, shapes):
    flat = packed.reshape(-1)
    out = []
    pos = 0
    for s in shapes:
        n = math.prod(s)
        out.append(flat[pos:pos + n].reshape(s))
        pos += n
    return out


def kernel(x, mix_norm, ffn_norm, a_w_in, a_v_gain, a_v_bias, a_w_s, a_b_s, a_w_out, b_w_in, b_conv_w, b_w_out, ffn_w_gate, ffn_w_up, ffn_w_down, final_norm, loss_target, m_mix_norm, m_ffn_norm, m_a_w_in, m_a_v_gain, m_a_v_bias, m_a_w_s, m_a_b_s, m_a_w_out, m_b_w_in, m_b_conv_w, m_b_w_out, m_ffn_w_gate, m_ffn_w_up, m_ffn_w_down, m_final_norm, v_mix_norm, v_ffn_norm, v_a_w_in, v_a_v_gain, v_a_v_bias, v_a_w_s, v_a_b_s, v_a_w_out, v_b_w_in, v_b_conv_w, v_b_w_out, v_ffn_w_gate, v_ffn_w_up, v_ffn_w_down, v_final_norm):
    bsz, seq, d = x.shape
    t_tok = bsz * seq
    me = _my_index()
    xt = x.reshape(t_tok, d)
    target = loss_target.reshape(t_tok, d)
    e_a = a_v_gain.shape[1]
    e_b = b_w_out.shape[1] * N_DEV
    n_layers = ffn_w_gate.shape[0]
    f_shard = ffn_w_gate.shape[2]
    f_full = f_shard * N_DEV

    conv_pad = jnp.pad(b_conv_w[0], ((0, SUBLANES - CONV_W), (0, 0)))
    sh_a = jnp.concatenate([a_w_in[0].T, a_w_out[0]]).astype(BF16)
    bfull = jnp.repeat(a_b_s[0].T, GROUP, axis=1)

    gate_t, up_t = ffn_w_gate.transpose(0, 2, 1), ffn_w_up.transpose(0, 2, 1)
    (sh_b, sh_f0, sh_f1g, sh_f1ud), (gath_a, conv_g) = _pack_shards(
        [[(b_w_in, 0, True), (b_w_out, 0, False)],
         [(gate_t, 0, False), (up_t, 0, False), (ffn_w_down, 0, False)],
         [(gate_t, 1, False)],
         [(up_t, 1, False), (ffn_w_down, 1, False)]],
        "pack_shards", hosted=[_HostedGathers([sh_a, conv_pad])])
    conv_full = jnp.pad(conv_g[:, :CONV_W, :].transpose(1, 0, 2).reshape(CONV_W, e_b), ((0, SUBLANES - CONV_W), (0, 0)))
    (x1, gd_a, u_a, vhat_a, sv_a, y_a, rstd_a), (gath_f0,) = _mixer_a_fwd(
        xt, mix_norm[0:1], gath_a, a_v_gain, a_v_bias, a_w_s[0], bfull, tm=TOKEN_TILE,
        hosted=[_HostedGathers([sh_f0])])
    srcs0 = [(gath_f0, 0), (gath_f0, f_shard), (gath_f0, 2 * f_shard)]
    (x2, gate0, up0), (gath_b, gath_f1g) = _ffn_fwd(x1, ffn_norm[0:1], srcs0, f_shard, tm=TOKEN_TILE, name="ffn_fwd0",
                                                    hosted=[_HostedGathers([sh_b, sh_f1g])])
    (x3, p_b), (gath_f1ud,) = _mixer_b_fwd(x2, mix_norm[1:2], gath_b, conv_full, tm=TOKEN_TILE, seq=seq,
                                           hosted=[_HostedGathers([sh_f1ud])])
    srcs1 = [(gath_f1g, 0), (gath_f1ud, 0), (gath_f1ud, f_shard)]
    (loss_part, dx4, dx4_bf, d_final, gate1, up1), _ = _ffn_fwd(
        x3, ffn_norm[1:2], srcs1, f_shard, tm=TOKEN_TILE_WIDE, name="ffn_fwd1", head=(target, final_norm.reshape(1, d)))

    ffn_entries = [(0, 0, f_shard), (0, f_full, f_shard), (1, 0, f_shard)]
    (dx3, dx3_bf, h_f1, act1, dgu1, d_fn1), _ = _ffn_bwd(dx4, x3, gate1, up1, ffn_norm[1:2], srcs1, f_shard, tm=TOKEN_TILE,
                                                         name="ffn_bwd1")
    g_down1 = _wgrad(act1, dx4_bf, WGRAD_ROWS, "wgrad_down1")
    g_gu1 = _wgrad(dgu1, h_f1, WGRAD_ROWS_WIDE, "wgrad_gate_up1")
    ps_f1 = _pair_reduce([g_gu1, g_down1], ffn_entries, "pair_reduce_f1")
    (dx2, dx2_bf, h_b, y_b, dp_b, d_mn1, d_conv), (land_f1gu,) = _mixer_b_bwd(
        dx3, x2, p_b, mix_norm[1:2], gath_b, conv_full, tm=TOKEN_TILE_WIDE, seq=seq,
        hosted=[_HostedChipScatter(ps_f1, 0, 2 * f_shard)])
    g_b_out = _wgrad(y_b, dx3_bf, WGRAD_ROWS, "wgrad_b_out")
    g_b_in = _wgrad(dp_b, h_b, WGRAD_ROWS_WIDE, "wgrad_b_in")
    ps_b = _pair_reduce([g_b_in, g_b_out], [(0, 0, b_w_in.shape[2]), (1, 0, b_w_out.shape[1])], "pair_reduce_b")
    (dx1, dx1_bf, h_f0, act0, dgu0, d_fn0), (land_f1d, land_b) = _ffn_bwd(
        dx2, x1, gate0, up0, ffn_norm[0:1], srcs0, f_shard, tm=TOKEN_TILE, name="ffn_bwd0",
        hosted=[_HostedChipScatter(ps_f1, 2 * f_shard, f_shard), _HostedChipScatter(ps_b)])
    g_down0 = _wgrad(act0, dx2_bf, WGRAD_ROWS, "wgrad_down0")
    g_gu0 = _wgrad(dgu0, h_f0, WGRAD_ROWS_WIDE, "wgrad_gate_up0")
    g_a_out = _wgrad(y_a, dx1_bf, WGRAD_ROWS, "wgrad_a_out")
    n_ao = a_w_out.shape[1]
    ps_f0ao = _pair_reduce([g_gu0, g_down0, g_a_out], ffn_entries + [(2, 0, n_ao)], "pair_reduce_f0_a_out")
    (dx0, _, h_a, dz_a, d_mn0, d_gain, d_bias, d_ws, d_bs_acc), (land_f0, land_ao) = _mixer_a_bwd(
        dx1, xt, gd_a, u_a, vhat_a, sv_a, rstd_a, mix_norm[0:1], gath_a, a_v_gain, a_v_bias, a_w_s[0], tm=TOKEN_TILE,
        hosted=[_HostedChipScatter(ps_f0ao, 0, 3 * f_shard), _HostedChipScatter(ps_f0ao, 3 * f_shard, n_ao)])
    d_bs = d_bs_acc.reshape(HEADS, CHUNK)

    small_grads = [jnp.concatenate([d_mn0, d_mn1]), jnp.concatenate([d_fn0, d_fn1]), d_gain, d_bias, d_ws, d_bs,
                   d_final, d_conv[:CONV_W], loss_part]
    small_shapes = [(n_layers, d), (n_layers, d), (1, e_a), (1, e_a), (1, HEADS, CHUNK, CHUNK), (1, HEADS, CHUNK), (d,),
                    (CONV_W, e_b), ()]
    n_small = sum(math.prod(s) for s in small_shapes)
    blk_rows = -(-n_small // (N_DEV * LANES * SUBLANES)) * SUBLANES
    small_rows = blk_rows * N_DEV
    packed = _pack_small(small_grads, small_rows)
    n_half = a_w_in.shape[2] // 2
    g_ai0, (small_land,) = _wgrad(dz_a, h_a, n_half, "wgrad_a_in0", hosted=[_HostedScatterAll(packed)], part=(2, 0))
    ps_ai0 = _pair_reduce([g_ai0], [(0, 0, n_half)], "pair_reduce_a_in0")
    small_sum = _sum_slots(small_land, blk_rows, "sum_small")
    g_ai1, (land_ai0,) = _wgrad(dz_a, h_a, n_half, "wgrad_a_in1", hosted=[_HostedChipScatter(ps_ai0)], part=(2, 1))
    ps_ai1 = _pair_reduce([g_ai1], [(0, 0, n_half)], "pair_reduce_a_in1")
    land_ai1, small_gath = _exchange([_HostedChipScatter(ps_ai1), _HostedGathers([small_sum])], "tail_exchange")
    small_all = small_gath.reshape(small_rows, LANES)

    n_b_in = b_w_in.shape[2]
    gate_out = _reduce_adamw([[(land_f0, 0, f_shard)], [(land_f1gu, 0, f_shard)]], gate_t,
                             m_ffn_w_gate.transpose(0, 2, 1), v_ffn_w_gate.transpose(0, 2, 1), False, "adamw_gate")
    up_out = _reduce_adamw([[(land_f0, f_shard, f_shard)], [(land_f1gu, f_shard, f_shard)]], up_t,
                           m_ffn_w_up.transpose(0, 2, 1), v_ffn_w_up.transpose(0, 2, 1), False, "adamw_up")
    res = {
        "a_w_in": _reduce_adamw([[(land_ai0, 0, n_half), (land_ai1, 0, n_half)]], a_w_in, m_a_w_in, v_a_w_in, True,
                                "adamw_a_in"),
        "a_w_out": _reduce_adamw([[(land_ao, 0, a_w_out.shape[1])]], a_w_out, m_a_w_out, v_a_w_out, False,
                                 "adamw_a_out"),
        "b_w_in": _reduce_adamw([[(land_b, 0, n_b_in)]], b_w_in, m_b_w_in, v_b_w_in, True, "adamw_b_in"),
        "b_w_out": _reduce_adamw([[(land_b, n_b_in, b_w_out.shape[1])]], b_w_out, m_b_w_out, v_b_w_out, False,
                                 "adamw_b_out"),
        "ffn_w_gate": [o.transpose(0, 2, 1) for o in gate_out],
        "ffn_w_up": [o.transpose(0, 2, 1) for o in up_out],
        "ffn_w_down": _reduce_adamw([[(land_f0, 2 * f_shard, f_shard)], [(land_f1d, 0, f_shard)]], ffn_w_down,
                                    m_ffn_w_down, v_ffn_w_down, False, "adamw_down"),
    }

    (gr_mix, gr_ffn, gr_gain, gr_bias, gr_ws, gr_bs, gr_final, gr_conv_full, loss) = _unpack_small(small_all, small_shapes)
    gr_conv = lax.dynamic_slice_in_dim(gr_conv_full, me * (e_b // N_DEV), e_b // N_DEV, axis=1)[None]

    small_w =[mix_norm, ffn_norm, a_v_gain, a_v_bias, a_w_s, a_b_s, final_norm]
    small_m = [m_mix_norm, m_ffn_norm, m_a_v_gain, m_a_v_bias, m_a_w_s, m_a_b_s, m_final_norm]
    small_v = [v_mix_norm, v_ffn_norm, v_a_v_gain, v_a_v_bias, v_a_w_s, v_a_b_s, v_final_norm]
    small_g = [gr_mix, gr_ffn, gr_gain, gr_bias, gr_ws, gr_bs, gr_final]
    sm_shapes = small_shapes[:len(small_w)]
    sm_out = _adamw(_pack_small(small_w, small_rows), _pack_small(small_g, small_rows), _pack_small(small_m, small_rows),
                    _pack_small(small_v, small_rows), small_rows, "adamw_small")
    sm_delta, sm_m, sm_v = [_unpack_small(o, sm_shapes) for o in sm_out]

    conv_out = _adamw(b_conv_w[0], gr_conv[0], m_b_conv_w[0], v_b_conv_w[0], CONV_W, "adamw_conv")
    conv_delta, conv_m, conv_v = [o[None] for o in conv_out]

    order = ["mix_norm", "ffn_norm", "a_w_in", "a_v_gain", "a_v_bias", "a_w_s", "a_b_s", "a_w_out", "b_w_in",
             "b_conv_w", "b_w_out", "ffn_w_gate", "ffn_w_up", "ffn_w_down", "final_norm"]
    small_names = ["mix_norm", "ffn_norm", "a_v_gain", "a_v_bias", "a_w_s", "a_b_s", "final_norm"]
    grads = {"b_conv_w": gr_conv}
    deltas, new_m, new_v = {}, {}, {}
    for k, name in enumerate(small_names):
        grads[name] = small_g[k]
        deltas[name], new_m[name], new_v[name] = sm_delta[k], sm_m[k], sm_v[k]
    deltas["b_conv_w"], new_m["b_conv_w"], new_v["b_conv_w"] = conv_delta, conv_m, conv_v
    for name, (gg, dl, mm, vv) in res.items():
        grads[name], deltas[name], new_m[name], new_v[name] = gg, dl, mm, vv

    grad_x = dx0.reshape(bsz, seq, d)
    return (loss, grad_x, *[grads[n] for n in order], *[deltas[n] for n in order],
            *[new_m[n] for n in order], *[new_v[n] for n in order])
```

```python
import math

import jax
import jax.numpy as jnp
from jax import lax
from jax.experimental import pallas as pl
from jax.experimental.pallas import tpu as pltpu

F32 = jnp.float32
BF16 = jnp.bfloat16

N_DEV = 8
N_CHIP = 4
CHUNK = 128
HEADS = 16
GROUP = 128
CONV_W = 3
NORM_EPS = 1e-6
GELU_C = math.sqrt(2.0 / math.pi)
GELU_K = 0.044715

ADAM_LR = 0.001
ADAM_B1 = 0.9
ADAM_B2 = 0.999
ADAM_EPS = 1e-08
ADAM_WD = 0.01
ADAM_STEP = 10

LANES = 128
SUBLANES = 8
VMEM_LIMIT = 60 * 1024 * 1024
HALO = 16
MXU_WIDTH = 256
FFN_CHUNKS = 2
TOKEN_TILE = 256
TOKEN_TILE_WIDE = 512
WGRAD_ROWS = 256
WGRAD_ROWS_WIDE = 512
GATHER_RELAY_AT = 0.56
GATHER_FORWARD_LEAD = 2

MESH = pl.DeviceIdType.MESH
ANY = pl.BlockSpec(memory_space=pl.ANY)

PEER_FLIPS = {"sibling": (0, 0, 1), "x": (1, 0, 0), "y": (0, 1, 0), "diagonal": (1, 1, 0),
              "x_other": (1, 0, 1), "y_other": (0, 1, 1), "diagonal_other": (1, 1, 1)}
COLLECTIVE_IDS = {frozenset(["sibling"]): 0,
                  frozenset(["sibling", "x", "y"]): 1,
                  frozenset(["sibling", "x", "y", "diagonal"]): 2,
                  frozenset(["x", "y", "diagonal"]): 3,
                  frozenset(PEER_FLIPS): 4}


def _params(sequential=True):
    return pltpu.CompilerParams(
        dimension_semantics=("arbitrary",) if sequential else None,
        vmem_limit_bytes=VMEM_LIMIT)


def _nn(a, b):
    return jnp.dot(a, b, preferred_element_type=F32)


def _nt(a, b):
    return lax.dot_general(a, b, (((1,), (1,)), ((), ())), preferred_element_type=F32)


def _tn(a, b):
    return lax.dot_general(a, b, (((0,), (0,)), ((), ())), preferred_element_type=F32)


def _row_mean(a):
    return jnp.mean(a, axis=-1, keepdims=True)


def _col_sum(a):
    return jnp.sum(a, axis=0, keepdims=True)


def _rms_fwd(x, g):
    r = lax.rsqrt(_row_mean(x * x) + NORM_EPS)
    xhat = x * r
    return xhat * g, xhat, r


def _rms_bwd(dh, xhat, r, g):
    a = dh * g
    dx = r * (a - xhat * _row_mean(a * xhat))
    return dx, _col_sum(dh * xhat)


def _gelu_and_grad(x):
    x2 = x * x
    t = jnp.tanh(x * (GELU_C + (GELU_C * GELU_K) * x2))
    half = 0.5 * t + 0.5
    d = half + x * (0.5 - 0.5 * (t * t)) * (GELU_C + (3.0 * GELU_C * GELU_K) * x2)
    return x * half, d


def _sigmoid(x):
    return 1.0 / (1.0 + jnp.exp(-x))


def _row_spec(tm, width):
    return pl.BlockSpec((tm, width), lambda i: (i, 0))


def _const_spec(shape):
    nd = len(shape)
    return pl.BlockSpec(shape, lambda i: (0,) * nd)


def _load_group(parts, sems):
    @pl.when(pl.program_id(0) == 0)
    def _():
        copies = []
        for k, (gath_ref, first, n, dst) in enumerate(parts):
            for j in range(N_DEV):
                copies.append(pltpu.make_async_copy(gath_ref.at[j, pl.ds(first, n), :], dst.at[pl.ds(j * n, n), :],
                                                    sems.at[k * N_DEV + j]))
        for cp in copies:
            cp.start()
        for cp in copies:
            cp.wait()


def _hosting_call(body, name, n_steps, arrays, in_specs, out_specs, out_shape, scratch, hosted=()):
    n_in, n_out, n_scr = len(arrays), len(out_shape), len(scratch)
    h_arrays = [a for h in hosted for a in h.arrays]
    h_shapes = [s for h in hosted for s in h.out_shapes]
    h_sems = [s for h in hosted for s in h.sem_shapes]
    peers = sorted(set().union(*[h.peers for h in hosted])) if hosted else []

    def handshake():
        @pl.when(pl.program_id(0) == 0)
        def _():
            x, y, c = lax.axis_index("x"), lax.axis_index("y"), lax.axis_index("c")
            barrier = pltpu.get_barrier_semaphore()
            for p in peers:
                fx, fy, fc = PEER_FLIPS[p]
                peer = (1 - x if fx else x, 1 - y if fy else y, 1 - c if fc else c)
                pl.semaphore_signal(barrier, inc=1, device_id=peer, device_id_type=MESH)
            pl.semaphore_wait(barrier, len(peers))

    def full_body(*refs):
        pos = 0
        groups = []
        for n in (n_in, len(h_arrays), n_out, len(h_shapes), n_scr, len(h_sems)):
            groups.append(refs[pos:pos + n])
            pos += n
        own_in, h_in, own_out, h_out, own_scr, h_sem = groups
        per_host = []
        pi = po = ps = 0
        for h in hosted:
            ni, no, ns = len(h.arrays), len(h.out_shapes), len(h.sem_shapes)
            per_host.append((h, h_in[pi:pi + ni], h_out[po:po + no], h_sem[ps:ps + ns]))
            pi, po, ps = pi + ni, po + no, ps + ns
        if hosted:
            handshake()
        for h, ins, outs, sems in per_host:
            h.begin(ins, outs, sems, n_steps)
        body(*own_in, *own_out, *own_scr)
        for h, ins, outs, sems in per_host:
            h.end(ins, outs, sems, n_steps)

    outs = pl.pallas_call(
        full_body, name=name, grid=(n_steps,),
        in_specs=list(in_specs) + [ANY] * len(h_arrays),
        out_specs=list(out_specs) + [ANY] * len(h_shapes),
        out_shape=list(out_shape) + h_shapes,
        scratch_shapes=list(scratch) + h_sems,
        compiler_params=pltpu.CompilerParams(
            dimension_semantics=("arbitrary",), vmem_limit_bytes=VMEM_LIMIT,
            collective_id=COLLECTIVE_IDS[frozenset(peers)] if hosted else None),
    )(*arrays, *h_arrays)
    return outs[:n_out], outs[n_out:]


def _my_index():
    return 4 * lax.axis_index("x") + 2 * lax.axis_index("y") + lax.axis_index("c")


GATHER_COPIES = 8


def _gather_relays(n_rows, dtype):
    rows_per_tile = SUBLANES * 4 // jnp.dtype(dtype).itemsize
    return n_rows % 2 == 0 and (n_rows // 2) % rows_per_tile == 0


class _Gather:
    def __init__(self, shard, out, send_sems, recv_sems, local_sem):
        self.shard, self.out = shard, out
        self.send_sems, self.recv_sems, self.local_sem = send_sems, recv_sems, local_sem
        x, y, c = lax.axis_index("x"), lax.axis_index("y"), lax.axis_index("c")
        self.c = c
        self.me, self.sibling = (x, y, c), (x, y, 1 - c)
        self.xn, self.yn, self.dg = (1 - x, y), (x, 1 - y), (1 - x, 1 - y)
        self.n = shard.shape[0]
        self.half = self.n // 2
        self.relays = _gather_relays(self.n, shard.dtype)

    def _slot(self, dev, lo=0, hi=None):
        hi = self.n if hi is None else hi
        return self.out.at[4 * dev[0] + 2 * dev[1] + dev[2], pl.ds(lo, hi - lo), :]

    def _copy(self, k, block, to, src=None, lo=0, hi=None):
        return pltpu.make_async_remote_copy(
            src_ref=self._slot(block, lo, hi) if src is None else src, dst_ref=self._slot(block, lo, hi),
            send_sem=self.send_sems.at[k], recv_sem=self.recv_sems.at[k], device_id=to, device_id_type=MESH)

    def _local(self):
        return pltpu.make_async_copy(self.shard, self._slot(self.me), self.local_sem)

    def start(self):
        c = self.c
        self._local().start()
        self._copy(0, self.me, self.sibling, src=self.shard).start()
        self._copy(1, self.me, (*self.xn, c), src=self.shard).start()
        self._copy(2, self.me, (*self.yn, c), src=self.shard).start()
        if not self.relays:
            self._copy(3, self.me, (*self.dg, c), src=self.shard).start()

    def relay(self):
        c = self.c
        if self.relays:
            self._copy(1, (*self.xn, c), self.me).wait_recv()
            self._copy(3, (*self.xn, c), (*self.yn, c), hi=self.half).start()
            self._copy(2, (*self.yn, c), self.me).wait_recv()
            self._copy(4, (*self.yn, c), (*self.xn, c), lo=self.half).start()

    def forward(self):
        c = self.c
        if self.relays:
            self._copy(5, (*self.xn, c), self.sibling).start()
            self._copy(6, (*self.yn, c), self.sibling).start()
            self._copy(3, (*self.dg, c), self.me, hi=self.half).wait_recv()
            self._copy(4, (*self.dg, c), self.me, lo=self.half).wait_recv()
        else:
            self._copy(1, (*self.xn, c), self.me).wait_recv()
            self._copy(5, (*self.xn, c), self.sibling).start()
            self._copy(2, (*self.yn, c), self.me).wait_recv()
            self._copy(6, (*self.yn, c), self.sibling).start()
            self._copy(3, (*self.dg, c), self.me).wait_recv()
        self._copy(7, (*self.dg, c), self.sibling).start()

    def finish(self):
        c = self.c
        self._copy(0, self.sibling, self.me).wait_recv()
        for k, chip in ((5, self.xn), (6, self.yn), (7, self.dg)):
            self._copy(k, (*chip, 1 - c), self.me).wait_recv()
        for k in (0, 1, 2, 5, 6, 7):
            self._copy(k, self.me, self.sibling).wait_send()
        if self.relays:
            self._copy(3, self.me, self.sibling, hi=self.half).wait_send()
            self._copy(4, self.me, self.sibling, lo=self.half).wait_send()
        else:
            self._copy(3, self.me, self.sibling).wait_send()
        self._local().wait()


class _HostedGathers:
    def __init__(self, shards, mid_lead=GATHER_FORWARD_LEAD, relay_at=GATHER_RELAY_AT):
        n = len(shards)
        self.arrays = shards
        self.mid_lead, self.relay_at = mid_lead, relay_at
        self.peers = {"sibling", "x", "y"}
        if not all(_gather_relays(s.shape[0], s.dtype) for s in shards):
            self.peers.add("diagonal")
        self.out_shapes = [jax.ShapeDtypeStruct((N_DEV,) + s.shape, s.dtype) for s in shards]
        self.sem_shapes = [pltpu.SemaphoreType.DMA((n, GATHER_COPIES)), pltpu.SemaphoreType.DMA((n, GATHER_COPIES)),
                           pltpu.SemaphoreType.DMA((n,))]

    def _gathers(self, ins, outs, sems):
        return [_Gather(ins[a], outs[a], sems[0].at[a], sems[1].at[a], sems[2].at[a]) for a in range(len(ins))]

    def begin(self, ins, outs, sems, n_steps):
        i = pl.program_id(0)
        forward_step = max(n_steps - 1 - self.mid_lead, 0)
        relay_step = min(int(self.relay_at * n_steps), forward_step)

        @pl.when(i == 0)
        def _():
            for g in self._gathers(ins, outs, sems):
                g.start()

        if n_steps == 1:
            return

        @pl.when(i == relay_step)
        def _():
            for g in self._gathers(ins, outs, sems):
                g.relay()

        @pl.when(i == forward_step)
        def _():
            for g in self._gathers(ins, outs, sems):
                g.forward()

    def end(self, ins, outs, sems, n_steps):
        @pl.when(pl.program_id(0) == n_steps - 1)
        def _():
            gathers = self._gathers(ins, outs, sems)
            if n_steps == 1:
                for g in gathers:
                    g.relay()
                for g in gathers:
                    g.forward()
            for g in gathers:
                g.finish()


def _exchange(hosted, name):
    return _hosting_call(lambda: None, name, 1, [], [], [], [], [], hosted=hosted)[1]


class _ChipScatter:
    def __init__(self, pairsum, row0, land, send_sems, recv_sems, local_sem):
        self.pairsum, self.row0, self.land = pairsum, row0, land
        self.send_sems, self.recv_sems, self.local_sem = send_sems, recv_sems, local_sem
        x, y, c = lax.axis_index("x"), lax.axis_index("y"), lax.axis_index("c")
        self.c = c
        self.chip = 2 * x + y
        self.others = [(1 - x, y), (x, 1 - y), (1 - x, 1 - y)]

    def _src(self, chip):
        return self.pairsum.at[chip, pl.ds(self.row0, self.land.shape[1]), :]

    def _copy(self, k):
        ox, oy = self.others[k]
        return pltpu.make_async_remote_copy(
            src_ref=self._src(2 * ox + oy), dst_ref=self.land.at[self.chip],
            send_sem=self.send_sems.at[k], recv_sem=self.recv_sems.at[k], device_id=(ox, oy, self.c),
            device_id_type=MESH)

    def _arrival(self, k):
        ox, oy = self.others[k]
        return pltpu.make_async_remote_copy(
            src_ref=self._src(self.chip), dst_ref=self.land.at[2 * ox + oy],
            send_sem=self.send_sems.at[k], recv_sem=self.recv_sems.at[k], device_id=(ox, oy, self.c),
            device_id_type=MESH)

    def _local(self):
        return pltpu.make_async_copy(self._src(self.chip), self.land.at[self.chip], self.local_sem)

    def start(self):
        self._local().start()
        for k in range(N_CHIP - 1):
            self._copy(k).start()

    def finish(self):
        for k in range(N_CHIP - 1):
            self._arrival(k).wait_recv()
        for k in range(N_CHIP - 1):
            self._copy(k).wait_send()
        self._local().wait()


class _HostedChipScatter:
    def __init__(self, pairsum, row0=0, n=None):
        n = pairsum.shape[1] - row0 if n is None else n
        self.row0 = row0
        self.peers = {"x", "y", "diagonal"}
        self.arrays = [pairsum]
        self.out_shapes = [jax.ShapeDtypeStruct((N_CHIP, n, pairsum.shape[2]), pairsum.dtype)]
        self.sem_shapes = [pltpu.SemaphoreType.DMA((N_CHIP - 1,)), pltpu.SemaphoreType.DMA((N_CHIP - 1,)),
                           pltpu.SemaphoreType.DMA(())]

    def begin(self, ins, outs, sems, n_steps):
        @pl.when(pl.program_id(0) == 0)
        def _():
            _ChipScatter(ins[0], self.row0, outs[0], *sems).start()

    def end(self, ins, outs, sems, n_steps):
        @pl.when(pl.program_id(0) == n_steps - 1)
        def _():
            _ChipScatter(ins[0], self.row0, outs[0], *sems).finish()


def _pair_reduce(arrays, entries, name):
    n_arr, n_ent = len(arrays), len(entries)
    cols = arrays[0].shape[1]
    offsets = []
    total = 0
    for _, _, n in entries:
        offsets.append(total)
        total += n

    def body(*refs):
        ins, out_ref = refs[:n_arr], refs[n_arr]
        rbuf, own, send_sems, recv_sems, own_sems = refs[n_arr + 1:]
        q = pl.program_id(0)
        x, y, c = lax.axis_index("x"), lax.axis_index("y"), lax.axis_index("c")

        def block(e, chip, core):
            ai, first, n = entries[e]
            return ins[ai].at[pl.ds(first + (2 * chip + core) * n, n), :]

        def to_sibling(e, chip):
            return pltpu.make_async_remote_copy(
                src_ref=block(e, chip, 1 - c), dst_ref=rbuf.at[chip, pl.ds(offsets[e], entries[e][2]), :],
                send_sem=send_sems.at[e, chip], recv_sem=recv_sems.at[e, chip], device_id=(x, y, 1 - c),
                device_id_type=MESH)

        def own_block(e, chip):
            return pltpu.make_async_copy(block(e, chip, c), own.at[chip, pl.ds(offsets[e], entries[e][2]), :],
                                         own_sems.at[e, chip])

        @pl.when(q == 0)
        def _():
            barrier = pltpu.get_barrier_semaphore()
            pl.semaphore_signal(barrier, inc=1, device_id=(x, y, 1 - c), device_id_type=MESH)
            pl.semaphore_wait(barrier, 1)
            for chip in range(N_CHIP):
                for e in range(n_ent):
                    to_sibling(e, chip).start()
            for chip in range(N_CHIP):
                for e in range(n_ent):
                    own_block(e, chip).start()

        for e in range(n_ent):
            own_block(e, q).wait()
            to_sibling(e, q).wait_recv()
        out_ref[...] = (own[q].astype(F32) + rbuf[q].astype(F32)).astype(out_ref.dtype)

        @pl.when(q == N_CHIP - 1)
        def _():
            for chip in range(N_CHIP):
                for e in range(n_ent):
                    to_sibling(e, chip).wait_send()

    return pl.pallas_call(
        body, name=name, grid=(N_CHIP,),
        in_specs=[ANY] * n_arr,
        out_specs=pl.BlockSpec((None, total, cols), lambda q: (q, 0, 0)),
        out_shape=jax.ShapeDtypeStruct((N_CHIP, total, cols), BF16),
        scratch_shapes=[pltpu.VMEM((N_CHIP, total, cols), BF16), pltpu.VMEM((N_CHIP, total, cols), BF16),
                        pltpu.SemaphoreType.DMA((n_ent, N_CHIP)), pltpu.SemaphoreType.DMA((n_ent, N_CHIP)),
                        pltpu.SemaphoreType.DMA((n_ent, N_CHIP))],
        compiler_params=pltpu.CompilerParams(dimension_semantics=("arbitrary",), vmem_limit_bytes=VMEM_LIMIT,
                                             collective_id=COLLECTIVE_IDS[frozenset(["sibling"])]),
    )(*arrays)


class _HostedScatterAll:
    def __init__(self, packed):
        n = packed.shape[0] // N_DEV
        self.n = n
        self.peers = set(PEER_FLIPS)
        self.arrays = [packed]
        self.out_shapes = [jax.ShapeDtypeStruct((N_DEV, n, packed.shape[1]), packed.dtype)]
        self.sem_shapes = [pltpu.SemaphoreType.DMA((N_DEV - 1,)), pltpu.SemaphoreType.DMA((N_DEV - 1,)),
                           pltpu.SemaphoreType.DMA(())]

    def _copies(self, ins, outs, sems, with_arrivals):
        src, land = ins[0], outs[0]
        send_sems, recv_sems, local_sem = sems
        me = _my_index()

        def block(p):
            return src.at[pl.ds(p * self.n, self.n), :]

        local = pltpu.make_async_copy(block(me), land.at[me], local_sem)
        sends, arrivals = [], []
        for k in range(1, N_DEV):
            p = (me + k) % N_DEV
            q = (me + N_DEV - k) % N_DEV
            sends.append(pltpu.make_async_remote_copy(
                src_ref=block(p), dst_ref=land.at[me], send_sem=send_sems.at[k - 1], recv_sem=recv_sems.at[k - 1],
                device_id=(p // 4, (p // 2) % 2, p % 2), device_id_type=MESH))
            if with_arrivals:
                arrivals.append(pltpu.make_async_remote_copy(
                    src_ref=block(me), dst_ref=land.at[q], send_sem=send_sems.at[k - 1], recv_sem=recv_sems.at[k - 1],
                    device_id=(q // 4, (q // 2) % 2, q % 2), device_id_type=MESH))
        return local, sends, arrivals

    def begin(self, ins, outs, sems, n_steps):
        @pl.when(pl.program_id(0) == 0)
        def _():
            local, sends, _ = self._copies(ins, outs, sems, with_arrivals=False)
            local.start()
            for cp in sends:
                cp.start()

    def end(self, ins, outs, sems, n_steps):
        @pl.when(pl.program_id(0) == n_steps - 1)
        def _():
            local, sends, arrivals = self._copies(ins, outs, sems, with_arrivals=True)
            for cp in arrivals:
                cp.wait_recv()
            for cp in sends:
                cp.wait_send()
            local.wait()


def _tril_weights(ws_ref):
    r = lax.broadcasted_iota(jnp.int32, (CHUNK, CHUNK), 0)
    c = lax.broadcasted_iota(jnp.int32, (CHUNK, CHUNK), 1)
    return [jnp.where(r >= c, ws_ref[h], 0.0).astype(BF16) for h in range(HEADS)]


def _sgu_stats(zpre, gain, bias):
    e = zpre.shape[1] // 2
    z, dz = _gelu_and_grad(zpre)
    u, v = z[:, :e], z[:, e:]
    vc = v - _row_mean(v)
    rstd = lax.rsqrt(_row_mean(vc * vc) + NORM_EPS)
    vhat = vc * rstd
    return u, vhat, rstd, vhat * gain + bias, dz


def _spatial_fwd(wt, vn_bf, bfull_ref, sv_ref, tm):
    for ci in range(tm // CHUNK):
        rows = slice(ci * CHUNK, (ci + 1) * CHUNK)
        for h in range(HEADS):
            cols = slice(h * GROUP, (h + 1) * GROUP)
            sv_ref[rows, cols] = _nn(wt[h], vn_bf[rows, cols]) + bfull_ref[:, cols]


def _mixer_a_fwd(x, g, gath, gain, bias, ws, bfull, tm, hosted=()):
    t_tok, d = x.shape
    e = gain.shape[1]
    e2 = 2 * e
    n_in, n_out = e2 // N_DEV, e // N_DEV

    def body(x_ref, g_ref, gain_ref, bias_ref, ws_ref, bfull_ref, gath_ref,
             xo_ref, gd_ref, u_ref, vhat_ref, svo_ref, y_ref, rstd_ref, win_v, wout_v, sv_v, sems):
        _load_group([(gath_ref, 0, n_in, win_v), (gath_ref, n_in, n_out, wout_v)], sems)
        xv = x_ref[...]
        h = _rms_fwd(xv, g_ref[...])[0].astype(BF16)
        zpre = _nt(h, win_v[...])
        u, vhat, rstd, vn, gelu_d = _sgu_stats(zpre, gain_ref[...], bias_ref[...])
        gd_ref[...] = gelu_d.astype(BF16)
        u_ref[...] = u.astype(BF16)
        vhat_ref[...] = vhat.astype(BF16)
        rstd_ref[...] = rstd
        _spatial_fwd(_tril_weights(ws_ref), vn.astype(BF16), bfull_ref, sv_v, tm)
        sv = sv_v[...]
        svo_ref[...] = sv.astype(BF16)
        y = (u * sv).astype(BF16)
        y_ref[...] = y
        xo_ref[...] = xv + _nn(y, wout_v[...])

    return _hosting_call(
        body, "mixer_a_fwd", t_tok // tm, [x, g, gain, bias, ws, bfull, gath],
        in_specs=[_row_spec(tm, d), _const_spec((1, d)), _const_spec((1, e)), _const_spec((1, e)),
                  _const_spec((HEADS, CHUNK, CHUNK)), _const_spec((CHUNK, e)), ANY],
        out_specs=[_row_spec(tm, d), _row_spec(tm, e2), _row_spec(tm, e), _row_spec(tm, e), _row_spec(tm, e),
                   _row_spec(tm, e), _row_spec(tm, 1)],
        out_shape=[jax.ShapeDtypeStruct((t_tok, d), F32), jax.ShapeDtypeStruct((t_tok, e2), BF16),
                   jax.ShapeDtypeStruct((t_tok, e), BF16), jax.ShapeDtypeStruct((t_tok, e), BF16),
                   jax.ShapeDtypeStruct((t_tok, e), BF16), jax.ShapeDtypeStruct((t_tok, e), BF16),
                   jax.ShapeDtypeStruct((t_tok, 1), F32)],
        scratch=[pltpu.VMEM((e2, d), BF16), pltpu.VMEM((e, d), BF16), pltpu.VMEM((tm, e), F32),
                 pltpu.SemaphoreType.DMA((2 * N_DEV,))],
        hosted=hosted)


def _mixer_a_bwd(dout, x, gd, u_sav, vhat_sav, sv_sav, rstd_sav, g, gath, gain, bias, ws, tm, hosted=()):
    t_tok, d = x.shape
    e = gain.shape[1]
    e2 = 2 * e
    n_in, n_out = e2 // N_DEV, e // N_DEV
    n_steps = t_tok // tm

    def body(dout_ref, x_ref, gd_ref, u_ref, vhat_ref, sv_ref, rstd_ref, g_ref, gain_ref, bias_ref, ws_ref, gath_ref,
             dx_ref, dxb_ref, h_ref, dz_ref, dg_ref, dgain_ref, dbias_ref, dws_ref, dbso_ref,
             win_v, wout_v, dvn_v, dbs_ref, sems):
        i = pl.program_id(0)
        _load_group([(gath_ref, 0, n_in, win_v), (gath_ref, n_in, n_out, wout_v)], sems)

        @pl.when(i == 0)
        def _():
            dg_ref[...] = jnp.zeros_like(dg_ref)
            dgain_ref[...] = jnp.zeros_like(dgain_ref)
            dbias_ref[...] = jnp.zeros_like(dbias_ref)
            dws_ref[...] = jnp.zeros_like(dws_ref)
            dbs_ref[...] = jnp.zeros_like(dbs_ref)

        xv = x_ref[...]
        gv = g_ref[...]
        hv, xhat, r = _rms_fwd(xv, gv)
        h_ref[...] = hv.astype(BF16)
        gain_v = gain_ref[...]
        vhat = vhat_ref[...].astype(F32)
        vn_bf = (vhat * gain_v + bias_ref[...]).astype(BF16)
        wt = _tril_weights(ws_ref)

        dov = dout_ref[...]
        dy = _nt(dov.astype(BF16), wout_v[...])
        du = dy * sv_ref[...].astype(F32)
        dsv = dy * u_ref[...].astype(F32)
        dsv_bf = dsv.astype(BF16)
        for ci in range(tm // CHUNK):
            rows = slice(ci * CHUNK, (ci + 1) * CHUNK)
            dbs_ref[...] += dsv[rows, :]
            for h in range(HEADS):
                cols = slice(h * GROUP, (h + 1) * GROUP)
                dvn_v[rows, cols] = _tn(wt[h], dsv_bf[rows, cols])
                dws_ref[h] += _nt(dsv_bf[rows, cols], vn_bf[rows, cols])
        dvn = dvn_v[...]
        dgain_ref[...] += _col_sum(dvn * vhat)
        dbias_ref[...] += _col_sum(dvn)
        dvhat = dvn * gain_v
        dv = rstd_ref[...] * (dvhat - _row_mean(dvhat) - vhat * _row_mean(dvhat * vhat))
        dzpre = (jnp.concatenate([du, dv], axis=1) * gd_ref[...].astype(F32)).astype(BF16)
        dz_ref[...] = dzpre
        dh = _nn(dzpre, win_v[...])
        dxr, dg_row = _rms_bwd(dh, xhat, r, gv)
        dg_ref[...] += dg_row
        dx = dov + dxr
        dx_ref[...] = dx
        dxb_ref[...] = dx.astype(BF16)

        @pl.when(i == n_steps - 1)
        def _():
            rr = lax.broadcasted_iota(jnp.int32, (CHUNK, CHUNK), 0)
            cc = lax.broadcasted_iota(jnp.int32, (CHUNK, CHUNK), 1)
            for h in range(HEADS):
                dws_ref[h] = jnp.where(rr >= cc, dws_ref[h], 0.0)
                dbso_ref[h] = jnp.sum(dbs_ref[:, h * GROUP:(h + 1) * GROUP], axis=1, keepdims=True)

    return _hosting_call(
        body, "mixer_a_bwd", n_steps, [dout, x, gd, u_sav, vhat_sav, sv_sav, rstd_sav, g, gain, bias, ws, gath],
        in_specs=[_row_spec(tm, d), _row_spec(tm, d), _row_spec(tm, e2), _row_spec(tm, e), _row_spec(tm, e),
                  _row_spec(tm, e), _row_spec(tm, 1), _const_spec((1, d)),
                  _const_spec((1, e)), _const_spec((1, e)), _const_spec((HEADS, CHUNK, CHUNK)), ANY],
        out_specs=[_row_spec(tm, d), _row_spec(tm, d), _row_spec(tm, d), _row_spec(tm, e2),
                   _const_spec((1, d)), _const_spec((1, e)), _const_spec((1, e)),
                   _const_spec((HEADS, CHUNK, CHUNK)), _const_spec((HEADS, CHUNK, 1))],
        out_shape=[jax.ShapeDtypeStruct((t_tok, d), F32), jax.ShapeDtypeStruct((t_tok, d), BF16),
                   jax.ShapeDtypeStruct((t_tok, d), BF16), jax.ShapeDtypeStruct((t_tok, e2), BF16),
                   jax.ShapeDtypeStruct((1, d), F32), jax.ShapeDtypeStruct((1, e), F32),
                   jax.ShapeDtypeStruct((1, e), F32), jax.ShapeDtypeStruct((HEADS, CHUNK, CHUNK), F32),
                   jax.ShapeDtypeStruct((HEADS, CHUNK, 1), F32)],
        scratch=[pltpu.VMEM((e2, d), BF16), pltpu.VMEM((e, d), BF16), pltpu.VMEM((tm, e), F32),
                 pltpu.VMEM((CHUNK, e), F32), pltpu.SemaphoreType.DMA((2 * N_DEV,))],
        hosted=hosted)


def _ffn_fwd(x, g, srcs, nf, tm, name, hosted=(), head=None):
    t_tok, d = x.shape
    f = nf * N_DEV
    firsts = [first for _, first in srcs]
    n_head = 2 if head else 0

    def body(*refs):
        x_ref, g_ref, sg_ref, su_ref, sd_ref = refs[:5]
        gate_ref, up_ref, wg_v, wu_v, wd_v, sems = refs[-6:]
        _load_group(
            [(sg_ref, firsts[0], nf, wg_v), (su_ref, firsts[1], nf, wu_v), (sd_ref, firsts[2], nf, wd_v)], sems)
        if head:
            t_ref, gf_ref, loss_ref, dx_ref, dxb_ref, dgf_ref = refs[5:11]

            @pl.when(pl.program_id(0) == 0)
            def _():
                loss_ref[...] = jnp.zeros_like(loss_ref)
                dgf_ref[...] = jnp.zeros_like(dgf_ref)

        xv = x_ref[...]
        h = _rms_fwd(xv, g_ref[...])[0].astype(BF16)
        gate = _nt(h, wg_v[...])
        up = _nt(h, wu_v[...])
        gate_ref[...] = gate.astype(BF16)
        up_ref[...] = up.astype(BF16)
        act = (gate * _sigmoid(gate) * up).astype(BF16)
        xo = xv + _nn(act, wd_v[...])
        if head:
            gfv = gf_ref[...]
            y, xhat, r = _rms_fwd(xo, gfv)
            err = y - t_ref[...]
            loss_ref[...] += 0.5 * jnp.sum(_row_mean(err * err), axis=0, keepdims=True)
            dxr, dg_row = _rms_bwd(err * (1.0 / d), xhat, r, gfv)
            dgf_ref[...] += dg_row
            dx_ref[...] = dxr
            dxb_ref[...] = dxr.astype(BF16)
        else:
            refs[5][...] = xo

    act_specs = [_row_spec(tm, f), _row_spec(tm, f)]
    act_shapes = [jax.ShapeDtypeStruct((t_tok, f), BF16), jax.ShapeDtypeStruct((t_tok, f), BF16)]
    if head:
        out_specs = [_const_spec((1, 1)), _row_spec(tm, d), _row_spec(tm, d), _const_spec((1, d))]
        out_shape = [jax.ShapeDtypeStruct((1, 1), F32), jax.ShapeDtypeStruct((t_tok, d), F32),
                     jax.ShapeDtypeStruct((t_tok, d), BF16), jax.ShapeDtypeStruct((1, d), F32)]
    else:
        out_specs = [_row_spec(tm, d)]
        out_shape = [jax.ShapeDtypeStruct((t_tok, d), F32)]
    return _hosting_call(
        body, name, t_tok // tm, [x, g] + [arr for arr, _ in srcs] + list(head or ()),
        in_specs=[_row_spec(tm, d), _const_spec((1, d)), ANY, ANY, ANY] + [_row_spec(tm, d), _const_spec((1, d))][:n_head],
        out_specs=out_specs + act_specs, out_shape=out_shape + act_shapes,
        scratch=[pltpu.VMEM((f, d), BF16), pltpu.VMEM((f, d), BF16), pltpu.VMEM((f, d), BF16),
                 pltpu.SemaphoreType.DMA((3 * N_DEV,))],
        hosted=hosted)


def _ffn_bwd(dout, x, gate, up, g, srcs, nf, tm, name, hosted=()):
    t_tok, d = x.shape
    f = nf * N_DEV
    firsts = [first for _, first in srcs]
    per_chunk = -(-f // (FFN_CHUNKS * MXU_WIDTH)) * MXU_WIDTH
    bounds = [min(ck * per_chunk, f) for ck in range(FFN_CHUNKS + 1)]

    def body(dout_ref, x_ref, gate_ref, up_ref, g_ref, sg_ref, su_ref, sd_ref,
             dx_ref, dxb_ref, h_ref, act_ref, dgu_ref, dg_ref, wg_v, wu_v, wd_v, sems):
        _load_group(
            [(sg_ref, firsts[0], nf, wg_v), (su_ref, firsts[1], nf, wu_v), (sd_ref, firsts[2], nf, wd_v)], sems)

        @pl.when(pl.program_id(0) == 0)
        def _():
            dg_ref[...] = jnp.zeros_like(dg_ref)

        xv = x_ref[...]
        gv = g_ref[...]
        hv, xhat, r = _rms_fwd(xv, gv)
        h_ref[...] = hv.astype(BF16)
        dov = dout_ref[...]
        dob = dov.astype(BF16)
        dh = None
        for ck in range(FFN_CHUNKS):
            cols = slice(bounds[ck], bounds[ck + 1])
            gate_v = gate_ref[:, cols].astype(F32)
            up_v = up_ref[:, cols].astype(F32)
            sig = _sigmoid(gate_v)
            silu = gate_v * sig
            act_ref[:, cols] = (silu * up_v).astype(BF16)
            dact = _nt(dob, wd_v[cols, :])
            dup = (dact * silu).astype(BF16)
            dgate = (dact * up_v * (sig * (1.0 + gate_v * (1.0 - sig)))).astype(BF16)
            dgu_ref[:, cols] = dgate
            dgu_ref[:, f + bounds[ck]:f + bounds[ck + 1]] = dup
            part = _nn(dgate, wg_v[cols, :]) + _nn(dup, wu_v[cols, :])
            dh = part if dh is None else dh + part
        dxr, dg_row = _rms_bwd(dh, xhat, r, gv)
        dg_ref[...] += dg_row
        dx = dov + dxr
        dx_ref[...] = dx
        dxb_ref[...] = dx.astype(BF16)

    return _hosting_call(
        body, name, t_tok // tm, [dout, x, gate, up, g] + [arr for arr, _ in srcs],
        in_specs=[_row_spec(tm, d), _row_spec(tm, d), _row_spec(tm, f), _row_spec(tm, f), _const_spec((1, d)),
                  ANY, ANY, ANY],
        out_specs=[_row_spec(tm, d), _row_spec(tm, d), _row_spec(tm, d), _row_spec(tm, f), _row_spec(tm, 2 * f),
                   _const_spec((1, d))],
        out_shape=[jax.ShapeDtypeStruct((t_tok, d), F32), jax.ShapeDtypeStruct((t_tok, d), BF16),
                   jax.ShapeDtypeStruct((t_tok, d), BF16), jax.ShapeDtypeStruct((t_tok, f), BF16),
                   jax.ShapeDtypeStruct((t_tok, 2 * f), BF16), jax.ShapeDtypeStruct((1, d), F32)],
        scratch=[pltpu.VMEM((f, d), BF16), pltpu.VMEM((f, d), BF16), pltpu.VMEM((f, d), BF16),
                 pltpu.SemaphoreType.DMA((3 * N_DEV,))],
        hosted=hosted)


def _shift_down(z, k, prev_rows):
    row = lax.broadcasted_iota(jnp.int32, z.shape, 0)
    out = pltpu.roll(z, k, 0)
    for j in range(k):
        out = jnp.where(row == j, prev_rows[j], out)
    return out


def _shift_up(z, k, next_rows):
    tm = z.shape[0]
    row = lax.broadcasted_iota(jnp.int32, z.shape, 0)
    out = pltpu.roll(z, tm - k, 0)
    for j in range(k):
        out = jnp.where(row == tm - k + j, next_rows[j], out)
    return out


def _mixer_b_fwd(x, g, gath, conv_w, tm, seq, hosted=()):
    t_tok, d = x.shape
    e = conv_w.shape[1]
    e3 = 3 * e
    n_in, n_out = e3 // N_DEV, e // N_DEV
    tiles_per_seq = seq // tm

    def body(x_ref, g_ref, cw_ref, gath_ref, xo_ref, p_ref, win_v, wout_v, tail_v, sems):
        i = pl.program_id(0)
        _load_group([(gath_ref, 0, n_in, win_v), (gath_ref, n_in, n_out, wout_v)], sems)

        @pl.when(i % tiles_per_seq == 0)
        def _():
            tail_v[...] = jnp.zeros_like(tail_v)

        xv = x_ref[...]
        h = _rms_fwd(xv, g_ref[...])[0].astype(BF16)
        p = _nt(h, win_v[...])
        p_ref[...] = p.astype(BF16)
        z = p[:, e:2 * e] * p[:, 2 * e:]
        prev = [tail_v[SUBLANES - 2:SUBLANES - 1, :], tail_v[SUBLANES - 1:SUBLANES, :]]
        conv = (cw_ref[2:3, :] * z + cw_ref[1:2, :] * _shift_down(z, 1, prev[1:])
                + cw_ref[0:1, :] * _shift_down(z, 2, prev))
        tail_v[...] = z[tm - SUBLANES:, :]
        y = (p[:, :e] * conv).astype(BF16)
        xo_ref[...] = xv + _nn(y, wout_v[...])

    return _hosting_call(
        body, "mixer_b_fwd", t_tok // tm, [x, g, conv_w, gath],
        in_specs=[_row_spec(tm, d), _const_spec((1, d)), _const_spec((SUBLANES, e)), ANY],
        out_specs=[_row_spec(tm, d), _row_spec(tm, e3)],
        out_shape=[jax.ShapeDtypeStruct((t_tok, d), F32), jax.ShapeDtypeStruct((t_tok, e3), BF16)],
        scratch=[pltpu.VMEM((e3, d), BF16), pltpu.VMEM((e, d), BF16), pltpu.VMEM((SUBLANES, e), F32),
                 pltpu.SemaphoreType.DMA((2 * N_DEV,))],
        hosted=hosted)


def _mixer_b_bwd(dout, x, p, g, gath, conv_w, tm, seq, hosted=()):
    t_tok, d = x.shape
    e = conv_w.shape[1]
    e3 = 3 * e
    n_in, n_out = e3 // N_DEV, e // N_DEV
    tiles_per_seq = seq // tm
    halo_per_tile = tm // HALO
    n_halo = t_tok // HALO

    def body(dout_ref, dnext_ref, x_ref, p_ref, pprev_ref, pnext_ref, g_ref, cw_ref, gath_ref,
             dx_ref, dxb_ref, h_ref, y_ref, dp_ref, dg_ref, dcw_ref, win_v, wout_v, sems):
        i = pl.program_id(0)
        _load_group([(gath_ref, 0, n_in, win_v), (gath_ref, n_in, n_out, wout_v)], sems)

        @pl.when(i == 0)
        def _():
            dg_ref[...] = jnp.zeros_like(dg_ref)
            dcw_ref[...] = jnp.zeros_like(dcw_ref)

        first = (i % tiles_per_seq == 0).astype(F32)
        last = (i % tiles_per_seq == tiles_per_seq - 1).astype(F32)
        xv = x_ref[...]
        gv = g_ref[...]
        hv, xhat, r = _rms_fwd(xv, gv)
        h_ref[...] = hv.astype(BF16)
        pv = p_ref[...].astype(F32)
        bg, cg, hx = pv[:, :e], pv[:, e:2 * e], pv[:, 2 * e:]
        z = cg * hx
        pprev = pprev_ref[...].astype(F32)
        zprev = pprev[:, e:2 * e] * pprev[:, 2 * e:] * (1.0 - first)
        prev = [zprev[HALO - 2:HALO - 1, :], zprev[HALO - 1:HALO, :]]
        zs1 = _shift_down(z, 1, prev[1:])
        zs2 = _shift_down(z, 2, prev)
        w0, w1, w2 = cw_ref[0:1, :], cw_ref[1:2, :], cw_ref[2:3, :]
        conv = w2 * z + w1 * zs1 + w0 * zs2
        y_ref[...] = (bg * conv).astype(BF16)

        dov = dout_ref[...]
        wout_bf = wout_v[...]
        dy = _nt(dov.astype(BF16), wout_bf)
        dconv = dy * bg
        dnext = _nt(dnext_ref[...].astype(BF16), wout_bf) * pnext_ref[:, :e].astype(F32) * (1.0 - last)
        nxt = [dnext[0:1, :], dnext[1:2, :]]
        dz = w2 * dconv + w1 * _shift_up(dconv, 1, nxt[:1]) + w0 * _shift_up(dconv, 2, nxt)
        dcw_ref[0:1, :] += _col_sum(dconv * zs2)
        dcw_ref[1:2, :] += _col_sum(dconv * zs1)
        dcw_ref[2:3, :] += _col_sum(dconv * z)
        dp = jnp.concatenate([dy * conv, dz * hx, dz * cg], axis=1).astype(BF16)
        dp_ref[...] = dp
        dh = _nn(dp, win_v[...])
        dxr, dg_row = _rms_bwd(dh, xhat, r, gv)
        dg_ref[...] += dg_row
        dx = dov + dxr
        dx_ref[...] = dx
        dxb_ref[...] = dx.astype(BF16)

    prev_spec = lambda w: pl.BlockSpec((HALO, w), lambda i: (jnp.maximum(i * halo_per_tile - 1, 0), 0))
    next_spec = lambda w: pl.BlockSpec((HALO, w), lambda i: (jnp.minimum((i + 1) * halo_per_tile, n_halo - 1), 0))
    return _hosting_call(
        body, "mixer_b_bwd", t_tok // tm, [dout, dout, x, p, p, p, g, conv_w, gath],
        in_specs=[_row_spec(tm, d), next_spec(d), _row_spec(tm, d), _row_spec(tm, e3), prev_spec(e3), next_spec(e3),
                  _const_spec((1, d)), _const_spec((SUBLANES, e)), ANY],
        out_specs=[_row_spec(tm, d), _row_spec(tm, d), _row_spec(tm, d), _row_spec(tm, e), _row_spec(tm, e3),
                   _const_spec((1, d)), _const_spec((SUBLANES, e))],
        out_shape=[jax.ShapeDtypeStruct((t_tok, d), F32), jax.ShapeDtypeStruct((t_tok, d), BF16),
                   jax.ShapeDtypeStruct((t_tok, d), BF16), jax.ShapeDtypeStruct((t_tok, e), BF16),
                   jax.ShapeDtypeStruct((t_tok, e3), BF16), jax.ShapeDtypeStruct((1, d), F32),
                   jax.ShapeDtypeStruct((SUBLANES, e), F32)],
        scratch=[pltpu.VMEM((e3, d), BF16), pltpu.VMEM((e, d), BF16), pltpu.SemaphoreType.DMA((2 * N_DEV,))],
        hosted=hosted)


def _wgrad(a, b, bm, name, hosted=(), part=(1, 0)):
    t_tok, m = a.shape
    n = b.shape[1]
    every, first = part

    def body(a_ref, b_ref, o_ref):
        o_ref[...] = _tn(a_ref[...], b_ref[...]).astype(o_ref.dtype)

    outs, h_outs = _hosting_call(
        body, name, m // (bm * every), [a, b],
        in_specs=[pl.BlockSpec((t_tok, bm), lambda i: (0, every * i + first)), _const_spec((t_tok, n))],
        out_specs=[pl.BlockSpec((bm, n), lambda i: (i, 0))],
        out_shape=[jax.ShapeDtypeStruct((m // every, n), BF16)],
        scratch=[], hosted=hosted)
    return (outs[0], h_outs) if hosted else outs[0]


def _sum_slots(land, rb, name):
    n_slots, rows, cols = land.shape

    def body(l_ref, o_ref):
        acc = l_ref[0].astype(F32)
        for k in range(1, n_slots):
            acc = acc + l_ref[k].astype(F32)
        o_ref[...] = acc

    return pl.pallas_call(
        body, name=name, grid=(rows // rb,),
        in_specs=[pl.BlockSpec((n_slots, rb, cols), lambda i: (0, i, 0))],
        out_specs=pl.BlockSpec((rb, cols), lambda i: (i, 0)),
        out_shape=jax.ShapeDtypeStruct((rows, cols), F32),
        compiler_params=_params(sequential=False),
    )(land)


def _adamw(w, grad, m, v, rb, name):
    rows, cols = w.shape
    c1 = 1.0 / (1.0 - ADAM_B1 ** ADAM_STEP)
    c2 = 1.0 / (1.0 - ADAM_B2 ** ADAM_STEP)

    def body(w_ref, g_ref, m_ref, v_ref, d_ref, mo_ref, vo_ref):
        gv = g_ref[...]
        mn = ADAM_B1 * m_ref[...] + (1.0 - ADAM_B1) * gv
        vn = ADAM_B2 * v_ref[...] + (1.0 - ADAM_B2) * (gv * gv)
        mo_ref[...] = mn
        vo_ref[...] = vn
        d_ref[...] = -ADAM_LR * ((mn * c1) / (jnp.sqrt(vn * c2) + ADAM_EPS) + ADAM_WD * w_ref[...])

    spec = pl.BlockSpec((rb, cols), lambda i: (i, 0))
    shape = jax.ShapeDtypeStruct((rows, cols), F32)
    return pl.pallas_call(
        body, name=name, grid=(rows // rb,),
        in_specs=[spec] * 4, out_specs=[spec] * 3, out_shape=[shape] * 3,
        compiler_params=_params(sequential=False),
    )(w, grad, m, v)


def _pack_shards(groups, name, hosted=()):
    flat = [(part, layer) for group in groups for part, layer, _ in group]
    rows = [[p.shape[2] if turn else p.shape[1] for p, _, turn in group] for group in groups]
    first, _, first_turn = groups[0][0]
    width = first.shape[1] if first_turn else first.shape[2]

    def body(*refs):
        ins, outs = refs[:len(flat)], refs[len(flat):]
        k = 0
        for gi, group in enumerate(groups):
            off = 0
            for (_, _, turn), n in zip(group, rows[gi]):
                part = ins[k][...].astype(BF16)
                if turn:
                    r = lax.broadcasted_iota(jnp.int32, (n, n), 0)
                    c = lax.broadcasted_iota(jnp.int32, (n, n), 1)
                    part = _nt((r == c).astype(BF16), part).astype(BF16)
                outs[gi][off:off + n, :] = part
                off += n
                k += 1

    return _hosting_call(
        body, name, 1, [p for p, _ in flat],
        in_specs=[pl.BlockSpec((None,) + p.shape[1:], lambda i, layer=layer: (layer, 0, 0)) for p, layer in flat],
        out_specs=[_const_spec((sum(r), width)) for r in rows],
        out_shape=[jax.ShapeDtypeStruct((sum(r), width), BF16) for r in rows],
        scratch=[], hosted=hosted)


def _split_bf16(a):
    hi = a.astype(BF16)
    rest = a - hi.astype(F32)
    mid = rest.astype(BF16)
    return hi, mid, (rest - mid.astype(F32)).astype(BF16)


def _reduce_adamw(lands, w, m, v, transpose, name, hosted=()):
    n_layers, rows_w, cols_w = w.shape
    c1 = 1.0 / (1.0 - ADAM_B1 ** ADAM_STEP)
    c2 = 1.0 / (1.0 - ADAM_B2 ** ADAM_STEP)
    flat = [piece for pieces in lands for piece in pieces]
    counts = [len(pieces) for pieces in lands]
    if transpose:
        tiles = rows_w // MXU_WIDTH
        blk = (MXU_WIDTH, cols_w)
        land_specs = [pl.BlockSpec((N_CHIP, n, MXU_WIDTH), lambda i, b=first // n: (0, b, i % tiles))
                      for _, first, n in flat]
        for _, first, n in flat:
            assert first % n == 0
    else:
        tiles = 2
        blk = (rows_w // tiles, cols_w)
        assert all(c == 1 for c in counts)
        land_specs = [pl.BlockSpec((N_CHIP,) + blk, lambda i, b=first // blk[0]: (0, b + i % tiles, 0))
                      for _, first, _ in flat]
        for _, first, _ in flat:
            assert first % blk[0] == 0

    def body(*refs):
        land_refs = refs[:len(flat)]
        w_ref, m_ref, v_ref, g_ref, d_ref, mo_ref, vo_ref = refs[len(flat):]
        layer = pl.program_id(0) // tiles

        def total(ref):
            acc = ref[0].astype(F32)
            for q in range(1, N_CHIP):
                acc = acc + ref[q].astype(F32)
            return acc

        def layer_sum(k):
            first = sum(counts[:k])
            parts = [total(land_refs[first + j]) for j in range(counts[k])]
            return parts[0] if len(parts) == 1 else jnp.concatenate(parts, axis=0)

        gv = layer_sum(0)
        for k in range(1, n_layers):
            gv = jnp.where(layer == k, layer_sum(k), gv)
        if transpose:
            r = lax.broadcasted_iota(jnp.int32, (MXU_WIDTH, MXU_WIDTH), 0)
            c = lax.broadcasted_iota(jnp.int32, (MXU_WIDTH, MXU_WIDTH), 1)
            eye = (r == c).astype(BF16)
            hi, mid, lo = _split_bf16(gv)
            gv = _nt(eye, hi) + _nt(eye, mid) + _nt(eye, lo)
        g_ref[...] = gv
        mn = ADAM_B1 * m_ref[...] + (1.0 - ADAM_B1) * gv
        vn = ADAM_B2 * v_ref[...] + (1.0 - ADAM_B2) * (gv * gv)
        mo_ref[...] = mn
        vo_ref[...] = vn
        d_ref[...] = -ADAM_LR * ((mn * c1) / (jnp.sqrt(vn * c2) + ADAM_EPS) + ADAM_WD * w_ref[...])

    spec = pl.BlockSpec((None,) + blk, lambda i: (i // tiles, i % tiles, 0))
    shape = jax.ShapeDtypeStruct(w.shape, F32)
    outs, h_outs = _hosting_call(
        body, name, n_layers * tiles, [land for land, _, _ in flat] + [w, m, v],
        in_specs=land_specs + [spec] * 3, out_specs=[spec] * 4, out_shape=[shape] * 4, scratch=[], hosted=hosted)
    return (outs, h_outs) if hosted else outs


def _pack_small(parts, rows):
    flat = jnp.concatenate([p.reshape(-1).astype(F32) for p in parts])
    return jnp.pad(flat, (0, rows * LANES - flat.shape[0])).reshape(rows, LANES)


def _unpack_small(packed, shapes):
    flat = packed.reshape(-1)
    out = []
    pos = 0
    for s in shapes:
        n = math.prod(s)
        out.append(flat[pos:pos + n].reshape(s))
        pos += n
    return out


def kernel(x, mix_norm, ffn_norm, a_w_in, a_v_gain, a_v_bias, a_w_s, a_b_s, a_w_out, b_w_in, b_conv_w, b_w_out, ffn_w_gate, ffn_w_up, ffn_w_down, final_norm, loss_target, m_mix_norm, m_ffn_norm, m_a_w_in, m_a_v_gain, m_a_v_bias, m_a_w_s, m_a_b_s, m_a_w_out, m_b_w_in, m_b_conv_w, m_b_w_out, m_ffn_w_gate, m_ffn_w_up, m_ffn_w_down, m_final_norm, v_mix_norm, v_ffn_norm, v_a_w_in, v_a_v_gain, v_a_v_bias, v_a_w_s, v_a_b_s, v_a_w_out, v_b_w_in, v_b_conv_w, v_b_w_out, v_ffn_w_gate, v_ffn_w_up, v_ffn_w_down, v_final_norm):
    bsz, seq, d = x.shape
    t_tok = bsz * seq
    me = _my_index()
    xt = x.reshape(t_tok, d)
    target = loss_target.reshape(t_tok, d)
    e_a = a_v_gain.shape[1]
    e_b = b_w_out.shape[1] * N_DEV
    n_layers = ffn_w_gate.shape[0]
    f_shard = ffn_w_gate.shape[2]
    f_full = f_shard * N_DEV

    conv_pad = jnp.pad(b_conv_w[0], ((0, SUBLANES - CONV_W), (0, 0)))
    sh_a = jnp.concatenate([a_w_in[0].T, a_w_out[0]]).astype(BF16)
    bfull = jnp.repeat(a_b_s[0].T, GROUP, axis=1)

    gate_t, up_t = ffn_w_gate.transpose(0, 2, 1), ffn_w_up.transpose(0, 2, 1)
    (sh_b, sh_f0, sh_f1g, sh_f1ud), (gath_a, conv_g) = _pack_shards(
        [[(b_w_in, 0, True), (b_w_out, 0, False)],
         [(gate_t, 0, False), (up_t, 0, False), (ffn_w_down, 0, False)],
         [(gate_t, 1, False)],
         [(up_t, 1, False), (ffn_w_down, 1, False)]],
        "pack_shards", hosted=[_HostedGathers([sh_a, conv_pad])])
    conv_full = jnp.pad(conv_g[:, :CONV_W, :].transpose(1, 0, 2).reshape(CONV_W, e_b), ((0, SUBLANES - CONV_W), (0, 0)))
    (x1, gd_a, u_a, vhat_a, sv_a, y_a, rstd_a), (gath_f0,) = _mixer_a_fwd(
        xt, mix_norm[0:1], gath_a, a_v_gain, a_v_bias, a_w_s[0], bfull, tm=TOKEN_TILE,
        hosted=[_HostedGathers([sh_f0])])
    srcs0 = [(gath_f0, 0), (gath_f0, f_shard), (gath_f0, 2 * f_shard)]
    (x2, gate0, up0), (gath_b, gath_f1g) = _ffn_fwd(x1, ffn_norm[0:1], srcs0, f_shard, tm=TOKEN_TILE, name="ffn_fwd0",
                                                    hosted=[_HostedGathers([sh_b, sh_f1g])])
    (x3, p_b), (gath_f1ud,) = _mixer_b_fwd(x2, mix_norm[1:2], gath_b, conv_full, tm=TOKEN_TILE, seq=seq,
                                           hosted=[_HostedGathers([sh_f1ud])])
    srcs1 = [(gath_f1g, 0), (gath_f1ud, 0), (gath_f1ud, f_shard)]
    (loss_part, dx4, dx4_bf, d_final, gate1, up1), _ = _ffn_fwd(
        x3, ffn_norm[1:2], srcs1, f_shard, tm=TOKEN_TILE_WIDE, name="ffn_fwd1", head=(target, final_norm.reshape(1, d)))

    ffn_entries = [(0, 0, f_shard), (0, f_full, f_shard), (1, 0, f_shard)]
    (dx3, dx3_bf, h_f1, act1, dgu1, d_fn1), _ = _ffn_bwd(dx4, x3, gate1, up1, ffn_norm[1:2], srcs1, f_shard, tm=TOKEN_TILE,
                                                         name="ffn_bwd1")
    g_down1 = _wgrad(act1, dx4_bf, WGRAD_ROWS, "wgrad_down1")
    g_gu1 = _wgrad(dgu1, h_f1, WGRAD_ROWS_WIDE, "wgrad_gate_up1")
    ps_f1 = _pair_reduce([g_gu1, g_down1], ffn_entries, "pair_reduce_f1")
    (dx2, dx2_bf, h_b, y_b, dp_b, d_mn1, d_conv), (land_f1gu,) = _mixer_b_bwd(
        dx3, x2, p_b, mix_norm[1:2], gath_b, conv_full, tm=TOKEN_TILE_WIDE, seq=seq,
        hosted=[_HostedChipScatter(ps_f1, 0, 2 * f_shard)])
    g_b_out = _wgrad(y_b, dx3_bf, WGRAD_ROWS, "wgrad_b_out")
    g_b_in = _wgrad(dp_b, h_b, WGRAD_ROWS_WIDE, "wgrad_b_in")
    ps_b = _pair_reduce([g_b_in, g_b_out], [(0, 0, b_w_in.shape[2]), (1, 0, b_w_out.shape[1])], "pair_reduce_b")
    (dx1, dx1_bf, h_f0, act0, dgu0, d_fn0), (land_f1d, land_b) = _ffn_bwd(
        dx2, x1, gate0, up0, ffn_norm[0:1], srcs0, f_shard, tm=TOKEN_TILE, name="ffn_bwd0",
        hosted=[_HostedChipScatter(ps_f1, 2 * f_shard, f_shard), _HostedChipScatter(ps_b)])
    g_down0 = _wgrad(act0, dx2_bf, WGRAD_ROWS, "wgrad_down0")
    g_gu0 = _wgrad(dgu0, h_f0, WGRAD_ROWS_WIDE, "wgrad_gate_up0")
    g_a_out = _wgrad(y_a, dx1_bf, WGRAD_ROWS, "wgrad_a_out")
    n_ao = a_w_out.shape[1]
    ps_f0ao = _pair_reduce([g_gu0, g_down0, g_a_out], ffn_entries + [(2, 0, n_ao)], "pair_reduce_f0_a_out")
    (dx0, _, h_a, dz_a, d_mn0, d_gain, d_bias, d_ws, d_bs_acc), (land_f0, land_ao) = _mixer_a_bwd(
        dx1, xt, gd_a, u_a, vhat_a, sv_a, rstd_a, mix_norm[0:1], gath_a, a_v_gain, a_v_bias, a_w_s[0], tm=TOKEN_TILE,
        hosted=[_HostedChipScatter(ps_f0ao, 0, 3 * f_shard), _HostedChipScatter(ps_f0ao, 3 * f_shard, n_ao)])
    d_bs = d_bs_acc.reshape(HEADS, CHUNK)

    small_grads = [jnp.concatenate([d_mn0, d_mn1]), jnp.concatenate([d_fn0, d_fn1]), d_gain, d_bias, d_ws, d_bs,
                   d_final, d_conv[:CONV_W], loss_part]
    small_shapes = [(n_layers, d), (n_layers, d), (1, e_a), (1, e_a), (1, HEADS, CHUNK, CHUNK), (1, HEADS, CHUNK), (d,),
                    (CONV_W, e_b), ()]
    n_small = sum(math.prod(s) for s in small_shapes)
    blk_rows = -(-n_small // (N_DEV * LANES * SUBLANES)) * SUBLANES
    small_rows = blk_rows * N_DEV
    packed = _pack_small(small_grads, small_rows)
    n_half = a_w_in.shape[2] // 2
    g_ai0, (small_land,) = _wgrad(dz_a, h_a, n_half, "wgrad_a_in0", hosted=[_HostedScatterAll(packed)], part=(2, 0))
    ps_ai0 = _pair_reduce([g_ai0], [(0, 0, n_half)], "pair_reduce_a_in0")
    small_sum = _sum_slots(small_land, blk_rows, "sum_small")
    g_ai1, (land_ai0,) = _wgrad(dz_a, h_a, n_half, "wgrad_a_in1", hosted=[_HostedChipScatter(ps_ai0)], part=(2, 1))
    ps_ai1 = _pair_reduce([g_ai1], [(0, 0, n_half)], "pair_reduce_a_in1")
    land_ai1, small_gath = _exchange([_HostedChipScatter(ps_ai1), _HostedGathers([small_sum])], "tail_exchange")
    small_all = small_gath.reshape(small_rows, LANES)

    n_b_in = b_w_in.shape[2]
    gate_out = _reduce_adamw([[(land_f0, 0, f_shard)], [(land_f1gu, 0, f_shard)]], gate_t,
                             m_ffn_w_gate.transpose(0, 2, 1), v_ffn_w_gate.transpose(0, 2, 1), False, "adamw_gate")
    up_out = _reduce_adamw([[(land_f0, f_shard, f_shard)], [(land_f1gu, f_shard, f_shard)]], up_t,
                           m_ffn_w_up.transpose(0, 2, 1), v_ffn_w_up.transpose(0, 2, 1), False, "adamw_up")
    res = {
        "a_w_in": _reduce_adamw([[(land_ai0, 0, n_half), (land_ai1, 0, n_half)]], a_w_in, m_a_w_in, v_a_w_in, True,
                                "adamw_a_in"),
        "a_w_out": _reduce_adamw([[(land_ao, 0, a_w_out.shape[1])]], a_w_out, m_a_w_out, v_a_w_out, False,
                                 "adamw_a_out"),
        "b_w_in": _reduce_adamw([[(land_b, 0, n_b_in)]], b_w_in, m_b_w_in, v_b_w_in, True, "adamw_b_in"),
        "b_w_out": _reduce_adamw([[(land_b, n_b_in, b_w_out.shape[1])]], b_w_out, m_b_w_out, v_b_w_out, False,
                                 "adamw_b_out"),
        "ffn_w_gate": [o.transpose(0, 2, 1) for o in gate_out],
        "ffn_w_up": [o.transpose(0, 2, 1) for o in up_out],
        "ffn_w_down": _reduce_adamw([[(land_f0, 2 * f_shard, f_shard)], [(land_f1d, 0, f_shard)]], ffn_w_down,
                                    m_ffn_w_down, v_ffn_w_down, False, "adamw_down"),
    }

    (gr_mix, gr_ffn, gr_gain, gr_bias, gr_ws, gr_bs, gr_final, gr_conv_full, loss) = _unpack_small(small_all, small_shapes)
    gr_conv = lax.dynamic_slice_in_dim(gr_conv_full, me * (e_b // N_DEV), e_b // N_DEV, axis=1)[None]

    small_w =[mix_norm, ffn_norm, a_v_gain, a_v_bias, a_w_s, a_b_s, final_norm]
    small_m = [m_mix_norm, m_ffn_norm, m_a_v_gain, m_a_v_bias, m_a_w_s, m_a_b_s, m_final_norm]
    small_v = [v_mix_norm, v_ffn_norm, v_a_v_gain, v_a_v_bias, v_a_w_s, v_a_b_s, v_final_norm]
    small_g = [gr_mix, gr_ffn, gr_gain, gr_bias, gr_ws, gr_bs, gr_final]
    sm_shapes = small_shapes[:len(small_w)]
    sm_out = _adamw(_pack_small(small_w, small_rows), _pack_small(small_g, small_rows), _pack_small(small_m, small_rows),
                    _pack_small(small_v, small_rows), small_rows, "adamw_small")
    sm_delta, sm_m, sm_v = [_unpack_small(o, sm_shapes) for o in sm_out]

    conv_out = _adamw(b_conv_w[0], gr_conv[0], m_b_conv_w[0], v_b_conv_w[0], CONV_W, "adamw_conv")
    conv_delta, conv_m, conv_v = [o[None] for o in conv_out]

    order = ["mix_norm", "ffn_norm", "a_w_in", "a_v_gain", "a_v_bias", "a_w_s", "a_b_s", "a_w_out", "b_w_in",
             "b_conv_w", "b_w_out", "ffn_w_gate", "ffn_w_up", "ffn_w_down", "final_norm"]
    small_names = ["mix_norm", "ffn_norm", "a_v_gain", "a_v_bias", "a_w_s", "a_b_s", "final_norm"]
    grads = {"b_conv_w": gr_conv}
    deltas, new_m, new_v = {}, {}, {}
    for k, name in enumerate(small_names):
        grads[name] = small_g[k]
        deltas[name], new_m[name], new_v[name] = sm_delta[k], sm_m[k], sm_v[k]
    deltas["b_conv_w"], new_m["b_conv_w"], new_v["b_conv_w"] = conv_delta, conv_m, conv_v
    for name, (gg, dl, mm, vv) in res.items():
        grads[name], deltas[name], new_m[name], new_v[name] = gg, dl, mm, vv

    grad_x = dx0.reshape(bsz, seq, d)
    return (loss, grad_x, *[grads[n] for n in order], *[deltas[n] for n in order],
            *[new_m[n] for n in order], *[new_v[n] for n in order])
```

```python
import math

import jax
import jax.numpy as jnp
from jax import lax
from jax.experimental import pallas as pl
from jax.experimental.pallas import tpu as pltpu

F32 = jnp.float32
BF16 = jnp.bfloat16

N_DEV = 8
N_CHIP = 4
CHUNK = 128
HEADS = 16
GROUP = 128
CONV_W = 3
NORM_EPS = 1e-6
GELU_C = math.sqrt(2.0 / math.pi)
GELU_K = 0.044715

ADAM_LR = 0.001
ADAM_B1 = 0.9
ADAM_B2 = 0.999
ADAM_EPS = 1e-08
ADAM_WD = 0.01
ADAM_STEP = 10

LANES = 128
SUBLANES = 8
VMEM_LIMIT = 60 * 1024 * 1024
HALO = 16
MXU_WIDTH = 256
FFN_CHUNKS = 2
TOKEN_TILE = 256
TOKEN_TILE_WIDE = 512
WGRAD_ROWS = 256
WGRAD_ROWS_WIDE = 512
GATHER_RELAY_AT = 0.56
GATHER_FORWARD_LEAD = 2

MESH = pl.DeviceIdType.MESH
ANY = pl.BlockSpec(memory_space=pl.ANY)

PEER_FLIPS = {"sibling": (0, 0, 1), "x": (1, 0, 0), "y": (0, 1, 0), "diagonal": (1, 1, 0),
              "x_other": (1, 0, 1), "y_other": (0, 1, 1), "diagonal_other": (1, 1, 1)}
COLLECTIVE_IDS = {frozenset(["sibling"]): 0,
                  frozenset(["sibling", "x", "y"]): 1,
                  frozenset(["sibling", "x", "y", "diagonal"]): 2,
                  frozenset(["x", "y", "diagonal"]): 3,
                  frozenset(PEER_FLIPS): 4}


def _params(sequential=True):
    return pltpu.CompilerParams(
        dimension_semantics=("arbitrary",) if sequential else None,
        vmem_limit_bytes=VMEM_LIMIT)


def _nn(a, b):
    return jnp.dot(a, b, preferred_element_type=F32)


def _nt(a, b):
    return lax.dot_general(a, b, (((1,), (1,)), ((), ())), preferred_element_type=F32)


def _tn(a, b):
    return lax.dot_general(a, b, (((0,), (0,)), ((), ())), preferred_element_type=F32)


def _row_mean(a):
    return jnp.mean(a, axis=-1, keepdims=True)


def _col_sum(a):
    return jnp.sum(a, axis=0, keepdims=True)


def _rms_fwd(x, g):
    r = lax.rsqrt(_row_mean(x * x) + NORM_EPS)
    xhat = x * r
    return xhat * g, xhat, r


def _rms_bwd(dh, xhat, r, g):
    a = dh * g
    dx = r * (a - xhat * _row_mean(a * xhat))
    return dx, _col_sum(dh * xhat)


def _gelu_and_grad(x):
    x2 = x * x
    t = jnp.tanh(x * (GELU_C + (GELU_C * GELU_K) * x2))
    half = 0.5 * t + 0.5
    d = half + x * (0.5 - 0.5 * (t * t)) * (GELU_C + (3.0 * GELU_C * GELU_K) * x2)
    return x * half, d


def _sigmoid(x):
    return 1.0 / (1.0 + jnp.exp(-x))


def _row_spec(tm, width):
    return pl.BlockSpec((tm, width), lambda i: (i, 0))


def _const_spec(shape):
    nd = len(shape)
    return pl.BlockSpec(shape, lambda i: (0,) * nd)


def _load_group(parts, sems):
    @pl.when(pl.program_id(0) == 0)
    def _():
        copies = []
        for k, (gath_ref, first, n, dst) in enumerate(parts):
            for j in range(N_DEV):
                copies.append(pltpu.make_async_copy(gath_ref.at[j, pl.ds(first, n), :], dst.at[pl.ds(j * n, n), :],
                                                    sems.at[k * N_DEV + j]))
        for cp in copies:
            cp.start()
        for cp in copies:
            cp.wait()


def _hosting_call(body, name, n_steps, arrays, in_specs, out_specs, out_shape, scratch, hosted=()):
    n_in, n_out, n_scr = len(arrays), len(out_shape), len(scratch)
    h_arrays = [a for h in hosted for a in h.arrays]
    h_shapes = [s for h in hosted for s in h.out_shapes]
    h_sems = [s for h in hosted for s in h.sem_shapes]
    peers = sorted(set().union(*[h.peers for h in hosted])) if hosted else []

    def handshake():
        @pl.when(pl.program_id(0) == 0)
        def _():
            x, y, c = lax.axis_index("x"), lax.axis_index("y"), lax.axis_index("c")
            barrier = pltpu.get_barrier_semaphore()
            for p in peers:
                fx, fy, fc = PEER_FLIPS[p]
                peer = (1 - x if fx else x, 1 - y if fy else y, 1 - c if fc else c)
                pl.semaphore_signal(barrier, inc=1, device_id=peer, device_id_type=MESH)
            pl.semaphore_wait(barrier, len(peers))

    def full_body(*refs):
        pos = 0
        groups = []
        for n in (n_in, len(h_arrays), n_out, len(h_shapes), n_scr, len(h_sems)):
            groups.append(refs[pos:pos + n])
            pos += n
        own_in, h_in, own_out, h_out, own_scr, h_sem = groups
        per_host = []
        pi = po = ps = 0
        for h in hosted:
            ni, no, ns = len(h.arrays), len(h.out_shapes), len(h.sem_shapes)
            per_host.append((h, h_in[pi:pi + ni], h_out[po:po + no], h_sem[ps:ps + ns]))
            pi, po, ps = pi + ni, po + no, ps + ns
        if hosted:
            handshake()
        for h, ins, outs, sems in per_host:
            h.begin(ins, outs, sems, n_steps)
        body(*own_in, *own_out, *own_scr)
        for h, ins, outs, sems in per_host:
            h.end(ins, outs, sems, n_steps)

    outs = pl.pallas_call(
        full_body, name=name, grid=(n_steps,),
        in_specs=list(in_specs) + [ANY] * len(h_arrays),
        out_specs=list(out_specs) + [ANY] * len(h_shapes),
        out_shape=list(out_shape) + h_shapes,
        scratch_shapes=list(scratch) + h_sems,
        compiler_params=pltpu.CompilerParams(
            dimension_semantics=("arbitrary",), vmem_limit_bytes=VMEM_LIMIT,
            collective_id=COLLECTIVE_IDS[frozenset(peers)] if hosted else None),
    )(*arrays, *h_arrays)
    return outs[:n_out], outs[n_out:]


def _my_index():
    return 4 * lax.axis_index("x") + 2 * lax.axis_index("y") + lax.axis_index("c")


GATHER_COPIES = 8


def _gather_relays(n_rows, dtype):
    rows_per_tile = SUBLANES * 4 // jnp.dtype(dtype).itemsize
    return n_rows % 2 == 0 and (n_rows // 2) % rows_per_tile == 0


class _Gather:
    def __init__(self, shard, out, send_sems, recv_sems, local_sem):
        self.shard, self.out = shard, out
        self.send_sems, self.recv_sems, self.local_sem = send_sems, recv_sems, local_sem
        x, y, c = lax.axis_index("x"), lax.axis_index("y"), lax.axis_index("c")
        self.c = c
        self.me, self.sibling = (x, y, c), (x, y, 1 - c)
        self.xn, self.yn, self.dg = (1 - x, y), (x, 1 - y), (1 - x, 1 - y)
        self.n = shard.shape[0]
        self.half = self.n // 2
        self.relays = _gather_relays(self.n, shard.dtype)

    def _slot(self, dev, lo=0, hi=None):
        hi = self.n if hi is None else hi
        return self.out.at[4 * dev[0] + 2 * dev[1] + dev[2], pl.ds(lo, hi - lo), :]

    def _copy(self, k, block, to, src=None, lo=0, hi=None):
        return pltpu.make_async_remote_copy(
            src_ref=self._slot(block, lo, hi) if src is None else src, dst_ref=self._slot(block, lo, hi),
            send_sem=self.send_sems.at[k], recv_sem=self.recv_sems.at[k], device_id=to, device_id_type=MESH)

    def _local(self):
        return pltpu.make_async_copy(self.shard, self._slot(self.me), self.local_sem)

    def start(self):
        c = self.c
        self._local().start()
        self._copy(0, self.me, self.sibling, src=self.shard).start()
        self._copy(1, self.me, (*self.xn, c), src=self.shard).start()
        self._copy(2, self.me, (*self.yn, c), src=self.shard).start()
        if not self.relays:
            self._copy(3, self.me, (*self.dg, c), src=self.shard).start()

    def relay(self):
        c = self.c
        if self.relays:
            self._copy(1, (*self.xn, c), self.me).wait_recv()
            self._copy(3, (*self.xn, c), (*self.yn, c), hi=self.half).start()
            self._copy(2, (*self.yn, c), self.me).wait_recv()
            self._copy(4, (*self.yn, c), (*self.xn, c), lo=self.half).start()

    def forward(self):
        c = self.c
        if self.relays:
            self._copy(5, (*self.xn, c), self.sibling).start()
            self._copy(6, (*self.yn, c), self.sibling).start()
            self._copy(3, (*self.dg, c), self.me, hi=self.half).wait_recv()
            self._copy(4, (*self.dg, c), self.me, lo=self.half).wait_recv()
        else:
            self._copy(1, (*self.xn, c), self.me).wait_recv()
            self._copy(5, (*self.xn, c), self.sibling).start()
            self._copy(2, (*self.yn, c), self.me).wait_recv()
            self._copy(6, (*self.yn, c), self.sibling).start()
            self._copy(3, (*self.dg, c), self.me).wait_recv()
        self._copy(7, (*self.dg, c), self.sibling).start()

    def finish(self):
        c = self.c
        self._copy(0, self.sibling, self.me).wait_recv()
        for k, chip in ((5, self.xn), (6, self.yn), (7, self.dg)):
            self._copy(k, (*chip, 1 - c), self.me).wait_recv()
        for k in (0, 1, 2, 5, 6, 7):
            self._copy(k, self.me, self.sibling).wait_send()
        if self.relays:
            self._copy(3, self.me, self.sibling, hi=self.half).wait_send()
            self._copy(4, self.me, self.sibling, lo=self.half).wait_send()
        else:
            self._copy(3, self.me, self.sibling).wait_send()
        self._local().wait()


class _HostedGathers:
    def __init__(self, shards, mid_lead=GATHER_FORWARD_LEAD, relay_at=GATHER_RELAY_AT):
        n = len(shards)
        self.arrays = shards
        self.mid_lead, self.relay_at = mid_lead, relay_at
        self.peers = {"sibling", "x", "y"}
        if not all(_gather_relays(s.shape[0], s.dtype) for s in shards):
            self.peers.add("diagonal")
        self.out_shapes = [jax.ShapeDtypeStruct((N_DEV,) + s.shape, s.dtype) for s in shards]
        self.sem_shapes = [pltpu.SemaphoreType.DMA((n, GATHER_COPIES)), pltpu.SemaphoreType.DMA((n, GATHER_COPIES)),
                           pltpu.SemaphoreType.DMA((n,))]

    def _gathers(self, ins, outs, sems):
        return [_Gather(ins[a], outs[a], sems[0].at[a], sems[1].at[a], sems[2].at[a]) for a in range(len(ins))]

    def begin(self, ins, outs, sems, n_steps):
        i = pl.program_id(0)
        forward_step = max(n_steps - 1 - self.mid_lead, 0)
        relay_step = min(int(self.relay_at * n_steps), forward_step)

        @pl.when(i == 0)
        def _():
            for g in self._gathers(ins, outs, sems):
                g.start()

        if n_steps == 1:
            return

        @pl.when(i == relay_step)
        def _():
            for g in self._gathers(ins, outs, sems):
                g.relay()

        @pl.when(i == forward_step)
        def _():
            for g in self._gathers(ins, outs, sems):
                g.forward()

    def end(self, ins, outs, sems, n_steps):
        @pl.when(pl.program_id(0) == n_steps - 1)
        def _():
            gathers = self._gathers(ins, outs, sems)
            if n_steps == 1:
                for g in gathers:
                    g.relay()
                for g in gathers:
                    g.forward()
            for g in gathers:
                g.finish()


def _exchange(hosted, name):
    return _hosting_call(lambda: None, name, 1, [], [], [], [], [], hosted=hosted)[1]


class _ChipScatter:
    def __init__(self, pairsum, row0, land, send_sems, recv_sems, local_sem):
        self.pairsum, self.row0, self.land = pairsum, row0, land
        self.send_sems, self.recv_sems, self.local_sem = send_sems, recv_sems, local_sem
        x, y, c = lax.axis_index("x"), lax.axis_index("y"), lax.axis_index("c")
        self.c = c
        self.chip = 2 * x + y
        self.others = [(1 - x, y), (x, 1 - y), (1 - x, 1 - y)]

    def _src(self, chip):
        return self.pairsum.at[chip, pl.ds(self.row0, self.land.shape[1]), :]

    def _copy(self, k):
        ox, oy = self.others[k]
        return pltpu.make_async_remote_copy(
            src_ref=self._src(2 * ox + oy), dst_ref=self.land.at[self.chip],
            send_sem=self.send_sems.at[k], recv_sem=self.recv_sems.at[k], device_id=(ox, oy, self.c),
            device_id_type=MESH)

    def _arrival(self, k):
        ox, oy = self.others[k]
        return pltpu.make_async_remote_copy(
            src_ref=self._src(self.chip), dst_ref=self.land.at[2 * ox + oy],
            send_sem=self.send_sems.at[k], recv_sem=self.recv_sems.at[k], device_id=(ox, oy, self.c),
            device_id_type=MESH)

    def _local(self):
        return pltpu.make_async_copy(self._src(self.chip), self.land.at[self.chip], self.local_sem)

    def start(self):
        self._local().start()
        for k in range(N_CHIP - 1):
            self._copy(k).start()

    def finish(self):
        for k in range(N_CHIP - 1):
            self._arrival(k).wait_recv()
        for k in range(N_CHIP - 1):
            self._copy(k).wait_send()
        self._local().wait()


class _HostedChipScatter:
    def __init__(self, pairsum, row0=0, n=None):
        n = pairsum.shape[1] - row0 if n is None else n
        self.row0 = row0
        self.peers = {"x", "y", "diagonal"}
        self.arrays = [pairsum]
        self.out_shapes = [jax.ShapeDtypeStruct((N_CHIP, n, pairsum.shape[2]), pairsum.dtype)]
        self.sem_shapes = [pltpu.SemaphoreType.DMA((N_CHIP - 1,)), pltpu.SemaphoreType.DMA((N_CHIP - 1,)),
                           pltpu.SemaphoreType.DMA(())]

    def begin(self, ins, outs, sems, n_steps):
        @pl.when(pl.program_id(0) == 0)
        def _():
            _ChipScatter(ins[0], self.row0, outs[0], *sems).start()

    def end(self, ins, outs, sems, n_steps):
        @pl.when(pl.program_id(0) == n_steps - 1)
        def _():
            _ChipScatter(ins[0], self.row0, outs[0], *sems).finish()


def _pair_reduce(arrays, entries, name):
    n_arr, n_ent = len(arrays), len(entries)
    cols = arrays[0].shape[1]
    offsets = []
    total = 0
    for _, _, n in entries:
        offsets.append(total)
        total += n

    def body(*refs):
        ins, out_ref = refs[:n_arr], refs[n_arr]
        rbuf, own, send_sems, recv_sems, own_sems = refs[n_arr + 1:]
        q = pl.program_id(0)
        x, y, c = lax.axis_index("x"), lax.axis_index("y"), lax.axis_index("c")

        def block(e, chip, core):
            ai, first, n = entries[e]
            return ins[ai].at[pl.ds(first + (2 * chip + core) * n, n), :]

        def to_sibling(e, chip):
            return pltpu.make_async_remote_copy(
                src_ref=block(e, chip, 1 - c), dst_ref=rbuf.at[chip, pl.ds(offsets[e], entries[e][2]), :],
                send_sem=send_sems.at[e, chip], recv_sem=recv_sems.at[e, chip], device_id=(x, y, 1 - c),
                device_id_type=MESH)

        def own_block(e, chip):
            return pltpu.make_async_copy(block(e, chip, c), own.at[chip, pl.ds(offsets[e], entries[e][2]), :],
                                         own_sems.at[e, chip])

        @pl.when(q == 0)
        def _():
            barrier = pltpu.get_barrier_semaphore()
            pl.semaphore_signal(barrier, inc=1, device_id=(x, y, 1 - c), device_id_type=MESH)
            pl.semaphore_wait(barrier, 1)
            for chip in range(N_CHIP):
                for e in range(n_ent):
                    to_sibling(e, chip).start()
            for chip in range(N_CHIP):
                for e in range(n_ent):
                    own_block(e, chip).start()

        for e in range(n_ent):
            own_block(e, q).wait()
            to_sibling(e, q).wait_recv()
        out_ref[...] = (own[q].astype(F32) + rbuf[q].astype(F32)).astype(out_ref.dtype)

        @pl.when(q == N_CHIP - 1)
        def _():
            for chip in range(N_CHIP):
                for e in range(n_ent):
                    to_sibling(e, chip).wait_send()

    return pl.pallas_call(
        body, name=name, grid=(N_CHIP,),
        in_specs=[ANY] * n_arr,
        out_specs=pl.BlockSpec((None, total, cols), lambda q: (q, 0, 0)),
        out_shape=jax.ShapeDtypeStruct((N_CHIP, total, cols), BF16),
        scratch_shapes=[pltpu.VMEM((N_CHIP, total, cols), BF16), pltpu.VMEM((N_CHIP, total, cols), BF16),
                        pltpu.SemaphoreType.DMA((n_ent, N_CHIP)), pltpu.SemaphoreType.DMA((n_ent, N_CHIP)),
                        pltpu.SemaphoreType.DMA((n_ent, N_CHIP))],
        compiler_params=pltpu.CompilerParams(dimension_semantics=("arbitrary",), vmem_limit_bytes=VMEM_LIMIT,
                                             collective_id=COLLECTIVE_IDS[frozenset(["sibling"])]),
    )(*arrays)


class _HostedScatterAll:
    def __init__(self, packed):
        n = packed.shape[0] // N_DEV
        self.n = n
        self.peers = set(PEER_FLIPS)
        self.arrays = [packed]
        self.out_shapes = [jax.ShapeDtypeStruct((N_DEV, n, packed.shape[1]), packed.dtype)]
        self.sem_shapes = [pltpu.SemaphoreType.DMA((N_DEV - 1,)), pltpu.SemaphoreType.DMA((N_DEV - 1,)),
                           pltpu.SemaphoreType.DMA(())]

    def _copies(self, ins, outs, sems, with_arrivals):
        src, land = ins[0], outs[0]
        send_sems, recv_sems, local_sem = sems
        me = _my_index()

        def block(p):
            return src.at[pl.ds(p * self.n, self.n), :]

        local = pltpu.make_async_copy(block(me), land.at[me], local_sem)
        sends, arrivals = [], []
        for k in range(1, N_DEV):
            p = (me + k) % N_DEV
            q = (me + N_DEV - k) % N_DEV
            sends.append(pltpu.make_async_remote_copy(
                src_ref=block(p), dst_ref=land.at[me], send_sem=send_sems.at[k - 1], recv_sem=recv_sems.at[k - 1],
                device_id=(p // 4, (p // 2) % 2, p % 2), device_id_type=MESH))
            if with_arrivals:
                arrivals.append(pltpu.make_async_remote_copy(
                    src_ref=block(me), dst_ref=land.at[q], send_sem=send_sems.at[k - 1], recv_sem=recv_sems.at[k - 1],
                    device_id=(q // 4, (q // 2) % 2, q % 2), device_id_type=MESH))
        return local, sends, arrivals

    def begin(self, ins, outs, sems, n_steps):
        @pl.when(pl.program_id(0) == 0)
        def _():
            local, sends, _ = self._copies(ins, outs, sems, with_arrivals=False)
            local.start()
            for cp in sends:
                cp.start()

    def end(self, ins, outs, sems, n_steps):
        @pl.when(pl.program_id(0) == n_steps - 1)
        def _():
            local, sends, arrivals = self._copies(ins, outs, sems, with_arrivals=True)
            for cp in arrivals:
                cp.wait_recv()
            for cp in sends:
                cp.wait_send()
            local.wait()


def _tril_weights(ws_ref):
    r = lax.broadcasted_iota(jnp.int32, (CHUNK, CHUNK), 0)
    c = lax.broadcasted_iota(jnp.int32, (CHUNK, CHUNK), 1)
    return [jnp.where(r >= c, ws_ref[h], 0.0).astype(BF16) for h in range(HEADS)]


def _sgu_stats(zpre, gain, bias):
    e = zpre.shape[1] // 2
    z, dz = _gelu_and_grad(zpre)
    u, v = z[:, :e], z[:, e:]
    vc = v - _row_mean(v)
    rstd = lax.rsqrt(_row_mean(vc * vc) + NORM_EPS)
    vhat = vc * rstd
    return u, vhat, rstd, vhat * gain + bias, dz


def _spatial_fwd(wt, vn_bf, bfull_ref, sv_ref, tm):
    for ci in range(tm // CHUNK):
        rows = slice(ci * CHUNK, (ci + 1) * CHUNK)
        for h in range(HEADS):
            cols = slice(h * GROUP, (h + 1) * GROUP)
            sv_ref[rows, cols] = _nn(wt[h], vn_bf[rows, cols]) + bfull_ref[:, cols]


def _mixer_a_fwd(x, g, gath, gain, bias, ws, bfull, tm, hosted=()):
    t_tok, d = x.shape
    e = gain.shape[1]
    e2 = 2 * e
    n_in, n_out = e2 // N_DEV, e // N_DEV

    def body(x_ref, g_ref, gain_ref, bias_ref, ws_ref, bfull_ref, gath_ref,
             xo_ref, gd_ref, u_ref, vhat_ref, svo_ref, y_ref, rstd_ref, win_v, wout_v, sv_v, sems):
        _load_group([(gath_ref, 0, n_in, win_v), (gath_ref, n_in, n_out, wout_v)], sems)
        xv = x_ref[...]
        h = _rms_fwd(xv, g_ref[...])[0].astype(BF16)
        zpre = _nt(h, win_v[...])
        u, vhat, rstd, vn, gelu_d = _sgu_stats(zpre, gain_ref[...], bias_ref[...])
        gd_ref[...] = gelu_d.astype(BF16)
        u_ref[...] = u.astype(BF16)
        vhat_ref[...] = vhat.astype(BF16)
        rstd_ref[...] = rstd
        _spatial_fwd(_tril_weights(ws_ref), vn.astype(BF16), bfull_ref, sv_v, tm)
        sv = sv_v[...]
        svo_ref[...] = sv.astype(BF16)
        y = (u * sv).astype(BF16)
        y_ref[...] = y
        xo_ref[...] = xv + _nn(y, wout_v[...])

    return _hosting_call(
        body, "mixer_a_fwd", t_tok // tm, [x, g, gain, bias, ws, bfull, gath],
        in_specs=[_row_spec(tm, d), _const_spec((1, d)), _const_spec((1, e)), _const_spec((1, e)),
                  _const_spec((HEADS, CHUNK, CHUNK)), _const_spec((CHUNK, e)), ANY],
        out_specs=[_row_spec(tm, d), _row_spec(tm, e2), _row_spec(tm, e), _row_spec(tm, e), _row_spec(tm, e),
                   _row_spec(tm, e), _row_spec(tm, 1)],
        out_shape=[jax.ShapeDtypeStruct((t_tok, d), F32), jax.ShapeDtypeStruct((t_tok, e2), BF16),
                   jax.ShapeDtypeStruct((t_tok, e), BF16), jax.ShapeDtypeStruct((t_tok, e), BF16),
                   jax.ShapeDtypeStruct((t_tok, e), BF16), jax.ShapeDtypeStruct((t_tok, e), BF16),
                   jax.ShapeDtypeStruct((t_tok, 1), F32)],
        scratch=[pltpu.VMEM((e2, d), BF16), pltpu.VMEM((e, d), BF16), pltpu.VMEM((tm, e), F32),
                 pltpu.SemaphoreType.DMA((2 * N_DEV,))],
        hosted=hosted)


def _mixer_a_bwd(dout, x, gd, u_sav, vhat_sav, sv_sav, rstd_sav, g, gath, gain, bias, ws, tm, hosted=()):
    t_tok, d = x.shape
    e = gain.shape[1]
    e2 = 2 * e
    n_in, n_out = e2 // N_DEV, e // N_DEV
    n_steps = t_tok // tm

    def body(dout_ref, x_ref, gd_ref, u_ref, vhat_ref, sv_ref, rstd_ref, g_ref, gain_ref, bias_ref, ws_ref, gath_ref,
             dx_ref, dxb_ref, h_ref, dz_ref, dg_ref, dgain_ref, dbias_ref, dws_ref, dbso_ref,
             win_v, wout_v, dvn_v, dbs_ref, sems):
        i = pl.program_id(0)
        _load_group([(gath_ref, 0, n_in, win_v), (gath_ref, n_in, n_out, wout_v)], sems)

        @pl.when(i == 0)
        def _():
            dg_ref[...] = jnp.zeros_like(dg_ref)
            dgain_ref[...] = jnp.zeros_like(dgain_ref)
            dbias_ref[...] = jnp.zeros_like(dbias_ref)
            dws_ref[...] = jnp.zeros_like(dws_ref)
            dbs_ref[...] = jnp.zeros_like(dbs_ref)

        xv = x_ref[...]
        gv = g_ref[...]
        hv, xhat, r = _rms_fwd(xv, gv)
        h_ref[...] = hv.astype(BF16)
        gain_v = gain_ref[...]
        vhat = vhat_ref[...].astype(F32)
        vn_bf = (vhat * gain_v + bias_ref[...]).astype(BF16)
        wt = _tril_weights(ws_ref)

        dov = dout_ref[...]
        dy = _nt(dov.astype(BF16), wout_v[...])
        du = dy * sv_ref[...].astype(F32)
        dsv = dy * u_ref[...].astype(F32)
        dsv_bf = dsv.astype(BF16)
        for ci in range(tm // CHUNK):
            rows = slice(ci * CHUNK, (ci + 1) * CHUNK)
            dbs_ref[...] += dsv[rows, :]
            for h in range(HEADS):
                cols = slice(h * GROUP, (h + 1) * GROUP)
                dvn_v[rows, cols] = _tn(wt[h], dsv_bf[rows, cols])
                dws_ref[h] += _nt(dsv_bf[rows, cols], vn_bf[rows, cols])
        dvn = dvn_v[...]
        dgain_ref[...] += _col_sum(dvn * vhat)
        dbias_ref[...] += _col_sum(dvn)
        dvhat = dvn * gain_v
        dv = rstd_ref[...] * (dvhat - _row_mean(dvhat) - vhat * _row_mean(dvhat * vhat))
        dzpre = (jnp.concatenate([du, dv], axis=1) * gd_ref[...].astype(F32)).astype(BF16)
        dz_ref[...] = dzpre
        dh = _nn(dzpre, win_v[...])
        dxr, dg_row = _rms_bwd(dh, xhat, r, gv)
        dg_ref[...] += dg_row
        dx = dov + dxr
        dx_ref[...] = dx
        dxb_ref[...] = dx.astype(BF16)

        @pl.when(i == n_steps - 1)
        def _():
            rr = lax.broadcasted_iota(jnp.int32, (CHUNK, CHUNK), 0)
            cc = lax.broadcasted_iota(jnp.int32, (CHUNK, CHUNK), 1)
            for h in range(HEADS):
                dws_ref[h] = jnp.where(rr >= cc, dws_ref[h], 0.0)
                dbso_ref[h] = jnp.sum(dbs_ref[:, h * GROUP:(h + 1) * GROUP], axis=1, keepdims=True)

    return _hosting_call(
        body, "mixer_a_bwd", n_steps, [dout, x, gd, u_sav, vhat_sav, sv_sav, rstd_sav, g, gain, bias, ws, gath],
        in_specs=[_row_spec(tm, d), _row_spec(tm, d), _row_spec(tm, e2), _row_spec(tm, e), _row_spec(tm, e),
                  _row_spec(tm, e), _row_spec(tm, 1), _const_spec((1, d)),
                  _const_spec((1, e)), _const_spec((1, e)), _const_spec((HEADS, CHUNK, CHUNK)), ANY],
        out_specs=[_row_spec(tm, d), _row_spec(tm, d), _row_spec(tm, d), _row_spec(tm, e2),
                   _const_spec((1, d)), _const_spec((1, e)), _const_spec((1, e)),
                   _const_spec((HEADS, CHUNK, CHUNK)), _const_spec((HEADS, CHUNK, 1))],
        out_shape=[jax.ShapeDtypeStruct((t_tok, d), F32), jax.ShapeDtypeStruct((t_tok, d), BF16),
                   jax.ShapeDtypeStruct((t_tok, d), BF16), jax.ShapeDtypeStruct((t_tok, e2), BF16),
                   jax.ShapeDtypeStruct((1, d), F32), jax.ShapeDtypeStruct((1, e), F32),
                   jax.ShapeDtypeStruct((1, e), F32), jax.ShapeDtypeStruct((HEADS, CHUNK, CHUNK), F32),
                   jax.ShapeDtypeStruct((HEADS, CHUNK, 1), F32)],
        scratch=[pltpu.VMEM((e2, d), BF16), pltpu.VMEM((e, d), BF16), pltpu.VMEM((tm, e), F32),
                 pltpu.VMEM((CHUNK, e), F32), pltpu.SemaphoreType.DMA((2 * N_DEV,))],
        hosted=hosted)


def _ffn_fwd(x, g, srcs, nf, tm, name, hosted=(), head=None):
    t_tok, d = x.shape
    f = nf * N_DEV
    firsts = [first for _, first in srcs]
    n_head = 2 if head else 0

    def body(*refs):
        x_ref, g_ref, sg_ref, su_ref, sd_ref = refs[:5]
        gate_ref, up_ref, wg_v, wu_v, wd_v, sems = refs[-6:]
        _load_group(
            [(sg_ref, firsts[0], nf, wg_v), (su_ref, firsts[1], nf, wu_v), (sd_ref, firsts[2], nf, wd_v)], sems)
        if head:
            t_ref, gf_ref, loss_ref, dx_ref, dxb_ref, dgf_ref = refs[5:11]

            @pl.when(pl.program_id(0) == 0)
            def _():
                loss_ref[...] = jnp.zeros_like(loss_ref)
                dgf_ref[...] = jnp.zeros_like(dgf_ref)

        xv = x_ref[...]
        h = _rms_fwd(xv, g_ref[...])[0].astype(BF16)
        gate = _nt(h, wg_v[...])
        up = _nt(h, wu_v[...])
        gate_ref[...] = gate.astype(BF16)
        up_ref[...] = up.astype(BF16)
        act = (gate * _sigmoid(gate) * up).astype(BF16)
        xo = xv + _nn(act, wd_v[...])
        if head:
            gfv = gf_ref[...]
            y, xhat, r = _rms_fwd(xo, gfv)
            err = y - t_ref[...]
            loss_ref[...] += 0.5 * jnp.sum(_row_mean(err * err), axis=0, keepdims=True)
            dxr, dg_row = _rms_bwd(err * (1.0 / d), xhat, r, gfv)
            dgf_ref[...] += dg_row
            dx_ref[...] = dxr
            dxb_ref[...] = dxr.astype(BF16)
        else:
            refs[5][...] = xo

    act_specs = [_row_spec(tm, f), _row_spec(tm, f)]
    act_shapes = [jax.ShapeDtypeStruct((t_tok, f), BF16), jax.ShapeDtypeStruct((t_tok, f), BF16)]
    if head:
        out_specs = [_const_spec((1, 1)), _row_spec(tm, d), _row_spec(tm, d), _const_spec((1, d))]
        out_shape = [jax.ShapeDtypeStruct((1, 1), F32), jax.ShapeDtypeStruct((t_tok, d), F32),
                     jax.ShapeDtypeStruct((t_tok, d), BF16), jax.ShapeDtypeStruct((1, d), F32)]
    else:
        out_specs = [_row_spec(tm, d)]
        out_shape = [jax.ShapeDtypeStruct((t_tok, d), F32)]
    return _hosting_call(
        body, name, t_tok // tm, [x, g] + [arr for arr, _ in srcs] + list(head or ()),
        in_specs=[_row_spec(tm, d), _const_spec((1, d)), ANY, ANY, ANY] + [_row_spec(tm, d), _const_spec((1, d))][:n_head],
        out_specs=out_specs + act_specs, out_shape=out_shape + act_shapes,
        scratch=[pltpu.VMEM((f, d), BF16), pltpu.VMEM((f, d), BF16), pltpu.VMEM((f, d), BF16),
                 pltpu.SemaphoreType.DMA((3 * N_DEV,))],
        hosted=hosted)


def _ffn_bwd(dout, x, gate, up, g, srcs, nf, tm, name, hosted=()):
    t_tok, d = x.shape
    f = nf * N_DEV
    firsts = [first for _, first in srcs]
    per_chunk = -(-f // (FFN_CHUNKS * MXU_WIDTH)) * MXU_WIDTH
    bounds = [min(ck * per_chunk, f) for ck in range(FFN_CHUNKS + 1)]

    def body(dout_ref, x_ref, gate_ref, up_ref, g_ref, sg_ref, su_ref, sd_ref,
             dx_ref, dxb_ref, h_ref, act_ref, dgu_ref, dg_ref, wg_v, wu_v, wd_v, sems):
        _load_group(
            [(sg_ref, firsts[0], nf, wg_v), (su_ref, firsts[1], nf, wu_v), (sd_ref, firsts[2], nf, wd_v)], sems)

        @pl.when(pl.program_id(0) == 0)
        def _():
            dg_ref[...] = jnp.zeros_like(dg_ref)

        xv = x_ref[...]
        gv = g_ref[...]
        hv, xhat, r = _rms_fwd(xv, gv)
        h_ref[...] = hv.astype(BF16)
        dov = dout_ref[...]
        dob = dov.astype(BF16)
        dh = None
        for ck in range(FFN_CHUNKS):
            cols = slice(bounds[ck], bounds[ck + 1])
            gate_v = gate_ref[:, cols].astype(F32)
            up_v = up_ref[:, cols].astype(F32)
            sig = _sigmoid(gate_v)
            silu = gate_v * sig
            act_ref[:, cols] = (silu * up_v).astype(BF16)
            dact = _nt(dob, wd_v[cols, :])
            dup = (dact * silu).astype(BF16)
            dgate = (dact * up_v * (sig * (1.0 + gate_v * (1.0 - sig)))).astype(BF16)
            dgu_ref[:, cols] = dgate
            dgu_ref[:, f + bounds[ck]:f + bounds[ck + 1]] = dup
            part = _nn(dgate, wg_v[cols, :]) + _nn(dup, wu_v[cols, :])
            dh = part if dh is None else dh + part
        dxr, dg_row = _rms_bwd(dh, xhat, r, gv)
        dg_ref[...] += dg_row
        dx = dov + dxr
        dx_ref[...] = dx
        dxb_ref[...] = dx.astype(BF16)

    return _hosting_call(
        body, name, t_tok // tm, [dout, x, gate, up, g] + [arr for arr, _ in srcs],
        in_specs=[_row_spec(tm, d), _row_spec(tm, d), _row_spec(tm, f), _row_spec(tm, f), _const_spec((1, d)),
                  ANY, ANY, ANY],
        out_specs=[_row_spec(tm, d), _row_spec(tm, d), _row_spec(tm, d), _row_spec(tm, f), _row_spec(tm, 2 * f),
                   _const_spec((1, d))],
        out_shape=[jax.ShapeDtypeStruct((t_tok, d), F32), jax.ShapeDtypeStruct((t_tok, d), BF16),
                   jax.ShapeDtypeStruct((t_tok, d), BF16), jax.ShapeDtypeStruct((t_tok, f), BF16),
                   jax.ShapeDtypeStruct((t_tok, 2 * f), BF16), jax.ShapeDtypeStruct((1, d), F32)],
        scratch=[pltpu.VMEM((f, d), BF16), pltpu.VMEM((f, d), BF16), pltpu.VMEM((f, d), BF16),
                 pltpu.SemaphoreType.DMA((3 * N_DEV,))],
        hosted=hosted)


def _shift_down(z, k, prev_rows):
    row = lax.broadcasted_iota(jnp.int32, z.shape, 0)
    out = pltpu.roll(z, k, 0)
    for j in range(k):
        out = jnp.where(row == j, prev_rows[j], out)
    return out


def _shift_up(z, k, next_rows):
    tm = z.shape[0]
    row = lax.broadcasted_iota(jnp.int32, z.shape, 0)
    out = pltpu.roll(z, tm - k, 0)
    for j in range(k):
        out = jnp.where(row == tm - k + j, next_rows[j], out)
    return out


def _mixer_b_fwd(x, g, gath, conv_w, tm, seq, hosted=()):
    t_tok, d = x.shape
    e = conv_w.shape[1]
    e3 = 3 * e
    n_in, n_out = e3 // N_DEV, e // N_DEV
    tiles_per_seq = seq // tm

    def body(x_ref, g_ref, cw_ref, gath_ref, xo_ref, p_ref, win_v, wout_v, tail_v, sems):
        i = pl.program_id(0)
        _load_group([(gath_ref, 0, n_in, win_v), (gath_ref, n_in, n_out, wout_v)], sems)

        @pl.when(i % tiles_per_seq == 0)
        def _():
            tail_v[...] = jnp.zeros_like(tail_v)

        xv = x_ref[...]
        h = _rms_fwd(xv, g_ref[...])[0].astype(BF16)
        p = _nt(h, win_v[...])
        p_ref[...] = p.astype(BF16)
        z = p[:, e:2 * e] * p[:, 2 * e:]
        prev = [tail_v[SUBLANES - 2:SUBLANES - 1, :], tail_v[SUBLANES - 1:SUBLANES, :]]
        conv = (cw_ref[2:3, :] * z + cw_ref[1:2, :] * _shift_down(z, 1, prev[1:])
                + cw_ref[0:1, :] * _shift_down(z, 2, prev))
        tail_v[...] = z[tm - SUBLANES:, :]
        y = (p[:, :e] * conv).astype(BF16)
        xo_ref[...] = xv + _nn(y, wout_v[...])

    return _hosting_call(
        body, "mixer_b_fwd", t_tok // tm, [x, g, conv_w, gath],
        in_specs=[_row_spec(tm, d), _const_spec((1, d)), _const_spec((SUBLANES, e)), ANY],
        out_specs=[_row_spec(tm, d), _row_spec(tm, e3)],
        out_shape=[jax.ShapeDtypeStruct((t_tok, d), F32), jax.ShapeDtypeStruct((t_tok, e3), BF16)],
        scratch=[pltpu.VMEM((e3, d), BF16), pltpu.VMEM((e, d), BF16), pltpu.VMEM((SUBLANES, e), F32),
                 pltpu.SemaphoreType.DMA((2 * N_DEV,))],
        hosted=hosted)


def _mixer_b_bwd(dout, x, p, g, gath, conv_w, tm, seq, hosted=()):
    t_tok, d = x.shape
    e = conv_w.shape[1]
    e3 = 3 * e
    n_in, n_out = e3 // N_DEV, e // N_DEV
    tiles_per_seq = seq // tm
    halo_per_tile = tm // HALO
    n_halo = t_tok // HALO

    def body(dout_ref, dnext_ref, x_ref, p_ref, pprev_ref, pnext_ref, g_ref, cw_ref, gath_ref,
             dx_ref, dxb_ref, h_ref, y_ref, dp_ref, dg_ref, dcw_ref, win_v, wout_v, sems):
        i = pl.program_id(0)
        _load_group([(gath_ref, 0, n_in, win_v), (gath_ref, n_in, n_out, wout_v)], sems)

        @pl.when(i == 0)
        def _():
            dg_ref[...] = jnp.zeros_like(dg_ref)
            dcw_ref[...] = jnp.zeros_like(dcw_ref)

        first = (i % tiles_per_seq == 0).astype(F32)
        last = (i % tiles_per_seq == tiles_per_seq - 1).astype(F32)
        xv = x_ref[...]
        gv = g_ref[...]
        hv, xhat, r = _rms_fwd(xv, gv)
        h_ref[...] = hv.astype(BF16)
        pv = p_ref[...].astype(F32)
        bg, cg, hx = pv[:, :e], pv[:, e:2 * e], pv[:, 2 * e:]
        z = cg * hx
        pprev = pprev_ref[...].astype(F32)
        zprev = pprev[:, e:2 * e] * pprev[:, 2 * e:] * (1.0 - first)
        prev = [zprev[HALO - 2:HALO - 1, :], zprev[HALO - 1:HALO, :]]
        zs1 = _shift_down(z, 1, prev[1:])
        zs2 = _shift_down(z, 2, prev)
        w0, w1, w2 = cw_ref[0:1, :], cw_ref[1:2, :], cw_ref[2:3, :]
        conv = w2 * z + w1 * zs1 + w0 * zs2
        y_ref[...] = (bg * conv).astype(BF16)

        dov = dout_ref[...]
        wout_bf = wout_v[...]
        dy = _nt(dov.astype(BF16), wout_bf)
        dconv = dy * bg
        dnext = _nt(dnext_ref[...].astype(BF16), wout_bf) * pnext_ref[:, :e].astype(F32) * (1.0 - last)
        nxt = [dnext[0:1, :], dnext[1:2, :]]
        dz = w2 * dconv + w1 * _shift_up(dconv, 1, nxt[:1]) + w0 * _shift_up(dconv, 2, nxt)
        dcw_ref[0:1, :] += _col_sum(dconv * zs2)
        dcw_ref[1:2, :] += _col_sum(dconv * zs1)
        dcw_ref[2:3, :] += _col_sum(dconv * z)
        dp = jnp.concatenate([dy * conv, dz * hx, dz * cg], axis=1).astype(BF16)
        dp_ref[...] = dp
        dh = _nn(dp, win_v[...])
        dxr, dg_row = _rms_bwd(dh, xhat, r, gv)
        dg_ref[...] += dg_row
        dx = dov + dxr
        dx_ref[...] = dx
        dxb_ref[...] = dx.astype(BF16)

    prev_spec = lambda w: pl.BlockSpec((HALO, w), lambda i: (jnp.maximum(i * halo_per_tile - 1, 0), 0))
    next_spec = lambda w: pl.BlockSpec((HALO, w), lambda i: (jnp.minimum((i + 1) * halo_per_tile, n_halo - 1), 0))
    return _hosting_call(
        body, "mixer_b_bwd", t_tok // tm, [dout, dout, x, p, p, p, g, conv_w, gath],
        in_specs=[_row_spec(tm, d), next_spec(d), _row_spec(tm, d), _row_spec(tm, e3), prev_spec(e3), next_spec(e3),
                  _const_spec((1, d)), _const_spec((SUBLANES, e)), ANY],
        out_specs=[_row_spec(tm, d), _row_spec(tm, d), _row_spec(tm, d), _row_spec(tm, e), _row_spec(tm, e3),
                   _const_spec((1, d)), _const_spec((SUBLANES, e))],
        out_shape=[jax.ShapeDtypeStruct((t_tok, d), F32), jax.ShapeDtypeStruct((t_tok, d), BF16),
                   jax.ShapeDtypeStruct((t_tok, d), BF16), jax.ShapeDtypeStruct((t_tok, e), BF16),
                   jax.ShapeDtypeStruct((t_tok, e3), BF16), jax.ShapeDtypeStruct((1, d), F32),
                   jax.ShapeDtypeStruct((SUBLANES, e), F32)],
        scratch=[pltpu.VMEM((e3, d), BF16), pltpu.VMEM((e, d), BF16), pltpu.SemaphoreType.DMA((2 * N_DEV,))],
        hosted=hosted)


def _wgrad(a, b, bm, name, hosted=(), part=(1, 0)):
    t_tok, m = a.shape
    n = b.shape[1]
    every, first = part

    def body(a_ref, b_ref, o_ref):
        o_ref[...] = _tn(a_ref[...], b_ref[...]).astype(o_ref.dtype)

    outs, h_outs = _hosting_call(
        body, name, m // (bm * every), [a, b],
        in_specs=[pl.BlockSpec((t_tok, bm), lambda i: (0, every * i + first)), _const_spec((t_tok, n))],
        out_specs=[pl.BlockSpec((bm, n), lambda i: (i, 0))],
        out_shape=[jax.ShapeDtypeStruct((m // every, n), BF16)],
        scratch=[], hosted=hosted)
    return (outs[0], h_outs) if hosted else outs[0]


def _sum_slots(land, rb, name):
    n_slots, rows, cols = land.shape

    def body(l_ref, o_ref):
        acc = l_ref[0].astype(F32)
        for k in range(1, n_slots):
            acc = acc + l_ref[k].astype(F32)
        o_ref[...] = acc

    return pl.pallas_call(
        body, name=name, grid=(rows // rb,),
        in_specs=[pl.BlockSpec((n_slots, rb, cols), lambda i: (0, i, 0))],
        out_specs=pl.BlockSpec((rb, cols), lambda i: (i, 0)),
        out_shape=jax.ShapeDtypeStruct((rows, cols), F32),
        compiler_params=_params(sequential=False),
    )(land)


def _adamw(w, grad, m, v, rb, name):
    rows, cols = w.shape
    c1 = 1.0 / (1.0 - ADAM_B1 ** ADAM_STEP)
    c2 = 1.0 / (1.0 - ADAM_B2 ** ADAM_STEP)

    def body(w_ref, g_ref, m_ref, v_ref, d_ref, mo_ref, vo_ref):
        gv = g_ref[...]
        mn = ADAM_B1 * m_ref[...] + (1.0 - ADAM_B1) * gv
        vn = ADAM_B2 * v_ref[...] + (1.0 - ADAM_B2) * (gv * gv)
        mo_ref[...] = mn
        vo_ref[...] = vn
        d_ref[...] = -ADAM_LR * ((mn * c1) / (jnp.sqrt(vn * c2) + ADAM_EPS) + ADAM_WD * w_ref[...])

    spec = pl.BlockSpec((rb, cols), lambda i: (i, 0))
    shape = jax.ShapeDtypeStruct((rows, cols), F32)
    return pl.pallas_call(
        body, name=name, grid=(rows // rb,),
        in_specs=[spec] * 4, out_specs=[spec] * 3, out_shape=[shape] * 3,
        compiler_params=_params(sequential=False),
    )(w, grad, m, v)


def _pack_shards(groups, name, hosted=()):
    flat = [(part, layer) for group in groups for part, layer, _ in group]
    rows = [[p.shape[2] if turn else p.shape[1] for p, _, turn in group] for group in groups]
    first, _, first_turn = groups[0][0]
    width = first.shape[1] if first_turn else first.shape[2]

    def body(*refs):
        ins, outs = refs[:len(flat)], refs[len(flat):]
        k = 0
        for gi, group in enumerate(groups):
            off = 0
            for (_, _, turn), n in zip(group, rows[gi]):
                part = ins[k][...].astype(BF16)
                if turn:
                    r = lax.broadcasted_iota(jnp.int32, (n, n), 0)
                    c = lax.broadcasted_iota(jnp.int32, (n, n), 1)
                    part = _nt((r == c).astype(BF16), part).astype(BF16)
                outs[gi][off:off + n, :] = part
                off += n
                k += 1

    return _hosting_call(
        body, name, 1, [p for p, _ in flat],
        in_specs=[pl.BlockSpec((None,) + p.shape[1:], lambda i, layer=layer: (layer, 0, 0)) for p, layer in flat],
        out_specs=[_const_spec((sum(r), width)) for r in rows],
        out_shape=[jax.ShapeDtypeStruct((sum(r), width), BF16) for r in rows],
        scratch=[], hosted=hosted)


def _split_bf16(a):
    hi = a.astype(BF16)
    rest = a - hi.astype(F32)
    mid = rest.astype(BF16)
    return hi, mid, (rest - mid.astype(F32)).astype(BF16)


def _reduce_adamw(lands, w, m, v, transpose, name, hosted=()):
    n_layers, rows_w, cols_w = w.shape
    c1 = 1.0 / (1.0 - ADAM_B1 ** ADAM_STEP)
    c2 = 1.0 / (1.0 - ADAM_B2 ** ADAM_STEP)
    flat = [piece for pieces in lands for piece in pieces]
    counts = [len(pieces) for pieces in lands]
    if transpose:
        tiles = rows_w // MXU_WIDTH
        blk = (MXU_WIDTH, cols_w)
        land_specs = [pl.BlockSpec((N_CHIP, n, MXU_WIDTH), lambda i, b=first // n: (0, b, i % tiles))
                      for _, first, n in flat]
        for _, first, n in flat:
            assert first % n == 0
    else:
        tiles = 2
        blk = (rows_w // tiles, cols_w)
        assert all(c == 1 for c in counts)
        land_specs = [pl.BlockSpec((N_CHIP,) + blk, lambda i, b=first // blk[0]: (0, b + i % tiles, 0))
                      for _, first, _ in flat]
        for _, first, _ in flat:
            assert first % blk[0] == 0

    def body(*refs):
        land_refs = refs[:len(flat)]
        w_ref, m_ref, v_ref, g_ref, d_ref, mo_ref, vo_ref = refs[len(flat):]
        layer = pl.program_id(0) // tiles

        def total(ref):
            acc = ref[0].astype(F32)
            for q in range(1, N_CHIP):
                acc = acc + ref[q].astype(F32)
            return acc

        def layer_sum(k):
            first = sum(counts[:k])
            parts = [total(land_refs[first + j]) for j in range(counts[k])]
            return parts[0] if len(parts) == 1 else jnp.concatenate(parts, axis=0)

        gv = layer_sum(0)
        for k in range(1, n_layers):
            gv = jnp.where(layer == k, layer_sum(k), gv)
        if transpose:
            r = lax.broadcasted_iota(jnp.int32, (MXU_WIDTH, MXU_WIDTH), 0)
            c = lax.broadcasted_iota(jnp.int32, (MXU_WIDTH, MXU_WIDTH), 1)
            eye = (r == c).astype(BF16)
            hi, mid, lo = _split_bf16(gv)
            gv = _nt(eye, hi) + _nt(eye, mid) + _nt(eye, lo)
        g_ref[...] = gv
        mn = ADAM_B1 * m_ref[...] + (1.0 - ADAM_B1) * gv
        vn = ADAM_B2 * v_ref[...] + (1.0 - ADAM_B2) * (gv * gv)
        mo_ref[...] = mn
        vo_ref[...] = vn
        d_ref[...] = -ADAM_LR * ((mn * c1) / (jnp.sqrt(vn * c2) + ADAM_EPS) + ADAM_WD * w_ref[...])

    spec = pl.BlockSpec((None,) + blk, lambda i: (i // tiles, i % tiles, 0))
    shape = jax.ShapeDtypeStruct(w.shape, F32)
    outs, h_outs = _hosting_call(
        body, name, n_layers * tiles, [land for land, _, _ in flat] + [w, m, v],
        in_specs=land_specs + [spec] * 3, out_specs=[spec] * 4, out_shape=[shape] * 4, scratch=[], hosted=hosted)
    return (outs, h_outs) if hosted else outs


def _pack_small(parts, rows):
    flat = jnp.concatenate([p.reshape(-1).astype(F32) for p in parts])
    return jnp.pad(flat, (0, rows * LANES - flat.shape[0])).reshape(rows, LANES)


def _unpack_small(packed, shapes):
    flat = packed.reshape(-1)
    out = []
    pos = 0
    for s in shapes:
        n = math.prod(s)
        out.append(flat[pos:pos + n].reshape(s))
        pos += n
    return out


def kernel(x, mix_norm, ffn_norm, a_w_in, a_v_gain, a_v_bias, a_w_s, a_b_s, a_w_out, b_w_in, b_conv_w, b_w_out, ffn_w_gate, ffn_w_up, ffn_w_down, final_norm, loss_target, m_mix_norm, m_ffn_norm, m_a_w_in, m_a_v_gain, m_a_v_bias, m_a_w_s, m_a_b_s, m_a_w_out, m_b_w_in, m_b_conv_w, m_b_w_out, m_ffn_w_gate, m_ffn_w_up, m_ffn_w_down, m_final_norm, v_mix_norm, v_ffn_norm, v_a_w_in, v_a_v_gain, v_a_v_bias, v_a_w_s, v_a_b_s, v_a_w_out, v_b_w_in, v_b_conv_w, v_b_w_out, v_ffn_w_gate, v_ffn_w_up, v_ffn_w_down, v_final_norm):
    bsz, seq, d = x.shape
    t_tok = bsz * seq
    me = _my_index()
    xt = x.reshape(t_tok, d)
    target = loss_target.reshape(t_tok, d)
    e_a = a_v_gain.shape[1]
    e_b = b_w_out.shape[1] * N_DEV
    n_layers = ffn_w_gate.shape[0]
    f_shard = ffn_w_gate.shape[2]
    f_full = f_shard * N_DEV

    conv_pad = jnp.pad(b_conv_w[0], ((0, SUBLANES - CONV_W), (0, 0)))
    sh_a = jnp.concatenate([a_w_in[0].T, a_w_out[0]]).astype(BF16)
    bfull = jnp.repeat(a_b_s[0].T, GROUP, axis=1)

    gate_t, up_t = ffn_w_gate.transpose(0, 2, 1), ffn_w_up.transpose(0, 2, 1)
    (sh_b, sh_f0, sh_f1g, sh_f1ud), (gath_a, conv_g) = _pack_shards(
        [[(b_w_in, 0, True), (b_w_out, 0, False)],
         [(gate_t, 0, False), (up_t, 0, False), (ffn_w_down, 0, False)],
         [(gate_t, 1, False)],
         [(up_t, 1, False), (ffn_w_down, 1, False)]],
        "pack_shards", hosted=[_HostedGathers([sh_a, conv_pad])])
    conv_full = jnp.pad(conv_g[:, :CONV_W, :].transpose(1, 0, 2).reshape(CONV_W, e_b), ((0, SUBLANES - CONV_W), (0, 0)))
    (x1, gd_a, u_a, vhat_a, sv_a, y_a, rstd_a), (gath_f0,) = _mixer_a_fwd(
        xt, mix_norm[0:1], gath_a, a_v_gain, a_v_bias, a_w_s[0], bfull, tm=TOKEN_TILE,
        hosted=[_HostedGathers([sh_f0])])
    srcs0 = [(gath_f0, 0), (gath_f0, f_shard), (gath_f0, 2 * f_shard)]
    (x2, gate0, up0), (gath_b, gath_f1g) = _ffn_fwd(x1, ffn_norm[0:1], srcs0, f_shard, tm=TOKEN_TILE, name="ffn_fwd0",
                                                    hosted=[_HostedGathers([sh_b, sh_f1g])])
    (x3, p_b), (gath_f1ud,) = _mixer_b_fwd(x2, mix_norm[1:2], gath_b, conv_full, tm=TOKEN_TILE, seq=seq,
                                           hosted=[_HostedGathers([sh_f1ud])])
    srcs1 = [(gath_f1g, 0), (gath_f1ud, 0), (gath_f1ud, f_shard)]
    (loss_part, dx4, dx4_bf, d_final, gate1, up1), _ = _ffn_fwd(
        x3, ffn_norm[1:2], srcs1, f_shard, tm=TOKEN_TILE_WIDE, name="ffn_fwd1", head=(target, final_norm.reshape(1, d)))

    ffn_entries = [(0, 0, f_shard), (0, f_full, f_shard), (1, 0, f_shard)]
    (dx3, dx3_bf, h_f1, act1, dgu1, d_fn1), _ = _ffn_bwd(dx4, x3, gate1, up1, ffn_norm[1:2], srcs1, f_shard, tm=TOKEN_TILE,
                                                         name="ffn_bwd1")
    g_down1 = _wgrad(act1, dx4_bf, WGRAD_ROWS, "wgrad_down1")
    g_gu1 = _wgrad(dgu1, h_f1, WGRAD_ROWS_WIDE, "wgrad_gate_up1")
    ps_f1 = _pair_reduce([g_gu1, g_down1], ffn_entries, "pair_reduce_f1")
    (dx2, dx2_bf, h_b, y_b, dp_b, d_mn1, d_conv), (land_f1gu,) = _mixer_b_bwd(
        dx3, x2, p_b, mix_norm[1:2], gath_b, conv_full, tm=TOKEN_TILE_WIDE, seq=seq,
        hosted=[_HostedChipScatter(ps_f1, 0, 2 * f_shard)])
    g_b_out = _wgrad(y_b, dx3_bf, WGRAD_ROWS_WIDE, "wgrad_b_out")
    g_b_in = _wgrad(dp_b, h_b, WGRAD_ROWS_WIDE, "wgrad_b_in")
    ps_b = _pair_reduce([g_b_in, g_b_out], [(0, 0, b_w_in.shape[2]), (1, 0, b_w_out.shape[1])], "pair_reduce_b")
    (dx1, dx1_bf, h_f0, act0, dgu0, d_fn0), (land_f1d, land_b) = _ffn_bwd(
        dx2, x1, gate0, up0, ffn_norm[0:1], srcs0, f_shard, tm=TOKEN_TILE, name="ffn_bwd0",
        hosted=[_HostedChipScatter(ps_f1, 2 * f_shard, f_shard), _HostedChipScatter(ps_b)])
    g_down0 = _wgrad(act0, dx2_bf, WGRAD_ROWS, "wgrad_down0")
    g_gu0 = _wgrad(dgu0, h_f0, WGRAD_ROWS_WIDE, "wgrad_gate_up0")
    g_a_out = _wgrad(y_a, dx1_bf, WGRAD_ROWS_WIDE, "wgrad_a_out")
    n_ao = a_w_out.shape[1]
    ps_f0ao = _pair_reduce([g_gu0, g_down0, g_a_out], ffn_entries + [(2, 0, n_ao)], "pair_reduce_f0_a_out")
    (dx0, _, h_a, dz_a, d_mn0, d_gain, d_bias, d_ws, d_bs_acc), (land_f0, land_ao) = _mixer_a_bwd(
        dx1, xt, gd_a, u_a, vhat_a, sv_a, rstd_a, mix_norm[0:1], gath_a, a_v_gain, a_v_bias, a_w_s[0], tm=TOKEN_TILE,
        hosted=[_HostedChipScatter(ps_f0ao, 0, 3 * f_shard), _HostedChipScatter(ps_f0ao, 3 * f_shard, n_ao)])
    d_bs = d_bs_acc.reshape(HEADS, CHUNK)

    small_grads = [jnp.concatenate([d_mn0, d_mn1]), jnp.concatenate([d_fn0, d_fn1]), d_gain, d_bias, d_ws, d_bs,
                   d_final, d_conv[:CONV_W], loss_part]
    small_shapes = [(n_layers, d), (n_layers, d), (1, e_a), (1, e_a), (1, HEADS, CHUNK, CHUNK), (1, HEADS, CHUNK), (d,),
                    (CONV_W, e_b), ()]
    n_small = sum(math.prod(s) for s in small_shapes)
    blk_rows = -(-n_small // (N_DEV * LANES * SUBLANES)) * SUBLANES
    small_rows = blk_rows * N_DEV
    packed = _pack_small(small_grads, small_rows)
    n_half = a_w_in.shape[2] // 2
    g_ai0, (small_land,) = _wgrad(dz_a, h_a, n_half, "wgrad_a_in0", hosted=[_HostedScatterAll(packed)], part=(2, 0))
    ps_ai0 = _pair_reduce([g_ai0], [(0, 0, n_half)], "pair_reduce_a_in0")
    small_sum = _sum_slots(small_land, blk_rows, "sum_small")
    g_ai1, (land_ai0,) = _wgrad(dz_a, h_a, n_half, "wgrad_a_in1", hosted=[_HostedChipScatter(ps_ai0)], part=(2, 1))
    ps_ai1 = _pair_reduce([g_ai1], [(0, 0, n_half)], "pair_reduce_a_in1")
    land_ai1, small_gath = _exchange([_HostedChipScatter(ps_ai1), _HostedGathers([small_sum])], "tail_exchange")
    small_all = small_gath.reshape(small_rows, LANES)

    n_b_in = b_w_in.shape[2]
    gate_out = _reduce_adamw([[(land_f0, 0, f_shard)], [(land_f1gu, 0, f_shard)]], gate_t,
                             m_ffn_w_gate.transpose(0, 2, 1), v_ffn_w_gate.transpose(0, 2, 1), False, "adamw_gate")
    up_out = _reduce_adamw([[(land_f0, f_shard, f_shard)], [(land_f1gu, f_shard, f_shard)]], up_t,
                           m_ffn_w_up.transpose(0, 2, 1), v_ffn_w_up.transpose(0, 2, 1), False, "adamw_up")
    res = {
        "a_w_in": _reduce_adamw([[(land_ai0, 0, n_half), (land_ai1, 0, n_half)]], a_w_in, m_a_w_in, v_a_w_in, True,
                                "adamw_a_in"),
        "a_w_out": _reduce_adamw([[(land_ao, 0, a_w_out.shape[1])]], a_w_out, m_a_w_out, v_a_w_out, False,
                                 "adamw_a_out"),
        "b_w_in": _reduce_adamw([[(land_b, 0, n_b_in)]], b_w_in, m_b_w_in, v_b_w_in, True, "adamw_b_in"),
        "b_w_out": _reduce_adamw([[(land_b, n_b_in, b_w_out.shape[1])]], b_w_out, m_b_w_out, v_b_w_out, False,
                                 "adamw_b_out"),
        "ffn_w_gate": [o.transpose(0, 2, 1) for o in gate_out],
        "ffn_w_up": [o.transpose(0, 2, 1) for o in up_out],
        "ffn_w_down": _reduce_adamw([[(land_f0, 2 * f_shard, f_shard)], [(land_f1d, 0, f_shard)]], ffn_w_down,
                                    m_ffn_w_down, v_ffn_w_down, False, "adamw_down"),
    }

    (gr_mix, gr_ffn, gr_gain, gr_bias, gr_ws, gr_bs, gr_final, gr_conv_full, loss) = _unpack_small(small_all, small_shapes)
    gr_conv = lax.dynamic_slice_in_dim(gr_conv_full, me * (e_b // N_DEV), e_b // N_DEV, axis=1)[None]

    small_w =[mix_norm, ffn_norm, a_v_gain, a_v_bias, a_w_s, a_b_s, final_norm]
    small_m = [m_mix_norm, m_ffn_norm, m_a_v_gain, m_a_v_bias, m_a_w_s, m_a_b_s, m_final_norm]
    small_v = [v_mix_norm, v_ffn_norm, v_a_v_gain, v_a_v_bias, v_a_w_s, v_a_b_s, v_final_norm]
    small_g = [gr_mix, gr_ffn, gr_gain, gr_bias, gr_ws, gr_bs, gr_final]
    sm_shapes = small_shapes[:len(small_w)]
    sm_out = _adamw(_pack_small(small_w, small_rows), _pack_small(small_g, small_rows), _pack_small(small_m, small_rows),
                    _pack_small(small_v, small_rows), small_rows, "adamw_small")
    sm_delta, sm_m, sm_v = [_unpack_small(o, sm_shapes) for o in sm_out]

    conv_out = _adamw(b_conv_w[0], gr_conv[0], m_b_conv_w[0], v_b_conv_w[0], CONV_W, "adamw_conv")
    conv_delta, conv_m, conv_v = [o[None] for o in conv_out]

    order = ["mix_norm", "ffn_norm", "a_w_in", "a_v_gain", "a_v_bias", "a_w_s", "a_b_s", "a_w_out", "b_w_in",
             "b_conv_w", "b_w_out", "ffn_w_gate", "ffn_w_up", "ffn_w_down", "final_norm"]
    small_names = ["mix_norm", "ffn_norm", "a_v_gain", "a_v_bias", "a_w_s", "a_b_s", "final_norm"]
    grads = {"b_conv_w": gr_conv}
    deltas, new_m, new_v = {}, {}, {}
    for k, name in enumerate(small_names):
        grads[name] = small_g[k]
        deltas[name], new_m[name], new_v[name] = sm_delta[k], sm_m[k], sm_v[k]
    deltas["b_conv_w"], new_m["b_conv_w"], new_v["b_conv_w"] = conv_delta, conv_m, conv_v
    for name, (gg, dl, mm, vv) in res.items():
        grads[name], deltas[name], new_m[name], new_v[name] = gg, dl, mm, vv

    grad_x = dx0.reshape(bsz, seq, d)
    return (loss, grad_x, *[grads[n] for n in order], *[deltas[n] for n in order],
            *[new_m[n] for n in order], *[new_v[n] for n in order])
```

```python
import math

import jax
import jax.numpy as jnp
from jax import lax
from jax.experimental import pallas as pl
from jax.experimental.pallas import tpu as pltpu

F32 = jnp.float32
BF16 = jnp.bfloat16

N_DEV = 8
N_CHIP = 4
CHUNK = 128
HEADS = 16
GROUP = 128
CONV_W = 3
NORM_EPS = 1e-6
GELU_C = math.sqrt(2.0 / math.pi)
GELU_K = 0.044715

ADAM_LR = 0.001
ADAM_B1 = 0.9
ADAM_B2 = 0.999
ADAM_EPS = 1e-08
ADAM_WD = 0.01
ADAM_STEP = 10

LANES = 128
SUBLANES = 8
VMEM_LIMIT = 60 * 1024 * 1024
HALO = 16
MXU_WIDTH = 256
FFN_CHUNKS = 2
TOKEN_TILE = 256
TOKEN_TILE_WIDE = 512
WGRAD_ROWS = 256
WGRAD_ROWS_WIDE = 512
GATHER_RELAY_AT = 0.56
GATHER_FORWARD_LEAD = 2

MESH = pl.DeviceIdType.MESH
ANY = pl.BlockSpec(memory_space=pl.ANY)

PEER_FLIPS = {"sibling": (0, 0, 1), "x": (1, 0, 0), "y": (0, 1, 0), "diagonal": (1, 1, 0),
              "x_other": (1, 0, 1), "y_other": (0, 1, 1), "diagonal_other": (1, 1, 1)}
COLLECTIVE_IDS = {frozenset(["sibling"]): 0,
                  frozenset(["sibling", "x", "y"]): 1,
                  frozenset(["sibling", "x", "y", "diagonal"]): 2,
                  frozenset(["x", "y", "diagonal"]): 3,
                  frozenset(PEER_FLIPS): 4}


def _params(sequential=True):
    return pltpu.CompilerParams(
        dimension_semantics=("arbitrary",) if sequential else None,
        vmem_limit_bytes=VMEM_LIMIT)


def _nn(a, b):
    return jnp.dot(a, b, preferred_element_type=F32)


def _nt(a, b):
    return lax.dot_general(a, b, (((1,), (1,)), ((), ())), preferred_element_type=F32)


def _tn(a, b):
    return lax.dot_general(a, b, (((0,), (0,)), ((), ())), preferred_element_type=F32)


def _row_mean(a):
    return jnp.mean(a, axis=-1, keepdims=True)


def _col_sum(a):
    return jnp.sum(a, axis=0, keepdims=True)


def _rms_fwd(x, g):
    r = lax.rsqrt(_row_mean(x * x) + NORM_EPS)
    xhat = x * r
    return xhat * g, xhat, r


def _rms_bwd(dh, xhat, r, g):
    a = dh * g
    dx = r * (a - xhat * _row_mean(a * xhat))
    return dx, _col_sum(dh * xhat)


def _gelu_and_grad(x):
    x2 = x * x
    t = jnp.tanh(x * (GELU_C + (GELU_C * GELU_K) * x2))
    half = 0.5 * t + 0.5
    d = half + x * (0.5 - 0.5 * (t * t)) * (GELU_C + (3.0 * GELU_C * GELU_K) * x2)
    return x * half, d


def _sigmoid(x):
    return 1.0 / (1.0 + jnp.exp(-x))


def _row_spec(tm, width):
    return pl.BlockSpec((tm, width), lambda i: (i, 0))


def _const_spec(shape):
    nd = len(shape)
    return pl.BlockSpec(shape, lambda i: (0,) * nd)


def _load_group(parts, sems):
    @pl.when(pl.program_id(0) == 0)
    def _():
        copies = []
        for k, (gath_ref, first, n, dst) in enumerate(parts):
            for j in range(N_DEV):
                copies.append(pltpu.make_async_copy(gath_ref.at[j, pl.ds(first, n), :], dst.at[pl.ds(j * n, n), :],
                                                    sems.at[k * N_DEV + j]))
        for cp in copies:
            cp.start()
        for cp in copies:
            cp.wait()


def _hosting_call(body, name, n_steps, arrays, in_specs, out_specs, out_shape, scratch, hosted=()):
    n_in, n_out, n_scr = len(arrays), len(out_shape), len(scratch)
    h_arrays = [a for h in hosted for a in h.arrays]
    h_shapes = [s for h in hosted for s in h.out_shapes]
    h_sems = [s for h in hosted for s in h.sem_shapes]
    peers = sorted(set().union(*[h.peers for h in hosted])) if hosted else []

    def handshake():
        @pl.when(pl.program_id(0) == 0)
        def _():
            x, y, c = lax.axis_index("x"), lax.axis_index("y"), lax.axis_index("c")
            barrier = pltpu.get_barrier_semaphore()
            for p in peers:
                fx, fy, fc = PEER_FLIPS[p]
                peer = (1 - x if fx else x, 1 - y if fy else y, 1 - c if fc else c)
                pl.semaphore_signal(barrier, inc=1, device_id=peer, device_id_type=MESH)
            pl.semaphore_wait(barrier, len(peers))

    def full_body(*refs):
        pos = 0
        groups = []
        for n in (n_in, len(h_arrays), n_out, len(h_shapes), n_scr, len(h_sems)):
            groups.append(refs[pos:pos + n])
            pos += n
        own_in, h_in, own_out, h_out, own_scr, h_sem = groups
        per_host = []
        pi = po = ps = 0
        for h in hosted:
            ni, no, ns = len(h.arrays), len(h.out_shapes), len(h.sem_shapes)
            per_host.append((h, h_in[pi:pi + ni], h_out[po:po + no], h_sem[ps:ps + ns]))
            pi, po, ps = pi + ni, po + no, ps + ns
        if hosted:
            handshake()
        for h, ins, outs, sems in per_host:
            h.begin(ins, outs, sems, n_steps)
        body(*own_in, *own_out, *own_scr)
        for h, ins, outs, sems in per_host:
            h.end(ins, outs, sems, n_steps)

    outs = pl.pallas_call(
        full_body, name=name, grid=(n_steps,),
        in_specs=list(in_specs) + [ANY] * len(h_arrays),
        out_specs=list(out_specs) + [ANY] * len(h_shapes),
        out_shape=list(out_shape) + h_shapes,
        scratch_shapes=list(scratch) + h_sems,
        compiler_params=pltpu.CompilerParams(
            dimension_semantics=("arbitrary",), vmem_limit_bytes=VMEM_LIMIT,
            collective_id=COLLECTIVE_IDS[frozenset(peers)] if hosted else None),
    )(*arrays, *h_arrays)
    return outs[:n_out], outs[n_out:]


def _my_index():
    return 4 * lax.axis_index("x") + 2 * lax.axis_index("y") + lax.axis_index("c")


GATHER_COPIES = 8


def _gather_relays(n_rows, dtype):
    rows_per_tile = SUBLANES * 4 // jnp.dtype(dtype).itemsize
    return n_rows % 2 == 0 and (n_rows // 2) % rows_per_tile == 0


class _Gather:
    def __init__(self, shard, out, send_sems, recv_sems, local_sem):
        self.shard, self.out = shard, out
        self.send_sems, self.recv_sems, self.local_sem = send_sems, recv_sems, local_sem
        x, y, c = lax.axis_index("x"), lax.axis_index("y"), lax.axis_index("c")
        self.c = c
        self.me, self.sibling = (x, y, c), (x, y, 1 - c)
        self.xn, self.yn, self.dg = (1 - x, y), (x, 1 - y), (1 - x, 1 - y)
        self.n = shard.shape[0]
        self.half = self.n // 2
        self.relays = _gather_relays(self.n, shard.dtype)

    def _slot(self, dev, lo=0, hi=None):
        hi = self.n if hi is None else hi
        return self.out.at[4 * dev[0] + 2 * dev[1] + dev[2], pl.ds(lo, hi - lo), :]

    def _copy(self, k, block, to, src=None, lo=0, hi=None):
        return pltpu.make_async_remote_copy(
            src_ref=self._slot(block, lo, hi) if src is None else src, dst_ref=self._slot(block, lo, hi),
            send_sem=self.send_sems.at[k], recv_sem=self.recv_sems.at[k], device_id=to, device_id_type=MESH)

    def _local(self):
        return pltpu.make_async_copy(self.shard, self._slot(self.me), self.local_sem)

    def start(self):
        c = self.c
        self._local().start()
        self._copy(0, self.me, self.sibling, src=self.shard).start()
        self._copy(1, self.me, (*self.xn, c), src=self.shard).start()
        self._copy(2, self.me, (*self.yn, c), src=self.shard).start()
        if not self.relays:
            self._copy(3, self.me, (*self.dg, c), src=self.shard).start()

    def relay(self):
        c = self.c
        if self.relays:
            self._copy(1, (*self.xn, c), self.me).wait_recv()
            self._copy(3, (*self.xn, c), (*self.yn, c), hi=self.half).start()
            self._copy(2, (*self.yn, c), self.me).wait_recv()
            self._copy(4, (*self.yn, c), (*self.xn, c), lo=self.half).start()

    def forward(self):
        c = self.c
        if self.relays:
            self._copy(5, (*self.xn, c), self.sibling).start()
            self._copy(6, (*self.yn, c), self.sibling).start()
            self._copy(3, (*self.dg, c), self.me, hi=self.half).wait_recv()
            self._copy(4, (*self.dg, c), self.me, lo=self.half).wait_recv()
        else:
            self._copy(1, (*self.xn, c), self.me).wait_recv()
            self._copy(5, (*self.xn, c), self.sibling).start()
            self._copy(2, (*self.yn, c), self.me).wait_recv()
            self._copy(6, (*self.yn, c), self.sibling).start()
            self._copy(3, (*self.dg, c), self.me).wait_recv()
        self._copy(7, (*self.dg, c), self.sibling).start()

    def finish(self):
        c = self.c
        self._copy(0, self.sibling, self.me).wait_recv()
        for k, chip in ((5, self.xn), (6, self.yn), (7, self.dg)):
            self._copy(k, (*chip, 1 - c), self.me).wait_recv()
        for k in (0, 1, 2, 5, 6, 7):
            self._copy(k, self.me, self.sibling).wait_send()
        if self.relays:
            self._copy(3, self.me, self.sibling, hi=self.half).wait_send()
            self._copy(4, self.me, self.sibling, lo=self.half).wait_send()
        else:
            self._copy(3, self.me, self.sibling).wait_send()
        self._local().wait()


class _HostedGathers:
    def __init__(self, shards, mid_lead=GATHER_FORWARD_LEAD, relay_at=GATHER_RELAY_AT):
        n = len(shards)
        self.arrays = shards
        self.mid_lead, self.relay_at = mid_lead, relay_at
        self.peers = {"sibling", "x", "y"}
        if not all(_gather_relays(s.shape[0], s.dtype) for s in shards):
            self.peers.add("diagonal")
        self.out_shapes = [jax.ShapeDtypeStruct((N_DEV,) + s.shape, s.dtype) for s in shards]
        self.sem_shapes = [pltpu.SemaphoreType.DMA((n, GATHER_COPIES)), pltpu.SemaphoreType.DMA((n, GATHER_COPIES)),
                           pltpu.SemaphoreType.DMA((n,))]

    def _gathers(self, ins, outs, sems):
        return [_Gather(ins[a], outs[a], sems[0].at[a], sems[1].at[a], sems[2].at[a]) for a in range(len(ins))]

    def begin(self, ins, outs, sems, n_steps):
        i = pl.program_id(0)
        forward_step = max(n_steps - 1 - self.mid_lead, 0)
        relay_step = min(int(self.relay_at * n_steps), forward_step)

        @pl.when(i == 0)
        def _():
            for g in self._gathers(ins, outs, sems):
                g.start()

        if n_steps == 1:
            return

        @pl.when(i == relay_step)
        def _():
            for g in self._gathers(ins, outs, sems):
                g.relay()

        @pl.when(i == forward_step)
        def _():
            for g in self._gathers(ins, outs, sems):
                g.forward()

    def end(self, ins, outs, sems, n_steps):
        @pl.when(pl.program_id(0) == n_steps - 1)
        def _():
            gathers = self._gathers(ins, outs, sems)
            if n_steps == 1:
                for g in gathers:
                    g.relay()
                for g in gathers:
                    g.forward()
            for g in gathers:
                g.finish()


def _exchange(hosted, name):
    return _hosting_call(lambda: None, name, 1, [], [], [], [], [], hosted=hosted)[1]


class _ChipScatter:
    def __init__(self, pairsum, row0, land, send_sems, recv_sems, local_sem):
        self.pairsum, self.row0, self.land = pairsum, row0, land
        self.send_sems, self.recv_sems, self.local_sem = send_sems, recv_sems, local_sem
        x, y, c = lax.axis_index("x"), lax.axis_index("y"), lax.axis_index("c")
        self.c = c
        self.chip = 2 * x + y
        self.others = [(1 - x, y), (x, 1 - y), (1 - x, 1 - y)]

    def _src(self, chip):
        return self.pairsum.at[chip, pl.ds(self.row0, self.land.shape[1]), :]

    def _copy(self, k):
        ox, oy = self.others[k]
        return pltpu.make_async_remote_copy(
            src_ref=self._src(2 * ox + oy), dst_ref=self.land.at[self.chip],
            send_sem=self.send_sems.at[k], recv_sem=self.recv_sems.at[k], device_id=(ox, oy, self.c),
            device_id_type=MESH)

    def _arrival(self, k):
        ox, oy = self.others[k]
        return pltpu.make_async_remote_copy(
            src_ref=self._src(self.chip), dst_ref=self.land.at[2 * ox + oy],
            send_sem=self.send_sems.at[k], recv_sem=self.recv_sems.at[k], device_id=(ox, oy, self.c),
            device_id_type=MESH)

    def _local(self):
        return pltpu.make_async_copy(self._src(self.chip), self.land.at[self.chip], self.local_sem)

    def start(self):
        self._local().start()
        for k in range(N_CHIP - 1):
            self._copy(k).start()

    def finish(self):
        for k in range(N_CHIP - 1):
            self._arrival(k).wait_recv()
        for k in range(N_CHIP - 1):
            self._copy(k).wait_send()
        self._local().wait()


class _HostedChipScatter:
    def __init__(self, pairsum, row0=0, n=None):
        n = pairsum.shape[1] - row0 if n is None else n
        self.row0 = row0
        self.peers = {"x", "y", "diagonal"}
        self.arrays = [pairsum]
        self.out_shapes = [jax.ShapeDtypeStruct((N_CHIP, n, pairsum.shape[2]), pairsum.dtype)]
        self.sem_shapes = [pltpu.SemaphoreType.DMA((N_CHIP - 1,)), pltpu.SemaphoreType.DMA((N_CHIP - 1,)),
                           pltpu.SemaphoreType.DMA(())]

    def begin(self, ins, outs, sems, n_steps):
        @pl.when(pl.program_id(0) == 0)
        def _():
            _ChipScatter(ins[0], self.row0, outs[0], *sems).start()

    def end(self, ins, outs, sems, n_steps):
        @pl.when(pl.program_id(0) == n_steps - 1)
        def _():
            _ChipScatter(ins[0], self.row0, outs[0], *sems).finish()


def _pair_reduce(arrays, entries, name):
    n_arr, n_ent = len(arrays), len(entries)
    cols = arrays[0].shape[1]
    offsets = []
    total = 0
    for _, _, n in entries:
        offsets.append(total)
        total += n

    def body(*refs):
        ins, out_ref = refs[:n_arr], refs[n_arr]
        rbuf, own, send_sems, recv_sems, own_sems = refs[n_arr + 1:]
        q = pl.program_id(0)
        x, y, c = lax.axis_index("x"), lax.axis_index("y"), lax.axis_index("c")

        def block(e, chip, core):
            ai, first, n = entries[e]
            return ins[ai].at[pl.ds(first + (2 * chip + core) * n, n), :]

        def to_sibling(e, chip):
            return pltpu.make_async_remote_copy(
                src_ref=block(e, chip, 1 - c), dst_ref=rbuf.at[chip, pl.ds(offsets[e], entries[e][2]), :],
                send_sem=send_sems.at[e, chip], recv_sem=recv_sems.at[e, chip], device_id=(x, y, 1 - c),
                device_id_type=MESH)

        def own_block(e, chip):
            return pltpu.make_async_copy(block(e, chip, c), own.at[chip, pl.ds(offsets[e], entries[e][2]), :],
                                         own_sems.at[e, chip])

        @pl.when(q == 0)
        def _():
            barrier = pltpu.get_barrier_semaphore()
            pl.semaphore_signal(barrier, inc=1, device_id=(x, y, 1 - c), device_id_type=MESH)
            pl.semaphore_wait(barrier, 1)
            for chip in range(N_CHIP):
                for e in range(n_ent):
                    to_sibling(e, chip).start()
            for chip in range(N_CHIP):
                for e in range(n_ent):
                    own_block(e, chip).start()

        for e in range(n_ent):
            own_block(e, q).wait()
            to_sibling(e, q).wait_recv()
        out_ref[...] = (own[q].astype(F32) + rbuf[q].astype(F32)).astype(out_ref.dtype)

        @pl.when(q == N_CHIP - 1)
        def _():
            for chip in range(N_CHIP):
                for e in range(n_ent):
                    to_sibling(e, chip).wait_send()

    return pl.pallas_call(
        body, name=name, grid=(N_CHIP,),
        in_specs=[ANY] * n_arr,
        out_specs=pl.BlockSpec((None, total, cols), lambda q: (q, 0, 0)),
        out_shape=jax.ShapeDtypeStruct((N_CHIP, total, cols), BF16),
        scratch_shapes=[pltpu.VMEM((N_CHIP, total, cols), BF16), pltpu.VMEM((N_CHIP, total, cols), BF16),
                        pltpu.SemaphoreType.DMA((n_ent, N_CHIP)), pltpu.SemaphoreType.DMA((n_ent, N_CHIP)),
                        pltpu.SemaphoreType.DMA((n_ent, N_CHIP))],
        compiler_params=pltpu.CompilerParams(dimension_semantics=("arbitrary",), vmem_limit_bytes=VMEM_LIMIT,
                                             collective_id=COLLECTIVE_IDS[frozenset(["sibling"])]),
    )(*arrays)


class _HostedScatterAll:
    def __init__(self, packed):
        n = packed.shape[0] // N_DEV
        self.n = n
        self.peers = set(PEER_FLIPS)
        self.arrays = [packed]
        self.out_shapes = [jax.ShapeDtypeStruct((N_DEV, n, packed.shape[1]), packed.dtype)]
        self.sem_shapes = [pltpu.SemaphoreType.DMA((N_DEV - 1,)), pltpu.SemaphoreType.DMA((N_DEV - 1,)),
                           pltpu.SemaphoreType.DMA(())]

    def _copies(self, ins, outs, sems, with_arrivals):
        src, land = ins[0], outs[0]
        send_sems, recv_sems, local_sem = sems
        me = _my_index()

        def block(p):
            return src.at[pl.ds(p * self.n, self.n), :]

        local = pltpu.make_async_copy(block(me), land.at[me], local_sem)
        sends, arrivals = [], []
        for k in range(1, N_DEV):
            p = (me + k) % N_DEV
            q = (me + N_DEV - k) % N_DEV
            sends.append(pltpu.make_async_remote_copy(
                src_ref=block(p), dst_ref=land.at[me], send_sem=send_sems.at[k - 1], recv_sem=recv_sems.at[k - 1],
                device_id=(p // 4, (p // 2) % 2, p % 2), device_id_type=MESH))
            if with_arrivals:
                arrivals.append(pltpu.make_async_remote_copy(
                    src_ref=block(me), dst_ref=land.at[q], send_sem=send_sems.at[k - 1], recv_sem=recv_sems.at[k - 1],
                    device_id=(q // 4, (q // 2) % 2, q % 2), device_id_type=MESH))
        return local, sends, arrivals

    def begin(self, ins, outs, sems, n_steps):
        @pl.when(pl.program_id(0) == 0)
        def _():
            local, sends, _ = self._copies(ins, outs, sems, with_arrivals=False)
            local.start()
            for cp in sends:
                cp.start()

    def end(self, ins, outs, sems, n_steps):
        @pl.when(pl.program_id(0) == n_steps - 1)
        def _():
            local, sends, arrivals = self._copies(ins, outs, sems, with_arrivals=True)
            for cp in arrivals:
                cp.wait_recv()
            for cp in sends:
                cp.wait_send()
            local.wait()


def _tril_weights(ws_ref):
    r = lax.broadcasted_iota(jnp.int32, (CHUNK, CHUNK), 0)
    c = lax.broadcasted_iota(jnp.int32, (CHUNK, CHUNK), 1)
    return [jnp.where(r >= c, ws_ref[h], 0.0).astype(BF16) for h in range(HEADS)]


def _sgu_stats(zpre, gain, bias):
    e = zpre.shape[1] // 2
    z, dz = _gelu_and_grad(zpre)
    u, v = z[:, :e], z[:, e:]
    vc = v - _row_mean(v)
    rstd = lax.rsqrt(_row_mean(vc * vc) + NORM_EPS)
    vhat = vc * rstd
    return u, vhat, rstd, vhat * gain + bias, dz


def _spatial_fwd(wt, vn_bf, bfull_ref, sv_ref, tm):
    for ci in range(tm // CHUNK):
        rows = slice(ci * CHUNK, (ci + 1) * CHUNK)
        for h in range(HEADS):
            cols = slice(h * GROUP, (h + 1) * GROUP)
            sv_ref[rows, cols] = _nn(wt[h], vn_bf[rows, cols]) + bfull_ref[:, cols]


def _mixer_a_fwd(x, g, gath, gain, bias, ws, bfull, tm, hosted=()):
    t_tok, d = x.shape
    e = gain.shape[1]
    e2 = 2 * e
    n_in, n_out = e2 // N_DEV, e // N_DEV

    def body(x_ref, g_ref, gain_ref, bias_ref, ws_ref, bfull_ref, gath_ref,
             xo_ref, gd_ref, u_ref, vhat_ref, svo_ref, y_ref, rstd_ref, win_v, wout_v, sv_v, sems):
        _load_group([(gath_ref, 0, n_in, win_v), (gath_ref, n_in, n_out, wout_v)], sems)
        xv = x_ref[...]
        h = _rms_fwd(xv, g_ref[...])[0].astype(BF16)
        zpre = _nt(h, win_v[...])
        u, vhat, rstd, vn, gelu_d = _sgu_stats(zpre, gain_ref[...], bias_ref[...])
        gd_ref[...] = gelu_d.astype(BF16)
        u_ref[...] = u.astype(BF16)
        vhat_ref[...] = vhat.astype(BF16)
        rstd_ref[...] = rstd
        _spatial_fwd(_tril_weights(ws_ref), vn.astype(BF16), bfull_ref, sv_v, tm)
        sv = sv_v[...]
        svo_ref[...] = sv.astype(BF16)
        y = (u * sv).astype(BF16)
        y_ref[...] = y
        xo_ref[...] = xv + _nn(y, wout_v[...])

    return _hosting_call(
        body, "mixer_a_fwd", t_tok // tm, [x, g, gain, bias, ws, bfull, gath],
        in_specs=[_row_spec(tm, d), _const_spec((1, d)), _const_spec((1, e)), _const_spec((1, e)),
                  _const_spec((HEADS, CHUNK, CHUNK)), _const_spec((CHUNK, e)), ANY],
        out_specs=[_row_spec(tm, d), _row_spec(tm, e2), _row_spec(tm, e), _row_spec(tm, e), _row_spec(tm, e),
                   _row_spec(tm, e), _row_spec(tm, 1)],
        out_shape=[jax.ShapeDtypeStruct((t_tok, d), F32), jax.ShapeDtypeStruct((t_tok, e2), BF16),
                   jax.ShapeDtypeStruct((t_tok, e), BF16), jax.ShapeDtypeStruct((t_tok, e), BF16),
                   jax.ShapeDtypeStruct((t_tok, e), BF16), jax.ShapeDtypeStruct((t_tok, e), BF16),
                   jax.ShapeDtypeStruct((t_tok, 1), F32)],
        scratch=[pltpu.VMEM((e2, d), BF16), pltpu.VMEM((e, d), BF16), pltpu.VMEM((tm, e), F32),
                 pltpu.SemaphoreType.DMA((2 * N_DEV,))],
        hosted=hosted)


def _mixer_a_bwd(dout, x, gd, u_sav, vhat_sav, sv_sav, rstd_sav, g, gath, gain, bias, ws, tm, hosted=()):
    t_tok, d = x.shape
    e = gain.shape[1]
    e2 = 2 * e
    n_in, n_out = e2 // N_DEV, e // N_DEV
    n_steps = t_tok // tm

    def body(dout_ref, x_ref, gd_ref, u_ref, vhat_ref, sv_ref, rstd_ref, g_ref, gain_ref, bias_ref, ws_ref, gath_ref,
             dx_ref, dxb_ref, h_ref, dz_ref, dg_ref, dgain_ref, dbias_ref, dws_ref, dbso_ref,
             win_v, wout_v, dvn_v, dbs_ref, sems):
        i = pl.program_id(0)
        _load_group([(gath_ref, 0, n_in, win_v), (gath_ref, n_in, n_out, wout_v)], sems)

        @pl.when(i == 0)
        def _():
            dg_ref[...] = jnp.zeros_like(dg_ref)
            dgain_ref[...] = jnp.zeros_like(dgain_ref)
            dbias_ref[...] = jnp.zeros_like(dbias_ref)
            dws_ref[...] = jnp.zeros_like(dws_ref)
            dbs_ref[...] = jnp.zeros_like(dbs_ref)

        xv = x_ref[...]
        gv = g_ref[...]
        hv, xhat, r = _rms_fwd(xv, gv)
        h_ref[...] = hv.astype(BF16)
        gain_v = gain_ref[...]
        vhat = vhat_ref[...].astype(F32)
        vn_bf = (vhat * gain_v + bias_ref[...]).astype(BF16)
        wt = _tril_weights(ws_ref)

        dov = dout_ref[...]
        dy = _nt(dov.astype(BF16), wout_v[...])
        du = dy * sv_ref[...].astype(F32)
        dsv = dy * u_ref[...].astype(F32)
        dsv_bf = dsv.astype(BF16)
        for ci in range(tm // CHUNK):
            rows = slice(ci * CHUNK, (ci + 1) * CHUNK)
            dbs_ref[...] += dsv[rows, :]
            for h in range(HEADS):
                cols = slice(h * GROUP, (h + 1) * GROUP)
                dvn_v[rows, cols] = _tn(wt[h], dsv_bf[rows, cols])
                dws_ref[h] += _nt(dsv_bf[rows, cols], vn_bf[rows, cols])
        dvn = dvn_v[...]
        dgain_ref[...] += _col_sum(dvn * vhat)
        dbias_ref[...] += _col_sum(dvn)
        dvhat = dvn * gain_v
        dv = rstd_ref[...] * (dvhat - _row_mean(dvhat) - vhat * _row_mean(dvhat * vhat))
        dzpre = (jnp.concatenate([du, dv], axis=1) * gd_ref[...].astype(F32)).astype(BF16)
        dz_ref[...] = dzpre
        dh = _nn(dzpre, win_v[...])
        dxr, dg_row = _rms_bwd(dh, xhat, r, gv)
        dg_ref[...] += dg_row
        dx = dov + dxr
        dx_ref[...] = dx
        dxb_ref[...] = dx.astype(BF16)

        @pl.when(i == n_steps - 1)
        def _():
            rr = lax.broadcasted_iota(jnp.int32, (CHUNK, CHUNK), 0)
            cc = lax.broadcasted_iota(jnp.int32, (CHUNK, CHUNK), 1)
            for h in range(HEADS):
                dws_ref[h] = jnp.where(rr >= cc, dws_ref[h], 0.0)
                dbso_ref[h] = jnp.sum(dbs_ref[:, h * GROUP:(h + 1) * GROUP], axis=1, keepdims=True)

    return _hosting_call(
        body, "mixer_a_bwd", n_steps, [dout, x, gd, u_sav, vhat_sav, sv_sav, rstd_sav, g, gain, bias, ws, gath],
        in_specs=[_row_spec(tm, d), _row_spec(tm, d), _row_spec(tm, e2), _row_spec(tm, e), _row_spec(tm, e),
                  _row_spec(tm, e), _row_spec(tm, 1), _const_spec((1, d)),
                  _const_spec((1, e)), _const_spec((1, e)), _const_spec((HEADS, CHUNK, CHUNK)), ANY],
        out_specs=[_row_spec(tm, d), _row_spec(tm, d), _row_spec(tm, d), _row_spec(tm, e2),
                   _const_spec((1, d)), _const_spec((1, e)), _const_spec((1, e)),
                   _const_spec((HEADS, CHUNK, CHUNK)), _const_spec((HEADS, CHUNK, 1))],
        out_shape=[jax.ShapeDtypeStruct((t_tok, d), F32), jax.ShapeDtypeStruct((t_tok, d), BF16),
                   jax.ShapeDtypeStruct((t_tok, d), BF16), jax.ShapeDtypeStruct((t_tok, e2), BF16),
                   jax.ShapeDtypeStruct((1, d), F32), jax.ShapeDtypeStruct((1, e), F32),
                   jax.ShapeDtypeStruct((1, e), F32), jax.ShapeDtypeStruct((HEADS, CHUNK, CHUNK), F32),
                   jax.ShapeDtypeStruct((HEADS, CHUNK, 1), F32)],
        scratch=[pltpu.VMEM((e2, d), BF16), pltpu.VMEM((e, d), BF16), pltpu.VMEM((tm, e), F32),
                 pltpu.VMEM((CHUNK, e), F32), pltpu.SemaphoreType.DMA((2 * N_DEV,))],
        hosted=hosted)


def _ffn_fwd(x, g, srcs, nf, tm, name, hosted=(), head=None):
    t_tok, d = x.shape
    f = nf * N_DEV
    firsts = [first for _, first in srcs]
    n_head = 2 if head else 0

    def body(*refs):
        x_ref, g_ref, sg_ref, su_ref, sd_ref = refs[:5]
        gate_ref, up_ref, wg_v, wu_v, wd_v, sems = refs[-6:]
        _load_group(
            [(sg_ref, firsts[0], nf, wg_v), (su_ref, firsts[1], nf, wu_v), (sd_ref, firsts[2], nf, wd_v)], sems)
        if head:
            t_ref, gf_ref, loss_ref, dx_ref, dxb_ref, dgf_ref = refs[5:11]

            @pl.when(pl.program_id(0) == 0)
            def _():
                loss_ref[...] = jnp.zeros_like(loss_ref)
                dgf_ref[...] = jnp.zeros_like(dgf_ref)

        xv = x_ref[...]
        h = _rms_fwd(xv, g_ref[...])[0].astype(BF16)
        gate = _nt(h, wg_v[...])
        up = _nt(h, wu_v[...])
        gate_ref[...] = gate.astype(BF16)
        up_ref[...] = up.astype(BF16)
        act = (gate * _sigmoid(gate) * up).astype(BF16)
        xo = xv + _nn(act, wd_v[...])
        if head:
            gfv = gf_ref[...]
            y, xhat, r = _rms_fwd(xo, gfv)
            err = y - t_ref[...]
            loss_ref[...] += 0.5 * jnp.sum(_row_mean(err * err), axis=0, keepdims=True)
            dxr, dg_row = _rms_bwd(err * (1.0 / d), xhat, r, gfv)
            dgf_ref[...] += dg_row
            dx_ref[...] = dxr
            dxb_ref[...] = dxr.astype(BF16)
        else:
            refs[5][...] = xo

    act_specs = [_row_spec(tm, f), _row_spec(tm, f)]
    act_shapes = [jax.ShapeDtypeStruct((t_tok, f), BF16), jax.ShapeDtypeStruct((t_tok, f), BF16)]
    if head:
        out_specs = [_const_spec((1, 1)), _row_spec(tm, d), _row_spec(tm, d), _const_spec((1, d))]
        out_shape = [jax.ShapeDtypeStruct((1, 1), F32), jax.ShapeDtypeStruct((t_tok, d), F32),
                     jax.ShapeDtypeStruct((t_tok, d), BF16), jax.ShapeDtypeStruct((1, d), F32)]
    else:
        out_specs = [_row_spec(tm, d)]
        out_shape = [jax.ShapeDtypeStruct((t_tok, d), F32)]
    return _hosting_call(
        body, name, t_tok // tm, [x, g] + [arr for arr, _ in srcs] + list(head or ()),
        in_specs=[_row_spec(tm, d), _const_spec((1, d)), ANY, ANY, ANY] + [_row_spec(tm, d), _const_spec((1, d))][:n_head],
        out_specs=out_specs + act_specs, out_shape=out_shape + act_shapes,
        scratch=[pltpu.VMEM((f, d), BF16), pltpu.VMEM((f, d), BF16), pltpu.VMEM((f, d), BF16),
                 pltpu.SemaphoreType.DMA((3 * N_DEV,))],
        hosted=hosted)


def _ffn_bwd(dout, x, gate, up, g, srcs, nf, tm, name, hosted=()):
    t_tok, d = x.shape
    f = nf * N_DEV
    firsts = [first for _, first in srcs]
    per_chunk = -(-f // (FFN_CHUNKS * MXU_WIDTH)) * MXU_WIDTH
    bounds = [min(ck * per_chunk, f) for ck in range(FFN_CHUNKS + 1)]

    def body(dout_ref, x_ref, gate_ref, up_ref, g_ref, sg_ref, su_ref, sd_ref,
             dx_ref, dxb_ref, h_ref, act_ref, dgu_ref, dg_ref, wg_v, wu_v, wd_v, sems):
        _load_group(
            [(sg_ref, firsts[0], nf, wg_v), (su_ref, firsts[1], nf, wu_v), (sd_ref, firsts[2], nf, wd_v)], sems)

        @pl.when(pl.program_id(0) == 0)
        def _():
            dg_ref[...] = jnp.zeros_like(dg_ref)

        xv = x_ref[...]
        gv = g_ref[...]
        hv, xhat, r = _rms_fwd(xv, gv)
        h_ref[...] = hv.astype(BF16)
        dov = dout_ref[...]
        dob = dov.astype(BF16)
        dh = None
        for ck in range(FFN_CHUNKS):
            cols = slice(bounds[ck], bounds[ck + 1])
            gate_v = gate_ref[:, cols].astype(F32)
            up_v = up_ref[:, cols].astype(F32)
            sig = _sigmoid(gate_v)
            silu = gate_v * sig
            act_ref[:, cols] = (silu * up_v).astype(BF16)
            dact = _nt(dob, wd_v[cols, :])
            dup = (dact * silu).astype(BF16)
            dgate = (dact * up_v * (sig * (1.0 + gate_v * (1.0 - sig)))).astype(BF16)
            dgu_ref[:, cols] = dgate
            dgu_ref[:, f + bounds[ck]:f + bounds[ck + 1]] = dup
            part = _nn(dgate, wg_v[cols, :]) + _nn(dup, wu_v[cols, :])
            dh = part if dh is None else dh + part
        dxr, dg_row = _rms_bwd(dh, xhat, r, gv)
        dg_ref[...] += dg_row
        dx = dov + dxr
        dx_ref[...] = dx
        dxb_ref[...] = dx.astype(BF16)

    return _hosting_call(
        body, name, t_tok // tm, [dout, x, gate, up, g] + [arr for arr, _ in srcs],
        in_specs=[_row_spec(tm, d), _row_spec(tm, d), _row_spec(tm, f), _row_spec(tm, f), _const_spec((1, d)),
                  ANY, ANY, ANY],
        out_specs=[_row_spec(tm, d), _row_spec(tm, d), _row_spec(tm, d), _row_spec(tm, f), _row_spec(tm, 2 * f),
                   _const_spec((1, d))],
        out_shape=[jax.ShapeDtypeStruct((t_tok, d), F32), jax.ShapeDtypeStruct((t_tok, d), BF16),
                   jax.ShapeDtypeStruct((t_tok, d), BF16), jax.ShapeDtypeStruct((t_tok, f), BF16),
                   jax.ShapeDtypeStruct((t_tok, 2 * f), BF16), jax.ShapeDtypeStruct((1, d), F32)],
        scratch=[pltpu.VMEM((f, d), BF16), pltpu.VMEM((f, d), BF16), pltpu.VMEM((f, d), BF16),
                 pltpu.SemaphoreType.DMA((3 * N_DEV,))],
        hosted=hosted)


def _shift_down(z, k, prev_rows):
    row = lax.broadcasted_iota(jnp.int32, z.shape, 0)
    out = pltpu.roll(z, k, 0)
    for j in range(k):
        out = jnp.where(row == j, prev_rows[j], out)
    return out


def _shift_up(z, k, next_rows):
    tm = z.shape[0]
    row = lax.broadcasted_iota(jnp.int32, z.shape, 0)
    out = pltpu.roll(z, tm - k, 0)
    for j in range(k):
        out = jnp.where(row == tm - k + j, next_rows[j], out)
    return out


def _mixer_b_fwd(x, g, gath, conv_w, tm, seq, hosted=()):
    t_tok, d = x.shape
    e = conv_w.shape[1]
    e3 = 3 * e
    n_in, n_out = e3 // N_DEV, e // N_DEV
    tiles_per_seq = seq // tm

    def body(x_ref, g_ref, cw_ref, gath_ref, xo_ref, p_ref, win_v, wout_v, tail_v, sems):
        i = pl.program_id(0)
        _load_group([(gath_ref, 0, n_in, win_v), (gath_ref, n_in, n_out, wout_v)], sems)

        @pl.when(i % tiles_per_seq == 0)
        def _():
            tail_v[...] = jnp.zeros_like(tail_v)

        xv = x_ref[...]
        h = _rms_fwd(xv, g_ref[...])[0].astype(BF16)
        p = _nt(h, win_v[...])
        p_ref[...] = p.astype(BF16)
        z = p[:, e:2 * e] * p[:, 2 * e:]
        prev = [tail_v[SUBLANES - 2:SUBLANES - 1, :], tail_v[SUBLANES - 1:SUBLANES, :]]
        conv = (cw_ref[2:3, :] * z + cw_ref[1:2, :] * _shift_down(z, 1, prev[1:])
                + cw_ref[0:1, :] * _shift_down(z, 2, prev))
        tail_v[...] = z[tm - SUBLANES:, :]
        y = (p[:, :e] * conv).astype(BF16)
        xo_ref[...] = xv + _nn(y, wout_v[...])

    return _hosting_call(
        body, "mixer_b_fwd", t_tok // tm, [x, g, conv_w, gath],
        in_specs=[_row_spec(tm, d), _const_spec((1, d)), _const_spec((SUBLANES, e)), ANY],
        out_specs=[_row_spec(tm, d), _row_spec(tm, e3)],
        out_shape=[jax.ShapeDtypeStruct((t_tok, d), F32), jax.ShapeDtypeStruct((t_tok, e3), BF16)],
        scratch=[pltpu.VMEM((e3, d), BF16), pltpu.VMEM((e, d), BF16), pltpu.VMEM((SUBLANES, e), F32),
                 pltpu.SemaphoreType.DMA((2 * N_DEV,))],
        hosted=hosted)


def _mixer_b_bwd(dout, x, p, g, gath, conv_w, tm, seq, hosted=()):
    t_tok, d = x.shape
    e = conv_w.shape[1]
    e3 = 3 * e
    n_in, n_out = e3 // N_DEV, e // N_DEV
    tiles_per_seq = seq // tm
    halo_per_tile = tm // HALO
    n_halo = t_tok // HALO

    def body(dout_ref, dnext_ref, x_ref, p_ref, pprev_ref, pnext_ref, g_ref, cw_ref, gath_ref,
             dx_ref, dxb_ref, h_ref, y_ref, dp_ref, dg_ref, dcw_ref, win_v, wout_v, sems):
        i = pl.program_id(0)
        _load_group([(gath_ref, 0, n_in, win_v), (gath_ref, n_in, n_out, wout_v)], sems)

        @pl.when(i == 0)
        def _():
            dg_ref[...] = jnp.zeros_like(dg_ref)
            dcw_ref[...] = jnp.zeros_like(dcw_ref)

        first = (i % tiles_per_seq == 0).astype(F32)
        last = (i % tiles_per_seq == tiles_per_seq - 1).astype(F32)
        xv = x_ref[...]
        gv = g_ref[...]
        hv, xhat, r = _rms_fwd(xv, gv)
        h_ref[...] = hv.astype(BF16)
        pv = p_ref[...].astype(F32)
        bg, cg, hx = pv[:, :e], pv[:, e:2 * e], pv[:, 2 * e:]
        z = cg * hx
        pprev = pprev_ref[...].astype(F32)
        zprev = pprev[:, e:2 * e] * pprev[:, 2 * e:] * (1.0 - first)
        prev = [zprev[HALO - 2:HALO - 1, :], zprev[HALO - 1:HALO, :]]
        zs1 = _shift_down(z, 1, prev[1:])
        zs2 = _shift_down(z, 2, prev)
        w0, w1, w2 = cw_ref[0:1, :], cw_ref[1:2, :], cw_ref[2:3, :]
        conv = w2 * z + w1 * zs1 + w0 * zs2
        y_ref[...] = (bg * conv).astype(BF16)

        dov = dout_ref[...]
        wout_bf = wout_v[...]
        dy = _nt(dov.astype(BF16), wout_bf)
        dconv = dy * bg
        dnext = _nt(dnext_ref[...].astype(BF16), wout_bf) * pnext_ref[:, :e].astype(F32) * (1.0 - last)
        nxt = [dnext[0:1, :], dnext[1:2, :]]
        dz = w2 * dconv + w1 * _shift_up(dconv, 1, nxt[:1]) + w0 * _shift_up(dconv, 2, nxt)
        dcw_ref[0:1, :] += _col_sum(dconv * zs2)
        dcw_ref[1:2, :] += _col_sum(dconv * zs1)
        dcw_ref[2:3, :] += _col_sum(dconv * z)
        dp = jnp.concatenate([dy * conv, dz * hx, dz * cg], axis=1).astype(BF16)
        dp_ref[...] = dp
        dh = _nn(dp, win_v[...])
        dxr, dg_row = _rms_bwd(dh, xhat, r, gv)
        dg_ref[...] += dg_row
        dx = dov + dxr
        dx_ref[...] = dx
        dxb_ref[...] = dx.astype(BF16)

    prev_spec = lambda w: pl.BlockSpec((HALO, w), lambda i: (jnp.maximum(i * halo_per_tile - 1, 0), 0))
    next_spec = lambda w: pl.BlockSpec((HALO, w), lambda i: (jnp.minimum((i + 1) * halo_per_tile, n_halo - 1), 0))
    return _hosting_call(
        body, "mixer_b_bwd", t_tok // tm, [dout, dout, x, p, p, p, g, conv_w, gath],
        in_specs=[_row_spec(tm, d), next_spec(d), _row_spec(tm, d), _row_spec(tm, e3), prev_spec(e3), next_spec(e3),
                  _const_spec((1, d)), _const_spec((SUBLANES, e)), ANY],
        out_specs=[_row_spec(tm, d), _row_spec(tm, d), _row_spec(tm, d), _row_spec(tm, e), _row_spec(tm, e3),
                   _const_spec((1, d)), _const_spec((SUBLANES, e))],
        out_shape=[jax.ShapeDtypeStruct((t_tok, d), F32), jax.ShapeDtypeStruct((t_tok, d), BF16),
                   jax.ShapeDtypeStruct((t_tok, d), BF16), jax.ShapeDtypeStruct((t_tok, e), BF16),
                   jax.ShapeDtypeStruct((t_tok, e3), BF16), jax.ShapeDtypeStruct((1, d), F32),
                   jax.ShapeDtypeStruct((SUBLANES, e), F32)],
        scratch=[pltpu.VMEM((e3, d), BF16), pltpu.VMEM((e, d), BF16), pltpu.SemaphoreType.DMA((2 * N_DEV,))],
        hosted=hosted)


def _wgrad(a, b, bm, name, hosted=(), part=(1, 0)):
    t_tok, m = a.shape
    n = b.shape[1]
    every, first = part

    def body(a_ref, b_ref, o_ref):
        o_ref[...] = _tn(a_ref[...], b_ref[...]).astype(o_ref.dtype)

    outs, h_outs = _hosting_call(
        body, name, m // (bm * every), [a, b],
        in_specs=[pl.BlockSpec((t_tok, bm), lambda i: (0, every * i + first)), _const_spec((t_tok, n))],
        out_specs=[pl.BlockSpec((bm, n), lambda i: (i, 0))],
        out_shape=[jax.ShapeDtypeStruct((m // every, n), BF16)],
        scratch=[], hosted=hosted)
    return (outs[0], h_outs) if hosted else outs[0]


def _sum_slots(land, rb, name):
    n_slots, rows, cols = land.shape

    def body(l_ref, o_ref):
        acc = l_ref[0].astype(F32)
        for k in range(1, n_slots):
            acc = acc + l_ref[k].astype(F32)
        o_ref[...] = acc

    return pl.pallas_call(
        body, name=name, grid=(rows // rb,),
        in_specs=[pl.BlockSpec((n_slots, rb, cols), lambda i: (0, i, 0))],
        out_specs=pl.BlockSpec((rb, cols), lambda i: (i, 0)),
        out_shape=jax.ShapeDtypeStruct((rows, cols), F32),
        compiler_params=_params(sequential=False),
    )(land)


def _adamw(w, grad, m, v, rb, name):
    rows, cols = w.shape
    c1 = 1.0 / (1.0 - ADAM_B1 ** ADAM_STEP)
    c2 = 1.0 / (1.0 - ADAM_B2 ** ADAM_STEP)

    def body(w_ref, g_ref, m_ref, v_ref, d_ref, mo_ref, vo_ref):
        gv = g_ref[...]
        mn = ADAM_B1 * m_ref[...] + (1.0 - ADAM_B1) * gv
        vn = ADAM_B2 * v_ref[...] + (1.0 - ADAM_B2) * (gv * gv)
        mo_ref[...] = mn
        vo_ref[...] = vn
        d_ref[...] = -ADAM_LR * ((mn * c1) / (jnp.sqrt(vn * c2) + ADAM_EPS) + ADAM_WD * w_ref[...])

    spec = pl.BlockSpec((rb, cols), lambda i: (i, 0))
    shape = jax.ShapeDtypeStruct((rows, cols), F32)
    return pl.pallas_call(
        body, name=name, grid=(rows // rb,),
        in_specs=[spec] * 4, out_specs=[spec] * 3, out_shape=[shape] * 3,
        compiler_params=_params(sequential=False),
    )(w, grad, m, v)


def _pack_shards(groups, name, hosted=()):
    flat = [(part, layer) for group in groups for part, layer, _ in group]
    rows = [[p.shape[2] if turn else p.shape[1] for p, _, turn in group] for group in groups]
    first, _, first_turn = groups[0][0]
    width = first.shape[1] if first_turn else first.shape[2]

    def body(*refs):
        ins, outs = refs[:len(flat)], refs[len(flat):]
        k = 0
        for gi, group in enumerate(groups):
            off = 0
            for (_, _, turn), n in zip(group, rows[gi]):
                part = ins[k][...].astype(BF16)
                if turn:
                    r = lax.broadcasted_iota(jnp.int32, (n, n), 0)
                    c = lax.broadcasted_iota(jnp.int32, (n, n), 1)
                    part = _nt((r == c).astype(BF16), part).astype(BF16)
                outs[gi][off:off + n, :] = part
                off += n
                k += 1

    return _hosting_call(
        body, name, 1, [p for p, _ in flat],
        in_specs=[pl.BlockSpec((None,) + p.shape[1:], lambda i, layer=layer: (layer, 0, 0)) for p, layer in flat],
        out_specs=[_const_spec((sum(r), width)) for r in rows],
        out_shape=[jax.ShapeDtypeStruct((sum(r), width), BF16) for r in rows],
        scratch=[], hosted=hosted)


def _split_bf16(a):
    hi = a.astype(BF16)
    rest = a - hi.astype(F32)
    mid = rest.astype(BF16)
    return hi, mid, (rest - mid.astype(F32)).astype(BF16)


def _reduce_adamw(lands, w, m, v, transpose, name, hosted=()):
    n_layers, rows_w, cols_w = w.shape
    c1 = 1.0 / (1.0 - ADAM_B1 ** ADAM_STEP)
    c2 = 1.0 / (1.0 - ADAM_B2 ** ADAM_STEP)
    flat = [piece for pieces in lands for piece in pieces]
    counts = [len(pieces) for pieces in lands]
    if transpose:
        tiles = rows_w // MXU_WIDTH
        blk = (MXU_WIDTH, cols_w)
        land_specs = [pl.BlockSpec((N_CHIP, n, MXU_WIDTH), lambda i, b=first // n: (0, b, i % tiles))
                      for _, first, n in flat]
        for _, first, n in flat:
            assert first % n == 0
    else:
        tiles = 2
        blk = (rows_w // tiles, cols_w)
        assert all(c == 1 for c in counts)
        land_specs = [pl.BlockSpec((N_CHIP,) + blk, lambda i, b=first // blk[0]: (0, b + i % tiles, 0))
                      for _, first, _ in flat]
        for _, first, _ in flat:
            assert first % blk[0] == 0

    def body(*refs):
        land_refs = refs[:len(flat)]
        w_ref, m_ref, v_ref, g_ref, d_ref, mo_ref, vo_ref = refs[len(flat):]
        layer = pl.program_id(0) // tiles

        def total(ref):
            acc = ref[0].astype(F32)
            for q in range(1, N_CHIP):
                acc = acc + ref[q].astype(F32)
            return acc

        def layer_sum(k):
            first = sum(counts[:k])
            parts = [total(land_refs[first + j]) for j in range(counts[k])]
            return parts[0] if len(parts) == 1 else jnp.concatenate(parts, axis=0)

        gv = layer_sum(0)
        for k in range(1, n_layers):
            gv = jnp.where(layer == k, layer_sum(k), gv)
        if transpose:
            r = lax.broadcasted_iota(jnp.int32, (MXU_WIDTH, MXU_WIDTH), 0)
            c = lax.broadcasted_iota(jnp.int32, (MXU_WIDTH, MXU_WIDTH), 1)
            eye = (r == c).astype(BF16)
            hi, mid, lo = _split_bf16(gv)
            gv = _nt(eye, hi) + _nt(eye, mid) + _nt(eye, lo)
        g_ref[...] = gv
        mn = ADAM_B1 * m_ref[...] + (1.0 - ADAM_B1) * gv
        vn = ADAM_B2 * v_ref[...] + (1.0 - ADAM_B2) * (gv * gv)
        mo_ref[...] = mn
        vo_ref[...] = vn
        d_ref[...] = -ADAM_LR * ((mn * c1) / (jnp.sqrt(vn * c2) + ADAM_EPS) + ADAM_WD * w_ref[...])

    spec = pl.BlockSpec((None,) + blk, lambda i: (i // tiles, i % tiles, 0))
    shape = jax.ShapeDtypeStruct(w.shape, F32)
    outs, h_outs = _hosting_call(
        body, name, n_layers * tiles, [land for land, _, _ in flat] + [w, m, v],
        in_specs=land_specs + [spec] * 3, out_specs=[spec] * 4, out_shape=[shape] * 4, scratch=[], hosted=hosted)
    return (outs, h_outs) if hosted else outs


def _pack_small(parts, rows):
    flat = jnp.concatenate([p.reshape(-1).astype(F32) for p in parts])
    return jnp.pad(flat, (0, rows * LANES - flat.shape[0])).reshape(rows, LANES)


def _unpack_small(packed, shapes):
    flat = packed.reshape(-1)
    out = []
    pos = 0
    for s in shapes:
        n = math.prod(s)
        out.append(flat[pos:pos + n].reshape(s))
        pos += n
    return out


def kernel(x, mix_norm, ffn_norm, a_w_in, a_v_gain, a_v_bias, a_w_s, a_b_s, a_w_out, b_w_in, b_conv_w, b_w_out, ffn_w_gate, ffn_w_up, ffn_w_down, final_norm, loss_target, m_mix_norm, m_ffn_norm, m_a_w_in, m_a_v_gain, m_a_v_bias, m_a_w_s, m_a_b_s, m_a_w_out, m_b_w_in, m_b_conv_w, m_b_w_out, m_ffn_w_gate, m_ffn_w_up, m_ffn_w_down, m_final_norm, v_mix_norm, v_ffn_norm, v_a_w_in, v_a_v_gain, v_a_v_bias, v_a_w_s, v_a_b_s, v_a_w_out, v_b_w_in, v_b_conv_w, v_b_w_out, v_ffn_w_gate, v_ffn_w_up, v_ffn_w_down, v_final_norm):
    bsz, seq, d = x.shape
    t_tok = bsz * seq
    me = _my_index()
    xt = x.reshape(t_tok, d)
    target = loss_target.reshape(t_tok, d)
    e_a = a_v_gain.shape[1]
    e_b = b_w_out.shape[1] * N_DEV
    n_layers = ffn_w_gate.shape[0]
    f_shard = ffn_w_gate.shape[2]
    f_full = f_shard * N_DEV

    conv_pad = jnp.pad(b_conv_w[0], ((0, SUBLANES - CONV_W), (0, 0)))
    sh_a = jnp.concatenate([a_w_in[0].T, a_w_out[0]]).astype(BF16)
    bfull = jnp.repeat(a_b_s[0].T, GROUP, axis=1)

    gate_t, up_t = ffn_w_gate.transpose(0, 2, 1), ffn_w_up.transpose(0, 2, 1)
    (sh_b, sh_f0, sh_f1g, sh_f1ud), (gath_a, conv_g) = _pack_shards(
        [[(b_w_in, 0, True), (b_w_out, 0, False)],
         [(gate_t, 0, False), (up_t, 0, False), (ffn_w_down, 0, False)],
         [(gate_t, 1, False)],
         [(up_t, 1, False), (ffn_w_down, 1, False)]],
        "pack_shards", hosted=[_HostedGathers([sh_a, conv_pad])])
    conv_full = jnp.pad(conv_g[:, :CONV_W, :].transpose(1, 0, 2).reshape(CONV_W, e_b), ((0, SUBLANES - CONV_W), (0, 0)))
    (x1, gd_a, u_a, vhat_a, sv_a, y_a, rstd_a), (gath_f0,) = _mixer_a_fwd(
        xt, mix_norm[0:1], gath_a, a_v_gain, a_v_bias, a_w_s[0], bfull, tm=TOKEN_TILE,
        hosted=[_HostedGathers([sh_f0])])
    srcs0 = [(gath_f0, 0), (gath_f0, f_shard), (gath_f0, 2 * f_shard)]
    (x2, gate0, up0), (gath_b, gath_f1g) = _ffn_fwd(x1, ffn_norm[0:1], srcs0, f_shard, tm=TOKEN_TILE, name="ffn_fwd0",
                                                    hosted=[_HostedGathers([sh_b, sh_f1g])])
    (x3, p_b), (gath_f1ud,) = _mixer_b_fwd(x2, mix_norm[1:2], gath_b, conv_full, tm=TOKEN_TILE, seq=seq,
                                           hosted=[_HostedGathers([sh_f1ud])])
    srcs1 = [(gath_f1g, 0), (gath_f1ud, 0), (gath_f1ud, f_shard)]
    (loss_part, dx4, dx4_bf, d_final, gate1, up1), _ = _ffn_fwd(
        x3, ffn_norm[1:2], srcs1, f_shard, tm=TOKEN_TILE_WIDE, name="ffn_fwd1", head=(target, final_norm.reshape(1, d)))

    ffn_entries = [(0, 0, f_shard), (0, f_full, f_shard), (1, 0, f_shard)]
    (dx3, dx3_bf, h_f1, act1, dgu1, d_fn1), _ = _ffn_bwd(dx4, x3, gate1, up1, ffn_norm[1:2], srcs1, f_shard, tm=TOKEN_TILE,
                                                         name="ffn_bwd1")
    g_down1 = _wgrad(act1, dx4_bf, WGRAD_ROWS, "wgrad_down1")
    g_gu1 = _wgrad(dgu1, h_f1, WGRAD_ROWS_WIDE, "wgrad_gate_up1")
    ps_f1 = _pair_reduce([g_gu1, g_down1], ffn_entries, "pair_reduce_f1")
    (dx2, dx2_bf, h_b, y_b, dp_b, d_mn1, d_conv), (land_f1gu,) = _mixer_b_bwd(
        dx3, x2, p_b, mix_norm[1:2], gath_b, conv_full, tm=TOKEN_TILE_WIDE, seq=seq,
        hosted=[_HostedChipScatter(ps_f1, 0, 2 * f_shard)])
    g_b_out = _wgrad(y_b, dx3_bf, WGRAD_ROWS, "wgrad_b_out")
    g_b_in = _wgrad(dp_b, h_b, WGRAD_ROWS_WIDE, "wgrad_b_in")
    ps_b = _pair_reduce([g_b_in, g_b_out], [(0, 0, b_w_in.shape[2]), (1, 0, b_w_out.shape[1])], "pair_reduce_b")
    (dx1, dx1_bf, h_f0, act0, dgu0, d_fn0), (land_f1d, land_b) = _ffn_bwd(
        dx2, x1, gate0, up0, ffn_norm[0:1], srcs0, f_shard, tm=TOKEN_TILE, name="ffn_bwd0",
        hosted=[_HostedChipScatter(ps_f1, 2 * f_shard, f_shard), _HostedChipScatter(ps_b)])
    g_down0 = _wgrad(act0, dx2_bf, WGRAD_ROWS, "wgrad_down0")
    g_gu0 = _wgrad(dgu0, h_f0, WGRAD_ROWS_WIDE, "wgrad_gate_up0")
    g_a_out = _wgrad(y_a, dx1_bf, WGRAD_ROWS, "wgrad_a_out")
    n_ao = a_w_out.shape[1]
    ps_f0ao = _pair_reduce([g_gu0, g_down0, g_a_out], ffn_entries + [(2, 0, n_ao)], "pair_reduce_f0_a_out")
    (dx0, _, h_a, dz_a, d_mn0, d_gain, d_bias, d_ws, d_bs_acc), (land_f0, land_ao) = _mixer_a_bwd(
        dx1, xt, gd_a, u_a, vhat_a, sv_a, rstd_a, mix_norm[0:1], gath_a, a_v_gain, a_v_bias, a_w_s[0], tm=TOKEN_TILE,
        hosted=[_HostedChipScatter(ps_f0ao, 0, 3 * f_shard), _HostedChipScatter(ps_f0ao, 3 * f_shard, n_ao)])
    d_bs = d_bs_acc.reshape(HEADS, CHUNK)

    small_grads = [jnp.concatenate([d_mn0, d_mn1]), jnp.concatenate([d_fn0, d_fn1]), d_gain, d_bias, d_ws, d_bs,
                   d_final, d_conv[:CONV_W], loss_part]
    small_shapes = [(n_layers, d), (n_layers, d), (1, e_a), (1, e_a), (1, HEADS, CHUNK, CHUNK), (1, HEADS, CHUNK), (d,),
                    (CONV_W, e_b), ()]
    n_small = sum(math.prod(s) for s in small_shapes)
    blk_rows = -(-n_small // (N_DEV * LANES * SUBLANES)) * SUBLANES
    small_rows = blk_rows * N_DEV
    packed = _pack_small(small_grads, small_rows)
    n_half = a_w_in.shape[2] // 2
    g_ai0, (small_land,) = _wgrad(dz_a, h_a, n_half, "wgrad_a_in0", hosted=[_HostedScatterAll(packed)], part=(2, 0))
    ps_ai0 = _pair_reduce([g_ai0], [(0, 0, n_half)], "pair_reduce_a_in0")
    small_sum = _sum_slots(small_land, blk_rows, "sum_small")
    g_ai1, (land_ai0,) = _wgrad(dz_a, h_a, n_half, "wgrad_a_in1", hosted=[_HostedChipScatter(ps_ai0)], part=(2, 1))
    ps_ai1 = _pair_reduce([g_ai1], [(0, 0, n_half)], "pair_reduce_a_in1")
    small_gath, land_ai1 = _exchange([_HostedGathers([small_sum]), _HostedChipScatter(ps_ai1)], "tail_exchange")
    small_all = small_gath.reshape(small_rows, LANES)

    n_b_in = b_w_in.shape[2]
    gate_out = _reduce_adamw([[(land_f0, 0, f_shard)], [(land_f1gu, 0, f_shard)]], gate_t,
                             m_ffn_w_gate.transpose(0, 2, 1), v_ffn_w_gate.transpose(0, 2, 1), False, "adamw_gate")
    up_out = _reduce_adamw([[(land_f0, f_shard, f_shard)], [(land_f1gu, f_shard, f_shard)]], up_t,
                           m_ffn_w_up.transpose(0, 2, 1), v_ffn_w_up.transpose(0, 2, 1), False, "adamw_up")
    res = {
        "a_w_in": _reduce_adamw([[(land_ai0, 0, n_half), (land_ai1, 0, n_half)]], a_w_in, m_a_w_in, v_a_w_in, True,
                                "adamw_a_in"),
        "a_w_out": _reduce_adamw([[(land_ao, 0, a_w_out.shape[1])]], a_w_out, m_a_w_out, v_a_w_out, False,
                                 "adamw_a_out"),
        "b_w_in": _reduce_adamw([[(land_b, 0, n_b_in)]], b_w_in, m_b_w_in, v_b_w_in, True, "adamw_b_in"),
        "b_w_out": _reduce_adamw([[(land_b, n_b_in, b_w_out.shape[1])]], b_w_out, m_b_w_out, v_b_w_out, False,
                                 "adamw_b_out"),
        "ffn_w_gate": [o.transpose(0, 2, 1) for o in gate_out],
        "ffn_w_up": [o.transpose(0, 2, 1) for o in up_out],
        "ffn_w_down": _reduce_adamw([[(land_f0, 2 * f_shard, f_shard)], [(land_f1d, 0, f_shard)]], ffn_w_down,
                                    m_ffn_w_down, v_ffn_w_down, False, "adamw_down"),
    }

    (gr_mix, gr_ffn, gr_gain, gr_bias, gr_ws, gr_bs, gr_final, gr_conv_full, loss) = _unpack_small(small_all, small_shapes)
    gr_conv = lax.dynamic_slice_in_dim(gr_conv_full, me * (e_b // N_DEV), e_b // N_DEV, axis=1)[None]

    small_w =[mix_norm, ffn_norm, a_v_gain, a_v_bias, a_w_s, a_b_s, final_norm]
    small_m = [m_mix_norm, m_ffn_norm, m_a_v_gain, m_a_v_bias, m_a_w_s, m_a_b_s, m_final_norm]
    small_v = [v_mix_norm, v_ffn_norm, v_a_v_gain, v_a_v_bias, v_a_w_s, v_a_b_s, v_final_norm]
    small_g = [gr_mix, gr_ffn, gr_gain, gr_bias, gr_ws, gr_bs, gr_final]
    sm_shapes = small_shapes[:len(small_w)]
    sm_out = _adamw(_pack_small(small_w, small_rows), _pack_small(small_g, small_rows), _pack_small(small_m, small_rows),
                    _pack_small(small_v, small_rows), small_rows, "adamw_small")
    sm_delta, sm_m, sm_v = [_unpack_small(o, sm_shapes) for o in sm_out]

    conv_out = _adamw(b_conv_w[0], gr_conv[0], m_b_conv_w[0], v_b_conv_w[0], CONV_W, "adamw_conv")
    conv_delta, conv_m, conv_v = [o[None] for o in conv_out]

    order = ["mix_norm", "ffn_norm", "a_w_in", "a_v_gain", "a_v_bias", "a_w_s", "a_b_s", "a_w_out", "b_w_in",
             "b_conv_w", "b_w_out", "ffn_w_gate", "ffn_w_up", "ffn_w_down", "final_norm"]
    small_names = ["mix_norm", "ffn_norm", "a_v_gain", "a_v_bias", "a_w_s", "a_b_s", "final_norm"]
    grads = {"b_conv_w": gr_conv}
    deltas, new_m, new_v = {}, {}, {}
    for k, name in enumerate(small_names):
        grads[name] = small_g[k]
        deltas[name], new_m[name], new_v[name] = sm_delta[k], sm_m[k], sm_v[k]
    deltas["b_conv_w"], new_m["b_conv_w"], new_v["b_conv_w"] = conv_delta, conv_m, conv_v
    for name, (gg, dl, mm, vv) in res.items():
        grads[name], deltas[name], new_m[name], new_v[name] = gg, dl, mm, vv

    grad_x = dx0.reshape(bsz, seq, d)
    return (loss, grad_x, *[grads[n] for n in order], *[deltas[n] for n in order],
            *[new_m[n] for n in order], *[new_v[n] for n in order])
```

```python
import math

import jax
import jax.numpy as jnp
from jax import lax
from jax.experimental import pallas as pl
from jax.experimental.pallas import tpu as pltpu

F32 = jnp.float32
BF16 = jnp.bfloat16

N_DEV = 8
N_CHIP = 4
CHUNK = 128
HEADS = 16
GROUP = 128
CONV_W = 3
NORM_EPS = 1e-6
GELU_C = math.sqrt(2.0 / math.pi)
GELU_K = 0.044715

ADAM_LR = 0.001
ADAM_B1 = 0.9
ADAM_B2 = 0.999
ADAM_EPS = 1e-08
ADAM_WD = 0.01
ADAM_STEP = 10

LANES = 128
SUBLANES = 8
VMEM_LIMIT = 60 * 1024 * 1024
HALO = 16
MXU_WIDTH = 256
FFN_CHUNKS = 2
TOKEN_TILE = 256
TOKEN_TILE_WIDE = 512
WGRAD_ROWS = 256
WGRAD_ROWS_WIDE = 512
GATHER_RELAY_AT = 0.56
GATHER_FORWARD_LEAD = 2

MESH = pl.DeviceIdType.MESH
ANY = pl.BlockSpec(memory_space=pl.ANY)

PEER_FLIPS = {"sibling": (0, 0, 1), "x": (1, 0, 0), "y": (0, 1, 0), "diagonal": (1, 1, 0),
              "x_other": (1, 0, 1), "y_other": (0, 1, 1), "diagonal_other": (1, 1, 1)}
COLLECTIVE_IDS = {frozenset(["sibling"]): 0,
                  frozenset(["sibling", "x", "y"]): 1,
                  frozenset(["sibling", "x", "y", "diagonal"]): 2,
                  frozenset(["x", "y", "diagonal"]): 3,
                  frozenset(PEER_FLIPS): 4}


def _params(sequential=True):
    return pltpu.CompilerParams(
        dimension_semantics=("arbitrary",) if sequential else None,
        vmem_limit_bytes=VMEM_LIMIT)


def _nn(a, b):
    return jnp.dot(a, b, preferred_element_type=F32)


def _nt(a, b):
    return lax.dot_general(a, b, (((1,), (1,)), ((), ())), preferred_element_type=F32)


def _tn(a, b):
    return lax.dot_general(a, b, (((0,), (0,)), ((), ())), preferred_element_type=F32)


def _row_mean(a):
    return jnp.mean(a, axis=-1, keepdims=True)


def _col_sum(a):
    return jnp.sum(a, axis=0, keepdims=True)


def _rms_fwd(x, g):
    r = lax.rsqrt(_row_mean(x * x) + NORM_EPS)
    xhat = x * r
    return xhat * g, xhat, r


def _rms_bwd(dh, xhat, r, g):
    a = dh * g
    dx = r * (a - xhat * _row_mean(a * xhat))
    return dx, _col_sum(dh * xhat)


def _gelu_and_grad(x):
    x2 = x * x
    t = jnp.tanh(x * (GELU_C + (GELU_C * GELU_K) * x2))
    half = 0.5 * t + 0.5
    d = half + x * (0.5 - 0.5 * (t * t)) * (GELU_C + (3.0 * GELU_C * GELU_K) * x2)
    return x * half, d


def _sigmoid(x):
    return 1.0 / (1.0 + jnp.exp(-x))


def _row_spec(tm, width):
    return pl.BlockSpec((tm, width), lambda i: (i, 0))


def _const_spec(shape):
    nd = len(shape)
    return pl.BlockSpec(shape, lambda i: (0,) * nd)


def _load_group(parts, sems):
    @pl.when(pl.program_id(0) == 0)
    def _():
        copies = []
        for k, (gath_ref, first, n, dst) in enumerate(parts):
            for j in range(N_DEV):
                copies.append(pltpu.make_async_copy(gath_ref.at[j, pl.ds(first, n), :], dst.at[pl.ds(j * n, n), :],
                                                    sems.at[k * N_DEV + j]))
        for cp in copies:
            cp.start()
        for cp in copies:
            cp.wait()


def _hosting_call(body, name, n_steps, arrays, in_specs, out_specs, out_shape, scratch, hosted=()):
    n_in, n_out, n_scr = len(arrays), len(out_shape), len(scratch)
    h_arrays = [a for h in hosted for a in h.arrays]
    h_shapes = [s for h in hosted for s in h.out_shapes]
    h_sems = [s for h in hosted for s in h.sem_shapes]
    peers = sorted(set().union(*[h.peers for h in hosted])) if hosted else []

    def handshake():
        @pl.when(pl.program_id(0) == 0)
        def _():
            x, y, c = lax.axis_index("x"), lax.axis_index("y"), lax.axis_index("c")
            barrier = pltpu.get_barrier_semaphore()
            for p in peers:
                fx, fy, fc = PEER_FLIPS[p]
                peer = (1 - x if fx else x, 1 - y if fy else y, 1 - c if fc else c)
                pl.semaphore_signal(barrier, inc=1, device_id=peer, device_id_type=MESH)
            pl.semaphore_wait(barrier, len(peers))

    def full_body(*refs):
        pos = 0
        groups = []
        for n in (n_in, len(h_arrays), n_out, len(h_shapes), n_scr, len(h_sems)):
            groups.append(refs[pos:pos + n])
            pos += n
        own_in, h_in, own_out, h_out, own_scr, h_sem = groups
        per_host = []
        pi = po = ps = 0
        for h in hosted:
            ni, no, ns = len(h.arrays), len(h.out_shapes), len(h.sem_shapes)
            per_host.append((h, h_in[pi:pi + ni], h_out[po:po + no], h_sem[ps:ps + ns]))
            pi, po, ps = pi + ni, po + no, ps + ns
        if hosted:
            handshake()
        for h, ins, outs, sems in per_host:
            h.begin(ins, outs, sems, n_steps)
        body(*own_in, *own_out, *own_scr)
        for h, ins, outs, sems in per_host:
            h.end(ins, outs, sems, n_steps)

    outs = pl.pallas_call(
        full_body, name=name, grid=(n_steps,),
        in_specs=list(in_specs) + [ANY] * len(h_arrays),
        out_specs=list(out_specs) + [ANY] * len(h_shapes),
        out_shape=list(out_shape) + h_shapes,
        scratch_shapes=list(scratch) + h_sems,
        compiler_params=pltpu.CompilerParams(
            dimension_semantics=("arbitrary",), vmem_limit_bytes=VMEM_LIMIT,
            collective_id=COLLECTIVE_IDS[frozenset(peers)] if hosted else None),
    )(*arrays, *h_arrays)
    return outs[:n_out], outs[n_out:]


def _my_index():
    return 4 * lax.axis_index("x") + 2 * lax.axis_index("y") + lax.axis_index("c")


GATHER_COPIES = 8


def _gather_relays(n_rows, dtype):
    rows_per_tile = SUBLANES * 4 // jnp.dtype(dtype).itemsize
    return n_rows % 2 == 0 and (n_rows // 2) % rows_per_tile == 0


class _Gather:
    def __init__(self, shard, out, send_sems, recv_sems, local_sem):
        self.shard, self.out = shard, out
        self.send_sems, self.recv_sems, self.local_sem = send_sems, recv_sems, local_sem
        x, y, c = lax.axis_index("x"), lax.axis_index("y"), lax.axis_index("c")
        self.c = c
        self.me, self.sibling = (x, y, c), (x, y, 1 - c)
        self.xn, self.yn, self.dg = (1 - x, y), (x, 1 - y), (1 - x, 1 - y)
        self.n = shard.shape[0]
        self.half = self.n // 2
        self.relays = _gather_relays(self.n, shard.dtype)

    def _slot(self, dev, lo=0, hi=None):
        hi = self.n if hi is None else hi
        return self.out.at[4 * dev[0] + 2 * dev[1] + dev[2], pl.ds(lo, hi - lo), :]

    def _copy(self, k, block, to, src=None, lo=0, hi=None):
        return pltpu.make_async_remote_copy(
            src_ref=self._slot(block, lo, hi) if src is None else src, dst_ref=self._slot(block, lo, hi),
            send_sem=self.send_sems.at[k], recv_sem=self.recv_sems.at[k], device_id=to, device_id_type=MESH)

    def _local(self):
        return pltpu.make_async_copy(self.shard, self._slot(self.me), self.local_sem)

    def start(self):
        c = self.c
        self._local().start()
        self._copy(0, self.me, self.sibling, src=self.shard).start()
        self._copy(1, self.me, (*self.xn, c), src=self.shard).start()
        self._copy(2, self.me, (*self.yn, c), src=self.shard).start()
        if not self.relays:
            self._copy(3, self.me, (*self.dg, c), src=self.shard).start()

    def relay(self):
        c = self.c
        if self.relays:
            self._copy(1, (*self.xn, c), self.me).wait_recv()
            self._copy(3, (*self.xn, c), (*self.yn, c), hi=self.half).start()
            self._copy(2, (*self.yn, c), self.me).wait_recv()
            self._copy(4, (*self.yn, c), (*self.xn, c), lo=self.half).start()

    def forward(self):
        c = self.c
        if self.relays:
            self._copy(5, (*self.xn, c), self.sibling).start()
            self._copy(6, (*self.yn, c), self.sibling).start()
            self._copy(3, (*self.dg, c), self.me, hi=self.half).wait_recv()
            self._copy(4, (*self.dg, c), self.me, lo=self.half).wait_recv()
        else:
            self._copy(1, (*self.xn, c), self.me).wait_recv()
            self._copy(5, (*self.xn, c), self.sibling).start()
            self._copy(2, (*self.yn, c), self.me).wait_recv()
            self._copy(6, (*self.yn, c), self.sibling).start()
            self._copy(3, (*self.dg, c), self.me).wait_recv()
        self._copy(7, (*self.dg, c), self.sibling).start()

    def finish(self):
        c = self.c
        self._copy(0, self.sibling, self.me).wait_recv()
        for k, chip in ((5, self.xn), (6, self.yn), (7, self.dg)):
            self._copy(k, (*chip, 1 - c), self.me).wait_recv()
        for k in (0, 1, 2, 5, 6, 7):
            self._copy(k, self.me, self.sibling).wait_send()
        if self.relays:
            self._copy(3, self.me, self.sibling, hi=self.half).wait_send()
            self._copy(4, self.me, self.sibling, lo=self.half).wait_send()
        else:
            self._copy(3, self.me, self.sibling).wait_send()
        self._local().wait()


class _HostedGathers:
    def __init__(self, shards, mid_lead=GATHER_FORWARD_LEAD, relay_at=GATHER_RELAY_AT):
        n = len(shards)
        self.arrays = shards
        self.mid_lead, self.relay_at = mid_lead, relay_at
        self.peers = {"sibling", "x", "y"}
        if not all(_gather_relays(s.shape[0], s.dtype) for s in shards):
            self.peers.add("diagonal")
        self.out_shapes = [jax.ShapeDtypeStruct((N_DEV,) + s.shape, s.dtype) for s in shards]
        self.sem_shapes = [pltpu.SemaphoreType.DMA((n, GATHER_COPIES)), pltpu.SemaphoreType.DMA((n, GATHER_COPIES)),
                           pltpu.SemaphoreType.DMA((n,))]

    def _gathers(self, ins, outs, sems):
        return [_Gather(ins[a], outs[a], sems[0].at[a], sems[1].at[a], sems[2].at[a]) for a in range(len(ins))]

    def begin(self, ins, outs, sems, n_steps):
        i = pl.program_id(0)
        forward_step = max(n_steps - 1 - self.mid_lead, 0)
        relay_step = min(int(self.relay_at * n_steps), forward_step)

        @pl.when(i == 0)
        def _():
            for g in self._gathers(ins, outs, sems):
                g.start()

        if n_steps == 1:
            return

        @pl.when(i == relay_step)
        def _():
            for g in self._gathers(ins, outs, sems):
                g.relay()

        @pl.when(i == forward_step)
        def _():
            for g in self._gathers(ins, outs, sems):
                g.forward()

    def end(self, ins, outs, sems, n_steps):
        @pl.when(pl.program_id(0) == n_steps - 1)
        def _():
            gathers = self._gathers(ins, outs, sems)
            if n_steps == 1:
                for g in gathers:
                    g.relay()
                for g in gathers:
                    g.forward()
            for g in gathers:
                g.finish()


def _exchange(hosted, name):
    return _hosting_call(lambda: None, name, 1, [], [], [], [], [], hosted=hosted)[1]


class _ChipScatter:
    def __init__(self, pairsum, row0, land, send_sems, recv_sems, local_sem):
        self.pairsum, self.row0, self.land = pairsum, row0, land
        self.send_sems, self.recv_sems, self.local_sem = send_sems, recv_sems, local_sem
        x, y, c = lax.axis_index("x"), lax.axis_index("y"), lax.axis_index("c")
        self.c = c
        self.chip = 2 * x + y
        self.others = [(1 - x, y), (x, 1 - y), (1 - x, 1 - y)]

    def _src(self, chip):
        return self.pairsum.at[chip, pl.ds(self.row0, self.land.shape[1]), :]

    def _copy(self, k):
        ox, oy = self.others[k]
        return pltpu.make_async_remote_copy(
            src_ref=self._src(2 * ox + oy), dst_ref=self.land.at[self.chip],
            send_sem=self.send_sems.at[k], recv_sem=self.recv_sems.at[k], device_id=(ox, oy, self.c),
            device_id_type=MESH)

    def _arrival(self, k):
        ox, oy = self.others[k]
        return pltpu.make_async_remote_copy(
            src_ref=self._src(self.chip), dst_ref=self.land.at[2 * ox + oy],
            send_sem=self.send_sems.at[k], recv_sem=self.recv_sems.at[k], device_id=(ox, oy, self.c),
            device_id_type=MESH)

    def _local(self):
        return pltpu.make_async_copy(self._src(self.chip), self.land.at[self.chip], self.local_sem)

    def start(self):
        self._local().start()
        for k in range(N_CHIP - 1):
            self._copy(k).start()

    def finish(self):
        for k in range(N_CHIP - 1):
            self._arrival(k).wait_recv()
        for k in range(N_CHIP - 1):
            self._copy(k).wait_send()
        self._local().wait()


class _HostedChipScatter:
    def __init__(self, pairsum, row0=0, n=None):
        n = pairsum.shape[1] - row0 if n is None else n
        self.row0 = row0
        self.peers = {"x", "y", "diagonal"}
        self.arrays = [pairsum]
        self.out_shapes = [jax.ShapeDtypeStruct((N_CHIP, n, pairsum.shape[2]), pairsum.dtype)]
        self.sem_shapes = [pltpu.SemaphoreType.DMA((N_CHIP - 1,)), pltpu.SemaphoreType.DMA((N_CHIP - 1,)),
                           pltpu.SemaphoreType.DMA(())]

    def begin(self, ins, outs, sems, n_steps):
        @pl.when(pl.program_id(0) == 0)
        def _():
            _ChipScatter(ins[0], self.row0, outs[0], *sems).start()

    def end(self, ins, outs, sems, n_steps):
        @pl.when(pl.program_id(0) == n_steps - 1)
        def _():
            _ChipScatter(ins[0], self.row0, outs[0], *sems).finish()


def _pair_reduce(arrays, entries, name):
    n_arr, n_ent = len(arrays), len(entries)
    cols = arrays[0].shape[1]
    offsets = []
    total = 0
    for _, _, n in entries:
        offsets.append(total)
        total += n

    def body(*refs):
        ins, out_ref = refs[:n_arr], refs[n_arr]
        rbuf, own, send_sems, recv_sems, own_sems = refs[n_arr + 1:]
        q = pl.program_id(0)
        x, y, c = lax.axis_index("x"), lax.axis_index("y"), lax.axis_index("c")

        def block(e, chip, core):
            ai, first, n = entries[e]
            return ins[ai].at[pl.ds(first + (2 * chip + core) * n, n), :]

        def to_sibling(e, chip):
            return pltpu.make_async_remote_copy(
                src_ref=block(e, chip, 1 - c), dst_ref=rbuf.at[chip, pl.ds(offsets[e], entries[e][2]), :],
                send_sem=send_sems.at[e, chip], recv_sem=recv_sems.at[e, chip], device_id=(x, y, 1 - c),
                device_id_type=MESH)

        def own_block(e, chip):
            return pltpu.make_async_copy(block(e, chip, c), own.at[chip, pl.ds(offsets[e], entries[e][2]), :],
                                         own_sems.at[e, chip])

        @pl.when(q == 0)
        def _():
            barrier = pltpu.get_barrier_semaphore()
            pl.semaphore_signal(barrier, inc=1, device_id=(x, y, 1 - c), device_id_type=MESH)
            pl.semaphore_wait(barrier, 1)
            for chip in range(N_CHIP):
                for e in range(n_ent):
                    to_sibling(e, chip).start()
            for chip in range(N_CHIP):
                for e in range(n_ent):
                    own_block(e, chip).start()

        for e in range(n_ent):
            own_block(e, q).wait()
            to_sibling(e, q).wait_recv()
        out_ref[...] = (own[q].astype(F32) + rbuf[q].astype(F32)).astype(out_ref.dtype)

        @pl.when(q == N_CHIP - 1)
        def _():
            for chip in range(N_CHIP):
                for e in range(n_ent):
                    to_sibling(e, chip).wait_send()

    return pl.pallas_call(
        body, name=name, grid=(N_CHIP,),
        in_specs=[ANY] * n_arr,
        out_specs=pl.BlockSpec((None, total, cols), lambda q: (q, 0, 0)),
        out_shape=jax.ShapeDtypeStruct((N_CHIP, total, cols), BF16),
        scratch_shapes=[pltpu.VMEM((N_CHIP, total, cols), BF16), pltpu.VMEM((N_CHIP, total, cols), BF16),
                        pltpu.SemaphoreType.DMA((n_ent, N_CHIP)), pltpu.SemaphoreType.DMA((n_ent, N_CHIP)),
                        pltpu.SemaphoreType.DMA((n_ent, N_CHIP))],
        compiler_params=pltpu.CompilerParams(dimension_semantics=("arbitrary",), vmem_limit_bytes=VMEM_LIMIT,
                                             collective_id=COLLECTIVE_IDS[frozenset(["sibling"])]),
    )(*arrays)


class _HostedScatterAll:
    def __init__(self, packed):
        n = packed.shape[0] // N_DEV
        self.n = n
        self.peers = set(PEER_FLIPS)
        self.arrays = [packed]
        self.out_shapes = [jax.ShapeDtypeStruct((N_DEV, n, packed.shape[1]), packed.dtype)]
        self.sem_shapes = [pltpu.SemaphoreType.DMA((N_DEV - 1,)), pltpu.SemaphoreType.DMA((N_DEV - 1,)),
                           pltpu.SemaphoreType.DMA(())]

    def _copies(self, ins, outs, sems, with_arrivals):
        src, land = ins[0], outs[0]
        send_sems, recv_sems, local_sem = sems
        me = _my_index()

        def block(p):
            return src.at[pl.ds(p * self.n, self.n), :]

        local = pltpu.make_async_copy(block(me), land.at[me], local_sem)
        sends, arrivals = [], []
        for k in range(1, N_DEV):
            p = (me + k) % N_DEV
            q = (me + N_DEV - k) % N_DEV
            sends.append(pltpu.make_async_remote_copy(
                src_ref=block(p), dst_ref=land.at[me], send_sem=send_sems.at[k - 1], recv_sem=recv_sems.at[k - 1],
                device_id=(p // 4, (p // 2) % 2, p % 2), device_id_type=MESH))
            if with_arrivals:
                arrivals.append(pltpu.make_async_remote_copy(
                    src_ref=block(me), dst_ref=land.at[q], send_sem=send_sems.at[k - 1], recv_sem=recv_sems.at[k - 1],
                    device_id=(q // 4, (q // 2) % 2, q % 2), device_id_type=MESH))
        return local, sends, arrivals

    def begin(self, ins, outs, sems, n_steps):
        @pl.when(pl.program_id(0) == 0)
        def _():
            local, sends, _ = self._copies(ins, outs, sems, with_arrivals=False)
            local.start()
            for cp in sends:
                cp.start()

    def end(self, ins, outs, sems, n_steps):
        @pl.when(pl.program_id(0) == n_steps - 1)
        def _():
            local, sends, arrivals = self._copies(ins, outs, sems, with_arrivals=True)
            for cp in arrivals:
                cp.wait_recv()
            for cp in sends:
                cp.wait_send()
            local.wait()


def _tril_weights(ws_ref):
    r = lax.broadcasted_iota(jnp.int32, (CHUNK, CHUNK), 0)
    c = lax.broadcasted_iota(jnp.int32, (CHUNK, CHUNK), 1)
    return [jnp.where(r >= c, ws_ref[h], 0.0).astype(BF16) for h in range(HEADS)]


def _sgu_stats(zpre, gain, bias):
    e = zpre.shape[1] // 2
    z, dz = _gelu_and_grad(zpre)
    u, v = z[:, :e], z[:, e:]
    vc = v - _row_mean(v)
    rstd = lax.rsqrt(_row_mean(vc * vc) + NORM_EPS)
    vhat = vc * rstd
    return u, vhat, rstd, vhat * gain + bias, dz


def _spatial_fwd(wt, vn_bf, bfull_ref, sv_ref, tm):
    for ci in range(tm // CHUNK):
        rows = slice(ci * CHUNK, (ci + 1) * CHUNK)
        for h in range(HEADS):
            cols = slice(h * GROUP, (h + 1) * GROUP)
            sv_ref[rows, cols] = _nn(wt[h], vn_bf[rows, cols]) + bfull_ref[:, cols]


def _mixer_a_fwd(x, g, gath, gain, bias, ws, bfull, tm, hosted=()):
    t_tok, d = x.shape
    e = gain.shape[1]
    e2 = 2 * e
    n_in, n_out = e2 // N_DEV, e // N_DEV

    def body(x_ref, g_ref, gain_ref, bias_ref, ws_ref, bfull_ref, gath_ref,
             xo_ref, gd_ref, u_ref, vhat_ref, svo_ref, y_ref, rstd_ref, win_v, wout_v, sv_v, sems):
        _load_group([(gath_ref, 0, n_in, win_v), (gath_ref, n_in, n_out, wout_v)], sems)
        xv = x_ref[...]
        h = _rms_fwd(xv, g_ref[...])[0].astype(BF16)
        zpre = _nt(h, win_v[...])
        u, vhat, rstd, vn, gelu_d = _sgu_stats(zpre, gain_ref[...], bias_ref[...])
        gd_ref[...] = gelu_d.astype(BF16)
        u_ref[...] = u.astype(BF16)
        vhat_ref[...] = vhat.astype(BF16)
        rstd_ref[...] = rstd
        _spatial_fwd(_tril_weights(ws_ref), vn.astype(BF16), bfull_ref, sv_v, tm)
        sv = sv_v[...]
        svo_ref[...] = sv.astype(BF16)
        y = (u * sv).astype(BF16)
        y_ref[...] = y
        xo_ref[...] = xv + _nn(y, wout_v[...])

    return _hosting_call(
        body, "mixer_a_fwd", t_tok // tm, [x, g, gain, bias, ws, bfull, gath],
        in_specs=[_row_spec(tm, d), _const_spec((1, d)), _const_spec((1, e)), _const_spec((1, e)),
                  _const_spec((HEADS, CHUNK, CHUNK)), _const_spec((CHUNK, e)), ANY],
        out_specs=[_row_spec(tm, d), _row_spec(tm, e2), _row_spec(tm, e), _row_spec(tm, e), _row_spec(tm, e),
                   _row_spec(tm, e), _row_spec(tm, 1)],
        out_shape=[jax.ShapeDtypeStruct((t_tok, d), F32), jax.ShapeDtypeStruct((t_tok, e2), BF16),
                   jax.ShapeDtypeStruct((t_tok, e), BF16), jax.ShapeDtypeStruct((t_tok, e), BF16),
                   jax.ShapeDtypeStruct((t_tok, e), BF16), jax.ShapeDtypeStruct((t_tok, e), BF16),
                   jax.ShapeDtypeStruct((t_tok, 1), F32)],
        scratch=[pltpu.VMEM((e2, d), BF16), pltpu.VMEM((e, d), BF16), pltpu.VMEM((tm, e), F32),
                 pltpu.SemaphoreType.DMA((2 * N_DEV,))],
        hosted=hosted)


def _mixer_a_bwd(dout, x, gd, u_sav, vhat_sav, sv_sav, rstd_sav, g, gath, gain, bias, ws, tm, hosted=()):
    t_tok, d = x.shape
    e = gain.shape[1]
    e2 = 2 * e
    n_in, n_out = e2 // N_DEV, e // N_DEV
    n_steps = t_tok // tm

    def body(dout_ref, x_ref, gd_ref, u_ref, vhat_ref, sv_ref, rstd_ref, g_ref, gain_ref, bias_ref, ws_ref, gath_ref,
             dx_ref, h_ref, dz_ref, dg_ref, dgain_ref, dbias_ref, dws_ref, dbso_ref,
             win_v, wout_v, dvn_v, dbs_ref, sems):
        i = pl.program_id(0)
        _load_group([(gath_ref, 0, n_in, win_v), (gath_ref, n_in, n_out, wout_v)], sems)

        @pl.when(i == 0)
        def _():
            dg_ref[...] = jnp.zeros_like(dg_ref)
            dgain_ref[...] = jnp.zeros_like(dgain_ref)
            dbias_ref[...] = jnp.zeros_like(dbias_ref)
            dws_ref[...] = jnp.zeros_like(dws_ref)
            dbs_ref[...] = jnp.zeros_like(dbs_ref)

        xv = x_ref[...]
        gv = g_ref[...]
        hv, xhat, r = _rms_fwd(xv, gv)
        h_ref[...] = hv.astype(BF16)
        gain_v = gain_ref[...]
        vhat = vhat_ref[...].astype(F32)
        vn_bf = (vhat * gain_v + bias_ref[...]).astype(BF16)
        wt = _tril_weights(ws_ref)

        dov = dout_ref[...]
        dy = _nt(dov.astype(BF16), wout_v[...])
        du = dy * sv_ref[...].astype(F32)
        dsv = dy * u_ref[...].astype(F32)
        dsv_bf = dsv.astype(BF16)
        for ci in range(tm // CHUNK):
            rows = slice(ci * CHUNK, (ci + 1) * CHUNK)
            dbs_ref[...] += dsv[rows, :]
            for h in range(HEADS):
                cols = slice(h * GROUP, (h + 1) * GROUP)
                dvn_v[rows, cols] = _tn(wt[h], dsv_bf[rows, cols])
                dws_ref[h] += _nt(dsv_bf[rows, cols], vn_bf[rows, cols])
        dvn = dvn_v[...]
        dgain_ref[...] += _col_sum(dvn * vhat)
        dbias_ref[...] += _col_sum(dvn)
        dvhat = dvn * gain_v
        dv = rstd_ref[...] * (dvhat - _row_mean(dvhat) - vhat * _row_mean(dvhat * vhat))
        dzpre = (jnp.concatenate([du, dv], axis=1) * gd_ref[...].astype(F32)).astype(BF16)
        dz_ref[...] = dzpre
        dh = _nn(dzpre, win_v[...])
        dxr, dg_row = _rms_bwd(dh, xhat, r, gv)
        dg_ref[...] += dg_row
        dx_ref[...] = dov + dxr

        @pl.when(i == n_steps - 1)
        def _():
            rr = lax.broadcasted_iota(jnp.int32, (CHUNK, CHUNK), 0)
            cc = lax.broadcasted_iota(jnp.int32, (CHUNK, CHUNK), 1)
            for h in range(HEADS):
                dws_ref[h] = jnp.where(rr >= cc, dws_ref[h], 0.0)
                dbso_ref[h] = jnp.sum(dbs_ref[:, h * GROUP:(h + 1) * GROUP], axis=1, keepdims=True)

    return _hosting_call(
        body, "mixer_a_bwd", n_steps, [dout, x, gd, u_sav, vhat_sav, sv_sav, rstd_sav, g, gain, bias, ws, gath],
        in_specs=[_row_spec(tm, d), _row_spec(tm, d), _row_spec(tm, e2), _row_spec(tm, e), _row_spec(tm, e),
                  _row_spec(tm, e), _row_spec(tm, 1), _const_spec((1, d)),
                  _const_spec((1, e)), _const_spec((1, e)), _const_spec((HEADS, CHUNK, CHUNK)), ANY],
        out_specs=[_row_spec(tm, d), _row_spec(tm, d), _row_spec(tm, e2),
                   _const_spec((1, d)), _const_spec((1, e)), _const_spec((1, e)),
                   _const_spec((HEADS, CHUNK, CHUNK)), _const_spec((HEADS, CHUNK, 1))],
        out_shape=[jax.ShapeDtypeStruct((t_tok, d), F32),
                   jax.ShapeDtypeStruct((t_tok, d), BF16), jax.ShapeDtypeStruct((t_tok, e2), BF16),
                   jax.ShapeDtypeStruct((1, d), F32), jax.ShapeDtypeStruct((1, e), F32),
                   jax.ShapeDtypeStruct((1, e), F32), jax.ShapeDtypeStruct((HEADS, CHUNK, CHUNK), F32),
                   jax.ShapeDtypeStruct((HEADS, CHUNK, 1), F32)],
        scratch=[pltpu.VMEM((e2, d), BF16), pltpu.VMEM((e, d), BF16), pltpu.VMEM((tm, e), F32),
                 pltpu.VMEM((CHUNK, e), F32), pltpu.SemaphoreType.DMA((2 * N_DEV,))],
        hosted=hosted)


def _ffn_fwd(x, g, srcs, nf, tm, name, hosted=(), head=None):
    t_tok, d = x.shape
    f = nf * N_DEV
    firsts = [first for _, first in srcs]
    n_head = 2 if head else 0

    def body(*refs):
        x_ref, g_ref, sg_ref, su_ref, sd_ref = refs[:5]
        gate_ref, up_ref, wg_v, wu_v, wd_v, sems = refs[-6:]
        _load_group(
            [(sg_ref, firsts[0], nf, wg_v), (su_ref, firsts[1], nf, wu_v), (sd_ref, firsts[2], nf, wd_v)], sems)
        if head:
            t_ref, gf_ref, loss_ref, dx_ref, dxb_ref, dgf_ref = refs[5:11]

            @pl.when(pl.program_id(0) == 0)
            def _():
                loss_ref[...] = jnp.zeros_like(loss_ref)
                dgf_ref[...] = jnp.zeros_like(dgf_ref)

        xv = x_ref[...]
        h = _rms_fwd(xv, g_ref[...])[0].astype(BF16)
        gate = _nt(h, wg_v[...])
        up = _nt(h, wu_v[...])
        gate_ref[...] = gate.astype(BF16)
        up_ref[...] = up.astype(BF16)
        act = (gate * _sigmoid(gate) * up).astype(BF16)
        xo = xv + _nn(act, wd_v[...])
        if head:
            gfv = gf_ref[...]
            y, xhat, r = _rms_fwd(xo, gfv)
            err = y - t_ref[...]
            loss_ref[...] += 0.5 * jnp.sum(_row_mean(err * err), axis=0, keepdims=True)
            dxr, dg_row = _rms_bwd(err * (1.0 / d), xhat, r, gfv)
            dgf_ref[...] += dg_row
            dx_ref[...] = dxr
            dxb_ref[...] = dxr.astype(BF16)
        else:
            refs[5][...] = xo

    act_specs = [_row_spec(tm, f), _row_spec(tm, f)]
    act_shapes = [jax.ShapeDtypeStruct((t_tok, f), BF16), jax.ShapeDtypeStruct((t_tok, f), BF16)]
    if head:
        out_specs = [_const_spec((1, 1)), _row_spec(tm, d), _row_spec(tm, d), _const_spec((1, d))]
        out_shape = [jax.ShapeDtypeStruct((1, 1), F32), jax.ShapeDtypeStruct((t_tok, d), F32),
                     jax.ShapeDtypeStruct((t_tok, d), BF16), jax.ShapeDtypeStruct((1, d), F32)]
    else:
        out_specs = [_row_spec(tm, d)]
        out_shape = [jax.ShapeDtypeStruct((t_tok, d), F32)]
    return _hosting_call(
        body, name, t_tok // tm, [x, g] + [arr for arr, _ in srcs] + list(head or ()),
        in_specs=[_row_spec(tm, d), _const_spec((1, d)), ANY, ANY, ANY] + [_row_spec(tm, d), _const_spec((1, d))][:n_head],
        out_specs=out_specs + act_specs, out_shape=out_shape + act_shapes,
        scratch=[pltpu.VMEM((f, d), BF16), pltpu.VMEM((f, d), BF16), pltpu.VMEM((f, d), BF16),
                 pltpu.SemaphoreType.DMA((3 * N_DEV,))],
        hosted=hosted)


def _ffn_bwd(dout, x, gate, up, g, srcs, nf, tm, name, hosted=()):
    t_tok, d = x.shape
    f = nf * N_DEV
    firsts = [first for _, first in srcs]
    per_chunk = -(-f // (FFN_CHUNKS * MXU_WIDTH)) * MXU_WIDTH
    bounds = [min(ck * per_chunk, f) for ck in range(FFN_CHUNKS + 1)]

    def body(dout_ref, x_ref, gate_ref, up_ref, g_ref, sg_ref, su_ref, sd_ref,
             dx_ref, dxb_ref, h_ref, act_ref, dgu_ref, dg_ref, wg_v, wu_v, wd_v, sems):
        _load_group(
            [(sg_ref, firsts[0], nf, wg_v), (su_ref, firsts[1], nf, wu_v), (sd_ref, firsts[2], nf, wd_v)], sems)

        @pl.when(pl.program_id(0) == 0)
        def _():
            dg_ref[...] = jnp.zeros_like(dg_ref)

        xv = x_ref[...]
        gv = g_ref[...]
        hv, xhat, r = _rms_fwd(xv, gv)
        h_ref[...] = hv.astype(BF16)
        dov = dout_ref[...]
        dob = dov.astype(BF16)
        dh = None
        for ck in range(FFN_CHUNKS):
            cols = slice(bounds[ck], bounds[ck + 1])
            gate_v = gate_ref[:, cols].astype(F32)
            up_v = up_ref[:, cols].astype(F32)
            sig = _sigmoid(gate_v)
            silu = gate_v * sig
            act_ref[:, cols] = (silu * up_v).astype(BF16)
            dact = _nt(dob, wd_v[cols, :])
            dup = (dact * silu).astype(BF16)
            dgate = (dact * up_v * (sig * (1.0 + gate_v * (1.0 - sig)))).astype(BF16)
            dgu_ref[:, cols] = dgate
            dgu_ref[:, f + bounds[ck]:f + bounds[ck + 1]] = dup
            part = _nn(dgate, wg_v[cols, :]) + _nn(dup, wu_v[cols, :])
            dh = part if dh is None else dh + part
        dxr, dg_row = _rms_bwd(dh, xhat, r, gv)
        dg_ref[...] += dg_row
        dx = dov + dxr
        dx_ref[...] = dx
        dxb_ref[...] = dx.astype(BF16)

    return _hosting_call(
        body, name, t_tok // tm, [dout, x, gate, up, g] + [arr for arr, _ in srcs],
        in_specs=[_row_spec(tm, d), _row_spec(tm, d), _row_spec(tm, f), _row_spec(tm, f), _const_spec((1, d)),
                  ANY, ANY, ANY],
        out_specs=[_row_spec(tm, d), _row_spec(tm, d), _row_spec(tm, d), _row_spec(tm, f), _row_spec(tm, 2 * f),
                   _const_spec((1, d))],
        out_shape=[jax.ShapeDtypeStruct((t_tok, d), F32), jax.ShapeDtypeStruct((t_tok, d), BF16),
                   jax.ShapeDtypeStruct((t_tok, d), BF16), jax.ShapeDtypeStruct((t_tok, f), BF16),
                   jax.ShapeDtypeStruct((t_tok, 2 * f), BF16), jax.ShapeDtypeStruct((1, d), F32)],
        scratch=[pltpu.VMEM((f, d), BF16), pltpu.VMEM((f, d), BF16), pltpu.VMEM((f, d), BF16),
                 pltpu.SemaphoreType.DMA((3 * N_DEV,))],
        hosted=hosted)


def _shift_down(z, k, prev_rows):
    row = lax.broadcasted_iota(jnp.int32, z.shape, 0)
    out = pltpu.roll(z, k, 0)
    for j in range(k):
        out = jnp.where(row == j, prev_rows[j], out)
    return out


def _shift_up(z, k, next_rows):
    tm = z.shape[0]
    row = lax.broadcasted_iota(jnp.int32, z.shape, 0)
    out = pltpu.roll(z, tm - k, 0)
    for j in range(k):
        out = jnp.where(row == tm - k + j, next_rows[j], out)
    return out


def _mixer_b_fwd(x, g, gath, conv_w, tm, seq, hosted=()):
    t_tok, d = x.shape
    e = conv_w.shape[1]
    e3 = 3 * e
    n_in, n_out = e3 // N_DEV, e // N_DEV
    tiles_per_seq = seq // tm

    def body(x_ref, g_ref, cw_ref, gath_ref, xo_ref, p_ref, win_v, wout_v, tail_v, sems):
        i = pl.program_id(0)
        _load_group([(gath_ref, 0, n_in, win_v), (gath_ref, n_in, n_out, wout_v)], sems)

        @pl.when(i % tiles_per_seq == 0)
        def _():
            tail_v[...] = jnp.zeros_like(tail_v)

        xv = x_ref[...]
        h = _rms_fwd(xv, g_ref[...])[0].astype(BF16)
        p = _nt(h, win_v[...])
        p_ref[...] = p.astype(BF16)
        z = p[:, e:2 * e] * p[:, 2 * e:]
        prev = [tail_v[SUBLANES - 2:SUBLANES - 1, :], tail_v[SUBLANES - 1:SUBLANES, :]]
        conv = (cw_ref[2:3, :] * z + cw_ref[1:2, :] * _shift_down(z, 1, prev[1:])
                + cw_ref[0:1, :] * _shift_down(z, 2, prev))
        tail_v[...] = z[tm - SUBLANES:, :]
        y = (p[:, :e] * conv).astype(BF16)
        xo_ref[...] = xv + _nn(y, wout_v[...])

    return _hosting_call(
        body, "mixer_b_fwd", t_tok // tm, [x, g, conv_w, gath],
        in_specs=[_row_spec(tm, d), _const_spec((1, d)), _const_spec((SUBLANES, e)), ANY],
        out_specs=[_row_spec(tm, d), _row_spec(tm, e3)],
        out_shape=[jax.ShapeDtypeStruct((t_tok, d), F32), jax.ShapeDtypeStruct((t_tok, e3), BF16)],
        scratch=[pltpu.VMEM((e3, d), BF16), pltpu.VMEM((e, d), BF16), pltpu.VMEM((SUBLANES, e), F32),
                 pltpu.SemaphoreType.DMA((2 * N_DEV,))],
        hosted=hosted)


def _mixer_b_bwd(dout, x, p, g, gath, conv_w, tm, seq, hosted=()):
    t_tok, d = x.shape
    e = conv_w.shape[1]
    e3 = 3 * e
    n_in, n_out = e3 // N_DEV, e // N_DEV
    tiles_per_seq = seq // tm
    halo_per_tile = tm // HALO
    n_halo = t_tok // HALO

    def body(dout_ref, dnext_ref, x_ref, p_ref, pprev_ref, pnext_ref, g_ref, cw_ref, gath_ref,
             dx_ref, dxb_ref, h_ref, y_ref, dp_ref, dg_ref, dcw_ref, win_v, wout_v, sems):
        i = pl.program_id(0)
        _load_group([(gath_ref, 0, n_in, win_v), (gath_ref, n_in, n_out, wout_v)], sems)

        @pl.when(i == 0)
        def _():
            dg_ref[...] = jnp.zeros_like(dg_ref)
            dcw_ref[...] = jnp.zeros_like(dcw_ref)

        first = (i % tiles_per_seq == 0).astype(F32)
        last = (i % tiles_per_seq == tiles_per_seq - 1).astype(F32)
        xv = x_ref[...]
        gv = g_ref[...]
        hv, xhat, r = _rms_fwd(xv, gv)
        h_ref[...] = hv.astype(BF16)
        pv = p_ref[...].astype(F32)
        bg, cg, hx = pv[:, :e], pv[:, e:2 * e], pv[:, 2 * e:]
        z = cg * hx
        pprev = pprev_ref[...].astype(F32)
        zprev = pprev[:, e:2 * e] * pprev[:, 2 * e:] * (1.0 - first)
        prev = [zprev[HALO - 2:HALO - 1, :], zprev[HALO - 1:HALO, :]]
        zs1 = _shift_down(z, 1, prev[1:])
        zs2 = _shift_down(z, 2, prev)
        w0, w1, w2 = cw_ref[0:1, :], cw_ref[1:2, :], cw_ref[2:3, :]
        conv = w2 * z + w1 * zs1 + w0 * zs2
        y_ref[...] = (bg * conv).astype(BF16)

        dov = dout_ref[...]
        wout_bf = wout_v[...]
        dy = _nt(dov.astype(BF16), wout_bf)
        dconv = dy * bg
        dnext = _nt(dnext_ref[...].astype(BF16), wout_bf) * pnext_ref[:, :e].astype(F32) * (1.0 - last)
        nxt = [dnext[0:1, :], dnext[1:2, :]]
        dz = w2 * dconv + w1 * _shift_up(dconv, 1, nxt[:1]) + w0 * _shift_up(dconv, 2, nxt)
        dcw_ref[0:1, :] += _col_sum(dconv * zs2)
        dcw_ref[1:2, :] += _col_sum(dconv * zs1)
        dcw_ref[2:3, :] += _col_sum(dconv * z)
        dp = jnp.concatenate([dy * conv, dz * hx, dz * cg], axis=1).astype(BF16)
        dp_ref[...] = dp
        dh = _nn(dp, win_v[...])
        dxr, dg_row = _rms_bwd(dh, xhat, r, gv)
        dg_ref[...] += dg_row
        dx = dov + dxr
        dx_ref[...] = dx
        dxb_ref[...] = dx.astype(BF16)

    prev_spec = lambda w: pl.BlockSpec((HALO, w), lambda i: (jnp.maximum(i * halo_per_tile - 1, 0), 0))
    next_spec = lambda w: pl.BlockSpec((HALO, w), lambda i: (jnp.minimum((i + 1) * halo_per_tile, n_halo - 1), 0))
    return _hosting_call(
        body, "mixer_b_bwd", t_tok // tm, [dout, dout, x, p, p, p, g, conv_w, gath],
        in_specs=[_row_spec(tm, d), next_spec(d), _row_spec(tm, d), _row_spec(tm, e3), prev_spec(e3), next_spec(e3),
                  _const_spec((1, d)), _const_spec((SUBLANES, e)), ANY],
        out_specs=[_row_spec(tm, d), _row_spec(tm, d), _row_spec(tm, d), _row_spec(tm, e), _row_spec(tm, e3),
                   _const_spec((1, d)), _const_spec((SUBLANES, e))],
        out_shape=[jax.ShapeDtypeStruct((t_tok, d), F32), jax.ShapeDtypeStruct((t_tok, d), BF16),
                   jax.ShapeDtypeStruct((t_tok, d), BF16), jax.ShapeDtypeStruct((t_tok, e), BF16),
                   jax.ShapeDtypeStruct((t_tok, e3), BF16), jax.ShapeDtypeStruct((1, d), F32),
                   jax.ShapeDtypeStruct((SUBLANES, e), F32)],
        scratch=[pltpu.VMEM((e3, d), BF16), pltpu.VMEM((e, d), BF16), pltpu.SemaphoreType.DMA((2 * N_DEV,))],
        hosted=hosted)


def _wgrad(a, b, bm, name, hosted=(), part=(1, 0)):
    t_tok, m = a.shape
    n = b.shape[1]
    every, first = part

    def body(a_ref, b_ref, o_ref):
        o_ref[...] = _tn(a_ref[...], b_ref[...]).astype(o_ref.dtype)

    outs, h_outs = _hosting_call(
        body, name, m // (bm * every), [a, b],
        in_specs=[pl.BlockSpec((t_tok, bm), lambda i: (0, every * i + first)), _const_spec((t_tok, n))],
        out_specs=[pl.BlockSpec((bm, n), lambda i: (i, 0))],
        out_shape=[jax.ShapeDtypeStruct((m // every, n), BF16)],
        scratch=[], hosted=hosted)
    return (outs[0], h_outs) if hosted else outs[0]


def _sum_slots(land, rb, name):
    n_slots, rows, cols = land.shape

    def body(l_ref, o_ref):
        acc = l_ref[0].astype(F32)
        for k in range(1, n_slots):
            acc = acc + l_ref[k].astype(F32)
        o_ref[...] = acc

    return pl.pallas_call(
        body, name=name, grid=(rows // rb,),
        in_specs=[pl.BlockSpec((n_slots, rb, cols), lambda i: (0, i, 0))],
        out_specs=pl.BlockSpec((rb, cols), lambda i: (i, 0)),
        out_shape=jax.ShapeDtypeStruct((rows, cols), F32),
        compiler_params=_params(sequential=False),
    )(land)


def _adamw(w, grad, m, v, rb, name):
    rows, cols = w.shape
    c1 = 1.0 / (1.0 - ADAM_B1 ** ADAM_STEP)
    c2 = 1.0 / (1.0 - ADAM_B2 ** ADAM_STEP)

    def body(w_ref, g_ref, m_ref, v_ref, d_ref, mo_ref, vo_ref):
        gv = g_ref[...]
        mn = ADAM_B1 * m_ref[...] + (1.0 - ADAM_B1) * gv
        vn = ADAM_B2 * v_ref[...] + (1.0 - ADAM_B2) * (gv * gv)
        mo_ref[...] = mn
        vo_ref[...] = vn
        d_ref[...] = -ADAM_LR * ((mn * c1) / (jnp.sqrt(vn * c2) + ADAM_EPS) + ADAM_WD * w_ref[...])

    spec = pl.BlockSpec((rb, cols), lambda i: (i, 0))
    shape = jax.ShapeDtypeStruct((rows, cols), F32)
    return pl.pallas_call(
        body, name=name, grid=(rows // rb,),
        in_specs=[spec] * 4, out_specs=[spec] * 3, out_shape=[shape] * 3,
        compiler_params=_params(sequential=False),
    )(w, grad, m, v)


def _pack_shards(groups, name, hosted=()):
    flat = [(part, layer) for group in groups for part, layer, _ in group]
    rows = [[p.shape[2] if turn else p.shape[1] for p, _, turn in group] for group in groups]
    first, _, first_turn = groups[0][0]
    width = first.shape[1] if first_turn else first.shape[2]

    def body(*refs):
        ins, outs = refs[:len(flat)], refs[len(flat):]
        k = 0
        for gi, group in enumerate(groups):
            off = 0
            for (_, _, turn), n in zip(group, rows[gi]):
                part = ins[k][...].astype(BF16)
                if turn:
                    r = lax.broadcasted_iota(jnp.int32, (n, n), 0)
                    c = lax.broadcasted_iota(jnp.int32, (n, n), 1)
                    part = _nt((r == c).astype(BF16), part).astype(BF16)
                outs[gi][off:off + n, :] = part
                off += n
                k += 1

    return _hosting_call(
        body, name, 1, [p for p, _ in flat],
        in_specs=[pl.BlockSpec((None,) + p.shape[1:], lambda i, layer=layer: (layer, 0, 0)) for p, layer in flat],
        out_specs=[_const_spec((sum(r), width)) for r in rows],
        out_shape=[jax.ShapeDtypeStruct((sum(r), width), BF16) for r in rows],
        scratch=[], hosted=hosted)


def _split_bf16(a):
    hi = a.astype(BF16)
    rest = a - hi.astype(F32)
    mid = rest.astype(BF16)
    return hi, mid, (rest - mid.astype(F32)).astype(BF16)


def _reduce_adamw(lands, w, m, v, transpose, name, hosted=()):
    n_layers, rows_w, cols_w = w.shape
    c1 = 1.0 / (1.0 - ADAM_B1 ** ADAM_STEP)
    c2 = 1.0 / (1.0 - ADAM_B2 ** ADAM_STEP)
    flat = [piece for pieces in lands for piece in pieces]
    counts = [len(pieces) for pieces in lands]
    if transpose:
        tiles = rows_w // MXU_WIDTH
        blk = (MXU_WIDTH, cols_w)
        land_specs = [pl.BlockSpec((N_CHIP, n, MXU_WIDTH), lambda i, b=first // n: (0, b, i % tiles))
                      for _, first, n in flat]
        for _, first, n in flat:
            assert first % n == 0
    else:
        tiles = 2
        blk = (rows_w // tiles, cols_w)
        assert all(c == 1 for c in counts)
        land_specs = [pl.BlockSpec((N_CHIP,) + blk, lambda i, b=first // blk[0]: (0, b + i % tiles, 0))
                      for _, first, _ in flat]
        for _, first, _ in flat:
            assert first % blk[0] == 0

    def body(*refs):
        land_refs = refs[:len(flat)]
        w_ref, m_ref, v_ref, g_ref, d_ref, mo_ref, vo_ref = refs[len(flat):]
        layer = pl.program_id(0) // tiles

        def total(ref):
            acc = ref[0].astype(F32)
            for q in range(1, N_CHIP):
                acc = acc + ref[q].astype(F32)
            return acc

        def layer_sum(k):
            first = sum(counts[:k])
            parts = [total(land_refs[first + j]) for j in range(counts[k])]
            return parts[0] if len(parts) == 1 else jnp.concatenate(parts, axis=0)

        gv = layer_sum(0)
        for k in range(1, n_layers):
            gv = jnp.where(layer == k, layer_sum(k), gv)
        if transpose:
            r = lax.broadcasted_iota(jnp.int32, (MXU_WIDTH, MXU_WIDTH), 0)
            c = lax.broadcasted_iota(jnp.int32, (MXU_WIDTH, MXU_WIDTH), 1)
            eye = (r == c).astype(BF16)
            hi, mid, lo = _split_bf16(gv)
            gv = _nt(eye, hi) + _nt(eye, mid) + _nt(eye, lo)
        g_ref[...] = gv
        mn = ADAM_B1 * m_ref[...] + (1.0 - ADAM_B1) * gv
        vn = ADAM_B2 * v_ref[...] + (1.0 - ADAM_B2) * (gv * gv)
        mo_ref[...] = mn
        vo_ref[...] = vn
        d_ref[...] = -ADAM_LR * ((mn * c1) / (jnp.sqrt(vn * c2) + ADAM_EPS) + ADAM_WD * w_ref[...])

    spec = pl.BlockSpec((None,) + blk, lambda i: (i // tiles, i % tiles, 0))
    shape = jax.ShapeDtypeStruct(w.shape, F32)
    outs, h_outs = _hosting_call(
        body, name, n_layers * tiles, [land for land, _, _ in flat] + [w, m, v],
        in_specs=land_specs + [spec] * 3, out_specs=[spec] * 4, out_shape=[shape] * 4, scratch=[], hosted=hosted)
    return (outs, h_outs) if hosted else outs


def _pack_small(parts, rows):
    flat = jnp.concatenate([p.reshape(-1).astype(F32) for p in parts])
    return jnp.pad(flat, (0, rows * LANES - flat.shape[0])).reshape(rows, LANES)


def _unpack_small(packed, shapes):
    flat = packed.reshape(-1)
    out = []
    pos = 0
    for s in shapes:
        n = math.prod(s)
        out.append(flat[pos:pos + n].reshape(s))
        pos += n
    return out


def kernel(x, mix_norm, ffn_norm, a_w_in, a_v_gain, a_v_bias, a_w_s, a_b_s, a_w_out, b_w_in, b_conv_w, b_w_out, ffn_w_gate, ffn_w_up, ffn_w_down, final_norm, loss_target, m_mix_norm, m_ffn_norm, m_a_w_in, m_a_v_gain, m_a_v_bias, m_a_w_s, m_a_b_s, m_a_w_out, m_b_w_in, m_b_conv_w, m_b_w_out, m_ffn_w_gate, m_ffn_w_up, m_ffn_w_down, m_final_norm, v_mix_norm, v_ffn_norm, v_a_w_in, v_a_v_gain, v_a_v_bias, v_a_w_s, v_a_b_s, v_a_w_out, v_b_w_in, v_b_conv_w, v_b_w_out, v_ffn_w_gate, v_ffn_w_up, v_ffn_w_down, v_final_norm):
    bsz, seq, d = x.shape
    t_tok = bsz * seq
    me = _my_index()
    xt = x.reshape(t_tok, d)
    target = loss_target.reshape(t_tok, d)
    e_a = a_v_gain.shape[1]
    e_b = b_w_out.shape[1] * N_DEV
    n_layers = ffn_w_gate.shape[0]
    f_shard = ffn_w_gate.shape[2]
    f_full = f_shard * N_DEV

    conv_pad = jnp.pad(b_conv_w[0], ((0, SUBLANES - CONV_W), (0, 0)))
    sh_a = jnp.concatenate([a_w_in[0].T, a_w_out[0]]).astype(BF16)
    bfull = jnp.repeat(a_b_s[0].T, GROUP, axis=1)

    gate_t, up_t = ffn_w_gate.transpose(0, 2, 1), ffn_w_up.transpose(0, 2, 1)
    (sh_b, sh_f0, sh_f1g, sh_f1ud), (gath_a, conv_g) = _pack_shards(
        [[(b_w_in, 0, True), (b_w_out, 0, False)],
         [(gate_t, 0, False), (up_t, 0, False), (ffn_w_down, 0, False)],
         [(gate_t, 1, False)],
         [(up_t, 1, False), (ffn_w_down, 1, False)]],
        "pack_shards", hosted=[_HostedGathers([sh_a, conv_pad])])
    conv_full = jnp.pad(conv_g[:, :CONV_W, :].transpose(1, 0, 2).reshape(CONV_W, e_b), ((0, SUBLANES - CONV_W), (0, 0)))
    (x1, gd_a, u_a, vhat_a, sv_a, y_a, rstd_a), (gath_f0,) = _mixer_a_fwd(
        xt, mix_norm[0:1], gath_a, a_v_gain, a_v_bias, a_w_s[0], bfull, tm=TOKEN_TILE,
        hosted=[_HostedGathers([sh_f0])])
    srcs0 = [(gath_f0, 0), (gath_f0, f_shard), (gath_f0, 2 * f_shard)]
    (x2, gate0, up0), (gath_b, gath_f1g) = _ffn_fwd(x1, ffn_norm[0:1], srcs0, f_shard, tm=TOKEN_TILE, name="ffn_fwd0",
                                                    hosted=[_HostedGathers([sh_b, sh_f1g])])
    (x3, p_b), (gath_f1ud,) = _mixer_b_fwd(x2, mix_norm[1:2], gath_b, conv_full, tm=TOKEN_TILE, seq=seq,
                                           hosted=[_HostedGathers([sh_f1ud])])
    srcs1 = [(gath_f1g, 0), (gath_f1ud, 0), (gath_f1ud, f_shard)]
    (loss_part, dx4, dx4_bf, d_final, gate1, up1), _ = _ffn_fwd(
        x3, ffn_norm[1:2], srcs1, f_shard, tm=TOKEN_TILE_WIDE, name="ffn_fwd1", head=(target, final_norm.reshape(1, d)))

    ffn_entries = [(0, 0, f_shard), (0, f_full, f_shard), (1, 0, f_shard)]
    (dx3, dx3_bf, h_f1, act1, dgu1, d_fn1), _ = _ffn_bwd(dx4, x3, gate1, up1, ffn_norm[1:2], srcs1, f_shard, tm=TOKEN_TILE,
                                                         name="ffn_bwd1")
    g_down1 = _wgrad(act1, dx4_bf, WGRAD_ROWS, "wgrad_down1")
    g_gu1 = _wgrad(dgu1, h_f1, WGRAD_ROWS_WIDE, "wgrad_gate_up1")
    ps_f1 = _pair_reduce([g_gu1, g_down1], ffn_entries, "pair_reduce_f1")
    (dx2, dx2_bf, h_b, y_b, dp_b, d_mn1, d_conv), (land_f1gu,) = _mixer_b_bwd(
        dx3, x2, p_b, mix_norm[1:2], gath_b, conv_full, tm=TOKEN_TILE_WIDE, seq=seq,
        hosted=[_HostedChipScatter(ps_f1, 0, 2 * f_shard)])
    g_b_out = _wgrad(y_b, dx3_bf, WGRAD_ROWS, "wgrad_b_out")
    g_b_in = _wgrad(dp_b, h_b, WGRAD_ROWS_WIDE, "wgrad_b_in")
    ps_b = _pair_reduce([g_b_in, g_b_out], [(0, 0, b_w_in.shape[2]), (1, 0, b_w_out.shape[1])], "pair_reduce_b")
    (dx1, dx1_bf, h_f0, act0, dgu0, d_fn0), (land_f1d, land_b) = _ffn_bwd(
        dx2, x1, gate0, up0, ffn_norm[0:1], srcs0, f_shard, tm=TOKEN_TILE, name="ffn_bwd0",
        hosted=[_HostedChipScatter(ps_f1, 2 * f_shard, f_shard), _HostedChipScatter(ps_b)])
    g_down0 = _wgrad(act0, dx2_bf, WGRAD_ROWS, "wgrad_down0")
    g_gu0 = _wgrad(dgu0, h_f0, WGRAD_ROWS_WIDE, "wgrad_gate_up0")
    g_a_out = _wgrad(y_a, dx1_bf, WGRAD_ROWS, "wgrad_a_out")
    n_ao = a_w_out.shape[1]
    ps_f0ao = _pair_reduce([g_gu0, g_down0, g_a_out], ffn_entries + [(2, 0, n_ao)], "pair_reduce_f0_a_out")
    (dx0, h_a, dz_a, d_mn0, d_gain, d_bias, d_ws, d_bs_acc), (land_f0, land_ao) = _mixer_a_bwd(
        dx1, xt, gd_a, u_a, vhat_a, sv_a, rstd_a, mix_norm[0:1], gath_a, a_v_gain, a_v_bias, a_w_s[0], tm=TOKEN_TILE,
        hosted=[_HostedChipScatter(ps_f0ao, 0, 3 * f_shard), _HostedChipScatter(ps_f0ao, 3 * f_shard, n_ao)])
    d_bs = d_bs_acc.reshape(HEADS, CHUNK)

    small_grads = [jnp.concatenate([d_mn0, d_mn1]), jnp.concatenate([d_fn0, d_fn1]), d_gain, d_bias, d_ws, d_bs,
                   d_final, d_conv[:CONV_W], loss_part]
    small_shapes = [(n_layers, d), (n_layers, d), (1, e_a), (1, e_a), (1, HEADS, CHUNK, CHUNK), (1, HEADS, CHUNK), (d,),
                    (CONV_W, e_b), ()]
    n_small = sum(math.prod(s) for s in small_shapes)
    blk_rows = -(-n_small // (N_DEV * LANES * SUBLANES)) * SUBLANES
    small_rows = blk_rows * N_DEV
    packed = _pack_small(small_grads, small_rows)
    n_half = a_w_in.shape[2] // 2
    g_ai0, (small_land,) = _wgrad(dz_a, h_a, n_half, "wgrad_a_in0", hosted=[_HostedScatterAll(packed)], part=(2, 0))
    ps_ai0 = _pair_reduce([g_ai0], [(0, 0, n_half)], "pair_reduce_a_in0")
    small_sum = _sum_slots(small_land, blk_rows, "sum_small")
    g_ai1, (land_ai0,) = _wgrad(dz_a, h_a, n_half, "wgrad_a_in1", hosted=[_HostedChipScatter(ps_ai0)], part=(2, 1))
    ps_ai1 = _pair_reduce([g_ai1], [(0, 0, n_half)], "pair_reduce_a_in1")
    land_ai1, small_gath = _exchange([_HostedChipScatter(ps_ai1), _HostedGathers([small_sum])], "tail_exchange")
    small_all = small_gath.reshape(small_rows, LANES)

    n_b_in = b_w_in.shape[2]
    gate_out = _reduce_adamw([[(land_f0, 0, f_shard)], [(land_f1gu, 0, f_shard)]], gate_t,
                             m_ffn_w_gate.transpose(0, 2, 1), v_ffn_w_gate.transpose(0, 2, 1), False, "adamw_gate")
    up_out = _reduce_adamw([[(land_f0, f_shard, f_shard)], [(land_f1gu, f_shard, f_shard)]], up_t,
                           m_ffn_w_up.transpose(0, 2, 1), v_ffn_w_up.transpose(0, 2, 1), False, "adamw_up")
    res = {
        "a_w_in": _reduce_adamw([[(land_ai0, 0, n_half), (land_ai1, 0, n_half)]], a_w_in, m_a_w_in, v_a_w_in, True,
                                "adamw_a_in"),
        "a_w_out": _reduce_adamw([[(land_ao, 0, a_w_out.shape[1])]], a_w_out, m_a_w_out, v_a_w_out, False,
                                 "adamw_a_out"),
        "b_w_in": _reduce_adamw([[(land_b, 0, n_b_in)]], b_w_in, m_b_w_in, v_b_w_in, True, "adamw_b_in"),
        "b_w_out": _reduce_adamw([[(land_b, n_b_in, b_w_out.shape[1])]], b_w_out, m_b_w_out, v_b_w_out, False,
                                 "adamw_b_out"),
        "ffn_w_gate": [o.transpose(0, 2, 1) for o in gate_out],
        "ffn_w_up": [o.transpose(0, 2, 1) for o in up_out],
        "ffn_w_down": _reduce_adamw([[(land_f0, 2 * f_shard, f_shard)], [(land_f1d, 0, f_shard)]], ffn_w_down,
                                    m_ffn_w_down, v_ffn_w_down, False, "adamw_down"),
    }

    (gr_mix, gr_ffn, gr_gain, gr_bias, gr_ws, gr_bs, gr_final, gr_conv_full, loss) = _unpack_small(small_all, small_shapes)
    gr_conv = lax.dynamic_slice_in_dim(gr_conv_full, me * (e_b // N_DEV), e_b // N_DEV, axis=1)[None]

    small_w =[mix_norm, ffn_norm, a_v_gain, a_v_bias, a_w_s, a_b_s, final_norm]
    small_m = [m_mix_norm, m_ffn_norm, m_a_v_gain, m_a_v_bias, m_a_w_s, m_a_b_s, m_final_norm]
    small_v = [v_mix_norm, v_ffn_norm, v_a_v_gain, v_a_v_bias, v_a_w_s, v_a_b_s, v_final_norm]
    small_g = [gr_mix, gr_ffn, gr_gain, gr_bias, gr_ws, gr_bs, gr_final]
    sm_shapes = small_shapes[:len(small_w)]
    sm_out = _adamw(_pack_small(small_w, small_rows), _pack_small(small_g, small_rows), _pack_small(small_m, small_rows),
                    _pack_small(small_v, small_rows), small_rows, "adamw_small")
    sm_delta, sm_m, sm_v = [_unpack_small(o, sm_shapes) for o in sm_out]

    conv_out = _adamw(b_conv_w[0], gr_conv[0], m_b_conv_w[0], v_b_conv_w[0], CONV_W, "adamw_conv")
    conv_delta, conv_m, conv_v = [o[None] for o in conv_out]

    order = ["mix_norm", "ffn_norm", "a_w_in", "a_v_gain", "a_v_bias", "a_w_s", "a_b_s", "a_w_out", "b_w_in",
             "b_conv_w", "b_w_out", "ffn_w_gate", "ffn_w_up", "ffn_w_down", "final_norm"]
    small_names = ["mix_norm", "ffn_norm", "a_v_gain", "a_v_bias", "a_w_s", "a_b_s", "final_norm"]
    grads = {"b_conv_w": gr_conv}
    deltas, new_m, new_v = {}, {}, {}
    for k, name in enumerate(small_names):
        grads[name] = small_g[k]
        deltas[name], new_m[name], new_v[name] = sm_delta[k], sm_m[k], sm_v[k]
    deltas["b_conv_w"], new_m["b_conv_w"], new_v["b_conv_w"] = conv_delta, conv_m, conv_v
    for name, (gg, dl, mm, vv) in res.items():
        grads[name], deltas[name], new_m[name], new_v[name] = gg, dl, mm, vv

    grad_x = dx0.reshape(bsz, seq, d)
    return (loss, grad_x, *[grads[n] for n in order], *[deltas[n] for n in order],
            *[new_m[n] for n in order], *[new_v[n] for n in order])
```

```python
import math

import jax
import jax.numpy as jnp
from jax import lax
from jax.experimental import pallas as pl
from jax.experimental.pallas import tpu as pltpu

F32 = jnp.float32
BF16 = jnp.bfloat16

N_DEV = 8
N_CHIP = 4
CHUNK = 128
HEADS = 16
GROUP = 128
CONV_W = 3
NORM_EPS = 1e-6
GELU_C = math.sqrt(2.0 / math.pi)
GELU_K = 0.044715

ADAM_LR = 0.001
ADAM_B1 = 0.9
ADAM_B2 = 0.999
ADAM_EPS = 1e-08
ADAM_WD = 0.01
ADAM_STEP = 10

LANES = 128
SUBLANES = 8
VMEM_LIMIT = 60 * 1024 * 1024
HALO = 16
MXU_WIDTH = 256
FFN_CHUNKS = 2
TOKEN_TILE = 256
TOKEN_TILE_WIDE = 512
WGRAD_ROWS = 256
WGRAD_ROWS_WIDE = 512
GATHER_RELAY_AT = 0.56
GATHER_FORWARD_LEAD = 2

MESH = pl.DeviceIdType.MESH
ANY = pl.BlockSpec(memory_space=pl.ANY)

PEER_FLIPS = {"sibling": (0, 0, 1), "x": (1, 0, 0), "y": (0, 1, 0), "diagonal": (1, 1, 0),
              "x_other": (1, 0, 1), "y_other": (0, 1, 1), "diagonal_other": (1, 1, 1)}
COLLECTIVE_IDS = {frozenset(["sibling"]): 0,
                  frozenset(["sibling", "x", "y"]): 1,
                  frozenset(["sibling", "x", "y", "diagonal"]): 2,
                  frozenset(["x", "y", "diagonal"]): 3,
                  frozenset(PEER_FLIPS): 4}


def _params(sequential=True):
    return pltpu.CompilerParams(
        dimension_semantics=("arbitrary",) if sequential else None,
        vmem_limit_bytes=VMEM_LIMIT)


def _nn(a, b):
    return jnp.dot(a, b, preferred_element_type=F32)


def _nt(a, b):
    return lax.dot_general(a, b, (((1,), (1,)), ((), ())), preferred_element_type=F32)


def _tn(a, b):
    return lax.dot_general(a, b, (((0,), (0,)), ((), ())), preferred_element_type=F32)


def _row_mean(a):
    return jnp.mean(a, axis=-1, keepdims=True)


def _col_sum(a):
    return jnp.sum(a, axis=0, keepdims=True)


def _rms_fwd(x, g):
    r = lax.rsqrt(_row_mean(x * x) + NORM_EPS)
    xhat = x * r
    return xhat * g, xhat, r


def _rms_bwd(dh, xhat, r, g):
    a = dh * g
    dx = r * (a - xhat * _row_mean(a * xhat))
    return dx, _col_sum(dh * xhat)


def _gelu_and_grad(x):
    x2 = x * x
    t = jnp.tanh(x * (GELU_C + (GELU_C * GELU_K) * x2))
    half = 0.5 * t + 0.5
    d = half + x * (0.5 - 0.5 * (t * t)) * (GELU_C + (3.0 * GELU_C * GELU_K) * x2)
    return x * half, d


def _sigmoid(x):
    return 1.0 / (1.0 + jnp.exp(-x))


def _row_spec(tm, width):
    return pl.BlockSpec((tm, width), lambda i: (i, 0))


def _const_spec(shape):
    nd = len(shape)
    return pl.BlockSpec(shape, lambda i: (0,) * nd)


def _load_group(parts, sems):
    @pl.when(pl.program_id(0) == 0)
    def _():
        copies = []
        for k, (gath_ref, first, n, dst) in enumerate(parts):
            for j in range(N_DEV):
                copies.append(pltpu.make_async_copy(gath_ref.at[j, pl.ds(first, n), :], dst.at[pl.ds(j * n, n), :],
                                                    sems.at[k * N_DEV + j]))
        for cp in copies:
            cp.start()
        for cp in copies:
            cp.wait()


def _hosting_call(body, name, n_steps, arrays, in_specs, out_specs, out_shape, scratch, hosted=()):
    n_in, n_out, n_scr = len(arrays), len(out_shape), len(scratch)
    h_arrays = [a for h in hosted for a in h.arrays]
    h_shapes = [s for h in hosted for s in h.out_shapes]
    h_sems = [s for h in hosted for s in h.sem_shapes]
    peers = sorted(set().union(*[h.peers for h in hosted])) if hosted else []

    def handshake():
        @pl.when(pl.program_id(0) == 0)
        def _():
            x, y, c = lax.axis_index("x"), lax.axis_index("y"), lax.axis_index("c")
            barrier = pltpu.get_barrier_semaphore()
            for p in peers:
                fx, fy, fc = PEER_FLIPS[p]
                peer = (1 - x if fx else x, 1 - y if fy else y, 1 - c if fc else c)
                pl.semaphore_signal(barrier, inc=1, device_id=peer, device_id_type=MESH)
            pl.semaphore_wait(barrier, len(peers))

    def full_body(*refs):
        pos = 0
        groups = []
        for n in (n_in, len(h_arrays), n_out, len(h_shapes), n_scr, len(h_sems)):
            groups.append(refs[pos:pos + n])
            pos += n
        own_in, h_in, own_out, h_out, own_scr, h_sem = groups
        per_host = []
        pi = po = ps = 0
        for h in hosted:
            ni, no, ns = len(h.arrays), len(h.out_shapes), len(h.sem_shapes)
            per_host.append((h, h_in[pi:pi + ni], h_out[po:po + no], h_sem[ps:ps + ns]))
            pi, po, ps = pi + ni, po + no, ps + ns
        if hosted:
            handshake()
        for h, ins, outs, sems in per_host:
            h.begin(ins, outs, sems, n_steps)
        body(*own_in, *own_out, *own_scr)
        for h, ins, outs, sems in per_host:
            h.end(ins, outs, sems, n_steps)

    outs = pl.pallas_call(
        full_body, name=name, grid=(n_steps,),
        in_specs=list(in_specs) + [ANY] * len(h_arrays),
        out_specs=list(out_specs) + [ANY] * len(h_shapes),
        out_shape=list(out_shape) + h_shapes,
        scratch_shapes=list(scratch) + h_sems,
        compiler_params=pltpu.CompilerParams(
            dimension_semantics=("arbitrary",), vmem_limit_bytes=VMEM_LIMIT,
            collective_id=COLLECTIVE_IDS[frozenset(peers)] if hosted else None),
    )(*arrays, *h_arrays)
    return outs[:n_out], outs[n_out:]


def _my_index():
    return 4 * lax.axis_index("x") + 2 * lax.axis_index("y") + lax.axis_index("c")


GATHER_COPIES = 8


def _gather_relays(n_rows, dtype):
    rows_per_tile = SUBLANES * 4 // jnp.dtype(dtype).itemsize
    return n_rows % 2 == 0 and (n_rows // 2) % rows_per_tile == 0


class _Gather:
    def __init__(self, shard, out, send_sems, recv_sems, local_sem):
        self.shard, self.out = shard, out
        self.send_sems, self.recv_sems, self.local_sem = send_sems, recv_sems, local_sem
        x, y, c = lax.axis_index("x"), lax.axis_index("y"), lax.axis_index("c")
        self.c = c
        self.me, self.sibling = (x, y, c), (x, y, 1 - c)
        self.xn, self.yn, self.dg = (1 - x, y), (x, 1 - y), (1 - x, 1 - y)
        self.n = shard.shape[0]
        self.half = self.n // 2
        self.relays = _gather_relays(self.n, shard.dtype)

    def _slot(self, dev, lo=0, hi=None):
        hi = self.n if hi is None else hi
        return self.out.at[4 * dev[0] + 2 * dev[1] + dev[2], pl.ds(lo, hi - lo), :]

    def _copy(self, k, block, to, src=None, lo=0, hi=None):
        return pltpu.make_async_remote_copy(
            src_ref=self._slot(block, lo, hi) if src is None else src, dst_ref=self._slot(block, lo, hi),
            send_sem=self.send_sems.at[k], recv_sem=self.recv_sems.at[k], device_id=to, device_id_type=MESH)

    def _local(self):
        return pltpu.make_async_copy(self.shard, self._slot(self.me), self.local_sem)

    def start(self):
        c = self.c
        self._local().start()
        self._copy(0, self.me, self.sibling, src=self.shard).start()
        self._copy(1, self.me, (*self.xn, c), src=self.shard).start()
        self._copy(2, self.me, (*self.yn, c), src=self.shard).start()
        if not self.relays:
            self._copy(3, self.me, (*self.dg, c), src=self.shard).start()

    def relay(self):
        c = self.c
        if self.relays:
            self._copy(1, (*self.xn, c), self.me).wait_recv()
            self._copy(3, (*self.xn, c), (*self.yn, c), hi=self.half).start()
            self._copy(2, (*self.yn, c), self.me).wait_recv()
            self._copy(4, (*self.yn, c), (*self.xn, c), lo=self.half).start()

    def forward(self):
        c = self.c
        if self.relays:
            self._copy(5, (*self.xn, c), self.sibling).start()
            self._copy(6, (*self.yn, c), self.sibling).start()
            self._copy(3, (*self.dg, c), self.me, hi=self.half).wait_recv()
            self._copy(4, (*self.dg, c), self.me, lo=self.half).wait_recv()
        else:
            self._copy(1, (*self.xn, c), self.me).wait_recv()
            self._copy(5, (*self.xn, c), self.sibling).start()
            self._copy(2, (*self.yn, c), self.me).wait_recv()
            self._copy(6, (*self.yn, c), self.sibling).start()
            self._copy(3, (*self.dg, c), self.me).wait_recv()
        self._copy(7, (*self.dg, c), self.sibling).start()

    def finish(self):
        c = self.c
        self._copy(0, self.sibling, self.me).wait_recv()
        for k, chip in ((5, self.xn), (6, self.yn), (7, self.dg)):
            self._copy(k, (*chip, 1 - c), self.me).wait_recv()
        for k in (0, 1, 2, 5, 6, 7):
            self._copy(k, self.me, self.sibling).wait_send()
        if self.relays:
            self._copy(3, self.me, self.sibling, hi=self.half).wait_send()
            self._copy(4, self.me, self.sibling, lo=self.half).wait_send()
        else:
            self._copy(3, self.me, self.sibling).wait_send()
        self._local().wait()


class _HostedGathers:
    def __init__(self, shards, mid_lead=GATHER_FORWARD_LEAD, relay_at=GATHER_RELAY_AT):
        n = len(shards)
        self.arrays = shards
        self.mid_lead, self.relay_at = mid_lead, relay_at
        self.peers = {"sibling", "x", "y"}
        if not all(_gather_relays(s.shape[0], s.dtype) for s in shards):
            self.peers.add("diagonal")
        self.out_shapes = [jax.ShapeDtypeStruct((N_DEV,) + s.shape, s.dtype) for s in shards]
        self.sem_shapes = [pltpu.SemaphoreType.DMA((n, GATHER_COPIES)), pltpu.SemaphoreType.DMA((n, GATHER_COPIES)),
                           pltpu.SemaphoreType.DMA((n,))]

    def _gathers(self, ins, outs, sems):
        return [_Gather(ins[a], outs[a], sems[0].at[a], sems[1].at[a], sems[2].at[a]) for a in range(len(ins))]

    def begin(self, ins, outs, sems, n_steps):
        i = pl.program_id(0)
        forward_step = max(n_steps - 1 - self.mid_lead, 0)
        relay_step = min(int(self.relay_at * n_steps), forward_step)

        @pl.when(i == 0)
        def _():
            for g in self._gathers(ins, outs, sems):
                g.start()

        if n_steps == 1:
            return

        @pl.when(i == relay_step)
        def _():
            for g in self._gathers(ins, outs, sems):
                g.relay()

        @pl.when(i == forward_step)
        def _():
            for g in self._gathers(ins, outs, sems):
                g.forward()

    def end(self, ins, outs, sems, n_steps):
        @pl.when(pl.program_id(0) == n_steps - 1)
        def _():
            gathers = self._gathers(ins, outs, sems)
            if n_steps == 1:
                for g in gathers:
                    g.relay()
                for g in gathers:
                    g.forward()
            for g in gathers:
                g.finish()


def _exchange(hosted, name):
    return _hosting_call(lambda: None, name, 1, [], [], [], [], [], hosted=hosted)[1]


class _ChipScatter:
    def __init__(self, pairsum, row0, land, send_sems, recv_sems, local_sem):
        self.pairsum, self.row0, self.land = pairsum, row0, land
        self.send_sems, self.recv_sems, self.local_sem = send_sems, recv_sems, local_sem
        x, y, c = lax.axis_index("x"), lax.axis_index("y"), lax.axis_index("c")
        self.c = c
        self.chip = 2 * x + y
        self.others = [(1 - x, y), (x, 1 - y), (1 - x, 1 - y)]

    def _src(self, chip):
        return self.pairsum.at[chip, pl.ds(self.row0, self.land.shape[1]), :]

    def _copy(self, k):
        ox, oy = self.others[k]
        return pltpu.make_async_remote_copy(
            src_ref=self._src(2 * ox + oy), dst_ref=self.land.at[self.chip],
            send_sem=self.send_sems.at[k], recv_sem=self.recv_sems.at[k], device_id=(ox, oy, self.c),
            device_id_type=MESH)

    def _arrival(self, k):
        ox, oy = self.others[k]
        return pltpu.make_async_remote_copy(
            src_ref=self._src(self.chip), dst_ref=self.land.at[2 * ox + oy],
            send_sem=self.send_sems.at[k], recv_sem=self.recv_sems.at[k], device_id=(ox, oy, self.c),
            device_id_type=MESH)

    def _local(self):
        return pltpu.make_async_copy(self._src(self.chip), self.land.at[self.chip], self.local_sem)

    def start(self):
        self._local().start()
        for k in range(N_CHIP - 1):
            self._copy(k).start()

    def finish(self):
        for k in range(N_CHIP - 1):
            self._arrival(k).wait_recv()
        for k in range(N_CHIP - 1):
            self._copy(k).wait_send()
        self._local().wait()


class _HostedChipScatter:
    def __init__(self, pairsum, row0=0, n=None):
        n = pairsum.shape[1] - row0 if n is None else n
        self.row0 = row0
        self.peers = {"x", "y", "diagonal"}
        self.arrays = [pairsum]
        self.out_shapes = [jax.ShapeDtypeStruct((N_CHIP, n, pairsum.shape[2]), pairsum.dtype)]
        self.sem_shapes = [pltpu.SemaphoreType.DMA((N_CHIP - 1,)), pltpu.SemaphoreType.DMA((N_CHIP - 1,)),
                           pltpu.SemaphoreType.DMA(())]

    def begin(self, ins, outs, sems, n_steps):
        @pl.when(pl.program_id(0) == 0)
        def _():
            _ChipScatter(ins[0], self.row0, outs[0], *sems).start()

    def end(self, ins, outs, sems, n_steps):
        @pl.when(pl.program_id(0) == n_steps - 1)
        def _():
            _ChipScatter(ins[0], self.row0, outs[0], *sems).finish()


def _pair_reduce(arrays, entries, name):
    n_arr, n_ent = len(arrays), len(entries)
    cols = arrays[0].shape[1]
    offsets = []
    total = 0
    for _, _, n in entries:
        offsets.append(total)
        total += n

    def body(*refs):
        ins, out_ref = refs[:n_arr], refs[n_arr]
        rbuf, own, send_sems, recv_sems, own_sems = refs[n_arr + 1:]
        q = pl.program_id(0)
        x, y, c = lax.axis_index("x"), lax.axis_index("y"), lax.axis_index("c")

        def block(e, chip, core):
            ai, first, n = entries[e]
            return ins[ai].at[pl.ds(first + (2 * chip + core) * n, n), :]

        def to_sibling(e, chip):
            return pltpu.make_async_remote_copy(
                src_ref=block(e, chip, 1 - c), dst_ref=rbuf.at[chip, pl.ds(offsets[e], entries[e][2]), :],
                send_sem=send_sems.at[e, chip], recv_sem=recv_sems.at[e, chip], device_id=(x, y, 1 - c),
                device_id_type=MESH)

        def own_block(e, chip):
            return pltpu.make_async_copy(block(e, chip, c), own.at[chip, pl.ds(offsets[e], entries[e][2]), :],
                                         own_sems.at[e, chip])

        @pl.when(q == 0)
        def _():
            barrier = pltpu.get_barrier_semaphore()
            pl.semaphore_signal(barrier, inc=1, device_id=(x, y, 1 - c), device_id_type=MESH)
            pl.semaphore_wait(barrier, 1)
            for chip in range(N_CHIP):
                for e in range(n_ent):
                    to_sibling(e, chip).start()
            for chip in range(N_CHIP):
                for e in range(n_ent):
                    own_block(e, chip).start()

        for e in range(n_ent):
            own_block(e, q).wait()
            to_sibling(e, q).wait_recv()
        out_ref[...] = (own[q].astype(F32) + rbuf[q].astype(F32)).astype(out_ref.dtype)

        @pl.when(q == N_CHIP - 1)
        def _():
            for chip in range(N_CHIP):
                for e in range(n_ent):
                    to_sibling(e, chip).wait_send()

    return pl.pallas_call(
        body, name=name, grid=(N_CHIP,),
        in_specs=[ANY] * n_arr,
        out_specs=pl.BlockSpec((None, total, cols), lambda q: (q, 0, 0)),
        out_shape=jax.ShapeDtypeStruct((N_CHIP, total, cols), BF16),
        scratch_shapes=[pltpu.VMEM((N_CHIP, total, cols), BF16), pltpu.VMEM((N_CHIP, total, cols), BF16),
                        pltpu.SemaphoreType.DMA((n_ent, N_CHIP)), pltpu.SemaphoreType.DMA((n_ent, N_CHIP)),
                        pltpu.SemaphoreType.DMA((n_ent, N_CHIP))],
        compiler_params=pltpu.CompilerParams(dimension_semantics=("arbitrary",), vmem_limit_bytes=VMEM_LIMIT,
                                             collective_id=COLLECTIVE_IDS[frozenset(["sibling"])]),
    )(*arrays)


class _HostedScatterAll:
    def __init__(self, packed):
        n = packed.shape[0] // N_DEV
        self.n = n
        self.peers = set(PEER_FLIPS)
        self.arrays = [packed]
        self.out_shapes = [jax.ShapeDtypeStruct((N_DEV, n, packed.shape[1]), packed.dtype)]
        self.sem_shapes = [pltpu.SemaphoreType.DMA((N_DEV - 1,)), pltpu.SemaphoreType.DMA((N_DEV - 1,)),
                           pltpu.SemaphoreType.DMA(())]

    def _copies(self, ins, outs, sems, with_arrivals):
        src, land = ins[0], outs[0]
        send_sems, recv_sems, local_sem = sems
        me = _my_index()

        def block(p):
            return src.at[pl.ds(p * self.n, self.n), :]

        local = pltpu.make_async_copy(block(me), land.at[me], local_sem)
        sends, arrivals = [], []
        for k in range(1, N_DEV):
            p = (me + k) % N_DEV
            q = (me + N_DEV - k) % N_DEV
            sends.append(pltpu.make_async_remote_copy(
                src_ref=block(p), dst_ref=land.at[me], send_sem=send_sems.at[k - 1], recv_sem=recv_sems.at[k - 1],
                device_id=(p // 4, (p // 2) % 2, p % 2), device_id_type=MESH))
            if with_arrivals:
                arrivals.append(pltpu.make_async_remote_copy(
                    src_ref=block(me), dst_ref=land.at[q], send_sem=send_sems.at[k - 1], recv_sem=recv_sems.at[k - 1],
                    device_id=(q // 4, (q // 2) % 2, q % 2), device_id_type=MESH))
        return local, sends, arrivals

    def begin(self, ins, outs, sems, n_steps):
        @pl.when(pl.program_id(0) == 0)
        def _():
            local, sends, _ = self._copies(ins, outs, sems, with_arrivals=False)
            local.start()
            for cp in sends:
                cp.start()

    def end(self, ins, outs, sems, n_steps):
        @pl.when(pl.program_id(0) == n_steps - 1)
        def _():
            local, sends, arrivals = self._copies(ins, outs, sems, with_arrivals=True)
            for cp in arrivals:
                cp.wait_recv()
            for cp in sends:
                cp.wait_send()
            local.wait()


def _tril_weights(ws_ref):
    r = lax.broadcasted_iota(jnp.int32, (CHUNK, CHUNK), 0)
    c = lax.broadcasted_iota(jnp.int32, (CHUNK, CHUNK), 1)
    return [jnp.where(r >= c, ws_ref[h], 0.0).astype(BF16) for h in range(HEADS)]


def _sgu_stats(zpre, gain, bias):
    e = zpre.shape[1] // 2
    z, dz = _gelu_and_grad(zpre)
    u, v = z[:, :e], z[:, e:]
    vc = v - _row_mean(v)
    rstd = lax.rsqrt(_row_mean(vc * vc) + NORM_EPS)
    vhat = vc * rstd
    return u, vhat, rstd, vhat * gain + bias, dz


def _spatial_fwd(wt, vn_bf, bfull_ref, sv_ref, tm):
    for ci in range(tm // CHUNK):
        rows = slice(ci * CHUNK, (ci + 1) * CHUNK)
        for h in range(HEADS):
            cols = slice(h * GROUP, (h + 1) * GROUP)
            sv_ref[rows, cols] = _nn(wt[h], vn_bf[rows, cols]) + bfull_ref[:, cols]


def _mixer_a_fwd(x, g, gath, gain, bias, ws, bfull, tm, hosted=()):
    t_tok, d = x.shape
    e = gain.shape[1]
    e2 = 2 * e
    n_in, n_out = e2 // N_DEV, e // N_DEV

    def body(x_ref, g_ref, gain_ref, bias_ref, ws_ref, bfull_ref, gath_ref,
             xo_ref, gd_ref, u_ref, vhat_ref, svo_ref, y_ref, rstd_ref, win_v, wout_v, sv_v, sems):
        _load_group([(gath_ref, 0, n_in, win_v), (gath_ref, n_in, n_out, wout_v)], sems)
        xv = x_ref[...]
        h = _rms_fwd(xv, g_ref[...])[0].astype(BF16)
        zpre = _nt(h, win_v[...])
        u, vhat, rstd, vn, gelu_d = _sgu_stats(zpre, gain_ref[...], bias_ref[...])
        gd_ref[...] = gelu_d.astype(BF16)
        u_ref[...] = u.astype(BF16)
        vhat_ref[...] = vhat.astype(BF16)
        rstd_ref[...] = rstd
        _spatial_fwd(_tril_weights(ws_ref), vn.astype(BF16), bfull_ref, sv_v, tm)
        sv = sv_v[...]
        svo_ref[...] = sv.astype(BF16)
        y = (u * sv).astype(BF16)
        y_ref[...] = y
        xo_ref[...] = xv + _nn(y, wout_v[...])

    return _hosting_call(
        body, "mixer_a_fwd", t_tok // tm, [x, g, gain, bias, ws, bfull, gath],
        in_specs=[_row_spec(tm, d), _const_spec((1, d)), _const_spec((1, e)), _const_spec((1, e)),
                  _const_spec((HEADS, CHUNK, CHUNK)), _const_spec((CHUNK, e)), ANY],
        out_specs=[_row_spec(tm, d), _row_spec(tm, e2), _row_spec(tm, e), _row_spec(tm, e), _row_spec(tm, e),
                   _row_spec(tm, e), _row_spec(tm, 1)],
        out_shape=[jax.ShapeDtypeStruct((t_tok, d), F32), jax.ShapeDtypeStruct((t_tok, e2), BF16),
                   jax.ShapeDtypeStruct((t_tok, e), BF16), jax.ShapeDtypeStruct((t_tok, e), BF16),
                   jax.ShapeDtypeStruct((t_tok, e), BF16), jax.ShapeDtypeStruct((t_tok, e), BF16),
                   jax.ShapeDtypeStruct((t_tok, 1), F32)],
        scratch=[pltpu.VMEM((e2, d), BF16), pltpu.VMEM((e, d), BF16), pltpu.VMEM((tm, e), F32),
                 pltpu.SemaphoreType.DMA((2 * N_DEV,))],
        hosted=hosted)


def _mixer_a_bwd(dout, x, gd, u_sav, vhat_sav, sv_sav, rstd_sav, g, gath, gain, bias, ws, tm, hosted=()):
    t_tok, d = x.shape
    e = gain.shape[1]
    e2 = 2 * e
    n_in, n_out = e2 // N_DEV, e // N_DEV
    n_steps = t_tok // tm

    def body(dout_ref, x_ref, gd_ref, u_ref, vhat_ref, sv_ref, rstd_ref, g_ref, gain_ref, bias_ref, ws_ref, gath_ref,
             dx_ref, dxb_ref, h_ref, dz_ref, dg_ref, dgain_ref, dbias_ref, dws_ref, dbso_ref,
             win_v, wout_v, dvn_v, dbs_ref, sems):
        i = pl.program_id(0)
        _load_group([(gath_ref, 0, n_in, win_v), (gath_ref, n_in, n_out, wout_v)], sems)

        @pl.when(i == 0)
        def _():
            dg_ref[...] = jnp.zeros_like(dg_ref)
            dgain_ref[...] = jnp.zeros_like(dgain_ref)
            dbias_ref[...] = jnp.zeros_like(dbias_ref)
            dws_ref[...] = jnp.zeros_like(dws_ref)
            dbs_ref[...] = jnp.zeros_like(dbs_ref)

        xv = x_ref[...]
        gv = g_ref[...]
        hv, xhat, r = _rms_fwd(xv, gv)
        h_ref[...] = hv.astype(BF16)
        gain_v = gain_ref[...]
        vhat = vhat_ref[...].astype(F32)
        vn_bf = (vhat * gain_v + bias_ref[...]).astype(BF16)
        wt = _tril_weights(ws_ref)

        dov = dout_ref[...]
        dy = _nt(dov.astype(BF16), wout_v[...])
        du = dy * sv_ref[...].astype(F32)
        dsv = dy * u_ref[...].astype(F32)
        dsv_bf = dsv.astype(BF16)
        for ci in range(tm // CHUNK):
            rows = slice(ci * CHUNK, (ci + 1) * CHUNK)
            dbs_ref[...] += dsv[rows, :]
            for h in range(HEADS):
                cols = slice(h * GROUP, (h + 1) * GROUP)
                dvn_v[rows, cols] = _tn(wt[h], dsv_bf[rows, cols])
                dws_ref[h] += _nt(dsv_bf[rows, cols], vn_bf[rows, cols])
        dvn = dvn_v[...]
        dgain_ref[...] += _col_sum(dvn * vhat)
        dbias_ref[...] += _col_sum(dvn)
        dvhat = dvn * gain_v
        dv = rstd_ref[...] * (dvhat - _row_mean(dvhat) - vhat * _row_mean(dvhat * vhat))
        dzpre = (jnp.concatenate([du, dv], axis=1) * gd_ref[...].astype(F32)).astype(BF16)
        dz_ref[...] = dzpre
        dh = _nn(dzpre, win_v[...])
        dxr, dg_row = _rms_bwd(dh, xhat, r, gv)
        dg_ref[...] += dg_row
        dx = dov + dxr
        dx_ref[...] = dx
        dxb_ref[...] = dx.astype(BF16)

        @pl.when(i == n_steps - 1)
        def _():
            rr = lax.broadcasted_iota(jnp.int32, (CHUNK, CHUNK), 0)
            cc = lax.broadcasted_iota(jnp.int32, (CHUNK, CHUNK), 1)
            for h in range(HEADS):
                dws_ref[h] = jnp.where(rr >= cc, dws_ref[h], 0.0)
                dbso_ref[h] = jnp.sum(dbs_ref[:, h * GROUP:(h + 1) * GROUP], axis=1, keepdims=True)

    return _hosting_call(
        body, "mixer_a_bwd", n_steps, [dout, x, gd, u_sav, vhat_sav, sv_sav, rstd_sav, g, gain, bias, ws, gath],
        in_specs=[_row_spec(tm, d), _row_spec(tm, d), _row_spec(tm, e2), _row_spec(tm, e), _row_spec(tm, e),
                  _row_spec(tm, e), _row_spec(tm, 1), _const_spec((1, d)),
                  _const_spec((1, e)), _const_spec((1, e)), _const_spec((HEADS, CHUNK, CHUNK)), ANY],
        out_specs=[_row_spec(tm, d), _row_spec(tm, d), _row_spec(tm, d), _row_spec(tm, e2),
                   _const_spec((1, d)), _const_spec((1, e)), _const_spec((1, e)),
                   _const_spec((HEADS, CHUNK, CHUNK)), _const_spec((HEADS, CHUNK, 1))],
        out_shape=[jax.ShapeDtypeStruct((t_tok, d), F32), jax.ShapeDtypeStruct((t_tok, d), BF16),
                   jax.ShapeDtypeStruct((t_tok, d), BF16), jax.ShapeDtypeStruct((t_tok, e2), BF16),
                   jax.ShapeDtypeStruct((1, d), F32), jax.ShapeDtypeStruct((1, e), F32),
                   jax.ShapeDtypeStruct((1, e), F32), jax.ShapeDtypeStruct((HEADS, CHUNK, CHUNK), F32),
                   jax.ShapeDtypeStruct((HEADS, CHUNK, 1), F32)],
        scratch=[pltpu.VMEM((e2, d), BF16), pltpu.VMEM((e, d), BF16), pltpu.VMEM((tm, e), F32),
                 pltpu.VMEM((CHUNK, e), F32), pltpu.SemaphoreType.DMA((2 * N_DEV,))],
        hosted=hosted)


def _ffn_fwd(x, g, srcs, nf, tm, name, hosted=(), head=None):
    t_tok, d = x.shape
    f = nf * N_DEV
    firsts = [first for _, first in srcs]
    n_head = 2 if head else 0

    def body(*refs):
        x_ref, g_ref, sg_ref, su_ref, sd_ref = refs[:5]
        gate_ref, up_ref, wg_v, wu_v, wd_v, sems = refs[-6:]
        _load_group(
            [(sg_ref, firsts[0], nf, wg_v), (su_ref, firsts[1], nf, wu_v), (sd_ref, firsts[2], nf, wd_v)], sems)
        if head:
            t_ref, gf_ref, loss_ref, dx_ref, dxb_ref, dgf_ref = refs[5:11]

            @pl.when(pl.program_id(0) == 0)
            def _():
                loss_ref[...] = jnp.zeros_like(loss_ref)
                dgf_ref[...] = jnp.zeros_like(dgf_ref)

        xv = x_ref[...]
        h = _rms_fwd(xv, g_ref[...])[0].astype(BF16)
        gate = _nt(h, wg_v[...])
        up = _nt(h, wu_v[...])
        gate_ref[...] = gate.astype(BF16)
        up_ref[...] = up.astype(BF16)
        act = (gate * _sigmoid(gate) * up).astype(BF16)
        xo = xv + _nn(act, wd_v[...])
        if head:
            gfv = gf_ref[...]
            y, xhat, r = _rms_fwd(xo, gfv)
            err = y - t_ref[...]
            loss_ref[...] += 0.5 * jnp.sum(_row_mean(err * err), axis=0, keepdims=True)
            dxr, dg_row = _rms_bwd(err * (1.0 / d), xhat, r, gfv)
            dgf_ref[...] += dg_row
            dx_ref[...] = dxr
            dxb_ref[...] = dxr.astype(BF16)
        else:
            refs[5][...] = xo

    act_specs = [_row_spec(tm, f), _row_spec(tm, f)]
    act_shapes = [jax.ShapeDtypeStruct((t_tok, f), BF16), jax.ShapeDtypeStruct((t_tok, f), BF16)]
    if head:
        out_specs = [_const_spec((1, 1)), _row_spec(tm, d), _row_spec(tm, d), _const_spec((1, d))]
        out_shape = [jax.ShapeDtypeStruct((1, 1), F32), jax.ShapeDtypeStruct((t_tok, d), F32),
                     jax.ShapeDtypeStruct((t_tok, d), BF16), jax.ShapeDtypeStruct((1, d), F32)]
    else:
        out_specs = [_row_spec(tm, d)]
        out_shape = [jax.ShapeDtypeStruct((t_tok, d), F32)]
    return _hosting_call(
        body, name, t_tok // tm, [x, g] + [arr for arr, _ in srcs] + list(head or ()),
        in_specs=[_row_spec(tm, d), _const_spec((1, d)), ANY, ANY, ANY] + [_row_spec(tm, d), _const_spec((1, d))][:n_head],
        out_specs=out_specs + act_specs, out_shape=out_shape + act_shapes,
        scratch=[pltpu.VMEM((f, d), BF16), pltpu.VMEM((f, d), BF16), pltpu.VMEM((f, d), BF16),
                 pltpu.SemaphoreType.DMA((3 * N_DEV,))],
        hosted=hosted)


def _ffn_bwd(dout, x, gate, up, g, srcs, nf, tm, name, hosted=()):
    t_tok, d = x.shape
    f = nf * N_DEV
    firsts = [first for _, first in srcs]
    per_chunk = -(-f // (FFN_CHUNKS * MXU_WIDTH)) * MXU_WIDTH
    bounds = [min(ck * per_chunk, f) for ck in range(FFN_CHUNKS + 1)]

    def body(dout_ref, x_ref, gate_ref, up_ref, g_ref, sg_ref, su_ref, sd_ref,
             dx_ref, dxb_ref, h_ref, act_ref, dgu_ref, dg_ref, wg_v, wu_v, wd_v, sems):
        _load_group(
            [(sg_ref, firsts[0], nf, wg_v), (su_ref, firsts[1], nf, wu_v), (sd_ref, firsts[2], nf, wd_v)], sems)

        @pl.when(pl.program_id(0) == 0)
        def _():
            dg_ref[...] = jnp.zeros_like(dg_ref)

        xv = x_ref[...]
        gv = g_ref[...]
        hv, xhat, r = _rms_fwd(xv, gv)
        h_ref[...] = hv.astype(BF16)
        dov = dout_ref[...]
        dob = dov.astype(BF16)
        dh = None
        for ck in range(FFN_CHUNKS):
            cols = slice(bounds[ck], bounds[ck + 1])
            gate_v = gate_ref[:, cols].astype(F32)
            up_v = up_ref[:, cols].astype(F32)
            sig = _sigmoid(gate_v)
            silu = gate_v * sig
            act_ref[:, cols] = (silu * up_v).astype(BF16)
            dact = _nt(dob, wd_v[cols, :])
            dup = (dact * silu).astype(BF16)
            dgate = (dact * up_v * (sig * (1.0 + gate_v * (1.0 - sig)))).astype(BF16)
            dgu_ref[:, cols] = dgate
            dgu_ref[:, f + bounds[ck]:f + bounds[ck + 1]] = dup
            part = _nn(dgate, wg_v[cols, :]) + _nn(dup, wu_v[cols, :])
            dh = part if dh is None else dh + part
        dxr, dg_row = _rms_bwd(dh, xhat, r, gv)
        dg_ref[...] += dg_row
        dx = dov + dxr
        dx_ref[...] = dx
        dxb_ref[...] = dx.astype(BF16)

    return _hosting_call(
        body, name, t_tok // tm, [dout, x, gate, up, g] + [arr for arr, _ in srcs],
        in_specs=[_row_spec(tm, d), _row_spec(tm, d), _row_spec(tm, f), _row_spec(tm, f), _const_spec((1, d)),
                  ANY, ANY, ANY],
        out_specs=[_row_spec(tm, d), _row_spec(tm, d), _row_spec(tm, d), _row_spec(tm, f), _row_spec(tm, 2 * f),
                   _const_spec((1, d))],
        out_shape=[jax.ShapeDtypeStruct((t_tok, d), F32), jax.ShapeDtypeStruct((t_tok, d), BF16),
                   jax.ShapeDtypeStruct((t_tok, d), BF16), jax.ShapeDtypeStruct((t_tok, f), BF16),
                   jax.ShapeDtypeStruct((t_tok, 2 * f), BF16), jax.ShapeDtypeStruct((1, d), F32)],
        scratch=[pltpu.VMEM((f, d), BF16), pltpu.VMEM((f, d), BF16), pltpu.VMEM((f, d), BF16),
                 pltpu.SemaphoreType.DMA((3 * N_DEV,))],
        hosted=hosted)


def _shift_down(z, k, prev_rows):
    row = lax.broadcasted_iota(jnp.int32, z.shape, 0)
    out = pltpu.roll(z, k, 0)
    for j in range(k):
        out = jnp.where(row == j, prev_rows[j], out)
    return out


def _shift_up(z, k, next_rows):
    tm = z.shape[0]
    row = lax.broadcasted_iota(jnp.int32, z.shape, 0)
    out = pltpu.roll(z, tm - k, 0)
    for j in range(k):
        out = jnp.where(row == tm - k + j, next_rows[j], out)
    return out


def _mixer_b_fwd(x, g, gath, conv_w, tm, seq, hosted=()):
    t_tok, d = x.shape
    e = conv_w.shape[1]
    e3 = 3 * e
    n_in, n_out = e3 // N_DEV, e // N_DEV
    tiles_per_seq = seq // tm

    def body(x_ref, g_ref, cw_ref, gath_ref, xo_ref, p_ref, win_v, wout_v, tail_v, sems):
        i = pl.program_id(0)
        _load_group([(gath_ref, 0, n_in, win_v), (gath_ref, n_in, n_out, wout_v)], sems)

        @pl.when(i % tiles_per_seq == 0)
        def _():
            tail_v[...] = jnp.zeros_like(tail_v)

        xv = x_ref[...]
        h = _rms_fwd(xv, g_ref[...])[0].astype(BF16)
        p = _nt(h, win_v[...])
        p_ref[...] = p.astype(BF16)
        z = p[:, e:2 * e] * p[:, 2 * e:]
        prev = [tail_v[SUBLANES - 2:SUBLANES - 1, :], tail_v[SUBLANES - 1:SUBLANES, :]]
        conv = (cw_ref[2:3, :] * z + cw_ref[1:2, :] * _shift_down(z, 1, prev[1:])
                + cw_ref[0:1, :] * _shift_down(z, 2, prev))
        tail_v[...] = z[tm - SUBLANES:, :]
        y = (p[:, :e] * conv).astype(BF16)
        xo_ref[...] = xv + _nn(y, wout_v[...])

    return _hosting_call(
        body, "mixer_b_fwd", t_tok // tm, [x, g, conv_w, gath],
        in_specs=[_row_spec(tm, d), _const_spec((1, d)), _const_spec((SUBLANES, e)), ANY],
        out_specs=[_row_spec(tm, d), _row_spec(tm, e3)],
        out_shape=[jax.ShapeDtypeStruct((t_tok, d), F32), jax.ShapeDtypeStruct((t_tok, e3), BF16)],
        scratch=[pltpu.VMEM((e3, d), BF16), pltpu.VMEM((e, d), BF16), pltpu.VMEM((SUBLANES, e), F32),
                 pltpu.SemaphoreType.DMA((2 * N_DEV,))],
        hosted=hosted)


def _mixer_b_bwd(dout, x, p, g, gath, conv_w, tm, seq, hosted=()):
    t_tok, d = x.shape
    e = conv_w.shape[1]
    e3 = 3 * e
    n_in, n_out = e3 // N_DEV, e // N_DEV
    tiles_per_seq = seq // tm
    halo_per_tile = tm // HALO
    n_halo = t_tok // HALO

    def body(dout_ref, dnext_ref, x_ref, p_ref, pprev_ref, pnext_ref, g_ref, cw_ref, gath_ref,
             dx_ref, dxb_ref, h_ref, y_ref, dp_ref, dg_ref, dcw_ref, win_v, wout_v, sems):
        i = pl.program_id(0)
        _load_group([(gath_ref, 0, n_in, win_v), (gath_ref, n_in, n_out, wout_v)], sems)

        @pl.when(i == 0)
        def _():
            dg_ref[...] = jnp.zeros_like(dg_ref)
            dcw_ref[...] = jnp.zeros_like(dcw_ref)

        first = (i % tiles_per_seq == 0).astype(F32)
        last = (i % tiles_per_seq == tiles_per_seq - 1).astype(F32)
        xv = x_ref[...]
        gv = g_ref[...]
        hv, xhat, r = _rms_fwd(xv, gv)
        h_ref[...] = hv.astype(BF16)
        pv = p_ref[...].astype(F32)
        bg, cg, hx = pv[:, :e], pv[:, e:2 * e], pv[:, 2 * e:]
        z = cg * hx
        pprev = pprev_ref[...].astype(F32)
        zprev = pprev[:, e:2 * e] * pprev[:, 2 * e:] * (1.0 - first)
        prev = [zprev[HALO - 2:HALO - 1, :], zprev[HALO - 1:HALO, :]]
        zs1 = _shift_down(z, 1, prev[1:])
        zs2 = _shift_down(z, 2, prev)
        w0, w1, w2 = cw_ref[0:1, :], cw_ref[1:2, :], cw_ref[2:3, :]
        conv = w2 * z + w1 * zs1 + w0 * zs2
        y_ref[...] = (bg * conv).astype(BF16)

        dov = dout_ref[...]
        wout_bf = wout_v[...]
        dy = _nt(dov.astype(BF16), wout_bf)
        dconv = dy * bg
        dnext = _nt(dnext_ref[...].astype(BF16), wout_bf) * pnext_ref[:, :e].astype(F32) * (1.0 - last)
        nxt = [dnext[0:1, :], dnext[1:2, :]]
        dz = w2 * dconv + w1 * _shift_up(dconv, 1, nxt[:1]) + w0 * _shift_up(dconv, 2, nxt)
        dcw_ref[0:1, :] += _col_sum(dconv * zs2)
        dcw_ref[1:2, :] += _col_sum(dconv * zs1)
        dcw_ref[2:3, :] += _col_sum(dconv * z)
        dp = jnp.concatenate([dy * conv, dz * hx, dz * cg], axis=1).astype(BF16)
        dp_ref[...] = dp
        dh = _nn(dp, win_v[...])
        dxr, dg_row = _rms_bwd(dh, xhat, r, gv)
        dg_ref[...] += dg_row
        dx = dov + dxr
        dx_ref[...] = dx
        dxb_ref[...] = dx.astype(BF16)

    prev_spec = lambda w: pl.BlockSpec((HALO, w), lambda i: (jnp.maximum(i * halo_per_tile - 1, 0), 0))
    next_spec = lambda w: pl.BlockSpec((HALO, w), lambda i: (jnp.minimum((i + 1) * halo_per_tile, n_halo - 1), 0))
    return _hosting_call(
        body, "mixer_b_bwd", t_tok // tm, [dout, dout, x, p, p, p, g, conv_w, gath],
        in_specs=[_row_spec(tm, d), next_spec(d), _row_spec(tm, d), _row_spec(tm, e3), prev_spec(e3), next_spec(e3),
                  _const_spec((1, d)), _const_spec((SUBLANES, e)), ANY],
        out_specs=[_row_spec(tm, d), _row_spec(tm, d), _row_spec(tm, d), _row_spec(tm, e), _row_spec(tm, e3),
                   _const_spec((1, d)), _const_spec((SUBLANES, e))],
        out_shape=[jax.ShapeDtypeStruct((t_tok, d), F32), jax.ShapeDtypeStruct((t_tok, d), BF16),
                   jax.ShapeDtypeStruct((t_tok, d), BF16), jax.ShapeDtypeStruct((t_tok, e), BF16),
                   jax.ShapeDtypeStruct((t_tok, e3), BF16), jax.ShapeDtypeStruct((1, d), F32),
                   jax.ShapeDtypeStruct((SUBLANES, e), F32)],
        scratch=[pltpu.VMEM((e3, d), BF16), pltpu.VMEM((e, d), BF16), pltpu.SemaphoreType.DMA((2 * N_DEV,))],
        hosted=hosted)


def _wgrad(a, b, bm, name, hosted=(), part=(1, 0)):
    t_tok, m = a.shape
    n = b.shape[1]
    every, first = part

    def body(a_ref, b_ref, o_ref):
        o_ref[...] = _tn(a_ref[...], b_ref[...]).astype(o_ref.dtype)

    outs, h_outs = _hosting_call(
        body, name, m // (bm * every), [a, b],
        in_specs=[pl.BlockSpec((t_tok, bm), lambda i: (0, every * i + first)), _const_spec((t_tok, n))],
        out_specs=[pl.BlockSpec((bm, n), lambda i: (i, 0))],
        out_shape=[jax.ShapeDtypeStruct((m // every, n), BF16)],
        scratch=[], hosted=hosted)
    return (outs[0], h_outs) if hosted else outs[0]


def _sum_slots(land, rb, name):
    n_slots, rows, cols = land.shape

    def body(l_ref, o_ref):
        acc = l_ref[0].astype(F32)
        for k in range(1, n_slots):
            acc = acc + l_ref[k].astype(F32)
        o_ref[...] = acc

    return pl.pallas_call(
        body, name=name, grid=(rows // rb,),
        in_specs=[pl.BlockSpec((n_slots, rb, cols), lambda i: (0, i, 0))],
        out_specs=pl.BlockSpec((rb, cols), lambda i: (i, 0)),
        out_shape=jax.ShapeDtypeStruct((rows, cols), F32),
        compiler_params=_params(sequential=False),
    )(land)


def _adamw(w, grad, m, v, rb, name):
    rows, cols = w.shape
    c1 = 1.0 / (1.0 - ADAM_B1 ** ADAM_STEP)
    c2 = 1.0 / (1.0 - ADAM_B2 ** ADAM_STEP)

    def body(w_ref, g_ref, m_ref, v_ref, d_ref, mo_ref, vo_ref):
        gv = g_ref[...]
        mn = ADAM_B1 * m_ref[...] + (1.0 - ADAM_B1) * gv
        vn = ADAM_B2 * v_ref[...] + (1.0 - ADAM_B2) * (gv * gv)
        mo_ref[...] = mn
        vo_ref[...] = vn
        d_ref[...] = -ADAM_LR * ((mn * c1) / (jnp.sqrt(vn * c2) + ADAM_EPS) + ADAM_WD * w_ref[...])

    spec = pl.BlockSpec((rb, cols), lambda i: (i, 0))
    shape = jax.ShapeDtypeStruct((rows, cols), F32)
    return pl.pallas_call(
        body, name=name, grid=(rows // rb,),
        in_specs=[spec] * 4, out_specs=[spec] * 3, out_shape=[shape] * 3,
        compiler_params=_params(sequential=False),
    )(w, grad, m, v)


def _pack_shards(groups, name, hosted=()):
    flat = [(part, layer) for group in groups for part, layer, _ in group]
    rows = [[p.shape[2] if turn else p.shape[1] for p, _, turn in group] for group in groups]
    first, _, first_turn = groups[0][0]
    width = first.shape[1] if first_turn else first.shape[2]

    def body(*refs):
        ins, outs = refs[:len(flat)], refs[len(flat):]
        k = 0
        for gi, group in enumerate(groups):
            off = 0
            for (_, _, turn), n in zip(group, rows[gi]):
                part = ins[k][...].astype(BF16)
                if turn:
                    r = lax.broadcasted_iota(jnp.int32, (n, n), 0)
                    c = lax.broadcasted_iota(jnp.int32, (n, n), 1)
                    part = _nt((r == c).astype(BF16), part).astype(BF16)
                outs[gi][off:off + n, :] = part
                off += n
                k += 1

    return _hosting_call(
        body, name, 1, [p for p, _ in flat],
        in_specs=[pl.BlockSpec((None,) + p.shape[1:], lambda i, layer=layer: (layer, 0, 0)) for p, layer in flat],
        out_specs=[_const_spec((sum(r), width)) for r in rows],
        out_shape=[jax.ShapeDtypeStruct((sum(r), width), BF16) for r in rows],
        scratch=[], hosted=hosted)


def _split_bf16(a):
    hi = a.astype(BF16)
    rest = a - hi.astype(F32)
    mid = rest.astype(BF16)
    return hi, mid, (rest - mid.astype(F32)).astype(BF16)


def _reduce_adamw(lands, w, m, v, transpose, name, hosted=()):
    n_layers, rows_w, cols_w = w.shape
    c1 = 1.0 / (1.0 - ADAM_B1 ** ADAM_STEP)
    c2 = 1.0 / (1.0 - ADAM_B2 ** ADAM_STEP)
    flat = [piece for pieces in lands for piece in pieces]
    counts = [len(pieces) for pieces in lands]
    if transpose:
        tiles = rows_w // MXU_WIDTH
        blk = (MXU_WIDTH, cols_w)
        land_specs = [pl.BlockSpec((N_CHIP, n, MXU_WIDTH), lambda i, b=first // n: (0, b, i % tiles))
                      for _, first, n in flat]
        for _, first, n in flat:
            assert first % n == 0
    else:
        tiles = 1
        blk = (rows_w // tiles, cols_w)
        assert all(c == 1 for c in counts)
        land_specs = [pl.BlockSpec((N_CHIP,) + blk, lambda i, b=first // blk[0]: (0, b + i % tiles, 0))
                      for _, first, _ in flat]
        for _, first, _ in flat:
            assert first % blk[0] == 0

    def body(*refs):
        land_refs = refs[:len(flat)]
        w_ref, m_ref, v_ref, g_ref, d_ref, mo_ref, vo_ref = refs[len(flat):]
        layer = pl.program_id(0) // tiles

        def total(ref):
            acc = ref[0].astype(F32)
            for q in range(1, N_CHIP):
                acc = acc + ref[q].astype(F32)
            return acc

        def layer_sum(k):
            first = sum(counts[:k])
            parts = [total(land_refs[first + j]) for j in range(counts[k])]
            return parts[0] if len(parts) == 1 else jnp.concatenate(parts, axis=0)

        gv = layer_sum(0)
        for k in range(1, n_layers):
            gv = jnp.where(layer == k, layer_sum(k), gv)
        if transpose:
            r = lax.broadcasted_iota(jnp.int32, (MXU_WIDTH, MXU_WIDTH), 0)
            c = lax.broadcasted_iota(jnp.int32, (MXU_WIDTH, MXU_WIDTH), 1)
            eye = (r == c).astype(BF16)
            hi, mid, lo = _split_bf16(gv)
            gv = _nt(eye, hi) + _nt(eye, mid) + _nt(eye, lo)
        g_ref[...] = gv
        mn = ADAM_B1 * m_ref[...] + (1.0 - ADAM_B1) * gv
        vn = ADAM_B2 * v_ref[...] + (1.0 - ADAM_B2) * (gv * gv)
        mo_ref[...] = mn
        vo_ref[...] = vn
        d_ref[...] = -ADAM_LR * ((mn * c1) / (jnp.sqrt(vn * c2) + ADAM_EPS) + ADAM_WD * w_ref[...])

    spec = pl.BlockSpec((None,) + blk, lambda i: (i // tiles, i % tiles, 0))
    shape = jax.ShapeDtypeStruct(w.shape, F32)
    outs, h_outs = _hosting_call(
        body, name, n_layers * tiles, [land for land, _, _ in flat] + [w, m, v],
        in_specs=land_specs + [spec] * 3, out_specs=[spec] * 4, out_shape=[shape] * 4, scratch=[], hosted=hosted)
    return (outs, h_outs) if hosted else outs


def _pack_small(parts, rows):
    flat = jnp.concatenate([p.reshape(-1).astype(F32) for p in parts])
    return jnp.pad(flat, (0, rows * LANES - flat.shape[0])).reshape(rows, LANES)


def _unpack_small(packed, shapes):
    flat = packed.reshape(-1)
    out = []
    pos = 0
    for s in shapes:
        n = math.prod(s)
        out.append(flat[pos:pos + n].reshape(s))
        pos += n
    return out


def kernel(x, mix_norm, ffn_norm, a_w_in, a_v_gain, a_v_bias, a_w_s, a_b_s, a_w_out, b_w_in, b_conv_w, b_w_out, ffn_w_gate, ffn_w_up, ffn_w_down, final_norm, loss_target, m_mix_norm, m_ffn_norm, m_a_w_in, m_a_v_gain, m_a_v_bias, m_a_w_s, m_a_b_s, m_a_w_out, m_b_w_in, m_b_conv_w, m_b_w_out, m_ffn_w_gate, m_ffn_w_up, m_ffn_w_down, m_final_norm, v_mix_norm, v_ffn_norm, v_a_w_in, v_a_v_gain, v_a_v_bias, v_a_w_s, v_a_b_s, v_a_w_out, v_b_w_in, v_b_conv_w, v_b_w_out, v_ffn_w_gate, v_ffn_w_up, v_ffn_w_down, v_final_norm):
    bsz, seq, d = x.shape
    t_tok = bsz * seq
    me = _my_index()
    xt = x.reshape(t_tok, d)
    target = loss_target.reshape(t_tok, d)
    e_a = a_v_gain.shape[1]
    e_b = b_w_out.shape[1] * N_DEV
    n_layers = ffn_w_gate.shape[0]
    f_shard = ffn_w_gate.shape[2]
    f_full = f_shard * N_DEV

    conv_pad = jnp.pad(b_conv_w[0], ((0, SUBLANES - CONV_W), (0, 0)))
    sh_a = jnp.concatenate([a_w_in[0].T, a_w_out[0]]).astype(BF16)
    bfull = jnp.repeat(a_b_s[0].T, GROUP, axis=1)

    gate_t, up_t = ffn_w_gate.transpose(0, 2, 1), ffn_w_up.transpose(0, 2, 1)
    (sh_b, sh_f0, sh_f1g, sh_f1ud), (gath_a, conv_g) = _pack_shards(
        [[(b_w_in, 0, True), (b_w_out, 0, False)],
         [(gate_t, 0, False), (up_t, 0, False), (ffn_w_down, 0, False)],
         [(gate_t, 1, False)],
         [(up_t, 1, False), (ffn_w_down, 1, False)]],
        "pack_shards", hosted=[_HostedGathers([sh_a, conv_pad])])
    conv_full = jnp.pad(conv_g[:, :CONV_W, :].transpose(1, 0, 2).reshape(CONV_W, e_b), ((0, SUBLANES - CONV_W), (0, 0)))
    (x1, gd_a, u_a, vhat_a, sv_a, y_a, rstd_a), (gath_f0,) = _mixer_a_fwd(
        xt, mix_norm[0:1], gath_a, a_v_gain, a_v_bias, a_w_s[0], bfull, tm=TOKEN_TILE,
        hosted=[_HostedGathers([sh_f0])])
    srcs0 = [(gath_f0, 0), (gath_f0, f_shard), (gath_f0, 2 * f_shard)]
    (x2, gate0, up0), (gath_b, gath_f1g) = _ffn_fwd(x1, ffn_norm[0:1], srcs0, f_shard, tm=TOKEN_TILE, name="ffn_fwd0",
                                                    hosted=[_HostedGathers([sh_b, sh_f1g])])
    (x3, p_b), (gath_f1ud,) = _mixer_b_fwd(x2, mix_norm[1:2], gath_b, conv_full, tm=TOKEN_TILE, seq=seq,
                                           hosted=[_HostedGathers([sh_f1ud])])
    srcs1 = [(gath_f1g, 0), (gath_f1ud, 0), (gath_f1ud, f_shard)]
    (loss_part, dx4, dx4_bf, d_final, gate1, up1), _ = _ffn_fwd(
        x3, ffn_norm[1:2], srcs1, f_shard, tm=TOKEN_TILE_WIDE, name="ffn_fwd1", head=(target, final_norm.reshape(1, d)))

    ffn_entries = [(0, 0, f_shard), (0, f_full, f_shard), (1, 0, f_shard)]
    (dx3, dx3_bf, h_f1, act1, dgu1, d_fn1), _ = _ffn_bwd(dx4, x3, gate1, up1, ffn_norm[1:2], srcs1, f_shard, tm=TOKEN_TILE,
                                                         name="ffn_bwd1")
    g_down1 = _wgrad(act1, dx4_bf, WGRAD_ROWS, "wgrad_down1")
    g_gu1 = _wgrad(dgu1, h_f1, WGRAD_ROWS_WIDE, "wgrad_gate_up1")
    ps_f1 = _pair_reduce([g_gu1, g_down1], ffn_entries, "pair_reduce_f1")
    (dx2, dx2_bf, h_b, y_b, dp_b, d_mn1, d_conv), (land_f1gu,) = _mixer_b_bwd(
        dx3, x2, p_b, mix_norm[1:2], gath_b, conv_full, tm=TOKEN_TILE_WIDE, seq=seq,
        hosted=[_HostedChipScatter(ps_f1, 0, 2 * f_shard)])
    g_b_out = _wgrad(y_b, dx3_bf, WGRAD_ROWS, "wgrad_b_out")
    g_b_in = _wgrad(dp_b, h_b, WGRAD_ROWS_WIDE, "wgrad_b_in")
    ps_b = _pair_reduce([g_b_in, g_b_out], [(0, 0, b_w_in.shape[2]), (1, 0, b_w_out.shape[1])], "pair_reduce_b")
    (dx1, dx1_bf, h_f0, act0, dgu0, d_fn0), (land_f1d, land_b) = _ffn_bwd(
        dx2, x1, gate0, up0, ffn_norm[0:1], srcs0, f_shard, tm=TOKEN_TILE, name="ffn_bwd0",
        hosted=[_HostedChipScatter(ps_f1, 2 * f_shard, f_shard), _HostedChipScatter(ps_b)])
    g_down0 = _wgrad(act0, dx2_bf, WGRAD_ROWS, "wgrad_down0")
    g_gu0 = _wgrad(dgu0, h_f0, WGRAD_ROWS_WIDE, "wgrad_gate_up0")
    g_a_out = _wgrad(y_a, dx1_bf, WGRAD_ROWS, "wgrad_a_out")
    n_ao = a_w_out.shape[1]
    ps_f0ao = _pair_reduce([g_gu0, g_down0, g_a_out], ffn_entries + [(2, 0, n_ao)], "pair_reduce_f0_a_out")
    (dx0, _, h_a, dz_a, d_mn0, d_gain, d_bias, d_ws, d_bs_acc), (land_f0, land_ao) = _mixer_a_bwd(
        dx1, xt, gd_a, u_a, vhat_a, sv_a, rstd_a, mix_norm[0:1], gath_a, a_v_gain, a_v_bias, a_w_s[0], tm=TOKEN_TILE,
        hosted=[_HostedChipScatter(ps_f0ao, 0, 3 * f_shard), _HostedChipScatter(ps_f0ao, 3 * f_shard, n_ao)])
    d_bs = d_bs_acc.reshape(HEADS, CHUNK)

    small_grads = [jnp.concatenate([d_mn0, d_mn1]), jnp.concatenate([d_fn0, d_fn1]), d_gain, d_bias, d_ws, d_bs,
                   d_final, d_conv[:CONV_W], loss_part]
    small_shapes = [(n_layers, d), (n_layers, d), (1, e_a), (1, e_a), (1, HEADS, CHUNK, CHUNK), (1, HEADS, CHUNK), (d,),
                    (CONV_W, e_b), ()]
    n_small = sum(math.prod(s) for s in small_shapes)
    blk_rows = -(-n_small // (N_DEV * LANES * SUBLANES)) * SUBLANES
    small_rows = blk_rows * N_DEV
    packed = _pack_small(small_grads, small_rows)
    n_half = a_w_in.shape[2] // 2
    g_ai0, (small_land,) = _wgrad(dz_a, h_a, n_half, "wgrad_a_in0", hosted=[_HostedScatterAll(packed)], part=(2, 0))
    ps_ai0 = _pair_reduce([g_ai0], [(0, 0, n_half)], "pair_reduce_a_in0")
    small_sum = _sum_slots(small_land, blk_rows, "sum_small")
    g_ai1, (land_ai0,) = _wgrad(dz_a, h_a, n_half, "wgrad_a_in1", hosted=[_HostedChipScatter(ps_ai0)], part=(2, 1))
    ps_ai1 = _pair_reduce([g_ai1], [(0, 0, n_half)], "pair_reduce_a_in1")
    land_ai1, small_gath = _exchange([_HostedChipScatter(ps_ai1), _HostedGathers([small_sum])], "tail_exchange")
    small_all = small_gath.reshape(small_rows, LANES)

    n_b_in = b_w_in.shape[2]
    gate_out = _reduce_adamw([[(land_f0, 0, f_shard)], [(land_f1gu, 0, f_shard)]], gate_t,
                             m_ffn_w_gate.transpose(0, 2, 1), v_ffn_w_gate.transpose(0, 2, 1), False, "adamw_gate")
    up_out = _reduce_adamw([[(land_f0, f_shard, f_shard)], [(land_f1gu, f_shard, f_shard)]], up_t,
                           m_ffn_w_up.transpose(0, 2, 1), v_ffn_w_up.transpose(0, 2, 1), False, "adamw_up")
    res = {
        "a_w_in": _reduce_adamw([[(land_ai0, 0, n_half), (land_ai1, 0, n_half)]], a_w_in, m_a_w_in, v_a_w_in, True,
                                "adamw_a_in"),
        "a_w_out": _reduce_adamw([[(land_ao, 0, a_w_out.shape[1])]], a_w_out, m_a_w_out, v_a_w_out, False,
                                 "adamw_a_out"),
        "b_w_in": _reduce_adamw([[(land_b, 0, n_b_in)]], b_w_in, m_b_w_in, v_b_w_in, True, "adamw_b_in"),
        "b_w_out": _reduce_adamw([[(land_b, n_b_in, b_w_out.shape[1])]], b_w_out, m_b_w_out, v_b_w_out, False,
                                 "adamw_b_out"),
        "ffn_w_gate": [o.transpose(0, 2, 1) for o in gate_out],
        "ffn_w_up": [o.transpose(0, 2, 1) for o in up_out],
        "ffn_w_down": _reduce_adamw([[(land_f0, 2 * f_shard, f_shard)], [(land_f1d, 0, f_shard)]], ffn_w_down,
                                    m_ffn_w_down, v_ffn_w_down, False, "adamw_down"),
    }

    (gr_mix, gr_ffn, gr_gain, gr_bias, gr_ws, gr_bs, gr_final, gr_conv_full, loss) = _unpack_small(small_all, small_shapes)
    gr_conv = lax.dynamic_slice_in_dim(gr_conv_full, me * (e_b // N_DEV), e_b // N_DEV, axis=1)[None]

    small_w =[mix_norm, ffn_norm, a_v_gain, a_v_bias, a_w_s, a_b_s, final_norm]
    small_m = [m_mix_norm, m_ffn_norm, m_a_v_gain, m_a_v_bias, m_a_w_s, m_a_b_s, m_final_norm]
    small_v = [v_mix_norm, v_ffn_norm, v_a_v_gain, v_a_v_bias, v_a_w_s, v_a_b_s, v_final_norm]
    small_g = [gr_mix, gr_ffn, gr_gain, gr_bias, gr_ws, gr_bs, gr_final]
    sm_shapes = small_shapes[:len(small_w)]
    sm_out = _adamw(_pack_small(small_w, small_rows), _pack_small(small_g, small_rows), _pack_small(small_m, small_rows),
                    _pack_small(small_v, small_rows), small_rows, "adamw_small")
    sm_delta, sm_m, sm_v = [_unpack_small(o, sm_shapes) for o in sm_out]

    conv_out = _adamw(b_conv_w[0], gr_conv[0], m_b_conv_w[0], v_b_conv_w[0], CONV_W, "adamw_conv")
    conv_delta, conv_m, conv_v = [o[None] for o in conv_out]

    order = ["mix_norm", "ffn_norm", "a_w_in", "a_v_gain", "a_v_bias", "a_w_s", "a_b_s", "a_w_out", "b_w_in",
             "b_conv_w", "b_w_out", "ffn_w_gate", "ffn_w_up", "ffn_w_down", "final_norm"]
    small_names = ["mix_norm", "ffn_norm", "a_v_gain", "a_v_bias", "a_w_s", "a_b_s", "final_norm"]
    grads = {"b_conv_w": gr_conv}
    deltas, new_m, new_v = {}, {}, {}
    for k, name in enumerate(small_names):
        grads[name] = small_g[k]
        deltas[name], new_m[name], new_v[name] = sm_delta[k], sm_m[k], sm_v[k]
    deltas["b_conv_w"], new_m["b_conv_w"], new_v["b_conv_w"] = conv_delta, conv_m, conv_v
    for name, (gg, dl, mm, vv) in res.items():
        grads[name], deltas[name], new_m[name], new_v[name] = gg, dl, mm, vv

    grad_x = dx0.reshape(bsz, seq, d)
    return (loss, grad_x, *[grads[n] for n in order], *[deltas[n] for n in order],
            *[new_m[n] for n in order], *[new_v[n] for n in order])
```

```python
import math

import jax
import jax.numpy as jnp
from jax import lax
from jax.experimental import pallas as pl
from jax.experimental.pallas import tpu as pltpu

F32 = jnp.float32
BF16 = jnp.bfloat16

N_DEV = 8
N_CHIP = 4
CHUNK = 128
HEADS = 16
GROUP = 128
CONV_W = 3
NORM_EPS = 1e-6
GELU_C = math.sqrt(2.0 / math.pi)
GELU_K = 0.044715

ADAM_LR = 0.001
ADAM_B1 = 0.9
ADAM_B2 = 0.999
ADAM_EPS = 1e-08
ADAM_WD = 0.01
ADAM_STEP = 10

LANES = 128
SUBLANES = 8
VMEM_LIMIT = 60 * 1024 * 1024
HALO = 16
MXU_WIDTH = 256
FFN_CHUNKS = 2
TOKEN_TILE = 256
TOKEN_TILE_WIDE = 512
WGRAD_ROWS = 256
WGRAD_ROWS_WIDE = 512
GATHER_RELAY_AT = 0.56
GATHER_FORWARD_LEAD = 2

MESH = pl.DeviceIdType.MESH
ANY = pl.BlockSpec(memory_space=pl.ANY)

PEER_FLIPS = {"sibling": (0, 0, 1), "x": (1, 0, 0), "y": (0, 1, 0), "diagonal": (1, 1, 0),
              "x_other": (1, 0, 1), "y_other": (0, 1, 1), "diagonal_other": (1, 1, 1)}
COLLECTIVE_IDS = {frozenset(["sibling"]): 0,
                  frozenset(["sibling", "x", "y"]): 1,
                  frozenset(["sibling", "x", "y", "diagonal"]): 2,
                  frozenset(["x", "y", "diagonal"]): 3,
                  frozenset(PEER_FLIPS): 4}


def _params(sequential=True):
    return pltpu.CompilerParams(
        dimension_semantics=("arbitrary",) if sequential else None,
        vmem_limit_bytes=VMEM_LIMIT)


def _nn(a, b):
    return jnp.dot(a, b, preferred_element_type=F32)


def _nt(a, b):
    return lax.dot_general(a, b, (((1,), (1,)), ((), ())), preferred_element_type=F32)


def _tn(a, b):
    return lax.dot_general(a, b, (((0,), (0,)), ((), ())), preferred_element_type=F32)


def _row_mean(a):
    return jnp.mean(a, axis=-1, keepdims=True)


def _col_sum(a):
    return jnp.sum(a, axis=0, keepdims=True)


def _rms_fwd(x, g):
    r = lax.rsqrt(_row_mean(x * x) + NORM_EPS)
    xhat = x * r
    return xhat * g, xhat, r


def _rms_bwd(dh, xhat, r, g):
    a = dh * g
    dx = r * (a - xhat * _row_mean(a * xhat))
    return dx, _col_sum(dh * xhat)


def _gelu_and_grad(x):
    x2 = x * x
    t = jnp.tanh(x * (GELU_C + (GELU_C * GELU_K) * x2))
    half = 0.5 * t + 0.5
    d = half + x * (0.5 - 0.5 * (t * t)) * (GELU_C + (3.0 * GELU_C * GELU_K) * x2)
    return x * half, d


def _sigmoid(x):
    return 1.0 / (1.0 + jnp.exp(-x))


def _row_spec(tm, width):
    return pl.BlockSpec((tm, width), lambda i: (i, 0))


def _const_spec(shape):
    nd = len(shape)
    return pl.BlockSpec(shape, lambda i: (0,) * nd)


def _load_group(parts, sems):
    @pl.when(pl.program_id(0) == 0)
    def _():
        copies = []
        for k, (gath_ref, first, n, dst) in enumerate(parts):
            for j in range(N_DEV):
                copies.append(pltpu.make_async_copy(gath_ref.at[j, pl.ds(first, n), :], dst.at[pl.ds(j * n, n), :],
                                                    sems.at[k * N_DEV + j]))
        for cp in copies:
            cp.start()
        for cp in copies:
            cp.wait()


def _hosting_call(body, name, n_steps, arrays, in_specs, out_specs, out_shape, scratch, hosted=()):
    n_in, n_out, n_scr = len(arrays), len(out_shape), len(scratch)
    h_arrays = [a for h in hosted for a in h.arrays]
    h_shapes = [s for h in hosted for s in h.out_shapes]
    h_sems = [s for h in hosted for s in h.sem_shapes]
    peers = sorted(set().union(*[h.peers for h in hosted])) if hosted else []

    def handshake():
        @pl.when(pl.program_id(0) == 0)
        def _():
            x, y, c = lax.axis_index("x"), lax.axis_index("y"), lax.axis_index("c")
            barrier = pltpu.get_barrier_semaphore()
            for p in peers:
                fx, fy, fc = PEER_FLIPS[p]
                peer = (1 - x if fx else x, 1 - y if fy else y, 1 - c if fc else c)
                pl.semaphore_signal(barrier, inc=1, device_id=peer, device_id_type=MESH)
            pl.semaphore_wait(barrier, len(peers))

    def full_body(*refs):
        pos = 0
        groups = []
        for n in (n_in, len(h_arrays), n_out, len(h_shapes), n_scr, len(h_sems)):
            groups.append(refs[pos:pos + n])
            pos += n
        own_in, h_in, own_out, h_out, own_scr, h_sem = groups
        per_host = []
        pi = po = ps = 0
        for h in hosted:
            ni, no, ns = len(h.arrays), len(h.out_shapes), len(h.sem_shapes)
            per_host.append((h, h_in[pi:pi + ni], h_out[po:po + no], h_sem[ps:ps + ns]))
            pi, po, ps = pi + ni, po + no, ps + ns
        if hosted:
            handshake()
        for h, ins, outs, sems in per_host:
            h.begin(ins, outs, sems, n_steps)
        body(*own_in, *own_out, *own_scr)
        for h, ins, outs, sems in per_host:
            h.end(ins, outs, sems, n_steps)

    outs = pl.pallas_call(
        full_body, name=name, grid=(n_steps,),
        in_specs=list(in_specs) + [ANY] * len(h_arrays),
        out_specs=list(out_specs) + [ANY] * len(h_shapes),
        out_shape=list(out_shape) + h_shapes,
        scratch_shapes=list(scratch) + h_sems,
        compiler_params=pltpu.CompilerParams(
            dimension_semantics=("arbitrary",), vmem_limit_bytes=VMEM_LIMIT,
            collective_id=COLLECTIVE_IDS[frozenset(peers)] if hosted else None),
    )(*arrays, *h_arrays)
    return outs[:n_out], outs[n_out:]


def _my_index():
    return 4 * lax.axis_index("x") + 2 * lax.axis_index("y") + lax.axis_index("c")


GATHER_COPIES = 8


def _gather_relays(n_rows, dtype):
    rows_per_tile = SUBLANES * 4 // jnp.dtype(dtype).itemsize
    return n_rows % 2 == 0 and (n_rows // 2) % rows_per_tile == 0


class _Gather:
    def __init__(self, shard, out, send_sems, recv_sems, local_sem):
        self.shard, self.out = shard, out
        self.send_sems, self.recv_sems, self.local_sem = send_sems, recv_sems, local_sem
        x, y, c = lax.axis_index("x"), lax.axis_index("y"), lax.axis_index("c")
        self.c = c
        self.me, self.sibling = (x, y, c), (x, y, 1 - c)
        self.xn, self.yn, self.dg = (1 - x, y), (x, 1 - y), (1 - x, 1 - y)
        self.n = shard.shape[0]
        self.half = self.n // 2
        self.relays = _gather_relays(self.n, shard.dtype)

    def _slot(self, dev, lo=0, hi=None):
        hi = self.n if hi is None else hi
        return self.out.at[4 * dev[0] + 2 * dev[1] + dev[2], pl.ds(lo, hi - lo), :]

    def _copy(self, k, block, to, src=None, lo=0, hi=None):
        return pltpu.make_async_remote_copy(
            src_ref=self._slot(block, lo, hi) if src is None else src, dst_ref=self._slot(block, lo, hi),
            send_sem=self.send_sems.at[k], recv_sem=self.recv_sems.at[k], device_id=to, device_id_type=MESH)

    def _local(self):
        return pltpu.make_async_copy(self.shard, self._slot(self.me), self.local_sem)

    def start(self):
        c = self.c
        self._local().start()
        self._copy(0, self.me, self.sibling, src=self.shard).start()
        self._copy(1, self.me, (*self.xn, c), src=self.shard).start()
        self._copy(2, self.me, (*self.yn, c), src=self.shard).start()
        if not self.relays:
            self._copy(3, self.me, (*self.dg, c), src=self.shard).start()

    def relay(self):
        c = self.c
        if self.relays:
            self._copy(1, (*self.xn, c), self.me).wait_recv()
            self._copy(3, (*self.xn, c), (*self.yn, c), hi=self.half).start()
            self._copy(2, (*self.yn, c), self.me).wait_recv()
            self._copy(4, (*self.yn, c), (*self.xn, c), lo=self.half).start()

    def forward(self):
        c = self.c
        if self.relays:
            self._copy(5, (*self.xn, c), self.sibling).start()
            self._copy(6, (*self.yn, c), self.sibling).start()
            self._copy(3, (*self.dg, c), self.me, hi=self.half).wait_recv()
            self._copy(4, (*self.dg, c), self.me, lo=self.half).wait_recv()
        else:
            self._copy(1, (*self.xn, c), self.me).wait_recv()
            self._copy(5, (*self.xn, c), self.sibling).start()
            self._copy(2, (*self.yn, c), self.me).wait_recv()
            self._copy(6, (*self.yn, c), self.sibling).start()
            self._copy(3, (*self.dg, c), self.me).wait_recv()
        self._copy(7, (*self.dg, c), self.sibling).start()

    def finish(self):
        c = self.c
        self._copy(0, self.sibling, self.me).wait_recv()
        for k, chip in ((5, self.xn), (6, self.yn), (7, self.dg)):
            self._copy(k, (*chip, 1 - c), self.me).wait_recv()
        for k in (0, 1, 2, 5, 6, 7):
            self._copy(k, self.me, self.sibling).wait_send()
        if self.relays:
            self._copy(3, self.me, self.sibling, hi=self.half).wait_send()
            self._copy(4, self.me, self.sibling, lo=self.half).wait_send()
        else:
            self._copy(3, self.me, self.sibling).wait_send()
        self._local().wait()


class _HostedGathers:
    def __init__(self, shards, mid_lead=GATHER_FORWARD_LEAD, relay_at=GATHER_RELAY_AT):
        n = len(shards)
        self.arrays = shards
        self.mid_lead, self.relay_at = mid_lead, relay_at
        self.peers = {"sibling", "x", "y"}
        if not all(_gather_relays(s.shape[0], s.dtype) for s in shards):
            self.peers.add("diagonal")
        self.out_shapes = [jax.ShapeDtypeStruct((N_DEV,) + s.shape, s.dtype) for s in shards]
        self.sem_shapes = [pltpu.SemaphoreType.DMA((n, GATHER_COPIES)), pltpu.SemaphoreType.DMA((n, GATHER_COPIES)),
                           pltpu.SemaphoreType.DMA((n,))]

    def _gathers(self, ins, outs, sems):
        return [_Gather(ins[a], outs[a], sems[0].at[a], sems[1].at[a], sems[2].at[a]) for a in range(len(ins))]

    def begin(self, ins, outs, sems, n_steps):
        i = pl.program_id(0)
        forward_step = max(n_steps - 1 - self.mid_lead, 0)
        relay_step = min(int(self.relay_at * n_steps), forward_step)

        @pl.when(i == 0)
        def _():
            for g in self._gathers(ins, outs, sems):
                g.start()

        if n_steps == 1:
            return

        @pl.when(i == relay_step)
        def _():
            for g in self._gathers(ins, outs, sems):
                g.relay()

        @pl.when(i == forward_step)
        def _():
            for g in self._gathers(ins, outs, sems):
                g.forward()

    def end(self, ins, outs, sems, n_steps):
        @pl.when(pl.program_id(0) == n_steps - 1)
        def _():
            gathers = self._gathers(ins, outs, sems)
            if n_steps == 1:
                for g in gathers:
                    g.relay()
                for g in gathers:
                    g.forward()
            for g in gathers:
                g.finish()


def _exchange(hosted, name):
    return _hosting_call(lambda: None, name, 1, [], [], [], [], [], hosted=hosted)[1]


class _ChipScatter:
    def __init__(self, pairsum, row0, land, send_sems, recv_sems, local_sem):
        self.pairsum, self.row0, self.land = pairsum, row0, land
        self.send_sems, self.recv_sems, self.local_sem = send_sems, recv_sems, local_sem
        x, y, c = lax.axis_index("x"), lax.axis_index("y"), lax.axis_index("c")
        self.c = c
        self.chip = 2 * x + y
        self.others = [(1 - x, y), (x, 1 - y), (1 - x, 1 - y)]

    def _src(self, chip):
        return self.pairsum.at[chip, pl.ds(self.row0, self.land.shape[1]), :]

    def _copy(self, k):
        ox, oy = self.others[k]
        return pltpu.make_async_remote_copy(
            src_ref=self._src(2 * ox + oy), dst_ref=self.land.at[self.chip],
            send_sem=self.send_sems.at[k], recv_sem=self.recv_sems.at[k], device_id=(ox, oy, self.c),
            device_id_type=MESH)

    def _arrival(self, k):
        ox, oy = self.others[k]
        return pltpu.make_async_remote_copy(
            src_ref=self._src(self.chip), dst_ref=self.land.at[2 * ox + oy],
            send_sem=self.send_sems.at[k], recv_sem=self.recv_sems.at[k], device_id=(ox, oy, self.c),
            device_id_type=MESH)

    def _local(self):
        return pltpu.make_async_copy(self._src(self.chip), self.land.at[self.chip], self.local_sem)

    def start(self):
        self._local().start()
        for k in range(N_CHIP - 1):
            self._copy(k).start()

    def finish(self):
        for k in range(N_CHIP - 1):
            self._arrival(k).wait_recv()
        for k in range(N_CHIP - 1):
            self._copy(k).wait_send()
        self._local().wait()


class _HostedChipScatter:
    def __init__(self, pairsum, row0=0, n=None):
        n = pairsum.shape[1] - row0 if n is None else n
        self.row0 = row0
        self.peers = {"x", "y", "diagonal"}
        self.arrays = [pairsum]
        self.out_shapes = [jax.ShapeDtypeStruct((N_CHIP, n, pairsum.shape[2]), pairsum.dtype)]
        self.sem_shapes = [pltpu.SemaphoreType.DMA((N_CHIP - 1,)), pltpu.SemaphoreType.DMA((N_CHIP - 1,)),
                           pltpu.SemaphoreType.DMA(())]

    def begin(self, ins, outs, sems, n_steps):
        @pl.when(pl.program_id(0) == 0)
        def _():
            _ChipScatter(ins[0], self.row0, outs[0], *sems).start()

    def end(self, ins, outs, sems, n_steps):
        @pl.when(pl.program_id(0) == n_steps - 1)
        def _():
            _ChipScatter(ins[0], self.row0, outs[0], *sems).finish()


def _pair_reduce(arrays, entries, name):
    n_arr, n_ent = len(arrays), len(entries)
    cols = arrays[0].shape[1]
    offsets = []
    total = 0
    for _, _, n in entries:
        offsets.append(total)
        total += n

    def body(*refs):
        ins, out_ref = refs[:n_arr], refs[n_arr]
        rbuf, own, send_sems, recv_sems, own_sems = refs[n_arr + 1:]
        q = pl.program_id(0)
        x, y, c = lax.axis_index("x"), lax.axis_index("y"), lax.axis_index("c")

        def block(e, chip, core):
            ai, first, n = entries[e]
            return ins[ai].at[pl.ds(first + (2 * chip + core) * n, n), :]

        def to_sibling(e, chip):
            return pltpu.make_async_remote_copy(
                src_ref=block(e, chip, 1 - c), dst_ref=rbuf.at[chip, pl.ds(offsets[e], entries[e][2]), :],
                send_sem=send_sems.at[e, chip], recv_sem=recv_sems.at[e, chip], device_id=(x, y, 1 - c),
                device_id_type=MESH)

        def own_block(e, chip):
            return pltpu.make_async_copy(block(e, chip, c), own.at[chip, pl.ds(offsets[e], entries[e][2]), :],
                                         own_sems.at[e, chip])

        @pl.when(q == 0)
        def _():
            barrier = pltpu.get_barrier_semaphore()
            pl.semaphore_signal(barrier, inc=1, device_id=(x, y, 1 - c), device_id_type=MESH)
            pl.semaphore_wait(barrier, 1)
            for chip in range(N_CHIP):
                for e in range(n_ent):
                    to_sibling(e, chip).start()
            for chip in range(N_CHIP):
                for e in range(n_ent):
                    own_block(e, chip).start()

        for e in range(n_ent):
            own_block(e, q).wait()
            to_sibling(e, q).wait_recv()
        out_ref[...] = (own[q].astype(F32) + rbuf[q].astype(F32)).astype(out_ref.dtype)

        @pl.when(q == N_CHIP - 1)
        def _():
            for chip in range(N_CHIP):
                for e in range(n_ent):
                    to_sibling(e, chip).wait_send()

    return pl.pallas_call(
        body, name=name, grid=(N_CHIP,),
        in_specs=[ANY] * n_arr,
        out_specs=pl.BlockSpec((None, total, cols), lambda q: (q, 0, 0)),
        out_shape=jax.ShapeDtypeStruct((N_CHIP, total, cols), BF16),
        scratch_shapes=[pltpu.VMEM((N_CHIP, total, cols), BF16), pltpu.VMEM((N_CHIP, total, cols), BF16),
                        pltpu.SemaphoreType.DMA((n_ent, N_CHIP)), pltpu.SemaphoreType.DMA((n_ent, N_CHIP)),
                        pltpu.SemaphoreType.DMA((n_ent, N_CHIP))],
        compiler_params=pltpu.CompilerParams(dimension_semantics=("arbitrary",), vmem_limit_bytes=VMEM_LIMIT,
                                             collective_id=COLLECTIVE_IDS[frozenset(["sibling"])]),
    )(*arrays)


class _HostedScatterAll:
    def __init__(self, packed):
        n = packed.shape[0] // N_DEV
        self.n = n
        self.peers = set(PEER_FLIPS)
        self.arrays = [packed]
        self.out_shapes = [jax.ShapeDtypeStruct((N_DEV, n, packed.shape[1]), packed.dtype)]
        self.sem_shapes = [pltpu.SemaphoreType.DMA((N_DEV - 1,)), pltpu.SemaphoreType.DMA((N_DEV - 1,)),
                           pltpu.SemaphoreType.DMA(())]

    def _copies(self, ins, outs, sems, with_arrivals):
        src, land = ins[0], outs[0]
        send_sems, recv_sems, local_sem = sems
        me = _my_index()

        def block(p):
            return src.at[pl.ds(p * self.n, self.n), :]

        local = pltpu.make_async_copy(block(me), land.at[me], local_sem)
        sends, arrivals = [], []
        for k in range(1, N_DEV):
            p = (me + k) % N_DEV
            q = (me + N_DEV - k) % N_DEV
            sends.append(pltpu.make_async_remote_copy(
                src_ref=block(p), dst_ref=land.at[me], send_sem=send_sems.at[k - 1], recv_sem=recv_sems.at[k - 1],
                device_id=(p // 4, (p // 2) % 2, p % 2), device_id_type=MESH))
            if with_arrivals:
                arrivals.append(pltpu.make_async_remote_copy(
                    src_ref=block(me), dst_ref=land.at[q], send_sem=send_sems.at[k - 1], recv_sem=recv_sems.at[k - 1],
                    device_id=(q // 4, (q // 2) % 2, q % 2), device_id_type=MESH))
        return local, sends, arrivals

    def begin(self, ins, outs, sems, n_steps):
        @pl.when(pl.program_id(0) == 0)
        def _():
            local, sends, _ = self._copies(ins, outs, sems, with_arrivals=False)
            local.start()
            for cp in sends:
                cp.start()

    def end(self, ins, outs, sems, n_steps):
        @pl.when(pl.program_id(0) == n_steps - 1)
        def _():
            local, sends, arrivals = self._copies(ins, outs, sems, with_arrivals=True)
            for cp in arrivals:
                cp.wait_recv()
            for cp in sends:
                cp.wait_send()
            local.wait()


def _tril_weights(ws_ref):
    r = lax.broadcasted_iota(jnp.int32, (CHUNK, CHUNK), 0)
    c = lax.broadcasted_iota(jnp.int32, (CHUNK, CHUNK), 1)
    return [jnp.where(r >= c, ws_ref[h], 0.0).astype(BF16) for h in range(HEADS)]


def _sgu_stats(zpre, gain, bias):
    e = zpre.shape[1] // 2
    z, dz = _gelu_and_grad(zpre)
    u, v = z[:, :e], z[:, e:]
    vc = v - _row_mean(v)
    rstd = lax.rsqrt(_row_mean(vc * vc) + NORM_EPS)
    vhat = vc * rstd
    return u, vhat, rstd, vhat * gain + bias, dz


def _spatial_fwd(wt, vn_bf, bfull_ref, sv_ref, tm):
    for ci in range(tm // CHUNK):
        rows = slice(ci * CHUNK, (ci + 1) * CHUNK)
        for h in range(HEADS):
            cols = slice(h * GROUP, (h + 1) * GROUP)
            sv_ref[rows, cols] = _nn(wt[h], vn_bf[rows, cols]) + bfull_ref[:, cols]


def _mixer_a_fwd(x, g, gath, gain, bias, ws, bfull, tm, hosted=()):
    t_tok, d = x.shape
    e = gain.shape[1]
    e2 = 2 * e
    n_in, n_out = e2 // N_DEV, e // N_DEV

    def body(x_ref, g_ref, gain_ref, bias_ref, ws_ref, bfull_ref, gath_ref,
             xo_ref, gd_ref, u_ref, vhat_ref, svo_ref, y_ref, rstd_ref, win_v, wout_v, sv_v, sems):
        _load_group([(gath_ref, 0, n_in, win_v), (gath_ref, n_in, n_out, wout_v)], sems)
        xv = x_ref[...]
        h = _rms_fwd(xv, g_ref[...])[0].astype(BF16)
        zpre = _nt(h, win_v[...])
        u, vhat, rstd, vn, gelu_d = _sgu_stats(zpre, gain_ref[...], bias_ref[...])
        gd_ref[...] = gelu_d.astype(BF16)
        u_ref[...] = u.astype(BF16)
        vhat_ref[...] = vhat.astype(BF16)
        rstd_ref[...] = rstd
        _spatial_fwd(_tril_weights(ws_ref), vn.astype(BF16), bfull_ref, sv_v, tm)
        sv = sv_v[...]
        svo_ref[...] = sv.astype(BF16)
        y = (u * sv).astype(BF16)
        y_ref[...] = y
        xo_ref[...] = xv + _nn(y, wout_v[...])

    return _hosting_call(
        body, "mixer_a_fwd", t_tok // tm, [x, g, gain, bias, ws, bfull, gath],
        in_specs=[_row_spec(tm, d), _const_spec((1, d)), _const_spec((1, e)), _const_spec((1, e)),
                  _const_spec((HEADS, CHUNK, CHUNK)), _const_spec((CHUNK, e)), ANY],
        out_specs=[_row_spec(tm, d), _row_spec(tm, e2), _row_spec(tm, e), _row_spec(tm, e), _row_spec(tm, e),
                   _row_spec(tm, e), _row_spec(tm, 1)],
        out_shape=[jax.ShapeDtypeStruct((t_tok, d), F32), jax.ShapeDtypeStruct((t_tok, e2), BF16),
                   jax.ShapeDtypeStruct((t_tok, e), BF16), jax.ShapeDtypeStruct((t_tok, e), BF16),
                   jax.ShapeDtypeStruct((t_tok, e), BF16), jax.ShapeDtypeStruct((t_tok, e), BF16),
                   jax.ShapeDtypeStruct((t_tok, 1), F32)],
        scratch=[pltpu.VMEM((e2, d), BF16), pltpu.VMEM((e, d), BF16), pltpu.VMEM((tm, e), F32),
                 pltpu.SemaphoreType.DMA((2 * N_DEV,))],
        hosted=hosted)


def _mixer_a_bwd(dout, x, gd, u_sav, vhat_sav, sv_sav, rstd_sav, g, gath, gain, bias, ws, tm, hosted=()):
    t_tok, d = x.shape
    e = gain.shape[1]
    e2 = 2 * e
    n_in, n_out = e2 // N_DEV, e // N_DEV
    n_steps = t_tok // tm

    def body(dout_ref, x_ref, gd_ref, u_ref, vhat_ref, sv_ref, rstd_ref, g_ref, gain_ref, bias_ref, ws_ref, gath_ref,
             dx_ref, h_ref, dz_ref, dg_ref, dgain_ref, dbias_ref, dws_ref, dbso_ref,
             win_v, wout_v, dvn_v, dbs_ref, sems):
        i = pl.program_id(0)
        _load_group([(gath_ref, 0, n_in, win_v), (gath_ref, n_in, n_out, wout_v)], sems)

        @pl.when(i == 0)
        def _():
            dg_ref[...] = jnp.zeros_like(dg_ref)
            dgain_ref[...] = jnp.zeros_like(dgain_ref)
            dbias_ref[...] = jnp.zeros_like(dbias_ref)
            dws_ref[...] = jnp.zeros_like(dws_ref)
            dbs_ref[...] = jnp.zeros_like(dbs_ref)

        xv = x_ref[...]
        gv = g_ref[...]
        hv, xhat, r = _rms_fwd(xv, gv)
        h_ref[...] = hv.astype(BF16)
        gain_v = gain_ref[...]
        vhat = vhat_ref[...].astype(F32)
        vn_bf = (vhat * gain_v + bias_ref[...]).astype(BF16)
        wt = _tril_weights(ws_ref)

        dov = dout_ref[...]
        dy = _nt(dov.astype(BF16), wout_v[...])
        du = dy * sv_ref[...].astype(F32)
        dsv = dy * u_ref[...].astype(F32)
        dsv_bf = dsv.astype(BF16)
        for ci in range(tm // CHUNK):
            rows = slice(ci * CHUNK, (ci + 1) * CHUNK)
            dbs_ref[...] += dsv[rows, :]
            for h in range(HEADS):
                cols = slice(h * GROUP, (h + 1) * GROUP)
                dvn_v[rows, cols] = _tn(wt[h], dsv_bf[rows, cols])
                dws_ref[h] += _nt(dsv_bf[rows, cols], vn_bf[rows, cols])
        dvn = dvn_v[...]
        dgain_ref[...] += _col_sum(dvn * vhat)
        dbias_ref[...] += _col_sum(dvn)
        dvhat = dvn * gain_v
        dv = rstd_ref[...] * (dvhat - _row_mean(dvhat) - vhat * _row_mean(dvhat * vhat))
        dzpre = (jnp.concatenate([du, dv], axis=1) * gd_ref[...].astype(F32)).astype(BF16)
        dz_ref[...] = dzpre
        dh = _nn(dzpre, win_v[...])
        dxr, dg_row = _rms_bwd(dh, xhat, r, gv)
        dg_ref[...] += dg_row
        dx_ref[...] = dov + dxr

        @pl.when(i == n_steps - 1)
        def _():
            rr = lax.broadcasted_iota(jnp.int32, (CHUNK, CHUNK), 0)
            cc = lax.broadcasted_iota(jnp.int32, (CHUNK, CHUNK), 1)
            for h in range(HEADS):
                dws_ref[h] = jnp.where(rr >= cc, dws_ref[h], 0.0)
                dbso_ref[h] = jnp.sum(dbs_ref[:, h * GROUP:(h + 1) * GROUP], axis=1, keepdims=True)

    return _hosting_call(
        body, "mixer_a_bwd", n_steps, [dout, x, gd, u_sav, vhat_sav, sv_sav, rstd_sav, g, gain, bias, ws, gath],
        in_specs=[_row_spec(tm, d), _row_spec(tm, d), _row_spec(tm, e2), _row_spec(tm, e), _row_spec(tm, e),
                  _row_spec(tm, e), _row_spec(tm, 1), _const_spec((1, d)),
                  _const_spec((1, e)), _const_spec((1, e)), _const_spec((HEADS, CHUNK, CHUNK)), ANY],
        out_specs=[_row_spec(tm, d), _row_spec(tm, d), _row_spec(tm, e2),
                   _const_spec((1, d)), _const_spec((1, e)), _const_spec((1, e)),
                   _const_spec((HEADS, CHUNK, CHUNK)), _const_spec((HEADS, CHUNK, 1))],
        out_shape=[jax.ShapeDtypeStruct((t_tok, d), F32),
                   jax.ShapeDtypeStruct((t_tok, d), BF16), jax.ShapeDtypeStruct((t_tok, e2), BF16),
                   jax.ShapeDtypeStruct((1, d), F32), jax.ShapeDtypeStruct((1, e), F32),
                   jax.ShapeDtypeStruct((1, e), F32), jax.ShapeDtypeStruct((HEADS, CHUNK, CHUNK), F32),
                   jax.ShapeDtypeStruct((HEADS, CHUNK, 1), F32)],
        scratch=[pltpu.VMEM((e2, d), BF16), pltpu.VMEM((e, d), BF16), pltpu.VMEM((tm, e), F32),
                 pltpu.VMEM((CHUNK, e), F32), pltpu.SemaphoreType.DMA((2 * N_DEV,))],
        hosted=hosted)


def _ffn_fwd(x, g, srcs, nf, tm, name, hosted=(), head=None):
    t_tok, d = x.shape
    f = nf * N_DEV
    firsts = [first for _, first in srcs]
    n_head = 2 if head else 0

    def body(*refs):
        x_ref, g_ref, sg_ref, su_ref, sd_ref = refs[:5]
        gate_ref, up_ref, wg_v, wu_v, wd_v, sems = refs[-6:]
        _load_group(
            [(sg_ref, firsts[0], nf, wg_v), (su_ref, firsts[1], nf, wu_v), (sd_ref, firsts[2], nf, wd_v)], sems)
        if head:
            t_ref, gf_ref, loss_ref, dx_ref, dxb_ref, dgf_ref = refs[5:11]

            @pl.when(pl.program_id(0) == 0)
            def _():
                loss_ref[...] = jnp.zeros_like(loss_ref)
                dgf_ref[...] = jnp.zeros_like(dgf_ref)

        xv = x_ref[...]
        h = _rms_fwd(xv, g_ref[...])[0].astype(BF16)
        gate = _nt(h, wg_v[...])
        up = _nt(h, wu_v[...])
        gate_ref[...] = gate.astype(BF16)
        up_ref[...] = up.astype(BF16)
        act = (gate * _sigmoid(gate) * up).astype(BF16)
        xo = xv + _nn(act, wd_v[...])
        if head:
            gfv = gf_ref[...]
            y, xhat, r = _rms_fwd(xo, gfv)
            err = y - t_ref[...]
            loss_ref[...] += 0.5 * jnp.sum(_row_mean(err * err), axis=0, keepdims=True)
            dxr, dg_row = _rms_bwd(err * (1.0 / d), xhat, r, gfv)
            dgf_ref[...] += dg_row
            dx_ref[...] = dxr
            dxb_ref[...] = dxr.astype(BF16)
        else:
            refs[5][...] = xo

    act_specs = [_row_spec(tm, f), _row_spec(tm, f)]
    act_shapes = [jax.ShapeDtypeStruct((t_tok, f), BF16), jax.ShapeDtypeStruct((t_tok, f), BF16)]
    if head:
        out_specs = [_const_spec((1, 1)), _row_spec(tm, d), _row_spec(tm, d), _const_spec((1, d))]
        out_shape = [jax.ShapeDtypeStruct((1, 1), F32), jax.ShapeDtypeStruct((t_tok, d), F32),
                     jax.ShapeDtypeStruct((t_tok, d), BF16), jax.ShapeDtypeStruct((1, d), F32)]
    else:
        out_specs = [_row_spec(tm, d)]
        out_shape = [jax.ShapeDtypeStruct((t_tok, d), F32)]
    return _hosting_call(
        body, name, t_tok // tm, [x, g] + [arr for arr, _ in srcs] + list(head or ()),
        in_specs=[_row_spec(tm, d), _const_spec((1, d)), ANY, ANY, ANY] + [_row_spec(tm, d), _const_spec((1, d))][:n_head],
        out_specs=out_specs + act_specs, out_shape=out_shape + act_shapes,
        scratch=[pltpu.VMEM((f, d), BF16), pltpu.VMEM((f, d), BF16), pltpu.VMEM((f, d), BF16),
                 pltpu.SemaphoreType.DMA((3 * N_DEV,))],
        hosted=hosted)


def _ffn_bwd(dout, x, gate, up, g, srcs, nf, tm, name, hosted=()):
    t_tok, d = x.shape
    f = nf * N_DEV
    firsts = [first for _, first in srcs]
    per_chunk = -(-f // (FFN_CHUNKS * MXU_WIDTH)) * MXU_WIDTH
    bounds = [min(ck * per_chunk, f) for ck in range(FFN_CHUNKS + 1)]

    def body(dout_ref, x_ref, gate_ref, up_ref, g_ref, sg_ref, su_ref, sd_ref,
             dx_ref, dxb_ref, h_ref, act_ref, dgu_ref, dg_ref, wg_v, wu_v, wd_v, sems):
        _load_group(
            [(sg_ref, firsts[0], nf, wg_v), (su_ref, firsts[1], nf, wu_v), (sd_ref, firsts[2], nf, wd_v)], sems)

        @pl.when(pl.program_id(0) == 0)
        def _():
            dg_ref[...] = jnp.zeros_like(dg_ref)

        xv = x_ref[...]
        gv = g_ref[...]
        hv, xhat, r = _rms_fwd(xv, gv)
        h_ref[...] = hv.astype(BF16)
        dov = dout_ref[...]
        dob = dov.astype(BF16)
        dh = None
        for ck in range(FFN_CHUNKS):
            cols = slice(bounds[ck], bounds[ck + 1])
            gate_v = gate_ref[:, cols].astype(F32)
            up_v = up_ref[:, cols].astype(F32)
            sig = _sigmoid(gate_v)
            silu = gate_v * sig
            act_ref[:, cols] = (silu * up_v).astype(BF16)
            dact = _nt(dob, wd_v[cols, :])
            dup = (dact * silu).astype(BF16)
            dgate = (dact * up_v * (sig * (1.0 + gate_v * (1.0 - sig)))).astype(BF16)
            dgu_ref[:, cols] = dgate
            dgu_ref[:, f + bounds[ck]:f + bounds[ck + 1]] = dup
            part = _nn(dgate, wg_v[cols, :]) + _nn(dup, wu_v[cols, :])
            dh = part if dh is None else dh + part
        dxr, dg_row = _rms_bwd(dh, xhat, r, gv)
        dg_ref[...] += dg_row
        dx = dov + dxr
        dx_ref[...] = dx
        dxb_ref[...] = dx.astype(BF16)

    return _hosting_call(
        body, name, t_tok // tm, [dout, x, gate, up, g] + [arr for arr, _ in srcs],
        in_specs=[_row_spec(tm, d), _row_spec(tm, d), _row_spec(tm, f), _row_spec(tm, f), _const_spec((1, d)),
                  ANY, ANY, ANY],
        out_specs=[_row_spec(tm, d), _row_spec(tm, d), _row_spec(tm, d), _row_spec(tm, f), _row_spec(tm, 2 * f),
                   _const_spec((1, d))],
        out_shape=[jax.ShapeDtypeStruct((t_tok, d), F32), jax.ShapeDtypeStruct((t_tok, d), BF16),
                   jax.ShapeDtypeStruct((t_tok, d), BF16), jax.ShapeDtypeStruct((t_tok, f), BF16),
                   jax.ShapeDtypeStruct((t_tok, 2 * f), BF16), jax.ShapeDtypeStruct((1, d), F32)],
        scratch=[pltpu.VMEM((f, d), BF16), pltpu.VMEM((f, d), BF16), pltpu.VMEM((f, d), BF16),
                 pltpu.SemaphoreType.DMA((3 * N_DEV,))],
        hosted=hosted)


def _shift_down(z, k, prev_rows):
    row = lax.broadcasted_iota(jnp.int32, z.shape, 0)
    out = pltpu.roll(z, k, 0)
    for j in range(k):
        out = jnp.where(row == j, prev_rows[j], out)
    return out


def _shift_up(z, k, next_rows):
    tm = z.shape[0]
    row = lax.broadcasted_iota(jnp.int32, z.shape, 0)
    out = pltpu.roll(z, tm - k, 0)
    for j in range(k):
        out = jnp.where(row == tm - k + j, next_rows[j], out)
    return out


def _mixer_b_fwd(x, g, gath, conv_w, tm, seq, hosted=()):
    t_tok, d = x.shape
    e = conv_w.shape[1]
    e3 = 3 * e
    n_in, n_out = e3 // N_DEV, e // N_DEV
    tiles_per_seq = seq // tm

    def body(x_ref, g_ref, cw_ref, gath_ref, xo_ref, p_ref, win_v, wout_v, tail_v, sems):
        i = pl.program_id(0)
        _load_group([(gath_ref, 0, n_in, win_v), (gath_ref, n_in, n_out, wout_v)], sems)

        @pl.when(i % tiles_per_seq == 0)
        def _():
            tail_v[...] = jnp.zeros_like(tail_v)

        xv = x_ref[...]
        h = _rms_fwd(xv, g_ref[...])[0].astype(BF16)
        p = _nt(h, win_v[...])
        p_ref[...] = p.astype(BF16)
        z = p[:, e:2 * e] * p[:, 2 * e:]
        prev = [tail_v[SUBLANES - 2:SUBLANES - 1, :], tail_v[SUBLANES - 1:SUBLANES, :]]
        conv = (cw_ref[2:3, :] * z + cw_ref[1:2, :] * _shift_down(z, 1, prev[1:])
                + cw_ref[0:1, :] * _shift_down(z, 2, prev))
        tail_v[...] = z[tm - SUBLANES:, :]
        y = (p[:, :e] * conv).astype(BF16)
        xo_ref[...] = xv + _nn(y, wout_v[...])

    return _hosting_call(
        body, "mixer_b_fwd", t_tok // tm, [x, g, conv_w, gath],
        in_specs=[_row_spec(tm, d), _const_spec((1, d)), _const_spec((SUBLANES, e)), ANY],
        out_specs=[_row_spec(tm, d), _row_spec(tm, e3)],
        out_shape=[jax.ShapeDtypeStruct((t_tok, d), F32), jax.ShapeDtypeStruct((t_tok, e3), BF16)],
        scratch=[pltpu.VMEM((e3, d), BF16), pltpu.VMEM((e, d), BF16), pltpu.VMEM((SUBLANES, e), F32),
                 pltpu.SemaphoreType.DMA((2 * N_DEV,))],
        hosted=hosted)


def _mixer_b_bwd(dout, x, p, g, gath, conv_w, tm, seq, hosted=()):
    t_tok, d = x.shape
    e = conv_w.shape[1]
    e3 = 3 * e
    n_in, n_out = e3 // N_DEV, e // N_DEV
    tiles_per_seq = seq // tm
    halo_per_tile = tm // HALO
    n_halo = t_tok // HALO

    def body(dout_ref, dnext_ref, x_ref, p_ref, pprev_ref, pnext_ref, g_ref, cw_ref, gath_ref,
             dx_ref, dxb_ref, h_ref, y_ref, dp_ref, dg_ref, dcw_ref, win_v, wout_v, sems):
        i = pl.program_id(0)
        _load_group([(gath_ref, 0, n_in, win_v), (gath_ref, n_in, n_out, wout_v)], sems)

        @pl.when(i == 0)
        def _():
            dg_ref[...] = jnp.zeros_like(dg_ref)
            dcw_ref[...] = jnp.zeros_like(dcw_ref)

        first = (i % tiles_per_seq == 0).astype(F32)
        last = (i % tiles_per_seq == tiles_per_seq - 1).astype(F32)
        xv = x_ref[...]
        gv = g_ref[...]
        hv, xhat, r = _rms_fwd(xv, gv)
        h_ref[...] = hv.astype(BF16)
        pv = p_ref[...].astype(F32)
        bg, cg, hx = pv[:, :e], pv[:, e:2 * e], pv[:, 2 * e:]
        z = cg * hx
        pprev = pprev_ref[...].astype(F32)
        zprev = pprev[:, e:2 * e] * pprev[:, 2 * e:] * (1.0 - first)
        prev = [zprev[HALO - 2:HALO - 1, :], zprev[HALO - 1:HALO, :]]
        zs1 = _shift_down(z, 1, prev[1:])
        zs2 = _shift_down(z, 2, prev)
        w0, w1, w2 = cw_ref[0:1, :], cw_ref[1:2, :], cw_ref[2:3, :]
        conv = w2 * z + w1 * zs1 + w0 * zs2
        y_ref[...] = (bg * conv).astype(BF16)

        dov = dout_ref[...]
        wout_bf = wout_v[...]
        dy = _nt(dov.astype(BF16), wout_bf)
        dconv = dy * bg
        dnext = _nt(dnext_ref[...].astype(BF16), wout_bf) * pnext_ref[:, :e].astype(F32) * (1.0 - last)
        nxt = [dnext[0:1, :], dnext[1:2, :]]
        dz = w2 * dconv + w1 * _shift_up(dconv, 1, nxt[:1]) + w0 * _shift_up(dconv, 2, nxt)
        dcw_ref[0:1, :] += _col_sum(dconv * zs2)
        dcw_ref[1:2, :] += _col_sum(dconv * zs1)
        dcw_ref[2:3, :] += _col_sum(dconv * z)
        dp = jnp.concatenate([dy * conv, dz * hx, dz * cg], axis=1).astype(BF16)
        dp_ref[...] = dp
        dh = _nn(dp, win_v[...])
        dxr, dg_row = _rms_bwd(dh, xhat, r, gv)
        dg_ref[...] += dg_row
        dx = dov + dxr
        dx_ref[...] = dx
        dxb_ref[...] = dx.astype(BF16)

    prev_spec = lambda w: pl.BlockSpec((HALO, w), lambda i: (jnp.maximum(i * halo_per_tile - 1, 0), 0))
    next_spec = lambda w: pl.BlockSpec((HALO, w), lambda i: (jnp.minimum((i + 1) * halo_per_tile, n_halo - 1), 0))
    return _hosting_call(
        body, "mixer_b_bwd", t_tok // tm, [dout, dout, x, p, p, p, g, conv_w, gath],
        in_specs=[_row_spec(tm, d), next_spec(d), _row_spec(tm, d), _row_spec(tm, e3), prev_spec(e3), next_spec(e3),
                  _const_spec((1, d)), _const_spec((SUBLANES, e)), ANY],
        out_specs=[_row_spec(tm, d), _row_spec(tm, d), _row_spec(tm, d), _row_spec(tm, e), _row_spec(tm, e3),
                   _const_spec((1, d)), _const_spec((SUBLANES, e))],
        out_shape=[jax.ShapeDtypeStruct((t_tok, d), F32), jax.ShapeDtypeStruct((t_tok, d), BF16),
                   jax.ShapeDtypeStruct((t_tok, d), BF16), jax.ShapeDtypeStruct((t_tok, e), BF16),
                   jax.ShapeDtypeStruct((t_tok, e3), BF16), jax.ShapeDtypeStruct((1, d), F32),
                   jax.ShapeDtypeStruct((SUBLANES, e), F32)],
        scratch=[pltpu.VMEM((e3, d), BF16), pltpu.VMEM((e, d), BF16), pltpu.SemaphoreType.DMA((2 * N_DEV,))],
        hosted=hosted)


def _wgrad(a, b, bm, name, hosted=(), part=(1, 0)):
    t_tok, m = a.shape
    n = b.shape[1]
    every, first = part

    def body(a_ref, b_ref, o_ref):
        o_ref[...] = _tn(a_ref[...], b_ref[...]).astype(o_ref.dtype)

    outs, h_outs = _hosting_call(
        body, name, m // (bm * every), [a, b],
        in_specs=[pl.BlockSpec((t_tok, bm), lambda i: (0, every * i + first)), _const_spec((t_tok, n))],
        out_specs=[pl.BlockSpec((bm, n), lambda i: (i, 0))],
        out_shape=[jax.ShapeDtypeStruct((m // every, n), BF16)],
        scratch=[], hosted=hosted)
    return (outs[0], h_outs) if hosted else outs[0]


def _sum_slots(land, rb, name):
    n_slots, rows, cols = land.shape

    def body(l_ref, o_ref):
        acc = l_ref[0].astype(F32)
        for k in range(1, n_slots):
            acc = acc + l_ref[k].astype(F32)
        o_ref[...] = acc

    return pl.pallas_call(
        body, name=name, grid=(rows // rb,),
        in_specs=[pl.BlockSpec((n_slots, rb, cols), lambda i: (0, i, 0))],
        out_specs=pl.BlockSpec((rb, cols), lambda i: (i, 0)),
        out_shape=jax.ShapeDtypeStruct((rows, cols), F32),
        compiler_params=_params(sequential=False),
    )(land)


def _adamw(w, grad, m, v, rb, name):
    rows, cols = w.shape
    c1 = 1.0 / (1.0 - ADAM_B1 ** ADAM_STEP)
    c2 = 1.0 / (1.0 - ADAM_B2 ** ADAM_STEP)

    def body(w_ref, g_ref, m_ref, v_ref, d_ref, mo_ref, vo_ref):
        gv = g_ref[...]
        mn = ADAM_B1 * m_ref[...] + (1.0 - ADAM_B1) * gv
        vn = ADAM_B2 * v_ref[...] + (1.0 - ADAM_B2) * (gv * gv)
        mo_ref[...] = mn
        vo_ref[...] = vn
        d_ref[...] = -ADAM_LR * ((mn * c1) / (jnp.sqrt(vn * c2) + ADAM_EPS) + ADAM_WD * w_ref[...])

    spec = pl.BlockSpec((rb, cols), lambda i: (i, 0))
    shape = jax.ShapeDtypeStruct((rows, cols), F32)
    return pl.pallas_call(
        body, name=name, grid=(rows // rb,),
        in_specs=[spec] * 4, out_specs=[spec] * 3, out_shape=[shape] * 3,
        compiler_params=_params(sequential=False),
    )(w, grad, m, v)


def _pack_shards(groups, name, hosted=()):
    flat = [(part, layer) for group in groups for part, layer, _ in group]
    rows = [[p.shape[2] if turn else p.shape[1] for p, _, turn in group] for group in groups]
    first, _, first_turn = groups[0][0]
    width = first.shape[1] if first_turn else first.shape[2]

    def body(*refs):
        ins, outs = refs[:len(flat)], refs[len(flat):]
        k = 0
        for gi, group in enumerate(groups):
            off = 0
            for (_, _, turn), n in zip(group, rows[gi]):
                part = ins[k][...].astype(BF16)
                if turn:
                    r = lax.broadcasted_iota(jnp.int32, (n, n), 0)
                    c = lax.broadcasted_iota(jnp.int32, (n, n), 1)
                    part = _nt((r == c).astype(BF16), part).astype(BF16)
                outs[gi][off:off + n, :] = part
                off += n
                k += 1

    return _hosting_call(
        body, name, 1, [p for p, _ in flat],
        in_specs=[pl.BlockSpec((None,) + p.shape[1:], lambda i, layer=layer: (layer, 0, 0)) for p, layer in flat],
        out_specs=[_const_spec((sum(r), width)) for r in rows],
        out_shape=[jax.ShapeDtypeStruct((sum(r), width), BF16) for r in rows],
        scratch=[], hosted=hosted)


def _split_bf16(a):
    hi = a.astype(BF16)
    rest = a - hi.astype(F32)
    mid = rest.astype(BF16)
    return hi, mid, (rest - mid.astype(F32)).astype(BF16)


def _reduce_adamw(lands, w, m, v, transpose, name, hosted=()):
    n_layers, rows_w, cols_w = w.shape
    c1 = 1.0 / (1.0 - ADAM_B1 ** ADAM_STEP)
    c2 = 1.0 / (1.0 - ADAM_B2 ** ADAM_STEP)
    flat = [piece for pieces in lands for piece in pieces]
    counts = [len(pieces) for pieces in lands]
    if transpose:
        tiles = rows_w // MXU_WIDTH
        blk = (MXU_WIDTH, cols_w)
        land_specs = [pl.BlockSpec((N_CHIP, n, MXU_WIDTH), lambda i, b=first // n: (0, b, i % tiles))
                      for _, first, n in flat]
        for _, first, n in flat:
            assert first % n == 0
    else:
        tiles = 1
        blk = (rows_w // tiles, cols_w)
        assert all(c == 1 for c in counts)
        land_specs = [pl.BlockSpec((N_CHIP,) + blk, lambda i, b=first // blk[0]: (0, b + i % tiles, 0))
                      for _, first, _ in flat]
        for _, first, _ in flat:
            assert first % blk[0] == 0

    def body(*refs):
        land_refs = refs[:len(flat)]
        w_ref, m_ref, v_ref, g_ref, d_ref, mo_ref, vo_ref = refs[len(flat):]
        layer = pl.program_id(0) // tiles

        def total(ref):
            acc = ref[0].astype(F32)
            for q in range(1, N_CHIP):
                acc = acc + ref[q].astype(F32)
            return acc

        def layer_sum(k):
            first = sum(counts[:k])
            parts = [total(land_refs[first + j]) for j in range(counts[k])]
            return parts[0] if len(parts) == 1 else jnp.concatenate(parts, axis=0)

        gv = layer_sum(0)
        for k in range(1, n_layers):
            gv = jnp.where(layer == k, layer_sum(k), gv)
        if transpose:
            r = lax.broadcasted_iota(jnp.int32, (MXU_WIDTH, MXU_WIDTH), 0)
            c = lax.broadcasted_iota(jnp.int32, (MXU_WIDTH, MXU_WIDTH), 1)
            eye = (r == c).astype(BF16)
            hi, mid, lo = _split_bf16(gv)
            gv = _nt(eye, hi) + _nt(eye, mid) + _nt(eye, lo)
        g_ref[...] = gv
        mn = ADAM_B1 * m_ref[...] + (1.0 - ADAM_B1) * gv
        vn = ADAM_B2 * v_ref[...] + (1.0 - ADAM_B2) * (gv * gv)
        mo_ref[...] = mn
        vo_ref[...] = vn
        d_ref[...] = -ADAM_LR * ((mn * c1) / (jnp.sqrt(vn * c2) + ADAM_EPS) + ADAM_WD * w_ref[...])

    spec = pl.BlockSpec((None,) + blk, lambda i: (i // tiles, i % tiles, 0))
    shape = jax.ShapeDtypeStruct(w.shape, F32)
    outs, h_outs = _hosting_call(
        body, name, n_layers * tiles, [land for land, _, _ in flat] + [w, m, v],
        in_specs=land_specs + [spec] * 3, out_specs=[spec] * 4, out_shape=[shape] * 4, scratch=[], hosted=hosted)
    return (outs, h_outs) if hosted else outs


def _pack_small(parts, rows):
    flat = jnp.concatenate([p.reshape(-1).astype(F32) for p in parts])
    return jnp.pad(flat, (0, rows * LANES - flat.shape[0])).reshape(rows, LANES)


def _unpack_small(packed, shapes):
    flat = packed.reshape(-1)
    out = []
    pos = 0
    for s in shapes:
        n = math.prod(s)
        out.append(flat[pos:pos + n].reshape(s))
        pos += n
    return out


def kernel(x, mix_norm, ffn_norm, a_w_in, a_v_gain, a_v_bias, a_w_s, a_b_s, a_w_out, b_w_in, b_conv_w, b_w_out, ffn_w_gate, ffn_w_up, ffn_w_down, final_norm, loss_target, m_mix_norm, m_ffn_norm, m_a_w_in, m_a_v_gain, m_a_v_bias, m_a_w_s, m_a_b_s, m_a_w_out, m_b_w_in, m_b_conv_w, m_b_w_out, m_ffn_w_gate, m_ffn_w_up, m_ffn_w_down, m_final_norm, v_mix_norm, v_ffn_norm, v_a_w_in, v_a_v_gain, v_a_v_bias, v_a_w_s, v_a_b_s, v_a_w_out, v_b_w_in, v_b_conv_w, v_b_w_out, v_ffn_w_gate, v_ffn_w_up, v_ffn_w_down, v_final_norm):
    bsz, seq, d = x.shape
    t_tok = bsz * seq
    me = _my_index()
    xt = x.reshape(t_tok, d)
    target = loss_target.reshape(t_tok, d)
    e_a = a_v_gain.shape[1]
    e_b = b_w_out.shape[1] * N_DEV
    n_layers = ffn_w_gate.shape[0]
    f_shard = ffn_w_gate.shape[2]
    f_full = f_shard * N_DEV

    conv_pad = jnp.pad(b_conv_w[0], ((0, SUBLANES - CONV_W), (0, 0)))
    sh_a = jnp.concatenate([a_w_in[0].T, a_w_out[0]]).astype(BF16)
    bfull = jnp.repeat(a_b_s[0].T, GROUP, axis=1)

    gate_t, up_t = ffn_w_gate.transpose(0, 2, 1), ffn_w_up.transpose(0, 2, 1)
    (sh_b, sh_f0, sh_f1g, sh_f1ud), (gath_a, conv_g) = _pack_shards(
        [[(b_w_in, 0, True), (b_w_out, 0, False)],
         [(gate_t, 0, False), (up_t, 0, False), (ffn_w_down, 0, False)],
         [(gate_t, 1, False)],
         [(up_t, 1, False), (ffn_w_down, 1, False)]],
        "pack_shards", hosted=[_HostedGathers([sh_a, conv_pad])])
    conv_full = jnp.pad(conv_g[:, :CONV_W, :].transpose(1, 0, 2).reshape(CONV_W, e_b), ((0, SUBLANES - CONV_W), (0, 0)))
    (x1, gd_a, u_a, vhat_a, sv_a, y_a, rstd_a), (gath_f0,) = _mixer_a_fwd(
        xt, mix_norm[0:1], gath_a, a_v_gain, a_v_bias, a_w_s[0], bfull, tm=TOKEN_TILE,
        hosted=[_HostedGathers([sh_f0])])
    srcs0 = [(gath_f0, 0), (gath_f0, f_shard), (gath_f0, 2 * f_shard)]
    (x2, gate0, up0), (gath_b, gath_f1g) = _ffn_fwd(x1, ffn_norm[0:1], srcs0, f_shard, tm=TOKEN_TILE, name="ffn_fwd0",
                                                    hosted=[_HostedGathers([sh_b, sh_f1g])])
    (x3, p_b), (gath_f1ud,) = _mixer_b_fwd(x2, mix_norm[1:2], gath_b, conv_full, tm=TOKEN_TILE, seq=seq,
                                           hosted=[_HostedGathers([sh_f1ud])])
    srcs1 = [(gath_f1g, 0), (gath_f1ud, 0), (gath_f1ud, f_shard)]
    (loss_part, dx4, dx4_bf, d_final, gate1, up1), _ = _ffn_fwd(
        x3, ffn_norm[1:2], srcs1, f_shard, tm=TOKEN_TILE_WIDE, name="ffn_fwd1", head=(target, final_norm.reshape(1, d)))

    ffn_entries = [(0, 0, f_shard), (0, f_full, f_shard), (1, 0, f_shard)]
    (dx3, dx3_bf, h_f1, act1, dgu1, d_fn1), _ = _ffn_bwd(dx4, x3, gate1, up1, ffn_norm[1:2], srcs1, f_shard, tm=TOKEN_TILE,
                                                         name="ffn_bwd1")
    g_down1 = _wgrad(act1, dx4_bf, WGRAD_ROWS, "wgrad_down1")
    g_gu1 = _wgrad(dgu1, h_f1, WGRAD_ROWS_WIDE, "wgrad_gate_up1")
    ps_f1 = _pair_reduce([g_gu1, g_down1], ffn_entries, "pair_reduce_f1")
    (dx2, dx2_bf, h_b, y_b, dp_b, d_mn1, d_conv), (land_f1gu,) = _mixer_b_bwd(
        dx3, x2, p_b, mix_norm[1:2], gath_b, conv_full, tm=TOKEN_TILE_WIDE, seq=seq,
        hosted=[_HostedChipScatter(ps_f1, 0, 2 * f_shard)])
    g_b_out = _wgrad(y_b, dx3_bf, WGRAD_ROWS, "wgrad_b_out")
    g_b_in = _wgrad(dp_b, h_b, WGRAD_ROWS_WIDE, "wgrad_b_in")
    ps_b = _pair_reduce([g_b_in, g_b_out], [(0, 0, b_w_in.shape[2]), (1, 0, b_w_out.shape[1])], "pair_reduce_b")
    (dx1, dx1_bf, h_f0, act0, dgu0, d_fn0), (land_f1d, land_b) = _ffn_bwd(
        dx2, x1, gate0, up0, ffn_norm[0:1], srcs0, f_shard, tm=TOKEN_TILE, name="ffn_bwd0",
        hosted=[_HostedChipScatter(ps_f1, 2 * f_shard, f_shard), _HostedChipScatter(ps_b)])
    g_down0 = _wgrad(act0, dx2_bf, WGRAD_ROWS, "wgrad_down0")
    g_gu0 = _wgrad(dgu0, h_f0, WGRAD_ROWS_WIDE, "wgrad_gate_up0")
    g_a_out = _wgrad(y_a, dx1_bf, WGRAD_ROWS, "wgrad_a_out")
    n_ao = a_w_out.shape[1]
    ps_f0ao = _pair_reduce([g_gu0, g_down0, g_a_out], ffn_entries + [(2, 0, n_ao)], "pair_reduce_f0_a_out")
    (dx0, h_a, dz_a, d_mn0, d_gain, d_bias, d_ws, d_bs_acc), (land_f0, land_ao) = _mixer_a_bwd(
        dx1, xt, gd_a, u_a, vhat_a, sv_a, rstd_a, mix_norm[0:1], gath_a, a_v_gain, a_v_bias, a_w_s[0], tm=TOKEN_TILE,
        hosted=[_HostedChipScatter(ps_f0ao, 0, 3 * f_shard), _HostedChipScatter(ps_f0ao, 3 * f_shard, n_ao)])
    d_bs = d_bs_acc.reshape(HEADS, CHUNK)

    small_grads = [jnp.concatenate([d_mn0, d_mn1]), jnp.concatenate([d_fn0, d_fn1]), d_gain, d_bias, d_ws, d_bs,
                   d_final, d_conv[:CONV_W], loss_part]
    small_shapes = [(n_layers, d), (n_layers, d), (1, e_a), (1, e_a), (1, HEADS, CHUNK, CHUNK), (1, HEADS, CHUNK), (d,),
                    (CONV_W, e_b), ()]
    n_small = sum(math.prod(s) for s in small_shapes)
    blk_rows = -(-n_small // (N_DEV * LANES * SUBLANES)) * SUBLANES
    small_rows = blk_rows * N_DEV
    packed = _pack_small(small_grads, small_rows)
    n_half = a_w_in.shape[2] // 2
    g_ai0, (small_land,) = _wgrad(dz_a, h_a, n_half, "wgrad_a_in0", hosted=[_HostedScatterAll(packed)], part=(2, 0))
    ps_ai0 = _pair_reduce([g_ai0], [(0, 0, n_half)], "pair_reduce_a_in0")
    small_sum = _sum_slots(small_land, blk_rows, "sum_small")
    g_ai1, (land_ai0,) = _wgrad(dz_a, h_a, n_half, "wgrad_a_in1", hosted=[_HostedChipScatter(ps_ai0)], part=(2, 1))
    ps_ai1 = _pair_reduce([g_ai1], [(0, 0, n_half)], "pair_reduce_a_in1")
    land_ai1, small_gath = _exchange([_HostedChipScatter(ps_ai1), _HostedGathers([small_sum])], "tail_exchange")
    small_all = small_gath.reshape(small_rows, LANES)

    n_b_in = b_w_in.shape[2]
    gate_out = _reduce_adamw([[(land_f0, 0, f_shard)], [(land_f1gu, 0, f_shard)]], gate_t,
                             m_ffn_w_gate.transpose(0, 2, 1), v_ffn_w_gate.transpose(0, 2, 1), False, "adamw_gate")
    up_out = _reduce_adamw([[(land_f0, f_shard, f_shard)], [(land_f1gu, f_shard, f_shard)]], up_t,
                           m_ffn_w_up.transpose(0, 2, 1), v_ffn_w_up.transpose(0, 2, 1), False, "adamw_up")
    res = {
        "a_w_in": _reduce_adamw([[(land_ai0, 0, n_half), (land_ai1, 0, n_half)]], a_w_in, m_a_w_in, v_a_w_in, True,
                                "adamw_a_in"),
        "a_w_out": _reduce_adamw([[(land_ao, 0, a_w_out.shape[1])]], a_w_out, m_a_w_out, v_a_w_out, False,
                                 "adamw_a_out"),
        "b_w_in": _reduce_adamw([[(land_b, 0, n_b_in)]], b_w_in, m_b_w_in, v_b_w_in, True, "adamw_b_in"),
        "b_w_out": _reduce_adamw([[(land_b, n_b_in, b_w_out.shape[1])]], b_w_out, m_b_w_out, v_b_w_out, False,
                                 "adamw_b_out"),
        "ffn_w_gate": [o.transpose(0, 2, 1) for o in gate_out],
        "ffn_w_up": [o.transpose(0, 2, 1) for o in up_out],
        "ffn_w_down": _reduce_adamw([[(land_f0, 2 * f_shard, f_shard)], [(land_f1d, 0, f_shard)]], ffn_w_down,
                                    m_ffn_w_down, v_ffn_w_down, False, "adamw_down"),
    }

    (gr_mix, gr_ffn, gr_gain, gr_bias, gr_ws, gr_bs, gr_final, gr_conv_full, loss) = _unpack_small(small_all, small_shapes)
    gr_conv = lax.dynamic_slice_in_dim(gr_conv_full, me * (e_b // N_DEV), e_b // N_DEV, axis=1)[None]

    small_w =[mix_norm, ffn_norm, a_v_gain, a_v_bias, a_w_s, a_b_s, final_norm]
    small_m = [m_mix_norm, m_ffn_norm, m_a_v_gain, m_a_v_bias, m_a_w_s, m_a_b_s, m_final_norm]
    small_v = [v_mix_norm, v_ffn_norm, v_a_v_gain, v_a_v_bias, v_a_w_s, v_a_b_s, v_final_norm]
    small_g = [gr_mix, gr_ffn, gr_gain, gr_bias, gr_ws, gr_bs, gr_final]
    sm_shapes = small_shapes[:len(small_w)]
    sm_out = _adamw(_pack_small(small_w, small_rows), _pack_small(small_g, small_rows), _pack_small(small_m, small_rows),
                    _pack_small(small_v, small_rows), small_rows, "adamw_small")
    sm_delta, sm_m, sm_v = [_unpack_small(o, sm_shapes) for o in sm_out]

    conv_out = _adamw(b_conv_w[0], gr_conv[0], m_b_conv_w[0], v_b_conv_w[0], CONV_W, "adamw_conv")
    conv_delta, conv_m, conv_v = [o[None] for o in conv_out]

    order = ["mix_norm", "ffn_norm", "a_w_in", "a_v_gain", "a_v_bias", "a_w_s", "a_b_s", "a_w_out", "b_w_in",
             "b_conv_w", "b_w_out", "ffn_w_gate", "ffn_w_up", "ffn_w_down", "final_norm"]
    small_names = ["mix_norm", "ffn_norm", "a_v_gain", "a_v_bias", "a_w_s", "a_b_s", "final_norm"]
    grads = {"b_conv_w": gr_conv}
    deltas, new_m, new_v = {}, {}, {}
    for k, name in enumerate(small_names):
        grads[name] = small_g[k]
        deltas[name], new_m[name], new_v[name] = sm_delta[k], sm_m[k], sm_v[k]
    deltas["b_conv_w"], new_m["b_conv_w"], new_v["b_conv_w"] = conv_delta, conv_m, conv_v
    for name, (gg, dl, mm, vv) in res.items():
        grads[name], deltas[name], new_m[name], new_v[name] = gg, dl, mm, vv

    grad_x = dx0.reshape(bsz, seq, d)
    return (loss, grad_x, *[grads[n] for n in order], *[deltas[n] for n in order],
            *[new_m[n] for n in order], *[new_v[n] for n in order])
```

```python
import math

import jax
import jax.numpy as jnp
from jax import lax
from jax.experimental import pallas as pl
from jax.experimental.pallas import tpu as pltpu

F32 = jnp.float32
BF16 = jnp.bfloat16

N_DEV = 8
N_CHIP = 4
CHUNK = 128
HEADS = 16
GROUP = 128
CONV_W = 3
NORM_EPS = 1e-6
GELU_C = math.sqrt(2.0 / math.pi)
GELU_K = 0.044715

ADAM_LR = 0.001
ADAM_B1 = 0.9
ADAM_B2 = 0.999
ADAM_EPS = 1e-08
ADAM_WD = 0.01
ADAM_STEP = 10

LANES = 128
SUBLANES = 8
VMEM_LIMIT = 60 * 1024 * 1024
HALO = 16
MXU_WIDTH = 256
FFN_CHUNKS = 2
TOKEN_TILE = 256
TOKEN_TILE_WIDE = 512
WGRAD_ROWS = 256
WGRAD_ROWS_WIDE = 512
GATHER_RELAY_AT = 0.56
GATHER_FORWARD_LEAD = 2

MESH = pl.DeviceIdType.MESH
ANY = pl.BlockSpec(memory_space=pl.ANY)

PEER_FLIPS = {"sibling": (0, 0, 1), "x": (1, 0, 0), "y": (0, 1, 0), "diagonal": (1, 1, 0),
              "x_other": (1, 0, 1), "y_other": (0, 1, 1), "diagonal_other": (1, 1, 1)}
COLLECTIVE_IDS = {frozenset(["sibling"]): 0,
                  frozenset(["sibling", "x", "y"]): 1,
                  frozenset(["sibling", "x", "y", "diagonal"]): 2,
                  frozenset(["x", "y", "diagonal"]): 3,
                  frozenset(PEER_FLIPS): 4}


def _params(sequential=True):
    return pltpu.CompilerParams(
        dimension_semantics=("arbitrary",) if sequential else None,
        vmem_limit_bytes=VMEM_LIMIT)


def _nn(a, b):
    return jnp.dot(a, b, preferred_element_type=F32)


def _nt(a, b):
    return lax.dot_general(a, b, (((1,), (1,)), ((), ())), preferred_element_type=F32)


def _tn(a, b):
    return lax.dot_general(a, b, (((0,), (0,)), ((), ())), preferred_element_type=F32)


def _row_mean(a):
    return jnp.mean(a, axis=-1, keepdims=True)


def _col_sum(a):
    return jnp.sum(a, axis=0, keepdims=True)


def _rms_fwd(x, g):
    r = lax.rsqrt(_row_mean(x * x) + NORM_EPS)
    xhat = x * r
    return xhat * g, xhat, r


def _rms_bwd(dh, xhat, r, g):
    a = dh * g
    dx = r * (a - xhat * _row_mean(a * xhat))
    return dx, _col_sum(dh * xhat)


def _gelu_and_grad(x):
    x2 = x * x
    t = jnp.tanh(x * (GELU_C + (GELU_C * GELU_K) * x2))
    half = 0.5 * t + 0.5
    d = half + x * (0.5 - 0.5 * (t * t)) * (GELU_C + (3.0 * GELU_C * GELU_K) * x2)
    return x * half, d


def _sigmoid(x):
    return 1.0 / (1.0 + jnp.exp(-x))


def _row_spec(tm, width):
    return pl.BlockSpec((tm, width), lambda i: (i, 0))


def _const_spec(shape):
    nd = len(shape)
    return pl.BlockSpec(shape, lambda i: (0,) * nd)


def _load_group(parts, sems):
    @pl.when(pl.program_id(0) == 0)
    def _():
        copies = []
        for k, (gath_ref, first, n, dst) in enumerate(parts):
            for j in range(N_DEV):
                copies.append(pltpu.make_async_copy(gath_ref.at[j, pl.ds(first, n), :], dst.at[pl.ds(j * n, n), :],
                                                    sems.at[k * N_DEV + j]))
        for cp in copies:
            cp.start()
        for cp in copies:
            cp.wait()


def _hosting_call(body, name, n_steps, arrays, in_specs, out_specs, out_shape, scratch, hosted=()):
    n_in, n_out, n_scr = len(arrays), len(out_shape), len(scratch)
    h_arrays = [a for h in hosted for a in h.arrays]
    h_shapes = [s for h in hosted for s in h.out_shapes]
    h_sems = [s for h in hosted for s in h.sem_shapes]
    peers = sorted(set().union(*[h.peers for h in hosted])) if hosted else []

    def handshake():
        @pl.when(pl.program_id(0) == 0)
        def _():
            x, y, c = lax.axis_index("x"), lax.axis_index("y"), lax.axis_index("c")
            barrier = pltpu.get_barrier_semaphore()
            for p in peers:
                fx, fy, fc = PEER_FLIPS[p]
                peer = (1 - x if fx else x, 1 - y if fy else y, 1 - c if fc else c)
                pl.semaphore_signal(barrier, inc=1, device_id=peer, device_id_type=MESH)
            pl.semaphore_wait(barrier, len(peers))

    def full_body(*refs):
        pos = 0
        groups = []
        for n in (n_in, len(h_arrays), n_out, len(h_shapes), n_scr, len(h_sems)):
            groups.append(refs[pos:pos + n])
            pos += n
        own_in, h_in, own_out, h_out, own_scr, h_sem = groups
        per_host = []
        pi = po = ps = 0
        for h in hosted:
            ni, no, ns = len(h.arrays), len(h.out_shapes), len(h.sem_shapes)
            per_host.append((h, h_in[pi:pi + ni], h_out[po:po + no], h_sem[ps:ps + ns]))
            pi, po, ps = pi + ni, po + no, ps + ns
        if hosted:
            handshake()
        for h, ins, outs, sems in per_host:
            h.begin(ins, outs, sems, n_steps)
        body(*own_in, *own_out, *own_scr)
        for h, ins, outs, sems in per_host:
            h.end(ins, outs, sems, n_steps)

    outs = pl.pallas_call(
        full_body, name=name, grid=(n_steps,),
        in_specs=list(in_specs) + [ANY] * len(h_arrays),
        out_specs=list(out_specs) + [ANY] * len(h_shapes),
        out_shape=list(out_shape) + h_shapes,
        scratch_shapes=list(scratch) + h_sems,
        compiler_params=pltpu.CompilerParams(
            dimension_semantics=("arbitrary",), vmem_limit_bytes=VMEM_LIMIT,
            collective_id=COLLECTIVE_IDS[frozenset(peers)] if hosted else None),
    )(*arrays, *h_arrays)
    return outs[:n_out], outs[n_out:]


def _my_index():
    return 4 * lax.axis_index("x") + 2 * lax.axis_index("y") + lax.axis_index("c")


GATHER_COPIES = 10


def _gather_relays(n_rows, dtype):
    rows_per_tile = SUBLANES * 4 // jnp.dtype(dtype).itemsize
    return n_rows % 2 == 0 and (n_rows // 2) % rows_per_tile == 0


class _Gather:
    def __init__(self, shard, out, send_sems, recv_sems, local_sem):
        self.shard, self.out = shard, out
        self.send_sems, self.recv_sems, self.local_sem = send_sems, recv_sems, local_sem
        x, y, c = lax.axis_index("x"), lax.axis_index("y"), lax.axis_index("c")
        self.c = c
        self.me, self.sibling = (x, y, c), (x, y, 1 - c)
        self.xn, self.yn, self.dg = (1 - x, y), (x, 1 - y), (1 - x, 1 - y)
        self.n = shard.shape[0]
        self.half = self.n // 2
        self.relays = _gather_relays(self.n, shard.dtype)

    def _slot(self, dev, lo=0, hi=None):
        hi = self.n if hi is None else hi
        return self.out.at[4 * dev[0] + 2 * dev[1] + dev[2], pl.ds(lo, hi - lo), :]

    def _copy(self, k, block, to, src=None, lo=0, hi=None):
        return pltpu.make_async_remote_copy(
            src_ref=self._slot(block, lo, hi) if src is None else src, dst_ref=self._slot(block, lo, hi),
            send_sem=self.send_sems.at[k], recv_sem=self.recv_sems.at[k], device_id=to, device_id_type=MESH)

    def _local(self):
        return pltpu.make_async_copy(self.shard, self._slot(self.me), self.local_sem)

    def start(self):
        c = self.c
        self._local().start()
        self._copy(0, self.me, self.sibling, src=self.shard).start()
        if self.relays:
            first, second = self.shard.at[pl.ds(0, self.half), :], self.shard.at[pl.ds(self.half, self.half), :]
            self._copy(1, self.me, (*self.xn, c), src=first, hi=self.half).start()
            self._copy(2, self.me, (*self.yn, c), src=second, lo=self.half).start()
            self._copy(8, self.me, (*self.xn, c), src=second, lo=self.half).start()
            self._copy(9, self.me, (*self.yn, c), src=first, hi=self.half).start()
        else:
            self._copy(1, self.me, (*self.xn, c), src=self.shard).start()
            self._copy(2, self.me, (*self.yn, c), src=self.shard).start()
            self._copy(3, self.me, (*self.dg, c), src=self.shard).start()

    def relay(self):
        c = self.c
        if self.relays:
            self._copy(1, (*self.xn, c), self.me, hi=self.half).wait_recv()
            self._copy(3, (*self.xn, c), (*self.yn, c), hi=self.half).start()
            self._copy(2, (*self.yn, c), self.me, lo=self.half).wait_recv()
            self._copy(4, (*self.yn, c), (*self.xn, c), lo=self.half).start()

    def forward(self):
        c = self.c
        if self.relays:
            self._copy(8, (*self.xn, c), self.me, lo=self.half).wait_recv()
            self._copy(9, (*self.yn, c), self.me, hi=self.half).wait_recv()
            self._copy(5, (*self.xn, c), self.sibling).start()
            self._copy(6, (*self.yn, c), self.sibling).start()
            self._copy(3, (*self.dg, c), self.me, hi=self.half).wait_recv()
            self._copy(4, (*self.dg, c), self.me, lo=self.half).wait_recv()
        else:
            self._copy(1, (*self.xn, c), self.me).wait_recv()
            self._copy(5, (*self.xn, c), self.sibling).start()
            self._copy(2, (*self.yn, c), self.me).wait_recv()
            self._copy(6, (*self.yn, c), self.sibling).start()
            self._copy(3, (*self.dg, c), self.me).wait_recv()
        self._copy(7, (*self.dg, c), self.sibling).start()

    def finish(self):
        c = self.c
        self._copy(0, self.sibling, self.me).wait_recv()
        for k, chip in ((5, self.xn), (6, self.yn), (7, self.dg)):
            self._copy(k, (*chip, 1 - c), self.me).wait_recv()
        for k in (0, 5, 6, 7):
            self._copy(k, self.me, self.sibling).wait_send()
        if self.relays:
            for k in (1, 3, 9):
                self._copy(k, self.me, self.sibling, hi=self.half).wait_send()
            for k in (2, 4, 8):
                self._copy(k, self.me, self.sibling, lo=self.half).wait_send()
        else:
            for k in (1, 2, 3):
                self._copy(k, self.me, self.sibling).wait_send()
        self._local().wait()


class _HostedGathers:
    def __init__(self, shards, mid_lead=GATHER_FORWARD_LEAD, relay_at=GATHER_RELAY_AT):
        n = len(shards)
        self.arrays = shards
        self.mid_lead, self.relay_at = mid_lead, relay_at
        self.peers = {"sibling", "x", "y"}
        if not all(_gather_relays(s.shape[0], s.dtype) for s in shards):
            self.peers.add("diagonal")
        self.out_shapes = [jax.ShapeDtypeStruct((N_DEV,) + s.shape, s.dtype) for s in shards]
        self.sem_shapes = [pltpu.SemaphoreType.DMA((n, GATHER_COPIES)), pltpu.SemaphoreType.DMA((n, GATHER_COPIES)),
                           pltpu.SemaphoreType.DMA((n,))]

    def _gathers(self, ins, outs, sems):
        return [_Gather(ins[a], outs[a], sems[0].at[a], sems[1].at[a], sems[2].at[a]) for a in range(len(ins))]

    def begin(self, ins, outs, sems, n_steps):
        i = pl.program_id(0)
        forward_step = max(n_steps - 1 - self.mid_lead, 0)
        relay_step = min(int(self.relay_at * n_steps), forward_step)

        @pl.when(i == 0)
        def _():
            for g in self._gathers(ins, outs, sems):
                g.start()

        if n_steps == 1:
            return

        @pl.when(i == relay_step)
        def _():
            for g in self._gathers(ins, outs, sems):
                g.relay()

        @pl.when(i == forward_step)
        def _():
            for g in self._gathers(ins, outs, sems):
                g.forward()

    def end(self, ins, outs, sems, n_steps):
        @pl.when(pl.program_id(0) == n_steps - 1)
        def _():
            gathers = self._gathers(ins, outs, sems)
            if n_steps == 1:
                for g in gathers:
                    g.relay()
                for g in gathers:
                    g.forward()
            for g in gathers:
                g.finish()


def _exchange(hosted, name):
    return _hosting_call(lambda: None, name, 1, [], [], [], [], [], hosted=hosted)[1]


class _ChipScatter:
    def __init__(self, pairsum, row0, land, send_sems, recv_sems, local_sem):
        self.pairsum, self.row0, self.land = pairsum, row0, land
        self.send_sems, self.recv_sems, self.local_sem = send_sems, recv_sems, local_sem
        x, y, c = lax.axis_index("x"), lax.axis_index("y"), lax.axis_index("c")
        self.c = c
        self.chip = 2 * x + y
        self.others = [(1 - x, y), (x, 1 - y), (1 - x, 1 - y)]

    def _src(self, chip):
        return self.pairsum.at[chip, pl.ds(self.row0, self.land.shape[1]), :]

    def _copy(self, k):
        ox, oy = self.others[k]
        return pltpu.make_async_remote_copy(
            src_ref=self._src(2 * ox + oy), dst_ref=self.land.at[self.chip],
            send_sem=self.send_sems.at[k], recv_sem=self.recv_sems.at[k], device_id=(ox, oy, self.c),
            device_id_type=MESH)

    def _arrival(self, k):
        ox, oy = self.others[k]
        return pltpu.make_async_remote_copy(
            src_ref=self._src(self.chip), dst_ref=self.land.at[2 * ox + oy],
            send_sem=self.send_sems.at[k], recv_sem=self.recv_sems.at[k], device_id=(ox, oy, self.c),
            device_id_type=MESH)

    def _local(self):
        return pltpu.make_async_copy(self._src(self.chip), self.land.at[self.chip], self.local_sem)

    def start(self):
        self._local().start()
        for k in range(N_CHIP - 1):
            self._copy(k).start()

    def finish(self):
        for k in range(N_CHIP - 1):
            self._arrival(k).wait_recv()
        for k in range(N_CHIP - 1):
            self._copy(k).wait_send()
        self._local().wait()


class _HostedChipScatter:
    def __init__(self, pairsum, row0=0, n=None):
        n = pairsum.shape[1] - row0 if n is None else n
        self.row0 = row0
        self.peers = {"x", "y", "diagonal"}
        self.arrays = [pairsum]
        self.out_shapes = [jax.ShapeDtypeStruct((N_CHIP, n, pairsum.shape[2]), pairsum.dtype)]
        self.sem_shapes = [pltpu.SemaphoreType.DMA((N_CHIP - 1,)), pltpu.SemaphoreType.DMA((N_CHIP - 1,)),
                           pltpu.SemaphoreType.DMA(())]

    def begin(self, ins, outs, sems, n_steps):
        @pl.when(pl.program_id(0) == 0)
        def _():
            _ChipScatter(ins[0], self.row0, outs[0], *sems).start()

    def end(self, ins, outs, sems, n_steps):
        @pl.when(pl.program_id(0) == n_steps - 1)
        def _():
            _ChipScatter(ins[0], self.row0, outs[0], *sems).finish()


def _pair_reduce(arrays, entries, name):
    n_arr, n_ent = len(arrays), len(entries)
    cols = arrays[0].shape[1]
    offsets = []
    total = 0
    for _, _, n in entries:
        offsets.append(total)
        total += n

    def body(*refs):
        ins, out_ref = refs[:n_arr], refs[n_arr]
        rbuf, own, send_sems, recv_sems, own_sems = refs[n_arr + 1:]
        q = pl.program_id(0)
        x, y, c = lax.axis_index("x"), lax.axis_index("y"), lax.axis_index("c")

        def block(e, chip, core):
            ai, first, n = entries[e]
            return ins[ai].at[pl.ds(first + (2 * chip + core) * n, n), :]

        def to_sibling(e, chip):
            return pltpu.make_async_remote_copy(
                src_ref=block(e, chip, 1 - c), dst_ref=rbuf.at[chip, pl.ds(offsets[e], entries[e][2]), :],
                send_sem=send_sems.at[e, chip], recv_sem=recv_sems.at[e, chip], device_id=(x, y, 1 - c),
                device_id_type=MESH)

        def own_block(e, chip):
            return pltpu.make_async_copy(block(e, chip, c), own.at[chip, pl.ds(offsets[e], entries[e][2]), :],
                                         own_sems.at[e, chip])

        @pl.when(q == 0)
        def _():
            barrier = pltpu.get_barrier_semaphore()
            pl.semaphore_signal(barrier, inc=1, device_id=(x, y, 1 - c), device_id_type=MESH)
            pl.semaphore_wait(barrier, 1)
            for chip in range(N_CHIP):
                for e in range(n_ent):
                    to_sibling(e, chip).start()
            for chip in range(N_CHIP):
                for e in range(n_ent):
                    own_block(e, chip).start()

        for e in range(n_ent):
            own_block(e, q).wait()
            to_sibling(e, q).wait_recv()
        out_ref[...] = (own[q].astype(F32) + rbuf[q].astype(F32)).astype(out_ref.dtype)

        @pl.when(q == N_CHIP - 1)
        def _():
            for chip in range(N_CHIP):
                for e in range(n_ent):
                    to_sibling(e, chip).wait_send()

    return pl.pallas_call(
        body, name=name, grid=(N_CHIP,),
        in_specs=[ANY] * n_arr,
        out_specs=pl.BlockSpec((None, total, cols), lambda q: (q, 0, 0)),
        out_shape=jax.ShapeDtypeStruct((N_CHIP, total, cols), BF16),
        scratch_shapes=[pltpu.VMEM((N_CHIP, total, cols), BF16), pltpu.VMEM((N_CHIP, total, cols), BF16),
                        pltpu.SemaphoreType.DMA((n_ent, N_CHIP)), pltpu.SemaphoreType.DMA((n_ent, N_CHIP)),
                        pltpu.SemaphoreType.DMA((n_ent, N_CHIP))],
        compiler_params=pltpu.CompilerParams(dimension_semantics=("arbitrary",), vmem_limit_bytes=VMEM_LIMIT,
                                             collective_id=COLLECTIVE_IDS[frozenset(["sibling"])]),
    )(*arrays)


class _HostedScatterAll:
    def __init__(self, packed):
        n = packed.shape[0] // N_DEV
        self.n = n
        self.peers = set(PEER_FLIPS)
        self.arrays = [packed]
        self.out_shapes = [jax.ShapeDtypeStruct((N_DEV, n, packed.shape[1]), packed.dtype)]
        self.sem_shapes = [pltpu.SemaphoreType.DMA((N_DEV - 1,)), pltpu.SemaphoreType.DMA((N_DEV - 1,)),
                           pltpu.SemaphoreType.DMA(())]

    def _copies(self, ins, outs, sems, with_arrivals):
        src, land = ins[0], outs[0]
        send_sems, recv_sems, local_sem = sems
        me = _my_index()

        def block(p):
            return src.at[pl.ds(p * self.n, self.n), :]

        local = pltpu.make_async_copy(block(me), land.at[me], local_sem)
        sends, arrivals = [], []
        for k in range(1, N_DEV):
            p = (me + k) % N_DEV
            q = (me + N_DEV - k) % N_DEV
            sends.append(pltpu.make_async_remote_copy(
                src_ref=block(p), dst_ref=land.at[me], send_sem=send_sems.at[k - 1], recv_sem=recv_sems.at[k - 1],
                device_id=(p // 4, (p // 2) % 2, p % 2), device_id_type=MESH))
            if with_arrivals:
                arrivals.append(pltpu.make_async_remote_copy(
                    src_ref=block(me), dst_ref=land.at[q], send_sem=send_sems.at[k - 1], recv_sem=recv_sems.at[k - 1],
                    device_id=(q // 4, (q // 2) % 2, q % 2), device_id_type=MESH))
        return local, sends, arrivals

    def begin(self, ins, outs, sems, n_steps):
        @pl.when(pl.program_id(0) == 0)
        def _():
            local, sends, _ = self._copies(ins, outs, sems, with_arrivals=False)
            local.start()
            for cp in sends:
                cp.start()

    def end(self, ins, outs, sems, n_steps):
        @pl.when(pl.program_id(0) == n_steps - 1)
        def _():
            local, sends, arrivals = self._copies(ins, outs, sems, with_arrivals=True)
            for cp in arrivals:
                cp.wait_recv()
            for cp in sends:
                cp.wait_send()
            local.wait()


def _tril_weights(ws_ref):
    r = lax.broadcasted_iota(jnp.int32, (CHUNK, CHUNK), 0)
    c = lax.broadcasted_iota(jnp.int32, (CHUNK, CHUNK), 1)
    return [jnp.where(r >= c, ws_ref[h], 0.0).astype(BF16) for h in range(HEADS)]


def _sgu_stats(zpre, gain, bias):
    e = zpre.shape[1] // 2
    z, dz = _gelu_and_grad(zpre)
    u, v = z[:, :e], z[:, e:]
    vc = v - _row_mean(v)
    rstd = lax.rsqrt(_row_mean(vc * vc) + NORM_EPS)
    vhat = vc * rstd
    return u, vhat, rstd, vhat * gain + bias, dz


def _spatial_fwd(wt, vn_bf, bfull_ref, sv_ref, tm):
    for ci in range(tm // CHUNK):
        rows = slice(ci * CHUNK, (ci + 1) * CHUNK)
        for h in range(HEADS):
            cols = slice(h * GROUP, (h + 1) * GROUP)
            sv_ref[rows, cols] = _nn(wt[h], vn_bf[rows, cols]) + bfull_ref[:, cols]


def _mixer_a_fwd(x, g, gath, gain, bias, ws, bfull, tm, hosted=()):
    t_tok, d = x.shape
    e = gain.shape[1]
    e2 = 2 * e
    n_in, n_out = e2 // N_DEV, e // N_DEV

    def body(x_ref, g_ref, gain_ref, bias_ref, ws_ref, bfull_ref, gath_ref,
             xo_ref, gd_ref, u_ref, vhat_ref, svo_ref, y_ref, rstd_ref, win_v, wout_v, sv_v, sems):
        _load_group([(gath_ref, 0, n_in, win_v), (gath_ref, n_in, n_out, wout_v)], sems)
        xv = x_ref[...]
        h = _rms_fwd(xv, g_ref[...])[0].astype(BF16)
        zpre = _nt(h, win_v[...])
        u, vhat, rstd, vn, gelu_d = _sgu_stats(zpre, gain_ref[...], bias_ref[...])
        gd_ref[...] = gelu_d.astype(BF16)
        u_ref[...] = u.astype(BF16)
        vhat_ref[...] = vhat.astype(BF16)
        rstd_ref[...] = rstd
        _spatial_fwd(_tril_weights(ws_ref), vn.astype(BF16), bfull_ref, sv_v, tm)
        sv = sv_v[...]
        svo_ref[...] = sv.astype(BF16)
        y = (u * sv).astype(BF16)
        y_ref[...] = y
        xo_ref[...] = xv + _nn(y, wout_v[...])

    return _hosting_call(
        body, "mixer_a_fwd", t_tok // tm, [x, g, gain, bias, ws, bfull, gath],
        in_specs=[_row_spec(tm, d), _const_spec((1, d)), _const_spec((1, e)), _const_spec((1, e)),
                  _const_spec((HEADS, CHUNK, CHUNK)), _const_spec((CHUNK, e)), ANY],
        out_specs=[_row_spec(tm, d), _row_spec(tm, e2), _row_spec(tm, e), _row_spec(tm, e), _row_spec(tm, e),
                   _row_spec(tm, e), _row_spec(tm, 1)],
        out_shape=[jax.ShapeDtypeStruct((t_tok, d), F32), jax.ShapeDtypeStruct((t_tok, e2), BF16),
                   jax.ShapeDtypeStruct((t_tok, e), BF16), jax.ShapeDtypeStruct((t_tok, e), BF16),
                   jax.ShapeDtypeStruct((t_tok, e), BF16), jax.ShapeDtypeStruct((t_tok, e), BF16),
                   jax.ShapeDtypeStruct((t_tok, 1), F32)],
        scratch=[pltpu.VMEM((e2, d), BF16), pltpu.VMEM((e, d), BF16), pltpu.VMEM((tm, e), F32),
                 pltpu.SemaphoreType.DMA((2 * N_DEV,))],
        hosted=hosted)


def _mixer_a_bwd(dout, x, gd, u_sav, vhat_sav, sv_sav, rstd_sav, g, gath, gain, bias, ws, tm, hosted=()):
    t_tok, d = x.shape
    e = gain.shape[1]
    e2 = 2 * e
    n_in, n_out = e2 // N_DEV, e // N_DEV
    n_steps = t_tok // tm

    def body(dout_ref, x_ref, gd_ref, u_ref, vhat_ref, sv_ref, rstd_ref, g_ref, gain_ref, bias_ref, ws_ref, gath_ref,
             dx_ref, h_ref, dz_ref, dg_ref, dgain_ref, dbias_ref, dws_ref, dbso_ref,
             win_v, wout_v, dvn_v, dbs_ref, sems):
        i = pl.program_id(0)
        _load_group([(gath_ref, 0, n_in, win_v), (gath_ref, n_in, n_out, wout_v)], sems)

        @pl.when(i == 0)
        def _():
            dg_ref[...] = jnp.zeros_like(dg_ref)
            dgain_ref[...] = jnp.zeros_like(dgain_ref)
            dbias_ref[...] = jnp.zeros_like(dbias_ref)
            dws_ref[...] = jnp.zeros_like(dws_ref)
            dbs_ref[...] = jnp.zeros_like(dbs_ref)

        xv = x_ref[...]
        gv = g_ref[...]
        hv, xhat, r = _rms_fwd(xv, gv)
        h_ref[...] = hv.astype(BF16)
        gain_v = gain_ref[...]
        vhat = vhat_ref[...].astype(F32)
        vn_bf = (vhat * gain_v + bias_ref[...]).astype(BF16)
        wt = _tril_weights(ws_ref)

        dov = dout_ref[...]
        dy = _nt(dov.astype(BF16), wout_v[...])
        du = dy * sv_ref[...].astype(F32)
        dsv = dy * u_ref[...].astype(F32)
        dsv_bf = dsv.astype(BF16)
        for ci in range(tm // CHUNK):
            rows = slice(ci * CHUNK, (ci + 1) * CHUNK)
            dbs_ref[...] += dsv[rows, :]
            for h in range(HEADS):
                cols = slice(h * GROUP, (h + 1) * GROUP)
                dvn_v[rows, cols] = _tn(wt[h], dsv_bf[rows, cols])
                dws_ref[h] += _nt(dsv_bf[rows, cols], vn_bf[rows, cols])
        dvn = dvn_v[...]
        dgain_ref[...] += _col_sum(dvn * vhat)
        dbias_ref[...] += _col_sum(dvn)
        dvhat = dvn * gain_v
        dv = rstd_ref[...] * (dvhat - _row_mean(dvhat) - vhat * _row_mean(dvhat * vhat))
        dzpre = (jnp.concatenate([du, dv], axis=1) * gd_ref[...].astype(F32)).astype(BF16)
        dz_ref[...] = dzpre
        dh = _nn(dzpre, win_v[...])
        dxr, dg_row = _rms_bwd(dh, xhat, r, gv)
        dg_ref[...] += dg_row
        dx_ref[...] = dov + dxr

        @pl.when(i == n_steps - 1)
        def _():
            rr = lax.broadcasted_iota(jnp.int32, (CHUNK, CHUNK), 0)
            cc = lax.broadcasted_iota(jnp.int32, (CHUNK, CHUNK), 1)
            for h in range(HEADS):
                dws_ref[h] = jnp.where(rr >= cc, dws_ref[h], 0.0)
                dbso_ref[h] = jnp.sum(dbs_ref[:, h * GROUP:(h + 1) * GROUP], axis=1, keepdims=True)

    return _hosting_call(
        body, "mixer_a_bwd", n_steps, [dout, x, gd, u_sav, vhat_sav, sv_sav, rstd_sav, g, gain, bias, ws, gath],
        in_specs=[_row_spec(tm, d), _row_spec(tm, d), _row_spec(tm, e2), _row_spec(tm, e), _row_spec(tm, e),
                  _row_spec(tm, e), _row_spec(tm, 1), _const_spec((1, d)),
                  _const_spec((1, e)), _const_spec((1, e)), _const_spec((HEADS, CHUNK, CHUNK)), ANY],
        out_specs=[_row_spec(tm, d), _row_spec(tm, d), _row_spec(tm, e2),
                   _const_spec((1, d)), _const_spec((1, e)), _const_spec((1, e)),
                   _const_spec((HEADS, CHUNK, CHUNK)), _const_spec((HEADS, CHUNK, 1))],
        out_shape=[jax.ShapeDtypeStruct((t_tok, d), F32),
                   jax.ShapeDtypeStruct((t_tok, d), BF16), jax.ShapeDtypeStruct((t_tok, e2), BF16),
                   jax.ShapeDtypeStruct((1, d), F32), jax.ShapeDtypeStruct((1, e), F32),
                   jax.ShapeDtypeStruct((1, e), F32), jax.ShapeDtypeStruct((HEADS, CHUNK, CHUNK), F32),
                   jax.ShapeDtypeStruct((HEADS, CHUNK, 1), F32)],
        scratch=[pltpu.VMEM((e2, d), BF16), pltpu.VMEM((e, d), BF16), pltpu.VMEM((tm, e), F32),
                 pltpu.VMEM((CHUNK, e), F32), pltpu.SemaphoreType.DMA((2 * N_DEV,))],
        hosted=hosted)


def _ffn_fwd(x, g, srcs, nf, tm, name, hosted=(), head=None):
    t_tok, d = x.shape
    f = nf * N_DEV
    firsts = [first for _, first in srcs]
    n_head = 2 if head else 0

    def body(*refs):
        x_ref, g_ref, sg_ref, su_ref, sd_ref = refs[:5]
        gate_ref, up_ref, wg_v, wu_v, wd_v, sems = refs[-6:]
        _load_group(
            [(sg_ref, firsts[0], nf, wg_v), (su_ref, firsts[1], nf, wu_v), (sd_ref, firsts[2], nf, wd_v)], sems)
        if head:
            t_ref, gf_ref, loss_ref, dx_ref, dxb_ref, dgf_ref = refs[5:11]

            @pl.when(pl.program_id(0) == 0)
            def _():
                loss_ref[...] = jnp.zeros_like(loss_ref)
                dgf_ref[...] = jnp.zeros_like(dgf_ref)

        xv = x_ref[...]
        h = _rms_fwd(xv, g_ref[...])[0].astype(BF16)
        gate = _nt(h, wg_v[...])
        up = _nt(h, wu_v[...])
        gate_ref[...] = gate.astype(BF16)
        up_ref[...] = up.astype(BF16)
        act = (gate * _sigmoid(gate) * up).astype(BF16)
        xo = xv + _nn(act, wd_v[...])
        if head:
            gfv = gf_ref[...]
            y, xhat, r = _rms_fwd(xo, gfv)
            err = y - t_ref[...]
            loss_ref[...] += 0.5 * jnp.sum(_row_mean(err * err), axis=0, keepdims=True)
            dxr, dg_row = _rms_bwd(err * (1.0 / d), xhat, r, gfv)
            dgf_ref[...] += dg_row
            dx_ref[...] = dxr
            dxb_ref[...] = dxr.astype(BF16)
        else:
            refs[5][...] = xo

    act_specs = [_row_spec(tm, f), _row_spec(tm, f)]
    act_shapes = [jax.ShapeDtypeStruct((t_tok, f), BF16), jax.ShapeDtypeStruct((t_tok, f), BF16)]
    if head:
        out_specs = [_const_spec((1, 1)), _row_spec(tm, d), _row_spec(tm, d), _const_spec((1, d))]
        out_shape = [jax.ShapeDtypeStruct((1, 1), F32), jax.ShapeDtypeStruct((t_tok, d), F32),
                     jax.ShapeDtypeStruct((t_tok, d), BF16), jax.ShapeDtypeStruct((1, d), F32)]
    else:
        out_specs = [_row_spec(tm, d)]
        out_shape = [jax.ShapeDtypeStruct((t_tok, d), F32)]
    return _hosting_call(
        body, name, t_tok // tm, [x, g] + [arr for arr, _ in srcs] + list(head or ()),
        in_specs=[_row_spec(tm, d), _const_spec((1, d)), ANY, ANY, ANY] + [_row_spec(tm, d), _const_spec((1, d))][:n_head],
        out_specs=out_specs + act_specs, out_shape=out_shape + act_shapes,
        scratch=[pltpu.VMEM((f, d), BF16), pltpu.VMEM((f, d), BF16), pltpu.VMEM((f, d), BF16),
                 pltpu.SemaphoreType.DMA((3 * N_DEV,))],
        hosted=hosted)


def _ffn_bwd(dout, x, gate, up, g, srcs, nf, tm, name, hosted=()):
    t_tok, d = x.shape
    f = nf * N_DEV
    firsts = [first for _, first in srcs]
    per_chunk = -(-f // (FFN_CHUNKS * MXU_WIDTH)) * MXU_WIDTH
    bounds = [min(ck * per_chunk, f) for ck in range(FFN_CHUNKS + 1)]

    def body(dout_ref, x_ref, gate_ref, up_ref, g_ref, sg_ref, su_ref, sd_ref,
             dx_ref, dxb_ref, h_ref, act_ref, dgu_ref, dg_ref, wg_v, wu_v, wd_v, sems):
        _load_group(
            [(sg_ref, firsts[0], nf, wg_v), (su_ref, firsts[1], nf, wu_v), (sd_ref, firsts[2], nf, wd_v)], sems)

        @pl.when(pl.program_id(0) == 0)
        def _():
            dg_ref[...] = jnp.zeros_like(dg_ref)

        xv = x_ref[...]
        gv = g_ref[...]
        hv, xhat, r = _rms_fwd(xv, gv)
        h_ref[...] = hv.astype(BF16)
        dov = dout_ref[...]
        dob = dov.astype(BF16)
        dh = None
        for ck in range(FFN_CHUNKS):
            cols = slice(bounds[ck], bounds[ck + 1])
            gate_v = gate_ref[:, cols].astype(F32)
            up_v = up_ref[:, cols].astype(F32)
            sig = _sigmoid(gate_v)
            silu = gate_v * sig
            act_ref[:, cols] = (silu * up_v).astype(BF16)
            dact = _nt(dob, wd_v[cols, :])
            dup = (dact * silu).astype(BF16)
            dgate = (dact * up_v * (sig * (1.0 + gate_v * (1.0 - sig)))).astype(BF16)
            dgu_ref[:, cols] = dgate
            dgu_ref[:, f + bounds[ck]:f + bounds[ck + 1]] = dup
            part = _nn(dgate, wg_v[cols, :]) + _nn(dup, wu_v[cols, :])
            dh = part if dh is None else dh + part
        dxr, dg_row = _rms_bwd(dh, xhat, r, gv)
        dg_ref[...] += dg_row
        dx = dov + dxr
        dx_ref[...] = dx
        dxb_ref[...] = dx.astype(BF16)

    return _hosting_call(
        body, name, t_tok // tm, [dout, x, gate, up, g] + [arr for arr, _ in srcs],
        in_specs=[_row_spec(tm, d), _row_spec(tm, d), _row_spec(tm, f), _row_spec(tm, f), _const_spec((1, d)),
                  ANY, ANY, ANY],
        out_specs=[_row_spec(tm, d), _row_spec(tm, d), _row_spec(tm, d), _row_spec(tm, f), _row_spec(tm, 2 * f),
                   _const_spec((1, d))],
        out_shape=[jax.ShapeDtypeStruct((t_tok, d), F32), jax.ShapeDtypeStruct((t_tok, d), BF16),
                   jax.ShapeDtypeStruct((t_tok, d), BF16), jax.ShapeDtypeStruct((t_tok, f), BF16),
                   jax.ShapeDtypeStruct((t_tok, 2 * f), BF16), jax.ShapeDtypeStruct((1, d), F32)],
        scratch=[pltpu.VMEM((f, d), BF16), pltpu.VMEM((f, d), BF16), pltpu.VMEM((f, d), BF16),
                 pltpu.SemaphoreType.DMA((3 * N_DEV,))],
        hosted=hosted)


def _shift_down(z, k, prev_rows):
    row = lax.broadcasted_iota(jnp.int32, z.shape, 0)
    out = pltpu.roll(z, k, 0)
    for j in range(k):
        out = jnp.where(row == j, prev_rows[j], out)
    return out


def _shift_up(z, k, next_rows):
    tm = z.shape[0]
    row = lax.broadcasted_iota(jnp.int32, z.shape, 0)
    out = pltpu.roll(z, tm - k, 0)
    for j in range(k):
        out = jnp.where(row == tm - k + j, next_rows[j], out)
    return out


def _mixer_b_fwd(x, g, gath, conv_w, tm, seq, hosted=()):
    t_tok, d = x.shape
    e = conv_w.shape[1]
    e3 = 3 * e
    n_in, n_out = e3 // N_DEV, e // N_DEV
    tiles_per_seq = seq // tm

    def body(x_ref, g_ref, cw_ref, gath_ref, xo_ref, p_ref, win_v, wout_v, tail_v, sems):
        i = pl.program_id(0)
        _load_group([(gath_ref, 0, n_in, win_v), (gath_ref, n_in, n_out, wout_v)], sems)

        @pl.when(i % tiles_per_seq == 0)
        def _():
            tail_v[...] = jnp.zeros_like(tail_v)

        xv = x_ref[...]
        h = _rms_fwd(xv, g_ref[...])[0].astype(BF16)
        p = _nt(h, win_v[...])
        p_ref[...] = p.astype(BF16)
        z = p[:, e:2 * e] * p[:, 2 * e:]
        prev = [tail_v[SUBLANES - 2:SUBLANES - 1, :], tail_v[SUBLANES - 1:SUBLANES, :]]
        conv = (cw_ref[2:3, :] * z + cw_ref[1:2, :] * _shift_down(z, 1, prev[1:])
                + cw_ref[0:1, :] * _shift_down(z, 2, prev))
        tail_v[...] = z[tm - SUBLANES:, :]
        y = (p[:, :e] * conv).astype(BF16)
        xo_ref[...] = xv + _nn(y, wout_v[...])

    return _hosting_call(
        body, "mixer_b_fwd", t_tok // tm, [x, g, conv_w, gath],
        in_specs=[_row_spec(tm, d), _const_spec((1, d)), _const_spec((SUBLANES, e)), ANY],
        out_specs=[_row_spec(tm, d), _row_spec(tm, e3)],
        out_shape=[jax.ShapeDtypeStruct((t_tok, d), F32), jax.ShapeDtypeStruct((t_tok, e3), BF16)],
        scratch=[pltpu.VMEM((e3, d), BF16), pltpu.VMEM((e, d), BF16), pltpu.VMEM((SUBLANES, e), F32),
                 pltpu.SemaphoreType.DMA((2 * N_DEV,))],
        hosted=hosted)


def _mixer_b_bwd(dout, x, p, g, gath, conv_w, tm, seq, hosted=()):
    t_tok, d = x.shape
    e = conv_w.shape[1]
    e3 = 3 * e
    n_in, n_out = e3 // N_DEV, e // N_DEV
    tiles_per_seq = seq // tm
    halo_per_tile = tm // HALO
    n_halo = t_tok // HALO

    def body(dout_ref, dnext_ref, x_ref, p_ref, pprev_ref, pnext_ref, g_ref, cw_ref, gath_ref,
             dx_ref, dxb_ref, h_ref, y_ref, dp_ref, dg_ref, dcw_ref, win_v, wout_v, sems):
        i = pl.program_id(0)
        _load_group([(gath_ref, 0, n_in, win_v), (gath_ref, n_in, n_out, wout_v)], sems)

        @pl.when(i == 0)
        def _():
            dg_ref[...] = jnp.zeros_like(dg_ref)
            dcw_ref[...] = jnp.zeros_like(dcw_ref)

        first = (i % tiles_per_seq == 0).astype(F32)
        last = (i % tiles_per_seq == tiles_per_seq - 1).astype(F32)
        xv = x_ref[...]
        gv = g_ref[...]
        hv, xhat, r = _rms_fwd(xv, gv)
        h_ref[...] = hv.astype(BF16)
        pv = p_ref[...].astype(F32)
        bg, cg, hx = pv[:, :e], pv[:, e:2 * e], pv[:, 2 * e:]
        z = cg * hx
        pprev = pprev_ref[...].astype(F32)
        zprev = pprev[:, e:2 * e] * pprev[:, 2 * e:] * (1.0 - first)
        prev = [zprev[HALO - 2:HALO - 1, :], zprev[HALO - 1:HALO, :]]
        zs1 = _shift_down(z, 1, prev[1:])
        zs2 = _shift_down(z, 2, prev)
        w0, w1, w2 = cw_ref[0:1, :], cw_ref[1:2, :], cw_ref[2:3, :]
        conv = w2 * z + w1 * zs1 + w0 * zs2
        y_ref[...] = (bg * conv).astype(BF16)

        dov = dout_ref[...]
        wout_bf = wout_v[...]
        dy = _nt(dov.astype(BF16), wout_bf)
        dconv = dy * bg
        dnext = _nt(dnext_ref[...].astype(BF16), wout_bf) * pnext_ref[:, :e].astype(F32) * (1.0 - last)
        nxt = [dnext[0:1, :], dnext[1:2, :]]
        dz = w2 * dconv + w1 * _shift_up(dconv, 1, nxt[:1]) + w0 * _shift_up(dconv, 2, nxt)
        dcw_ref[0:1, :] += _col_sum(dconv * zs2)
        dcw_ref[1:2, :] += _col_sum(dconv * zs1)
        dcw_ref[2:3, :] += _col_sum(dconv * z)
        dp = jnp.concatenate([dy * conv, dz * hx, dz * cg], axis=1).astype(BF16)
        dp_ref[...] = dp
        dh = _nn(dp, win_v[...])
        dxr, dg_row = _rms_bwd(dh, xhat, r, gv)
        dg_ref[...] += dg_row
        dx = dov + dxr
        dx_ref[...] = dx
        dxb_ref[...] = dx.astype(BF16)

    prev_spec = lambda w: pl.BlockSpec((HALO, w), lambda i: (jnp.maximum(i * halo_per_tile - 1, 0), 0))
    next_spec = lambda w: pl.BlockSpec((HALO, w), lambda i: (jnp.minimum((i + 1) * halo_per_tile, n_halo - 1), 0))
    return _hosting_call(
        body, "mixer_b_bwd", t_tok // tm, [dout, dout, x, p, p, p, g, conv_w, gath],
        in_specs=[_row_spec(tm, d), next_spec(d), _row_spec(tm, d), _row_spec(tm, e3), prev_spec(e3), next_spec(e3),
                  _const_spec((1, d)), _const_spec((SUBLANES, e)), ANY],
        out_specs=[_row_spec(tm, d), _row_spec(tm, d), _row_spec(tm, d), _row_spec(tm, e), _row_spec(tm, e3),
                   _const_spec((1, d)), _const_spec((SUBLANES, e))],
        out_shape=[jax.ShapeDtypeStruct((t_tok, d), F32), jax.ShapeDtypeStruct((t_tok, d), BF16),
                   jax.ShapeDtypeStruct((t_tok, d), BF16), jax.ShapeDtypeStruct((t_tok, e), BF16),
                   jax.ShapeDtypeStruct((t_tok, e3), BF16), jax.ShapeDtypeStruct((1, d), F32),
                   jax.ShapeDtypeStruct((SUBLANES, e), F32)],
        scratch=[pltpu.VMEM((e3, d), BF16), pltpu.VMEM((e, d), BF16), pltpu.SemaphoreType.DMA((2 * N_DEV,))],
        hosted=hosted)


def _wgrad(a, b, bm, name, hosted=(), part=(1, 0)):
    t_tok, m = a.shape
    n = b.shape[1]
    every, first = part

    def body(a_ref, b_ref, o_ref):
        o_ref[...] = _tn(a_ref[...], b_ref[...]).astype(o_ref.dtype)

    outs, h_outs = _hosting_call(
        body, name, m // (bm * every), [a, b],
        in_specs=[pl.BlockSpec((t_tok, bm), lambda i: (0, every * i + first)), _const_spec((t_tok, n))],
        out_specs=[pl.BlockSpec((bm, n), lambda i: (i, 0))],
        out_shape=[jax.ShapeDtypeStruct((m // every, n), BF16)],
        scratch=[], hosted=hosted)
    return (outs[0], h_outs) if hosted else outs[0]


def _sum_slots(land, rb, name):
    n_slots, rows, cols = land.shape

    def body(l_ref, o_ref):
        acc = l_ref[0].astype(F32)
        for k in range(1, n_slots):
            acc = acc + l_ref[k].astype(F32)
        o_ref[...] = acc

    return pl.pallas_call(
        body, name=name, grid=(rows // rb,),
        in_specs=[pl.BlockSpec((n_slots, rb, cols), lambda i: (0, i, 0))],
        out_specs=pl.BlockSpec((rb, cols), lambda i: (i, 0)),
        out_shape=jax.ShapeDtypeStruct((rows, cols), F32),
        compiler_params=_params(sequential=False),
    )(land)


def _adamw(w, grad, m, v, rb, name):
    rows, cols = w.shape
    c1 = 1.0 / (1.0 - ADAM_B1 ** ADAM_STEP)
    c2 = 1.0 / (1.0 - ADAM_B2 ** ADAM_STEP)

    def body(w_ref, g_ref, m_ref, v_ref, d_ref, mo_ref, vo_ref):
        gv = g_ref[...]
        mn = ADAM_B1 * m_ref[...] + (1.0 - ADAM_B1) * gv
        vn = ADAM_B2 * v_ref[...] + (1.0 - ADAM_B2) * (gv * gv)
        mo_ref[...] = mn
        vo_ref[...] = vn
        d_ref[...] = -ADAM_LR * ((mn * c1) / (jnp.sqrt(vn * c2) + ADAM_EPS) + ADAM_WD * w_ref[...])

    spec = pl.BlockSpec((rb, cols), lambda i: (i, 0))
    shape = jax.ShapeDtypeStruct((rows, cols), F32)
    return pl.pallas_call(
        body, name=name, grid=(rows // rb,),
        in_specs=[spec] * 4, out_specs=[spec] * 3, out_shape=[shape] * 3,
        compiler_params=_params(sequential=False),
    )(w, grad, m, v)


def _pack_shards(groups, name, hosted=()):
    flat = [(part, layer) for group in groups for part, layer, _ in group]
    rows = [[p.shape[2] if turn else p.shape[1] for p, _, turn in group] for group in groups]
    first, _, first_turn = groups[0][0]
    width = first.shape[1] if first_turn else first.shape[2]

    def body(*refs):
        ins, outs = refs[:len(flat)], refs[len(flat):]
        k = 0
        for gi, group in enumerate(groups):
            off = 0
            for (_, _, turn), n in zip(group, rows[gi]):
                part = ins[k][...].astype(BF16)
                if turn:
                    r = lax.broadcasted_iota(jnp.int32, (n, n), 0)
                    c = lax.broadcasted_iota(jnp.int32, (n, n), 1)
                    part = _nt((r == c).astype(BF16), part).astype(BF16)
                outs[gi][off:off + n, :] = part
                off += n
                k += 1

    return _hosting_call(
        body, name, 1, [p for p, _ in flat],
        in_specs=[pl.BlockSpec((None,) + p.shape[1:], lambda i, layer=layer: (layer, 0, 0)) for p, layer in flat],
        out_specs=[_const_spec((sum(r), width)) for r in rows],
        out_shape=[jax.ShapeDtypeStruct((sum(r), width), BF16) for r in rows],
        scratch=[], hosted=hosted)


def _split_bf16(a):
    hi = a.astype(BF16)
    rest = a - hi.astype(F32)
    mid = rest.astype(BF16)
    return hi, mid, (rest - mid.astype(F32)).astype(BF16)


def _reduce_adamw(lands, w, m, v, transpose, name, hosted=()):
    n_layers, rows_w, cols_w = w.shape
    c1 = 1.0 / (1.0 - ADAM_B1 ** ADAM_STEP)
    c2 = 1.0 / (1.0 - ADAM_B2 ** ADAM_STEP)
    flat = [piece for pieces in lands for piece in pieces]
    counts = [len(pieces) for pieces in lands]
    if transpose:
        tiles = rows_w // MXU_WIDTH
        blk = (MXU_WIDTH, cols_w)
        land_specs = [pl.BlockSpec((N_CHIP, n, MXU_WIDTH), lambda i, b=first // n: (0, b, i % tiles))
                      for _, first, n in flat]
        for _, first, n in flat:
            assert first % n == 0
    else:
        tiles = 1
        blk = (rows_w // tiles, cols_w)
        assert all(c == 1 for c in counts)
        land_specs = [pl.BlockSpec((N_CHIP,) + blk, lambda i, b=first // blk[0]: (0, b + i % tiles, 0))
                      for _, first, _ in flat]
        for _, first, _ in flat:
            assert first % blk[0] == 0

    def body(*refs):
        land_refs = refs[:len(flat)]
        w_ref, m_ref, v_ref, g_ref, d_ref, mo_ref, vo_ref = refs[len(flat):]
        layer = pl.program_id(0) // tiles

        def total(ref):
            acc = ref[0].astype(F32)
            for q in range(1, N_CHIP):
                acc = acc + ref[q].astype(F32)
            return acc

        def layer_sum(k):
            first = sum(counts[:k])
            parts = [total(land_refs[first + j]) for j in range(counts[k])]
            return parts[0] if len(parts) == 1 else jnp.concatenate(parts, axis=0)

        gv = layer_sum(0)
        for k in range(1, n_layers):
            gv = jnp.where(layer == k, layer_sum(k), gv)
        if transpose:
            r = lax.broadcasted_iota(jnp.int32, (MXU_WIDTH, MXU_WIDTH), 0)
            c = lax.broadcasted_iota(jnp.int32, (MXU_WIDTH, MXU_WIDTH), 1)
            eye = (r == c).astype(BF16)
            hi, mid, lo = _split_bf16(gv)
            gv = _nt(eye, hi) + _nt(eye, mid) + _nt(eye, lo)
        g_ref[...] = gv
        mn = ADAM_B1 * m_ref[...] + (1.0 - ADAM_B1) * gv
        vn = ADAM_B2 * v_ref[...] + (1.0 - ADAM_B2) * (gv * gv)
        mo_ref[...] = mn
        vo_ref[...] = vn
        d_ref[...] = -ADAM_LR * ((mn * c1) / (jnp.sqrt(vn * c2) + ADAM_EPS) + ADAM_WD * w_ref[...])

    spec = pl.BlockSpec((None,) + blk, lambda i: (i // tiles, i % tiles, 0))
    shape = jax.ShapeDtypeStruct(w.shape, F32)
    outs, h_outs = _hosting_call(
        body, name, n_layers * tiles, [land for land, _, _ in flat] + [w, m, v],
        in_specs=land_specs + [spec] * 3, out_specs=[spec] * 4, out_shape=[shape] * 4, scratch=[], hosted=hosted)
    return (outs, h_outs) if hosted else outs


def _pack_small(parts, rows):
    flat = jnp.concatenate([p.reshape(-1).astype(F32) for p in parts])
    return jnp.pad(flat, (0, rows * LANES - flat.shape[0])).reshape(rows, LANES)


def _unpack_small(packed, shapes):
    flat = packed.reshape(-1)
    out = []
    pos = 0
    for s in shapes:
        n = math.prod(s)
        out.append(flat[pos:pos + n].reshape(s))
        pos += n
    return out


def kernel(x, mix_norm, ffn_norm, a_w_in, a_v_gain, a_v_bias, a_w_s, a_b_s, a_w_out, b_w_in, b_conv_w, b_w_out, ffn_w_gate, ffn_w_up, ffn_w_down, final_norm, loss_target, m_mix_norm, m_ffn_norm, m_a_w_in, m_a_v_gain, m_a_v_bias, m_a_w_s, m_a_b_s, m_a_w_out, m_b_w_in, m_b_conv_w, m_b_w_out, m_ffn_w_gate, m_ffn_w_up, m_ffn_w_down, m_final_norm, v_mix_norm, v_ffn_norm, v_a_w_in, v_a_v_gain, v_a_v_bias, v_a_w_s, v_a_b_s, v_a_w_out, v_b_w_in, v_b_conv_w, v_b_w_out, v_ffn_w_gate, v_ffn_w_up, v_ffn_w_down, v_final_norm):
    bsz, seq, d = x.shape
    t_tok = bsz * seq
    me = _my_index()
    xt = x.reshape(t_tok, d)
    target = loss_target.reshape(t_tok, d)
    e_a = a_v_gain.shape[1]
    e_b = b_w_out.shape[1] * N_DEV
    n_layers = ffn_w_gate.shape[0]
    f_shard = ffn_w_gate.shape[2]
    f_full = f_shard * N_DEV

    conv_pad = jnp.pad(b_conv_w[0], ((0, SUBLANES - CONV_W), (0, 0)))
    sh_a = jnp.concatenate([a_w_in[0].T, a_w_out[0]]).astype(BF16)
    bfull = jnp.repeat(a_b_s[0].T, GROUP, axis=1)

    gate_t, up_t = ffn_w_gate.transpose(0, 2, 1), ffn_w_up.transpose(0, 2, 1)
    (sh_b, sh_f0, sh_f1g, sh_f1ud), (gath_a, conv_g) = _pack_shards(
        [[(b_w_in, 0, True), (b_w_out, 0, False)],
         [(gate_t, 0, False), (up_t, 0, False), (ffn_w_down, 0, False)],
         [(gate_t, 1, False)],
         [(up_t, 1, False), (ffn_w_down, 1, False)]],
        "pack_shards", hosted=[_HostedGathers([sh_a, conv_pad])])
    conv_full = jnp.pad(conv_g[:, :CONV_W, :].transpose(1, 0, 2).reshape(CONV_W, e_b), ((0, SUBLANES - CONV_W), (0, 0)))
    (x1, gd_a, u_a, vhat_a, sv_a, y_a, rstd_a), (gath_f0,) = _mixer_a_fwd(
        xt, mix_norm[0:1], gath_a, a_v_gain, a_v_bias, a_w_s[0], bfull, tm=TOKEN_TILE,
        hosted=[_HostedGathers([sh_f0])])
    srcs0 = [(gath_f0, 0), (gath_f0, f_shard), (gath_f0, 2 * f_shard)]
    (x2, gate0, up0), (gath_b, gath_f1g) = _ffn_fwd(x1, ffn_norm[0:1], srcs0, f_shard, tm=TOKEN_TILE, name="ffn_fwd0",
                                                    hosted=[_HostedGathers([sh_b, sh_f1g])])
    (x3, p_b), (gath_f1ud,) = _mixer_b_fwd(x2, mix_norm[1:2], gath_b, conv_full, tm=TOKEN_TILE, seq=seq,
                                           hosted=[_HostedGathers([sh_f1ud])])
    srcs1 = [(gath_f1g, 0), (gath_f1ud, 0), (gath_f1ud, f_shard)]
    (loss_part, dx4, dx4_bf, d_final, gate1, up1), _ = _ffn_fwd(
        x3, ffn_norm[1:2], srcs1, f_shard, tm=TOKEN_TILE_WIDE, name="ffn_fwd1", head=(target, final_norm.reshape(1, d)))

    ffn_entries = [(0, 0, f_shard), (0, f_full, f_shard), (1, 0, f_shard)]
    (dx3, dx3_bf, h_f1, act1, dgu1, d_fn1), _ = _ffn_bwd(dx4, x3, gate1, up1, ffn_norm[1:2], srcs1, f_shard, tm=TOKEN_TILE,
                                                         name="ffn_bwd1")
    g_down1 = _wgrad(act1, dx4_bf, WGRAD_ROWS, "wgrad_down1")
    g_gu1 = _wgrad(dgu1, h_f1, WGRAD_ROWS_WIDE, "wgrad_gate_up1")
    ps_f1 = _pair_reduce([g_gu1, g_down1], ffn_entries, "pair_reduce_f1")
    (dx2, dx2_bf, h_b, y_b, dp_b, d_mn1, d_conv), (land_f1gu,) = _mixer_b_bwd(
        dx3, x2, p_b, mix_norm[1:2], gath_b, conv_full, tm=TOKEN_TILE_WIDE, seq=seq,
        hosted=[_HostedChipScatter(ps_f1, 0, 2 * f_shard)])
    g_b_out = _wgrad(y_b, dx3_bf, WGRAD_ROWS, "wgrad_b_out")
    g_b_in = _wgrad(dp_b, h_b, WGRAD_ROWS_WIDE, "wgrad_b_in")
    ps_b = _pair_reduce([g_b_in, g_b_out], [(0, 0, b_w_in.shape[2]), (1, 0, b_w_out.shape[1])], "pair_reduce_b")
    (dx1, dx1_bf, h_f0, act0, dgu0, d_fn0), (land_f1d, land_b) = _ffn_bwd(
        dx2, x1, gate0, up0, ffn_norm[0:1], srcs0, f_shard, tm=TOKEN_TILE, name="ffn_bwd0",
        hosted=[_HostedChipScatter(ps_f1, 2 * f_shard, f_shard), _HostedChipScatter(ps_b)])
    g_down0 = _wgrad(act0, dx2_bf, WGRAD_ROWS, "wgrad_down0")
    g_gu0 = _wgrad(dgu0, h_f0, WGRAD_ROWS_WIDE, "wgrad_gate_up0")
    g_a_out = _wgrad(y_a, dx1_bf, WGRAD_ROWS, "wgrad_a_out")
    n_ao = a_w_out.shape[1]
    ps_f0ao = _pair_reduce([g_gu0, g_down0, g_a_out], ffn_entries + [(2, 0, n_ao)], "pair_reduce_f0_a_out")
    (dx0, h_a, dz_a, d_mn0, d_gain, d_bias, d_ws, d_bs_acc), (land_f0, land_ao) = _mixer_a_bwd(
        dx1, xt, gd_a, u_a, vhat_a, sv_a, rstd_a, mix_norm[0:1], gath_a, a_v_gain, a_v_bias, a_w_s[0], tm=TOKEN_TILE,
        hosted=[_HostedChipScatter(ps_f0ao, 0, 3 * f_shard), _HostedChipScatter(ps_f0ao, 3 * f_shard, n_ao)])
    d_bs = d_bs_acc.reshape(HEADS, CHUNK)

    small_grads = [jnp.concatenate([d_mn0, d_mn1]), jnp.concatenate([d_fn0, d_fn1]), d_gain, d_bias, d_ws, d_bs,
                   d_final, d_conv[:CONV_W], loss_part]
    small_shapes = [(n_layers, d), (n_layers, d), (1, e_a), (1, e_a), (1, HEADS, CHUNK, CHUNK), (1, HEADS, CHUNK), (d,),
                    (CONV_W, e_b), ()]
    n_small = sum(math.prod(s) for s in small_shapes)
    blk_rows = -(-n_small // (N_DEV * LANES * SUBLANES)) * SUBLANES
    small_rows = blk_rows * N_DEV
    packed = _pack_small(small_grads, small_rows)
    n_half = a_w_in.shape[2] // 2
    g_ai0, (small_land,) = _wgrad(dz_a, h_a, n_half, "wgrad_a_in0", hosted=[_HostedScatterAll(packed)], part=(2, 0))
    ps_ai0 = _pair_reduce([g_ai0], [(0, 0, n_half)], "pair_reduce_a_in0")
    small_sum = _sum_slots(small_land, blk_rows, "sum_small")
    g_ai1, (land_ai0,) = _wgrad(dz_a, h_a, n_half, "wgrad_a_in1", hosted=[_HostedChipScatter(ps_ai0)], part=(2, 1))
    ps_ai1 = _pair_reduce([g_ai1], [(0, 0, n_half)], "pair_reduce_a_in1")
    land_ai1, small_gath = _exchange([_HostedChipScatter(ps_ai1), _HostedGathers([small_sum])], "tail_exchange")
    small_all = small_gath.reshape(small_rows, LANES)

    n_b_in = b_w_in.shape[2]
    gate_out = _reduce_adamw([[(land_f0, 0, f_shard)], [(land_f1gu, 0, f_shard)]], gate_t,
                             m_ffn_w_gate.transpose(0, 2, 1), v_ffn_w_gate.transpose(0, 2, 1), False, "adamw_gate")
    up_out = _reduce_adamw([[(land_f0, f_shard, f_shard)], [(land_f1gu, f_shard, f_shard)]], up_t,
                           m_ffn_w_up.transpose(0, 2, 1), v_ffn_w_up.transpose(0, 2, 1), False, "adamw_up")
    res = {
        "a_w_in": _reduce_adamw([[(land_ai0, 0, n_half), (land_ai1, 0, n_half)]], a_w_in, m_a_w_in, v_a_w_in, True,
                                "adamw_a_in"),
        "a_w_out": _reduce_adamw([[(land_ao, 0, a_w_out.shape[1])]], a_w_out, m_a_w_out, v_a_w_out, False,
                                 "adamw_a_out"),
        "b_w_in": _reduce_adamw([[(land_b, 0, n_b_in)]], b_w_in, m_b_w_in, v_b_w_in, True, "adamw_b_in"),
        "b_w_out": _reduce_adamw([[(land_b, n_b_in, b_w_out.shape[1])]], b_w_out, m_b_w_out, v_b_w_out, False,
                                 "adamw_b_out"),
        "ffn_w_gate": [o.transpose(0, 2, 1) for o in gate_out],
        "ffn_w_up": [o.transpose(0, 2, 1) for o in up_out],
        "ffn_w_down": _reduce_adamw([[(land_f0, 2 * f_shard, f_shard)], [(land_f1d, 0, f_shard)]], ffn_w_down,
                                    m_ffn_w_down, v_ffn_w_down, False, "adamw_down"),
    }

    (gr_mix, gr_ffn, gr_gain, gr_bias, gr_ws, gr_bs, gr_final, gr_conv_full, loss) = _unpack_small(small_all, small_shapes)
    gr_conv = lax.dynamic_slice_in_dim(gr_conv_full, me * (e_b // N_DEV), e_b // N_DEV, axis=1)[None]

    small_w =[mix_norm, ffn_norm, a_v_gain, a_v_bias, a_w_s, a_b_s, final_norm]
    small_m = [m_mix_norm, m_ffn_norm, m_a_v_gain, m_a_v_bias, m_a_w_s, m_a_b_s, m_final_norm]
    small_v = [v_mix_norm, v_ffn_norm, v_a_v_gain, v_a_v_bias, v_a_w_s, v_a_b_s, v_final_norm]
    small_g = [gr_mix, gr_ffn, gr_gain, gr_bias, gr_ws, gr_bs, gr_final]
    sm_shapes = small_shapes[:len(small_w)]
    sm_out = _adamw(_pack_small(small_w, small_rows), _pack_small(small_g, small_rows), _pack_small(small_m, small_rows),
                    _pack_small(small_v, small_rows), small_rows, "adamw_small")
    sm_delta, sm_m, sm_v = [_unpack_small(o, sm_shapes) for o in sm_out]

    conv_out = _adamw(b_conv_w[0], gr_conv[0], m_b_conv_w[0], v_b_conv_w[0], CONV_W, "adamw_conv")
    conv_delta, conv_m, conv_v = [o[None] for o in conv_out]

    order = ["mix_norm", "ffn_norm", "a_w_in", "a_v_gain", "a_v_bias", "a_w_s", "a_b_s", "a_w_out", "b_w_in",
             "b_conv_w", "b_w_out", "ffn_w_gate", "ffn_w_up", "ffn_w_down", "final_norm"]
    small_names = ["mix_norm", "ffn_norm", "a_v_gain", "a_v_bias", "a_w_s", "a_b_s", "final_norm"]
    grads = {"b_conv_w": gr_conv}
    deltas, new_m, new_v = {}, {}, {}
    for k, name in enumerate(small_names):
        grads[name] = small_g[k]
        deltas[name], new_m[name], new_v[name] = sm_delta[k], sm_m[k], sm_v[k]
    deltas["b_conv_w"], new_m["b_conv_w"], new_v["b_conv_w"] = conv_delta, conv_m, conv_v
    for name, (gg, dl, mm, vv) in res.items():
        grads[name], deltas[name], new_m[name], new_v[name] = gg, dl, mm, vv

    grad_x = dx0.reshape(bsz, seq, d)
    return (loss, grad_x, *[grads[n] for n in order], *[deltas[n] for n in order],
            *[new_m[n] for n in order], *[new_v[n] for n in order])
```
